```python
import jax, jax.numpy as jnp
from jax import lax
import numpy as np

D_MODEL = 2048
BATCH = 8
SEQ = 4096
DEPTH = 1

GRID_W = 64
PLE_DIM = 256
D_ATTN = 1024
N_HEADS = 8
N_KV_HEADS = 2
HEAD_DIM = D_ATTN // N_HEADS
ROPE_THETA = 10000.0
Q_BLOCK = 128
D_SSM = 1024
SSM_GROUP = 16
N_SSM_GROUPS = D_SSM // SSM_GROUP
SSM_STATE = 64
DT_MIN = 0.001
DT_MAX = 0.1
D_MIX = D_ATTN + D_SSM
D_KV = N_KV_HEADS * HEAD_DIM
IN_SPLITS = (D_ATTN, D_KV, D_KV, D_ATTN, D_SSM, D_SSM)
D_IN = sum(IN_SPLITS)
EPS = 1e-6

kernel_name = "hybrid_gqa_axialrope_bis5_block"


def rms_norm(x, g):
    xf = x.astype(jnp.float32)
    y = xf * lax.rsqrt(jnp.mean(xf * xf, axis=-1, keepdims=True) + EPS)
    return (y * g.astype(jnp.float32)).astype(x.dtype)


def _rotate(x, ang):
    x1, x2 = jnp.split(x, 2, axis=-1)
    c = jnp.cos(ang)[None, :, None, :]
    s = jnp.sin(ang)[None, :, None, :]
    return jnp.concatenate([x1 * c - x2 * s, x2 * c + x1 * s], axis=-1)


def axial_rope(x, ang_row, ang_col):
    xf = x.astype(jnp.float32)
    half = x.shape[-1] // 2
    out = jnp.concatenate([_rotate(xf[..., :half], ang_row),
                           _rotate(xf[..., half:], ang_col)], axis=-1)
    return out.astype(x.dtype)


def grid_angles(L):
    rows_n = L // GRID_W
    rows = jnp.repeat(jnp.arange(rows_n), GRID_W).astype(jnp.float32)
    cols = jnp.tile(jnp.arange(GRID_W), rows_n).astype(jnp.float32)
    n_freq = HEAD_DIM // 4
    inv_freq = ROPE_THETA ** (-jnp.arange(n_freq, dtype=jnp.float32) / n_freq)
    return rows[:, None] * inv_freq[None, :], cols[:, None] * inv_freq[None, :]


def block_attention(q, k, v):
    bsz, L, H, Dh = q.shape
    kv = k.shape[2]
    rep = H // kv
    nb = L // Q_BLOCK
    scale = Dh ** -0.5
    qb = q.reshape(bsz, nb, Q_BLOCK, kv, rep, Dh).transpose(1, 0, 2, 3, 4, 5)

    def one_block(qblk):
        s = jnp.einsum('bqkrd,bskd->bkrqs', qblk, k).astype(jnp.float32) * scale
        pr = jax.nn.softmax(s, axis=-1).astype(v.dtype)
        return jnp.einsum('bkrqs,bskd->bqkrd', pr, v)

    o = lax.map(one_block, qb)
    return o.transpose(1, 0, 2, 3, 4, 5).reshape(bsz, L, H * Dh)


def _lin_combine(e1, e2):
    a1, b1 = e1
    a2, b2 = e2
    return a1 * a2, a2 * b1 + b2


def s5_bidirectional(u, a_re, a_im, log_dt, b_re, b_im, c_re, c_im, d):
    bsz, L, _ = u.shape
    ug = u.astype(jnp.float32).reshape(bsz, L, N_SSM_GROUPS, SSM_GROUP)
    uc = ug.astype(jnp.complex64)
    y = d.astype(jnp.float32).reshape(N_SSM_GROUPS, SSM_GROUP) * ug
    for direction in range(2):
        lam = lax.complex(jnp.minimum(a_re[direction].astype(jnp.float32), -1e-4),
                          a_im[direction].astype(jnp.float32))
        dt = jnp.exp(log_dt[direction].astype(jnp.float32))[:, None]
        lam_bar = jnp.exp(lam * dt)
        bmat = lax.complex(b_re[direction].astype(jnp.float32),
                           b_im[direction].astype(jnp.float32))
        b_bar = ((lam_bar - 1.0) / lam)[..., None] * bmat
        bu = jnp.einsum('blgh,gph->blgp', uc, b_bar)
        a = jnp.broadcast_to(lam_bar, bu.shape)
        _, xs = lax.associative_scan(_lin_combine, (a, bu), axis=1,
                                     reverse=(direction == 1))
        cmat = lax.complex(c_re[direction].astype(jnp.float32),
                           c_im[direction].astype(jnp.float32))
        y = y + jnp.real(jnp.einsum('blgp,ghp->blgh', xs, cmat))
    return y.reshape(bsz, L, D_SSM)


def _fwd_setup_inputs(seed: int = 0) -> dict:
    key = jax.random.key(seed)
    ks = jax.random.split(key, 24)
    f32 = jnp.float32
    G, P, H = N_SSM_GROUPS, SSM_STATE, SSM_GROUP
    nrm = lambda k, shape, s: jax.random.normal(k, shape, f32) * s
    x = jax.random.normal(ks[0], (BATCH, SEQ, D_MODEL), f32)
    p = jax.random.normal(ks[1], (DEPTH, BATCH, SEQ, PLE_DIM), f32)
    norm_mix = 1.0 + nrm(ks[2], (DEPTH, D_MODEL), 0.02)
    w_in = nrm(ks[3], (DEPTH, D_MODEL, D_IN), D_MODEL ** -0.5)
    q_norm = 1.0 + nrm(ks[4], (DEPTH, HEAD_DIM), 0.02)
    k_norm = 1.0 + nrm(ks[5], (DEPTH, HEAD_DIM), 0.02)
    ssm_a_re = -0.5 + nrm(ks[6], (DEPTH, 2, G, P), 0.01)
    ssm_a_im = (np.pi * jnp.arange(P, dtype=f32))[None, None, None, :] + nrm(ks[7], (DEPTH, 2, G, P), 0.01)
    ssm_log_dt = jax.random.uniform(ks[8], (DEPTH, 2, G), f32,
                                    minval=float(np.log(DT_MIN)), maxval=float(np.log(DT_MAX)))
    ssm_b_re = nrm(ks[9], (DEPTH, 2, G, P, H), (0.5 / H) ** 0.5)
    ssm_b_im = nrm(ks[10], (DEPTH, 2, G, P, H), (0.5 / H) ** 0.5)
    ssm_c_re = nrm(ks[11], (DEPTH, 2, G, H, P), (0.5 / P) ** 0.5)
    ssm_c_im = nrm(ks[12], (DEPTH, 2, G, H, P), (0.5 / P) ** 0.5)
    ssm_d = nrm(ks[13], (DEPTH, D_SSM), 1.0)
    w_glu = nrm(ks[14], (DEPTH, D_SSM, 2 * D_SSM), D_SSM ** -0.5)
    b_glu = nrm(ks[15], (DEPTH, 2 * D_SSM), 0.01)
    w_out = nrm(ks[16], (DEPTH, D_MIX, D_MODEL), D_MIX ** -0.5)
    norm_ple = 1.0 + nrm(ks[17], (DEPTH, D_MODEL), 0.02)
    w_ple_gate = nrm(ks[18], (DEPTH, D_MODEL, D_MODEL), D_MODEL ** -0.5)
    w_ple_proj = nrm(ks[19], (DEPTH, PLE_DIM, D_MODEL), PLE_DIM ** -0.5)
    norm_final = 1.0 + nrm(ks[20], (D_MODEL,), 0.02)
    return {"x": x, "p": p, "norm_mix": norm_mix, "w_in": w_in,
            "q_norm": q_norm, "k_norm": k_norm,
            "ssm_a_re": ssm_a_re, "ssm_a_im": ssm_a_im, "ssm_log_dt": ssm_log_dt,
            "ssm_b_re": ssm_b_re, "ssm_b_im": ssm_b_im,
            "ssm_c_re": ssm_c_re, "ssm_c_im": ssm_c_im, "ssm_d": ssm_d,
            "w_glu": w_glu, "b_glu": b_glu, "w_out": w_out,
            "norm_ple": norm_ple, "w_ple_gate": w_ple_gate, "w_ple_proj": w_ple_proj,
            "norm_final": norm_final}


def _fwd_reference(x, p, norm_mix, w_in, q_norm, k_norm, ssm_a_re, ssm_a_im, ssm_log_dt,
              ssm_b_re, ssm_b_im, ssm_c_re, ssm_c_im, ssm_d, w_glu, b_glu, w_out,
              norm_ple, w_ple_gate, w_ple_proj, norm_final):
    bsz, L, _ = x.shape
    ang_row, ang_col = grid_angles(L)
    split_idx = list(np.cumsum(IN_SPLITS)[:-1])
    h = x
    for i in range(DEPTH):
        hn = rms_norm(h, norm_mix[i])
        z = hn @ w_in[i]
        q, k, v, gate_a, u, gate_s = jnp.split(z, split_idx, axis=-1)

        q = q.reshape(bsz, L, N_HEADS, HEAD_DIM)
        k = k.reshape(bsz, L, N_KV_HEADS, HEAD_DIM)
        v = v.reshape(bsz, L, N_KV_HEADS, HEAD_DIM)
        q = axial_rope(rms_norm(q, q_norm[i]), ang_row, ang_col)
        k = axial_rope(rms_norm(k, k_norm[i]), ang_row, ang_col)
        y_attn = block_attention(q, k, v) * jax.nn.silu(gate_a)

        y_ssm = s5_bidirectional(u, ssm_a_re[i], ssm_a_im[i], ssm_log_dt[i],
                                 ssm_b_re[i], ssm_b_im[i], ssm_c_re[i], ssm_c_im[i],
                                 ssm_d[i])
        y_ssm = jax.nn.gelu(y_ssm).astype(x.dtype)
        glu = y_ssm @ w_glu[i] + b_glu[i]
        y_ssm = glu[..., :D_SSM] * jax.nn.sigmoid(glu[..., D_SSM:])
        y_ssm = y_ssm * jax.nn.silu(gate_s)

        h = h + jnp.concatenate([y_attn, y_ssm], axis=-1) @ w_out[i]

        gate = jax.nn.sigmoid(rms_norm(h, norm_ple[i]) @ w_ple_gate[i])
        h = h + gate * (p[i] @ w_ple_proj[i])
    return rms_norm(h, norm_final)


import jax as _jax
import jax.numpy as _jnp

TWIN_FORMAT = 'train_step'
FWD_PARAMS = ['x', 'p', 'norm_mix', 'w_in', 'q_norm', 'k_norm', 'ssm_a_re', 'ssm_a_im', 'ssm_log_dt', 'ssm_b_re', 'ssm_b_im', 'ssm_c_re', 'ssm_c_im', 'ssm_d', 'w_glu', 'b_glu', 'w_out', 'norm_ple', 'w_ple_gate', 'w_ple_proj', 'norm_final']
TWIN_WEIGHTS = ['norm_mix', 'w_in', 'q_norm', 'k_norm', 'ssm_a_re', 'ssm_a_im', 'ssm_log_dt', 'ssm_b_re', 'ssm_b_im', 'ssm_c_re', 'ssm_c_im', 'ssm_d', 'w_glu', 'b_glu', 'w_out', 'norm_ple', 'w_ple_gate', 'w_ple_proj', 'norm_final']
TWIN_DIFF_INPUT = 'x'
TWIN_INPUTS = ['x', 'p', 'norm_mix', 'w_in', 'q_norm', 'k_norm', 'ssm_a_re', 'ssm_a_im', 'ssm_log_dt', 'ssm_b_re', 'ssm_b_im', 'ssm_c_re', 'ssm_c_im', 'ssm_d', 'w_glu', 'b_glu', 'w_out', 'norm_ple', 'w_ple_gate', 'w_ple_proj', 'norm_final', 'loss_target', 'm_norm_mix', 'm_w_in', 'm_q_norm', 'm_k_norm', 'm_ssm_a_re', 'm_ssm_a_im', 'm_ssm_log_dt', 'm_ssm_b_re', 'm_ssm_b_im', 'm_ssm_c_re', 'm_ssm_c_im', 'm_ssm_d', 'm_w_glu', 'm_b_glu', 'm_w_out', 'm_norm_ple', 'm_w_ple_gate', 'm_w_ple_proj', 'm_norm_final', 'v_norm_mix', 'v_w_in', 'v_q_norm', 'v_k_norm', 'v_ssm_a_re', 'v_ssm_a_im', 'v_ssm_log_dt', 'v_ssm_b_re', 'v_ssm_b_im', 'v_ssm_c_re', 'v_ssm_c_im', 'v_ssm_d', 'v_w_glu', 'v_b_glu', 'v_w_out', 'v_norm_ple', 'v_w_ple_gate', 'v_w_ple_proj', 'v_norm_final']
TWIN_OUTPUTS = ['loss', 'grad_x', 'grad_norm_mix', 'grad_w_in', 'grad_q_norm', 'grad_k_norm', 'grad_ssm_a_re', 'grad_ssm_a_im', 'grad_ssm_log_dt', 'grad_ssm_b_re', 'grad_ssm_b_im', 'grad_ssm_c_re', 'grad_ssm_c_im', 'grad_ssm_d', 'grad_w_glu', 'grad_b_glu', 'grad_w_out', 'grad_norm_ple', 'grad_w_ple_gate', 'grad_w_ple_proj', 'grad_norm_final', 'delta_norm_mix', 'delta_w_in', 'delta_q_norm', 'delta_k_norm', 'delta_ssm_a_re', 'delta_ssm_a_im', 'delta_ssm_log_dt', 'delta_ssm_b_re', 'delta_ssm_b_im', 'delta_ssm_c_re', 'delta_ssm_c_im', 'delta_ssm_d', 'delta_w_glu', 'delta_b_glu', 'delta_w_out', 'delta_norm_ple', 'delta_w_ple_gate', 'delta_w_ple_proj', 'delta_norm_final', 'new_m_norm_mix', 'new_m_w_in', 'new_m_q_norm', 'new_m_k_norm', 'new_m_ssm_a_re', 'new_m_ssm_a_im', 'new_m_ssm_log_dt', 'new_m_ssm_b_re', 'new_m_ssm_b_im', 'new_m_ssm_c_re', 'new_m_ssm_c_im', 'new_m_ssm_d', 'new_m_w_glu', 'new_m_b_glu', 'new_m_w_out', 'new_m_norm_ple', 'new_m_w_ple_gate', 'new_m_w_ple_proj', 'new_m_norm_final', 'new_v_norm_mix', 'new_v_w_in', 'new_v_q_norm', 'new_v_k_norm', 'new_v_ssm_a_re', 'new_v_ssm_a_im', 'new_v_ssm_log_dt', 'new_v_ssm_b_re', 'new_v_ssm_b_im', 'new_v_ssm_c_re', 'new_v_ssm_c_im', 'new_v_ssm_d', 'new_v_w_glu', 'new_v_b_glu', 'new_v_w_out', 'new_v_norm_ple', 'new_v_w_ple_gate', 'new_v_w_ple_proj', 'new_v_norm_final']
TWIN_LEAF_KINDS = {'loss': 'loss', 'grad_x': 'grad_x', 'grad_norm_mix': 'grad_w', 'grad_w_in': 'grad_w', 'grad_q_norm': 'grad_w', 'grad_k_norm': 'grad_w', 'grad_ssm_a_re': 'grad_w', 'grad_ssm_a_im': 'grad_w', 'grad_ssm_log_dt': 'grad_w', 'grad_ssm_b_re': 'grad_w', 'grad_ssm_b_im': 'grad_w', 'grad_ssm_c_re': 'grad_w', 'grad_ssm_c_im': 'grad_w', 'grad_ssm_d': 'grad_w', 'grad_w_glu': 'grad_w', 'grad_b_glu': 'grad_w', 'grad_w_out': 'grad_w', 'grad_norm_ple': 'grad_w', 'grad_w_ple_gate': 'grad_w', 'grad_w_ple_proj': 'grad_w', 'grad_norm_final': 'grad_w', 'delta_norm_mix': 'delta_w', 'delta_w_in': 'delta_w', 'delta_q_norm': 'delta_w', 'delta_k_norm': 'delta_w', 'delta_ssm_a_re': 'delta_w', 'delta_ssm_a_im': 'delta_w', 'delta_ssm_log_dt': 'delta_w', 'delta_ssm_b_re': 'delta_w', 'delta_ssm_b_im': 'delta_w', 'delta_ssm_c_re': 'delta_w', 'delta_ssm_c_im': 'delta_w', 'delta_ssm_d': 'delta_w', 'delta_w_glu': 'delta_w', 'delta_b_glu': 'delta_w', 'delta_w_out': 'delta_w', 'delta_norm_ple': 'delta_w', 'delta_w_ple_gate': 'delta_w', 'delta_w_ple_proj': 'delta_w', 'delta_norm_final': 'delta_w', 'new_m_norm_mix': 'new_m', 'new_m_w_in': 'new_m', 'new_m_q_norm': 'new_m', 'new_m_k_norm': 'new_m', 'new_m_ssm_a_re': 'new_m', 'new_m_ssm_a_im': 'new_m', 'new_m_ssm_log_dt': 'new_m', 'new_m_ssm_b_re': 'new_m', 'new_m_ssm_b_im': 'new_m', 'new_m_ssm_c_re': 'new_m', 'new_m_ssm_c_im': 'new_m', 'new_m_ssm_d': 'new_m', 'new_m_w_glu': 'new_m', 'new_m_b_glu': 'new_m', 'new_m_w_out': 'new_m', 'new_m_norm_ple': 'new_m', 'new_m_w_ple_gate': 'new_m', 'new_m_w_ple_proj': 'new_m', 'new_m_norm_final': 'new_m', 'new_v_norm_mix': 'new_v', 'new_v_w_in': 'new_v', 'new_v_q_norm': 'new_v', 'new_v_k_norm': 'new_v', 'new_v_ssm_a_re': 'new_v', 'new_v_ssm_a_im': 'new_v', 'new_v_ssm_log_dt': 'new_v', 'new_v_ssm_b_re': 'new_v', 'new_v_ssm_b_im': 'new_v', 'new_v_ssm_c_re': 'new_v', 'new_v_ssm_c_im': 'new_v', 'new_v_ssm_d': 'new_v', 'new_v_w_glu': 'new_v', 'new_v_b_glu': 'new_v', 'new_v_w_out': 'new_v', 'new_v_norm_ple': 'new_v', 'new_v_w_ple_gate': 'new_v', 'new_v_w_ple_proj': 'new_v', 'new_v_norm_final': 'new_v'}


def _forward(args):
    return _fwd_reference(*[args[k] for k in FWD_PARAMS])


def _output_shape():
    def fwd():
        inp = _fwd_setup_inputs(0)
        return _fwd_reference(*[inp[k] for k in FWD_PARAMS])
    out = _jax.eval_shape(fwd)
    return out.shape, out.dtype

N_MICROBATCH = 1
ADAM_LR = 0.001
ADAM_B1 = 0.9
ADAM_B2 = 0.999
ADAM_EPS = 1e-08
ADAM_WD = 0.01
ADAM_STEP = 10
PER_EXAMPLE_BATCH_AXIS = {'x': 0, 'p': 1, 'loss_target': 0}
SHARED_INPUTS = []
_WEIGHT_DTYPES = {'norm_mix': _jnp.float32, 'w_in': _jnp.float32, 'q_norm': _jnp.float32, 'k_norm': _jnp.float32, 'ssm_a_re': _jnp.float32, 'ssm_a_im': _jnp.float32, 'ssm_log_dt': _jnp.float32, 'ssm_b_re': _jnp.float32, 'ssm_b_im': _jnp.float32, 'ssm_c_re': _jnp.float32, 'ssm_c_im': _jnp.float32, 'ssm_d': _jnp.float32, 'w_glu': _jnp.float32, 'b_glu': _jnp.float32, 'w_out': _jnp.float32, 'norm_ple': _jnp.float32, 'w_ple_gate': _jnp.float32, 'w_ple_proj': _jnp.float32, 'norm_final': _jnp.float32}
MOMENT_SCALE = {'norm_mix': 1.939103e-02, 'w_in': 1.267600e-02, 'q_norm': 1.390177e-02, 'k_norm': 1.425357e-02, 'ssm_a_re': 8.490417e-04, 'ssm_a_im': 8.441685e-04, 'ssm_log_dt': 6.465299e-01, 'ssm_b_re': 5.641902e-04, 'ssm_b_im': 5.646076e-04, 'ssm_c_re': 1.122055e-03, 'ssm_c_im': 1.144986e-03, 'ssm_d': 1.776158e-02, 'w_glu': 1.265199e-02, 'b_glu': 1.861813e-02, 'w_out': 1.244792e-02, 'norm_ple': 1.686064e-02, 'w_ple_gate': 1.638565e-02, 'w_ple_proj': 4.226239e-02, 'norm_final': 1.601217e+01}


def _to_microbatches(a, axis):
    t = _jnp.moveaxis(a, axis, 0)
    t = t.reshape((N_MICROBATCH, t.shape[0] // N_MICROBATCH) + t.shape[1:])
    return _jnp.moveaxis(t, 1, axis + 1)


def setup_inputs(seed: int = 0) -> dict:
    inp = _fwd_setup_inputs(seed)
    key = _jax.random.fold_in(_jax.random.key(seed), 7919)
    shape, _ = _output_shape()
    out = dict(inp)
    out["loss_target"] = _jax.random.normal(_jax.random.fold_in(key, 0), shape, _jnp.float32)
    for i, name in enumerate(TWIN_WEIGHTS):
        w = inp[name].astype(_jnp.float32)
        if MOMENT_SCALE is None:
            s = _jnp.sqrt(_jnp.mean(_jnp.square(w)) + 1e-30)
        else:
            s = MOMENT_SCALE[name]
        km, kv = _jax.random.split(_jax.random.fold_in(key, i + 1))
        out[name] = w
        out["m_" + name] = s * _jax.random.normal(km, w.shape, _jnp.float32)
        out["v_" + name] = (s * s) * _jax.random.uniform(kv, w.shape, _jnp.float32, 0.5, 1.5)
    if N_MICROBATCH > 1:
        for name, axis in PER_EXAMPLE_BATCH_AXIS.items():
            out[name] = _to_microbatches(out[name], axis)
    return {'x': out['x'], 'p': out['p'], 'norm_mix': out['norm_mix'], 'w_in': out['w_in'], 'q_norm': out['q_norm'], 'k_norm': out['k_norm'], 'ssm_a_re': out['ssm_a_re'], 'ssm_a_im': out['ssm_a_im'], 'ssm_log_dt': out['ssm_log_dt'], 'ssm_b_re': out['ssm_b_re'], 'ssm_b_im': out['ssm_b_im'], 'ssm_c_re': out['ssm_c_re'], 'ssm_c_im': out['ssm_c_im'], 'ssm_d': out['ssm_d'], 'w_glu': out['w_glu'], 'b_glu': out['b_glu'], 'w_out': out['w_out'], 'norm_ple': out['norm_ple'], 'w_ple_gate': out['w_ple_gate'], 'w_ple_proj': out['w_ple_proj'], 'norm_final': out['norm_final'], 'loss_target': out['loss_target'], 'm_norm_mix': out['m_norm_mix'], 'm_w_in': out['m_w_in'], 'm_q_norm': out['m_q_norm'], 'm_k_norm': out['m_k_norm'], 'm_ssm_a_re': out['m_ssm_a_re'], 'm_ssm_a_im': out['m_ssm_a_im'], 'm_ssm_log_dt': out['m_ssm_log_dt'], 'm_ssm_b_re': out['m_ssm_b_re'], 'm_ssm_b_im': out['m_ssm_b_im'], 'm_ssm_c_re': out['m_ssm_c_re'], 'm_ssm_c_im': out['m_ssm_c_im'], 'm_ssm_d': out['m_ssm_d'], 'm_w_glu': out['m_w_glu'], 'm_b_glu': out['m_b_glu'], 'm_w_out': out['m_w_out'], 'm_norm_ple': out['m_norm_ple'], 'm_w_ple_gate': out['m_w_ple_gate'], 'm_w_ple_proj': out['m_w_ple_proj'], 'm_norm_final': out['m_norm_final'], 'v_norm_mix': out['v_norm_mix'], 'v_w_in': out['v_w_in'], 'v_q_norm': out['v_q_norm'], 'v_k_norm': out['v_k_norm'], 'v_ssm_a_re': out['v_ssm_a_re'], 'v_ssm_a_im': out['v_ssm_a_im'], 'v_ssm_log_dt': out['v_ssm_log_dt'], 'v_ssm_b_re': out['v_ssm_b_re'], 'v_ssm_b_im': out['v_ssm_b_im'], 'v_ssm_c_re': out['v_ssm_c_re'], 'v_ssm_c_im': out['v_ssm_c_im'], 'v_ssm_d': out['v_ssm_d'], 'v_w_glu': out['v_w_glu'], 'v_b_glu': out['v_b_glu'], 'v_w_out': out['v_w_out'], 'v_norm_ple': out['v_norm_ple'], 'v_w_ple_gate': out['v_w_ple_gate'], 'v_w_ple_proj': out['v_w_ple_proj'], 'v_norm_final': out['v_norm_final']}


def _loss(weights, diff, rest, loss_target):
    with _jax.named_scope("forward"):
        args = {**rest, TWIN_DIFF_INPUT: diff, **{k: w.astype(_WEIGHT_DTYPES[k]) for k, w in weights.items()}}
        y = _forward(args)
    with _jax.named_scope("loss_head"):
        err = _jnp.square(y.astype(_jnp.float32) - loss_target)
        return 0.5 * _jnp.sum(_jnp.mean(err, axis=-1)) if err.ndim else 0.5 * err


def _adamw(w, g, m, v):
    m = ADAM_B1 * m + (1.0 - ADAM_B1) * g
    v = ADAM_B2 * v + (1.0 - ADAM_B2) * _jnp.square(g)
    m_hat = m / (1.0 - ADAM_B1 ** ADAM_STEP)
    v_hat = v / (1.0 - ADAM_B2 ** ADAM_STEP)
    delta = -ADAM_LR * (m_hat / (_jnp.sqrt(v_hat) + ADAM_EPS) + ADAM_WD * w)
    return delta, m, v


def reference(x, p, norm_mix, w_in, q_norm, k_norm, ssm_a_re, ssm_a_im, ssm_log_dt, ssm_b_re, ssm_b_im, ssm_c_re, ssm_c_im, ssm_d, w_glu, b_glu, w_out, norm_ple, w_ple_gate, w_ple_proj, norm_final, loss_target, m_norm_mix, m_w_in, m_q_norm, m_k_norm, m_ssm_a_re, m_ssm_a_im, m_ssm_log_dt, m_ssm_b_re, m_ssm_b_im, m_ssm_c_re, m_ssm_c_im, m_ssm_d, m_w_glu, m_b_glu, m_w_out, m_norm_ple, m_w_ple_gate, m_w_ple_proj, m_norm_final, v_norm_mix, v_w_in, v_q_norm, v_k_norm, v_ssm_a_re, v_ssm_a_im, v_ssm_log_dt, v_ssm_b_re, v_ssm_b_im, v_ssm_c_re, v_ssm_c_im, v_ssm_d, v_w_glu, v_b_glu, v_w_out, v_norm_ple, v_w_ple_gate, v_w_ple_proj, v_norm_final):
    given = dict(x=x, p=p, norm_mix=norm_mix, w_in=w_in, q_norm=q_norm, k_norm=k_norm, ssm_a_re=ssm_a_re, ssm_a_im=ssm_a_im, ssm_log_dt=ssm_log_dt, ssm_b_re=ssm_b_re, ssm_b_im=ssm_b_im, ssm_c_re=ssm_c_re, ssm_c_im=ssm_c_im, ssm_d=ssm_d, w_glu=w_glu, b_glu=b_glu, w_out=w_out, norm_ple=norm_ple, w_ple_gate=w_ple_gate, w_ple_proj=w_ple_proj, norm_final=norm_final, loss_target=loss_target, m_norm_mix=m_norm_mix, m_w_in=m_w_in, m_q_norm=m_q_norm, m_k_norm=m_k_norm, m_ssm_a_re=m_ssm_a_re, m_ssm_a_im=m_ssm_a_im, m_ssm_log_dt=m_ssm_log_dt, m_ssm_b_re=m_ssm_b_re, m_ssm_b_im=m_ssm_b_im, m_ssm_c_re=m_ssm_c_re, m_ssm_c_im=m_ssm_c_im, m_ssm_d=m_ssm_d, m_w_glu=m_w_glu, m_b_glu=m_b_glu, m_w_out=m_w_out, m_norm_ple=m_norm_ple, m_w_ple_gate=m_w_ple_gate, m_w_ple_proj=m_w_ple_proj, m_norm_final=m_norm_final, v_norm_mix=v_norm_mix, v_w_in=v_w_in, v_q_norm=v_q_norm, v_k_norm=v_k_norm, v_ssm_a_re=v_ssm_a_re, v_ssm_a_im=v_ssm_a_im, v_ssm_log_dt=v_ssm_log_dt, v_ssm_b_re=v_ssm_b_re, v_ssm_b_im=v_ssm_b_im, v_ssm_c_re=v_ssm_c_re, v_ssm_c_im=v_ssm_c_im, v_ssm_d=v_ssm_d, v_w_glu=v_w_glu, v_b_glu=v_b_glu, v_w_out=v_w_out, v_norm_ple=v_norm_ple, v_w_ple_gate=v_w_ple_gate, v_w_ple_proj=v_w_ple_proj, v_norm_final=v_norm_final)
    weights = {n: given[n] for n in TWIN_WEIGHTS}
    shared = {n: given[n] for n in SHARED_INPUTS}
    per_example = {n: given[n] for n in ['x', 'p']}
    grad_fn = _jax.value_and_grad(_loss, argnums=(0, 1))

    def one_microbatch(ex, loss_target):
        ex = dict(ex)
        diff = ex.pop(TWIN_DIFF_INPUT)
        return grad_fn(weights, diff, {**shared, **ex}, loss_target)

    if N_MICROBATCH == 1:
        loss, (grad_w, grad_x) = one_microbatch(per_example, given["loss_target"])
    else:
        def body(carry, xs):
            loss_sum, grad_sum = carry
            l_k, (gw_k, gx_k) = one_microbatch(xs[0], xs[1])
            with _jax.named_scope("update"):
                return (loss_sum + l_k, _jax.tree.map(_jnp.add, grad_sum, gw_k)), gx_k

        init = (_jnp.zeros((), _jnp.float32), _jax.tree.map(_jnp.zeros_like, weights))
        (loss, grad_w), grad_x = _jax.lax.scan(body, init, (per_example, given["loss_target"]))
    with _jax.named_scope("update"):
        delta_w, new_m, new_v = {}, {}, {}
        for n in TWIN_WEIGHTS:
            delta_w[n], new_m[n], new_v[n] = _adamw(weights[n], grad_w[n], given["m_" + n], given["v_" + n])
    return (loss, grad_x, *[grad_w[n] for n in TWIN_WEIGHTS], *[delta_w[n] for n in TWIN_WEIGHTS],
            *[new_m[n] for n in TWIN_WEIGHTS], *[new_v[n] for n in TWIN_WEIGHTS])
```

```python
import functools
import math

import jax
import jax.numpy as jnp
import numpy as np
from jax import lax
from jax.experimental import pallas as pl
from jax.experimental.pallas import tpu as pltpu

D_MODEL = 2048
GRID_W = 64
PLE_DIM = 256
D_ATTN = 1024
N_HEADS = 8
N_KV = 2
HEAD_DIM = 128
ROPE_THETA = 10000.0
D_SSM = 1024
SSM_H = 16
SSM_G = 64
SSM_P = 64
D_KV = N_KV * HEAD_DIM
D_IN = 2 * D_ATTN + 2 * D_KV + 2 * D_SSM
EPS = 1e-6
Z_Q, Z_K, Z_V, Z_GA, Z_U, Z_GS = 0, 1024, 1280, 1536, 2560, 3584

ADAM_LR, ADAM_B1, ADAM_B2, ADAM_EPS, ADAM_WD, ADAM_STEP = 0.001, 0.9, 0.999, 1e-08, 0.01, 10

N_CHIPS = 4
VMEM_LIMIT_V7X = 56 * 1024 * 1024
F32 = jnp.float32
BF16 = jnp.bfloat16
HI = lax.Precision.HIGHEST


def _params(sem, vmem=VMEM_LIMIT_V7X):
    return pltpu.CompilerParams(dimension_semantics=sem, vmem_limit_bytes=vmem)


def _matmul(a, b, *, ta=False, tb=False, out_dtype=F32, tm=1024, tn=512, tk=512, name):
    M, K = (a.shape[1], a.shape[0]) if ta else a.shape
    N = b.shape[0] if tb else b.shape[1]
    tm, tn, tk = min(tm, M), min(tn, N), min(tk, K)
    assert M % tm == 0 and N % tn == 0 and K % tk == 0, (name, M, N, K)
    nk = K // tk
    dims = (((0 if ta else 1,), (1 if tb else 0,)), ((), ()))

    def body(a_ref, b_ref, o_ref, acc_ref):
        k = pl.program_id(2)

        @pl.when(k == 0)
        def _():
            acc_ref[...] = jnp.zeros_like(acc_ref)

        acc_ref[...] += lax.dot_general(a_ref[...], b_ref[...], dims, preferred_element_type=F32)

        @pl.when(k == nk - 1)
        def _():
            o_ref[...] = acc_ref[...].astype(o_ref.dtype)

    a_spec = pl.BlockSpec((tk, tm), lambda i, j, k: (k, i)) if ta else pl.BlockSpec((tm, tk), lambda i, j, k: (i, k))
    b_spec = pl.BlockSpec((tn, tk), lambda i, j, k: (j, k)) if tb else pl.BlockSpec((tk, tn), lambda i, j, k: (k, j))
    return pl.pallas_call(
        body, name=name,
        out_shape=jax.ShapeDtypeStruct((M, N), out_dtype),
        grid=(M // tm, N // tn, nk),
        in_specs=[a_spec, b_spec],
        out_specs=pl.BlockSpec((tm, tn), lambda i, j, k: (i, j)),
        scratch_shapes=[pltpu.VMEM((tm, tn), F32)],
        compiler_params=_params(("parallel", "parallel", "arbitrary")),
    )(a, b)


def _rowwise(fn, rows, consts, outs, accs=(), *, tr=256, name):
    L = rows[0][0].shape[0]
    tr = min(tr, L)
    assert L % tr == 0
    n_in, n_c, n_o, n_a = len(rows), len(consts), len(outs), len(accs)

    def body(*refs):
        ins = [r[...] for r in refs[:n_in + n_c]]
        res = fn(*ins)
        if not isinstance(res, (tuple, list)):
            res = (res,)
        o_refs = refs[n_in + n_c:n_in + n_c + n_o]
        a_refs = refs[n_in + n_c + n_o:]
        for r, v in zip(o_refs, res[:n_o]):
            r[...] = v.astype(r.dtype)
        if n_a:
            first = pl.program_id(0) == 0

            @pl.when(first)
            def _():
                for r, v in zip(a_refs, res[n_o:]):
                    r[...] = v.astype(F32)

            @pl.when(jnp.logical_not(first))
            def _():
                for r, v in zip(a_refs, res[n_o:]):
                    r[...] += v.astype(F32)

    in_specs = []
    for arr, off, w in rows:
        assert off % w == 0, (name, off, w)
        in_specs.append(pl.BlockSpec((tr, w), functools.partial(lambda i, c: (i, c), c=off // w)))
    for c in consts:
        in_specs.append(pl.BlockSpec(c.shape, lambda i: (0, 0)))
    out_shape = [jax.ShapeDtypeStruct((L, w), dt) for w, dt in outs] + [jax.ShapeDtypeStruct(s, F32) for s in accs]
    out_specs = [pl.BlockSpec((tr, w), lambda i: (i, 0)) for w, _ in outs] + [pl.BlockSpec(s, lambda i: (0, 0)) for s in accs]
    res = pl.pallas_call(
        body, name=name,
        out_shape=out_shape,
        grid=(L // tr,),
        in_specs=in_specs,
        out_specs=out_specs,
        compiler_params=_params(("arbitrary",) if n_a else ("parallel",)),
    )(*[r[0] for r in rows], *consts)
    return res


def _sig(x):
    return jax.nn.sigmoid(x)


def _silu_and_grad(x):
    s = _sig(x)
    return x * s, s * (1.0 + x * (1.0 - s))


_GELU_C = math.sqrt(2.0 / math.pi)


def _gelu(x):
    return 0.5 * x * (1.0 + jnp.tanh(_GELU_C * (x + 0.044715 * x * x * x)))


def _gelu_grad(x):
    t = jnp.tanh(_GELU_C * (x + 0.044715 * x * x * x))
    return 0.5 * (1.0 + t) + 0.5 * x * (1.0 - t * t) * _GELU_C * (1.0 + 3.0 * 0.044715 * x * x)


def _rms(x):
    return lax.rsqrt(jnp.mean(x * x, axis=-1, keepdims=True) + EPS)


def _rms_bwd(x, g, dy):
    r = _rms(x)
    n = x * r
    dn = dy * g
    return r * (dn - n * jnp.mean(dn * n, axis=-1, keepdims=True)), dy * n


def _colsum(v):
    return jnp.sum(v, axis=0, keepdims=True)


def _rope_partner(x):
    lane = lax.broadcasted_iota(jnp.int32, x.shape, x.ndim - 1)
    return jnp.where(lane % 64 < 32, pltpu.roll(x, 96, x.ndim - 1), pltpu.roll(x, 32, x.ndim - 1))


def _rope_tables(L):
    rows_n = L // GRID_W
    rows = jnp.repeat(jnp.arange(rows_n), GRID_W).astype(F32)
    cols = jnp.tile(jnp.arange(GRID_W), rows_n).astype(F32)
    n_freq = HEAD_DIM // 4
    inv_freq = ROPE_THETA ** (-jnp.arange(n_freq, dtype=F32) / n_freq)
    ar, ac = rows[:, None] * inv_freq[None, :], cols[:, None] * inv_freq[None, :]
    cos = jnp.concatenate([jnp.cos(ar), jnp.cos(ar), jnp.cos(ac), jnp.cos(ac)], axis=-1)
    sin = jnp.concatenate([-jnp.sin(ar), jnp.sin(ar), -jnp.sin(ac), jnp.sin(ac)], axis=-1)
    return cos, sin


def _heads(v):
    return [v[:, h * HEAD_DIM:(h + 1) * HEAD_DIM] for h in range(v.shape[1] // HEAD_DIM)]


def _attn_prep(z, q_norm, k_norm, cos, sin):
    def fn(q, k, v, cos, sin, gq, gk):
        def one(xh, g):
            xn = xh * _rms(xh) * g
            return xn * cos + _rope_partner(xn) * sin
        qr = jnp.concatenate([one(h, gq) for h in _heads(q)], axis=1)
        kr = jnp.concatenate([one(h, gk) for h in _heads(k)], axis=1)
        return qr, kr, v
    return _rowwise(fn, [(z, Z_Q, D_ATTN), (z, Z_K, D_KV), (z, Z_V, D_KV), (cos, 0, HEAD_DIM), (sin, 0, HEAD_DIM)],
                    [q_norm, k_norm], [(D_ATTN, BF16), (D_KV, BF16), (D_KV, BF16)], name="attn_prep")


def _attn_prep_bwd(dqr, dkr, z, q_norm, k_norm, cos, sin):
    def fn(dqr, dkr, q, k, cos, sin, gq, gk):
        def one(dyh, xh, g):
            dn = dyh * cos + _rope_partner(dyh * sin)
            return _rms_bwd(xh, g, dn)
        rq = [one(a, b, gq) for a, b in zip(_heads(dqr), _heads(q))]
        rk = [one(a, b, gk) for a, b in zip(_heads(dkr), _heads(k))]
        dq = jnp.concatenate([r[0] for r in rq], axis=1)
        dk = jnp.concatenate([r[0] for r in rk], axis=1)
        return dq, dk, _colsum(sum(r[1] for r in rq)), _colsum(sum(r[1] for r in rk))
    return _rowwise(fn, [(dqr, 0, D_ATTN), (dkr, 0, D_KV), (z, Z_Q, D_ATTN), (z, Z_K, D_KV), (cos, 0, HEAD_DIM), (sin, 0, HEAD_DIM)],
                    [q_norm, k_norm], [(D_ATTN, BF16), (D_KV, BF16)], [(1, HEAD_DIM), (1, HEAD_DIM)], name="attn_prep_bwd")


_QK_T = (((1,), (1,)), ((), ()))
_TA = (((0,), (0,)), ((), ()))
_REP = N_HEADS // N_KV


def _attn_fwd(qr, kr, vb, *, tq=256):
    L = qr.shape[0]
    tq = min(tq, L)
    scale = HEAD_DIM ** -0.5

    def body(q_ref, k_ref, v_ref, o_ref, lse_ref):
        s = lax.dot_general(q_ref[...], k_ref[...], _QK_T, preferred_element_type=F32) * scale
        m = jnp.max(s, axis=-1, keepdims=True)
        p = jnp.exp(s - m)
        l = jnp.sum(p, axis=-1, keepdims=True)
        pn = (p * (1.0 / l)).astype(BF16)
        o_ref[...] = jnp.dot(pn, v_ref[...], preferred_element_type=F32)
        lse_ref[...] = m + jnp.log(l)

    return pl.pallas_call(
        body, name="attn_fwd",
        out_shape=[jax.ShapeDtypeStruct((L, D_ATTN), F32), jax.ShapeDtypeStruct((N_HEADS, L, 1), F32)],
        grid=(N_HEADS, L // tq),
        in_specs=[pl.BlockSpec((tq, HEAD_DIM), lambda h, i: (i, h)),
                  pl.BlockSpec((L, HEAD_DIM), lambda h, i: (0, h // _REP)),
                  pl.BlockSpec((L, HEAD_DIM), lambda h, i: (0, h // _REP))],
        out_specs=[pl.BlockSpec((tq, HEAD_DIM), lambda h, i: (i, h)),
                   pl.BlockSpec((None, tq, 1), lambda h, i: (h, i, 0))],
        compiler_params=_params(("parallel", "parallel")),
    )(qr, kr, vb)


def _attn_bwd(qr, kr, vb, do, lse, *, tq=256):
    L = qr.shape[0]
    tq = min(tq, L)
    scale = HEAD_DIM ** -0.5

    def body(q_ref, k_ref, v_ref, do_ref, lse_ref, dq_ref, dk_ref, dv_ref):
        @pl.when((pl.program_id(1) == 0) & (pl.program_id(2) == 0))
        def _():
            dk_ref[...] = jnp.zeros_like(dk_ref)
            dv_ref[...] = jnp.zeros_like(dv_ref)

        q, k, v, do = q_ref[...], k_ref[...], v_ref[...], do_ref[...]
        s = lax.dot_general(q, k, _QK_T, preferred_element_type=F32) * scale
        p = jnp.exp(s - lse_ref[...])
        dv_ref[...] += lax.dot_general(p.astype(BF16), do, _TA, preferred_element_type=F32)
        dp = lax.dot_general(do, v, _QK_T, preferred_element_type=F32)
        delta = jnp.sum(p * dp, axis=-1, keepdims=True)
        ds = (p * (dp - delta) * scale).astype(BF16)
        dq_ref[...] = jnp.dot(ds, k, preferred_element_type=F32)
        dk_ref[...] += lax.dot_general(ds, q, _TA, preferred_element_type=F32)

    head = lambda g, r, i: (i, g * _REP + r)
    return pl.pallas_call(
        body, name="attn_bwd",
        out_shape=[jax.ShapeDtypeStruct((L, D_ATTN), F32), jax.ShapeDtypeStruct((L, D_KV), F32), jax.ShapeDtypeStruct((L, D_KV), F32)],
        grid=(N_KV, _REP, L // tq),
        in_specs=[pl.BlockSpec((tq, HEAD_DIM), head),
                  pl.BlockSpec((L, HEAD_DIM), lambda g, r, i: (0, g)),
                  pl.BlockSpec((L, HEAD_DIM), lambda g, r, i: (0, g)),
                  pl.BlockSpec((tq, HEAD_DIM), head),
                  pl.BlockSpec((None, tq, 1), lambda g, r, i: (g * _REP + r, i, 0))],
        out_specs=[pl.BlockSpec((tq, HEAD_DIM), head),
                   pl.BlockSpec((L, HEAD_DIM), lambda g, r, i: (0, g)),
                   pl.BlockSpec((L, HEAD_DIM), lambda g, r, i: (0, g))],
        compiler_params=_params(("parallel", "arbitrary", "arbitrary")),
    )(qr, kr, vb, do, lse)


SSM_BLK = 8
SSM_NB = SSM_G // SSM_BLK
SSM_BW = SSM_BLK * SSM_H
SSM_SW = SSM_BLK * 2 * SSM_P
SSM_CW = 512
SSM_SEG = 8


def _seg_perm(a):
    L, C = a.shape
    return a.reshape(SSM_SEG, L // SSM_SEG, C).transpose(1, 0, 2).reshape(L, C)


def _seg_unperm(a):
    L, C = a.shape
    return a.reshape(L // SSM_SEG, SSM_SEG, C).transpose(1, 0, 2).reshape(L, C)


def _cplx_pow2(a, b, n):
    for _ in range(int(math.log2(n))):
        a, b = a * a - b * b, 2.0 * a * b
    return a, b


def _seg_scan(ref, a, b, T, reverse):
    npair = len(a)
    zero = jnp.zeros((SSM_SEG, 128), F32)

    def make_step(store):
        def step(t, carry):
            lt = (T - 1 - t) if reverse else t
            row = pl.multiple_of(lt * SSM_SEG, SSM_SEG)
            blk = ref[pl.ds(row, SSM_SEG), :]
            new = []
            for q in range(npair):
                re, im = carry[2 * q], carry[2 * q + 1]
                nre = a[q] * re - b[q] * im + blk[:, q * 256:q * 256 + 128]
                nim = a[q] * im + b[q] * re + blk[:, q * 256 + 128:q * 256 + 256]
                new += [nre, nim]
            if store:
                ref[pl.ds(row, SSM_SEG), :] = jnp.concatenate(new, axis=1)
            return tuple(new)
        return step

    ends = lax.fori_loop(0, T, make_step(False), (zero,) * (2 * npair))
    sub = lax.broadcasted_iota(jnp.int32, (SSM_SEG, 128), 0)
    keep = (sub != SSM_SEG - 1) if reverse else (sub != 0)
    shift = (SSM_SEG - 1) if reverse else 1
    init = []
    for q in range(npair):
        pa, pb = _cplx_pow2(a[q], b[q], T)
        xr, xi = zero, zero
        for _ in range(SSM_SEG - 1):
            fr = ends[2 * q] + pa * xr - pb * xi
            fi = ends[2 * q + 1] + pa * xi + pb * xr
            xr = jnp.where(keep, pltpu.roll(fr, shift, 0), 0.0)
            xi = jnp.where(keep, pltpu.roll(fi, shift, 0), 0.0)
        init += [xr, xi]
    lax.fori_loop(0, T, make_step(True), tuple(init))
    return init


def _lam_rows(are_ref, aim_ref, d, jb):
    a, b = [], []
    for q in range(SSM_CW // 256):
        j = jb * (SSM_CW // 256) + q
        a.append(jnp.broadcast_to(are_ref[d, j:j + 1, :], (SSM_SEG, 128)))
        b.append(jnp.broadcast_to(aim_ref[d, j:j + 1, :], (SSM_SEG, 128)))
    return a, b


def _ssm_fwd(u_p, wb, wc, are, aim, dvec):
    L = u_p.shape[0]
    T = L // SSM_SEG
    RC = min(512, L)

    def body(u_ref, wb_ref, wc_ref, are_ref, aim_ref, d_ref, y_ref, x_scr):
        y_ref[...] = u_ref[...] * d_ref[...]
        for d in range(2):
            for jb in range(SSM_SW // SSM_CW):
                cols = slice(jb * SSM_CW, (jb + 1) * SSM_CW)

                def bu_chunk(c, _):
                    rows = pl.ds(pl.multiple_of(c * RC, RC), RC)
                    x_scr[rows, :] = jnp.dot(u_ref[rows, :], wb_ref[d, :, cols], precision=HI, preferred_element_type=F32)
                    return 0
                lax.fori_loop(0, L // RC, bu_chunk, 0)
                a, b = _lam_rows(are_ref, aim_ref, d, jb)
                _seg_scan(x_scr, a, b, T, reverse=(d == 1))

                def y_chunk(c, _):
                    rows = pl.ds(pl.multiple_of(c * RC, RC), RC)
                    y_ref[rows, :] += jnp.dot(x_scr[rows, :], wc_ref[d, cols, :], precision=HI, preferred_element_type=F32)
                    return 0
                lax.fori_loop(0, L // RC, y_chunk, 0)

    blk4 = lambda g: (g, 0, 0, 0)
    return pl.pallas_call(
        body, name="ssm_fwd",
        out_shape=jax.ShapeDtypeStruct((L, D_SSM), F32),
        grid=(SSM_NB,),
        in_specs=[pl.BlockSpec((L, SSM_BW), lambda g: (0, g)),
                  pl.BlockSpec((None, 2, SSM_BW, SSM_SW), blk4),
                  pl.BlockSpec((None, 2, SSM_SW, SSM_BW), blk4),
                  pl.BlockSpec((None, 2, 4, 128), blk4),
                  pl.BlockSpec((None, 2, 4, 128), blk4),
                  pl.BlockSpec((1, SSM_BW), lambda g: (0, g))],
        out_specs=pl.BlockSpec((L, SSM_BW), lambda g: (0, g)),
        scratch_shapes=[pltpu.VMEM((L, SSM_CW), F32)],
        compiler_params=_params(("parallel",)),
    )(u_p, wb, wc, are, aim, dvec)


def _ssm_bwd(u_p, dy_p, wb, wc, are, aim, dvec):
    L = u_p.shape[0]
    T = L // SSM_SEG
    RC = min(512, L)
    npair = SSM_CW // 256

    def lam_acc(acc, sb, xb):
        new = []
        for q in range(npair):
            sr, si = sb[:, q * 256:q * 256 + 128], sb[:, q * 256 + 128:q * 256 + 256]
            xr, xi = xb[:, q * 256:q * 256 + 128], xb[:, q * 256 + 128:q * 256 + 256]
            new += [acc[2 * q] + sr * xr + si * xi, acc[2 * q + 1] + si * xr - sr * xi]
        return tuple(new)

    def body(u_ref, dy_ref, wb_ref, wc_ref, are_ref, aim_ref, d_ref,
             du_ref, dwb_ref, dwc_ref, dare_ref, daim_ref, dd_ref, x_scr, s_scr):
        du_ref[...] = dy_ref[...] * d_ref[...]
        dd_ref[...] = _colsum(dy_ref[...] * u_ref[...])
        dwb_ref[...] = jnp.zeros_like(dwb_ref)
        dwc_ref[...] = jnp.zeros_like(dwc_ref)
        for d in range(2):
            rev = d == 1
            for jb in range(SSM_SW // SSM_CW):
                cols = slice(jb * SSM_CW, (jb + 1) * SSM_CW)

                def in_chunk(c, _):
                    rows = pl.ds(pl.multiple_of(c * RC, RC), RC)
                    x_scr[rows, :] = jnp.dot(u_ref[rows, :], wb_ref[d, :, cols], precision=HI, preferred_element_type=F32)
                    s_scr[rows, :] = lax.dot_general(dy_ref[rows, :], wc_ref[d, cols, :], _QK_T, precision=HI,
                                                     preferred_element_type=F32)
                    return 0
                lax.fori_loop(0, L // RC, in_chunk, 0)
                a, b = _lam_rows(are_ref, aim_ref, d, jb)
                x_in = _seg_scan(x_scr, a, b, T, reverse=rev)
                _seg_scan(s_scr, a, [-v for v in b], T, reverse=not rev)

                def lam_step(t, acc):
                    lt = (T - 2 - t) if rev else (t + 1)
                    srow = pl.multiple_of(lt * SSM_SEG, SSM_SEG)
                    xrow = pl.multiple_of((lt + 1 if rev else lt - 1) * SSM_SEG, SSM_SEG)
                    return lam_acc(acc, s_scr[pl.ds(srow, SSM_SEG), :], x_scr[pl.ds(xrow, SSM_SEG), :])

                edge = pl.ds(((T - 1) if rev else 0) * SSM_SEG, SSM_SEG)
                acc0 = lam_acc((jnp.zeros((SSM_SEG, 128), F32),) * (2 * npair), s_scr[edge, :], jnp.concatenate(x_in, axis=1))
                acc = lax.fori_loop(0, T - 1, lam_step, acc0)
                for q in range(npair):
                    j = jb * npair + q
                    dare_ref[d, j:j + 1, :] = _colsum(acc[2 * q])
                    daim_ref[d, j:j + 1, :] = _colsum(acc[2 * q + 1])

                def out_chunk(c, _):
                    rows = pl.ds(pl.multiple_of(c * RC, RC), RC)
                    xs, ss, uu, dd = x_scr[rows, :], s_scr[rows, :], u_ref[rows, :], dy_ref[rows, :]
                    dwc_ref[d, cols, :] += lax.dot_general(xs, dd, _TA, precision=HI, preferred_element_type=F32)
                    dwb_ref[d, :, cols] += lax.dot_general(uu, ss, _TA, precision=HI, preferred_element_type=F32)
                    du_ref[rows, :] += lax.dot_general(ss, wb_ref[d, :, cols], _QK_T, precision=HI, preferred_element_type=F32)
                    return 0
                lax.fori_loop(0, L // RC, out_chunk, 0)

    blk4 = lambda g: (g, 0, 0, 0)
    chan = pl.BlockSpec((L, SSM_BW), lambda g: (0, g))
    par_specs = [pl.BlockSpec((None, 2, SSM_BW, SSM_SW), blk4),
                 pl.BlockSpec((None, 2, SSM_SW, SSM_BW), blk4),
                 pl.BlockSpec((None, 2, 4, 128), blk4),
                 pl.BlockSpec((None, 2, 4, 128), blk4),
                 pl.BlockSpec((1, SSM_BW), lambda g: (0, g))]
    return pl.pallas_call(
        body, name="ssm_bwd",
        out_shape=[jax.ShapeDtypeStruct((L, D_SSM), F32),
                   jax.ShapeDtypeStruct((SSM_NB, 2, SSM_BW, SSM_SW), F32),
                   jax.ShapeDtypeStruct((SSM_NB, 2, SSM_SW, SSM_BW), F32),
                   jax.ShapeDtypeStruct((SSM_NB, 2, 4, 128), F32),
                   jax.ShapeDtypeStruct((SSM_NB, 2, 4, 128), F32),
                   jax.ShapeDtypeStruct((1, D_SSM), F32)],
        grid=(SSM_NB,),
        in_specs=[chan, chan] + par_specs,
        out_specs=[chan] + par_specs,
        scratch_shapes=[pltpu.VMEM((L, SSM_CW), F32), pltpu.VMEM((L, SSM_CW), F32)],
        compiler_params=_params(("parallel",)),
    )(u_p, dy_p, wb, wc, are, aim, dvec)


def _ssm_disc(a_re, a_im, log_dt, b_re, b_im):
    lam = lax.complex(jnp.minimum(a_re, -1e-4), a_im)
    dt = jnp.exp(log_dt)[..., None]
    lam_bar = jnp.exp(lam * dt)
    b_bar = ((lam_bar - 1.0) / lam)[..., None] * lax.complex(b_re, b_im)
    return jnp.real(lam_bar), jnp.imag(lam_bar), jnp.real(b_bar), jnp.imag(b_bar)


_EYE8 = np.eye(SSM_BLK, dtype=np.float32)


def _to_wb(bb_re, bb_im):
    bb = jnp.stack([bb_re, bb_im], axis=2).reshape(2, SSM_NB, SSM_BLK, 2, SSM_P, SSM_H)
    t = jnp.einsum("gc,dncrph->ndghcrp", _EYE8, bb)
    t = t.reshape(SSM_NB, 2, SSM_BLK, SSM_H, 4, 2, 2, SSM_P).transpose(0, 1, 2, 3, 4, 6, 5, 7)
    return t.reshape(SSM_NB, 2, SSM_BW, SSM_SW)


def _from_wb(dwb):
    t = dwb.reshape(SSM_NB, 2, SSM_BLK, SSM_H, 4, 2, 2, SSM_P).transpose(0, 1, 2, 3, 4, 6, 5, 7)
    t = t.reshape(SSM_NB, 2, SSM_BLK, SSM_H, SSM_BLK, 2, SSM_P)
    bb = jnp.einsum("gc,ndghcrp->dncrph", _EYE8, t).reshape(2, SSM_G, 2, SSM_P, SSM_H)
    return bb[:, :, 0], bb[:, :, 1]


def _to_wc(c_re, c_im):
    cc = jnp.stack([c_re, -c_im], axis=2).reshape(2, SSM_NB, SSM_BLK, 2, SSM_H, SSM_P)
    t = jnp.einsum("gc,dncrhp->ndcrpgh", _EYE8, cc)
    t = t.reshape(SSM_NB, 2, 4, 2, 2, SSM_P, SSM_BLK, SSM_H).transpose(0, 1, 2, 4, 3, 5, 6, 7)
    return t.reshape(SSM_NB, 2, SSM_SW, SSM_BW)


def _from_wc(dwc):
    t = dwc.reshape(SSM_NB, 2, 4, 2, 2, SSM_P, SSM_BLK, SSM_H).transpose(0, 1, 2, 4, 3, 5, 6, 7)
    t = t.reshape(SSM_NB, 2, SSM_BLK, 2, SSM_P, SSM_BLK, SSM_H)
    cc = jnp.einsum("gc,ndcrpgh->dncrhp", _EYE8, t).reshape(2, SSM_G, 2, SSM_H, SSM_P)
    return cc[:, :, 0], -cc[:, :, 1]


def _to_lam(v):
    return v.reshape(2, SSM_NB, 4, 128).transpose(1, 0, 2, 3)


def _from_lam(v):
    return v.transpose(1, 0, 2, 3).reshape(2, SSM_G, SSM_P)


_MESH = pl.DeviceIdType.MESH
_ANY = pl.BlockSpec(memory_space=pl.ANY)
_BIG = (("w_in", (D_MODEL, D_IN), 1, D_IN // N_CHIPS),
        ("w_glu", (D_SSM, 2 * D_SSM), 1, 2 * D_SSM // N_CHIPS),
        ("w_out", (D_ATTN + D_SSM, D_MODEL), 0, (D_ATTN + D_SSM) // N_CHIPS),
        ("w_ple_gate", (D_MODEL, D_MODEL), 0, D_MODEL // N_CHIPS),
        ("w_ple_proj", (PLE_DIM, D_MODEL), 1, D_MODEL // N_CHIPS))


def _place():
    x, y, c = lax.axis_index("x"), lax.axis_index("y"), lax.axis_index("c")
    return x, y, c, [(1 - x, y), (x, 1 - y), (1 - x, 1 - y)]


def _gather_weights(shards):
    nt = len(_BIG)

    def body(*refs):
        srcs, dsts = refs[:nt], refs[nt:2 * nt]
        send_sems, recv_sems, loc_sems = refs[2 * nt:]
        x, y, c, chips = _place()

        def win(t, kk):
            _, _, axis, sz = _BIG[t]
            sl = pl.ds(pl.multiple_of(kk * sz, sz), sz)
            return dsts[t].at[:, sl] if axis == 1 else dsts[t].at[sl, :]

        def remote(j, t, kk):
            px, py = chips[j]
            return pltpu.make_async_remote_copy(src_ref=srcs[t], dst_ref=win(t, kk), send_sem=send_sems.at[j, t],
                                                recv_sem=recv_sems.at[j, t], device_id=(px, py, c), device_id_type=_MESH)

        k = 2 * x + y
        local = [pltpu.make_async_copy(srcs[t], win(t, k), loc_sems.at[t]) for t in range(nt)]
        sends = [remote(j, t, k) for j in range(3) for t in range(nt)]
        for cp in local + sends:
            cp.start()
        for j, (px, py) in enumerate(chips):
            for t in range(nt):
                remote(j, t, 2 * px + py).wait_recv()
        for cp in sends:
            cp.wait_send()
        for cp in local:
            cp.wait()

    return pl.pallas_call(
        body, name="gather_weights",
        out_shape=[jax.ShapeDtypeStruct(shape, BF16) for _, shape, _, _ in _BIG],
        in_specs=[_ANY] * nt, out_specs=[_ANY] * nt,
        scratch_shapes=[pltpu.SemaphoreType.DMA((3, nt)), pltpu.SemaphoreType.DMA((3, nt)), pltpu.SemaphoreType.DMA((nt,))],
    )(*shards)


MSG_W = 1024
_PIECE = tuple((shape[0] // 2, sz) if ax == 1 else (sz // 2, shape[1]) for _, shape, ax, sz in _BIG)
BIG_ROWS = sum(r * c for r, c in _PIECE) // MSG_W
SMALL_ROWS = 192
MSG_ROWS = BIG_ROWS + SMALL_ROWS
N_SMALL = 8 * SMALL_ROWS * MSG_W


def _pack_messages(big_grads, small_flat):
    parts = []
    for g, (_, shape, ax, sz), (pr, pc) in zip(big_grads, _BIG, _PIECE):
        if ax == 1:
            t = g.reshape(2, pr, N_CHIPS, pc).transpose(2, 0, 1, 3)
        else:
            t = g.reshape(N_CHIPS, 2, pr, pc)
        parts.append(t.reshape(N_CHIPS, 2, pr * pc))
    parts.append(small_flat.reshape(N_CHIPS, 2, SMALL_ROWS * MSG_W))
    return jnp.concatenate(parts, axis=-1).reshape(N_CHIPS, 2, MSG_ROWS, MSG_W)


def _unpack_big(big):
    flat = big.reshape(2, BIG_ROWS * MSG_W)
    out, off = [], 0
    for (pr, pc) in _PIECE:
        out.append(flat[:, off:off + pr * pc].reshape(2 * pr, pc))
        off += pr * pc
    return out


def _sibling_exchange(src):
    n = src.shape[0]

    def body(src_ref, dst_ref, send_sems, recv_sems):
        x, y, c, _ = _place()
        cps = [pltpu.make_async_remote_copy(src_ref=src_ref.at[i], dst_ref=dst_ref.at[i], send_sem=send_sems.at[i],
                                            recv_sem=recv_sems.at[i], device_id=(x, y, 1 - c), device_id_type=_MESH)
               for i in range(n)]
        for cp in cps:
            cp.start()
        for cp in cps:
            cp.wait()

    return pl.pallas_call(
        body, name="grad_sibling_exchange",
        out_shape=jax.ShapeDtypeStruct(src.shape, src.dtype),
        in_specs=[_ANY], out_specs=_ANY,
        scratch_shapes=[pltpu.SemaphoreType.DMA((n,)), pltpu.SemaphoreType.DMA((n,))],
    )(src)


def _chip_exchange(src):
    def body(src_ref, dst_ref, send_sems, recv_sems):
        x, y, c, chips = _place()
        cps = [pltpu.make_async_remote_copy(src_ref=src_ref.at[2 * px + py], dst_ref=dst_ref.at[j], send_sem=send_sems.at[j],
                                            recv_sem=recv_sems.at[j], device_id=(px, py, c), device_id_type=_MESH)
               for j, (px, py) in enumerate(chips)]
        for cp in cps:
            cp.start()
        for cp in cps:
            cp.wait()

    return pl.pallas_call(
        body, name="grad_chip_exchange",
        out_shape=jax.ShapeDtypeStruct((3,) + src.shape[1:], src.dtype),
        in_specs=[_ANY], out_specs=_ANY,
        scratch_shapes=[pltpu.SemaphoreType.DMA((3,)), pltpu.SemaphoreType.DMA((3,))],
    )(src)


def _final_exchange(total):
    def body(src_ref, big_ref, small_ref, send_sems, recv_sems, loc_sems):
        x, y, c, chips = _place()
        me = 4 * x + 2 * y + c
        src_big, src_small = src_ref.at[pl.ds(0, BIG_ROWS)], src_ref.at[pl.ds(BIG_ROWS, SMALL_ROWS)]
        others = [(x, y, 1 - c)] + [(px, py, cc) for (px, py) in chips for cc in (c, 1 - c)]

        def small_copy(i, slot):
            return pltpu.make_async_remote_copy(src_ref=src_small, dst_ref=small_ref.at[slot], send_sem=send_sems.at[i],
                                                recv_sem=recv_sems.at[i], device_id=others[i], device_id_type=_MESH)

        def big_copy(half):
            return pltpu.make_async_remote_copy(src_ref=src_big, dst_ref=big_ref.at[half], send_sem=send_sems.at[7],
                                                recv_sem=recv_sems.at[7], device_id=others[0], device_id_type=_MESH)

        local = [pltpu.make_async_copy(src_big, big_ref.at[c], loc_sems.at[0]),
                 pltpu.make_async_copy(src_small, small_ref.at[me], loc_sems.at[1])]
        sends = [small_copy(i, me) for i in range(7)] + [big_copy(c)]
        for cp in local + sends:
            cp.start()
        for i, (px, py, pc) in enumerate(others):
            small_copy(i, 4 * px + 2 * py + pc).wait_recv()
        big_copy(1 - c).wait_recv()
        for cp in sends:
            cp.wait_send()
        for cp in local:
            cp.wait()

    return pl.pallas_call(
        body, name="grad_final_exchange",
        out_shape=[jax.ShapeDtypeStruct((2, BIG_ROWS, MSG_W), F32), jax.ShapeDtypeStruct((8, SMALL_ROWS, MSG_W), F32)],
        in_specs=[_ANY], out_specs=[_ANY, _ANY],
        scratch_shapes=[pltpu.SemaphoreType.DMA((8,)), pltpu.SemaphoreType.DMA((8,)), pltpu.SemaphoreType.DMA((2,))],
    )(total)


def _add_n(arrs, name):
    def fn(*vs):
        t = vs[0]
        for v in vs[1:]:
            t = t + v
        return t
    W = arrs[0].shape[1]
    return _rowwise(fn, [(a, 0, W) for a in arrs], [], [(W, F32)], tr=128, name=name)[0]


def _reduce_gradients(big_grads, small_flat):
    c = lax.axis_index("c")
    k = 2 * lax.axis_index("x") + lax.axis_index("y")
    msg = _pack_messages(big_grads, small_flat)
    mine = lax.dynamic_index_in_dim(msg, c, 1, keepdims=False)
    to_sibling = lax.dynamic_index_in_dim(msg, 1 - c, 1, keepdims=False)
    from_sibling = _sibling_exchange(to_sibling)
    flat = lambda a: a.reshape(-1, MSG_W)
    chip_sum = _add_n([flat(mine), flat(from_sibling)], "grad_sum_chip").reshape(N_CHIPS, MSG_ROWS, MSG_W)
    from_chips = _chip_exchange(chip_sum)
    own = lax.dynamic_index_in_dim(chip_sum, k, 0, keepdims=False)
    total = _add_n([own, from_chips[0], from_chips[1], from_chips[2]], "grad_sum_all")
    big, small = _final_exchange(total)
    return _unpack_big(big), small.reshape(-1)


def _adamw(w, g, m, v, name):
    def fn(w, g, m, v):
        m = ADAM_B1 * m + (1.0 - ADAM_B1) * g
        v = ADAM_B2 * v + (1.0 - ADAM_B2) * (g * g)
        m_hat = m / (1.0 - ADAM_B1 ** ADAM_STEP)
        v_hat = v / (1.0 - ADAM_B2 ** ADAM_STEP)
        return -ADAM_LR * (m_hat / (jnp.sqrt(v_hat) + ADAM_EPS) + ADAM_WD * w), m, v
    W = w.shape[1]
    return _rowwise(fn, [(a, 0, W) for a in (w, g, m, v)], [], [(W, F32)] * 3, tr=128, name=name)


def _chunks(arr, off, width, w=512):
    return [(arr, off + i * w, w) for i in range(width // w)]


def _cat(vs):
    return jnp.concatenate(vs, axis=1)


def _forward_backward(x, p_b, tgt, full, small):
    L = x.shape[0]
    w_in, w_glu, w_out, w_pg, w_pp = full
    row = lambda v: v.reshape(1, -1)
    g_mix, g_ple, g_fin = row(small["norm_mix"]), row(small["norm_ple"]), row(small["norm_final"])
    gq, gk, b_glu = row(small["q_norm"]), row(small["k_norm"]), row(small["b_glu"])
    cos, sin = _rope_tables(L)

    hn_b, = _rowwise(lambda x, g: x * _rms(x) * g, [(x, 0, D_MODEL)], [g_mix], [(D_MODEL, BF16)], name="norm_mix")
    z = _matmul(hn_b, w_in, name="mm_in")
    qr, kr, vb = _attn_prep(z, gq, gk, cos, sin)
    o, lse = _attn_fwd(qr, kr, vb)

    ssm_names = ("ssm_a_re", "ssm_a_im", "ssm_log_dt", "ssm_b_re", "ssm_b_im")
    (lre, lim, bre, bim), disc_vjp = jax.vjp(_ssm_disc, *[small[n][0] for n in ssm_names])
    wb, wc = _to_wb(bre, bim), _to_wc(small["ssm_c_re"][0], small["ssm_c_im"][0])
    are, aim, dvec = _to_lam(lre), _to_lam(lim), row(small["ssm_d"])
    u_p = _seg_perm(z[:, Z_U:Z_U + D_SSM])
    y_s = _seg_unperm(_ssm_fwd(u_p, wb, wc, are, aim, dvec))
    ge_b, = _rowwise(_gelu, [(y_s, 0, D_SSM)], [], [(D_SSM, BF16)], name="gelu")
    glu = _matmul(ge_b, w_glu, name="mm_glu")

    def merge(ga0, ga1, a, b, gs0, gs1, o, bias):
        sa, _ = _silu_and_grad(_cat([ga0, ga1]))
        ss, _ = _silu_and_grad(_cat([gs0, gs1]))
        y2 = (a + bias[:, :D_SSM]) * _sig(b + bias[:, D_SSM:])
        return _cat([o * sa, y2 * ss])
    merge_rows = _chunks(z, Z_GA, D_ATTN) + [(glu, 0, D_SSM), (glu, D_SSM, D_SSM)] + _chunks(z, Z_GS, D_SSM) + [(o, 0, D_ATTN)]
    cat_b, = _rowwise(merge, merge_rows, [b_glu], [(D_MODEL, BF16)], name="merge")
    t_out = _matmul(cat_b, w_out, name="mm_out")

    def resid(x, t, g):
        h1 = x + t
        return h1, h1 * _rms(h1) * g
    h1, hp_b = _rowwise(resid, [(x, 0, D_MODEL), (t_out, 0, D_MODEL)], [g_ple], [(D_MODEL, F32), (D_MODEL, BF16)], name="resid_norm")
    gl = _matmul(hp_b, w_pg, name="mm_ple_gate")
    pp = _matmul(p_b, w_pp, name="mm_ple_proj")

    def head(h1, gl, pp, tgt, g):
        gate = _sig(gl)
        h2 = h1 + gate * pp
        r = _rms(h2)
        n = h2 * r
        err = n * g - tgt
        dy = err * (1.0 / D_MODEL)
        dn = dy * g
        dh2 = r * (dn - n * jnp.mean(dn * n, axis=-1, keepdims=True))
        dgate = dh2 * pp
        return dh2, dh2 * gate, dgate * gate * (1.0 - gate), _colsum(dy * n), _colsum(0.5 * err * err * (1.0 / D_MODEL))
    dh2, dpp_b, dgl_b, dg_fin, loss_cols = _rowwise(
        head, [(a, 0, D_MODEL) for a in (h1, gl, pp, tgt)], [g_fin],
        [(D_MODEL, F32), (D_MODEL, BF16), (D_MODEL, BF16)], [(1, D_MODEL), (1, D_MODEL)], name="loss_head")

    dw_pp = _matmul(p_b, dpp_b, ta=True, name="mm_d_w_ple_proj")
    dw_pg = _matmul(hp_b, dgl_b, ta=True, name="mm_d_w_ple_gate")
    dhp = _matmul(dgl_b, w_pg, tb=True, name="mm_d_hp")

    def resid_bwd(dhp, h1, dh2, g):
        dx, dg = _rms_bwd(h1, g, dhp)
        dh1 = dh2 + dx
        return dh1, dh1, _colsum(dg)
    dh1, dh1_b, dg_ple = _rowwise(resid_bwd, [(a, 0, D_MODEL) for a in (dhp, h1, dh2)], [g_ple],
                                  [(D_MODEL, F32), (D_MODEL, BF16)], [(1, D_MODEL)], name="resid_norm_bwd")
    dw_out = _matmul(cat_b, dh1_b, ta=True, name="mm_d_w_out")
    dcat = _matmul(dh1_b, w_out, tb=True, name="mm_d_cat")

    def merge_bwd(dya, dys, ga0, ga1, a, b, gs0, gs1, o, bias):
        ga, gs = _cat([ga0, ga1]), _cat([gs0, gs1])
        sa, dsa = _silu_and_grad(ga)
        ss, dss = _silu_and_grad(gs)
        a, sb = a + bias[:, :D_SSM], _sig(b + bias[:, D_SSM:])
        dy2 = dys * ss
        dglu = _cat([dy2 * sb, dy2 * a * sb * (1.0 - sb)])
        return dya * sa, dya * o * dsa, dys * (a * sb) * dss, dglu, _colsum(dglu)
    do_b, dga_b, dgs_b, dglu_b, db_glu = _rowwise(
        merge_bwd, [(dcat, 0, D_ATTN), (dcat, D_ATTN, D_SSM)] + merge_rows, [b_glu],
        [(D_ATTN, BF16), (D_ATTN, BF16), (D_SSM, BF16), (2 * D_SSM, BF16)], [(1, 2 * D_SSM)], name="merge_bwd")
    dw_glu = _matmul(ge_b, dglu_b, ta=True, name="mm_d_w_glu")
    dge = _matmul(dglu_b, w_glu, tb=True, name="mm_d_ge")
    dy_s, = _rowwise(lambda dge, y: dge * _gelu_grad(y), [(dge, 0, D_SSM), (y_s, 0, D_SSM)], [], [(D_SSM, F32)], name="gelu_bwd")
    du_p, dwb, dwc, dare, daim, d_ssm_d = _ssm_bwd(u_p, _seg_perm(dy_s), wb, wc, are, aim, dvec)
    db_re_bar, db_im_bar = _from_wb(dwb)
    dc_re, dc_im = _from_wc(dwc)
    da_re, da_im, dlog_dt, db_re, db_im = disc_vjp((_from_lam(dare), _from_lam(daim), db_re_bar, db_im_bar))

    dqr, dkr, dv = _attn_bwd(qr, kr, vb, do_b, lse)
    dq_b, dk_b, dgq, dgk = _attn_prep_bwd(dqr, dkr, z, gq, gk, cos, sin)
    dz_b = _cat([dq_b, dk_b, dv.astype(BF16), dga_b, _seg_unperm(du_p).astype(BF16), dgs_b])
    dw_in = _matmul(hn_b, dz_b, ta=True, name="mm_d_w_in")
    dhn = _matmul(dz_b, w_in, tb=True, name="mm_d_hn")

    def norm_bwd(dhn, x, dh1, g):
        dx, dg = _rms_bwd(x, g, dhn)
        return dh1 + dx, _colsum(dg)
    grad_x, dg_mix = _rowwise(norm_bwd, [(a, 0, D_MODEL) for a in (dhn, x, dh1)], [g_mix], [(D_MODEL, F32)], [(1, D_MODEL)],
                              name="norm_mix_bwd")

    small_grads = {"norm_mix": dg_mix, "q_norm": dgq, "k_norm": dgk, "ssm_a_re": da_re, "ssm_a_im": da_im, "ssm_log_dt": dlog_dt,
                   "ssm_b_re": db_re, "ssm_b_im": db_im, "ssm_c_re": dc_re, "ssm_c_im": dc_im, "ssm_d": d_ssm_d,
                   "b_glu": db_glu, "norm_ple": dg_ple, "norm_final": dg_fin}
    return jnp.sum(loss_cols), grad_x, [dw_in, dw_glu, dw_out, dw_pg, dw_pp], small_grads


_SMALL = ("norm_mix", "q_norm", "k_norm", "ssm_a_re", "ssm_a_im", "ssm_log_dt", "ssm_b_re", "ssm_b_im", "ssm_c_re", "ssm_c_im",
          "ssm_d", "b_glu", "norm_ple", "norm_final")
_WEIGHTS = ("norm_mix", "w_in", "q_norm", "k_norm", "ssm_a_re", "ssm_a_im", "ssm_log_dt", "ssm_b_re", "ssm_b_im", "ssm_c_re",
            "ssm_c_im", "ssm_d", "w_glu", "b_glu", "w_out", "norm_ple", "w_ple_gate", "w_ple_proj", "norm_final")


def _flat_small(d):
    flat = jnp.concatenate([d[n].reshape(-1).astype(F32) for n in _SMALL])
    return jnp.pad(flat, (0, N_SMALL - flat.shape[0]))


def _split_small(flat, like):
    out, off = {}, 0
    for n in _SMALL:
        sz = math.prod(like[n].shape)
        out[n] = flat[off:off + sz].reshape(like[n].shape)
        off += sz
    return out


def kernel(x, p, norm_mix, w_in, q_norm, k_norm, ssm_a_re, ssm_a_im, ssm_log_dt, ssm_b_re, ssm_b_im, ssm_c_re, ssm_c_im, ssm_d, w_glu, b_glu, w_out, norm_ple, w_ple_gate, w_ple_proj, norm_final, loss_target, m_norm_mix, m_w_in, m_q_norm, m_k_norm, m_ssm_a_re, m_ssm_a_im, m_ssm_log_dt, m_ssm_b_re, m_ssm_b_im, m_ssm_c_re, m_ssm_c_im, m_ssm_d, m_w_glu, m_b_glu, m_w_out, m_norm_ple, m_w_ple_gate, m_w_ple_proj, m_norm_final, v_norm_mix, v_w_in, v_q_norm, v_k_norm, v_ssm_a_re, v_ssm_a_im, v_ssm_log_dt, v_ssm_b_re, v_ssm_b_im, v_ssm_c_re, v_ssm_c_im, v_ssm_d, v_w_glu, v_b_glu, v_w_out, v_norm_ple, v_w_ple_gate, v_w_ple_proj, v_norm_final):
    w = dict(norm_mix=norm_mix, w_in=w_in, q_norm=q_norm, k_norm=k_norm, ssm_a_re=ssm_a_re, ssm_a_im=ssm_a_im,
             ssm_log_dt=ssm_log_dt, ssm_b_re=ssm_b_re, ssm_b_im=ssm_b_im, ssm_c_re=ssm_c_re, ssm_c_im=ssm_c_im, ssm_d=ssm_d,
             w_glu=w_glu, b_glu=b_glu, w_out=w_out, norm_ple=norm_ple, w_ple_gate=w_ple_gate, w_ple_proj=w_ple_proj,
             norm_final=norm_final)
    m = dict(norm_mix=m_norm_mix, w_in=m_w_in, q_norm=m_q_norm, k_norm=m_k_norm, ssm_a_re=m_ssm_a_re, ssm_a_im=m_ssm_a_im,
             ssm_log_dt=m_ssm_log_dt, ssm_b_re=m_ssm_b_re, ssm_b_im=m_ssm_b_im, ssm_c_re=m_ssm_c_re, ssm_c_im=m_ssm_c_im,
             ssm_d=m_ssm_d, w_glu=m_w_glu, b_glu=m_b_glu, w_out=m_w_out, norm_ple=m_norm_ple, w_ple_gate=m_w_ple_gate,
             w_ple_proj=m_w_ple_proj, norm_final=m_norm_final)
    v = dict(norm_mix=v_norm_mix, w_in=v_w_in, q_norm=v_q_norm, k_norm=v_k_norm, ssm_a_re=v_ssm_a_re, ssm_a_im=v_ssm_a_im,
             ssm_log_dt=v_ssm_log_dt, ssm_b_re=v_ssm_b_re, ssm_b_im=v_ssm_b_im, ssm_c_re=v_ssm_c_re, ssm_c_im=v_ssm_c_im,
             ssm_d=v_ssm_d, w_glu=v_w_glu, b_glu=v_b_glu, w_out=v_w_out, norm_ple=v_norm_ple, w_ple_gate=v_w_ple_gate,
             w_ple_proj=v_w_ple_proj, norm_final=v_norm_final)
    big_names = [n for n, _, _, _ in _BIG]

    full = _gather_weights([w[n][0].astype(BF16) for n in big_names])
    small = {n: w[n] for n in _SMALL}
    loss_part, grad_x, big_grads, small_grads = _forward_backward(
        x[0], p[0, 0].astype(BF16), loss_target[0], full, small)
    loss = lax.psum(loss_part, ("x", "y", "c"))

    big_red, small_red = _reduce_gradients(big_grads, _flat_small(small_grads))
    grads = _split_small(small_red, w)
    delta, new_m, new_v = {}, {}, {}
    for n, g in zip(big_names, big_red):
        grads[n] = g[None]
        d_, m_, v_ = _adamw(w[n][0], g, m[n][0], v[n][0], "adamw_" + n)
        delta[n], new_m[n], new_v[n] = d_[None], m_[None], v_[None]
    d_, m_, v_ = _adamw(*[a.reshape(-1, MSG_W) for a in (_flat_small(w), small_red, _flat_small(m), _flat_small(v))], "adamw_small")
    delta.update(_split_small(d_.reshape(-1), w))
    new_m.update(_split_small(m_.reshape(-1), w))
    new_v.update(_split_small(v_.reshape(-1), w))
    return (loss, grad_x[None], *[grads[n] for n in _WEIGHTS], *[delta[n] for n in _WEIGHTS],
            *[new_m[n] for n in _WEIGHTS], *[new_v[n] for n in _WEIGHTS])
```

```python
import functools
import math

import jax
import jax.numpy as jnp
import numpy as np
from jax import lax
from jax.experimental import pallas as pl
from jax.experimental.pallas import tpu as pltpu

D_MODEL = 2048
GRID_W = 64
PLE_DIM = 256
D_ATTN = 1024
N_HEADS = 8
N_KV = 2
HEAD_DIM = 128
ROPE_THETA = 10000.0
D_SSM = 1024
SSM_H = 16
SSM_G = 64
SSM_P = 64
D_KV = N_KV * HEAD_DIM
D_IN = 2 * D_ATTN + 2 * D_KV + 2 * D_SSM
EPS = 1e-6
Z_Q, Z_K, Z_V, Z_GA, Z_U, Z_GS = 0, 1024, 1280, 1536, 2560, 3584

ADAM_LR, ADAM_B1, ADAM_B2, ADAM_EPS, ADAM_WD, ADAM_STEP = 0.001, 0.9, 0.999, 1e-08, 0.01, 10

N_CHIPS = 4
VMEM_LIMIT_V7X = 56 * 1024 * 1024
F32 = jnp.float32
BF16 = jnp.bfloat16


def _params(sem, vmem=VMEM_LIMIT_V7X):
    return pltpu.CompilerParams(dimension_semantics=sem, vmem_limit_bytes=vmem)


def _matmul(a, b, *, ta=False, tb=False, out_dtype=F32, tm=1024, tn=512, tk=512, name):
    M, K = (a.shape[1], a.shape[0]) if ta else a.shape
    N = b.shape[0] if tb else b.shape[1]
    tm, tn, tk = min(tm, M), min(tn, N), min(tk, K)
    assert M % tm == 0 and N % tn == 0 and K % tk == 0, (name, M, N, K)
    nk = K // tk
    dims = (((0 if ta else 1,), (1 if tb else 0,)), ((), ()))

    def body(a_ref, b_ref, o_ref, acc_ref):
        k = pl.program_id(2)

        @pl.when(k == 0)
        def _():
            acc_ref[...] = jnp.zeros_like(acc_ref)

        acc_ref[...] += lax.dot_general(a_ref[...], b_ref[...], dims, preferred_element_type=F32)

        @pl.when(k == nk - 1)
        def _():
            o_ref[...] = acc_ref[...].astype(o_ref.dtype)

    a_spec = pl.BlockSpec((tk, tm), lambda i, j, k: (k, i)) if ta else pl.BlockSpec((tm, tk), lambda i, j, k: (i, k))
    b_spec = pl.BlockSpec((tn, tk), lambda i, j, k: (j, k)) if tb else pl.BlockSpec((tk, tn), lambda i, j, k: (k, j))
    return pl.pallas_call(
        body, name=name,
        out_shape=jax.ShapeDtypeStruct((M, N), out_dtype),
        grid=(M // tm, N // tn, nk),
        in_specs=[a_spec, b_spec],
        out_specs=pl.BlockSpec((tm, tn), lambda i, j, k: (i, j)),
        scratch_shapes=[pltpu.VMEM((tm, tn), F32)],
        compiler_params=_params(("parallel", "parallel", "arbitrary")),
    )(a, b)


def _rowwise(fn, rows, consts, outs, accs=(), *, tr=256, name):
    L = rows[0][0].shape[0]
    tr = min(tr, L)
    assert L % tr == 0
    n_in, n_c, n_o, n_a = len(rows), len(consts), len(outs), len(accs)

    def body(*refs):
        ins = [r[...] for r in refs[:n_in + n_c]]
        res = fn(*ins)
        if not isinstance(res, (tuple, list)):
            res = (res,)
        o_refs = refs[n_in + n_c:n_in + n_c + n_o]
        a_refs = refs[n_in + n_c + n_o:]
        for r, v in zip(o_refs, res[:n_o]):
            r[...] = v.astype(r.dtype)
        if n_a:
            first = pl.program_id(0) == 0

            @pl.when(first)
            def _():
                for r, v in zip(a_refs, res[n_o:]):
                    r[...] = v.astype(F32)

            @pl.when(jnp.logical_not(first))
            def _():
                for r, v in zip(a_refs, res[n_o:]):
                    r[...] += v.astype(F32)

    in_specs = []
    for arr, off, w in rows:
        assert off % w == 0, (name, off, w)
        in_specs.append(pl.BlockSpec((tr, w), functools.partial(lambda i, c: (i, c), c=off // w)))
    for c in consts:
        in_specs.append(pl.BlockSpec(c.shape, lambda i: (0, 0)))
    out_shape = [jax.ShapeDtypeStruct((L, w), dt) for w, dt in outs] + [jax.ShapeDtypeStruct(s, F32) for s in accs]
    out_specs = [pl.BlockSpec((tr, w), lambda i: (i, 0)) for w, _ in outs] + [pl.BlockSpec(s, lambda i: (0, 0)) for s in accs]
    res = pl.pallas_call(
        body, name=name,
        out_shape=out_shape,
        grid=(L // tr,),
        in_specs=in_specs,
        out_specs=out_specs,
        compiler_params=_params(("arbitrary",) if n_a else ("parallel",)),
    )(*[r[0] for r in rows], *consts)
    return res


def _sig(x):
    return jax.nn.sigmoid(x)


def _silu_and_grad(x):
    s = _sig(x)
    return x * s, s * (1.0 + x * (1.0 - s))


_GELU_C = math.sqrt(2.0 / math.pi)


def _gelu(x):
    return 0.5 * x * (1.0 + jnp.tanh(_GELU_C * (x + 0.044715 * x * x * x)))


def _gelu_grad(x):
    t = jnp.tanh(_GELU_C * (x + 0.044715 * x * x * x))
    return 0.5 * (1.0 + t) + 0.5 * x * (1.0 - t * t) * _GELU_C * (1.0 + 3.0 * 0.044715 * x * x)


def _rms(x):
    return lax.rsqrt(jnp.mean(x * x, axis=-1, keepdims=True) + EPS)


def _rms_bwd(x, g, dy):
    r = _rms(x)
    n = x * r
    dn = dy * g
    return r * (dn - n * jnp.mean(dn * n, axis=-1, keepdims=True)), dy * n


def _colsum(v):
    return jnp.sum(v, axis=0, keepdims=True)


def _rope_partner(x):
    lane = lax.broadcasted_iota(jnp.int32, x.shape, x.ndim - 1)
    return jnp.where(lane % 64 < 32, pltpu.roll(x, 96, x.ndim - 1), pltpu.roll(x, 32, x.ndim - 1))


def _rope_tables(L):
    rows_n = L // GRID_W
    rows = jnp.repeat(jnp.arange(rows_n), GRID_W).astype(F32)
    cols = jnp.tile(jnp.arange(GRID_W), rows_n).astype(F32)
    n_freq = HEAD_DIM // 4
    inv_freq = ROPE_THETA ** (-jnp.arange(n_freq, dtype=F32) / n_freq)
    ar, ac = rows[:, None] * inv_freq[None, :], cols[:, None] * inv_freq[None, :]
    cos = jnp.concatenate([jnp.cos(ar), jnp.cos(ar), jnp.cos(ac), jnp.cos(ac)], axis=-1)
    sin = jnp.concatenate([-jnp.sin(ar), jnp.sin(ar), -jnp.sin(ac), jnp.sin(ac)], axis=-1)
    return cos, sin


def _heads(v):
    return [v[:, h * HEAD_DIM:(h + 1) * HEAD_DIM] for h in range(v.shape[1] // HEAD_DIM)]


def _attn_prep(z, q_norm, k_norm, cos, sin):
    def fn(q, k, v, cos, sin, gq, gk):
        def one(xh, g):
            xn = xh * _rms(xh) * g
            return xn * cos + _rope_partner(xn) * sin
        qr = jnp.concatenate([one(h, gq) for h in _heads(q)], axis=1)
        kr = jnp.concatenate([one(h, gk) for h in _heads(k)], axis=1)
        return qr, kr, v
    return _rowwise(fn, [(z, Z_Q, D_ATTN), (z, Z_K, D_KV), (z, Z_V, D_KV), (cos, 0, HEAD_DIM), (sin, 0, HEAD_DIM)],
                    [q_norm, k_norm], [(D_ATTN, BF16), (D_KV, BF16), (D_KV, BF16)], name="attn_prep")


def _attn_prep_bwd(dqr, dkr, z, q_norm, k_norm, cos, sin):
    def fn(dqr, dkr, q, k, cos, sin, gq, gk):
        def one(dyh, xh, g):
            dn = dyh * cos + _rope_partner(dyh * sin)
            return _rms_bwd(xh, g, dn)
        rq = [one(a, b, gq) for a, b in zip(_heads(dqr), _heads(q))]
        rk = [one(a, b, gk) for a, b in zip(_heads(dkr), _heads(k))]
        dq = jnp.concatenate([r[0] for r in rq], axis=1)
        dk = jnp.concatenate([r[0] for r in rk], axis=1)
        return dq, dk, _colsum(sum(r[1] for r in rq)), _colsum(sum(r[1] for r in rk))
    return _rowwise(fn, [(dqr, 0, D_ATTN), (dkr, 0, D_KV), (z, Z_Q, D_ATTN), (z, Z_K, D_KV), (cos, 0, HEAD_DIM), (sin, 0, HEAD_DIM)],
                    [q_norm, k_norm], [(D_ATTN, BF16), (D_KV, BF16)], [(1, HEAD_DIM), (1, HEAD_DIM)], name="attn_prep_bwd")


_QK_T = (((1,), (1,)), ((), ()))
_TA = (((0,), (0,)), ((), ()))
_REP = N_HEADS // N_KV


def _attn_fwd(qr, kr, vb, *, tq=256):
    L = qr.shape[0]
    tq = min(tq, L)
    scale = HEAD_DIM ** -0.5

    def body(q_ref, k_ref, v_ref, o_ref, lse_ref):
        s = lax.dot_general(q_ref[...], k_ref[...], _QK_T, preferred_element_type=F32) * scale
        m = jnp.max(s, axis=-1, keepdims=True)
        p = jnp.exp(s - m)
        l = jnp.sum(p, axis=-1, keepdims=True)
        pn = (p * (1.0 / l)).astype(BF16)
        o_ref[...] = jnp.dot(pn, v_ref[...], preferred_element_type=F32)
        lse_ref[...] = m + jnp.log(l)

    return pl.pallas_call(
        body, name="attn_fwd",
        out_shape=[jax.ShapeDtypeStruct((L, D_ATTN), F32), jax.ShapeDtypeStruct((N_HEADS, L, 1), F32)],
        grid=(N_HEADS, L // tq),
        in_specs=[pl.BlockSpec((tq, HEAD_DIM), lambda h, i: (i, h)),
                  pl.BlockSpec((L, HEAD_DIM), lambda h, i: (0, h // _REP)),
                  pl.BlockSpec((L, HEAD_DIM), lambda h, i: (0, h // _REP))],
        out_specs=[pl.BlockSpec((tq, HEAD_DIM), lambda h, i: (i, h)),
                   pl.BlockSpec((None, tq, 1), lambda h, i: (h, i, 0))],
        compiler_params=_params(("parallel", "parallel")),
    )(qr, kr, vb)


def _attn_bwd(qr, kr, vb, do, lse, *, tq=256):
    L = qr.shape[0]
    tq = min(tq, L)
    scale = HEAD_DIM ** -0.5

    def body(q_ref, k_ref, v_ref, do_ref, lse_ref, dq_ref, dk_ref, dv_ref):
        @pl.when((pl.program_id(1) == 0) & (pl.program_id(2) == 0))
        def _():
            dk_ref[...] = jnp.zeros_like(dk_ref)
            dv_ref[...] = jnp.zeros_like(dv_ref)

        q, k, v, do = q_ref[...], k_ref[...], v_ref[...], do_ref[...]
        s = lax.dot_general(q, k, _QK_T, preferred_element_type=F32) * scale
        p = jnp.exp(s - lse_ref[...])
        dv_ref[...] += lax.dot_general(p.astype(BF16), do, _TA, preferred_element_type=F32)
        dp = lax.dot_general(do, v, _QK_T, preferred_element_type=F32)
        delta = jnp.sum(p * dp, axis=-1, keepdims=True)
        ds = (p * (dp - delta) * scale).astype(BF16)
        dq_ref[...] = jnp.dot(ds, k, preferred_element_type=F32)
        dk_ref[...] += lax.dot_general(ds, q, _TA, preferred_element_type=F32)

    head = lambda g, r, i: (i, g * _REP + r)
    return pl.pallas_call(
        body, name="attn_bwd",
        out_shape=[jax.ShapeDtypeStruct((L, D_ATTN), F32), jax.ShapeDtypeStruct((L, D_KV), F32), jax.ShapeDtypeStruct((L, D_KV), F32)],
        grid=(N_KV, _REP, L // tq),
        in_specs=[pl.BlockSpec((tq, HEAD_DIM), head),
                  pl.BlockSpec((L, HEAD_DIM), lambda g, r, i: (0, g)),
                  pl.BlockSpec((L, HEAD_DIM), lambda g, r, i: (0, g)),
                  pl.BlockSpec((tq, HEAD_DIM), head),
                  pl.BlockSpec((None, tq, 1), lambda g, r, i: (g * _REP + r, i, 0))],
        out_specs=[pl.BlockSpec((tq, HEAD_DIM), head),
                   pl.BlockSpec((L, HEAD_DIM), lambda g, r, i: (0, g)),
                   pl.BlockSpec((L, HEAD_DIM), lambda g, r, i: (0, g))],
        compiler_params=_params(("parallel", "arbitrary", "arbitrary")),
    )(qr, kr, vb, do, lse)


SSM_BLK = 8
SSM_NB = SSM_G // SSM_BLK
SSM_BW = SSM_BLK * SSM_H
SSM_SW = SSM_BLK * 2 * SSM_P
SSM_SEG = 8


def _seg_perm(a):
    L, C = a.shape
    return a.reshape(SSM_SEG, L // SSM_SEG, C).transpose(1, 0, 2).reshape(L, C)


def _seg_unperm(a):
    L, C = a.shape
    return a.reshape(L // SSM_SEG, SSM_SEG, C).transpose(1, 0, 2).reshape(L, C)


def _cplx_pow2(a, b, n):
    for _ in range(int(math.log2(n))):
        a, b = a * a - b * b, 2.0 * a * b
    return a, b


def _seg_scan(ref, a, b, T, reverse):
    npair = len(a)
    zero = jnp.zeros((SSM_SEG, 128), F32)

    def make_step(store):
        def step(t, carry):
            lt = (T - 1 - t) if reverse else t
            row = pl.multiple_of(lt * SSM_SEG, SSM_SEG)
            blk = ref[pl.ds(row, SSM_SEG), :]
            new = []
            for q in range(npair):
                re, im = carry[2 * q], carry[2 * q + 1]
                nre = a[q] * re - b[q] * im + blk[:, q * 256:q * 256 + 128]
                nim = a[q] * im + b[q] * re + blk[:, q * 256 + 128:q * 256 + 256]
                new += [nre, nim]
            if store:
                ref[pl.ds(row, SSM_SEG), :] = jnp.concatenate(new, axis=1)
            return tuple(new)
        return step

    ends = lax.fori_loop(0, T, make_step(False), (zero,) * (2 * npair))
    sub = lax.broadcasted_iota(jnp.int32, (SSM_SEG, 128), 0)
    keep = (sub != SSM_SEG - 1) if reverse else (sub != 0)
    shift = (SSM_SEG - 1) if reverse else 1
    init = []
    for q in range(npair):
        pa, pb = _cplx_pow2(a[q], b[q], T)
        xr, xi = zero, zero
        for _ in range(SSM_SEG - 1):
            fr = ends[2 * q] + pa * xr - pb * xi
            fi = ends[2 * q + 1] + pa * xi + pb * xr
            xr = jnp.where(keep, pltpu.roll(fr, shift, 0), 0.0)
            xi = jnp.where(keep, pltpu.roll(fi, shift, 0), 0.0)
        init += [xr, xi]
    lax.fori_loop(0, T, make_step(True), tuple(init))
    return init


def _lam_rows(are_ref, aim_ref, d, jb, npair):
    a, b = [], []
    for q in range(npair):
        j = jb * npair + q
        a.append(jnp.broadcast_to(are_ref[d, j:j + 1, :], (SSM_SEG, 128)))
        b.append(jnp.broadcast_to(aim_ref[d, j:j + 1, :], (SSM_SEG, 128)))
    return a, b


def _ssm_fwd(u_p, wb, wc, are, aim, dvec, cw=SSM_SW):
    L = u_p.shape[0]
    T = L // SSM_SEG
    RC = min(512, L)

    def body(u_ref, wb_ref, wc_ref, are_ref, aim_ref, d_ref, y_ref, x_scr):
        y_ref[...] = u_ref[...] * d_ref[...]
        for d in range(2):
            for jb in range(SSM_SW // cw):
                cols = slice(jb * cw, (jb + 1) * cw)

                def bu_chunk(c, _):
                    rows = pl.ds(pl.multiple_of(c * RC, RC), RC)
                    x_scr[rows, :] = jnp.dot(u_ref[rows, :].astype(BF16), wb_ref[d, :, cols], preferred_element_type=F32)
                    return 0
                lax.fori_loop(0, L // RC, bu_chunk, 0)
                a, b = _lam_rows(are_ref, aim_ref, d, jb, cw // 256)
                _seg_scan(x_scr, a, b, T, reverse=(d == 1))

                def y_chunk(c, _):
                    rows = pl.ds(pl.multiple_of(c * RC, RC), RC)
                    y_ref[rows, :] += jnp.dot(x_scr[rows, :].astype(BF16), wc_ref[d, cols, :], preferred_element_type=F32)
                    return 0
                lax.fori_loop(0, L // RC, y_chunk, 0)

    blk4 = lambda g: (g, 0, 0, 0)
    return pl.pallas_call(
        body, name="ssm_fwd",
        out_shape=jax.ShapeDtypeStruct((L, D_SSM), F32),
        grid=(SSM_NB,),
        in_specs=[pl.BlockSpec((L, SSM_BW), lambda g: (0, g)),
                  pl.BlockSpec((None, 2, SSM_BW, SSM_SW), blk4),
                  pl.BlockSpec((None, 2, SSM_SW, SSM_BW), blk4),
                  pl.BlockSpec((None, 2, 4, 128), blk4),
                  pl.BlockSpec((None, 2, 4, 128), blk4),
                  pl.BlockSpec((1, SSM_BW), lambda g: (0, g))],
        out_specs=pl.BlockSpec((L, SSM_BW), lambda g: (0, g)),
        scratch_shapes=[pltpu.VMEM((L, cw), F32)],
        compiler_params=_params(("parallel",)),
    )(u_p, wb, wc, are, aim, dvec)


def _ssm_bwd(u_p, dy_p, wb, wc, are, aim, dvec, cw=SSM_SW):
    L = u_p.shape[0]
    T = L // SSM_SEG
    RC = min(512, L)
    npair = cw // 256

    def lam_acc(acc, sb, xb):
        new = []
        for q in range(npair):
            sr, si = sb[:, q * 256:q * 256 + 128], sb[:, q * 256 + 128:q * 256 + 256]
            xr, xi = xb[:, q * 256:q * 256 + 128], xb[:, q * 256 + 128:q * 256 + 256]
            new += [acc[2 * q] + sr * xr + si * xi, acc[2 * q + 1] + si * xr - sr * xi]
        return tuple(new)

    def body(u_ref, dy_ref, wb_ref, wc_ref, are_ref, aim_ref, d_ref,
             du_ref, dwb_ref, dwc_ref, dare_ref, daim_ref, dd_ref, x_scr, s_scr):
        du_ref[...] = dy_ref[...] * d_ref[...]
        dd_ref[...] = _colsum(dy_ref[...] * u_ref[...])
        dwb_ref[...] = jnp.zeros_like(dwb_ref)
        dwc_ref[...] = jnp.zeros_like(dwc_ref)
        for d in range(2):
            rev = d == 1
            for jb in range(SSM_SW // cw):
                cols = slice(jb * cw, (jb + 1) * cw)

                def in_chunk(c, _):
                    rows = pl.ds(pl.multiple_of(c * RC, RC), RC)
                    x_scr[rows, :] = jnp.dot(u_ref[rows, :].astype(BF16), wb_ref[d, :, cols], preferred_element_type=F32)
                    s_scr[rows, :] = lax.dot_general(dy_ref[rows, :].astype(BF16), wc_ref[d, cols, :], _QK_T,
                                                     preferred_element_type=F32)
                    return 0
                lax.fori_loop(0, L // RC, in_chunk, 0)
                a, b = _lam_rows(are_ref, aim_ref, d, jb, npair)
                x_in = _seg_scan(x_scr, a, b, T, reverse=rev)
                _seg_scan(s_scr, a, [-v for v in b], T, reverse=not rev)

                def lam_step(t, acc):
                    lt = (T - 2 - t) if rev else (t + 1)
                    srow = pl.multiple_of(lt * SSM_SEG, SSM_SEG)
                    xrow = pl.multiple_of((lt + 1 if rev else lt - 1) * SSM_SEG, SSM_SEG)
                    return lam_acc(acc, s_scr[pl.ds(srow, SSM_SEG), :], x_scr[pl.ds(xrow, SSM_SEG), :])

                edge = pl.ds(((T - 1) if rev else 0) * SSM_SEG, SSM_SEG)
                acc0 = lam_acc((jnp.zeros((SSM_SEG, 128), F32),) * (2 * npair), s_scr[edge, :], jnp.concatenate(x_in, axis=1))
                acc = lax.fori_loop(0, T - 1, lam_step, acc0)
                for q in range(npair):
                    j = jb * npair + q
                    dare_ref[d, j:j + 1, :] = _colsum(acc[2 * q])
                    daim_ref[d, j:j + 1, :] = _colsum(acc[2 * q + 1])

                def out_chunk(c, _):
                    rows = pl.ds(pl.multiple_of(c * RC, RC), RC)
                    xs, ss = x_scr[rows, :].astype(BF16), s_scr[rows, :].astype(BF16)
                    uu, dd = u_ref[rows, :].astype(BF16), dy_ref[rows, :].astype(BF16)
                    dwc_ref[d, cols, :] += lax.dot_general(xs, dd, _TA, preferred_element_type=F32)
                    dwb_ref[d, :, cols] += lax.dot_general(uu, ss, _TA, preferred_element_type=F32)
                    du_ref[rows, :] += lax.dot_general(ss, wb_ref[d, :, cols], _QK_T, preferred_element_type=F32)
                    return 0
                lax.fori_loop(0, L // RC, out_chunk, 0)

    blk4 = lambda g: (g, 0, 0, 0)
    chan = pl.BlockSpec((L, SSM_BW), lambda g: (0, g))
    par_specs = [pl.BlockSpec((None, 2, SSM_BW, SSM_SW), blk4),
                 pl.BlockSpec((None, 2, SSM_SW, SSM_BW), blk4),
                 pl.BlockSpec((None, 2, 4, 128), blk4),
                 pl.BlockSpec((None, 2, 4, 128), blk4),
                 pl.BlockSpec((1, SSM_BW), lambda g: (0, g))]
    return pl.pallas_call(
        body, name="ssm_bwd",
        out_shape=[jax.ShapeDtypeStruct((L, D_SSM), F32),
                   jax.ShapeDtypeStruct((SSM_NB, 2, SSM_BW, SSM_SW), F32),
                   jax.ShapeDtypeStruct((SSM_NB, 2, SSM_SW, SSM_BW), F32),
                   jax.ShapeDtypeStruct((SSM_NB, 2, 4, 128), F32),
                   jax.ShapeDtypeStruct((SSM_NB, 2, 4, 128), F32),
                   jax.ShapeDtypeStruct((1, D_SSM), F32)],
        grid=(SSM_NB,),
        in_specs=[chan, chan] + par_specs,
        out_specs=[chan] + par_specs,
        scratch_shapes=[pltpu.VMEM((L, cw), F32), pltpu.VMEM((L, cw), F32)],
        compiler_params=_params(("parallel",)),
    )(u_p, dy_p, wb, wc, are, aim, dvec)


def _ssm_disc(a_re, a_im, log_dt, b_re, b_im):
    lam = lax.complex(jnp.minimum(a_re, -1e-4), a_im)
    dt = jnp.exp(log_dt)[..., None]
    lam_bar = jnp.exp(lam * dt)
    b_bar = ((lam_bar - 1.0) / lam)[..., None] * lax.complex(b_re, b_im)
    return jnp.real(lam_bar), jnp.imag(lam_bar), jnp.real(b_bar), jnp.imag(b_bar)


_EYE8 = np.eye(SSM_BLK, dtype=np.float32)


def _to_wb(bb_re, bb_im):
    bb = jnp.stack([bb_re, bb_im], axis=2).reshape(2, SSM_NB, SSM_BLK, 2, SSM_P, SSM_H)
    t = bb.transpose(1, 0, 5, 2, 3, 4)[:, :, None] * _EYE8[None, None, :, None, :, None, None]
    t = t.reshape(SSM_NB, 2, SSM_BLK, SSM_H, 4, 2, 2, SSM_P).transpose(0, 1, 2, 3, 4, 6, 5, 7)
    return t.reshape(SSM_NB, 2, SSM_BW, SSM_SW)


def _from_wb(dwb):
    t = dwb.reshape(SSM_NB, 2, SSM_BLK, SSM_H, 4, 2, 2, SSM_P).transpose(0, 1, 2, 3, 4, 6, 5, 7)
    t = t.reshape(SSM_NB, 2, SSM_BLK, SSM_H, SSM_BLK, 2, SSM_P)
    bb = jnp.sum(t * _EYE8[None, None, :, None, :, None, None], axis=2)
    bb = bb.transpose(1, 0, 3, 4, 5, 2).reshape(2, SSM_G, 2, SSM_P, SSM_H)
    return bb[:, :, 0], bb[:, :, 1]


def _to_wc(c_re, c_im):
    cc = jnp.stack([c_re, -c_im], axis=2).reshape(2, SSM_NB, SSM_BLK, 2, SSM_H, SSM_P)
    t = cc.transpose(1, 0, 2, 3, 5, 4)[:, :, :, :, :, None] * _EYE8.T[None, None, :, None, None, :, None]
    t = t.reshape(SSM_NB, 2, 4, 2, 2, SSM_P, SSM_BLK, SSM_H).transpose(0, 1, 2, 4, 3, 5, 6, 7)
    return t.reshape(SSM_NB, 2, SSM_SW, SSM_BW)


def _from_wc(dwc):
    t = dwc.reshape(SSM_NB, 2, 4, 2, 2, SSM_P, SSM_BLK, SSM_H).transpose(0, 1, 2, 4, 3, 5, 6, 7)
    t = t.reshape(SSM_NB, 2, SSM_BLK, 2, SSM_P, SSM_BLK, SSM_H)
    cc = jnp.sum(t * _EYE8.T[None, None, :, None, None, :, None], axis=5)
    cc = cc.transpose(1, 0, 2, 3, 5, 4).reshape(2, SSM_G, 2, SSM_H, SSM_P)
    return cc[:, :, 0], -cc[:, :, 1]


def _to_lam(v):
    return v.reshape(2, SSM_NB, 4, 128).transpose(1, 0, 2, 3)


def _from_lam(v):
    return v.transpose(1, 0, 2, 3).reshape(2, SSM_G, SSM_P)


_MESH = pl.DeviceIdType.MESH
_ANY = pl.BlockSpec(memory_space=pl.ANY)
_BIG = (("w_in", (D_MODEL, D_IN), 1, D_IN // N_CHIPS),
        ("w_glu", (D_SSM, 2 * D_SSM), 1, 2 * D_SSM // N_CHIPS),
        ("w_out", (D_ATTN + D_SSM, D_MODEL), 0, (D_ATTN + D_SSM) // N_CHIPS),
        ("w_ple_gate", (D_MODEL, D_MODEL), 0, D_MODEL // N_CHIPS),
        ("w_ple_proj", (PLE_DIM, D_MODEL), 1, D_MODEL // N_CHIPS))


def _place():
    x, y, c = lax.axis_index("x"), lax.axis_index("y"), lax.axis_index("c")
    return x, y, c, [(1 - x, y), (x, 1 - y), (1 - x, 1 - y)]


def _gather_weights(shards):
    nt = len(_BIG)

    def body(*refs):
        srcs, dsts = refs[:nt], refs[nt:2 * nt]
        send_sems, recv_sems, loc_sems = refs[2 * nt:]
        x, y, c, chips = _place()

        def win(t, kk):
            _, _, axis, sz = _BIG[t]
            sl = pl.ds(pl.multiple_of(kk * sz, sz), sz)
            return dsts[t].at[:, sl] if axis == 1 else dsts[t].at[sl, :]

        def remote(j, t, kk):
            px, py = chips[j]
            return pltpu.make_async_remote_copy(src_ref=srcs[t], dst_ref=win(t, kk), send_sem=send_sems.at[j, t],
                                                recv_sem=recv_sems.at[j, t], device_id=(px, py, c), device_id_type=_MESH)

        k = 2 * x + y
        local = [pltpu.make_async_copy(srcs[t], win(t, k), loc_sems.at[t]) for t in range(nt)]
        sends = [remote(j, t, k) for j in range(3) for t in range(nt)]
        for cp in local + sends:
            cp.start()
        for j, (px, py) in enumerate(chips):
            for t in range(nt):
                remote(j, t, 2 * px + py).wait_recv()
        for cp in sends:
            cp.wait_send()
        for cp in local:
            cp.wait()

    return pl.pallas_call(
        body, name="gather_weights",
        out_shape=[jax.ShapeDtypeStruct(shape, BF16) for _, shape, _, _ in _BIG],
        in_specs=[_ANY] * nt, out_specs=[_ANY] * nt,
        scratch_shapes=[pltpu.SemaphoreType.DMA((3, nt)), pltpu.SemaphoreType.DMA((3, nt)), pltpu.SemaphoreType.DMA((nt,))],
    )(*shards)


MSG_W = 1024
_PIECE = tuple((shape[0] // 2, sz) if ax == 1 else (sz // 2, shape[1]) for _, shape, ax, sz in _BIG)
BIG_ROWS = sum(r * c for r, c in _PIECE) // MSG_W
SMALL_ROWS = 192
MSG_ROWS = BIG_ROWS + SMALL_ROWS
N_SMALL = 8 * SMALL_ROWS * MSG_W


def _pack_messages(big_grads, small_flat):
    parts = []
    for g, (_, shape, ax, sz), (pr, pc) in zip(big_grads, _BIG, _PIECE):
        if ax == 1:
            t = g.reshape(2, pr, N_CHIPS, pc).transpose(2, 0, 1, 3)
        else:
            t = g.reshape(N_CHIPS, 2, pr, pc)
        parts.append(t.reshape(N_CHIPS, 2, pr * pc))
    parts.append(small_flat.reshape(N_CHIPS, 2, SMALL_ROWS * MSG_W))
    return jnp.concatenate(parts, axis=-1).reshape(N_CHIPS, 2, MSG_ROWS, MSG_W)


def _unpack_big(big):
    flat = big.reshape(2, BIG_ROWS * MSG_W)
    out, off = [], 0
    for (pr, pc) in _PIECE:
        out.append(flat[:, off:off + pr * pc].reshape(2 * pr, pc))
        off += pr * pc
    return out


def _sibling_exchange(src):
    n = src.shape[0]

    def body(src_ref, dst_ref, send_sems, recv_sems):
        x, y, c, _ = _place()
        cps = [pltpu.make_async_remote_copy(src_ref=src_ref.at[i], dst_ref=dst_ref.at[i], send_sem=send_sems.at[i],
                                            recv_sem=recv_sems.at[i], device_id=(x, y, 1 - c), device_id_type=_MESH)
               for i in range(n)]
        for cp in cps:
            cp.start()
        for cp in cps:
            cp.wait()

    return pl.pallas_call(
        body, name="grad_sibling_exchange",
        out_shape=jax.ShapeDtypeStruct(src.shape, src.dtype),
        in_specs=[_ANY], out_specs=_ANY,
        scratch_shapes=[pltpu.SemaphoreType.DMA((n,)), pltpu.SemaphoreType.DMA((n,))],
    )(src)


def _chip_exchange(src):
    def body(src_ref, dst_ref, send_sems, recv_sems):
        x, y, c, chips = _place()
        cps = [pltpu.make_async_remote_copy(src_ref=src_ref.at[2 * px + py], dst_ref=dst_ref.at[j], send_sem=send_sems.at[j],
                                            recv_sem=recv_sems.at[j], device_id=(px, py, c), device_id_type=_MESH)
               for j, (px, py) in enumerate(chips)]
        for cp in cps:
            cp.start()
        for cp in cps:
            cp.wait()

    return pl.pallas_call(
        body, name="grad_chip_exchange",
        out_shape=jax.ShapeDtypeStruct((3,) + src.shape[1:], src.dtype),
        in_specs=[_ANY], out_specs=_ANY,
        scratch_shapes=[pltpu.SemaphoreType.DMA((3,)), pltpu.SemaphoreType.DMA((3,))],
    )(src)


def _final_exchange(total):
    def body(src_ref, big_ref, small_ref, send_sems, recv_sems, loc_sems):
        x, y, c, chips = _place()
        me = 4 * x + 2 * y + c
        src_big, src_small = src_ref.at[pl.ds(0, BIG_ROWS)], src_ref.at[pl.ds(BIG_ROWS, SMALL_ROWS)]
        others = [(x, y, 1 - c)] + [(px, py, cc) for (px, py) in chips for cc in (c, 1 - c)]

        def small_copy(i, slot):
            return pltpu.make_async_remote_copy(src_ref=src_small, dst_ref=small_ref.at[slot], send_sem=send_sems.at[i],
                                                recv_sem=recv_sems.at[i], device_id=others[i], device_id_type=_MESH)

        def big_copy(half):
            return pltpu.make_async_remote_copy(src_ref=src_big, dst_ref=big_ref.at[half], send_sem=send_sems.at[7],
                                                recv_sem=recv_sems.at[7], device_id=others[0], device_id_type=_MESH)

        local = [pltpu.make_async_copy(src_big, big_ref.at[c], loc_sems.at[0]),
                 pltpu.make_async_copy(src_small, small_ref.at[me], loc_sems.at[1])]
        sends = [small_copy(i, me) for i in range(7)] + [big_copy(c)]
        for cp in local + sends:
            cp.start()
        for i, (px, py, pc) in enumerate(others):
            small_copy(i, 4 * px + 2 * py + pc).wait_recv()
        big_copy(1 - c).wait_recv()
        for cp in sends:
            cp.wait_send()
        for cp in local:
            cp.wait()

    return pl.pallas_call(
        body, name="grad_final_exchange",
        out_shape=[jax.ShapeDtypeStruct((2, BIG_ROWS, MSG_W), F32), jax.ShapeDtypeStruct((8, SMALL_ROWS, MSG_W), F32)],
        in_specs=[_ANY], out_specs=[_ANY, _ANY],
        scratch_shapes=[pltpu.SemaphoreType.DMA((8,)), pltpu.SemaphoreType.DMA((8,)), pltpu.SemaphoreType.DMA((2,))],
    )(total)


def _add_n(arrs, name):
    def fn(*vs):
        t = vs[0]
        for v in vs[1:]:
            t = t + v
        return t
    W = arrs[0].shape[1]
    return _rowwise(fn, [(a, 0, W) for a in arrs], [], [(W, F32)], tr=128, name=name)[0]


def _reduce_gradients(big_grads, small_flat):
    c = lax.axis_index("c")
    k = 2 * lax.axis_index("x") + lax.axis_index("y")
    msg = _pack_messages(big_grads, small_flat)
    mine = lax.dynamic_index_in_dim(msg, c, 1, keepdims=False)
    to_sibling = lax.dynamic_index_in_dim(msg, 1 - c, 1, keepdims=False)
    from_sibling = _sibling_exchange(to_sibling)
    flat = lambda a: a.reshape(-1, MSG_W)
    chip_sum = _add_n([flat(mine), flat(from_sibling)], "grad_sum_chip").reshape(N_CHIPS, MSG_ROWS, MSG_W)
    from_chips = _chip_exchange(chip_sum)
    own = lax.dynamic_index_in_dim(chip_sum, k, 0, keepdims=False)
    total = _add_n([own, from_chips[0], from_chips[1], from_chips[2]], "grad_sum_all")
    big, small = _final_exchange(total)
    return _unpack_big(big), small.reshape(-1)


def _adamw(w, g, m, v, name):
    def fn(w, g, m, v):
        m = ADAM_B1 * m + (1.0 - ADAM_B1) * g
        v = ADAM_B2 * v + (1.0 - ADAM_B2) * (g * g)
        m_hat = m / (1.0 - ADAM_B1 ** ADAM_STEP)
        v_hat = v / (1.0 - ADAM_B2 ** ADAM_STEP)
        return -ADAM_LR * (m_hat / (jnp.sqrt(v_hat) + ADAM_EPS) + ADAM_WD * w), m, v
    W = w.shape[1]
    return _rowwise(fn, [(a, 0, W) for a in (w, g, m, v)], [], [(W, F32)] * 3, tr=128, name=name)


def _chunks(arr, off, width, w=512):
    return [(arr, off + i * w, w) for i in range(width // w)]


def _cat(vs):
    return jnp.concatenate(vs, axis=1)


def _forward_backward(x, p_b, tgt, full, small):
    L = x.shape[0]
    w_in, w_glu, w_out, w_pg, w_pp = full
    row = lambda v: v.reshape(1, -1)
    g_mix, g_ple, g_fin = row(small["norm_mix"]), row(small["norm_ple"]), row(small["norm_final"])
    gq, gk, b_glu = row(small["q_norm"]), row(small["k_norm"]), row(small["b_glu"])
    cos, sin = _rope_tables(L)

    hn_b, = _rowwise(lambda x, g: x * _rms(x) * g, [(x, 0, D_MODEL)], [g_mix], [(D_MODEL, BF16)], name="norm_mix")
    z = _matmul(hn_b, w_in, name="mm_in")
    qr, kr, vb = _attn_prep(z, gq, gk, cos, sin)
    o, lse = _attn_fwd(qr, kr, vb)

    ssm_names = ("ssm_a_re", "ssm_a_im", "ssm_log_dt", "ssm_b_re", "ssm_b_im")
    (lre, lim, bre, bim), disc_vjp = jax.vjp(_ssm_disc, *[small[n][0] for n in ssm_names])
    wb, wc = _to_wb(bre, bim).astype(BF16), _to_wc(small["ssm_c_re"][0], small["ssm_c_im"][0]).astype(BF16)
    are, aim, dvec = _to_lam(lre), _to_lam(lim), row(small["ssm_d"])
    u_p = _seg_perm(z[:, Z_U:Z_U + D_SSM])
    y_s = _seg_unperm(_ssm_fwd(u_p, wb, wc, are, aim, dvec))
    ge_b, = _rowwise(_gelu, [(y_s, 0, D_SSM)], [], [(D_SSM, BF16)], name="gelu")
    glu = _matmul(ge_b, w_glu, name="mm_glu")

    def merge(ga0, ga1, a, b, gs0, gs1, o, bias):
        sa, _ = _silu_and_grad(_cat([ga0, ga1]))
        ss, _ = _silu_and_grad(_cat([gs0, gs1]))
        y2 = (a + bias[:, :D_SSM]) * _sig(b + bias[:, D_SSM:])
        return _cat([o * sa, y2 * ss])
    merge_rows = _chunks(z, Z_GA, D_ATTN) + [(glu, 0, D_SSM), (glu, D_SSM, D_SSM)] + _chunks(z, Z_GS, D_SSM) + [(o, 0, D_ATTN)]
    cat_b, = _rowwise(merge, merge_rows, [b_glu], [(D_MODEL, BF16)], name="merge")
    t_out = _matmul(cat_b, w_out, name="mm_out")

    def resid(x, t, g):
        h1 = x + t
        return h1, h1 * _rms(h1) * g
    h1, hp_b = _rowwise(resid, [(x, 0, D_MODEL), (t_out, 0, D_MODEL)], [g_ple], [(D_MODEL, F32), (D_MODEL, BF16)], name="resid_norm")
    gl = _matmul(hp_b, w_pg, name="mm_ple_gate")
    pp = _matmul(p_b, w_pp, name="mm_ple_proj")

    def head(h1, gl, pp, tgt, g):
        gate = _sig(gl)
        h2 = h1 + gate * pp
        r = _rms(h2)
        n = h2 * r
        err = n * g - tgt
        dy = err * (1.0 / D_MODEL)
        dn = dy * g
        dh2 = r * (dn - n * jnp.mean(dn * n, axis=-1, keepdims=True))
        dgate = dh2 * pp
        return dh2, dh2 * gate, dgate * gate * (1.0 - gate), _colsum(dy * n), _colsum(0.5 * err * err * (1.0 / D_MODEL))
    dh2, dpp_b, dgl_b, dg_fin, loss_cols = _rowwise(
        head, [(a, 0, D_MODEL) for a in (h1, gl, pp, tgt)], [g_fin],
        [(D_MODEL, F32), (D_MODEL, BF16), (D_MODEL, BF16)], [(1, D_MODEL), (1, D_MODEL)], name="loss_head")

    dw_pp = _matmul(p_b, dpp_b, ta=True, name="mm_d_w_ple_proj")
    dw_pg = _matmul(hp_b, dgl_b, ta=True, name="mm_d_w_ple_gate")
    dhp = _matmul(dgl_b, w_pg, tb=True, name="mm_d_hp")

    def resid_bwd(dhp, h1, dh2, g):
        dx, dg = _rms_bwd(h1, g, dhp)
        dh1 = dh2 + dx
        return dh1, dh1, _colsum(dg)
    dh1, dh1_b, dg_ple = _rowwise(resid_bwd, [(a, 0, D_MODEL) for a in (dhp, h1, dh2)], [g_ple],
                                  [(D_MODEL, F32), (D_MODEL, BF16)], [(1, D_MODEL)], name="resid_norm_bwd")
    dw_out = _matmul(cat_b, dh1_b, ta=True, name="mm_d_w_out")
    dcat = _matmul(dh1_b, w_out, tb=True, name="mm_d_cat")

    def merge_bwd(dya, dys, ga0, ga1, a, b, gs0, gs1, o, bias):
        ga, gs = _cat([ga0, ga1]), _cat([gs0, gs1])
        sa, dsa = _silu_and_grad(ga)
        ss, dss = _silu_and_grad(gs)
        a, sb = a + bias[:, :D_SSM], _sig(b + bias[:, D_SSM:])
        dy2 = dys * ss
        dglu = _cat([dy2 * sb, dy2 * a * sb * (1.0 - sb)])
        return dya * sa, dya * o * dsa, dys * (a * sb) * dss, dglu, _colsum(dglu)
    do_b, dga_b, dgs_b, dglu_b, db_glu = _rowwise(
        merge_bwd, [(dcat, 0, D_ATTN), (dcat, D_ATTN, D_SSM)] + merge_rows, [b_glu],
        [(D_ATTN, BF16), (D_ATTN, BF16), (D_SSM, BF16), (2 * D_SSM, BF16)], [(1, 2 * D_SSM)], name="merge_bwd")
    dw_glu = _matmul(ge_b, dglu_b, ta=True, name="mm_d_w_glu")
    dge = _matmul(dglu_b, w_glu, tb=True, name="mm_d_ge")
    dy_s, = _rowwise(lambda dge, y: dge * _gelu_grad(y), [(dge, 0, D_SSM), (y_s, 0, D_SSM)], [], [(D_SSM, F32)], name="gelu_bwd")
    du_p, dwb, dwc, dare, daim, d_ssm_d = _ssm_bwd(u_p, _seg_perm(dy_s), wb, wc, are, aim, dvec)
    db_re_bar, db_im_bar = _from_wb(dwb)
    dc_re, dc_im = _from_wc(dwc)
    da_re, da_im, dlog_dt, db_re, db_im = disc_vjp((_from_lam(dare), _from_lam(daim), db_re_bar, db_im_bar))

    dqr, dkr, dv = _attn_bwd(qr, kr, vb, do_b, lse)
    dq_b, dk_b, dgq, dgk = _attn_prep_bwd(dqr, dkr, z, gq, gk, cos, sin)
    dz_b = _cat([dq_b, dk_b, dv.astype(BF16), dga_b, _seg_unperm(du_p).astype(BF16), dgs_b])
    dw_in = _matmul(hn_b, dz_b, ta=True, name="mm_d_w_in")
    dhn = _matmul(dz_b, w_in, tb=True, name="mm_d_hn")

    def norm_bwd(dhn, x, dh1, g):
        dx, dg = _rms_bwd(x, g, dhn)
        return dh1 + dx, _colsum(dg)
    grad_x, dg_mix = _rowwise(norm_bwd, [(a, 0, D_MODEL) for a in (dhn, x, dh1)], [g_mix], [(D_MODEL, F32)], [(1, D_MODEL)],
                              name="norm_mix_bwd")

    small_grads = {"norm_mix": dg_mix, "q_norm": dgq, "k_norm": dgk, "ssm_a_re": da_re, "ssm_a_im": da_im, "ssm_log_dt": dlog_dt,
                   "ssm_b_re": db_re, "ssm_b_im": db_im, "ssm_c_re": dc_re, "ssm_c_im": dc_im, "ssm_d": d_ssm_d,
                   "b_glu": db_glu, "norm_ple": dg_ple, "norm_final": dg_fin}
    return jnp.sum(loss_cols), grad_x, [dw_in, dw_glu, dw_out, dw_pg, dw_pp], small_grads


_SMALL = ("norm_mix", "q_norm", "k_norm", "ssm_a_re", "ssm_a_im", "ssm_log_dt", "ssm_b_re", "ssm_b_im", "ssm_c_re", "ssm_c_im",
          "ssm_d", "b_glu", "norm_ple", "norm_final")
_WEIGHTS = ("norm_mix", "w_in", "q_norm", "k_norm", "ssm_a_re", "ssm_a_im", "ssm_log_dt", "ssm_b_re", "ssm_b_im", "ssm_c_re",
            "ssm_c_im", "ssm_d", "w_glu", "b_glu", "w_out", "norm_ple", "w_ple_gate", "w_ple_proj", "norm_final")


def _flat_small(d):
    flat = jnp.concatenate([d[n].reshape(-1).astype(F32) for n in _SMALL])
    return jnp.pad(flat, (0, N_SMALL - flat.shape[0]))


def _split_small(flat, like):
    out, off = {}, 0
    for n in _SMALL:
        sz = math.prod(like[n].shape)
        out[n] = flat[off:off + sz].reshape(like[n].shape)
        off += sz
    return out


def kernel(x, p, norm_mix, w_in, q_norm, k_norm, ssm_a_re, ssm_a_im, ssm_log_dt, ssm_b_re, ssm_b_im, ssm_c_re, ssm_c_im, ssm_d, w_glu, b_glu, w_out, norm_ple, w_ple_gate, w_ple_proj, norm_final, loss_target, m_norm_mix, m_w_in, m_q_norm, m_k_norm, m_ssm_a_re, m_ssm_a_im, m_ssm_log_dt, m_ssm_b_re, m_ssm_b_im, m_ssm_c_re, m_ssm_c_im, m_ssm_d, m_w_glu, m_b_glu, m_w_out, m_norm_ple, m_w_ple_gate, m_w_ple_proj, m_norm_final, v_norm_mix, v_w_in, v_q_norm, v_k_norm, v_ssm_a_re, v_ssm_a_im, v_ssm_log_dt, v_ssm_b_re, v_ssm_b_im, v_ssm_c_re, v_ssm_c_im, v_ssm_d, v_w_glu, v_b_glu, v_w_out, v_norm_ple, v_w_ple_gate, v_w_ple_proj, v_norm_final):
    w = dict(norm_mix=norm_mix, w_in=w_in, q_norm=q_norm, k_norm=k_norm, ssm_a_re=ssm_a_re, ssm_a_im=ssm_a_im,
             ssm_log_dt=ssm_log_dt, ssm_b_re=ssm_b_re, ssm_b_im=ssm_b_im, ssm_c_re=ssm_c_re, ssm_c_im=ssm_c_im, ssm_d=ssm_d,
             w_glu=w_glu, b_glu=b_glu, w_out=w_out, norm_ple=norm_ple, w_ple_gate=w_ple_gate, w_ple_proj=w_ple_proj,
             norm_final=norm_final)
    m = dict(norm_mix=m_norm_mix, w_in=m_w_in, q_norm=m_q_norm, k_norm=m_k_norm, ssm_a_re=m_ssm_a_re, ssm_a_im=m_ssm_a_im,
             ssm_log_dt=m_ssm_log_dt, ssm_b_re=m_ssm_b_re, ssm_b_im=m_ssm_b_im, ssm_c_re=m_ssm_c_re, ssm_c_im=m_ssm_c_im,
             ssm_d=m_ssm_d, w_glu=m_w_glu, b_glu=m_b_glu, w_out=m_w_out, norm_ple=m_norm_ple, w_ple_gate=m_w_ple_gate,
             w_ple_proj=m_w_ple_proj, norm_final=m_norm_final)
    v = dict(norm_mix=v_norm_mix, w_in=v_w_in, q_norm=v_q_norm, k_norm=v_k_norm, ssm_a_re=v_ssm_a_re, ssm_a_im=v_ssm_a_im,
             ssm_log_dt=v_ssm_log_dt, ssm_b_re=v_ssm_b_re, ssm_b_im=v_ssm_b_im, ssm_c_re=v_ssm_c_re, ssm_c_im=v_ssm_c_im,
             ssm_d=v_ssm_d, w_glu=v_w_glu, b_glu=v_b_glu, w_out=v_w_out, norm_ple=v_norm_ple, w_ple_gate=v_w_ple_gate,
             w_ple_proj=v_w_ple_proj, norm_final=v_norm_final)
    big_names = [n for n, _, _, _ in _BIG]

    full = _gather_weights([w[n][0].astype(BF16) for n in big_names])
    small = {n: w[n] for n in _SMALL}
    loss_part, grad_x, big_grads, small_grads = _forward_backward(
        x[0], p[0, 0].astype(BF16), loss_target[0], full, small)
    loss = lax.psum(loss_part, ("x", "y", "c"))

    big_red, small_red = _reduce_gradients(big_grads, _flat_small(small_grads))
    grads = _split_small(small_red, w)
    delta, new_m, new_v = {}, {}, {}
    for n, g in zip(big_names, big_red):
        grads[n] = g[None]
        d_, m_, v_ = _adamw(w[n][0], g, m[n][0], v[n][0], "adamw_" + n)
        delta[n], new_m[n], new_v[n] = d_[None], m_[None], v_[None]
    d_, m_, v_ = _adamw(*[a.reshape(-1, MSG_W) for a in (_flat_small(w), small_red, _flat_small(m), _flat_small(v))], "adamw_small")
    delta.update(_split_small(d_.reshape(-1), w))
    new_m.update(_split_small(m_.reshape(-1), w))
    new_v.update(_split_small(v_.reshape(-1), w))
    return (loss, grad_x[None], *[grads[n] for n in _WEIGHTS], *[delta[n] for n in _WEIGHTS],
            *[new_m[n] for n in _WEIGHTS], *[new_v[n] for n in _WEIGHTS])
```

```python
import functools
import math

import jax
import jax.numpy as jnp
import numpy as np
from jax import lax
from jax.experimental import pallas as pl
from jax.experimental.pallas import tpu as pltpu

D_MODEL = 2048
GRID_W = 64
PLE_DIM = 256
D_ATTN = 1024
N_HEADS = 8
N_KV = 2
HEAD_DIM = 128
ROPE_THETA = 10000.0
D_SSM = 1024
SSM_H = 16
SSM_G = 64
SSM_P = 64
D_KV = N_KV * HEAD_DIM
D_IN = 2 * D_ATTN + 2 * D_KV + 2 * D_SSM
EPS = 1e-6
Z_Q, Z_K, Z_V, Z_GA, Z_U, Z_GS = 0, 1024, 1280, 1536, 2560, 3584

ADAM_LR, ADAM_B1, ADAM_B2, ADAM_EPS, ADAM_WD, ADAM_STEP = 0.001, 0.9, 0.999, 1e-08, 0.01, 10

N_CHIPS = 4
VMEM_LIMIT_V7X = 56 * 1024 * 1024
F32 = jnp.float32
BF16 = jnp.bfloat16


def _params(sem, vmem=VMEM_LIMIT_V7X):
    return pltpu.CompilerParams(dimension_semantics=sem, vmem_limit_bytes=vmem)


def _matmul(a, b, *, ta=False, tb=False, out_dtype=F32, tm=1024, tn=512, tk=512, name):
    M, K = (a.shape[1], a.shape[0]) if ta else a.shape
    N = b.shape[0] if tb else b.shape[1]
    tm, tn, tk = min(tm, M), min(tn, N), min(tk, K)
    assert M % tm == 0 and N % tn == 0 and K % tk == 0, (name, M, N, K)
    nk = K // tk
    dims = (((0 if ta else 1,), (1 if tb else 0,)), ((), ()))

    def body(a_ref, b_ref, o_ref, acc_ref):
        k = pl.program_id(2)

        @pl.when(k == 0)
        def _():
            acc_ref[...] = jnp.zeros_like(acc_ref)

        acc_ref[...] += lax.dot_general(a_ref[...], b_ref[...], dims, preferred_element_type=F32)

        @pl.when(k == nk - 1)
        def _():
            o_ref[...] = acc_ref[...].astype(o_ref.dtype)

    a_spec = pl.BlockSpec((tk, tm), lambda i, j, k: (k, i)) if ta else pl.BlockSpec((tm, tk), lambda i, j, k: (i, k))
    b_spec = pl.BlockSpec((tn, tk), lambda i, j, k: (j, k)) if tb else pl.BlockSpec((tk, tn), lambda i, j, k: (k, j))
    return pl.pallas_call(
        body, name=name,
        out_shape=jax.ShapeDtypeStruct((M, N), out_dtype),
        grid=(M // tm, N // tn, nk),
        in_specs=[a_spec, b_spec],
        out_specs=pl.BlockSpec((tm, tn), lambda i, j, k: (i, j)),
        scratch_shapes=[pltpu.VMEM((tm, tn), F32)],
        compiler_params=_params(("parallel", "parallel", "arbitrary")),
    )(a, b)


def _rowwise(fn, rows, consts, outs, accs=(), *, tr=256, name):
    L = rows[0][0].shape[0]
    tr = math.gcd(tr, L)
    assert tr % 8 == 0 or tr == L, (name, L, tr)
    n_in, n_c, n_o, n_a = len(rows), len(consts), len(outs), len(accs)

    def body(*refs):
        ins = [r[...] for r in refs[:n_in + n_c]]
        res = fn(*ins)
        if not isinstance(res, (tuple, list)):
            res = (res,)
        o_refs = refs[n_in + n_c:n_in + n_c + n_o]
        a_refs = refs[n_in + n_c + n_o:]
        for r, v in zip(o_refs, res[:n_o]):
            r[...] = v.astype(r.dtype)
        if n_a:
            first = pl.program_id(0) == 0

            @pl.when(first)
            def _():
                for r, v in zip(a_refs, res[n_o:]):
                    r[...] = v.astype(F32)

            @pl.when(jnp.logical_not(first))
            def _():
                for r, v in zip(a_refs, res[n_o:]):
                    r[...] += v.astype(F32)

    in_specs = []
    for arr, off, w in rows:
        assert off % w == 0, (name, off, w)
        in_specs.append(pl.BlockSpec((tr, w), functools.partial(lambda i, c: (i, c), c=off // w)))
    for c in consts:
        in_specs.append(pl.BlockSpec(c.shape, lambda i: (0, 0)))
    out_shape = [jax.ShapeDtypeStruct((L, w), dt) for w, dt in outs] + [jax.ShapeDtypeStruct(s, F32) for s in accs]
    out_specs = [pl.BlockSpec((tr, w), lambda i: (i, 0)) for w, _ in outs] + [pl.BlockSpec(s, lambda i: (0, 0)) for s in accs]
    res = pl.pallas_call(
        body, name=name,
        out_shape=out_shape,
        grid=(L // tr,),
        in_specs=in_specs,
        out_specs=out_specs,
        compiler_params=_params(("arbitrary",) if n_a else ("parallel",)),
    )(*[r[0] for r in rows], *consts)
    return res


def _sig(x):
    return jax.nn.sigmoid(x)


def _silu_and_grad(x):
    s = _sig(x)
    return x * s, s * (1.0 + x * (1.0 - s))


_GELU_C = math.sqrt(2.0 / math.pi)


def _gelu(x):
    return 0.5 * x * (1.0 + jnp.tanh(_GELU_C * (x + 0.044715 * x * x * x)))


def _gelu_grad(x):
    t = jnp.tanh(_GELU_C * (x + 0.044715 * x * x * x))
    return 0.5 * (1.0 + t) + 0.5 * x * (1.0 - t * t) * _GELU_C * (1.0 + 3.0 * 0.044715 * x * x)


def _rms(x):
    return lax.rsqrt(jnp.mean(x * x, axis=-1, keepdims=True) + EPS)


def _rms_bwd(x, g, dy):
    r = _rms(x)
    n = x * r
    dn = dy * g
    return r * (dn - n * jnp.mean(dn * n, axis=-1, keepdims=True)), dy * n


def _colsum(v):
    return jnp.sum(v, axis=0, keepdims=True)


def _rope_partner(x):
    lane = lax.broadcasted_iota(jnp.int32, x.shape, x.ndim - 1)
    return jnp.where(lane % 64 < 32, pltpu.roll(x, 96, x.ndim - 1), pltpu.roll(x, 32, x.ndim - 1))


def _rope_tables(L):
    rows_n = L // GRID_W
    rows = jnp.repeat(jnp.arange(rows_n), GRID_W).astype(F32)
    cols = jnp.tile(jnp.arange(GRID_W), rows_n).astype(F32)
    n_freq = HEAD_DIM // 4
    inv_freq = ROPE_THETA ** (-jnp.arange(n_freq, dtype=F32) / n_freq)
    ar, ac = rows[:, None] * inv_freq[None, :], cols[:, None] * inv_freq[None, :]
    cos = jnp.concatenate([jnp.cos(ar), jnp.cos(ar), jnp.cos(ac), jnp.cos(ac)], axis=-1)
    sin = jnp.concatenate([-jnp.sin(ar), jnp.sin(ar), -jnp.sin(ac), jnp.sin(ac)], axis=-1)
    return cos, sin


def _heads(v):
    return [v[:, h * HEAD_DIM:(h + 1) * HEAD_DIM] for h in range(v.shape[1] // HEAD_DIM)]


def _attn_prep(z, q_norm, k_norm, cos, sin):
    def fn(q, k, v, cos, sin, gq, gk):
        def one(xh, g):
            xn = xh * _rms(xh) * g
            return xn * cos + _rope_partner(xn) * sin
        qr = jnp.concatenate([one(h, gq) for h in _heads(q)], axis=1)
        kr = jnp.concatenate([one(h, gk) for h in _heads(k)], axis=1)
        return qr, kr, v
    return _rowwise(fn, [(z, Z_Q, D_ATTN), (z, Z_K, D_KV), (z, Z_V, D_KV), (cos, 0, HEAD_DIM), (sin, 0, HEAD_DIM)],
                    [q_norm, k_norm], [(D_ATTN, BF16), (D_KV, BF16), (D_KV, BF16)], name="attn_prep")


def _attn_prep_bwd(dqr, dkr, z, q_norm, k_norm, cos, sin):
    def fn(dqr, dkr, q, k, cos, sin, gq, gk):
        def one(dyh, xh, g):
            dn = dyh * cos + _rope_partner(dyh * sin)
            return _rms_bwd(xh, g, dn)
        rq = [one(a, b, gq) for a, b in zip(_heads(dqr), _heads(q))]
        rk = [one(a, b, gk) for a, b in zip(_heads(dkr), _heads(k))]
        dq = jnp.concatenate([r[0] for r in rq], axis=1)
        dk = jnp.concatenate([r[0] for r in rk], axis=1)
        return dq, dk, _colsum(sum(r[1] for r in rq)), _colsum(sum(r[1] for r in rk))
    return _rowwise(fn, [(dqr, 0, D_ATTN), (dkr, 0, D_KV), (z, Z_Q, D_ATTN), (z, Z_K, D_KV), (cos, 0, HEAD_DIM), (sin, 0, HEAD_DIM)],
                    [q_norm, k_norm], [(D_ATTN, BF16), (D_KV, BF16)], [(1, HEAD_DIM), (1, HEAD_DIM)], name="attn_prep_bwd")


_QK_T = (((1,), (1,)), ((), ()))
_TA = (((0,), (0,)), ((), ()))
_REP = N_HEADS // N_KV


def _attn_fwd(qr, kr, vb, *, tq=256):
    L = qr.shape[0]
    tq = min(tq, L)
    scale = HEAD_DIM ** -0.5

    def body(q_ref, k_ref, v_ref, o_ref, lse_ref):
        s = lax.dot_general(q_ref[...], k_ref[...], _QK_T, preferred_element_type=F32) * scale
        m = jnp.max(s, axis=-1, keepdims=True)
        p = jnp.exp(s - m)
        l = jnp.sum(p, axis=-1, keepdims=True)
        pn = (p * (1.0 / l)).astype(BF16)
        o_ref[...] = jnp.dot(pn, v_ref[...], preferred_element_type=F32)
        lse_ref[...] = m + jnp.log(l)

    return pl.pallas_call(
        body, name="attn_fwd",
        out_shape=[jax.ShapeDtypeStruct((L, D_ATTN), F32), jax.ShapeDtypeStruct((N_HEADS, L, 1), F32)],
        grid=(N_HEADS, L // tq),
        in_specs=[pl.BlockSpec((tq, HEAD_DIM), lambda h, i: (i, h)),
                  pl.BlockSpec((L, HEAD_DIM), lambda h, i: (0, h // _REP)),
                  pl.BlockSpec((L, HEAD_DIM), lambda h, i: (0, h // _REP))],
        out_specs=[pl.BlockSpec((tq, HEAD_DIM), lambda h, i: (i, h)),
                   pl.BlockSpec((None, tq, 1), lambda h, i: (h, i, 0))],
        compiler_params=_params(("parallel", "parallel")),
    )(qr, kr, vb)


def _attn_bwd(qr, kr, vb, do, lse, *, tq=256):
    L = qr.shape[0]
    tq = min(tq, L)
    scale = HEAD_DIM ** -0.5

    def body(q_ref, k_ref, v_ref, do_ref, lse_ref, dq_ref, dk_ref, dv_ref):
        @pl.when((pl.program_id(1) == 0) & (pl.program_id(2) == 0))
        def _():
            dk_ref[...] = jnp.zeros_like(dk_ref)
            dv_ref[...] = jnp.zeros_like(dv_ref)

        q, k, v, do = q_ref[...], k_ref[...], v_ref[...], do_ref[...]
        s = lax.dot_general(q, k, _QK_T, preferred_element_type=F32) * scale
        p = jnp.exp(s - lse_ref[...])
        dv_ref[...] += lax.dot_general(p.astype(BF16), do, _TA, preferred_element_type=F32)
        dp = lax.dot_general(do, v, _QK_T, preferred_element_type=F32)
        delta = jnp.sum(p * dp, axis=-1, keepdims=True)
        ds = (p * (dp - delta) * scale).astype(BF16)
        dq_ref[...] = jnp.dot(ds, k, preferred_element_type=F32)
        dk_ref[...] += lax.dot_general(ds, q, _TA, preferred_element_type=F32)

    head = lambda g, r, i: (i, g * _REP + r)
    return pl.pallas_call(
        body, name="attn_bwd",
        out_shape=[jax.ShapeDtypeStruct((L, D_ATTN), F32), jax.ShapeDtypeStruct((L, D_KV), F32), jax.ShapeDtypeStruct((L, D_KV), F32)],
        grid=(N_KV, _REP, L // tq),
        in_specs=[pl.BlockSpec((tq, HEAD_DIM), head),
                  pl.BlockSpec((L, HEAD_DIM), lambda g, r, i: (0, g)),
                  pl.BlockSpec((L, HEAD_DIM), lambda g, r, i: (0, g)),
                  pl.BlockSpec((tq, HEAD_DIM), head),
                  pl.BlockSpec((None, tq, 1), lambda g, r, i: (g * _REP + r, i, 0))],
        out_specs=[pl.BlockSpec((tq, HEAD_DIM), head),
                   pl.BlockSpec((L, HEAD_DIM), lambda g, r, i: (0, g)),
                   pl.BlockSpec((L, HEAD_DIM), lambda g, r, i: (0, g))],
        compiler_params=_params(("parallel", "arbitrary", "arbitrary")),
    )(qr, kr, vb, do, lse)


SSM_BLK = 8
SSM_NB = SSM_G // SSM_BLK
SSM_BW = SSM_BLK * SSM_H
SSM_SW = SSM_BLK * 2 * SSM_P
SSM_SEG = 8


def _seg_perm(a):
    L, C = a.shape
    return a.reshape(SSM_SEG, L // SSM_SEG, C).transpose(1, 0, 2).reshape(L, C)


def _seg_unperm(a):
    L, C = a.shape
    return a.reshape(L // SSM_SEG, SSM_SEG, C).transpose(1, 0, 2).reshape(L, C)


def _cplx_pow2(a, b, n):
    for _ in range(int(math.log2(n))):
        a, b = a * a - b * b, 2.0 * a * b
    return a, b


def _seg_scan(ref, a, b, T, reverse):
    npair = len(a)
    zero = jnp.zeros((SSM_SEG, 128), F32)

    def make_step(store):
        def step(t, carry):
            lt = (T - 1 - t) if reverse else t
            row = pl.multiple_of(lt * SSM_SEG, SSM_SEG)
            blk = ref[pl.ds(row, SSM_SEG), :]
            new = []
            for q in range(npair):
                re, im = carry[2 * q], carry[2 * q + 1]
                nre = a[q] * re - b[q] * im + blk[:, q * 256:q * 256 + 128]
                nim = a[q] * im + b[q] * re + blk[:, q * 256 + 128:q * 256 + 256]
                new += [nre, nim]
            if store:
                ref[pl.ds(row, SSM_SEG), :] = jnp.concatenate(new, axis=1)
            return tuple(new)
        return step

    ends = lax.fori_loop(0, T, make_step(False), (zero,) * (2 * npair))
    sub = lax.broadcasted_iota(jnp.int32, (SSM_SEG, 128), 0)
    keep = (sub != SSM_SEG - 1) if reverse else (sub != 0)
    shift = (SSM_SEG - 1) if reverse else 1
    init = []
    for q in range(npair):
        pa, pb = _cplx_pow2(a[q], b[q], T)
        xr, xi = zero, zero
        for _ in range(SSM_SEG - 1):
            fr = ends[2 * q] + pa * xr - pb * xi
            fi = ends[2 * q + 1] + pa * xi + pb * xr
            xr = jnp.where(keep, pltpu.roll(fr, shift, 0), 0.0)
            xi = jnp.where(keep, pltpu.roll(fi, shift, 0), 0.0)
        init += [xr, xi]
    lax.fori_loop(0, T, make_step(True), tuple(init))
    return init


def _lam_rows(are_ref, aim_ref, d, jb, npair):
    a, b = [], []
    for q in range(npair):
        j = jb * npair + q
        a.append(jnp.broadcast_to(are_ref[d, j:j + 1, :], (SSM_SEG, 128)))
        b.append(jnp.broadcast_to(aim_ref[d, j:j + 1, :], (SSM_SEG, 128)))
    return a, b


def _ssm_fwd(u_p, wb, wc, are, aim, dvec, cw=SSM_SW):
    L = u_p.shape[0]
    T = L // SSM_SEG
    RC = min(512, L)

    def body(u_ref, wb_ref, wc_ref, are_ref, aim_ref, d_ref, y_ref, x_scr):
        y_ref[...] = u_ref[...] * d_ref[...]
        for d in range(2):
            for jb in range(SSM_SW // cw):
                cols = slice(jb * cw, (jb + 1) * cw)

                def bu_chunk(c, _):
                    rows = pl.ds(pl.multiple_of(c * RC, RC), RC)
                    x_scr[rows, :] = jnp.dot(u_ref[rows, :].astype(BF16), wb_ref[d, :, cols], preferred_element_type=F32)
                    return 0
                lax.fori_loop(0, L // RC, bu_chunk, 0)
                a, b = _lam_rows(are_ref, aim_ref, d, jb, cw // 256)
                _seg_scan(x_scr, a, b, T, reverse=(d == 1))

                def y_chunk(c, _):
                    rows = pl.ds(pl.multiple_of(c * RC, RC), RC)
                    y_ref[rows, :] += jnp.dot(x_scr[rows, :].astype(BF16), wc_ref[d, cols, :], preferred_element_type=F32)
                    return 0
                lax.fori_loop(0, L // RC, y_chunk, 0)

    blk4 = lambda g: (g, 0, 0, 0)
    return pl.pallas_call(
        body, name="ssm_fwd",
        out_shape=jax.ShapeDtypeStruct((L, D_SSM), F32),
        grid=(SSM_NB,),
        in_specs=[pl.BlockSpec((L, SSM_BW), lambda g: (0, g)),
                  pl.BlockSpec((None, 2, SSM_BW, SSM_SW), blk4),
                  pl.BlockSpec((None, 2, SSM_SW, SSM_BW), blk4),
                  pl.BlockSpec((None, 2, 4, 128), blk4),
                  pl.BlockSpec((None, 2, 4, 128), blk4),
                  pl.BlockSpec((1, SSM_BW), lambda g: (0, g))],
        out_specs=pl.BlockSpec((L, SSM_BW), lambda g: (0, g)),
        scratch_shapes=[pltpu.VMEM((L, cw), F32)],
        compiler_params=_params(("parallel",)),
    )(u_p, wb, wc, are, aim, dvec)


def _ssm_bwd(u_p, dy_p, wb, wc, are, aim, dvec, cw=SSM_SW):
    L = u_p.shape[0]
    T = L // SSM_SEG
    RC = min(512, L)
    npair = cw // 256

    def lam_acc(acc, sb, xb):
        new = []
        for q in range(npair):
            sr, si = sb[:, q * 256:q * 256 + 128], sb[:, q * 256 + 128:q * 256 + 256]
            xr, xi = xb[:, q * 256:q * 256 + 128], xb[:, q * 256 + 128:q * 256 + 256]
            new += [acc[2 * q] + sr * xr + si * xi, acc[2 * q + 1] + si * xr - sr * xi]
        return tuple(new)

    def body(u_ref, dy_ref, wb_ref, wc_ref, are_ref, aim_ref, d_ref,
             du_ref, dwb_ref, dwc_ref, dare_ref, daim_ref, dd_ref, x_scr, s_scr):
        du_ref[...] = dy_ref[...] * d_ref[...]
        dd_ref[...] = _colsum(dy_ref[...] * u_ref[...])
        dwb_ref[...] = jnp.zeros_like(dwb_ref)
        dwc_ref[...] = jnp.zeros_like(dwc_ref)
        for d in range(2):
            rev = d == 1
            for jb in range(SSM_SW // cw):
                cols = slice(jb * cw, (jb + 1) * cw)

                def in_chunk(c, _):
                    rows = pl.ds(pl.multiple_of(c * RC, RC), RC)
                    x_scr[rows, :] = jnp.dot(u_ref[rows, :].astype(BF16), wb_ref[d, :, cols], preferred_element_type=F32)
                    s_scr[rows, :] = lax.dot_general(dy_ref[rows, :].astype(BF16), wc_ref[d, cols, :], _QK_T,
                                                     preferred_element_type=F32)
                    return 0
                lax.fori_loop(0, L // RC, in_chunk, 0)
                a, b = _lam_rows(are_ref, aim_ref, d, jb, npair)
                x_in = _seg_scan(x_scr, a, b, T, reverse=rev)
                _seg_scan(s_scr, a, [-v for v in b], T, reverse=not rev)

                def lam_step(t, acc):
                    lt = (T - 2 - t) if rev else (t + 1)
                    srow = pl.multiple_of(lt * SSM_SEG, SSM_SEG)
                    xrow = pl.multiple_of((lt + 1 if rev else lt - 1) * SSM_SEG, SSM_SEG)
                    return lam_acc(acc, s_scr[pl.ds(srow, SSM_SEG), :], x_scr[pl.ds(xrow, SSM_SEG), :])

                edge = pl.ds(((T - 1) if rev else 0) * SSM_SEG, SSM_SEG)
                acc0 = lam_acc((jnp.zeros((SSM_SEG, 128), F32),) * (2 * npair), s_scr[edge, :], jnp.concatenate(x_in, axis=1))
                acc = lax.fori_loop(0, T - 1, lam_step, acc0)
                for q in range(npair):
                    j = jb * npair + q
                    dare_ref[d, j:j + 1, :] = _colsum(acc[2 * q])
                    daim_ref[d, j:j + 1, :] = _colsum(acc[2 * q + 1])

                def out_chunk(c, _):
                    rows = pl.ds(pl.multiple_of(c * RC, RC), RC)
                    xs, ss = x_scr[rows, :].astype(BF16), s_scr[rows, :].astype(BF16)
                    uu, dd = u_ref[rows, :].astype(BF16), dy_ref[rows, :].astype(BF16)
                    dwc_ref[d, cols, :] += lax.dot_general(xs, dd, _TA, preferred_element_type=F32)
                    dwb_ref[d, :, cols] += lax.dot_general(uu, ss, _TA, preferred_element_type=F32)
                    du_ref[rows, :] += lax.dot_general(ss, wb_ref[d, :, cols], _QK_T, preferred_element_type=F32)
                    return 0
                lax.fori_loop(0, L // RC, out_chunk, 0)

    blk4 = lambda g: (g, 0, 0, 0)
    chan = pl.BlockSpec((L, SSM_BW), lambda g: (0, g))
    par_specs = [pl.BlockSpec((None, 2, SSM_BW, SSM_SW), blk4),
                 pl.BlockSpec((None, 2, SSM_SW, SSM_BW), blk4),
                 pl.BlockSpec((None, 2, 4, 128), blk4),
                 pl.BlockSpec((None, 2, 4, 128), blk4),
                 pl.BlockSpec((1, SSM_BW), lambda g: (0, g))]
    return pl.pallas_call(
        body, name="ssm_bwd",
        out_shape=[jax.ShapeDtypeStruct((L, D_SSM), F32),
                   jax.ShapeDtypeStruct((SSM_NB, 2, SSM_BW, SSM_SW), F32),
                   jax.ShapeDtypeStruct((SSM_NB, 2, SSM_SW, SSM_BW), F32),
                   jax.ShapeDtypeStruct((SSM_NB, 2, 4, 128), F32),
                   jax.ShapeDtypeStruct((SSM_NB, 2, 4, 128), F32),
                   jax.ShapeDtypeStruct((1, D_SSM), F32)],
        grid=(SSM_NB,),
        in_specs=[chan, chan] + par_specs,
        out_specs=[chan] + par_specs,
        scratch_shapes=[pltpu.VMEM((L, cw), F32), pltpu.VMEM((L, cw), F32)],
        compiler_params=_params(("parallel",)),
    )(u_p, dy_p, wb, wc, are, aim, dvec)


def _ssm_disc(a_re, a_im, log_dt, b_re, b_im):
    lam = lax.complex(jnp.minimum(a_re, -1e-4), a_im)
    dt = jnp.exp(log_dt)[..., None]
    lam_bar = jnp.exp(lam * dt)
    b_bar = ((lam_bar - 1.0) / lam)[..., None] * lax.complex(b_re, b_im)
    return jnp.real(lam_bar), jnp.imag(lam_bar), jnp.real(b_bar), jnp.imag(b_bar)


_EYE8 = np.eye(SSM_BLK, dtype=np.float32)


def _to_wb(bb_re, bb_im):
    bb = jnp.stack([bb_re, bb_im], axis=2).reshape(2, SSM_NB, SSM_BLK, 2, SSM_P, SSM_H)
    t = bb.transpose(1, 0, 5, 2, 3, 4)[:, :, None] * _EYE8[None, None, :, None, :, None, None]
    t = t.reshape(SSM_NB, 2, SSM_BLK, SSM_H, 4, 2, 2, SSM_P).transpose(0, 1, 2, 3, 4, 6, 5, 7)
    return t.reshape(SSM_NB, 2, SSM_BW, SSM_SW)


def _from_wb(dwb):
    t = dwb.reshape(SSM_NB, 2, SSM_BLK, SSM_H, 4, 2, 2, SSM_P).transpose(0, 1, 2, 3, 4, 6, 5, 7)
    t = t.reshape(SSM_NB, 2, SSM_BLK, SSM_H, SSM_BLK, 2, SSM_P)
    bb = jnp.sum(t * _EYE8[None, None, :, None, :, None, None], axis=2)
    bb = bb.transpose(1, 0, 3, 4, 5, 2).reshape(2, SSM_G, 2, SSM_P, SSM_H)
    return bb[:, :, 0], bb[:, :, 1]


def _to_wc(c_re, c_im):
    cc = jnp.stack([c_re, -c_im], axis=2).reshape(2, SSM_NB, SSM_BLK, 2, SSM_H, SSM_P)
    t = cc.transpose(1, 0, 2, 3, 5, 4)[:, :, :, :, :, None] * _EYE8.T[None, None, :, None, None, :, None]
    t = t.reshape(SSM_NB, 2, 4, 2, 2, SSM_P, SSM_BLK, SSM_H).transpose(0, 1, 2, 4, 3, 5, 6, 7)
    return t.reshape(SSM_NB, 2, SSM_SW, SSM_BW)


def _from_wc(dwc):
    t = dwc.reshape(SSM_NB, 2, 4, 2, 2, SSM_P, SSM_BLK, SSM_H).transpose(0, 1, 2, 4, 3, 5, 6, 7)
    t = t.reshape(SSM_NB, 2, SSM_BLK, 2, SSM_P, SSM_BLK, SSM_H)
    cc = jnp.sum(t * _EYE8.T[None, None, :, None, None, :, None], axis=5)
    cc = cc.transpose(1, 0, 2, 3, 5, 4).reshape(2, SSM_G, 2, SSM_H, SSM_P)
    return cc[:, :, 0], -cc[:, :, 1]


def _to_lam(v):
    return v.reshape(2, SSM_NB, 4, 128).transpose(1, 0, 2, 3)


def _from_lam(v):
    return v.transpose(1, 0, 2, 3).reshape(2, SSM_G, SSM_P)


_MESH = pl.DeviceIdType.MESH
_ANY = pl.BlockSpec(memory_space=pl.ANY)
_BIG = (("w_in", (D_MODEL, D_IN), 1, D_IN // N_CHIPS),
        ("w_glu", (D_SSM, 2 * D_SSM), 1, 2 * D_SSM // N_CHIPS),
        ("w_out", (D_ATTN + D_SSM, D_MODEL), 0, (D_ATTN + D_SSM) // N_CHIPS),
        ("w_ple_gate", (D_MODEL, D_MODEL), 0, D_MODEL // N_CHIPS),
        ("w_ple_proj", (PLE_DIM, D_MODEL), 1, D_MODEL // N_CHIPS))


def _place():
    x, y, c = lax.axis_index("x"), lax.axis_index("y"), lax.axis_index("c")
    return x, y, c, [(1 - x, y), (x, 1 - y), (1 - x, 1 - y)]


def _gather_weights(shards):
    nt = len(_BIG)

    def body(*refs):
        srcs, dsts = refs[:nt], refs[nt:2 * nt]
        send_sems, recv_sems, fwd_send_sems, fwd_recv_sems, loc_sems = refs[2 * nt:]
        x, y, c, chips = _place()
        k = 2 * x + y

        def shard_of(t, kk):
            _, _, axis, sz = _BIG[t]
            sl = pl.ds(pl.multiple_of(kk * sz, sz), sz)
            return dsts[t].at[:, sl] if axis == 1 else dsts[t].at[sl, :]

        def half_of(ref, t, cc):
            n = ref.shape[0] // 2
            return ref.at[pl.ds(pl.multiple_of(cc * n, n), n), :]

        def ici(j, t, kk):
            px, py = chips[j]
            return pltpu.make_async_remote_copy(src_ref=half_of(srcs[t], t, c), dst_ref=half_of(shard_of(t, kk), t, c),
                                                send_sem=send_sems.at[j, t], recv_sem=recv_sems.at[j, t],
                                                device_id=(px, py, c), device_id_type=_MESH)

        def forward(j, t, kk, cc):
            part = half_of(shard_of(t, kk), t, cc)
            return pltpu.make_async_remote_copy(src_ref=part, dst_ref=part, send_sem=fwd_send_sems.at[j, t],
                                                recv_sem=fwd_recv_sems.at[j, t], device_id=(x, y, 1 - c), device_id_type=_MESH)

        local = [pltpu.make_async_copy(srcs[t], shard_of(t, k), loc_sems.at[t]) for t in range(nt)]
        sends = [ici(j, t, k) for t in range(nt) for j in range(3)]
        for cp in local + sends:
            cp.start()
        forwards = []
        for t in range(nt):
            for j, (px, py) in enumerate(chips):
                ici(j, t, 2 * px + py).wait_recv()
                forwards.append(forward(j, t, 2 * px + py, c))
                forwards[-1].start()
        for t in range(nt):
            for j, (px, py) in enumerate(chips):
                forward(j, t, 2 * px + py, 1 - c).wait_recv()
        for cp in sends + forwards:
            cp.wait_send()
        for cp in local:
            cp.wait()

    sems = pltpu.SemaphoreType.DMA((3, nt))
    return pl.pallas_call(
        body, name="gather_weights",
        out_shape=[jax.ShapeDtypeStruct(shape, BF16) for _, shape, _, _ in _BIG],
        in_specs=[_ANY] * nt, out_specs=[_ANY] * nt,
        scratch_shapes=[sems, sems, sems, sems, pltpu.SemaphoreType.DMA((nt,))],
    )(*shards)


SMALL_W = 1024
SMALL_ROWS = 72
N_SMALL = 8 * SMALL_ROWS * SMALL_W
_RED = tuple((shape, ax, (shape[0] // 2, sz) if ax == 1 else (sz // 2, shape[1]), BF16) for _, shape, ax, sz in _BIG) + \
    (((8 * SMALL_ROWS, SMALL_W), 0, (SMALL_ROWS, SMALL_W), F32),)
_RED_TR = 128


def _piece(ref, t, kk, cc):
    _, ax, (pr, pc), _ = _RED[t]
    if ax == 1:
        return ref.at[pl.ds(pl.multiple_of(cc * pr, pr), pr), pl.ds(pl.multiple_of(kk * pc, pc), pc)]
    return ref.at[pl.ds(pl.multiple_of((2 * kk + cc) * pr, pr), pr), :]


def _half_shape(t):
    shape, ax, (pr, pc), _ = _RED[t]
    return (pr, shape[1]) if ax == 1 else (N_CHIPS * pr, pc)


def _piece_in_half(ref, t, kk):
    _, ax, (pr, pc), _ = _RED[t]
    return ref.at[:, pl.ds(pl.multiple_of(kk * pc, pc), pc)] if ax == 1 else ref.at[pl.ds(pl.multiple_of(kk * pr, pr), pr), :]


def _grad_sibling_exchange(grads):
    n = len(_RED)
    n_dma = sum(1 if ax == 1 else N_CHIPS for _, ax, _, _ in _RED)

    def body(*refs):
        srcs, dsts, (send_sems, recv_sems) = refs[:n], refs[n:2 * n], refs[2 * n:]
        x, y, c, _ = _place()
        pairs = []
        for t, (_, ax, (pr, _), _) in enumerate(_RED):
            if ax == 1:
                pairs.append((srcs[t].at[pl.ds(pl.multiple_of((1 - c) * pr, pr), pr), :], dsts[t]))
            else:
                pairs += [(_piece(srcs[t], t, kk, 1 - c), _piece_in_half(dsts[t], t, kk)) for kk in range(N_CHIPS)]
        cps = [pltpu.make_async_remote_copy(src_ref=s, dst_ref=d, send_sem=send_sems.at[i], recv_sem=recv_sems.at[i],
                                            device_id=(x, y, 1 - c), device_id_type=_MESH) for i, (s, d) in enumerate(pairs)]
        for cp in cps:
            cp.start()
        for cp in cps:
            cp.wait()

    return pl.pallas_call(
        body, name="grad_sibling_exchange",
        out_shape=[jax.ShapeDtypeStruct(_half_shape(t), F32) for t in range(n)],
        in_specs=[_ANY] * n, out_specs=[_ANY] * n,
        scratch_shapes=[pltpu.SemaphoreType.DMA((n_dma,)), pltpu.SemaphoreType.DMA((n_dma,))],
    )(*grads)


def _chip_sum(t, g, rs, place):
    shape, ax, (pr, pc), dt = _RED[t]
    W = shape[1]
    tr = min(pr, _RED_TR)
    nb = pr // tr

    def body(place_ref, g_ref, rs_ref, o_ref):
        o_ref[...] = (g_ref[...] + rs_ref[...]).astype(o_ref.dtype)

    return pl.pallas_call(
        body, name="grad_chip_sum_%d" % t,
        out_shape=jax.ShapeDtypeStruct(rs.shape, dt),
        grid_spec=pltpu.PrefetchScalarGridSpec(
            num_scalar_prefetch=1, grid=(1 if ax == 1 else N_CHIPS, nb),
            in_specs=[pl.BlockSpec((tr, W), lambda kk, i, pr_: ((2 * kk + pr_[0]) * nb + i, 0)),
                      pl.BlockSpec((tr, W), lambda kk, i, pr_: (kk * nb + i, 0))],
            out_specs=pl.BlockSpec((tr, W), lambda kk, i, pr_: (kk * nb + i, 0))),
        compiler_params=_params(("parallel", "parallel")),
    )(place, g, rs)


def _grad_chip_exchange(sums):
    n = len(_RED)

    def body(*refs):
        srcs, dsts, (send_sems, recv_sems) = refs[:n], refs[n:2 * n], refs[2 * n:]
        x, y, c, chips = _place()
        cps = [pltpu.make_async_remote_copy(src_ref=_piece_in_half(srcs[t], t, 2 * px + py), dst_ref=dsts[t].at[j],
                                            send_sem=send_sems.at[j, t], recv_sem=recv_sems.at[j, t],
                                            device_id=(px, py, c), device_id_type=_MESH)
               for t in range(n) for j, (px, py) in enumerate(chips)]
        for cp in cps:
            cp.start()
        for cp in cps:
            cp.wait()

    return pl.pallas_call(
        body, name="grad_chip_exchange",
        out_shape=[jax.ShapeDtypeStruct((3,) + piece, dt) for _, _, piece, dt in _RED],
        in_specs=[_ANY] * n, out_specs=[_ANY] * n,
        scratch_shapes=[pltpu.SemaphoreType.DMA((3, n)), pltpu.SemaphoreType.DMA((3, n))],
    )(*sums)


def _total_sum(t, g, rs, rc, place):
    shape, ax, (pr, pc), _ = _RED[t]
    tr = min(pr, _RED_TR)
    nb = pr // tr

    def body(place_ref, g_ref, rs_ref, rc_ref, o_ref):
        o_ref[...] = (g_ref[...] + rs_ref[...]) + rc_ref[0].astype(F32) + rc_ref[1].astype(F32) + rc_ref[2].astype(F32)

    if ax == 1:
        g_map = lambda i, pr_: (pr_[0] * nb + i, pr_[1])
        rs_map = lambda i, pr_: (i, pr_[1])
    else:
        g_map = lambda i, pr_: ((2 * pr_[1] + pr_[0]) * nb + i, 0)
        rs_map = lambda i, pr_: (pr_[1] * nb + i, 0)
    return pl.pallas_call(
        body, name="grad_total_sum_%d" % t,
        out_shape=jax.ShapeDtypeStruct((pr, pc), F32),
        grid_spec=pltpu.PrefetchScalarGridSpec(
            num_scalar_prefetch=1, grid=(nb,),
            in_specs=[pl.BlockSpec((tr, pc), g_map), pl.BlockSpec((tr, pc), rs_map),
                      pl.BlockSpec((3, tr, pc), lambda i, pr_: (0, i, 0))],
            out_specs=pl.BlockSpec((tr, pc), lambda i, pr_: (i, 0))),
        compiler_params=_params(("parallel",)),
    )(place, g, rs, rc)


def _grad_final_exchange(totals):
    n = len(_RED)
    nb = n - 1

    def body(*refs):
        srcs, dsts, (send_sems, recv_sems, loc_sems) = refs[:n], refs[n:2 * n], refs[2 * n:]
        x, y, c, chips = _place()
        me = 4 * x + 2 * y + c
        others = [(x, y, 1 - c)] + [(px, py, cc) for (px, py) in chips for cc in (c, 1 - c)]

        def half(t, cc):
            pr = _RED[t][2][0]
            return dsts[t].at[pl.ds(pl.multiple_of(cc * pr, pr), pr), :]

        def eighth(dev):
            return dsts[nb].at[pl.ds(pl.multiple_of(dev * SMALL_ROWS, SMALL_ROWS), SMALL_ROWS), :]

        def big_copy(t, cc):
            return pltpu.make_async_remote_copy(src_ref=srcs[t], dst_ref=half(t, cc), send_sem=send_sems.at[t],
                                                recv_sem=recv_sems.at[t], device_id=others[0], device_id_type=_MESH)

        def small_copy(i, dev):
            return pltpu.make_async_remote_copy(src_ref=srcs[nb], dst_ref=eighth(dev), send_sem=send_sems.at[nb + i],
                                                recv_sem=recv_sems.at[nb + i], device_id=others[i], device_id_type=_MESH)

        local = [pltpu.make_async_copy(srcs[t], half(t, c), loc_sems.at[t]) for t in range(nb)]
        local.append(pltpu.make_async_copy(srcs[nb], eighth(me), loc_sems.at[nb]))
        sends = [big_copy(t, c) for t in range(nb)] + [small_copy(i, me) for i in range(7)]
        for cp in local + sends:
            cp.start()
        for t in range(nb):
            big_copy(t, 1 - c).wait_recv()
        for i, (px, py, pc) in enumerate(others):
            small_copy(i, 4 * px + 2 * py + pc).wait_recv()
        for cp in sends:
            cp.wait_send()
        for cp in local:
            cp.wait()

    return pl.pallas_call(
        body, name="grad_final_exchange",
        out_shape=[jax.ShapeDtypeStruct((2 * pr, pc), F32) for _, _, (pr, pc), _ in _RED[:nb]] +
                  [jax.ShapeDtypeStruct((8 * SMALL_ROWS, SMALL_W), F32)],
        in_specs=[_ANY] * n, out_specs=[_ANY] * n,
        scratch_shapes=[pltpu.SemaphoreType.DMA((nb + 7,)), pltpu.SemaphoreType.DMA((nb + 7,)), pltpu.SemaphoreType.DMA((n,))],
    )(*totals)


def _reduce_gradients(big_grads, small_flat):
    c = lax.axis_index("c")
    k = 2 * lax.axis_index("x") + lax.axis_index("y")
    place = jnp.stack([c, k]).astype(jnp.int32)
    grads = list(big_grads) + [small_flat.reshape(8 * SMALL_ROWS, SMALL_W)]
    n = len(grads)
    from_sibling = _grad_sibling_exchange(grads)
    chip_sums = [_chip_sum(t, grads[t], from_sibling[t], place) for t in range(n)]
    from_chips = _grad_chip_exchange(chip_sums)
    totals = [_total_sum(t, grads[t], from_sibling[t], from_chips[t], place) for t in range(n)]
    *shards, small = _grad_final_exchange(totals)
    return shards, small.reshape(-1)


def _adamw(w, g, m, v, name):
    def fn(w, g, m, v):
        m = ADAM_B1 * m + (1.0 - ADAM_B1) * g
        v = ADAM_B2 * v + (1.0 - ADAM_B2) * (g * g)
        m_hat = m / (1.0 - ADAM_B1 ** ADAM_STEP)
        v_hat = v / (1.0 - ADAM_B2 ** ADAM_STEP)
        return -ADAM_LR * (m_hat / (jnp.sqrt(v_hat) + ADAM_EPS) + ADAM_WD * w), m, v
    W = w.shape[1]
    return _rowwise(fn, [(a, 0, W) for a in (w, g, m, v)], [], [(W, F32)] * 3, tr=128, name=name)


def _chunks(arr, off, width, w=512):
    return [(arr, off + i * w, w) for i in range(width // w)]


def _cat(vs):
    return jnp.concatenate(vs, axis=1)


def _forward_backward(x, p_b, tgt, full, small):
    L = x.shape[0]
    w_in, w_glu, w_out, w_pg, w_pp = full
    row = lambda v: v.reshape(1, -1)
    g_mix, g_ple, g_fin = row(small["norm_mix"]), row(small["norm_ple"]), row(small["norm_final"])
    gq, gk, b_glu = row(small["q_norm"]), row(small["k_norm"]), row(small["b_glu"])
    cos, sin = _rope_tables(L)

    hn_b, = _rowwise(lambda x, g: x * _rms(x) * g, [(x, 0, D_MODEL)], [g_mix], [(D_MODEL, BF16)], name="norm_mix")
    z = _matmul(hn_b, w_in, name="mm_in")
    qr, kr, vb = _attn_prep(z, gq, gk, cos, sin)
    o, lse = _attn_fwd(qr, kr, vb)

    ssm_names = ("ssm_a_re", "ssm_a_im", "ssm_log_dt", "ssm_b_re", "ssm_b_im")
    (lre, lim, bre, bim), disc_vjp = jax.vjp(_ssm_disc, *[small[n][0] for n in ssm_names])
    wb, wc = _to_wb(bre, bim).astype(BF16), _to_wc(small["ssm_c_re"][0], small["ssm_c_im"][0]).astype(BF16)
    are, aim, dvec = _to_lam(lre), _to_lam(lim), row(small["ssm_d"])
    u_p = _seg_perm(z[:, Z_U:Z_U + D_SSM])
    y_s = _seg_unperm(_ssm_fwd(u_p, wb, wc, are, aim, dvec))
    ge_b, = _rowwise(_gelu, [(y_s, 0, D_SSM)], [], [(D_SSM, BF16)], name="gelu")
    glu = _matmul(ge_b, w_glu, name="mm_glu")

    def merge(ga0, ga1, a, b, gs0, gs1, o, bias):
        sa, _ = _silu_and_grad(_cat([ga0, ga1]))
        ss, _ = _silu_and_grad(_cat([gs0, gs1]))
        y2 = (a + bias[:, :D_SSM]) * _sig(b + bias[:, D_SSM:])
        return _cat([o * sa, y2 * ss])
    merge_rows = _chunks(z, Z_GA, D_ATTN) + [(glu, 0, D_SSM), (glu, D_SSM, D_SSM)] + _chunks(z, Z_GS, D_SSM) + [(o, 0, D_ATTN)]
    cat_b, = _rowwise(merge, merge_rows, [b_glu], [(D_MODEL, BF16)], name="merge")
    t_out = _matmul(cat_b, w_out, name="mm_out")

    def resid(x, t, g):
        h1 = x + t
        return h1, h1 * _rms(h1) * g
    h1, hp_b = _rowwise(resid, [(x, 0, D_MODEL), (t_out, 0, D_MODEL)], [g_ple], [(D_MODEL, F32), (D_MODEL, BF16)], name="resid_norm")
    gl = _matmul(hp_b, w_pg, name="mm_ple_gate")
    pp = _matmul(p_b, w_pp, name="mm_ple_proj")

    def head(h1, gl, pp, tgt, g):
        gate = _sig(gl)
        h2 = h1 + gate * pp
        r = _rms(h2)
        n = h2 * r
        err = n * g - tgt
        dy = err * (1.0 / D_MODEL)
        dn = dy * g
        dh2 = r * (dn - n * jnp.mean(dn * n, axis=-1, keepdims=True))
        dgate = dh2 * pp
        return dh2, dh2 * gate, dgate * gate * (1.0 - gate), _colsum(dy * n), _colsum(0.5 * err * err * (1.0 / D_MODEL))
    dh2, dpp_b, dgl_b, dg_fin, loss_cols = _rowwise(
        head, [(a, 0, D_MODEL) for a in (h1, gl, pp, tgt)], [g_fin],
        [(D_MODEL, F32), (D_MODEL, BF16), (D_MODEL, BF16)], [(1, D_MODEL), (1, D_MODEL)], name="loss_head")

    dw_pp = _matmul(p_b, dpp_b, ta=True, name="mm_d_w_ple_proj")
    dw_pg = _matmul(hp_b, dgl_b, ta=True, name="mm_d_w_ple_gate")
    dhp = _matmul(dgl_b, w_pg, tb=True, name="mm_d_hp")

    def resid_bwd(dhp, h1, dh2, g):
        dx, dg = _rms_bwd(h1, g, dhp)
        dh1 = dh2 + dx
        return dh1, dh1, _colsum(dg)
    dh1, dh1_b, dg_ple = _rowwise(resid_bwd, [(a, 0, D_MODEL) for a in (dhp, h1, dh2)], [g_ple],
                                  [(D_MODEL, F32), (D_MODEL, BF16)], [(1, D_MODEL)], name="resid_norm_bwd")
    dw_out = _matmul(cat_b, dh1_b, ta=True, name="mm_d_w_out")
    dcat = _matmul(dh1_b, w_out, tb=True, name="mm_d_cat")

    def merge_bwd(dya, dys, ga0, ga1, a, b, gs0, gs1, o, bias):
        ga, gs = _cat([ga0, ga1]), _cat([gs0, gs1])
        sa, dsa = _silu_and_grad(ga)
        ss, dss = _silu_and_grad(gs)
        a, sb = a + bias[:, :D_SSM], _sig(b + bias[:, D_SSM:])
        dy2 = dys * ss
        dglu = _cat([dy2 * sb, dy2 * a * sb * (1.0 - sb)])
        return dya * sa, dya * o * dsa, dys * (a * sb) * dss, dglu, _colsum(dglu)
    do_b, dga_b, dgs_b, dglu_b, db_glu = _rowwise(
        merge_bwd, [(dcat, 0, D_ATTN), (dcat, D_ATTN, D_SSM)] + merge_rows, [b_glu],
        [(D_ATTN, BF16), (D_ATTN, BF16), (D_SSM, BF16), (2 * D_SSM, BF16)], [(1, 2 * D_SSM)], name="merge_bwd")
    dw_glu = _matmul(ge_b, dglu_b, ta=True, name="mm_d_w_glu")
    dge = _matmul(dglu_b, w_glu, tb=True, name="mm_d_ge")
    dy_s, = _rowwise(lambda dge, y: dge * _gelu_grad(y), [(dge, 0, D_SSM), (y_s, 0, D_SSM)], [], [(D_SSM, F32)], name="gelu_bwd")
    du_p, dwb, dwc, dare, daim, d_ssm_d = _ssm_bwd(u_p, _seg_perm(dy_s), wb, wc, are, aim, dvec)
    db_re_bar, db_im_bar = _from_wb(dwb)
    dc_re, dc_im = _from_wc(dwc)
    da_re, da_im, dlog_dt, db_re, db_im = disc_vjp((_from_lam(dare), _from_lam(daim), db_re_bar, db_im_bar))

    dqr, dkr, dv = _attn_bwd(qr, kr, vb, do_b, lse)
    dq_b, dk_b, dgq, dgk = _attn_prep_bwd(dqr, dkr, z, gq, gk, cos, sin)
    dz_b = _cat([dq_b, dk_b, dv.astype(BF16), dga_b, _seg_unperm(du_p).astype(BF16), dgs_b])
    dw_in = _matmul(hn_b, dz_b, ta=True, name="mm_d_w_in")
    dhn = _matmul(dz_b, w_in, tb=True, name="mm_d_hn")

    def norm_bwd(dhn, x, dh1, g):
        dx, dg = _rms_bwd(x, g, dhn)
        return dh1 + dx, _colsum(dg)
    grad_x, dg_mix = _rowwise(norm_bwd, [(a, 0, D_MODEL) for a in (dhn, x, dh1)], [g_mix], [(D_MODEL, F32)], [(1, D_MODEL)],
                              name="norm_mix_bwd")

    small_grads = {"norm_mix": dg_mix, "q_norm": dgq, "k_norm": dgk, "ssm_a_re": da_re, "ssm_a_im": da_im, "ssm_log_dt": dlog_dt,
                   "ssm_b_re": db_re, "ssm_b_im": db_im, "ssm_c_re": dc_re, "ssm_c_im": dc_im, "ssm_d": d_ssm_d,
                   "b_glu": db_glu, "norm_ple": dg_ple, "norm_final": dg_fin}
    return jnp.sum(loss_cols), grad_x, [dw_in, dw_glu, dw_out, dw_pg, dw_pp], small_grads


_SMALL = ("norm_mix", "q_norm", "k_norm", "ssm_a_re", "ssm_a_im", "ssm_log_dt", "ssm_b_re", "ssm_b_im", "ssm_c_re", "ssm_c_im",
          "ssm_d", "b_glu", "norm_ple", "norm_final")
_WEIGHTS = ("norm_mix", "w_in", "q_norm", "k_norm", "ssm_a_re", "ssm_a_im", "ssm_log_dt", "ssm_b_re", "ssm_b_im", "ssm_c_re",
            "ssm_c_im", "ssm_d", "w_glu", "b_glu", "w_out", "norm_ple", "w_ple_gate", "w_ple_proj", "norm_final")


def _flat_small(d):
    flat = jnp.concatenate([d[n].reshape(-1).astype(F32) for n in _SMALL])
    return jnp.pad(flat, (0, N_SMALL - flat.shape[0]))


def _split_small(flat, like):
    out, off = {}, 0
    for n in _SMALL:
        sz = math.prod(like[n].shape)
        out[n] = flat[off:off + sz].reshape(like[n].shape)
        off += sz
    return out


def kernel(x, p, norm_mix, w_in, q_norm, k_norm, ssm_a_re, ssm_a_im, ssm_log_dt, ssm_b_re, ssm_b_im, ssm_c_re, ssm_c_im, ssm_d, w_glu, b_glu, w_out, norm_ple, w_ple_gate, w_ple_proj, norm_final, loss_target, m_norm_mix, m_w_in, m_q_norm, m_k_norm, m_ssm_a_re, m_ssm_a_im, m_ssm_log_dt, m_ssm_b_re, m_ssm_b_im, m_ssm_c_re, m_ssm_c_im, m_ssm_d, m_w_glu, m_b_glu, m_w_out, m_norm_ple, m_w_ple_gate, m_w_ple_proj, m_norm_final, v_norm_mix, v_w_in, v_q_norm, v_k_norm, v_ssm_a_re, v_ssm_a_im, v_ssm_log_dt, v_ssm_b_re, v_ssm_b_im, v_ssm_c_re, v_ssm_c_im, v_ssm_d, v_w_glu, v_b_glu, v_w_out, v_norm_ple, v_w_ple_gate, v_w_ple_proj, v_norm_final):
    w = dict(norm_mix=norm_mix, w_in=w_in, q_norm=q_norm, k_norm=k_norm, ssm_a_re=ssm_a_re, ssm_a_im=ssm_a_im,
             ssm_log_dt=ssm_log_dt, ssm_b_re=ssm_b_re, ssm_b_im=ssm_b_im, ssm_c_re=ssm_c_re, ssm_c_im=ssm_c_im, ssm_d=ssm_d,
             w_glu=w_glu, b_glu=b_glu, w_out=w_out, norm_ple=norm_ple, w_ple_gate=w_ple_gate, w_ple_proj=w_ple_proj,
             norm_final=norm_final)
    m = dict(norm_mix=m_norm_mix, w_in=m_w_in, q_norm=m_q_norm, k_norm=m_k_norm, ssm_a_re=m_ssm_a_re, ssm_a_im=m_ssm_a_im,
             ssm_log_dt=m_ssm_log_dt, ssm_b_re=m_ssm_b_re, ssm_b_im=m_ssm_b_im, ssm_c_re=m_ssm_c_re, ssm_c_im=m_ssm_c_im,
             ssm_d=m_ssm_d, w_glu=m_w_glu, b_glu=m_b_glu, w_out=m_w_out, norm_ple=m_norm_ple, w_ple_gate=m_w_ple_gate,
             w_ple_proj=m_w_ple_proj, norm_final=m_norm_final)
    v = dict(norm_mix=v_norm_mix, w_in=v_w_in, q_norm=v_q_norm, k_norm=v_k_norm, ssm_a_re=v_ssm_a_re, ssm_a_im=v_ssm_a_im,
             ssm_log_dt=v_ssm_log_dt, ssm_b_re=v_ssm_b_re, ssm_b_im=v_ssm_b_im, ssm_c_re=v_ssm_c_re, ssm_c_im=v_ssm_c_im,
             ssm_d=v_ssm_d, w_glu=v_w_glu, b_glu=v_b_glu, w_out=v_w_out, norm_ple=v_norm_ple, w_ple_gate=v_w_ple_gate,
             w_ple_proj=v_w_ple_proj, norm_final=v_norm_final)
    big_names = [n for n, _, _, _ in _BIG]

    full = _gather_weights([w[n][0].astype(BF16) for n in big_names])
    small = {n: w[n] for n in _SMALL}
    loss_part, grad_x, big_grads, small_grads = _forward_backward(
        x[0], p[0, 0].astype(BF16), loss_target[0], full, small)
    loss = lax.psum(loss_part, ("x", "y", "c"))

    big_red, small_red = _reduce_gradients(big_grads, _flat_small(small_grads))
    grads = _split_small(small_red, w)
    delta, new_m, new_v = {}, {}, {}
    for n, g in zip(big_names, big_red):
        grads[n] = g[None]
        d_, m_, v_ = _adamw(w[n][0], g, m[n][0], v[n][0], "adamw_" + n)
        delta[n], new_m[n], new_v[n] = d_[None], m_[None], v_[None]
    d_, m_, v_ = _adamw(*[a.reshape(-1, SMALL_W) for a in (_flat_small(w), small_red, _flat_small(m), _flat_small(v))], "adamw_small")
    delta.update(_split_small(d_.reshape(-1), w))
    new_m.update(_split_small(m_.reshape(-1), w))
    new_v.update(_split_small(v_.reshape(-1), w))
    return (loss, grad_x[None], *[grads[n] for n in _WEIGHTS], *[delta[n] for n in _WEIGHTS],
            *[new_m[n] for n in _WEIGHTS], *[new_v[n] for n in _WEIGHTS])
```

```python
import functools
import math

import jax
import jax.numpy as jnp
import numpy as np
from jax import lax
from jax.experimental import pallas as pl
from jax.experimental.pallas import tpu as pltpu

D_MODEL = 2048
GRID_W = 64
PLE_DIM = 256
D_ATTN = 1024
N_HEADS = 8
N_KV = 2
HEAD_DIM = 128
ROPE_THETA = 10000.0
D_SSM = 1024
SSM_H = 16
SSM_G = 64
SSM_P = 64
D_KV = N_KV * HEAD_DIM
D_IN = 2 * D_ATTN + 2 * D_KV + 2 * D_SSM
EPS = 1e-6
Z_Q, Z_K, Z_V, Z_GA, Z_U, Z_GS = 0, 1024, 1280, 1536, 2560, 3584

ADAM_LR, ADAM_B1, ADAM_B2, ADAM_EPS, ADAM_WD, ADAM_STEP = 0.001, 0.9, 0.999, 1e-08, 0.01, 10

N_CHIPS = 4
VMEM_LIMIT_V7X = 56 * 1024 * 1024
F32 = jnp.float32
BF16 = jnp.bfloat16


def _params(sem, vmem=VMEM_LIMIT_V7X):
    return pltpu.CompilerParams(dimension_semantics=sem, vmem_limit_bytes=vmem)


def _matmul(a, b, *, ta=False, tb=False, out_dtype=F32, tm=1024, tn=512, name):
    M, K = (a.shape[1], a.shape[0]) if ta else a.shape
    N = b.shape[0] if tb else b.shape[1]
    tm, tn = min(tm, M), min(tn, N)
    assert M % tm == 0 and N % tn == 0, (name, M, N, K)
    dims = (((0 if ta else 1,), (1 if tb else 0,)), ((), ()))

    def body(a_ref, b_ref, o_ref):
        o_ref[...] = lax.dot_general(a_ref[...], b_ref[...], dims, preferred_element_type=F32).astype(o_ref.dtype)

    a_spec = pl.BlockSpec((K, tm), lambda i, j: (0, i)) if ta else pl.BlockSpec((tm, K), lambda i, j: (i, 0))
    b_spec = pl.BlockSpec((tn, K), lambda i, j: (j, 0)) if tb else pl.BlockSpec((K, tn), lambda i, j: (0, j))
    return pl.pallas_call(
        body, name=name,
        out_shape=jax.ShapeDtypeStruct((M, N), out_dtype),
        grid=(M // tm, N // tn),
        in_specs=[a_spec, b_spec],
        out_specs=pl.BlockSpec((tm, tn), lambda i, j: (i, j)),
        compiler_params=_params(("parallel", "parallel")),
    )(a, b)


def _rowwise(fn, rows, consts, outs, accs=(), *, tr=256, name):
    L = rows[0][0].shape[0]
    tr = math.gcd(tr, L)
    assert tr % 8 == 0 or tr == L, (name, L, tr)
    n_in, n_c, n_o, n_a = len(rows), len(consts), len(outs), len(accs)

    def body(*refs):
        ins = [r[...] for r in refs[:n_in + n_c]]
        res = fn(*ins)
        if not isinstance(res, (tuple, list)):
            res = (res,)
        o_refs = refs[n_in + n_c:n_in + n_c + n_o]
        a_refs = refs[n_in + n_c + n_o:]
        for r, v in zip(o_refs, res[:n_o]):
            r[...] = v.astype(r.dtype)
        if n_a:
            first = pl.program_id(0) == 0

            @pl.when(first)
            def _():
                for r, v in zip(a_refs, res[n_o:]):
                    r[...] = v.astype(F32)

            @pl.when(jnp.logical_not(first))
            def _():
                for r, v in zip(a_refs, res[n_o:]):
                    r[...] += v.astype(F32)

    in_specs = []
    for arr, off, w in rows:
        assert off % w == 0, (name, off, w)
        in_specs.append(pl.BlockSpec((tr, w), functools.partial(lambda i, c: (i, c), c=off // w)))
    for c in consts:
        in_specs.append(pl.BlockSpec(c.shape, lambda i: (0, 0)))
    out_shape = [jax.ShapeDtypeStruct((L, w), dt) for w, dt in outs] + [jax.ShapeDtypeStruct(s, F32) for s in accs]
    out_specs = [pl.BlockSpec((tr, w), lambda i: (i, 0)) for w, _ in outs] + [pl.BlockSpec(s, lambda i: (0, 0)) for s in accs]
    res = pl.pallas_call(
        body, name=name,
        out_shape=out_shape,
        grid=(L // tr,),
        in_specs=in_specs,
        out_specs=out_specs,
        compiler_params=_params(("arbitrary",) if n_a else ("parallel",)),
    )(*[r[0] for r in rows], *consts)
    return res


def _sig(x):
    return jax.nn.sigmoid(x)


def _silu_and_grad(x):
    s = _sig(x)
    return x * s, s * (1.0 + x * (1.0 - s))


_GELU_C = math.sqrt(2.0 / math.pi)


def _gelu(x):
    return 0.5 * x * (1.0 + jnp.tanh(_GELU_C * (x + 0.044715 * x * x * x)))


def _gelu_grad(x):
    t = jnp.tanh(_GELU_C * (x + 0.044715 * x * x * x))
    return 0.5 * (1.0 + t) + 0.5 * x * (1.0 - t * t) * _GELU_C * (1.0 + 3.0 * 0.044715 * x * x)


def _rms(x):
    return lax.rsqrt(jnp.mean(x * x, axis=-1, keepdims=True) + EPS)


def _rms_bwd(x, g, dy):
    r = _rms(x)
    n = x * r
    dn = dy * g
    return r * (dn - n * jnp.mean(dn * n, axis=-1, keepdims=True)), dy * n


def _colsum(v):
    return jnp.sum(v, axis=0, keepdims=True)


def _rope_partner(x):
    lane = lax.broadcasted_iota(jnp.int32, x.shape, x.ndim - 1)
    return jnp.where(lane % 64 < 32, pltpu.roll(x, 96, x.ndim - 1), pltpu.roll(x, 32, x.ndim - 1))


def _rope_tables(L):
    rows_n = L // GRID_W
    rows = jnp.repeat(jnp.arange(rows_n), GRID_W).astype(F32)
    cols = jnp.tile(jnp.arange(GRID_W), rows_n).astype(F32)
    n_freq = HEAD_DIM // 4
    inv_freq = ROPE_THETA ** (-jnp.arange(n_freq, dtype=F32) / n_freq)
    ar, ac = rows[:, None] * inv_freq[None, :], cols[:, None] * inv_freq[None, :]
    cos = jnp.concatenate([jnp.cos(ar), jnp.cos(ar), jnp.cos(ac), jnp.cos(ac)], axis=-1)
    sin = jnp.concatenate([-jnp.sin(ar), jnp.sin(ar), -jnp.sin(ac), jnp.sin(ac)], axis=-1)
    return cos, sin


def _heads(v):
    return [v[:, h * HEAD_DIM:(h + 1) * HEAD_DIM] for h in range(v.shape[1] // HEAD_DIM)]


def _attn_prep(z, q_norm, k_norm, cos, sin):
    def fn(q, k, v, cos, sin, gq, gk):
        def one(xh, g):
            xn = xh * _rms(xh) * g
            return xn * cos + _rope_partner(xn) * sin
        qr = jnp.concatenate([one(h, gq) for h in _heads(q)], axis=1)
        kr = jnp.concatenate([one(h, gk) for h in _heads(k)], axis=1)
        return qr, kr, v
    return _rowwise(fn, [(z, Z_Q, D_ATTN), (z, Z_K, D_KV), (z, Z_V, D_KV), (cos, 0, HEAD_DIM), (sin, 0, HEAD_DIM)],
                    [q_norm, k_norm], [(D_ATTN, BF16), (D_KV, BF16), (D_KV, BF16)], name="attn_prep")


def _attn_prep_bwd(dqr, dkr, z, q_norm, k_norm, cos, sin):
    def fn(dqr, dkr, q, k, cos, sin, gq, gk):
        def one(dyh, xh, g):
            dn = dyh * cos + _rope_partner(dyh * sin)
            return _rms_bwd(xh, g, dn)
        rq = [one(a, b, gq) for a, b in zip(_heads(dqr), _heads(q))]
        rk = [one(a, b, gk) for a, b in zip(_heads(dkr), _heads(k))]
        dq = jnp.concatenate([r[0] for r in rq], axis=1)
        dk = jnp.concatenate([r[0] for r in rk], axis=1)
        return dq, dk, _colsum(sum(r[1] for r in rq)), _colsum(sum(r[1] for r in rk))
    return _rowwise(fn, [(dqr, 0, D_ATTN), (dkr, 0, D_KV), (z, Z_Q, D_ATTN), (z, Z_K, D_KV), (cos, 0, HEAD_DIM), (sin, 0, HEAD_DIM)],
                    [q_norm, k_norm], [(D_ATTN, BF16), (D_KV, BF16)], [(1, HEAD_DIM), (1, HEAD_DIM)], name="attn_prep_bwd")


_QK_T = (((1,), (1,)), ((), ()))
_TA = (((0,), (0,)), ((), ()))
_REP = N_HEADS // N_KV


def _attn_fwd(qr, kr, vb, *, tq=256):
    L = qr.shape[0]
    tq = min(tq, L)
    scale = HEAD_DIM ** -0.5

    def body(q_ref, k_ref, v_ref, o_ref, lse_ref):
        s = lax.dot_general(q_ref[...], k_ref[...], _QK_T, preferred_element_type=F32) * scale
        m = jnp.max(s, axis=-1, keepdims=True)
        p = jnp.exp(s - m)
        l = jnp.sum(p, axis=-1, keepdims=True)
        pn = (p * (1.0 / l)).astype(BF16)
        o_ref[...] = jnp.dot(pn, v_ref[...], preferred_element_type=F32)
        lse_ref[...] = m + jnp.log(l)

    return pl.pallas_call(
        body, name="attn_fwd",
        out_shape=[jax.ShapeDtypeStruct((L, D_ATTN), F32), jax.ShapeDtypeStruct((N_HEADS, L, 1), F32)],
        grid=(N_HEADS, L // tq),
        in_specs=[pl.BlockSpec((tq, HEAD_DIM), lambda h, i: (i, h)),
                  pl.BlockSpec((L, HEAD_DIM), lambda h, i: (0, h // _REP)),
                  pl.BlockSpec((L, HEAD_DIM), lambda h, i: (0, h // _REP))],
        out_specs=[pl.BlockSpec((tq, HEAD_DIM), lambda h, i: (i, h)),
                   pl.BlockSpec((None, tq, 1), lambda h, i: (h, i, 0))],
        compiler_params=_params(("parallel", "parallel")),
    )(qr, kr, vb)


def _attn_bwd(qr, kr, vb, do, lse, *, tq=256):
    L = qr.shape[0]
    tq = min(tq, L)
    scale = HEAD_DIM ** -0.5

    def body(q_ref, k_ref, v_ref, do_ref, lse_ref, dq_ref, dk_ref, dv_ref):
        @pl.when((pl.program_id(1) == 0) & (pl.program_id(2) == 0))
        def _():
            dk_ref[...] = jnp.zeros_like(dk_ref)
            dv_ref[...] = jnp.zeros_like(dv_ref)

        q, k, v, do = q_ref[...], k_ref[...], v_ref[...], do_ref[...]
        s = lax.dot_general(q, k, _QK_T, preferred_element_type=F32) * scale
        p = jnp.exp(s - lse_ref[...])
        dv_ref[...] += lax.dot_general(p.astype(BF16), do, _TA, preferred_element_type=F32)
        dp = lax.dot_general(do, v, _QK_T, preferred_element_type=F32)
        delta = jnp.sum(p * dp, axis=-1, keepdims=True)
        ds = (p * (dp - delta) * scale).astype(BF16)
        dq_ref[...] = jnp.dot(ds, k, preferred_element_type=F32)
        dk_ref[...] += lax.dot_general(ds, q, _TA, preferred_element_type=F32)

    head = lambda g, r, i: (i, g * _REP + r)
    return pl.pallas_call(
        body, name="attn_bwd",
        out_shape=[jax.ShapeDtypeStruct((L, D_ATTN), F32), jax.ShapeDtypeStruct((L, D_KV), F32), jax.ShapeDtypeStruct((L, D_KV), F32)],
        grid=(N_KV, _REP, L // tq),
        in_specs=[pl.BlockSpec((tq, HEAD_DIM), head),
                  pl.BlockSpec((L, HEAD_DIM), lambda g, r, i: (0, g)),
                  pl.BlockSpec((L, HEAD_DIM), lambda g, r, i: (0, g)),
                  pl.BlockSpec((tq, HEAD_DIM), head),
                  pl.BlockSpec((None, tq, 1), lambda g, r, i: (g * _REP + r, i, 0))],
        out_specs=[pl.BlockSpec((tq, HEAD_DIM), head),
                   pl.BlockSpec((L, HEAD_DIM), lambda g, r, i: (0, g)),
                   pl.BlockSpec((L, HEAD_DIM), lambda g, r, i: (0, g))],
        compiler_params=_params(("parallel", "arbitrary", "arbitrary")),
    )(qr, kr, vb, do, lse)


SSM_BLK = 8
SSM_NB = SSM_G // SSM_BLK
SSM_BW = SSM_BLK * SSM_H
SSM_SW = SSM_BLK * 2 * SSM_P
SSM_SEG = 8


def _seg_perm(a):
    L, C = a.shape
    return a.reshape(SSM_SEG, L // SSM_SEG, C).transpose(1, 0, 2).reshape(L, C)


def _seg_unperm(a):
    L, C = a.shape
    return a.reshape(L // SSM_SEG, SSM_SEG, C).transpose(1, 0, 2).reshape(L, C)


def _cplx_pow2(a, b, n):
    for _ in range(int(math.log2(n))):
        a, b = a * a - b * b, 2.0 * a * b
    return a, b


def _seg_scan(ref, a, b, T, reverse):
    npair = len(a)
    zero = jnp.zeros((SSM_SEG, 128), F32)

    def make_step(store):
        def step(t, carry):
            lt = (T - 1 - t) if reverse else t
            row = pl.multiple_of(lt * SSM_SEG, SSM_SEG)
            blk = ref[pl.ds(row, SSM_SEG), :]
            new = []
            for q in range(npair):
                re, im = carry[2 * q], carry[2 * q + 1]
                nre = a[q] * re - b[q] * im + blk[:, q * 256:q * 256 + 128]
                nim = a[q] * im + b[q] * re + blk[:, q * 256 + 128:q * 256 + 256]
                new += [nre, nim]
            if store:
                ref[pl.ds(row, SSM_SEG), :] = jnp.concatenate(new, axis=1)
            return tuple(new)
        return step

    ends = lax.fori_loop(0, T, make_step(False), (zero,) * (2 * npair))
    sub = lax.broadcasted_iota(jnp.int32, (SSM_SEG, 128), 0)
    keep = (sub != SSM_SEG - 1) if reverse else (sub != 0)
    shift = (SSM_SEG - 1) if reverse else 1
    init = []
    for q in range(npair):
        pa, pb = _cplx_pow2(a[q], b[q], T)
        xr, xi = zero, zero
        for _ in range(SSM_SEG - 1):
            fr = ends[2 * q] + pa * xr - pb * xi
            fi = ends[2 * q + 1] + pa * xi + pb * xr
            xr = jnp.where(keep, pltpu.roll(fr, shift, 0), 0.0)
            xi = jnp.where(keep, pltpu.roll(fi, shift, 0), 0.0)
        init += [xr, xi]
    lax.fori_loop(0, T, make_step(True), tuple(init))
    return init


def _lam_rows(are_ref, aim_ref, d, jb, npair):
    a, b = [], []
    for q in range(npair):
        j = jb * npair + q
        a.append(jnp.broadcast_to(are_ref[d, j:j + 1, :], (SSM_SEG, 128)))
        b.append(jnp.broadcast_to(aim_ref[d, j:j + 1, :], (SSM_SEG, 128)))
    return a, b


def _ssm_fwd(u_p, wb, wc, are, aim, dvec, cw=SSM_SW):
    L = u_p.shape[0]
    T = L // SSM_SEG
    RC = min(512, L)

    def body(u_ref, wb_ref, wc_ref, are_ref, aim_ref, d_ref, y_ref, x_scr):
        y_ref[...] = u_ref[...] * d_ref[...]
        for d in range(2):
            for jb in range(SSM_SW // cw):
                cols = slice(jb * cw, (jb + 1) * cw)

                def bu_chunk(c, _):
                    rows = pl.ds(pl.multiple_of(c * RC, RC), RC)
                    x_scr[rows, :] = jnp.dot(u_ref[rows, :].astype(BF16), wb_ref[d, :, cols], preferred_element_type=F32)
                    return 0
                lax.fori_loop(0, L // RC, bu_chunk, 0)
                a, b = _lam_rows(are_ref, aim_ref, d, jb, cw // 256)
                _seg_scan(x_scr, a, b, T, reverse=(d == 1))

                def y_chunk(c, _):
                    rows = pl.ds(pl.multiple_of(c * RC, RC), RC)
                    y_ref[rows, :] += jnp.dot(x_scr[rows, :].astype(BF16), wc_ref[d, cols, :], preferred_element_type=F32)
                    return 0
                lax.fori_loop(0, L // RC, y_chunk, 0)

    blk4 = lambda g: (g, 0, 0, 0)
    return pl.pallas_call(
        body, name="ssm_fwd",
        out_shape=jax.ShapeDtypeStruct((L, D_SSM), F32),
        grid=(SSM_NB,),
        in_specs=[pl.BlockSpec((L, SSM_BW), lambda g: (0, g)),
                  pl.BlockSpec((None, 2, SSM_BW, SSM_SW), blk4),
                  pl.BlockSpec((None, 2, SSM_SW, SSM_BW), blk4),
                  pl.BlockSpec((None, 2, 4, 128), blk4),
                  pl.BlockSpec((None, 2, 4, 128), blk4),
                  pl.BlockSpec((1, SSM_BW), lambda g: (0, g))],
        out_specs=pl.BlockSpec((L, SSM_BW), lambda g: (0, g)),
        scratch_shapes=[pltpu.VMEM((L, cw), F32)],
        compiler_params=_params(("parallel",)),
    )(u_p, wb, wc, are, aim, dvec)


def _ssm_bwd(u_p, dy_p, wb, wc, are, aim, dvec, cw=SSM_SW):
    L = u_p.shape[0]
    T = L // SSM_SEG
    RC = min(512, L)
    npair = cw // 256

    def lam_acc(acc, sb, xb):
        new = []
        for q in range(npair):
            sr, si = sb[:, q * 256:q * 256 + 128], sb[:, q * 256 + 128:q * 256 + 256]
            xr, xi = xb[:, q * 256:q * 256 + 128], xb[:, q * 256 + 128:q * 256 + 256]
            new += [acc[2 * q] + sr * xr + si * xi, acc[2 * q + 1] + si * xr - sr * xi]
        return tuple(new)

    def body(u_ref, dy_ref, wb_ref, wc_ref, are_ref, aim_ref, d_ref,
             du_ref, dwb_ref, dwc_ref, dare_ref, daim_ref, dd_ref, x_scr, s_scr):
        du_ref[...] = dy_ref[...] * d_ref[...]
        dd_ref[...] = _colsum(dy_ref[...] * u_ref[...])
        dwb_ref[...] = jnp.zeros_like(dwb_ref)
        dwc_ref[...] = jnp.zeros_like(dwc_ref)
        for d in range(2):
            rev = d == 1
            for jb in range(SSM_SW // cw):
                cols = slice(jb * cw, (jb + 1) * cw)

                def in_chunk(c, _):
                    rows = pl.ds(pl.multiple_of(c * RC, RC), RC)
                    x_scr[rows, :] = jnp.dot(u_ref[rows, :].astype(BF16), wb_ref[d, :, cols], preferred_element_type=F32)
                    s_scr[rows, :] = lax.dot_general(dy_ref[rows, :].astype(BF16), wc_ref[d, cols, :], _QK_T,
                                                     preferred_element_type=F32)
                    return 0
                lax.fori_loop(0, L // RC, in_chunk, 0)
                a, b = _lam_rows(are_ref, aim_ref, d, jb, npair)
                x_in = _seg_scan(x_scr, a, b, T, reverse=rev)
                _seg_scan(s_scr, a, [-v for v in b], T, reverse=not rev)

                def lam_step(t, acc):
                    lt = (T - 2 - t) if rev else (t + 1)
                    srow = pl.multiple_of(lt * SSM_SEG, SSM_SEG)
                    xrow = pl.multiple_of((lt + 1 if rev else lt - 1) * SSM_SEG, SSM_SEG)
                    return lam_acc(acc, s_scr[pl.ds(srow, SSM_SEG), :], x_scr[pl.ds(xrow, SSM_SEG), :])

                edge = pl.ds(((T - 1) if rev else 0) * SSM_SEG, SSM_SEG)
                acc0 = lam_acc((jnp.zeros((SSM_SEG, 128), F32),) * (2 * npair), s_scr[edge, :], jnp.concatenate(x_in, axis=1))
                acc = lax.fori_loop(0, T - 1, lam_step, acc0)
                for q in range(npair):
                    j = jb * npair + q
                    dare_ref[d, j:j + 1, :] = _colsum(acc[2 * q])
                    daim_ref[d, j:j + 1, :] = _colsum(acc[2 * q + 1])

                def out_chunk(c, _):
                    rows = pl.ds(pl.multiple_of(c * RC, RC), RC)
                    xs, ss = x_scr[rows, :].astype(BF16), s_scr[rows, :].astype(BF16)
                    uu, dd = u_ref[rows, :].astype(BF16), dy_ref[rows, :].astype(BF16)
                    dwc_ref[d, cols, :] += lax.dot_general(xs, dd, _TA, preferred_element_type=F32)
                    dwb_ref[d, :, cols] += lax.dot_general(uu, ss, _TA, preferred_element_type=F32)
                    du_ref[rows, :] += lax.dot_general(ss, wb_ref[d, :, cols], _QK_T, preferred_element_type=F32)
                    return 0
                lax.fori_loop(0, L // RC, out_chunk, 0)

    blk4 = lambda g: (g, 0, 0, 0)
    chan = pl.BlockSpec((L, SSM_BW), lambda g: (0, g))
    par_specs = [pl.BlockSpec((None, 2, SSM_BW, SSM_SW), blk4),
                 pl.BlockSpec((None, 2, SSM_SW, SSM_BW), blk4),
                 pl.BlockSpec((None, 2, 4, 128), blk4),
                 pl.BlockSpec((None, 2, 4, 128), blk4),
                 pl.BlockSpec((1, SSM_BW), lambda g: (0, g))]
    return pl.pallas_call(
        body, name="ssm_bwd",
        out_shape=[jax.ShapeDtypeStruct((L, D_SSM), F32),
                   jax.ShapeDtypeStruct((SSM_NB, 2, SSM_BW, SSM_SW), F32),
                   jax.ShapeDtypeStruct((SSM_NB, 2, SSM_SW, SSM_BW), F32),
                   jax.ShapeDtypeStruct((SSM_NB, 2, 4, 128), F32),
                   jax.ShapeDtypeStruct((SSM_NB, 2, 4, 128), F32),
                   jax.ShapeDtypeStruct((1, D_SSM), F32)],
        grid=(SSM_NB,),
        in_specs=[chan, chan] + par_specs,
        out_specs=[chan] + par_specs,
        scratch_shapes=[pltpu.VMEM((L, cw), F32), pltpu.VMEM((L, cw), F32)],
        compiler_params=_params(("parallel",)),
    )(u_p, dy_p, wb, wc, are, aim, dvec)


def _ssm_disc(a_re, a_im, log_dt, b_re, b_im):
    lam = lax.complex(jnp.minimum(a_re, -1e-4), a_im)
    dt = jnp.exp(log_dt)[..., None]
    lam_bar = jnp.exp(lam * dt)
    b_bar = ((lam_bar - 1.0) / lam)[..., None] * lax.complex(b_re, b_im)
    return jnp.real(lam_bar), jnp.imag(lam_bar), jnp.real(b_bar), jnp.imag(b_bar)


_EYE8 = np.eye(SSM_BLK, dtype=np.float32)


def _to_wb(bb_re, bb_im):
    bb = jnp.stack([bb_re, bb_im], axis=2).reshape(2, SSM_NB, SSM_BLK, 2, SSM_P, SSM_H)
    t = bb.transpose(1, 0, 5, 2, 3, 4)[:, :, None] * _EYE8[None, None, :, None, :, None, None]
    t = t.reshape(SSM_NB, 2, SSM_BLK, SSM_H, 4, 2, 2, SSM_P).transpose(0, 1, 2, 3, 4, 6, 5, 7)
    return t.reshape(SSM_NB, 2, SSM_BW, SSM_SW)


def _from_wb(dwb):
    t = dwb.reshape(SSM_NB, 2, SSM_BLK, SSM_H, 4, 2, 2, SSM_P).transpose(0, 1, 2, 3, 4, 6, 5, 7)
    t = t.reshape(SSM_NB, 2, SSM_BLK, SSM_H, SSM_BLK, 2, SSM_P)
    bb = jnp.sum(t * _EYE8[None, None, :, None, :, None, None], axis=2)
    bb = bb.transpose(1, 0, 3, 4, 5, 2).reshape(2, SSM_G, 2, SSM_P, SSM_H)
    return bb[:, :, 0], bb[:, :, 1]


def _to_wc(c_re, c_im):
    cc = jnp.stack([c_re, -c_im], axis=2).reshape(2, SSM_NB, SSM_BLK, 2, SSM_H, SSM_P)
    t = cc.transpose(1, 0, 2, 3, 5, 4)[:, :, :, :, :, None] * _EYE8.T[None, None, :, None, None, :, None]
    t = t.reshape(SSM_NB, 2, 4, 2, 2, SSM_P, SSM_BLK, SSM_H).transpose(0, 1, 2, 4, 3, 5, 6, 7)
    return t.reshape(SSM_NB, 2, SSM_SW, SSM_BW)


def _from_wc(dwc):
    t = dwc.reshape(SSM_NB, 2, 4, 2, 2, SSM_P, SSM_BLK, SSM_H).transpose(0, 1, 2, 4, 3, 5, 6, 7)
    t = t.reshape(SSM_NB, 2, SSM_BLK, 2, SSM_P, SSM_BLK, SSM_H)
    cc = jnp.sum(t * _EYE8.T[None, None, :, None, None, :, None], axis=5)
    cc = cc.transpose(1, 0, 2, 3, 5, 4).reshape(2, SSM_G, 2, SSM_H, SSM_P)
    return cc[:, :, 0], -cc[:, :, 1]


def _to_lam(v):
    return v.reshape(2, SSM_NB, 4, 128).transpose(1, 0, 2, 3)


def _from_lam(v):
    return v.transpose(1, 0, 2, 3).reshape(2, SSM_G, SSM_P)


_MESH = pl.DeviceIdType.MESH
_ANY = pl.BlockSpec(memory_space=pl.ANY)
_BIG = (("w_in", (D_MODEL, D_IN), 1, D_IN // N_CHIPS),
        ("w_glu", (D_SSM, 2 * D_SSM), 1, 2 * D_SSM // N_CHIPS),
        ("w_out", (D_ATTN + D_SSM, D_MODEL), 0, (D_ATTN + D_SSM) // N_CHIPS),
        ("w_ple_gate", (D_MODEL, D_MODEL), 0, D_MODEL // N_CHIPS),
        ("w_ple_proj", (PLE_DIM, D_MODEL), 1, D_MODEL // N_CHIPS))


def _place():
    x, y, c = lax.axis_index("x"), lax.axis_index("y"), lax.axis_index("c")
    return x, y, c, [(1 - x, y), (x, 1 - y), (1 - x, 1 - y)]


def _gather_weights(shards):
    nt = len(_BIG)

    def body(*refs):
        srcs, dsts, stage = refs[:nt], refs[nt:2 * nt], refs[2 * nt:3 * nt]
        send_sems, recv_sems, fwd_send_sems, fwd_recv_sems, loc_sems = refs[3 * nt:]
        x, y, c, chips = _place()
        k = 2 * x + y

        def shard_of(t, kk):
            _, _, axis, sz = _BIG[t]
            sl = pl.ds(pl.multiple_of(kk * sz, sz), sz)
            return dsts[t].at[:, sl] if axis == 1 else dsts[t].at[sl, :]

        def half_of(ref, t, cc):
            n = ref.shape[0] // 2
            return ref.at[pl.ds(pl.multiple_of(cc * n, n), n), :]

        def ici(j, t, kk):
            px, py = chips[j]
            return pltpu.make_async_remote_copy(src_ref=half_of(srcs[t], t, c), dst_ref=half_of(shard_of(t, kk), t, c),
                                                send_sem=send_sems.at[j, t], recv_sem=recv_sems.at[j, t],
                                                device_id=(px, py, c), device_id_type=_MESH)

        def forward(j, t, kk, cc):
            part = half_of(shard_of(t, kk), t, cc)
            return pltpu.make_async_remote_copy(src_ref=part, dst_ref=part, send_sem=fwd_send_sems.at[j, t],
                                                recv_sem=fwd_recv_sems.at[j, t], device_id=(x, y, 1 - c), device_id_type=_MESH)

        load = [pltpu.make_async_copy(srcs[t], stage[t], loc_sems.at[t]) for t in range(nt)]
        local = [pltpu.make_async_copy(stage[t], shard_of(t, k), loc_sems.at[t]) for t in range(nt)]
        sends = [ici(j, t, k) for t in range(nt) for j in range(3)]
        for cp in load + sends:
            cp.start()
        for t in range(nt):
            load[t].wait()
            local[t].start()
        forwards = []
        for t in range(nt):
            for j, (px, py) in enumerate(chips):
                ici(j, t, 2 * px + py).wait_recv()
                forwards.append(forward(j, t, 2 * px + py, c))
                forwards[-1].start()
        for t in range(nt):
            for j, (px, py) in enumerate(chips):
                forward(j, t, 2 * px + py, 1 - c).wait_recv()
        for cp in sends + forwards:
            cp.wait_send()
        for cp in local:
            cp.wait()

    sems = pltpu.SemaphoreType.DMA((3, nt))
    return pl.pallas_call(
        body, name="gather_weights",
        out_shape=[jax.ShapeDtypeStruct(shape, BF16) for _, shape, _, _ in _BIG],
        in_specs=[_ANY] * nt, out_specs=[_ANY] * nt,
        scratch_shapes=[pltpu.VMEM(s.shape, BF16) for s in shards] + [sems, sems, sems, sems, pltpu.SemaphoreType.DMA((nt,))],
        compiler_params=pltpu.CompilerParams(vmem_limit_bytes=VMEM_LIMIT_V7X),
    )(*shards)


SMALL_W = 1024
SMALL_ROWS = 72
N_SMALL = 8 * SMALL_ROWS * SMALL_W
_RED = tuple((shape, ax, (shape[0] // 2, sz) if ax == 1 else (sz // 2, shape[1]), BF16) for _, shape, ax, sz in _BIG) + \
    (((8 * SMALL_ROWS, SMALL_W), 0, (SMALL_ROWS, SMALL_W), F32),)
_RED_TR = 128


def _piece(ref, t, kk, cc):
    _, ax, (pr, pc), _ = _RED[t]
    if ax == 1:
        return ref.at[pl.ds(pl.multiple_of(cc * pr, pr), pr), pl.ds(pl.multiple_of(kk * pc, pc), pc)]
    return ref.at[pl.ds(pl.multiple_of((2 * kk + cc) * pr, pr), pr), :]


def _half_shape(t):
    shape, ax, (pr, pc), _ = _RED[t]
    return (pr, shape[1]) if ax == 1 else (N_CHIPS * pr, pc)


def _piece_in_half(ref, t, kk):
    _, ax, (pr, pc), _ = _RED[t]
    return ref.at[:, pl.ds(pl.multiple_of(kk * pc, pc), pc)] if ax == 1 else ref.at[pl.ds(pl.multiple_of(kk * pr, pr), pr), :]


def _grad_sibling_exchange(grads):
    n = len(_RED)
    n_dma = sum(1 if ax == 1 else N_CHIPS for _, ax, _, _ in _RED)

    def body(*refs):
        srcs, dsts, (send_sems, recv_sems) = refs[:n], refs[n:2 * n], refs[2 * n:]
        x, y, c, _ = _place()
        pairs = []
        for t, (_, ax, (pr, _), _) in enumerate(_RED):
            if ax == 1:
                pairs.append((srcs[t].at[pl.ds(pl.multiple_of((1 - c) * pr, pr), pr), :], dsts[t]))
            else:
                pairs += [(_piece(srcs[t], t, kk, 1 - c), _piece_in_half(dsts[t], t, kk)) for kk in range(N_CHIPS)]
        cps = [pltpu.make_async_remote_copy(src_ref=s, dst_ref=d, send_sem=send_sems.at[i], recv_sem=recv_sems.at[i],
                                            device_id=(x, y, 1 - c), device_id_type=_MESH) for i, (s, d) in enumerate(pairs)]
        for cp in cps:
            cp.start()
        for cp in cps:
            cp.wait()

    return pl.pallas_call(
        body, name="grad_sibling_exchange",
        out_shape=[jax.ShapeDtypeStruct(_half_shape(t), F32) for t in range(n)],
        in_specs=[_ANY] * n, out_specs=[_ANY] * n,
        scratch_shapes=[pltpu.SemaphoreType.DMA((n_dma,)), pltpu.SemaphoreType.DMA((n_dma,))],
    )(*grads)


def _chip_sum(t, g, rs, place):
    shape, ax, (pr, pc), dt = _RED[t]
    W = shape[1]
    tr = min(pr, _RED_TR)
    nb = pr // tr

    def body(place_ref, g_ref, rs_ref, o_ref):
        o_ref[...] = (g_ref[...] + rs_ref[...]).astype(o_ref.dtype)

    return pl.pallas_call(
        body, name="grad_chip_sum_%d" % t,
        out_shape=jax.ShapeDtypeStruct(rs.shape, dt),
        grid_spec=pltpu.PrefetchScalarGridSpec(
            num_scalar_prefetch=1, grid=(1 if ax == 1 else N_CHIPS, nb),
            in_specs=[pl.BlockSpec((tr, W), lambda kk, i, pr_: ((2 * kk + pr_[0]) * nb + i, 0)),
                      pl.BlockSpec((tr, W), lambda kk, i, pr_: (kk * nb + i, 0))],
            out_specs=pl.BlockSpec((tr, W), lambda kk, i, pr_: (kk * nb + i, 0))),
        compiler_params=_params(("parallel", "parallel")),
    )(place, g, rs)


def _grad_chip_exchange(sums):
    n = len(_RED)

    def body(*refs):
        srcs, dsts, (send_sems, recv_sems) = refs[:n], refs[n:2 * n], refs[2 * n:]
        x, y, c, chips = _place()
        cps = [pltpu.make_async_remote_copy(src_ref=_piece_in_half(srcs[t], t, 2 * px + py), dst_ref=dsts[t].at[j],
                                            send_sem=send_sems.at[j, t], recv_sem=recv_sems.at[j, t],
                                            device_id=(px, py, c), device_id_type=_MESH)
               for t in range(n) for j, (px, py) in enumerate(chips)]
        for cp in cps:
            cp.start()
        for cp in cps:
            cp.wait()

    return pl.pallas_call(
        body, name="grad_chip_exchange",
        out_shape=[jax.ShapeDtypeStruct((3,) + piece, dt) for _, _, piece, dt in _RED],
        in_specs=[_ANY] * n, out_specs=[_ANY] * n,
        scratch_shapes=[pltpu.SemaphoreType.DMA((3, n)), pltpu.SemaphoreType.DMA((3, n))],
    )(*sums)


def _total_sum(t, g, rs, rc, place):
    shape, ax, (pr, pc), _ = _RED[t]
    tr = min(pr, _RED_TR)
    nb = pr // tr
    small = t == len(_RED) - 1

    def body(place_ref, g_ref, rs_ref, rc_ref, o_ref):
        o_ref[...] = (g_ref[...] + rs_ref[...]) + rc_ref[0].astype(F32) + rc_ref[1].astype(F32) + rc_ref[2].astype(F32)

    if ax == 1:
        g_map = lambda i, pr_: (pr_[0] * nb + i, pr_[1])
        rs_map = lambda i, pr_: (i, pr_[1])
    else:
        g_map = lambda i, pr_: ((2 * pr_[1] + pr_[0]) * nb + i, 0)
        rs_map = lambda i, pr_: (pr_[1] * nb + i, 0)
    o_map = (lambda i, pr_: ((2 * pr_[1] + pr_[0]) * nb + i, 0)) if small else (lambda i, pr_: (pr_[0] * nb + i, 0))
    return pl.pallas_call(
        body, name="grad_total_sum_%d" % t,
        out_shape=jax.ShapeDtypeStruct(((8 if small else 2) * pr, pc), F32),
        grid_spec=pltpu.PrefetchScalarGridSpec(
            num_scalar_prefetch=1, grid=(nb,),
            in_specs=[pl.BlockSpec((tr, pc), g_map), pl.BlockSpec((tr, pc), rs_map),
                      pl.BlockSpec((3, tr, pc), lambda i, pr_: (0, i, 0))],
            out_specs=pl.BlockSpec((tr, pc), o_map)),
        compiler_params=_params(("parallel",)),
    )(place, g, rs, rc)


def _grad_final_exchange(totals):
    n = len(_RED)
    nb = n - 1

    def body(*refs):
        srcs, dsts, (send_sems, recv_sems) = refs[:n], refs[n:2 * n], refs[2 * n:]
        x, y, c, chips = _place()
        me = 4 * x + 2 * y + c
        others = [(x, y, 1 - c)] + [(px, py, cc) for (px, py) in chips for cc in (c, 1 - c)]

        def half(ref, t, cc):
            pr = _RED[t][2][0]
            return ref.at[pl.ds(pl.multiple_of(cc * pr, pr), pr), :]

        def eighth(ref, dev):
            return ref.at[pl.ds(pl.multiple_of(dev * SMALL_ROWS, SMALL_ROWS), SMALL_ROWS), :]

        def big_copy(t, cc):
            return pltpu.make_async_remote_copy(src_ref=half(srcs[t], t, cc), dst_ref=half(dsts[t], t, cc), send_sem=send_sems.at[t],
                                                recv_sem=recv_sems.at[t], device_id=others[0], device_id_type=_MESH)

        def small_copy(i, dev):
            return pltpu.make_async_remote_copy(src_ref=eighth(srcs[nb], dev), dst_ref=eighth(dsts[nb], dev),
                                                send_sem=send_sems.at[nb + i], recv_sem=recv_sems.at[nb + i],
                                                device_id=others[i], device_id_type=_MESH)

        sends = [big_copy(t, c) for t in range(nb)] + [small_copy(i, me) for i in range(7)]
        for cp in sends:
            cp.start()
        for t in range(nb):
            big_copy(t, 1 - c).wait_recv()
        for i, (px, py, pc) in enumerate(others):
            small_copy(i, 4 * px + 2 * py + pc).wait_recv()
        for cp in sends:
            cp.wait_send()

    return pl.pallas_call(
        body, name="grad_final_exchange",
        out_shape=[jax.ShapeDtypeStruct(a.shape, F32) for a in totals],
        in_specs=[_ANY] * n, out_specs=[_ANY] * n,
        input_output_aliases={t: t for t in range(n)},
        scratch_shapes=[pltpu.SemaphoreType.DMA((nb + 7,)), pltpu.SemaphoreType.DMA((nb + 7,))],
    )(*totals)


def _reduce_gradients(big_grads, small_flat):
    c = lax.axis_index("c")
    k = 2 * lax.axis_index("x") + lax.axis_index("y")
    place = jnp.stack([c, k]).astype(jnp.int32)
    grads = list(big_grads) + [small_flat.reshape(8 * SMALL_ROWS, SMALL_W)]
    n = len(grads)
    from_sibling = _grad_sibling_exchange(grads)
    chip_sums = [_chip_sum(t, grads[t], from_sibling[t], place) for t in range(n)]
    from_chips = _grad_chip_exchange(chip_sums)
    totals = [_total_sum(t, grads[t], from_sibling[t], from_chips[t], place) for t in range(n)]
    *shards, small = _grad_final_exchange(totals)
    return shards, small.reshape(-1)


def _adamw(w, g, m, v, name):
    def fn(w, g, m, v):
        m = ADAM_B1 * m + (1.0 - ADAM_B1) * g
        v = ADAM_B2 * v + (1.0 - ADAM_B2) * (g * g)
        m_hat = m / (1.0 - ADAM_B1 ** ADAM_STEP)
        v_hat = v / (1.0 - ADAM_B2 ** ADAM_STEP)
        return -ADAM_LR * (m_hat / (jnp.sqrt(v_hat) + ADAM_EPS) + ADAM_WD * w), m, v
    W = w.shape[1]
    return _rowwise(fn, [(a, 0, W) for a in (w, g, m, v)], [], [(W, F32)] * 3, tr=128, name=name)


def _chunks(arr, off, width, w=512):
    return [(arr, off + i * w, w) for i in range(width // w)]


def _cat(vs):
    return jnp.concatenate(vs, axis=1)


def _forward_backward(x, p_b, tgt, full, small):
    L = x.shape[0]
    w_in, w_glu, w_out, w_pg, w_pp = full
    row = lambda v: v.reshape(1, -1)
    g_mix, g_ple, g_fin = row(small["norm_mix"]), row(small["norm_ple"]), row(small["norm_final"])
    gq, gk, b_glu = row(small["q_norm"]), row(small["k_norm"]), row(small["b_glu"])
    cos, sin = _rope_tables(L)

    hn_b, = _rowwise(lambda x, g: x * _rms(x) * g, [(x, 0, D_MODEL)], [g_mix], [(D_MODEL, BF16)], name="norm_mix")
    z = _matmul(hn_b, w_in, name="mm_in")
    qr, kr, vb = _attn_prep(z, gq, gk, cos, sin)
    o, lse = _attn_fwd(qr, kr, vb)

    ssm_names = ("ssm_a_re", "ssm_a_im", "ssm_log_dt", "ssm_b_re", "ssm_b_im")
    (lre, lim, bre, bim), disc_vjp = jax.vjp(_ssm_disc, *[small[n][0] for n in ssm_names])
    wb, wc = _to_wb(bre, bim).astype(BF16), _to_wc(small["ssm_c_re"][0], small["ssm_c_im"][0]).astype(BF16)
    are, aim, dvec = _to_lam(lre), _to_lam(lim), row(small["ssm_d"])
    u_p = _seg_perm(z[:, Z_U:Z_U + D_SSM])
    y_s = _seg_unperm(_ssm_fwd(u_p, wb, wc, are, aim, dvec))
    ge_b, = _rowwise(_gelu, [(y_s, 0, D_SSM)], [], [(D_SSM, BF16)], name="gelu")
    glu = _matmul(ge_b, w_glu, name="mm_glu")

    def merge(ga0, ga1, a, b, gs0, gs1, o, bias):
        sa, _ = _silu_and_grad(_cat([ga0, ga1]))
        ss, _ = _silu_and_grad(_cat([gs0, gs1]))
        y2 = (a + bias[:, :D_SSM]) * _sig(b + bias[:, D_SSM:])
        return _cat([o * sa, y2 * ss])
    merge_rows = _chunks(z, Z_GA, D_ATTN) + [(glu, 0, D_SSM), (glu, D_SSM, D_SSM)] + _chunks(z, Z_GS, D_SSM) + [(o, 0, D_ATTN)]
    cat_b, = _rowwise(merge, merge_rows, [b_glu], [(D_MODEL, BF16)], name="merge")
    t_out = _matmul(cat_b, w_out, name="mm_out")

    def resid(x, t, g):
        h1 = x + t
        return h1, h1 * _rms(h1) * g
    h1, hp_b = _rowwise(resid, [(x, 0, D_MODEL), (t_out, 0, D_MODEL)], [g_ple], [(D_MODEL, F32), (D_MODEL, BF16)], name="resid_norm")
    gl = _matmul(hp_b, w_pg, name="mm_ple_gate")
    pp = _matmul(p_b, w_pp, name="mm_ple_proj")

    def head(h1, gl, pp, tgt, g):
        gate = _sig(gl)
        h2 = h1 + gate * pp
        r = _rms(h2)
        n = h2 * r
        err = n * g - tgt
        dy = err * (1.0 / D_MODEL)
        dn = dy * g
        dh2 = r * (dn - n * jnp.mean(dn * n, axis=-1, keepdims=True))
        dgate = dh2 * pp
        return dh2, dh2 * gate, dgate * gate * (1.0 - gate), _colsum(dy * n), _colsum(0.5 * err * err * (1.0 / D_MODEL))
    dh2, dpp_b, dgl_b, dg_fin, loss_cols = _rowwise(
        head, [(a, 0, D_MODEL) for a in (h1, gl, pp, tgt)], [g_fin],
        [(D_MODEL, F32), (D_MODEL, BF16), (D_MODEL, BF16)], [(1, D_MODEL), (1, D_MODEL)], name="loss_head")

    dw_pp = _matmul(p_b, dpp_b, ta=True, name="mm_d_w_ple_proj")
    dw_pg = _matmul(hp_b, dgl_b, ta=True, name="mm_d_w_ple_gate")
    dhp = _matmul(dgl_b, w_pg, tb=True, name="mm_d_hp")

    def resid_bwd(dhp, h1, dh2, g):
        dx, dg = _rms_bwd(h1, g, dhp)
        dh1 = dh2 + dx
        return dh1, dh1, _colsum(dg)
    dh1, dh1_b, dg_ple = _rowwise(resid_bwd, [(a, 0, D_MODEL) for a in (dhp, h1, dh2)], [g_ple],
                                  [(D_MODEL, F32), (D_MODEL, BF16)], [(1, D_MODEL)], name="resid_norm_bwd")
    dw_out = _matmul(cat_b, dh1_b, ta=True, name="mm_d_w_out")
    dcat = _matmul(dh1_b, w_out, tb=True, name="mm_d_cat")

    def merge_bwd(dya, dys, ga0, ga1, a, b, gs0, gs1, o, bias):
        ga, gs = _cat([ga0, ga1]), _cat([gs0, gs1])
        sa, dsa = _silu_and_grad(ga)
        ss, dss = _silu_and_grad(gs)
        a, sb = a + bias[:, :D_SSM], _sig(b + bias[:, D_SSM:])
        dy2 = dys * ss
        dglu = _cat([dy2 * sb, dy2 * a * sb * (1.0 - sb)])
        return dya * sa, dya * o * dsa, dys * (a * sb) * dss, dglu, _colsum(dglu)
    do_b, dga_b, dgs_b, dglu_b, db_glu = _rowwise(
        merge_bwd, [(dcat, 0, D_ATTN), (dcat, D_ATTN, D_SSM)] + merge_rows, [b_glu],
        [(D_ATTN, BF16), (D_ATTN, BF16), (D_SSM, BF16), (2 * D_SSM, BF16)], [(1, 2 * D_SSM)], name="merge_bwd")
    dw_glu = _matmul(ge_b, dglu_b, ta=True, name="mm_d_w_glu")
    dge = _matmul(dglu_b, w_glu, tb=True, name="mm_d_ge")
    dy_s, = _rowwise(lambda dge, y: dge * _gelu_grad(y), [(dge, 0, D_SSM), (y_s, 0, D_SSM)], [], [(D_SSM, F32)], name="gelu_bwd")
    du_p, dwb, dwc, dare, daim, d_ssm_d = _ssm_bwd(u_p, _seg_perm(dy_s), wb, wc, are, aim, dvec)
    db_re_bar, db_im_bar = _from_wb(dwb)
    dc_re, dc_im = _from_wc(dwc)
    da_re, da_im, dlog_dt, db_re, db_im = disc_vjp((_from_lam(dare), _from_lam(daim), db_re_bar, db_im_bar))

    dqr, dkr, dv = _attn_bwd(qr, kr, vb, do_b, lse)
    dq_b, dk_b, dgq, dgk = _attn_prep_bwd(dqr, dkr, z, gq, gk, cos, sin)
    dz_b = _cat([dq_b, dk_b, dv.astype(BF16), dga_b, _seg_unperm(du_p).astype(BF16), dgs_b])
    dw_in = _matmul(hn_b, dz_b, ta=True, name="mm_d_w_in")
    dhn = _matmul(dz_b, w_in, tb=True, name="mm_d_hn")

    def norm_bwd(dhn, x, dh1, g):
        dx, dg = _rms_bwd(x, g, dhn)
        return dh1 + dx, _colsum(dg)
    grad_x, dg_mix = _rowwise(norm_bwd, [(a, 0, D_MODEL) for a in (dhn, x, dh1)], [g_mix], [(D_MODEL, F32)], [(1, D_MODEL)],
                              name="norm_mix_bwd")

    small_grads = {"norm_mix": dg_mix, "q_norm": dgq, "k_norm": dgk, "ssm_a_re": da_re, "ssm_a_im": da_im, "ssm_log_dt": dlog_dt,
                   "ssm_b_re": db_re, "ssm_b_im": db_im, "ssm_c_re": dc_re, "ssm_c_im": dc_im, "ssm_d": d_ssm_d,
                   "b_glu": db_glu, "norm_ple": dg_ple, "norm_final": dg_fin}
    return jnp.sum(loss_cols), grad_x, [dw_in, dw_glu, dw_out, dw_pg, dw_pp], small_grads


_SMALL = ("norm_mix", "q_norm", "k_norm", "ssm_a_re", "ssm_a_im", "ssm_log_dt", "ssm_b_re", "ssm_b_im", "ssm_c_re", "ssm_c_im",
          "ssm_d", "b_glu", "norm_ple", "norm_final")
_WEIGHTS = ("norm_mix", "w_in", "q_norm", "k_norm", "ssm_a_re", "ssm_a_im", "ssm_log_dt", "ssm_b_re", "ssm_b_im", "ssm_c_re",
            "ssm_c_im", "ssm_d", "w_glu", "b_glu", "w_out", "norm_ple", "w_ple_gate", "w_ple_proj", "norm_final")


def _flat_small(d):
    flat = jnp.concatenate([d[n].reshape(-1).astype(F32) for n in _SMALL])
    return jnp.pad(flat, (0, N_SMALL - flat.shape[0]))


def _split_small(flat, like):
    out, off = {}, 0
    for n in _SMALL:
        sz = math.prod(like[n].shape)
        out[n] = flat[off:off + sz].reshape(like[n].shape)
        off += sz
    return out


def kernel(x, p, norm_mix, w_in, q_norm, k_norm, ssm_a_re, ssm_a_im, ssm_log_dt, ssm_b_re, ssm_b_im, ssm_c_re, ssm_c_im, ssm_d, w_glu, b_glu, w_out, norm_ple, w_ple_gate, w_ple_proj, norm_final, loss_target, m_norm_mix, m_w_in, m_q_norm, m_k_norm, m_ssm_a_re, m_ssm_a_im, m_ssm_log_dt, m_ssm_b_re, m_ssm_b_im, m_ssm_c_re, m_ssm_c_im, m_ssm_d, m_w_glu, m_b_glu, m_w_out, m_norm_ple, m_w_ple_gate, m_w_ple_proj, m_norm_final, v_norm_mix, v_w_in, v_q_norm, v_k_norm, v_ssm_a_re, v_ssm_a_im, v_ssm_log_dt, v_ssm_b_re, v_ssm_b_im, v_ssm_c_re, v_ssm_c_im, v_ssm_d, v_w_glu, v_b_glu, v_w_out, v_norm_ple, v_w_ple_gate, v_w_ple_proj, v_norm_final):
    w = dict(norm_mix=norm_mix, w_in=w_in, q_norm=q_norm, k_norm=k_norm, ssm_a_re=ssm_a_re, ssm_a_im=ssm_a_im,
             ssm_log_dt=ssm_log_dt, ssm_b_re=ssm_b_re, ssm_b_im=ssm_b_im, ssm_c_re=ssm_c_re, ssm_c_im=ssm_c_im, ssm_d=ssm_d,
             w_glu=w_glu, b_glu=b_glu, w_out=w_out, norm_ple=norm_ple, w_ple_gate=w_ple_gate, w_ple_proj=w_ple_proj,
             norm_final=norm_final)
    m = dict(norm_mix=m_norm_mix, w_in=m_w_in, q_norm=m_q_norm, k_norm=m_k_norm, ssm_a_re=m_ssm_a_re, ssm_a_im=m_ssm_a_im,
             ssm_log_dt=m_ssm_log_dt, ssm_b_re=m_ssm_b_re, ssm_b_im=m_ssm_b_im, ssm_c_re=m_ssm_c_re, ssm_c_im=m_ssm_c_im,
             ssm_d=m_ssm_d, w_glu=m_w_glu, b_glu=m_b_glu, w_out=m_w_out, norm_ple=m_norm_ple, w_ple_gate=m_w_ple_gate,
             w_ple_proj=m_w_ple_proj, norm_final=m_norm_final)
    v = dict(norm_mix=v_norm_mix, w_in=v_w_in, q_norm=v_q_norm, k_norm=v_k_norm, ssm_a_re=v_ssm_a_re, ssm_a_im=v_ssm_a_im,
             ssm_log_dt=v_ssm_log_dt, ssm_b_re=v_ssm_b_re, ssm_b_im=v_ssm_b_im, ssm_c_re=v_ssm_c_re, ssm_c_im=v_ssm_c_im,
             ssm_d=v_ssm_d, w_glu=v_w_glu, b_glu=v_b_glu, w_out=v_w_out, norm_ple=v_norm_ple, w_ple_gate=v_w_ple_gate,
             w_ple_proj=v_w_ple_proj, norm_final=v_norm_final)
    big_names = [n for n, _, _, _ in _BIG]

    full = _gather_weights([w[n][0].astype(BF16) for n in big_names])
    small = {n: w[n] for n in _SMALL}
    loss_part, grad_x, big_grads, small_grads = _forward_backward(
        x[0], p[0, 0].astype(BF16), loss_target[0], full, small)
    loss = lax.psum(loss_part, ("x", "y", "c"))

    big_red, small_red = _reduce_gradients(big_grads, _flat_small(small_grads))
    grads = _split_small(small_red, w)
    delta, new_m, new_v = {}, {}, {}
    for n, g in zip(big_names, big_red):
        grads[n] = g[None]
        d_, m_, v_ = _adamw(w[n][0], g, m[n][0], v[n][0], "adamw_" + n)
        delta[n], new_m[n], new_v[n] = d_[None], m_[None], v_[None]
    d_, m_, v_ = _adamw(*[a.reshape(-1, SMALL_W) for a in (_flat_small(w), small_red, _flat_small(m), _flat_small(v))], "adamw_small")
    delta.update(_split_small(d_.reshape(-1), w))
    new_m.update(_split_small(m_.reshape(-1), w))
    new_v.update(_split_small(v_.reshape(-1), w))
    return (loss, grad_x[None], *[grads[n] for n in _WEIGHTS], *[delta[n] for n in _WEIGHTS],
            *[new_m[n] for n in _WEIGHTS], *[new_v[n] for n in _WEIGHTS])
```

```python
import functools
import math

import jax
import jax.numpy as jnp
import numpy as np
from jax import lax
from jax.experimental import pallas as pl
from jax.experimental.pallas import tpu as pltpu

D_MODEL = 2048
GRID_W = 64
PLE_DIM = 256
D_ATTN = 1024
N_HEADS = 8
N_KV = 2
HEAD_DIM = 128
ROPE_THETA = 10000.0
D_SSM = 1024
SSM_H = 16
SSM_G = 64
SSM_P = 64
D_KV = N_KV * HEAD_DIM
D_IN = 2 * D_ATTN + 2 * D_KV + 2 * D_SSM
EPS = 1e-6
Z_Q, Z_K, Z_V, Z_GA, Z_U, Z_GS = 0, 1024, 1280, 1536, 2560, 3584

ADAM_LR, ADAM_B1, ADAM_B2, ADAM_EPS, ADAM_WD, ADAM_STEP = 0.001, 0.9, 0.999, 1e-08, 0.01, 10

N_CHIPS = 4
VMEM_LIMIT_V7X = 56 * 1024 * 1024
F32 = jnp.float32
BF16 = jnp.bfloat16


def _params(sem, vmem=VMEM_LIMIT_V7X):
    return pltpu.CompilerParams(dimension_semantics=sem, vmem_limit_bytes=vmem)


def _matmul(a, b, *, ta=False, tb=False, out_dtype=F32, tm=1024, tn=512, name):
    M, K = (a.shape[1], a.shape[0]) if ta else a.shape
    N = b.shape[0] if tb else b.shape[1]
    tm, tn = min(tm, M), min(tn, N)
    assert M % tm == 0 and N % tn == 0, (name, M, N, K)
    dims = (((0 if ta else 1,), (1 if tb else 0,)), ((), ()))

    def body(a_ref, b_ref, o_ref):
        o_ref[...] = lax.dot_general(a_ref[...], b_ref[...], dims, preferred_element_type=F32).astype(o_ref.dtype)

    a_spec = pl.BlockSpec((K, tm), lambda i, j: (0, i)) if ta else pl.BlockSpec((tm, K), lambda i, j: (i, 0))
    b_spec = pl.BlockSpec((tn, K), lambda i, j: (j, 0)) if tb else pl.BlockSpec((K, tn), lambda i, j: (0, j))
    return pl.pallas_call(
        body, name=name,
        out_shape=jax.ShapeDtypeStruct((M, N), out_dtype),
        grid=(M // tm, N // tn),
        in_specs=[a_spec, b_spec],
        out_specs=pl.BlockSpec((tm, tn), lambda i, j: (i, j)),
        compiler_params=_params(("parallel", "parallel")),
    )(a, b)


def _rowwise(fn, rows, consts, outs, accs=(), *, tr=256, name):
    L = rows[0][0].shape[0]
    tr = math.gcd(tr, L)
    assert tr % 8 == 0 or tr == L, (name, L, tr)
    n_in, n_c, n_o, n_a = len(rows), len(consts), len(outs), len(accs)

    def body(*refs):
        ins = [r[...] for r in refs[:n_in + n_c]]
        res = fn(*ins)
        if not isinstance(res, (tuple, list)):
            res = (res,)
        o_refs = refs[n_in + n_c:n_in + n_c + n_o]
        a_refs = refs[n_in + n_c + n_o:]
        for r, v in zip(o_refs, res[:n_o]):
            r[...] = v.astype(r.dtype)
        if n_a:
            first = pl.program_id(0) == 0

            @pl.when(first)
            def _():
                for r, v in zip(a_refs, res[n_o:]):
                    r[...] = v.astype(F32)

            @pl.when(jnp.logical_not(first))
            def _():
                for r, v in zip(a_refs, res[n_o:]):
                    r[...] += v.astype(F32)

    in_specs = []
    for arr, off, w in rows:
        assert off % w == 0, (name, off, w)
        in_specs.append(pl.BlockSpec((tr, w), functools.partial(lambda i, c: (i, c), c=off // w)))
    for c in consts:
        in_specs.append(pl.BlockSpec(c.shape, lambda i: (0, 0)))
    out_shape = [jax.ShapeDtypeStruct((L, w), dt) for w, dt in outs] + [jax.ShapeDtypeStruct(s, F32) for s in accs]
    out_specs = [pl.BlockSpec((tr, w), lambda i: (i, 0)) for w, _ in outs] + [pl.BlockSpec(s, lambda i: (0, 0)) for s in accs]
    res = pl.pallas_call(
        body, name=name,
        out_shape=out_shape,
        grid=(L // tr,),
        in_specs=in_specs,
        out_specs=out_specs,
        compiler_params=_params(("arbitrary",) if n_a else ("parallel",)),
    )(*[r[0] for r in rows], *consts)
    return res


def _sig(x):
    return jax.nn.sigmoid(x)


def _silu_and_grad(x):
    s = _sig(x)
    return x * s, s * (1.0 + x * (1.0 - s))


_GELU_C = math.sqrt(2.0 / math.pi)


def _gelu(x):
    return 0.5 * x * (1.0 + jnp.tanh(_GELU_C * (x + 0.044715 * x * x * x)))


def _gelu_grad(x):
    t = jnp.tanh(_GELU_C * (x + 0.044715 * x * x * x))
    return 0.5 * (1.0 + t) + 0.5 * x * (1.0 - t * t) * _GELU_C * (1.0 + 3.0 * 0.044715 * x * x)


def _rms(x):
    return lax.rsqrt(jnp.mean(x * x, axis=-1, keepdims=True) + EPS)


def _rms_bwd(x, g, dy):
    r = _rms(x)
    n = x * r
    dn = dy * g
    return r * (dn - n * jnp.mean(dn * n, axis=-1, keepdims=True)), dy * n


def _colsum(v):
    return jnp.sum(v, axis=0, keepdims=True)


def _rope_partner(x):
    lane = lax.broadcasted_iota(jnp.int32, x.shape, x.ndim - 1)
    return jnp.where(lane % 64 < 32, pltpu.roll(x, 96, x.ndim - 1), pltpu.roll(x, 32, x.ndim - 1))


def _rope_tables(L):
    rows_n = L // GRID_W
    rows = jnp.repeat(jnp.arange(rows_n), GRID_W).astype(F32)
    cols = jnp.tile(jnp.arange(GRID_W), rows_n).astype(F32)
    n_freq = HEAD_DIM // 4
    inv_freq = ROPE_THETA ** (-jnp.arange(n_freq, dtype=F32) / n_freq)
    ar, ac = rows[:, None] * inv_freq[None, :], cols[:, None] * inv_freq[None, :]
    cos = jnp.concatenate([jnp.cos(ar), jnp.cos(ar), jnp.cos(ac), jnp.cos(ac)], axis=-1)
    sin = jnp.concatenate([-jnp.sin(ar), jnp.sin(ar), -jnp.sin(ac), jnp.sin(ac)], axis=-1)
    return cos, sin


def _heads(v):
    return [v[:, h * HEAD_DIM:(h + 1) * HEAD_DIM] for h in range(v.shape[1] // HEAD_DIM)]


def _attn_prep(z, q_norm, k_norm, cos, sin):
    def fn(q, k, v, cos, sin, gq, gk):
        def one(xh, g):
            xn = xh * _rms(xh) * g
            return xn * cos + _rope_partner(xn) * sin
        qr = jnp.concatenate([one(h, gq) for h in _heads(q)], axis=1)
        kr = jnp.concatenate([one(h, gk) for h in _heads(k)], axis=1)
        return qr, kr, v
    return _rowwise(fn, [(z, Z_Q, D_ATTN), (z, Z_K, D_KV), (z, Z_V, D_KV), (cos, 0, HEAD_DIM), (sin, 0, HEAD_DIM)],
                    [q_norm, k_norm], [(D_ATTN, BF16), (D_KV, BF16), (D_KV, BF16)], name="attn_prep")


def _attn_prep_bwd(dqr, dkr, z, q_norm, k_norm, cos, sin):
    def fn(dqr, dkr, q, k, cos, sin, gq, gk):
        def one(dyh, xh, g):
            dn = dyh * cos + _rope_partner(dyh * sin)
            return _rms_bwd(xh, g, dn)
        rq = [one(a, b, gq) for a, b in zip(_heads(dqr), _heads(q))]
        rk = [one(a, b, gk) for a, b in zip(_heads(dkr), _heads(k))]
        dq = jnp.concatenate([r[0] for r in rq], axis=1)
        dk = jnp.concatenate([r[0] for r in rk], axis=1)
        return dq, dk, _colsum(sum(r[1] for r in rq)), _colsum(sum(r[1] for r in rk))
    return _rowwise(fn, [(dqr, 0, D_ATTN), (dkr, 0, D_KV), (z, Z_Q, D_ATTN), (z, Z_K, D_KV), (cos, 0, HEAD_DIM), (sin, 0, HEAD_DIM)],
                    [q_norm, k_norm], [(D_ATTN, BF16), (D_KV, BF16)], [(1, HEAD_DIM), (1, HEAD_DIM)], name="attn_prep_bwd")


_QK_T = (((1,), (1,)), ((), ()))
_TA = (((0,), (0,)), ((), ()))
_REP = N_HEADS // N_KV


_EXP2_SCALE = HEAD_DIM ** -0.5 * math.log2(math.e)
ATTN_FWD_KEY_CHUNKS = 4
ATTN_BWD_KEY_CHUNKS = 8


def _attn_fwd(qr, kr, vb, *, tq=1024):
    L = qr.shape[0]
    tq = min(tq, L)
    kc = L // ATTN_FWD_KEY_CHUNKS

    def body(q_ref, k_ref, v_ref, o_ref, lse_ref):
        q = q_ref[...]
        m = jnp.full((tq, 1), -jnp.inf, F32)
        l = jnp.zeros((tq, 1), F32)
        o = jnp.zeros((tq, HEAD_DIM), F32)
        for c in range(ATTN_FWD_KEY_CHUNKS):
            ks = slice(c * kc, (c + 1) * kc)
            s = lax.dot_general(q, k_ref[ks, :], _QK_T, preferred_element_type=F32)
            m_new = jnp.maximum(m, jnp.max(s, axis=1, keepdims=True))
            a = jnp.exp2((m - m_new) * _EXP2_SCALE)
            p = jnp.exp2((s - m_new) * _EXP2_SCALE)
            l = a * l + jnp.sum(p, axis=1, keepdims=True)
            o = a * o + jnp.dot(p.astype(BF16), v_ref[ks, :], preferred_element_type=F32)
            m = m_new
        o_ref[...] = o * (1.0 / l)
        lse_ref[...] = m * _EXP2_SCALE + jnp.log2(l)

    kv = pl.BlockSpec((L, HEAD_DIM), lambda h, i: (0, h // _REP))
    return pl.pallas_call(
        body, name="attn_fwd",
        out_shape=[jax.ShapeDtypeStruct((L, D_ATTN), F32), jax.ShapeDtypeStruct((N_HEADS, L, 1), F32)],
        grid=(N_HEADS, L // tq),
        in_specs=[pl.BlockSpec((tq, HEAD_DIM), lambda h, i: (i, h)), kv, kv],
        out_specs=[pl.BlockSpec((tq, HEAD_DIM), lambda h, i: (i, h)),
                   pl.BlockSpec((None, tq, 1), lambda h, i: (h, i, 0))],
        compiler_params=_params(("parallel", "parallel")),
    )(qr, kr, vb)


def _attn_bwd(qr, kr, k_t, vb, do, lse, *, tq=512):
    L = qr.shape[0]
    tq = min(tq, L)
    scale = HEAD_DIM ** -0.5
    kc = L // ATTN_BWD_KEY_CHUNKS

    def body(q_ref, k_ref, kt_ref, v_ref, do_ref, lse_ref, dq_ref, dk_ref, dv_ref):
        @pl.when((pl.program_id(1) == 0) & (pl.program_id(2) == 0))
        def _():
            dk_ref[...] = jnp.zeros_like(dk_ref)
            dv_ref[...] = jnp.zeros_like(dv_ref)

        q, do, lse = q_ref[...], do_ref[...], lse_ref[...]
        keys = [slice(c * kc, (c + 1) * kc) for c in range(ATTN_BWD_KEY_CHUNKS)]
        ps, dps = [], []
        for ks in keys:
            st = lax.dot_general(k_ref[ks, :], q, _QK_T, preferred_element_type=F32)
            p = jnp.exp2(st * _EXP2_SCALE - lse)
            dv_ref[ks, :] += jnp.dot(p.astype(BF16), do, preferred_element_type=F32)
            ps.append(p)
            dps.append(lax.dot_general(v_ref[ks, :], do, _QK_T, preferred_element_type=F32))
        delta = sum(jnp.sum(p * dp, axis=0, keepdims=True) for p, dp in zip(ps, dps))
        dq_t = 0.0
        for ks, p, dp in zip(keys, ps, dps):
            ds = (p * (dp - delta) * scale).astype(BF16)
            dk_ref[ks, :] += jnp.dot(ds, q, preferred_element_type=F32)
            dq_t = dq_t + jnp.dot(kt_ref[:, ks], ds, preferred_element_type=F32)
        dq_ref[...] = dq_t.T

    head = lambda g, r, i: (i, g * _REP + r)
    kv = pl.BlockSpec((L, HEAD_DIM), lambda g, r, i: (0, g))
    return pl.pallas_call(
        body, name="attn_bwd",
        out_shape=[jax.ShapeDtypeStruct((L, D_ATTN), F32), jax.ShapeDtypeStruct((L, D_KV), F32), jax.ShapeDtypeStruct((L, D_KV), F32)],
        grid=(N_KV, _REP, L // tq),
        in_specs=[pl.BlockSpec((tq, HEAD_DIM), head), kv,
                  pl.BlockSpec((HEAD_DIM, L), lambda g, r, i: (g, 0)), kv,
                  pl.BlockSpec((tq, HEAD_DIM), head),
                  pl.BlockSpec((None, 1, tq), lambda g, r, i: (g * _REP + r, 0, i))],
        out_specs=[pl.BlockSpec((tq, HEAD_DIM), head), kv, kv],
        compiler_params=_params(("parallel", "arbitrary", "arbitrary")),
    )(qr, kr, k_t, vb, do, lse)


SSM_BLK = 8
SSM_NB = SSM_G // SSM_BLK
SSM_BW = SSM_BLK * SSM_H
SSM_SW = SSM_BLK * 2 * SSM_P
SSM_SEG = 8
SSM_UNROLL = 4


def _seg_perm(a):
    L, C = a.shape
    return a.reshape(SSM_SEG, L // SSM_SEG, C).transpose(1, 0, 2).reshape(L, C)


def _seg_unperm(a):
    L, C = a.shape
    return a.reshape(L // SSM_SEG, SSM_SEG, C).transpose(1, 0, 2).reshape(L, C)


def _cplx_pow2(a, b, n):
    for _ in range(int(math.log2(n))):
        a, b = a * a - b * b, 2.0 * a * b
    return a, b


def _seg_scan(ref, a, b, T, reverse):
    npair = len(a)
    zero = jnp.zeros((SSM_SEG, 128), F32)

    def make_step(store):
        def step(t, carry):
            lt = (T - 1 - t) if reverse else t
            row = pl.multiple_of(lt * SSM_SEG, SSM_SEG)
            blk = ref[pl.ds(row, SSM_SEG), :]
            new = []
            for q in range(npair):
                re, im = carry[2 * q], carry[2 * q + 1]
                nre = a[q] * re - b[q] * im + blk[:, q * 256:q * 256 + 128]
                nim = a[q] * im + b[q] * re + blk[:, q * 256 + 128:q * 256 + 256]
                new += [nre, nim]
            if store:
                ref[pl.ds(row, SSM_SEG), :] = jnp.concatenate(new, axis=1)
            return tuple(new)
        return step

    ends = lax.fori_loop(0, T, make_step(False), (zero,) * (2 * npair), unroll=SSM_UNROLL)
    sub = lax.broadcasted_iota(jnp.int32, (SSM_SEG, 128), 0)
    keep = (sub != SSM_SEG - 1) if reverse else (sub != 0)
    shift = (SSM_SEG - 1) if reverse else 1
    init = []
    for q in range(npair):
        pa, pb = _cplx_pow2(a[q], b[q], T)
        xr, xi = zero, zero
        for _ in range(SSM_SEG - 1):
            fr = ends[2 * q] + pa * xr - pb * xi
            fi = ends[2 * q + 1] + pa * xi + pb * xr
            xr = jnp.where(keep, pltpu.roll(fr, shift, 0), 0.0)
            xi = jnp.where(keep, pltpu.roll(fi, shift, 0), 0.0)
        init += [xr, xi]
    lax.fori_loop(0, T, make_step(True), tuple(init), unroll=SSM_UNROLL)
    return init


def _lam_rows(are_ref, aim_ref, d, jb, npair):
    a, b = [], []
    for q in range(npair):
        j = jb * npair + q
        a.append(jnp.broadcast_to(are_ref[d, j:j + 1, :], (SSM_SEG, 128)))
        b.append(jnp.broadcast_to(aim_ref[d, j:j + 1, :], (SSM_SEG, 128)))
    return a, b


def _ssm_fwd(u_p, wb, wc, are, aim, dvec, cw=SSM_SW):
    L = u_p.shape[0]
    T = L // SSM_SEG
    RC = min(512, L)

    def body(u_ref, wb_ref, wc_ref, are_ref, aim_ref, d_ref, y_ref, x_scr):
        y_ref[...] = u_ref[...] * d_ref[...]
        for d in range(2):
            for jb in range(SSM_SW // cw):
                cols = slice(jb * cw, (jb + 1) * cw)

                def bu_chunk(c, _):
                    rows = pl.ds(pl.multiple_of(c * RC, RC), RC)
                    x_scr[rows, :] = jnp.dot(u_ref[rows, :].astype(BF16), wb_ref[d, :, cols], preferred_element_type=F32)
                    return 0
                lax.fori_loop(0, L // RC, bu_chunk, 0)
                a, b = _lam_rows(are_ref, aim_ref, d, jb, cw // 256)
                _seg_scan(x_scr, a, b, T, reverse=(d == 1))

                def y_chunk(c, _):
                    rows = pl.ds(pl.multiple_of(c * RC, RC), RC)
                    y_ref[rows, :] += jnp.dot(x_scr[rows, :].astype(BF16), wc_ref[d, cols, :], preferred_element_type=F32)
                    return 0
                lax.fori_loop(0, L // RC, y_chunk, 0)

    blk4 = lambda g: (g, 0, 0, 0)
    return pl.pallas_call(
        body, name="ssm_fwd",
        out_shape=jax.ShapeDtypeStruct((L, D_SSM), F32),
        grid=(SSM_NB,),
        in_specs=[pl.BlockSpec((L, SSM_BW), lambda g: (0, g)),
                  pl.BlockSpec((None, 2, SSM_BW, SSM_SW), blk4),
                  pl.BlockSpec((None, 2, SSM_SW, SSM_BW), blk4),
                  pl.BlockSpec((None, 2, 4, 128), blk4),
                  pl.BlockSpec((None, 2, 4, 128), blk4),
                  pl.BlockSpec((1, SSM_BW), lambda g: (0, g))],
        out_specs=pl.BlockSpec((L, SSM_BW), lambda g: (0, g)),
        scratch_shapes=[pltpu.VMEM((L, cw), F32)],
        compiler_params=_params(("parallel",)),
    )(u_p, wb, wc, are, aim, dvec)


def _ssm_bwd(u_p, dy_p, wb, wc, are, aim, dvec, cw=SSM_SW):
    L = u_p.shape[0]
    T = L // SSM_SEG
    RC = min(512, L)
    npair = cw // 256

    def lam_acc(acc, sb, xb):
        new = []
        for q in range(npair):
            sr, si = sb[:, q * 256:q * 256 + 128], sb[:, q * 256 + 128:q * 256 + 256]
            xr, xi = xb[:, q * 256:q * 256 + 128], xb[:, q * 256 + 128:q * 256 + 256]
            new += [acc[2 * q] + sr * xr + si * xi, acc[2 * q + 1] + si * xr - sr * xi]
        return tuple(new)

    def body(u_ref, dy_ref, wb_ref, wc_ref, are_ref, aim_ref, d_ref,
             du_ref, dwb_ref, dwc_ref, dare_ref, daim_ref, dd_ref, x_scr, s_scr):
        du_ref[...] = dy_ref[...] * d_ref[...]
        dd_ref[...] = _colsum(dy_ref[...] * u_ref[...])
        dwb_ref[...] = jnp.zeros_like(dwb_ref)
        dwc_ref[...] = jnp.zeros_like(dwc_ref)
        for d in range(2):
            rev = d == 1
            for jb in range(SSM_SW // cw):
                cols = slice(jb * cw, (jb + 1) * cw)

                def in_chunk(c, _):
                    rows = pl.ds(pl.multiple_of(c * RC, RC), RC)
                    x_scr[rows, :] = jnp.dot(u_ref[rows, :].astype(BF16), wb_ref[d, :, cols], preferred_element_type=F32)
                    s_scr[rows, :] = lax.dot_general(dy_ref[rows, :].astype(BF16), wc_ref[d, cols, :], _QK_T,
                                                     preferred_element_type=F32)
                    return 0
                lax.fori_loop(0, L // RC, in_chunk, 0)
                a, b = _lam_rows(are_ref, aim_ref, d, jb, npair)
                x_in = _seg_scan(x_scr, a, b, T, reverse=rev)
                _seg_scan(s_scr, a, [-v for v in b], T, reverse=not rev)

                def lam_step(t, acc):
                    lt = (T - 2 - t) if rev else (t + 1)
                    srow = pl.multiple_of(lt * SSM_SEG, SSM_SEG)
                    xrow = pl.multiple_of((lt + 1 if rev else lt - 1) * SSM_SEG, SSM_SEG)
                    return lam_acc(acc, s_scr[pl.ds(srow, SSM_SEG), :], x_scr[pl.ds(xrow, SSM_SEG), :])

                edge = pl.ds(((T - 1) if rev else 0) * SSM_SEG, SSM_SEG)
                acc0 = lam_acc((jnp.zeros((SSM_SEG, 128), F32),) * (2 * npair), s_scr[edge, :], jnp.concatenate(x_in, axis=1))
                acc = lax.fori_loop(0, T - 1, lam_step, acc0, unroll=SSM_UNROLL)
                for q in range(npair):
                    j = jb * npair + q
                    dare_ref[d, j:j + 1, :] = _colsum(acc[2 * q])
                    daim_ref[d, j:j + 1, :] = _colsum(acc[2 * q + 1])

                def out_chunk(c, _):
                    rows = pl.ds(pl.multiple_of(c * RC, RC), RC)
                    xs, ss = x_scr[rows, :].astype(BF16), s_scr[rows, :].astype(BF16)
                    uu, dd = u_ref[rows, :].astype(BF16), dy_ref[rows, :].astype(BF16)
                    dwc_ref[d, cols, :] += lax.dot_general(xs, dd, _TA, preferred_element_type=F32)
                    dwb_ref[d, :, cols] += lax.dot_general(uu, ss, _TA, preferred_element_type=F32)
                    du_ref[rows, :] += lax.dot_general(ss, wb_ref[d, :, cols], _QK_T, preferred_element_type=F32)
                    return 0
                lax.fori_loop(0, L // RC, out_chunk, 0)

    blk4 = lambda g: (g, 0, 0, 0)
    chan = pl.BlockSpec((L, SSM_BW), lambda g: (0, g))
    par_specs = [pl.BlockSpec((None, 2, SSM_BW, SSM_SW), blk4),
                 pl.BlockSpec((None, 2, SSM_SW, SSM_BW), blk4),
                 pl.BlockSpec((None, 2, 4, 128), blk4),
                 pl.BlockSpec((None, 2, 4, 128), blk4),
                 pl.BlockSpec((1, SSM_BW), lambda g: (0, g))]
    return pl.pallas_call(
        body, name="ssm_bwd",
        out_shape=[jax.ShapeDtypeStruct((L, D_SSM), F32),
                   jax.ShapeDtypeStruct((SSM_NB, 2, SSM_BW, SSM_SW), F32),
                   jax.ShapeDtypeStruct((SSM_NB, 2, SSM_SW, SSM_BW), F32),
                   jax.ShapeDtypeStruct((SSM_NB, 2, 4, 128), F32),
                   jax.ShapeDtypeStruct((SSM_NB, 2, 4, 128), F32),
                   jax.ShapeDtypeStruct((1, D_SSM), F32)],
        grid=(SSM_NB,),
        in_specs=[chan, chan] + par_specs,
        out_specs=[chan] + par_specs,
        scratch_shapes=[pltpu.VMEM((L, cw), F32), pltpu.VMEM((L, cw), F32)],
        compiler_params=_params(("parallel",)),
    )(u_p, dy_p, wb, wc, are, aim, dvec)


def _ssm_disc(a_re, a_im, log_dt, b_re, b_im):
    lam = lax.complex(jnp.minimum(a_re, -1e-4), a_im)
    dt = jnp.exp(log_dt)[..., None]
    lam_bar = jnp.exp(lam * dt)
    b_bar = ((lam_bar - 1.0) / lam)[..., None] * lax.complex(b_re, b_im)
    return jnp.real(lam_bar), jnp.imag(lam_bar), jnp.real(b_bar), jnp.imag(b_bar)


_EYE8 = np.eye(SSM_BLK, dtype=np.float32)


def _to_wb(bb_re, bb_im):
    bb = jnp.stack([bb_re, bb_im], axis=2).reshape(2, SSM_NB, SSM_BLK, 2, SSM_P, SSM_H)
    t = bb.transpose(1, 0, 5, 2, 3, 4)[:, :, None] * _EYE8[None, None, :, None, :, None, None]
    t = t.reshape(SSM_NB, 2, SSM_BLK, SSM_H, 4, 2, 2, SSM_P).transpose(0, 1, 2, 3, 4, 6, 5, 7)
    return t.reshape(SSM_NB, 2, SSM_BW, SSM_SW)


def _from_wb(dwb):
    t = dwb.reshape(SSM_NB, 2, SSM_BLK, SSM_H, 4, 2, 2, SSM_P).transpose(0, 1, 2, 3, 4, 6, 5, 7)
    t = t.reshape(SSM_NB, 2, SSM_BLK, SSM_H, SSM_BLK, 2, SSM_P)
    bb = jnp.sum(t * _EYE8[None, None, :, None, :, None, None], axis=2)
    bb = bb.transpose(1, 0, 3, 4, 5, 2).reshape(2, SSM_G, 2, SSM_P, SSM_H)
    return bb[:, :, 0], bb[:, :, 1]


def _to_wc(c_re, c_im):
    cc = jnp.stack([c_re, -c_im], axis=2).reshape(2, SSM_NB, SSM_BLK, 2, SSM_H, SSM_P)
    t = cc.transpose(1, 0, 2, 3, 5, 4)[:, :, :, :, :, None] * _EYE8.T[None, None, :, None, None, :, None]
    t = t.reshape(SSM_NB, 2, 4, 2, 2, SSM_P, SSM_BLK, SSM_H).transpose(0, 1, 2, 4, 3, 5, 6, 7)
    return t.reshape(SSM_NB, 2, SSM_SW, SSM_BW)


def _from_wc(dwc):
    t = dwc.reshape(SSM_NB, 2, 4, 2, 2, SSM_P, SSM_BLK, SSM_H).transpose(0, 1, 2, 4, 3, 5, 6, 7)
    t = t.reshape(SSM_NB, 2, SSM_BLK, 2, SSM_P, SSM_BLK, SSM_H)
    cc = jnp.sum(t * _EYE8.T[None, None, :, None, None, :, None], axis=5)
    cc = cc.transpose(1, 0, 2, 3, 5, 4).reshape(2, SSM_G, 2, SSM_H, SSM_P)
    return cc[:, :, 0], -cc[:, :, 1]


def _to_lam(v):
    return v.reshape(2, SSM_NB, 4, 128).transpose(1, 0, 2, 3)


def _from_lam(v):
    return v.transpose(1, 0, 2, 3).reshape(2, SSM_G, SSM_P)


_MESH = pl.DeviceIdType.MESH
_ANY = pl.BlockSpec(memory_space=pl.ANY)
_BIG = (("w_in", (D_MODEL, D_IN), 1, D_IN // N_CHIPS),
        ("w_glu", (D_SSM, 2 * D_SSM), 1, 2 * D_SSM // N_CHIPS),
        ("w_out", (D_ATTN + D_SSM, D_MODEL), 0, (D_ATTN + D_SSM) // N_CHIPS),
        ("w_ple_gate", (D_MODEL, D_MODEL), 0, D_MODEL // N_CHIPS),
        ("w_ple_proj", (PLE_DIM, D_MODEL), 1, D_MODEL // N_CHIPS))


def _place():
    x, y, c = lax.axis_index("x"), lax.axis_index("y"), lax.axis_index("c")
    return x, y, c, [(1 - x, y), (x, 1 - y), (1 - x, 1 - y)]


def _gather_weights(shards):
    nt = len(_BIG)

    def body(*refs):
        srcs, dsts, stage = refs[:nt], refs[nt:2 * nt], refs[2 * nt:3 * nt]
        send_sems, recv_sems, fwd_send_sems, fwd_recv_sems, loc_sems = refs[3 * nt:]
        x, y, c, chips = _place()
        k = 2 * x + y

        def shard_of(t, kk):
            _, _, axis, sz = _BIG[t]
            sl = pl.ds(pl.multiple_of(kk * sz, sz), sz)
            return dsts[t].at[:, sl] if axis == 1 else dsts[t].at[sl, :]

        def half_of(ref, t, cc):
            n = ref.shape[0] // 2
            return ref.at[pl.ds(pl.multiple_of(cc * n, n), n), :]

        def ici(j, t, kk):
            px, py = chips[j]
            return pltpu.make_async_remote_copy(src_ref=half_of(srcs[t], t, c), dst_ref=half_of(shard_of(t, kk), t, c),
                                                send_sem=send_sems.at[j, t], recv_sem=recv_sems.at[j, t],
                                                device_id=(px, py, c), device_id_type=_MESH)

        def forward(j, t, kk, cc):
            part = half_of(shard_of(t, kk), t, cc)
            return pltpu.make_async_remote_copy(src_ref=part, dst_ref=part, send_sem=fwd_send_sems.at[j, t],
                                                recv_sem=fwd_recv_sems.at[j, t], device_id=(x, y, 1 - c), device_id_type=_MESH)

        load = [pltpu.make_async_copy(srcs[t], stage[t], loc_sems.at[t]) for t in range(nt)]
        local = [pltpu.make_async_copy(stage[t], shard_of(t, k), loc_sems.at[t]) for t in range(nt)]
        sends = [ici(j, t, k) for t in range(nt) for j in range(3)]
        for cp in load + sends:
            cp.start()
        for t in range(nt):
            load[t].wait()
            local[t].start()
        forwards = []
        for t in range(nt):
            for j, (px, py) in enumerate(chips):
                ici(j, t, 2 * px + py).wait_recv()
                forwards.append(forward(j, t, 2 * px + py, c))
                forwards[-1].start()
        for t in range(nt):
            for j, (px, py) in enumerate(chips):
                forward(j, t, 2 * px + py, 1 - c).wait_recv()
        for cp in sends + forwards:
            cp.wait_send()
        for cp in local:
            cp.wait()

    sems = pltpu.SemaphoreType.DMA((3, nt))
    return pl.pallas_call(
        body, name="gather_weights",
        out_shape=[jax.ShapeDtypeStruct(shape, BF16) for _, shape, _, _ in _BIG],
        in_specs=[_ANY] * nt, out_specs=[_ANY] * nt,
        scratch_shapes=[pltpu.VMEM(s.shape, BF16) for s in shards] + [sems, sems, sems, sems, pltpu.SemaphoreType.DMA((nt,))],
        compiler_params=pltpu.CompilerParams(vmem_limit_bytes=VMEM_LIMIT_V7X),
    )(*shards)


SMALL_W = 1024
SMALL_ROWS = 72
N_SMALL = 8 * SMALL_ROWS * SMALL_W
_RED = tuple((shape, ax, (shape[0] // 2, sz) if ax == 1 else (sz // 2, shape[1]), BF16) for _, shape, ax, sz in _BIG) + \
    (((8 * SMALL_ROWS, SMALL_W), 0, (SMALL_ROWS, SMALL_W), F32),)
_RED_TR = 128


def _piece(ref, t, kk, cc):
    _, ax, (pr, pc), _ = _RED[t]
    if ax == 1:
        return ref.at[pl.ds(pl.multiple_of(cc * pr, pr), pr), pl.ds(pl.multiple_of(kk * pc, pc), pc)]
    return ref.at[pl.ds(pl.multiple_of((2 * kk + cc) * pr, pr), pr), :]


def _half_shape(t):
    shape, ax, (pr, pc), _ = _RED[t]
    return (pr, shape[1]) if ax == 1 else (N_CHIPS * pr, pc)


def _piece_in_half(ref, t, kk):
    _, ax, (pr, pc), _ = _RED[t]
    return ref.at[:, pl.ds(pl.multiple_of(kk * pc, pc), pc)] if ax == 1 else ref.at[pl.ds(pl.multiple_of(kk * pr, pr), pr), :]


def _grad_sibling_exchange(grads):
    n = len(_RED)
    n_dma = sum(1 if ax == 1 else N_CHIPS for _, ax, _, _ in _RED)

    def body(*refs):
        srcs, dsts, (send_sems, recv_sems) = refs[:n], refs[n:2 * n], refs[2 * n:]
        x, y, c, _ = _place()
        pairs = []
        for t, (_, ax, (pr, _), _) in enumerate(_RED):
            if ax == 1:
                pairs.append((srcs[t].at[pl.ds(pl.multiple_of((1 - c) * pr, pr), pr), :], dsts[t]))
            else:
                pairs += [(_piece(srcs[t], t, kk, 1 - c), _piece_in_half(dsts[t], t, kk)) for kk in range(N_CHIPS)]
        cps = [pltpu.make_async_remote_copy(src_ref=s, dst_ref=d, send_sem=send_sems.at[i], recv_sem=recv_sems.at[i],
                                            device_id=(x, y, 1 - c), device_id_type=_MESH) for i, (s, d) in enumerate(pairs)]
        for cp in cps:
            cp.start()
        for cp in cps:
            cp.wait()

    return pl.pallas_call(
        body, name="grad_sibling_exchange",
        out_shape=[jax.ShapeDtypeStruct(_half_shape(t), F32) for t in range(n)],
        in_specs=[_ANY] * n, out_specs=[_ANY] * n,
        scratch_shapes=[pltpu.SemaphoreType.DMA((n_dma,)), pltpu.SemaphoreType.DMA((n_dma,))],
    )(*grads)


def _chip_sum(t, g, rs, place):
    shape, ax, (pr, pc), dt = _RED[t]
    W = shape[1]
    tr = min(pr, _RED_TR)
    nb = pr // tr

    def body(place_ref, g_ref, rs_ref, o_ref):
        o_ref[...] = (g_ref[...] + rs_ref[...]).astype(o_ref.dtype)

    return pl.pallas_call(
        body, name="grad_chip_sum_%d" % t,
        out_shape=jax.ShapeDtypeStruct(rs.shape, dt),
        grid_spec=pltpu.PrefetchScalarGridSpec(
            num_scalar_prefetch=1, grid=(1 if ax == 1 else N_CHIPS, nb),
            in_specs=[pl.BlockSpec((tr, W), lambda kk, i, pr_: ((2 * kk + pr_[0]) * nb + i, 0)),
                      pl.BlockSpec((tr, W), lambda kk, i, pr_: (kk * nb + i, 0))],
            out_specs=pl.BlockSpec((tr, W), lambda kk, i, pr_: (kk * nb + i, 0))),
        compiler_params=_params(("parallel", "parallel")),
    )(place, g, rs)


def _grad_chip_exchange(sums):
    n = len(_RED)

    def body(*refs):
        srcs, dsts, (send_sems, recv_sems) = refs[:n], refs[n:2 * n], refs[2 * n:]
        x, y, c, chips = _place()
        cps = [pltpu.make_async_remote_copy(src_ref=_piece_in_half(srcs[t], t, 2 * px + py), dst_ref=dsts[t].at[j],
                                            send_sem=send_sems.at[j, t], recv_sem=recv_sems.at[j, t],
                                            device_id=(px, py, c), device_id_type=_MESH)
               for t in range(n) for j, (px, py) in enumerate(chips)]
        for cp in cps:
            cp.start()
        for cp in cps:
            cp.wait()

    return pl.pallas_call(
        body, name="grad_chip_exchange",
        out_shape=[jax.ShapeDtypeStruct((3,) + piece, dt) for _, _, piece, dt in _RED],
        in_specs=[_ANY] * n, out_specs=[_ANY] * n,
        scratch_shapes=[pltpu.SemaphoreType.DMA((3, n)), pltpu.SemaphoreType.DMA((3, n))],
    )(*sums)


def _total_sum(t, g, rs, rc, place):
    shape, ax, (pr, pc), _ = _RED[t]
    tr = min(pr, _RED_TR)
    nb = pr // tr
    small = t == len(_RED) - 1

    def body(place_ref, g_ref, rs_ref, rc_ref, o_ref):
        o_ref[...] = (g_ref[...] + rs_ref[...]) + rc_ref[0].astype(F32) + rc_ref[1].astype(F32) + rc_ref[2].astype(F32)

    if ax == 1:
        g_map = lambda i, pr_: (pr_[0] * nb + i, pr_[1])
        rs_map = lambda i, pr_: (i, pr_[1])
    else:
        g_map = lambda i, pr_: ((2 * pr_[1] + pr_[0]) * nb + i, 0)
        rs_map = lambda i, pr_: (pr_[1] * nb + i, 0)
    o_map = (lambda i, pr_: ((2 * pr_[1] + pr_[0]) * nb + i, 0)) if small else (lambda i, pr_: (pr_[0] * nb + i, 0))
    return pl.pallas_call(
        body, name="grad_total_sum_%d" % t,
        out_shape=jax.ShapeDtypeStruct(((8 if small else 2) * pr, pc), F32),
        grid_spec=pltpu.PrefetchScalarGridSpec(
            num_scalar_prefetch=1, grid=(nb,),
            in_specs=[pl.BlockSpec((tr, pc), g_map), pl.BlockSpec((tr, pc), rs_map),
                      pl.BlockSpec((3, tr, pc), lambda i, pr_: (0, i, 0))],
            out_specs=pl.BlockSpec((tr, pc), o_map)),
        compiler_params=_params(("parallel",)),
    )(place, g, rs, rc)


def _grad_final_exchange(totals):
    n = len(_RED)
    nb = n - 1

    def body(*refs):
        srcs, dsts, (send_sems, recv_sems) = refs[:n], refs[n:2 * n], refs[2 * n:]
        x, y, c, chips = _place()
        me = 4 * x + 2 * y + c
        others = [(x, y, 1 - c)] + [(px, py, cc) for (px, py) in chips for cc in (c, 1 - c)]

        def half(ref, t, cc):
            pr = _RED[t][2][0]
            return ref.at[pl.ds(pl.multiple_of(cc * pr, pr), pr), :]

        def eighth(ref, dev):
            return ref.at[pl.ds(pl.multiple_of(dev * SMALL_ROWS, SMALL_ROWS), SMALL_ROWS), :]

        def big_copy(t, cc):
            return pltpu.make_async_remote_copy(src_ref=half(srcs[t], t, cc), dst_ref=half(dsts[t], t, cc), send_sem=send_sems.at[t],
                                                recv_sem=recv_sems.at[t], device_id=others[0], device_id_type=_MESH)

        def small_copy(i, dev):
            return pltpu.make_async_remote_copy(src_ref=eighth(srcs[nb], dev), dst_ref=eighth(dsts[nb], dev),
                                                send_sem=send_sems.at[nb + i], recv_sem=recv_sems.at[nb + i],
                                                device_id=others[i], device_id_type=_MESH)

        sends = [big_copy(t, c) for t in range(nb)] + [small_copy(i, me) for i in range(7)]
        for cp in sends:
            cp.start()
        for t in range(nb):
            big_copy(t, 1 - c).wait_recv()
        for i, (px, py, pc) in enumerate(others):
            small_copy(i, 4 * px + 2 * py + pc).wait_recv()
        for cp in sends:
            cp.wait_send()

    return pl.pallas_call(
        body, name="grad_final_exchange",
        out_shape=[jax.ShapeDtypeStruct(a.shape, F32) for a in totals],
        in_specs=[_ANY] * n, out_specs=[_ANY] * n,
        input_output_aliases={t: t for t in range(n)},
        scratch_shapes=[pltpu.SemaphoreType.DMA((nb + 7,)), pltpu.SemaphoreType.DMA((nb + 7,))],
    )(*totals)


def _reduce_gradients(big_grads, small_flat):
    c = lax.axis_index("c")
    k = 2 * lax.axis_index("x") + lax.axis_index("y")
    place = jnp.stack([c, k]).astype(jnp.int32)
    grads = list(big_grads) + [small_flat.reshape(8 * SMALL_ROWS, SMALL_W)]
    n = len(grads)
    from_sibling = _grad_sibling_exchange(grads)
    chip_sums = [_chip_sum(t, grads[t], from_sibling[t], place) for t in range(n)]
    from_chips = _grad_chip_exchange(chip_sums)
    totals = [_total_sum(t, grads[t], from_sibling[t], from_chips[t], place) for t in range(n)]
    *shards, small = _grad_final_exchange(totals)
    return shards, small.reshape(-1)


def _adamw(w, g, m, v, name):
    def fn(w, g, m, v):
        m = ADAM_B1 * m + (1.0 - ADAM_B1) * g
        v = ADAM_B2 * v + (1.0 - ADAM_B2) * (g * g)
        m_hat = m / (1.0 - ADAM_B1 ** ADAM_STEP)
        v_hat = v / (1.0 - ADAM_B2 ** ADAM_STEP)
        return -ADAM_LR * (m_hat / (jnp.sqrt(v_hat) + ADAM_EPS) + ADAM_WD * w), m, v
    W = w.shape[1]
    return _rowwise(fn, [(a, 0, W) for a in (w, g, m, v)], [], [(W, F32)] * 3, tr=128, name=name)


def _chunks(arr, off, width, w=512):
    return [(arr, off + i * w, w) for i in range(width // w)]


def _cat(vs):
    return jnp.concatenate(vs, axis=1)


def _forward_backward(x, p_b, tgt, full, small):
    L = x.shape[0]
    w_in, w_glu, w_out, w_pg, w_pp = full
    row = lambda v: v.reshape(1, -1)
    g_mix, g_ple, g_fin = row(small["norm_mix"]), row(small["norm_ple"]), row(small["norm_final"])
    gq, gk, b_glu = row(small["q_norm"]), row(small["k_norm"]), row(small["b_glu"])
    cos, sin = _rope_tables(L)

    hn_b, = _rowwise(lambda x, g: x * _rms(x) * g, [(x, 0, D_MODEL)], [g_mix], [(D_MODEL, BF16)], name="norm_mix")
    z = _matmul(hn_b, w_in, name="mm_in")
    qr, kr, vb = _attn_prep(z, gq, gk, cos, sin)
    o, lse = _attn_fwd(qr, kr, vb)

    ssm_names = ("ssm_a_re", "ssm_a_im", "ssm_log_dt", "ssm_b_re", "ssm_b_im")
    (lre, lim, bre, bim), disc_vjp = jax.vjp(_ssm_disc, *[small[n][0] for n in ssm_names])
    wb, wc = _to_wb(bre, bim).astype(BF16), _to_wc(small["ssm_c_re"][0], small["ssm_c_im"][0]).astype(BF16)
    are, aim, dvec = _to_lam(lre), _to_lam(lim), row(small["ssm_d"])
    u_p = _seg_perm(z[:, Z_U:Z_U + D_SSM])
    y_s = _seg_unperm(_ssm_fwd(u_p, wb, wc, are, aim, dvec))
    ge_b, = _rowwise(_gelu, [(y_s, 0, D_SSM)], [], [(D_SSM, BF16)], name="gelu")
    glu = _matmul(ge_b, w_glu, name="mm_glu")

    def merge(ga0, ga1, a, b, gs0, gs1, o, bias):
        sa, _ = _silu_and_grad(_cat([ga0, ga1]))
        ss, _ = _silu_and_grad(_cat([gs0, gs1]))
        y2 = (a + bias[:, :D_SSM]) * _sig(b + bias[:, D_SSM:])
        return _cat([o * sa, y2 * ss])
    merge_rows = _chunks(z, Z_GA, D_ATTN) + [(glu, 0, D_SSM), (glu, D_SSM, D_SSM)] + _chunks(z, Z_GS, D_SSM) + [(o, 0, D_ATTN)]
    cat_b, = _rowwise(merge, merge_rows, [b_glu], [(D_MODEL, BF16)], name="merge")
    t_out = _matmul(cat_b, w_out, name="mm_out")

    def resid(x, t, g):
        h1 = x + t
        return h1, h1 * _rms(h1) * g
    h1, hp_b = _rowwise(resid, [(x, 0, D_MODEL), (t_out, 0, D_MODEL)], [g_ple], [(D_MODEL, F32), (D_MODEL, BF16)], name="resid_norm")
    gl = _matmul(hp_b, w_pg, name="mm_ple_gate")
    pp = _matmul(p_b, w_pp, name="mm_ple_proj")

    def head(h1, gl, pp, tgt, g):
        gate = _sig(gl)
        h2 = h1 + gate * pp
        r = _rms(h2)
        n = h2 * r
        err = n * g - tgt
        dy = err * (1.0 / D_MODEL)
        dn = dy * g
        dh2 = r * (dn - n * jnp.mean(dn * n, axis=-1, keepdims=True))
        dgate = dh2 * pp
        return dh2, dh2 * gate, dgate * gate * (1.0 - gate), _colsum(dy * n), _colsum(0.5 * err * err * (1.0 / D_MODEL))
    dh2, dpp_b, dgl_b, dg_fin, loss_cols = _rowwise(
        head, [(a, 0, D_MODEL) for a in (h1, gl, pp, tgt)], [g_fin],
        [(D_MODEL, F32), (D_MODEL, BF16), (D_MODEL, BF16)], [(1, D_MODEL), (1, D_MODEL)], name="loss_head")

    dw_pp = _matmul(p_b, dpp_b, ta=True, name="mm_d_w_ple_proj")
    dw_pg = _matmul(hp_b, dgl_b, ta=True, name="mm_d_w_ple_gate")
    dhp = _matmul(dgl_b, w_pg, tb=True, name="mm_d_hp")

    def resid_bwd(dhp, h1, dh2, g):
        dx, dg = _rms_bwd(h1, g, dhp)
        dh1 = dh2 + dx
        return dh1, dh1, _colsum(dg)
    dh1, dh1_b, dg_ple = _rowwise(resid_bwd, [(a, 0, D_MODEL) for a in (dhp, h1, dh2)], [g_ple],
                                  [(D_MODEL, F32), (D_MODEL, BF16)], [(1, D_MODEL)], name="resid_norm_bwd")
    dw_out = _matmul(cat_b, dh1_b, ta=True, name="mm_d_w_out")
    dcat = _matmul(dh1_b, w_out, tb=True, name="mm_d_cat")

    def merge_bwd(dya, dys, ga0, ga1, a, b, gs0, gs1, o, bias):
        ga, gs = _cat([ga0, ga1]), _cat([gs0, gs1])
        sa, dsa = _silu_and_grad(ga)
        ss, dss = _silu_and_grad(gs)
        a, sb = a + bias[:, :D_SSM], _sig(b + bias[:, D_SSM:])
        dy2 = dys * ss
        dglu = _cat([dy2 * sb, dy2 * a * sb * (1.0 - sb)])
        return dya * sa, dya * o * dsa, dys * (a * sb) * dss, dglu, _colsum(dglu)
    do_b, dga_b, dgs_b, dglu_b, db_glu = _rowwise(
        merge_bwd, [(dcat, 0, D_ATTN), (dcat, D_ATTN, D_SSM)] + merge_rows, [b_glu],
        [(D_ATTN, BF16), (D_ATTN, BF16), (D_SSM, BF16), (2 * D_SSM, BF16)], [(1, 2 * D_SSM)], name="merge_bwd")
    dw_glu = _matmul(ge_b, dglu_b, ta=True, name="mm_d_w_glu")
    dge = _matmul(dglu_b, w_glu, tb=True, name="mm_d_ge")
    dy_s, = _rowwise(lambda dge, y: dge * _gelu_grad(y), [(dge, 0, D_SSM), (y_s, 0, D_SSM)], [], [(D_SSM, F32)], name="gelu_bwd")
    du_p, dwb, dwc, dare, daim, d_ssm_d = _ssm_bwd(u_p, _seg_perm(dy_s), wb, wc, are, aim, dvec)
    db_re_bar, db_im_bar = _from_wb(dwb)
    dc_re, dc_im = _from_wc(dwc)
    da_re, da_im, dlog_dt, db_re, db_im = disc_vjp((_from_lam(dare), _from_lam(daim), db_re_bar, db_im_bar))

    dqr, dkr, dv = _attn_bwd(qr, kr, kr.T, vb, do_b, lse.reshape(N_HEADS, 1, L))
    dq_b, dk_b, dgq, dgk = _attn_prep_bwd(dqr, dkr, z, gq, gk, cos, sin)
    dz_b = _cat([dq_b, dk_b, dv.astype(BF16), dga_b, _seg_unperm(du_p).astype(BF16), dgs_b])
    dw_in = _matmul(hn_b, dz_b, ta=True, name="mm_d_w_in")
    dhn = _matmul(dz_b, w_in, tb=True, name="mm_d_hn")

    def norm_bwd(dhn, x, dh1, g):
        dx, dg = _rms_bwd(x, g, dhn)
        return dh1 + dx, _colsum(dg)
    grad_x, dg_mix = _rowwise(norm_bwd, [(a, 0, D_MODEL) for a in (dhn, x, dh1)], [g_mix], [(D_MODEL, F32)], [(1, D_MODEL)],
                              name="norm_mix_bwd")

    small_grads = {"norm_mix": dg_mix, "q_norm": dgq, "k_norm": dgk, "ssm_a_re": da_re, "ssm_a_im": da_im, "ssm_log_dt": dlog_dt,
                   "ssm_b_re": db_re, "ssm_b_im": db_im, "ssm_c_re": dc_re, "ssm_c_im": dc_im, "ssm_d": d_ssm_d,
                   "b_glu": db_glu, "norm_ple": dg_ple, "norm_final": dg_fin}
    return jnp.sum(loss_cols), grad_x, [dw_in, dw_glu, dw_out, dw_pg, dw_pp], small_grads


_SMALL = ("norm_mix", "q_norm", "k_norm", "ssm_a_re", "ssm_a_im", "ssm_log_dt", "ssm_b_re", "ssm_b_im", "ssm_c_re", "ssm_c_im",
          "ssm_d", "b_glu", "norm_ple", "norm_final")
_WEIGHTS = ("norm_mix", "w_in", "q_norm", "k_norm", "ssm_a_re", "ssm_a_im", "ssm_log_dt", "ssm_b_re", "ssm_b_im", "ssm_c_re",
            "ssm_c_im", "ssm_d", "w_glu", "b_glu", "w_out", "norm_ple", "w_ple_gate", "w_ple_proj", "norm_final")


def _flat_small(d):
    flat = jnp.concatenate([d[n].reshape(-1).astype(F32) for n in _SMALL])
    return jnp.pad(flat, (0, N_SMALL - flat.shape[0]))


def _split_small(flat, like):
    out, off = {}, 0
    for n in _SMALL:
        sz = math.prod(like[n].shape)
        out[n] = flat[off:off + sz].reshape(like[n].shape)
        off += sz
    return out


def kernel(x, p, norm_mix, w_in, q_norm, k_norm, ssm_a_re, ssm_a_im, ssm_log_dt, ssm_b_re, ssm_b_im, ssm_c_re, ssm_c_im, ssm_d, w_glu, b_glu, w_out, norm_ple, w_ple_gate, w_ple_proj, norm_final, loss_target, m_norm_mix, m_w_in, m_q_norm, m_k_norm, m_ssm_a_re, m_ssm_a_im, m_ssm_log_dt, m_ssm_b_re, m_ssm_b_im, m_ssm_c_re, m_ssm_c_im, m_ssm_d, m_w_glu, m_b_glu, m_w_out, m_norm_ple, m_w_ple_gate, m_w_ple_proj, m_norm_final, v_norm_mix, v_w_in, v_q_norm, v_k_norm, v_ssm_a_re, v_ssm_a_im, v_ssm_log_dt, v_ssm_b_re, v_ssm_b_im, v_ssm_c_re, v_ssm_c_im, v_ssm_d, v_w_glu, v_b_glu, v_w_out, v_norm_ple, v_w_ple_gate, v_w_ple_proj, v_norm_final):
    w = dict(norm_mix=norm_mix, w_in=w_in, q_norm=q_norm, k_norm=k_norm, ssm_a_re=ssm_a_re, ssm_a_im=ssm_a_im,
             ssm_log_dt=ssm_log_dt, ssm_b_re=ssm_b_re, ssm_b_im=ssm_b_im, ssm_c_re=ssm_c_re, ssm_c_im=ssm_c_im, ssm_d=ssm_d,
             w_glu=w_glu, b_glu=b_glu, w_out=w_out, norm_ple=norm_ple, w_ple_gate=w_ple_gate, w_ple_proj=w_ple_proj,
             norm_final=norm_final)
    m = dict(norm_mix=m_norm_mix, w_in=m_w_in, q_norm=m_q_norm, k_norm=m_k_norm, ssm_a_re=m_ssm_a_re, ssm_a_im=m_ssm_a_im,
             ssm_log_dt=m_ssm_log_dt, ssm_b_re=m_ssm_b_re, ssm_b_im=m_ssm_b_im, ssm_c_re=m_ssm_c_re, ssm_c_im=m_ssm_c_im,
             ssm_d=m_ssm_d, w_glu=m_w_glu, b_glu=m_b_glu, w_out=m_w_out, norm_ple=m_norm_ple, w_ple_gate=m_w_ple_gate,
             w_ple_proj=m_w_ple_proj, norm_final=m_norm_final)
    v = dict(norm_mix=v_norm_mix, w_in=v_w_in, q_norm=v_q_norm, k_norm=v_k_norm, ssm_a_re=v_ssm_a_re, ssm_a_im=v_ssm_a_im,
             ssm_log_dt=v_ssm_log_dt, ssm_b_re=v_ssm_b_re, ssm_b_im=v_ssm_b_im, ssm_c_re=v_ssm_c_re, ssm_c_im=v_ssm_c_im,
             ssm_d=v_ssm_d, w_glu=v_w_glu, b_glu=v_b_glu, w_out=v_w_out, norm_ple=v_norm_ple, w_ple_gate=v_w_ple_gate,
             w_ple_proj=v_w_ple_proj, norm_final=v_norm_final)
    big_names = [n for n, _, _, _ in _BIG]

    full = _gather_weights([w[n][0].astype(BF16) for n in big_names])
    small = {n: w[n] for n in _SMALL}
    loss_part, grad_x, big_grads, small_grads = _forward_backward(
        x[0], p[0, 0].astype(BF16), loss_target[0], full, small)
    loss = lax.psum(loss_part, ("x", "y", "c"))

    big_red, small_red = _reduce_gradients(big_grads, _flat_small(small_grads))
    grads = _split_small(small_red, w)
    delta, new_m, new_v = {}, {}, {}
    for n, g in zip(big_names, big_red):
        grads[n] = g[None]
        d_, m_, v_ = _adamw(w[n][0], g, m[n][0], v[n][0], "adamw_" + n)
        delta[n], new_m[n], new_v[n] = d_[None], m_[None], v_[None]
    d_, m_, v_ = _adamw(*[a.reshape(-1, SMALL_W) for a in (_flat_small(w), small_red, _flat_small(m), _flat_small(v))], "adamw_small")
    delta.update(_split_small(d_.reshape(-1), w))
    new_m.update(_split_small(m_.reshape(-1), w))
    new_v.update(_split_small(v_.reshape(-1), w))
    return (loss, grad_x[None], *[grads[n] for n in _WEIGHTS], *[delta[n] for n in _WEIGHTS],
            *[new_m[n] for n in _WEIGHTS], *[new_v[n] for n in _WEIGHTS])
```

```python
import functools
import math

import jax
import jax.numpy as jnp
import numpy as np
from jax import lax
from jax.experimental import pallas as pl
from jax.experimental.pallas import tpu as pltpu

D_MODEL = 2048
GRID_W = 64
PLE_DIM = 256
D_ATTN = 1024
N_HEADS = 8
N_KV = 2
HEAD_DIM = 128
ROPE_THETA = 10000.0
D_SSM = 1024
SSM_H = 16
SSM_G = 64
SSM_P = 64
D_KV = N_KV * HEAD_DIM
D_IN = 2 * D_ATTN + 2 * D_KV + 2 * D_SSM
EPS = 1e-6
Z_Q, Z_K, Z_V, Z_GA, Z_U, Z_GS = 0, 1024, 1280, 1536, 2560, 3584

ADAM_LR, ADAM_B1, ADAM_B2, ADAM_EPS, ADAM_WD, ADAM_STEP = 0.001, 0.9, 0.999, 1e-08, 0.01, 10

N_CHIPS = 4
VMEM_LIMIT_V7X = 56 * 1024 * 1024
F32 = jnp.float32
BF16 = jnp.bfloat16


def _params(sem, vmem=VMEM_LIMIT_V7X):
    return pltpu.CompilerParams(dimension_semantics=sem, vmem_limit_bytes=vmem)


def _matmul(a, b, *, ta=False, tb=False, out_dtype=F32, tm=1024, tn=512, name):
    M, K = (a.shape[1], a.shape[0]) if ta else a.shape
    N = b.shape[0] if tb else b.shape[1]
    tm, tn = min(tm, M), min(tn, N)
    assert M % tm == 0 and N % tn == 0, (name, M, N, K)
    dims = (((0 if ta else 1,), (1 if tb else 0,)), ((), ()))

    def body(a_ref, b_ref, o_ref):
        o_ref[...] = lax.dot_general(a_ref[...], b_ref[...], dims, preferred_element_type=F32).astype(o_ref.dtype)

    a_spec = pl.BlockSpec((K, tm), lambda i, j: (0, i)) if ta else pl.BlockSpec((tm, K), lambda i, j: (i, 0))
    b_spec = pl.BlockSpec((tn, K), lambda i, j: (j, 0)) if tb else pl.BlockSpec((K, tn), lambda i, j: (0, j))
    return pl.pallas_call(
        body, name=name,
        out_shape=jax.ShapeDtypeStruct((M, N), out_dtype),
        grid=(M // tm, N // tn),
        in_specs=[a_spec, b_spec],
        out_specs=pl.BlockSpec((tm, tn), lambda i, j: (i, j)),
        compiler_params=_params(("parallel", "parallel")),
    )(a, b)


def _rowwise(fn, rows, consts, outs, accs=(), *, tr=256, name):
    L = rows[0][0].shape[0]
    tr = math.gcd(tr, L)
    assert tr % 8 == 0 or tr == L, (name, L, tr)
    n_in, n_c, n_o, n_a = len(rows), len(consts), len(outs), len(accs)

    def body(*refs):
        ins = [r[...] for r in refs[:n_in + n_c]]
        res = fn(*ins)
        if not isinstance(res, (tuple, list)):
            res = (res,)
        o_refs = refs[n_in + n_c:n_in + n_c + n_o]
        a_refs = refs[n_in + n_c + n_o:]
        for r, v in zip(o_refs, res[:n_o]):
            r[...] = v.astype(r.dtype)
        if n_a:
            first = pl.program_id(0) == 0

            @pl.when(first)
            def _():
                for r, v in zip(a_refs, res[n_o:]):
                    r[...] = v.astype(F32)

            @pl.when(jnp.logical_not(first))
            def _():
                for r, v in zip(a_refs, res[n_o:]):
                    r[...] += v.astype(F32)

    in_specs = []
    for arr, off, w in rows:
        assert off % w == 0, (name, off, w)
        in_specs.append(pl.BlockSpec((tr, w), functools.partial(lambda i, c: (i, c), c=off // w)))
    for c in consts:
        in_specs.append(pl.BlockSpec(c.shape, lambda i: (0, 0)))
    out_shape = [jax.ShapeDtypeStruct((L, w), dt) for w, dt in outs] + [jax.ShapeDtypeStruct(s, F32) for s in accs]
    out_specs = [pl.BlockSpec((tr, w), lambda i: (i, 0)) for w, _ in outs] + [pl.BlockSpec(s, lambda i: (0, 0)) for s in accs]
    res = pl.pallas_call(
        body, name=name,
        out_shape=out_shape,
        grid=(L // tr,),
        in_specs=in_specs,
        out_specs=out_specs,
        compiler_params=_params(("arbitrary",) if n_a else ("parallel",)),
    )(*[r[0] for r in rows], *consts)
    return res


def _sig(x):
    return jax.nn.sigmoid(x)


def _silu_and_grad(x):
    s = _sig(x)
    return x * s, s * (1.0 + x * (1.0 - s))


_GELU_C = math.sqrt(2.0 / math.pi)


def _gelu(x):
    return 0.5 * x * (1.0 + jnp.tanh(_GELU_C * (x + 0.044715 * x * x * x)))


def _gelu_grad(x):
    t = jnp.tanh(_GELU_C * (x + 0.044715 * x * x * x))
    return 0.5 * (1.0 + t) + 0.5 * x * (1.0 - t * t) * _GELU_C * (1.0 + 3.0 * 0.044715 * x * x)


def _rms(x):
    return lax.rsqrt(jnp.mean(x * x, axis=-1, keepdims=True) + EPS)


def _rms_bwd(x, g, dy):
    r = _rms(x)
    n = x * r
    dn = dy * g
    return r * (dn - n * jnp.mean(dn * n, axis=-1, keepdims=True)), dy * n


def _colsum(v):
    return jnp.sum(v, axis=0, keepdims=True)


def _rope_partner(x):
    lane = lax.broadcasted_iota(jnp.int32, x.shape, x.ndim - 1)
    return jnp.where(lane % 64 < 32, pltpu.roll(x, 96, x.ndim - 1), pltpu.roll(x, 32, x.ndim - 1))


def _rope_tables(L):
    rows_n = L // GRID_W
    rows = jnp.repeat(jnp.arange(rows_n), GRID_W).astype(F32)
    cols = jnp.tile(jnp.arange(GRID_W), rows_n).astype(F32)
    n_freq = HEAD_DIM // 4
    inv_freq = ROPE_THETA ** (-jnp.arange(n_freq, dtype=F32) / n_freq)
    ar, ac = rows[:, None] * inv_freq[None, :], cols[:, None] * inv_freq[None, :]
    cos = jnp.concatenate([jnp.cos(ar), jnp.cos(ar), jnp.cos(ac), jnp.cos(ac)], axis=-1)
    sin = jnp.concatenate([-jnp.sin(ar), jnp.sin(ar), -jnp.sin(ac), jnp.sin(ac)], axis=-1)
    return cos, sin


def _heads(v):
    return [v[:, h * HEAD_DIM:(h + 1) * HEAD_DIM] for h in range(v.shape[1] // HEAD_DIM)]


def _attn_prep(z, q_norm, k_norm, cos, sin):
    def fn(q, k, v, cos, sin, gq, gk):
        def one(xh, g):
            xn = xh * _rms(xh) * g
            return xn * cos + _rope_partner(xn) * sin
        qr = jnp.concatenate([one(h, gq) for h in _heads(q)], axis=1)
        kr = jnp.concatenate([one(h, gk) for h in _heads(k)], axis=1)
        return qr, kr, v
    return _rowwise(fn, [(z, Z_Q, D_ATTN), (z, Z_K, D_KV), (z, Z_V, D_KV), (cos, 0, HEAD_DIM), (sin, 0, HEAD_DIM)],
                    [q_norm, k_norm], [(D_ATTN, BF16), (D_KV, BF16), (D_KV, BF16)], name="attn_prep")


def _attn_prep_bwd(dqr, dkr, z, q_norm, k_norm, cos, sin):
    def fn(dqr, dkr, q, k, cos, sin, gq, gk):
        def one(dyh, xh, g):
            dn = dyh * cos + _rope_partner(dyh * sin)
            return _rms_bwd(xh, g, dn)
        rq = [one(a, b, gq) for a, b in zip(_heads(dqr), _heads(q))]
        rk = [one(a, b, gk) for a, b in zip(_heads(dkr), _heads(k))]
        dq = jnp.concatenate([r[0] for r in rq], axis=1)
        dk = jnp.concatenate([r[0] for r in rk], axis=1)
        return dq, dk, _colsum(sum(r[1] for r in rq)), _colsum(sum(r[1] for r in rk))
    return _rowwise(fn, [(dqr, 0, D_ATTN), (dkr, 0, D_KV), (z, Z_Q, D_ATTN), (z, Z_K, D_KV), (cos, 0, HEAD_DIM), (sin, 0, HEAD_DIM)],
                    [q_norm, k_norm], [(D_ATTN, BF16), (D_KV, BF16)], [(1, HEAD_DIM), (1, HEAD_DIM)], name="attn_prep_bwd")


_QK_T = (((1,), (1,)), ((), ()))
_TA = (((0,), (0,)), ((), ()))
_REP = N_HEADS // N_KV


_EXP2_SCALE = HEAD_DIM ** -0.5 * math.log2(math.e)
ATTN_FWD_KEY_CHUNKS = 4
ATTN_BWD_KEY_CHUNKS = 8


def _attn_fwd(qr, kr, vb, *, tq=1024):
    L = qr.shape[0]
    tq = min(tq, L)
    kc = L // ATTN_FWD_KEY_CHUNKS

    def body(q_ref, k_ref, v_ref, o_ref, lse_ref):
        q = q_ref[...]
        m = jnp.full((tq, 1), -jnp.inf, F32)
        l = jnp.zeros((tq, 1), F32)
        o = jnp.zeros((tq, HEAD_DIM), F32)
        for c in range(ATTN_FWD_KEY_CHUNKS):
            ks = slice(c * kc, (c + 1) * kc)
            s = lax.dot_general(q, k_ref[ks, :], _QK_T, preferred_element_type=F32)
            m_new = jnp.maximum(m, jnp.max(s, axis=1, keepdims=True))
            a = jnp.exp2((m - m_new) * _EXP2_SCALE)
            p = jnp.exp2((s - m_new) * _EXP2_SCALE)
            l = a * l + jnp.sum(p, axis=1, keepdims=True)
            o = a * o + jnp.dot(p.astype(BF16), v_ref[ks, :], preferred_element_type=F32)
            m = m_new
        o_ref[...] = o * (1.0 / l)
        lse_ref[...] = m * _EXP2_SCALE + jnp.log2(l)

    kv = pl.BlockSpec((L, HEAD_DIM), lambda h, i: (0, h // _REP))
    return pl.pallas_call(
        body, name="attn_fwd",
        out_shape=[jax.ShapeDtypeStruct((L, D_ATTN), F32), jax.ShapeDtypeStruct((N_HEADS, L, 1), F32)],
        grid=(N_HEADS, L // tq),
        in_specs=[pl.BlockSpec((tq, HEAD_DIM), lambda h, i: (i, h)), kv, kv],
        out_specs=[pl.BlockSpec((tq, HEAD_DIM), lambda h, i: (i, h)),
                   pl.BlockSpec((None, tq, 1), lambda h, i: (h, i, 0))],
        compiler_params=_params(("parallel", "parallel")),
    )(qr, kr, vb)


def _attn_bwd(qr, kr, k_t, vb, do, lse, *, tq=512):
    L = qr.shape[0]
    tq = min(tq, L)
    scale = HEAD_DIM ** -0.5
    kc = L // ATTN_BWD_KEY_CHUNKS

    def body(q_ref, k_ref, kt_ref, v_ref, do_ref, lse_ref, dq_ref, dk_ref, dv_ref):
        @pl.when((pl.program_id(1) == 0) & (pl.program_id(2) == 0))
        def _():
            dk_ref[...] = jnp.zeros_like(dk_ref)
            dv_ref[...] = jnp.zeros_like(dv_ref)

        q, do, lse = q_ref[...], do_ref[...], lse_ref[...]
        keys = [slice(c * kc, (c + 1) * kc) for c in range(ATTN_BWD_KEY_CHUNKS)]
        ps, dps = [], []
        for ks in keys:
            st = lax.dot_general(k_ref[ks, :], q, _QK_T, preferred_element_type=F32)
            p = jnp.exp2(st * _EXP2_SCALE - lse)
            dv_ref[ks, :] += jnp.dot(p.astype(BF16), do, preferred_element_type=F32)
            ps.append(p)
            dps.append(lax.dot_general(v_ref[ks, :], do, _QK_T, preferred_element_type=F32))
        delta = sum(jnp.sum(p * dp, axis=0, keepdims=True) for p, dp in zip(ps, dps))
        dq_t = 0.0
        for ks, p, dp in zip(keys, ps, dps):
            ds = (p * (dp - delta) * scale).astype(BF16)
            dk_ref[ks, :] += jnp.dot(ds, q, preferred_element_type=F32)
            dq_t = dq_t + jnp.dot(kt_ref[:, ks], ds, preferred_element_type=F32)
        dq_ref[...] = dq_t.T

    head = lambda g, r, i: (i, g * _REP + r)
    kv = pl.BlockSpec((L, HEAD_DIM), lambda g, r, i: (0, g))
    return pl.pallas_call(
        body, name="attn_bwd",
        out_shape=[jax.ShapeDtypeStruct((L, D_ATTN), F32), jax.ShapeDtypeStruct((L, D_KV), F32), jax.ShapeDtypeStruct((L, D_KV), F32)],
        grid=(N_KV, _REP, L // tq),
        in_specs=[pl.BlockSpec((tq, HEAD_DIM), head), kv,
                  pl.BlockSpec((HEAD_DIM, L), lambda g, r, i: (g, 0)), kv,
                  pl.BlockSpec((tq, HEAD_DIM), head),
                  pl.BlockSpec((None, 1, tq), lambda g, r, i: (g * _REP + r, 0, i))],
        out_specs=[pl.BlockSpec((tq, HEAD_DIM), head), kv, kv],
        compiler_params=_params(("parallel", "arbitrary", "arbitrary")),
    )(qr, kr, k_t, vb, do, lse)


SSM_BLK = 8
SSM_NB = SSM_G // SSM_BLK
SSM_SEG = 8
SSM_UNROLL = 4


def _unrolled_loop(n, step, carry):
    u = SSM_UNROLL

    def trip(i, c):
        for j in range(u):
            c = step(i * u + j, c)
        return c
    carry = lax.fori_loop(0, n // u, trip, carry)
    for t in range(n - n % u, n):
        carry = step(jnp.int32(t), carry)
    return carry


def _cplx_pow2(a, b, n):
    for _ in range(int(math.log2(n))):
        a, b = a * a - b * b, 2.0 * a * b
    return a, b


def _seg_scan(ref, a, b, T, reverse, exclusive=False):
    npair = len(a)
    zero = jnp.zeros((SSM_SEG, 128), F32)

    def make_step(store):
        def step(t, carry):
            lt = (T - 1 - t) if reverse else t
            row = pl.multiple_of(lt * SSM_SEG, SSM_SEG)
            blk = ref[pl.ds(row, SSM_SEG), :]
            new = []
            for q in range(npair):
                re, im = carry[2 * q], carry[2 * q + 1]
                nre = a[q] * re - b[q] * im + blk[:, q * 256:q * 256 + 128]
                nim = a[q] * im + b[q] * re + blk[:, q * 256 + 128:q * 256 + 256]
                new += [nre, nim]
            if store:
                ref[pl.ds(row, SSM_SEG), :] = jnp.concatenate(carry if exclusive else new, axis=1)
            return tuple(new)
        return step

    ends = _unrolled_loop(T, make_step(False), (zero,) * (2 * npair))
    sub = lax.broadcasted_iota(jnp.int32, (SSM_SEG, 128), 0)
    keep = (sub != SSM_SEG - 1) if reverse else (sub != 0)
    shift = (SSM_SEG - 1) if reverse else 1
    init = []
    for q in range(npair):
        pa, pb = _cplx_pow2(a[q], b[q], T)
        xr, xi = zero, zero
        for _ in range(SSM_SEG - 1):
            fr = ends[2 * q] + pa * xr - pb * xi
            fi = ends[2 * q + 1] + pa * xi + pb * xr
            xr = jnp.where(keep, pltpu.roll(fr, shift, 0), 0.0)
            xi = jnp.where(keep, pltpu.roll(fi, shift, 0), 0.0)
        init += [xr, xi]
    _unrolled_loop(T, make_step(True), tuple(init))
    return init


def _to_lam(v):
    return v.reshape(2, SSM_NB, 4, 128).transpose(1, 0, 2, 3)


SSD_Q = 16
SSD_GW = SSD_Q * SSM_H
SSD_BW = SSM_BLK * SSD_GW


def _to_chunks(u):
    L = u.shape[0]
    tc = L // SSD_Q // SSM_SEG
    t = u.reshape(SSM_SEG, tc, SSD_Q, SSM_G, SSM_H).transpose(1, 0, 3, 2, 4)
    return t.reshape(L // SSD_Q, SSM_G * SSD_GW)


def _from_chunks(y):
    nc = y.shape[0]
    tc = nc // SSM_SEG
    t = y.reshape(tc, SSM_SEG, SSM_G, SSD_Q, SSM_H).transpose(1, 0, 3, 2, 4)
    return t.reshape(nc * SSD_Q, D_SSM)


def _ssd_mats(a_re, a_im, log_dt, b_re, b_im, c_re, c_im):
    G, P, H, Q = SSM_G, SSM_P, SSM_H, SSD_Q
    lam = lax.complex(jnp.minimum(a_re, -1e-4), a_im)
    lam_dt = lam * jnp.exp(log_dt)[..., None]
    pw = jnp.exp(lam_dt[..., None] * jnp.arange(Q + 1, dtype=F32))
    b_bar = ((pw[..., 1] - 1.0) / lam)[..., None] * lax.complex(b_re, b_im)
    cm = lax.complex(c_re, c_im)
    lag = jnp.real(jnp.einsum("dghp,dgpl,dgpk->dglhk", cm, pw[..., :Q], b_bar))
    idx = np.arange(Q)
    toep = np.stack([(idx[None, None, :] - idx[None, :, None] == idx[:, None, None]),
                     (idx[None, :, None] - idx[None, None, :] == idx[:, None, None])]).astype(np.float32)
    a_mat = jnp.einsum("dlji,dglhk->dgjkih", toep, lag).reshape(2, G, Q * H, Q * H)
    pw_s = jnp.stack([pw[0][..., Q - 1::-1][..., :Q], pw[1][..., :Q]])
    pw_r = jnp.stack([pw[0][..., 1:], pw[1][..., Q:0:-1]])
    s_c = jnp.einsum("dgpj,dgpk->dgjkp", pw_s, b_bar)
    r_c = jnp.einsum("dghp,dgpi->dgpih", cm, pw_r)
    s_g = jnp.stack([jnp.real(s_c), jnp.imag(s_c)], axis=4).reshape(2, G, Q * H, 2, P)
    r_g = jnp.stack([jnp.real(r_c), -jnp.imag(r_c)], axis=2).reshape(2, G, 2, P, Q * H)
    eye2 = np.eye(2, dtype=np.float32)
    s_p = s_g.reshape(2, G // 2, 2, Q * H, 2, P)[:, :, :, :, :, None, :] * eye2[None, None, :, None, None, :, None]
    s_p = s_p.reshape(2, G // 2, 2 * Q * H, 2 * 2 * P)
    r_p = r_g.reshape(2, G // 2, 2, 2, P, Q * H).transpose(0, 1, 3, 2, 4, 5)[:, :, :, :, :, None, :] * \
        eye2[None, None, None, :, None, :, None]
    r_p = r_p.reshape(2, G // 2, 2 * 2 * P, 2 * Q * H)
    blk = lambda m, per: m.reshape((2, SSM_NB, per) + m.shape[2:]).swapaxes(0, 1)
    lq = pw[..., Q]
    return (blk(a_mat, SSM_BLK), blk(s_p, SSM_BLK // 2), blk(r_p, SSM_BLK // 2),
            _to_lam(jnp.real(lq)), _to_lam(jnp.imag(lq)))


def _ssd_lam(re_ref, im_ref, d):
    a = [jnp.broadcast_to(re_ref[d, j:j + 1, :], (SSM_SEG, 128)) for j in range(SSM_BLK // 2)]
    b = [jnp.broadcast_to(im_ref[d, j:j + 1, :], (SSM_SEG, 128)) for j in range(SSM_BLK // 2)]
    return a, b


_SSD_PAR_SPECS = [pl.BlockSpec((None, 2, SSM_BLK, SSD_GW, SSD_GW), lambda g: (g, 0, 0, 0, 0)),
                  pl.BlockSpec((None, 2, SSM_BLK // 2, 2 * SSD_GW, 256), lambda g: (g, 0, 0, 0, 0)),
                  pl.BlockSpec((None, 2, SSM_BLK // 2, 256, 2 * SSD_GW), lambda g: (g, 0, 0, 0, 0)),
                  pl.BlockSpec((None, 2, SSM_BLK // 2, 128), lambda g: (g, 0, 0, 0)),
                  pl.BlockSpec((None, 2, SSM_BLK // 2, 128), lambda g: (g, 0, 0, 0))]


def _ssd_fwd(u_c, a_m, s_m, r_m, lq_re, lq_im, d_c):
    nc = u_c.shape[0]
    tc = nc // SSM_SEG
    npair = SSM_BLK // 2

    def body(u_ref, a_ref, s_ref, r_ref, lre_ref, lim_ref, d_ref, y_ref, x_scr):
        u = u_ref[...]
        ub = u.astype(BF16)
        y_ref[...] = u * d_ref[...]
        for d in range(2):
            for j in range(npair):
                x_scr[:, j * 256:(j + 1) * 256] = jnp.dot(ub[:, j * 512:(j + 1) * 512], s_ref[d, j], preferred_element_type=F32)
            a, b = _ssd_lam(lre_ref, lim_ref, d)
            _seg_scan(x_scr, a, b, tc, reverse=(d == 1), exclusive=True)
            for g in range(SSM_BLK):
                cols = slice(g * SSD_GW, (g + 1) * SSD_GW)
                y_ref[:, cols] += jnp.dot(ub[:, cols], a_ref[d, g], preferred_element_type=F32)
            for j in range(npair):
                y_ref[:, j * 512:(j + 1) * 512] += jnp.dot(x_scr[:, j * 256:(j + 1) * 256].astype(BF16), r_ref[d, j],
                                                           preferred_element_type=F32)

    chan = pl.BlockSpec((nc, SSD_BW), lambda g: (0, g))
    return pl.pallas_call(
        body, name="ssd_fwd",
        out_shape=jax.ShapeDtypeStruct(u_c.shape, F32),
        grid=(SSM_NB,),
        in_specs=[chan] + _SSD_PAR_SPECS + [pl.BlockSpec((1, SSD_BW), lambda g: (0, g))],
        out_specs=chan,
        scratch_shapes=[pltpu.VMEM((nc, npair * 256), F32)],
        compiler_params=_params(("parallel",)),
    )(u_c, a_m, s_m, r_m, lq_re, lq_im, d_c)


def _ssd_bwd(u_c, dy_c, a_m, s_m, r_m, lq_re, lq_im, d_c):
    nc = u_c.shape[0]
    tc = nc // SSM_SEG
    npair = SSM_BLK // 2

    def body(u_ref, dy_ref, a_ref, s_ref, r_ref, lre_ref, lim_ref, d_ref,
             du_ref, da_ref, ds_ref, dr_ref, dlre_ref, dlim_ref, dd_ref, x_scr, g_scr):
        u, dy = u_ref[...], dy_ref[...]
        ub, dyb = u.astype(BF16), dy.astype(BF16)
        du_ref[...] = dy * d_ref[...]
        dd_ref[...] = _colsum(dy * u)
        for d in range(2):
            rev = d == 1
            for j in range(npair):
                x_scr[:, j * 256:(j + 1) * 256] = jnp.dot(ub[:, j * 512:(j + 1) * 512], s_ref[d, j], preferred_element_type=F32)
                g_scr[:, j * 256:(j + 1) * 256] = lax.dot_general(dyb[:, j * 512:(j + 1) * 512], r_ref[d, j], _QK_T,
                                                                  preferred_element_type=F32)
            a, b = _ssd_lam(lre_ref, lim_ref, d)
            _seg_scan(x_scr, a, b, tc, reverse=rev, exclusive=True)
            _seg_scan(g_scr, a, [-v for v in b], tc, reverse=not rev, exclusive=True)
            xin, sx = x_scr[...], g_scr[...]
            xb, sb = xin.astype(BF16), sx.astype(BF16)
            for j in range(npair):
                st = slice(j * 256, (j + 1) * 256)
                sr, si, xr, xi = sx[:, j * 256:j * 256 + 128], sx[:, j * 256 + 128:(j + 1) * 256], \
                    xin[:, j * 256:j * 256 + 128], xin[:, j * 256 + 128:(j + 1) * 256]
                dlre_ref[d, j:j + 1, :] = _colsum(sr * xr + si * xi)
                dlim_ref[d, j:j + 1, :] = _colsum(si * xr - sr * xi)
                io = slice(j * 512, (j + 1) * 512)
                dr_ref[d, j] = lax.dot_general(xb[:, st], dyb[:, io], _TA, preferred_element_type=F32)
                ds_ref[d, j] = lax.dot_general(ub[:, io], sb[:, st], _TA, preferred_element_type=F32)
                du_ref[:, io] += lax.dot_general(sb[:, st], s_ref[d, j], _QK_T, preferred_element_type=F32)
            for g in range(SSM_BLK):
                cols = slice(g * SSD_GW, (g + 1) * SSD_GW)
                da_ref[d, g] = lax.dot_general(ub[:, cols], dyb[:, cols], _TA, preferred_element_type=F32)
                du_ref[:, cols] += lax.dot_general(dyb[:, cols], a_ref[d, g], _QK_T, preferred_element_type=F32)

    chan = pl.BlockSpec((nc, SSD_BW), lambda g: (0, g))
    col = pl.BlockSpec((1, SSD_BW), lambda g: (0, g))
    return pl.pallas_call(
        body, name="ssd_bwd",
        out_shape=[jax.ShapeDtypeStruct(u_c.shape, F32)] + [jax.ShapeDtypeStruct(m.shape, F32) for m in (a_m, s_m, r_m, lq_re, lq_im)] +
                  [jax.ShapeDtypeStruct(d_c.shape, F32)],
        grid=(SSM_NB,),
        in_specs=[chan, chan] + _SSD_PAR_SPECS + [col],
        out_specs=[chan] + _SSD_PAR_SPECS + [col],
        scratch_shapes=[pltpu.VMEM((nc, npair * 256), F32), pltpu.VMEM((nc, npair * 256), F32)],
        compiler_params=_params(("parallel",)),
    )(u_c, dy_c, a_m, s_m, r_m, lq_re, lq_im, d_c)


_MESH = pl.DeviceIdType.MESH
_ANY = pl.BlockSpec(memory_space=pl.ANY)
_BIG = (("w_in", (D_MODEL, D_IN), 1, D_IN // N_CHIPS),
        ("w_glu", (D_SSM, 2 * D_SSM), 1, 2 * D_SSM // N_CHIPS),
        ("w_out", (D_ATTN + D_SSM, D_MODEL), 0, (D_ATTN + D_SSM) // N_CHIPS),
        ("w_ple_gate", (D_MODEL, D_MODEL), 0, D_MODEL // N_CHIPS),
        ("w_ple_proj", (PLE_DIM, D_MODEL), 1, D_MODEL // N_CHIPS))


def _place():
    x, y, c = lax.axis_index("x"), lax.axis_index("y"), lax.axis_index("c")
    return x, y, c, [(1 - x, y), (x, 1 - y), (1 - x, 1 - y)]


def _gather_weights(shards):
    nt = len(_BIG)

    def body(*refs):
        srcs, dsts, stage = refs[:nt], refs[nt:2 * nt], refs[2 * nt:3 * nt]
        send_sems, recv_sems, fwd_send_sems, fwd_recv_sems, loc_sems = refs[3 * nt:]
        x, y, c, chips = _place()
        k = 2 * x + y

        def shard_of(t, kk):
            _, _, axis, sz = _BIG[t]
            sl = pl.ds(pl.multiple_of(kk * sz, sz), sz)
            return dsts[t].at[:, sl] if axis == 1 else dsts[t].at[sl, :]

        def half_of(ref, t, cc):
            n = ref.shape[0] // 2
            return ref.at[pl.ds(pl.multiple_of(cc * n, n), n), :]

        def ici(j, t, kk):
            px, py = chips[j]
            return pltpu.make_async_remote_copy(src_ref=half_of(srcs[t], t, c), dst_ref=half_of(shard_of(t, kk), t, c),
                                                send_sem=send_sems.at[j, t], recv_sem=recv_sems.at[j, t],
                                                device_id=(px, py, c), device_id_type=_MESH)

        def forward(j, t, kk, cc):
            part = half_of(shard_of(t, kk), t, cc)
            return pltpu.make_async_remote_copy(src_ref=part, dst_ref=part, send_sem=fwd_send_sems.at[j, t],
                                                recv_sem=fwd_recv_sems.at[j, t], device_id=(x, y, 1 - c), device_id_type=_MESH)

        load = [pltpu.make_async_copy(srcs[t], stage[t], loc_sems.at[t]) for t in range(nt)]
        local = [pltpu.make_async_copy(stage[t], shard_of(t, k), loc_sems.at[t]) for t in range(nt)]
        sends = [ici(j, t, k) for t in range(nt) for j in range(3)]
        for cp in load + sends:
            cp.start()
        for t in range(nt):
            load[t].wait()
            local[t].start()
        forwards = []
        for t in range(nt):
            for j, (px, py) in enumerate(chips):
                ici(j, t, 2 * px + py).wait_recv()
                forwards.append(forward(j, t, 2 * px + py, c))
                forwards[-1].start()
        for t in range(nt):
            for j, (px, py) in enumerate(chips):
                forward(j, t, 2 * px + py, 1 - c).wait_recv()
        for cp in sends + forwards:
            cp.wait_send()
        for cp in local:
            cp.wait()

    sems = pltpu.SemaphoreType.DMA((3, nt))
    return pl.pallas_call(
        body, name="gather_weights",
        out_shape=[jax.ShapeDtypeStruct(shape, BF16) for _, shape, _, _ in _BIG],
        in_specs=[_ANY] * nt, out_specs=[_ANY] * nt,
        scratch_shapes=[pltpu.VMEM(s.shape, BF16) for s in shards] + [sems, sems, sems, sems, pltpu.SemaphoreType.DMA((nt,))],
        compiler_params=pltpu.CompilerParams(vmem_limit_bytes=VMEM_LIMIT_V7X),
    )(*shards)


SMALL_W = 1024
SMALL_ROWS = 72
N_SMALL = 8 * SMALL_ROWS * SMALL_W
_RED = tuple((shape, ax, (shape[0] // 2, sz) if ax == 1 else (sz // 2, shape[1]), BF16) for _, shape, ax, sz in _BIG) + \
    (((8 * SMALL_ROWS, SMALL_W), 0, (SMALL_ROWS, SMALL_W), F32),)
_RED_TR = 128


def _piece(ref, t, kk, cc):
    _, ax, (pr, pc), _ = _RED[t]
    if ax == 1:
        return ref.at[pl.ds(pl.multiple_of(cc * pr, pr), pr), pl.ds(pl.multiple_of(kk * pc, pc), pc)]
    return ref.at[pl.ds(pl.multiple_of((2 * kk + cc) * pr, pr), pr), :]


def _half_shape(t):
    shape, ax, (pr, pc), _ = _RED[t]
    return (pr, shape[1]) if ax == 1 else (N_CHIPS * pr, pc)


def _piece_in_half(ref, t, kk):
    _, ax, (pr, pc), _ = _RED[t]
    return ref.at[:, pl.ds(pl.multiple_of(kk * pc, pc), pc)] if ax == 1 else ref.at[pl.ds(pl.multiple_of(kk * pr, pr), pr), :]


def _grad_sibling_exchange(grads):
    n = len(_RED)
    n_dma = sum(1 if ax == 1 else N_CHIPS for _, ax, _, _ in _RED)

    def body(*refs):
        srcs, dsts, (send_sems, recv_sems) = refs[:n], refs[n:2 * n], refs[2 * n:]
        x, y, c, _ = _place()
        pairs = []
        for t, (_, ax, (pr, _), _) in enumerate(_RED):
            if ax == 1:
                pairs.append((srcs[t].at[pl.ds(pl.multiple_of((1 - c) * pr, pr), pr), :], dsts[t]))
            else:
                pairs += [(_piece(srcs[t], t, kk, 1 - c), _piece_in_half(dsts[t], t, kk)) for kk in range(N_CHIPS)]
        cps = [pltpu.make_async_remote_copy(src_ref=s, dst_ref=d, send_sem=send_sems.at[i], recv_sem=recv_sems.at[i],
                                            device_id=(x, y, 1 - c), device_id_type=_MESH) for i, (s, d) in enumerate(pairs)]
        for cp in cps:
            cp.start()
        for cp in cps:
            cp.wait()

    return pl.pallas_call(
        body, name="grad_sibling_exchange",
        out_shape=[jax.ShapeDtypeStruct(_half_shape(t), F32) for t in range(n)],
        in_specs=[_ANY] * n, out_specs=[_ANY] * n,
        scratch_shapes=[pltpu.SemaphoreType.DMA((n_dma,)), pltpu.SemaphoreType.DMA((n_dma,))],
    )(*grads)


def _chip_sum(t, g, rs, place):
    shape, ax, (pr, pc), dt = _RED[t]
    W = shape[1]
    tr = min(pr, _RED_TR)
    nb = pr // tr

    def body(place_ref, g_ref, rs_ref, o_ref):
        o_ref[...] = (g_ref[...] + rs_ref[...]).astype(o_ref.dtype)

    return pl.pallas_call(
        body, name="grad_chip_sum_%d" % t,
        out_shape=jax.ShapeDtypeStruct(rs.shape, dt),
        grid_spec=pltpu.PrefetchScalarGridSpec(
            num_scalar_prefetch=1, grid=(1 if ax == 1 else N_CHIPS, nb),
            in_specs=[pl.BlockSpec((tr, W), lambda kk, i, pr_: ((2 * kk + pr_[0]) * nb + i, 0)),
                      pl.BlockSpec((tr, W), lambda kk, i, pr_: (kk * nb + i, 0))],
            out_specs=pl.BlockSpec((tr, W), lambda kk, i, pr_: (kk * nb + i, 0))),
        compiler_params=_params(("parallel", "parallel")),
    )(place, g, rs)


def _grad_chip_exchange(sums):
    n = len(_RED)

    def body(*refs):
        srcs, dsts, (send_sems, recv_sems) = refs[:n], refs[n:2 * n], refs[2 * n:]
        x, y, c, chips = _place()
        cps = [pltpu.make_async_remote_copy(src_ref=_piece_in_half(srcs[t], t, 2 * px + py), dst_ref=dsts[t].at[j],
                                            send_sem=send_sems.at[j, t], recv_sem=recv_sems.at[j, t],
                                            device_id=(px, py, c), device_id_type=_MESH)
               for t in range(n) for j, (px, py) in enumerate(chips)]
        for cp in cps:
            cp.start()
        for cp in cps:
            cp.wait()

    return pl.pallas_call(
        body, name="grad_chip_exchange",
        out_shape=[jax.ShapeDtypeStruct((3,) + piece, dt) for _, _, piece, dt in _RED],
        in_specs=[_ANY] * n, out_specs=[_ANY] * n,
        scratch_shapes=[pltpu.SemaphoreType.DMA((3, n)), pltpu.SemaphoreType.DMA((3, n))],
    )(*sums)


def _total_sum(t, g, rs, rc, place):
    shape, ax, (pr, pc), _ = _RED[t]
    tr = min(pr, _RED_TR)
    nb = pr // tr
    small = t == len(_RED) - 1

    def body(place_ref, g_ref, rs_ref, rc_ref, o_ref):
        o_ref[...] = (g_ref[...] + rs_ref[...]) + rc_ref[0].astype(F32) + rc_ref[1].astype(F32) + rc_ref[2].astype(F32)

    if ax == 1:
        g_map = lambda i, pr_: (pr_[0] * nb + i, pr_[1])
        rs_map = lambda i, pr_: (i, pr_[1])
    else:
        g_map = lambda i, pr_: ((2 * pr_[1] + pr_[0]) * nb + i, 0)
        rs_map = lambda i, pr_: (pr_[1] * nb + i, 0)
    o_map = (lambda i, pr_: ((2 * pr_[1] + pr_[0]) * nb + i, 0)) if small else (lambda i, pr_: (pr_[0] * nb + i, 0))
    return pl.pallas_call(
        body, name="grad_total_sum_%d" % t,
        out_shape=jax.ShapeDtypeStruct(((8 if small else 2) * pr, pc), F32),
        grid_spec=pltpu.PrefetchScalarGridSpec(
            num_scalar_prefetch=1, grid=(nb,),
            in_specs=[pl.BlockSpec((tr, pc), g_map), pl.BlockSpec((tr, pc), rs_map),
                      pl.BlockSpec((3, tr, pc), lambda i, pr_: (0, i, 0))],
            out_specs=pl.BlockSpec((tr, pc), o_map)),
        compiler_params=_params(("parallel",)),
    )(place, g, rs, rc)


def _grad_final_exchange(totals):
    n = len(_RED)
    nb = n - 1

    def body(*refs):
        srcs, dsts, (send_sems, recv_sems) = refs[:n], refs[n:2 * n], refs[2 * n:]
        x, y, c, chips = _place()
        me = 4 * x + 2 * y + c
        others = [(x, y, 1 - c)] + [(px, py, cc) for (px, py) in chips for cc in (c, 1 - c)]

        def half(ref, t, cc):
            pr = _RED[t][2][0]
            return ref.at[pl.ds(pl.multiple_of(cc * pr, pr), pr), :]

        def eighth(ref, dev):
            return ref.at[pl.ds(pl.multiple_of(dev * SMALL_ROWS, SMALL_ROWS), SMALL_ROWS), :]

        def big_copy(t, cc):
            return pltpu.make_async_remote_copy(src_ref=half(srcs[t], t, cc), dst_ref=half(dsts[t], t, cc), send_sem=send_sems.at[t],
                                                recv_sem=recv_sems.at[t], device_id=others[0], device_id_type=_MESH)

        def small_copy(i, dev):
            return pltpu.make_async_remote_copy(src_ref=eighth(srcs[nb], dev), dst_ref=eighth(dsts[nb], dev),
                                                send_sem=send_sems.at[nb + i], recv_sem=recv_sems.at[nb + i],
                                                device_id=others[i], device_id_type=_MESH)

        sends = [big_copy(t, c) for t in range(nb)] + [small_copy(i, me) for i in range(7)]
        for cp in sends:
            cp.start()
        for t in range(nb):
            big_copy(t, 1 - c).wait_recv()
        for i, (px, py, pc) in enumerate(others):
            small_copy(i, 4 * px + 2 * py + pc).wait_recv()
        for cp in sends:
            cp.wait_send()

    return pl.pallas_call(
        body, name="grad_final_exchange",
        out_shape=[jax.ShapeDtypeStruct(a.shape, F32) for a in totals],
        in_specs=[_ANY] * n, out_specs=[_ANY] * n,
        input_output_aliases={t: t for t in range(n)},
        scratch_shapes=[pltpu.SemaphoreType.DMA((nb + 7,)), pltpu.SemaphoreType.DMA((nb + 7,))],
    )(*totals)


def _reduce_gradients(big_grads, small_flat):
    c = lax.axis_index("c")
    k = 2 * lax.axis_index("x") + lax.axis_index("y")
    place = jnp.stack([c, k]).astype(jnp.int32)
    grads = list(big_grads) + [small_flat.reshape(8 * SMALL_ROWS, SMALL_W)]
    n = len(grads)
    from_sibling = _grad_sibling_exchange(grads)
    chip_sums = [_chip_sum(t, grads[t], from_sibling[t], place) for t in range(n)]
    from_chips = _grad_chip_exchange(chip_sums)
    totals = [_total_sum(t, grads[t], from_sibling[t], from_chips[t], place) for t in range(n)]
    *shards, small = _grad_final_exchange(totals)
    return shards, small.reshape(-1)


def _adamw(w, g, m, v, name):
    def fn(w, g, m, v):
        m = ADAM_B1 * m + (1.0 - ADAM_B1) * g
        v = ADAM_B2 * v + (1.0 - ADAM_B2) * (g * g)
        m_hat = m / (1.0 - ADAM_B1 ** ADAM_STEP)
        v_hat = v / (1.0 - ADAM_B2 ** ADAM_STEP)
        return -ADAM_LR * (m_hat / (jnp.sqrt(v_hat) + ADAM_EPS) + ADAM_WD * w), m, v
    W = w.shape[1]
    return _rowwise(fn, [(a, 0, W) for a in (w, g, m, v)], [], [(W, F32)] * 3, tr=128, name=name)


def _chunks(arr, off, width, w=512):
    return [(arr, off + i * w, w) for i in range(width // w)]


def _cat(vs):
    return jnp.concatenate(vs, axis=1)


def _forward_backward(x, p_b, tgt, full, small):
    L = x.shape[0]
    w_in, w_glu, w_out, w_pg, w_pp = full
    row = lambda v: v.reshape(1, -1)
    g_mix, g_ple, g_fin = row(small["norm_mix"]), row(small["norm_ple"]), row(small["norm_final"])
    gq, gk, b_glu = row(small["q_norm"]), row(small["k_norm"]), row(small["b_glu"])
    cos, sin = _rope_tables(L)

    hn_b, = _rowwise(lambda x, g: x * _rms(x) * g, [(x, 0, D_MODEL)], [g_mix], [(D_MODEL, BF16)], name="norm_mix")
    z = _matmul(hn_b, w_in, name="mm_in")
    qr, kr, vb = _attn_prep(z, gq, gk, cos, sin)
    o, lse = _attn_fwd(qr, kr, vb)

    ssm_names = ("ssm_a_re", "ssm_a_im", "ssm_log_dt", "ssm_b_re", "ssm_b_im", "ssm_c_re", "ssm_c_im")
    (a_f, s_f, r_f, lq_re, lq_im), mats_vjp = jax.vjp(_ssd_mats, *[small[n][0] for n in ssm_names])
    ssd = (a_f.astype(BF16), s_f.astype(BF16), r_f.astype(BF16), lq_re, lq_im,
           jnp.tile(small["ssm_d"].reshape(SSM_G, 1, SSM_H), (1, SSD_Q, 1)).reshape(1, -1))
    u_c = _to_chunks(z[:, Z_U:Z_U + D_SSM])
    y_s = _from_chunks(_ssd_fwd(u_c, *ssd))
    ge_b, = _rowwise(_gelu, [(y_s, 0, D_SSM)], [], [(D_SSM, BF16)], name="gelu")
    glu = _matmul(ge_b, w_glu, name="mm_glu")

    def merge(ga0, ga1, a, b, gs0, gs1, o, bias):
        sa, _ = _silu_and_grad(_cat([ga0, ga1]))
        ss, _ = _silu_and_grad(_cat([gs0, gs1]))
        y2 = (a + bias[:, :D_SSM]) * _sig(b + bias[:, D_SSM:])
        return _cat([o * sa, y2 * ss])
    merge_rows = _chunks(z, Z_GA, D_ATTN) + [(glu, 0, D_SSM), (glu, D_SSM, D_SSM)] + _chunks(z, Z_GS, D_SSM) + [(o, 0, D_ATTN)]
    cat_b, = _rowwise(merge, merge_rows, [b_glu], [(D_MODEL, BF16)], name="merge")
    t_out = _matmul(cat_b, w_out, name="mm_out")

    def resid(x, t, g):
        h1 = x + t
        return h1, h1 * _rms(h1) * g
    h1, hp_b = _rowwise(resid, [(x, 0, D_MODEL), (t_out, 0, D_MODEL)], [g_ple], [(D_MODEL, F32), (D_MODEL, BF16)], name="resid_norm")
    gl = _matmul(hp_b, w_pg, name="mm_ple_gate")
    pp = _matmul(p_b, w_pp, name="mm_ple_proj")

    def head(h1, gl, pp, tgt, g):
        gate = _sig(gl)
        h2 = h1 + gate * pp
        r = _rms(h2)
        n = h2 * r
        err = n * g - tgt
        dy = err * (1.0 / D_MODEL)
        dn = dy * g
        dh2 = r * (dn - n * jnp.mean(dn * n, axis=-1, keepdims=True))
        dgate = dh2 * pp
        return dh2, dh2 * gate, dgate * gate * (1.0 - gate), _colsum(dy * n), _colsum(0.5 * err * err * (1.0 / D_MODEL))
    dh2, dpp_b, dgl_b, dg_fin, loss_cols = _rowwise(
        head, [(a, 0, D_MODEL) for a in (h1, gl, pp, tgt)], [g_fin],
        [(D_MODEL, F32), (D_MODEL, BF16), (D_MODEL, BF16)], [(1, D_MODEL), (1, D_MODEL)], name="loss_head")

    dw_pp = _matmul(p_b, dpp_b, ta=True, name="mm_d_w_ple_proj")
    dw_pg = _matmul(hp_b, dgl_b, ta=True, name="mm_d_w_ple_gate")
    dhp = _matmul(dgl_b, w_pg, tb=True, name="mm_d_hp")

    def resid_bwd(dhp, h1, dh2, g):
        dx, dg = _rms_bwd(h1, g, dhp)
        dh1 = dh2 + dx
        return dh1, dh1, _colsum(dg)
    dh1, dh1_b, dg_ple = _rowwise(resid_bwd, [(a, 0, D_MODEL) for a in (dhp, h1, dh2)], [g_ple],
                                  [(D_MODEL, F32), (D_MODEL, BF16)], [(1, D_MODEL)], name="resid_norm_bwd")
    dw_out = _matmul(cat_b, dh1_b, ta=True, name="mm_d_w_out")
    dcat = _matmul(dh1_b, w_out, tb=True, name="mm_d_cat")

    def merge_bwd(dya, dys, ga0, ga1, a, b, gs0, gs1, o, bias):
        ga, gs = _cat([ga0, ga1]), _cat([gs0, gs1])
        sa, dsa = _silu_and_grad(ga)
        ss, dss = _silu_and_grad(gs)
        a, sb = a + bias[:, :D_SSM], _sig(b + bias[:, D_SSM:])
        dy2 = dys * ss
        dglu = _cat([dy2 * sb, dy2 * a * sb * (1.0 - sb)])
        return dya * sa, dya * o * dsa, dys * (a * sb) * dss, dglu, _colsum(dglu)
    do_b, dga_b, dgs_b, dglu_b, db_glu = _rowwise(
        merge_bwd, [(dcat, 0, D_ATTN), (dcat, D_ATTN, D_SSM)] + merge_rows, [b_glu],
        [(D_ATTN, BF16), (D_ATTN, BF16), (D_SSM, BF16), (2 * D_SSM, BF16)], [(1, 2 * D_SSM)], name="merge_bwd")
    dw_glu = _matmul(ge_b, dglu_b, ta=True, name="mm_d_w_glu")
    dge = _matmul(dglu_b, w_glu, tb=True, name="mm_d_ge")
    dy_s, = _rowwise(lambda dge, y: dge * _gelu_grad(y), [(dge, 0, D_SSM), (y_s, 0, D_SSM)], [], [(D_SSM, F32)], name="gelu_bwd")
    du_c, d_a, d_s, d_r, d_lq_re, d_lq_im, d_dc = _ssd_bwd(u_c, _to_chunks(dy_s), *ssd)
    da_re, da_im, dlog_dt, db_re, db_im, dc_re, dc_im = mats_vjp((d_a, d_s, d_r, d_lq_re, d_lq_im))
    d_ssm_d = d_dc.reshape(SSM_G, SSD_Q, SSM_H).sum(axis=1).reshape(1, -1)

    dqr, dkr, dv = _attn_bwd(qr, kr, kr.T, vb, do_b, lse.reshape(N_HEADS, 1, L))
    dq_b, dk_b, dgq, dgk = _attn_prep_bwd(dqr, dkr, z, gq, gk, cos, sin)
    dz_b = _cat([dq_b, dk_b, dv.astype(BF16), dga_b, _from_chunks(du_c).astype(BF16), dgs_b])
    dw_in = _matmul(hn_b, dz_b, ta=True, name="mm_d_w_in")
    dhn = _matmul(dz_b, w_in, tb=True, name="mm_d_hn")

    def norm_bwd(dhn, x, dh1, g):
        dx, dg = _rms_bwd(x, g, dhn)
        return dh1 + dx, _colsum(dg)
    grad_x, dg_mix = _rowwise(norm_bwd, [(a, 0, D_MODEL) for a in (dhn, x, dh1)], [g_mix], [(D_MODEL, F32)], [(1, D_MODEL)],
                              name="norm_mix_bwd")

    small_grads = {"norm_mix": dg_mix, "q_norm": dgq, "k_norm": dgk, "ssm_a_re": da_re, "ssm_a_im": da_im, "ssm_log_dt": dlog_dt,
                   "ssm_b_re": db_re, "ssm_b_im": db_im, "ssm_c_re": dc_re, "ssm_c_im": dc_im, "ssm_d": d_ssm_d,
                   "b_glu": db_glu, "norm_ple": dg_ple, "norm_final": dg_fin}
    return jnp.sum(loss_cols), grad_x, [dw_in, dw_glu, dw_out, dw_pg, dw_pp], small_grads


_SMALL = ("norm_mix", "q_norm", "k_norm", "ssm_a_re", "ssm_a_im", "ssm_log_dt", "ssm_b_re", "ssm_b_im", "ssm_c_re", "ssm_c_im",
          "ssm_d", "b_glu", "norm_ple", "norm_final")
_WEIGHTS = ("norm_mix", "w_in", "q_norm", "k_norm", "ssm_a_re", "ssm_a_im", "ssm_log_dt", "ssm_b_re", "ssm_b_im", "ssm_c_re",
            "ssm_c_im", "ssm_d", "w_glu", "b_glu", "w_out", "norm_ple", "w_ple_gate", "w_ple_proj", "norm_final")


def _flat_small(d):
    flat = jnp.concatenate([d[n].reshape(-1).astype(F32) for n in _SMALL])
    return jnp.pad(flat, (0, N_SMALL - flat.shape[0]))


def _split_small(flat, like):
    out, off = {}, 0
    for n in _SMALL:
        sz = math.prod(like[n].shape)
        out[n] = flat[off:off + sz].reshape(like[n].shape)
        off += sz
    return out


def kernel(x, p, norm_mix, w_in, q_norm, k_norm, ssm_a_re, ssm_a_im, ssm_log_dt, ssm_b_re, ssm_b_im, ssm_c_re, ssm_c_im, ssm_d, w_glu, b_glu, w_out, norm_ple, w_ple_gate, w_ple_proj, norm_final, loss_target, m_norm_mix, m_w_in, m_q_norm, m_k_norm, m_ssm_a_re, m_ssm_a_im, m_ssm_log_dt, m_ssm_b_re, m_ssm_b_im, m_ssm_c_re, m_ssm_c_im, m_ssm_d, m_w_glu, m_b_glu, m_w_out, m_norm_ple, m_w_ple_gate, m_w_ple_proj, m_norm_final, v_norm_mix, v_w_in, v_q_norm, v_k_norm, v_ssm_a_re, v_ssm_a_im, v_ssm_log_dt, v_ssm_b_re, v_ssm_b_im, v_ssm_c_re, v_ssm_c_im, v_ssm_d, v_w_glu, v_b_glu, v_w_out, v_norm_ple, v_w_ple_gate, v_w_ple_proj, v_norm_final):
    w = dict(norm_mix=norm_mix, w_in=w_in, q_norm=q_norm, k_norm=k_norm, ssm_a_re=ssm_a_re, ssm_a_im=ssm_a_im,
             ssm_log_dt=ssm_log_dt, ssm_b_re=ssm_b_re, ssm_b_im=ssm_b_im, ssm_c_re=ssm_c_re, ssm_c_im=ssm_c_im, ssm_d=ssm_d,
             w_glu=w_glu, b_glu=b_glu, w_out=w_out, norm_ple=norm_ple, w_ple_gate=w_ple_gate, w_ple_proj=w_ple_proj,
             norm_final=norm_final)
    m = dict(norm_mix=m_norm_mix, w_in=m_w_in, q_norm=m_q_norm, k_norm=m_k_norm, ssm_a_re=m_ssm_a_re, ssm_a_im=m_ssm_a_im,
             ssm_log_dt=m_ssm_log_dt, ssm_b_re=m_ssm_b_re, ssm_b_im=m_ssm_b_im, ssm_c_re=m_ssm_c_re, ssm_c_im=m_ssm_c_im,
             ssm_d=m_ssm_d, w_glu=m_w_glu, b_glu=m_b_glu, w_out=m_w_out, norm_ple=m_norm_ple, w_ple_gate=m_w_ple_gate,
             w_ple_proj=m_w_ple_proj, norm_final=m_norm_final)
    v = dict(norm_mix=v_norm_mix, w_in=v_w_in, q_norm=v_q_norm, k_norm=v_k_norm, ssm_a_re=v_ssm_a_re, ssm_a_im=v_ssm_a_im,
             ssm_log_dt=v_ssm_log_dt, ssm_b_re=v_ssm_b_re, ssm_b_im=v_ssm_b_im, ssm_c_re=v_ssm_c_re, ssm_c_im=v_ssm_c_im,
             ssm_d=v_ssm_d, w_glu=v_w_glu, b_glu=v_b_glu, w_out=v_w_out, norm_ple=v_norm_ple, w_ple_gate=v_w_ple_gate,
             w_ple_proj=v_w_ple_proj, norm_final=v_norm_final)
    big_names = [n for n, _, _, _ in _BIG]

    full = _gather_weights([w[n][0].astype(BF16) for n in big_names])
    small = {n: w[n] for n in _SMALL}
    loss_part, grad_x, big_grads, small_grads = _forward_backward(
        x[0], p[0, 0].astype(BF16), loss_target[0], full, small)
    loss = lax.psum(loss_part, ("x", "y", "c"))

    big_red, small_red = _reduce_gradients(big_grads, _flat_small(small_grads))
    grads = _split_small(small_red, w)
    delta, new_m, new_v = {}, {}, {}
    for n, g in zip(big_names, big_red):
        grads[n] = g[None]
        d_, m_, v_ = _adamw(w[n][0], g, m[n][0], v[n][0], "adamw_" + n)
        delta[n], new_m[n], new_v[n] = d_[None], m_[None], v_[None]
    d_, m_, v_ = _adamw(*[a.reshape(-1, SMALL_W) for a in (_flat_small(w), small_red, _flat_small(m), _flat_small(v))], "adamw_small")
    delta.update(_split_small(d_.reshape(-1), w))
    new_m.update(_split_small(m_.reshape(-1), w))
    new_v.update(_split_small(v_.reshape(-1), w))
    return (loss, grad_x[None], *[grads[n] for n in _WEIGHTS], *[delta[n] for n in _WEIGHTS],
            *[new_m[n] for n in _WEIGHTS], *[new_v[n] for n in _WEIGHTS])
```

```python
import functools
import math

import jax
import jax.numpy as jnp
import numpy as np
from jax import lax
from jax.experimental import pallas as pl
from jax.experimental.pallas import tpu as pltpu

D_MODEL = 2048
GRID_W = 64
PLE_DIM = 256
D_ATTN = 1024
N_HEADS = 8
N_KV = 2
HEAD_DIM = 128
ROPE_THETA = 10000.0
D_SSM = 1024
SSM_H = 16
SSM_G = 64
SSM_P = 64
D_KV = N_KV * HEAD_DIM
D_IN = 2 * D_ATTN + 2 * D_KV + 2 * D_SSM
EPS = 1e-6
Z_Q, Z_K, Z_V, Z_GA, Z_U, Z_GS = 0, 1024, 1280, 1536, 2560, 3584

ADAM_LR, ADAM_B1, ADAM_B2, ADAM_EPS, ADAM_WD, ADAM_STEP = 0.001, 0.9, 0.999, 1e-08, 0.01, 10

N_CHIPS = 4
VMEM_LIMIT_V7X = 56 * 1024 * 1024
F32 = jnp.float32
BF16 = jnp.bfloat16


def _params(sem, vmem=VMEM_LIMIT_V7X):
    return pltpu.CompilerParams(dimension_semantics=sem, vmem_limit_bytes=vmem)


def _matmul(a, b, *, ta=False, tb=False, out_dtype=F32, tm=1024, tn=512, name):
    M, K = (a.shape[1], a.shape[0]) if ta else a.shape
    N = b.shape[0] if tb else b.shape[1]
    tm, tn = min(tm, M), min(tn, N)
    assert M % tm == 0 and N % tn == 0, (name, M, N, K)
    dims = (((0 if ta else 1,), (1 if tb else 0,)), ((), ()))

    def body(a_ref, b_ref, o_ref):
        o_ref[...] = lax.dot_general(a_ref[...], b_ref[...], dims, preferred_element_type=F32).astype(o_ref.dtype)

    a_spec = pl.BlockSpec((K, tm), lambda i, j: (0, i)) if ta else pl.BlockSpec((tm, K), lambda i, j: (i, 0))
    b_spec = pl.BlockSpec((tn, K), lambda i, j: (j, 0)) if tb else pl.BlockSpec((K, tn), lambda i, j: (0, j))
    return pl.pallas_call(
        body, name=name,
        out_shape=jax.ShapeDtypeStruct((M, N), out_dtype),
        grid=(M // tm, N // tn),
        in_specs=[a_spec, b_spec],
        out_specs=pl.BlockSpec((tm, tn), lambda i, j: (i, j)),
        compiler_params=_params(("parallel", "parallel")),
    )(a, b)


def _rowwise(fn, rows, consts, outs, accs=(), *, tr=256, name):
    L = rows[0][0].shape[0]
    tr = math.gcd(tr, L)
    assert tr % 8 == 0 or tr == L, (name, L, tr)
    n_in, n_c, n_o, n_a = len(rows), len(consts), len(outs), len(accs)

    def body(*refs):
        ins = [r[...] for r in refs[:n_in + n_c]]
        res = fn(*ins)
        if not isinstance(res, (tuple, list)):
            res = (res,)
        o_refs = refs[n_in + n_c:n_in + n_c + n_o]
        a_refs = refs[n_in + n_c + n_o:]
        for r, v in zip(o_refs, res[:n_o]):
            r[...] = v.astype(r.dtype)
        if n_a:
            first = pl.program_id(0) == 0

            @pl.when(first)
            def _():
                for r, v in zip(a_refs, res[n_o:]):
                    r[...] = v.astype(F32)

            @pl.when(jnp.logical_not(first))
            def _():
                for r, v in zip(a_refs, res[n_o:]):
                    r[...] += v.astype(F32)

    in_specs = []
    for arr, off, w in rows:
        assert off % w == 0, (name, off, w)
        in_specs.append(pl.BlockSpec((tr, w), functools.partial(lambda i, c: (i, c), c=off // w)))
    for c in consts:
        in_specs.append(pl.BlockSpec(c.shape, lambda i: (0, 0)))
    out_shape = [jax.ShapeDtypeStruct((L, w), dt) for w, dt in outs] + [jax.ShapeDtypeStruct(s, F32) for s in accs]
    out_specs = [pl.BlockSpec((tr, w), lambda i: (i, 0)) for w, _ in outs] + [pl.BlockSpec(s, lambda i: (0, 0)) for s in accs]
    res = pl.pallas_call(
        body, name=name,
        out_shape=out_shape,
        grid=(L // tr,),
        in_specs=in_specs,
        out_specs=out_specs,
        compiler_params=_params(("arbitrary",) if n_a else ("parallel",)),
    )(*[r[0] for r in rows], *consts)
    return res


def _sig(x):
    return jax.nn.sigmoid(x)


def _silu_and_grad(x):
    s = _sig(x)
    return x * s, s * (1.0 + x * (1.0 - s))


_GELU_C = math.sqrt(2.0 / math.pi)


def _gelu(x):
    return 0.5 * x * (1.0 + jnp.tanh(_GELU_C * (x + 0.044715 * x * x * x)))


def _gelu_grad(x):
    t = jnp.tanh(_GELU_C * (x + 0.044715 * x * x * x))
    return 0.5 * (1.0 + t) + 0.5 * x * (1.0 - t * t) * _GELU_C * (1.0 + 3.0 * 0.044715 * x * x)


def _rms(x):
    return lax.rsqrt(jnp.mean(x * x, axis=-1, keepdims=True) + EPS)


def _rms_bwd(x, g, dy):
    r = _rms(x)
    n = x * r
    dn = dy * g
    return r * (dn - n * jnp.mean(dn * n, axis=-1, keepdims=True)), dy * n


def _colsum(v):
    return jnp.sum(v, axis=0, keepdims=True)


def _rope_partner(x):
    lane = lax.broadcasted_iota(jnp.int32, x.shape, x.ndim - 1)
    return jnp.where(lane % 64 < 32, pltpu.roll(x, 96, x.ndim - 1), pltpu.roll(x, 32, x.ndim - 1))


def _rope_tables(L):
    rows_n = L // GRID_W
    rows = jnp.repeat(jnp.arange(rows_n), GRID_W).astype(F32)
    cols = jnp.tile(jnp.arange(GRID_W), rows_n).astype(F32)
    n_freq = HEAD_DIM // 4
    inv_freq = ROPE_THETA ** (-jnp.arange(n_freq, dtype=F32) / n_freq)
    ar, ac = rows[:, None] * inv_freq[None, :], cols[:, None] * inv_freq[None, :]
    cos = jnp.concatenate([jnp.cos(ar), jnp.cos(ar), jnp.cos(ac), jnp.cos(ac)], axis=-1)
    sin = jnp.concatenate([-jnp.sin(ar), jnp.sin(ar), -jnp.sin(ac), jnp.sin(ac)], axis=-1)
    return cos, sin


def _heads(v):
    return [v[:, h * HEAD_DIM:(h + 1) * HEAD_DIM] for h in range(v.shape[1] // HEAD_DIM)]


def _attn_prep(z, q_norm, k_norm, cos, sin):
    def fn(q, k, v, cos, sin, gq, gk):
        def one(xh, g):
            xn = xh * _rms(xh) * g
            return xn * cos + _rope_partner(xn) * sin
        qr = jnp.concatenate([one(h, gq) for h in _heads(q)], axis=1)
        kr = jnp.concatenate([one(h, gk) for h in _heads(k)], axis=1)
        return qr, kr, v
    return _rowwise(fn, [(z, Z_Q, D_ATTN), (z, Z_K, D_KV), (z, Z_V, D_KV), (cos, 0, HEAD_DIM), (sin, 0, HEAD_DIM)],
                    [q_norm, k_norm], [(D_ATTN, BF16), (D_KV, BF16), (D_KV, BF16)], name="attn_prep")


def _attn_prep_bwd(dqr, dkr, z, q_norm, k_norm, cos, sin):
    def fn(dqr, dkr, q, k, cos, sin, gq, gk):
        def one(dyh, xh, g):
            dn = dyh * cos + _rope_partner(dyh * sin)
            return _rms_bwd(xh, g, dn)
        rq = [one(a, b, gq) for a, b in zip(_heads(dqr), _heads(q))]
        rk = [one(a, b, gk) for a, b in zip(_heads(dkr), _heads(k))]
        dq = jnp.concatenate([r[0] for r in rq], axis=1)
        dk = jnp.concatenate([r[0] for r in rk], axis=1)
        return dq, dk, _colsum(sum(r[1] for r in rq)), _colsum(sum(r[1] for r in rk))
    return _rowwise(fn, [(dqr, 0, D_ATTN), (dkr, 0, D_KV), (z, Z_Q, D_ATTN), (z, Z_K, D_KV), (cos, 0, HEAD_DIM), (sin, 0, HEAD_DIM)],
                    [q_norm, k_norm], [(D_ATTN, BF16), (D_KV, BF16)], [(1, HEAD_DIM), (1, HEAD_DIM)], name="attn_prep_bwd")


_QK_T = (((1,), (1,)), ((), ()))
_TA = (((0,), (0,)), ((), ()))
_REP = N_HEADS // N_KV


_EXP2_SCALE = HEAD_DIM ** -0.5 * math.log2(math.e)
ATTN_FWD_KEY_CHUNKS = 4
ATTN_BWD_KEY_CHUNKS = 8


def _attn_fwd(qr, kr, vb, *, tq=1024):
    L = qr.shape[0]
    tq = min(tq, L)
    kc = L // ATTN_FWD_KEY_CHUNKS

    def body(q_ref, k_ref, v_ref, o_ref, lse_ref):
        q = q_ref[...]
        m = jnp.full((tq, 1), -jnp.inf, F32)
        l = jnp.zeros((tq, 1), F32)
        o = jnp.zeros((tq, HEAD_DIM), F32)
        for c in range(ATTN_FWD_KEY_CHUNKS):
            ks = slice(c * kc, (c + 1) * kc)
            s = lax.dot_general(q, k_ref[ks, :], _QK_T, preferred_element_type=F32)
            m_new = jnp.maximum(m, jnp.max(s, axis=1, keepdims=True))
            a = jnp.exp2((m - m_new) * _EXP2_SCALE)
            p = jnp.exp2((s - m_new) * _EXP2_SCALE)
            l = a * l + jnp.sum(p, axis=1, keepdims=True)
            o = a * o + jnp.dot(p.astype(BF16), v_ref[ks, :], preferred_element_type=F32)
            m = m_new
        o_ref[...] = o * (1.0 / l)
        lse_ref[...] = m * _EXP2_SCALE + jnp.log2(l)

    kv = pl.BlockSpec((L, HEAD_DIM), lambda h, i: (0, h // _REP))
    return pl.pallas_call(
        body, name="attn_fwd",
        out_shape=[jax.ShapeDtypeStruct((L, D_ATTN), F32), jax.ShapeDtypeStruct((N_HEADS, L, 1), F32)],
        grid=(N_HEADS, L // tq),
        in_specs=[pl.BlockSpec((tq, HEAD_DIM), lambda h, i: (i, h)), kv, kv],
        out_specs=[pl.BlockSpec((tq, HEAD_DIM), lambda h, i: (i, h)),
                   pl.BlockSpec((None, tq, 1), lambda h, i: (h, i, 0))],
        compiler_params=_params(("parallel", "parallel")),
    )(qr, kr, vb)


def _attn_bwd(qr, kr, k_t, vb, do, lse, *, tq=512):
    L = qr.shape[0]
    tq = min(tq, L)
    scale = HEAD_DIM ** -0.5
    kc = L // ATTN_BWD_KEY_CHUNKS

    def body(q_ref, k_ref, kt_ref, v_ref, do_ref, lse_ref, dq_ref, dk_ref, dv_ref):
        @pl.when((pl.program_id(1) == 0) & (pl.program_id(2) == 0))
        def _():
            dk_ref[...] = jnp.zeros_like(dk_ref)
            dv_ref[...] = jnp.zeros_like(dv_ref)

        q, do, lse = q_ref[...], do_ref[...], lse_ref[...]
        keys = [slice(c * kc, (c + 1) * kc) for c in range(ATTN_BWD_KEY_CHUNKS)]
        ps, dps = [], []
        for ks in keys:
            st = lax.dot_general(k_ref[ks, :], q, _QK_T, preferred_element_type=F32)
            p = jnp.exp2(st * _EXP2_SCALE - lse)
            dv_ref[ks, :] += jnp.dot(p.astype(BF16), do, preferred_element_type=F32)
            ps.append(p)
            dps.append(lax.dot_general(v_ref[ks, :], do, _QK_T, preferred_element_type=F32))
        delta = sum(jnp.sum(p * dp, axis=0, keepdims=True) for p, dp in zip(ps, dps))
        dq_t = 0.0
        for ks, p, dp in zip(keys, ps, dps):
            ds = (p * (dp - delta) * scale).astype(BF16)
            dk_ref[ks, :] += jnp.dot(ds, q, preferred_element_type=F32)
            dq_t = dq_t + jnp.dot(kt_ref[:, ks], ds, preferred_element_type=F32)
        dq_ref[...] = dq_t.T

    head = lambda g, r, i: (i, g * _REP + r)
    kv = pl.BlockSpec((L, HEAD_DIM), lambda g, r, i: (0, g))
    return pl.pallas_call(
        body, name="attn_bwd",
        out_shape=[jax.ShapeDtypeStruct((L, D_ATTN), F32), jax.ShapeDtypeStruct((L, D_KV), F32), jax.ShapeDtypeStruct((L, D_KV), F32)],
        grid=(N_KV, _REP, L // tq),
        in_specs=[pl.BlockSpec((tq, HEAD_DIM), head), kv,
                  pl.BlockSpec((HEAD_DIM, L), lambda g, r, i: (g, 0)), kv,
                  pl.BlockSpec((tq, HEAD_DIM), head),
                  pl.BlockSpec((None, 1, tq), lambda g, r, i: (g * _REP + r, 0, i))],
        out_specs=[pl.BlockSpec((tq, HEAD_DIM), head), kv, kv],
        compiler_params=_params(("parallel", "arbitrary", "arbitrary")),
    )(qr, kr, k_t, vb, do, lse)


SSM_BLK = 8
SSM_NB = SSM_G // SSM_BLK
SSM_SEG = 8
SSM_UNROLL = 4


def _unrolled_loop(n, step, carry):
    u = SSM_UNROLL

    def trip(i, c):
        for j in range(u):
            c = step(i * u + j, c)
        return c
    carry = lax.fori_loop(0, n // u, trip, carry)
    for t in range(n - n % u, n):
        carry = step(jnp.int32(t), carry)
    return carry


def _cplx_pow2(a, b, n):
    for _ in range(int(math.log2(n))):
        a, b = a * a - b * b, 2.0 * a * b
    return a, b


def _seg_scan(ref, a, b, T, reverse):
    npair = len(a)
    zero = jnp.zeros((SSM_SEG, 128), F32)

    def make_step(store):
        def step(t, carry):
            lt = (T - 1 - t) if reverse else t
            row = pl.multiple_of(lt * SSM_SEG, SSM_SEG)
            blk = ref[pl.ds(row, SSM_SEG), :]
            new = []
            for q in range(npair):
                re, im = carry[2 * q], carry[2 * q + 1]
                nre = a[q] * re - b[q] * im + blk[:, q * 256:q * 256 + 128]
                nim = a[q] * im + b[q] * re + blk[:, q * 256 + 128:q * 256 + 256]
                new += [nre, nim]
            if store:
                ref[pl.ds(row, SSM_SEG), :] = jnp.concatenate(new, axis=1)
            return tuple(new)
        return step

    ends = _unrolled_loop(T, make_step(False), (zero,) * (2 * npair))
    sub = lax.broadcasted_iota(jnp.int32, (SSM_SEG, 128), 0)
    keep = (sub != SSM_SEG - 1) if reverse else (sub != 0)
    shift = (SSM_SEG - 1) if reverse else 1
    init = []
    for q in range(npair):
        pa, pb = _cplx_pow2(a[q], b[q], T)
        xr, xi = zero, zero
        for _ in range(SSM_SEG - 1):
            fr = ends[2 * q] + pa * xr - pb * xi
            fi = ends[2 * q + 1] + pa * xi + pb * xr
            xr = jnp.where(keep, pltpu.roll(fr, shift, 0), 0.0)
            xi = jnp.where(keep, pltpu.roll(fi, shift, 0), 0.0)
        init += [xr, xi]
    _unrolled_loop(T, make_step(True), tuple(init))
    return init


SSM_BW = SSM_BLK * SSM_H
SSM_SW = SSM_BLK * 2 * SSM_P
SSM_NPAIR = SSM_BLK // 2


def _seg_perm(a):
    L, C = a.shape
    return a.reshape(SSM_SEG, L // SSM_SEG, C).transpose(1, 0, 2).reshape(L, C)


def _seg_unperm(a):
    L, C = a.shape
    return a.reshape(L // SSM_SEG, SSM_SEG, C).transpose(1, 0, 2).reshape(L, C)


def _lam_rows(are_ref, aim_ref, d):
    a = [jnp.broadcast_to(are_ref[d, j:j + 1, :], (SSM_SEG, 128)) for j in range(SSM_NPAIR)]
    b = [jnp.broadcast_to(aim_ref[d, j:j + 1, :], (SSM_SEG, 128)) for j in range(SSM_NPAIR)]
    return a, b


def _ssm_fwd(u_p, wb, wc, are, aim, dvec):
    L = u_p.shape[0]
    T = L // SSM_SEG
    RC = min(512, L)

    def body(u_ref, wb_ref, wc_ref, are_ref, aim_ref, d_ref, y_ref, x_scr):
        y_ref[...] = u_ref[...] * d_ref[...]
        for d in range(2):
            def bu_chunk(c, _):
                rows = pl.ds(pl.multiple_of(c * RC, RC), RC)
                x_scr[rows, :] = jnp.dot(u_ref[rows, :].astype(BF16), wb_ref[d], preferred_element_type=F32)
                return 0
            lax.fori_loop(0, L // RC, bu_chunk, 0)
            a, b = _lam_rows(are_ref, aim_ref, d)
            _seg_scan(x_scr, a, b, T, reverse=(d == 1))

            def y_chunk(c, _):
                rows = pl.ds(pl.multiple_of(c * RC, RC), RC)
                y_ref[rows, :] += jnp.dot(x_scr[rows, :].astype(BF16), wc_ref[d], preferred_element_type=F32)
                return 0
            lax.fori_loop(0, L // RC, y_chunk, 0)

    blk4 = lambda g: (g, 0, 0, 0)
    return pl.pallas_call(
        body, name="ssm_fwd",
        out_shape=jax.ShapeDtypeStruct((L, D_SSM), F32),
        grid=(SSM_NB,),
        in_specs=[pl.BlockSpec((L, SSM_BW), lambda g: (0, g)),
                  pl.BlockSpec((None, 2, SSM_BW, SSM_SW), blk4),
                  pl.BlockSpec((None, 2, SSM_SW, SSM_BW), blk4),
                  pl.BlockSpec((None, 2, SSM_NPAIR, 128), blk4),
                  pl.BlockSpec((None, 2, SSM_NPAIR, 128), blk4),
                  pl.BlockSpec((1, SSM_BW), lambda g: (0, g))],
        out_specs=pl.BlockSpec((L, SSM_BW), lambda g: (0, g)),
        scratch_shapes=[pltpu.VMEM((L, SSM_SW), F32)],
        compiler_params=_params(("parallel",)),
    )(u_p, wb, wc, are, aim, dvec)


def _ssm_bwd(u_p, dy_p, wb, wc, are, aim, dvec):
    L = u_p.shape[0]
    T = L // SSM_SEG
    RC = min(512, L)

    def lam_acc(acc, sb, xb):
        new = []
        for q in range(SSM_NPAIR):
            sr, si = sb[:, q * 256:q * 256 + 128], sb[:, q * 256 + 128:q * 256 + 256]
            xr, xi = xb[:, q * 256:q * 256 + 128], xb[:, q * 256 + 128:q * 256 + 256]
            new += [acc[2 * q] + sr * xr + si * xi, acc[2 * q + 1] + si * xr - sr * xi]
        return tuple(new)

    def body(u_ref, dy_ref, wb_ref, wc_ref, are_ref, aim_ref, d_ref,
             du_ref, dwb_ref, dwc_ref, dare_ref, daim_ref, dd_ref, x_scr, s_scr):
        du_ref[...] = dy_ref[...] * d_ref[...]
        dd_ref[...] = _colsum(dy_ref[...] * u_ref[...])
        dwb_ref[...] = jnp.zeros_like(dwb_ref)
        dwc_ref[...] = jnp.zeros_like(dwc_ref)
        for d in range(2):
            rev = d == 1

            def in_chunk(c, _):
                rows = pl.ds(pl.multiple_of(c * RC, RC), RC)
                x_scr[rows, :] = jnp.dot(u_ref[rows, :].astype(BF16), wb_ref[d], preferred_element_type=F32)
                s_scr[rows, :] = lax.dot_general(dy_ref[rows, :].astype(BF16), wc_ref[d], _QK_T, preferred_element_type=F32)
                return 0
            lax.fori_loop(0, L // RC, in_chunk, 0)
            a, b = _lam_rows(are_ref, aim_ref, d)
            x_in = _seg_scan(x_scr, a, b, T, reverse=rev)
            _seg_scan(s_scr, a, [-v for v in b], T, reverse=not rev)

            def lam_step(t, acc):
                lt = (T - 2 - t) if rev else (t + 1)
                srow = pl.multiple_of(lt * SSM_SEG, SSM_SEG)
                xrow = pl.multiple_of((lt + 1 if rev else lt - 1) * SSM_SEG, SSM_SEG)
                return lam_acc(acc, s_scr[pl.ds(srow, SSM_SEG), :], x_scr[pl.ds(xrow, SSM_SEG), :])

            edge = pl.ds(((T - 1) if rev else 0) * SSM_SEG, SSM_SEG)
            acc0 = lam_acc((jnp.zeros((SSM_SEG, 128), F32),) * (2 * SSM_NPAIR), s_scr[edge, :], jnp.concatenate(x_in, axis=1))
            acc = _unrolled_loop(T - 1, lam_step, acc0)
            for q in range(SSM_NPAIR):
                dare_ref[d, q:q + 1, :] = _colsum(acc[2 * q])
                daim_ref[d, q:q + 1, :] = _colsum(acc[2 * q + 1])

            def out_chunk(c, _):
                rows = pl.ds(pl.multiple_of(c * RC, RC), RC)
                xs, ss = x_scr[rows, :].astype(BF16), s_scr[rows, :].astype(BF16)
                uu, dd = u_ref[rows, :].astype(BF16), dy_ref[rows, :].astype(BF16)
                dwc_ref[d] += lax.dot_general(xs, dd, _TA, preferred_element_type=F32)
                dwb_ref[d] += lax.dot_general(uu, ss, _TA, preferred_element_type=F32)
                du_ref[rows, :] += lax.dot_general(ss, wb_ref[d], _QK_T, preferred_element_type=F32)
                return 0
            lax.fori_loop(0, L // RC, out_chunk, 0)

    blk4 = lambda g: (g, 0, 0, 0)
    chan = pl.BlockSpec((L, SSM_BW), lambda g: (0, g))
    par_specs = [pl.BlockSpec((None, 2, SSM_BW, SSM_SW), blk4),
                 pl.BlockSpec((None, 2, SSM_SW, SSM_BW), blk4),
                 pl.BlockSpec((None, 2, SSM_NPAIR, 128), blk4),
                 pl.BlockSpec((None, 2, SSM_NPAIR, 128), blk4),
                 pl.BlockSpec((1, SSM_BW), lambda g: (0, g))]
    return pl.pallas_call(
        body, name="ssm_bwd",
        out_shape=[jax.ShapeDtypeStruct((L, D_SSM), F32),
                   jax.ShapeDtypeStruct((SSM_NB, 2, SSM_BW, SSM_SW), F32),
                   jax.ShapeDtypeStruct((SSM_NB, 2, SSM_SW, SSM_BW), F32),
                   jax.ShapeDtypeStruct((SSM_NB, 2, SSM_NPAIR, 128), F32),
                   jax.ShapeDtypeStruct((SSM_NB, 2, SSM_NPAIR, 128), F32),
                   jax.ShapeDtypeStruct((1, D_SSM), F32)],
        grid=(SSM_NB,),
        in_specs=[chan, chan] + par_specs,
        out_specs=[chan] + par_specs,
        scratch_shapes=[pltpu.VMEM((L, SSM_SW), F32), pltpu.VMEM((L, SSM_SW), F32)],
        compiler_params=_params(("parallel",)),
    )(u_p, dy_p, wb, wc, are, aim, dvec)


def _ssm_disc(a_re, a_im, log_dt, b_re, b_im):
    lam = lax.complex(jnp.minimum(a_re, -1e-4), a_im)
    dt = jnp.exp(log_dt)[..., None]
    lam_bar = jnp.exp(lam * dt)
    b_bar = ((lam_bar - 1.0) / lam)[..., None] * lax.complex(b_re, b_im)
    return jnp.real(lam_bar), jnp.imag(lam_bar), jnp.real(b_bar), jnp.imag(b_bar)


_EYE8 = np.eye(SSM_BLK, dtype=np.float32)


def _to_wb(bb_re, bb_im):
    bb = jnp.stack([bb_re, bb_im], axis=2).reshape(2, SSM_NB, SSM_BLK, 2, SSM_P, SSM_H)
    t = bb.transpose(1, 0, 5, 2, 3, 4)[:, :, None] * _EYE8[None, None, :, None, :, None, None]
    t = t.reshape(SSM_NB, 2, SSM_BLK, SSM_H, 4, 2, 2, SSM_P).transpose(0, 1, 2, 3, 4, 6, 5, 7)
    return t.reshape(SSM_NB, 2, SSM_BW, SSM_SW)


def _from_wb(dwb):
    t = dwb.reshape(SSM_NB, 2, SSM_BLK, SSM_H, 4, 2, 2, SSM_P).transpose(0, 1, 2, 3, 4, 6, 5, 7)
    t = t.reshape(SSM_NB, 2, SSM_BLK, SSM_H, SSM_BLK, 2, SSM_P)
    bb = jnp.sum(t * _EYE8[None, None, :, None, :, None, None], axis=2)
    bb = bb.transpose(1, 0, 3, 4, 5, 2).reshape(2, SSM_G, 2, SSM_P, SSM_H)
    return bb[:, :, 0], bb[:, :, 1]


def _to_wc(c_re, c_im):
    cc = jnp.stack([c_re, -c_im], axis=2).reshape(2, SSM_NB, SSM_BLK, 2, SSM_H, SSM_P)
    t = cc.transpose(1, 0, 2, 3, 5, 4)[:, :, :, :, :, None] * _EYE8.T[None, None, :, None, None, :, None]
    t = t.reshape(SSM_NB, 2, 4, 2, 2, SSM_P, SSM_BLK, SSM_H).transpose(0, 1, 2, 4, 3, 5, 6, 7)
    return t.reshape(SSM_NB, 2, SSM_SW, SSM_BW)


def _from_wc(dwc):
    t = dwc.reshape(SSM_NB, 2, 4, 2, 2, SSM_P, SSM_BLK, SSM_H).transpose(0, 1, 2, 4, 3, 5, 6, 7)
    t = t.reshape(SSM_NB, 2, SSM_BLK, 2, SSM_P, SSM_BLK, SSM_H)
    cc = jnp.sum(t * _EYE8.T[None, None, :, None, None, :, None], axis=5)
    cc = cc.transpose(1, 0, 2, 3, 5, 4).reshape(2, SSM_G, 2, SSM_H, SSM_P)
    return cc[:, :, 0], -cc[:, :, 1]


def _to_lam(v):
    return v.reshape(2, SSM_NB, SSM_NPAIR, 128).transpose(1, 0, 2, 3)


def _from_lam(v):
    return v.transpose(1, 0, 2, 3).reshape(2, SSM_G, SSM_P)


_MESH = pl.DeviceIdType.MESH
_ANY = pl.BlockSpec(memory_space=pl.ANY)
_BIG = (("w_in", (D_MODEL, D_IN), 1, D_IN // N_CHIPS),
        ("w_glu", (D_SSM, 2 * D_SSM), 1, 2 * D_SSM // N_CHIPS),
        ("w_out", (D_ATTN + D_SSM, D_MODEL), 0, (D_ATTN + D_SSM) // N_CHIPS),
        ("w_ple_gate", (D_MODEL, D_MODEL), 0, D_MODEL // N_CHIPS),
        ("w_ple_proj", (PLE_DIM, D_MODEL), 1, D_MODEL // N_CHIPS))


def _place():
    x, y, c = lax.axis_index("x"), lax.axis_index("y"), lax.axis_index("c")
    return x, y, c, [(1 - x, y), (x, 1 - y), (1 - x, 1 - y)]


def _gather_weights(shards):
    nt = len(_BIG)

    def body(*refs):
        srcs, dsts, stage = refs[:nt], refs[nt:2 * nt], refs[2 * nt:3 * nt]
        send_sems, recv_sems, fwd_send_sems, fwd_recv_sems, loc_sems = refs[3 * nt:]
        x, y, c, chips = _place()
        k = 2 * x + y

        def shard_of(t, kk):
            _, _, axis, sz = _BIG[t]
            sl = pl.ds(pl.multiple_of(kk * sz, sz), sz)
            return dsts[t].at[:, sl] if axis == 1 else dsts[t].at[sl, :]

        def half_of(ref, t, cc):
            n = ref.shape[0] // 2
            return ref.at[pl.ds(pl.multiple_of(cc * n, n), n), :]

        def ici(j, t, kk):
            px, py = chips[j]
            return pltpu.make_async_remote_copy(src_ref=half_of(srcs[t], t, c), dst_ref=half_of(shard_of(t, kk), t, c),
                                                send_sem=send_sems.at[j, t], recv_sem=recv_sems.at[j, t],
                                                device_id=(px, py, c), device_id_type=_MESH)

        def forward(j, t, kk, cc):
            part = half_of(shard_of(t, kk), t, cc)
            return pltpu.make_async_remote_copy(src_ref=part, dst_ref=part, send_sem=fwd_send_sems.at[j, t],
                                                recv_sem=fwd_recv_sems.at[j, t], device_id=(x, y, 1 - c), device_id_type=_MESH)

        load = [pltpu.make_async_copy(srcs[t], stage[t], loc_sems.at[t]) for t in range(nt)]
        local = [pltpu.make_async_copy(stage[t], shard_of(t, k), loc_sems.at[t]) for t in range(nt)]
        sends = [ici(j, t, k) for t in range(nt) for j in range(3)]
        for cp in load + sends:
            cp.start()
        for t in range(nt):
            load[t].wait()
            local[t].start()
        forwards = []
        for t in range(nt):
            for j, (px, py) in enumerate(chips):
                ici(j, t, 2 * px + py).wait_recv()
                forwards.append(forward(j, t, 2 * px + py, c))
                forwards[-1].start()
        for t in range(nt):
            for j, (px, py) in enumerate(chips):
                forward(j, t, 2 * px + py, 1 - c).wait_recv()
        for cp in sends + forwards:
            cp.wait_send()
        for cp in local:
            cp.wait()

    sems = pltpu.SemaphoreType.DMA((3, nt))
    return pl.pallas_call(
        body, name="gather_weights",
        out_shape=[jax.ShapeDtypeStruct(shape, BF16) for _, shape, _, _ in _BIG],
        in_specs=[_ANY] * nt, out_specs=[_ANY] * nt,
        scratch_shapes=[pltpu.VMEM(s.shape, BF16) for s in shards] + [sems, sems, sems, sems, pltpu.SemaphoreType.DMA((nt,))],
        compiler_params=pltpu.CompilerParams(vmem_limit_bytes=VMEM_LIMIT_V7X),
    )(*shards)


SMALL_W = 1024
SMALL_ROWS = 72
N_SMALL = 8 * SMALL_ROWS * SMALL_W
_RED = tuple((shape, ax, (shape[0] // 2, sz) if ax == 1 else (sz // 2, shape[1]), BF16) for _, shape, ax, sz in _BIG) + \
    (((8 * SMALL_ROWS, SMALL_W), 0, (SMALL_ROWS, SMALL_W), F32),)
_RED_TR = 128


def _piece(ref, t, kk, cc):
    _, ax, (pr, pc), _ = _RED[t]
    if ax == 1:
        return ref.at[pl.ds(pl.multiple_of(cc * pr, pr), pr), pl.ds(pl.multiple_of(kk * pc, pc), pc)]
    return ref.at[pl.ds(pl.multiple_of((2 * kk + cc) * pr, pr), pr), :]


def _half_shape(t):
    shape, ax, (pr, pc), _ = _RED[t]
    return (pr, shape[1]) if ax == 1 else (N_CHIPS * pr, pc)


def _piece_in_half(ref, t, kk):
    _, ax, (pr, pc), _ = _RED[t]
    return ref.at[:, pl.ds(pl.multiple_of(kk * pc, pc), pc)] if ax == 1 else ref.at[pl.ds(pl.multiple_of(kk * pr, pr), pr), :]


def _grad_sibling_exchange(grads):
    n = len(_RED)
    n_dma = sum(1 if ax == 1 else N_CHIPS for _, ax, _, _ in _RED)

    def body(*refs):
        srcs, dsts, (send_sems, recv_sems) = refs[:n], refs[n:2 * n], refs[2 * n:]
        x, y, c, _ = _place()
        pairs = []
        for t, (_, ax, (pr, _), _) in enumerate(_RED):
            if ax == 1:
                pairs.append((srcs[t].at[pl.ds(pl.multiple_of((1 - c) * pr, pr), pr), :], dsts[t]))
            else:
                pairs += [(_piece(srcs[t], t, kk, 1 - c), _piece_in_half(dsts[t], t, kk)) for kk in range(N_CHIPS)]
        cps = [pltpu.make_async_remote_copy(src_ref=s, dst_ref=d, send_sem=send_sems.at[i], recv_sem=recv_sems.at[i],
                                            device_id=(x, y, 1 - c), device_id_type=_MESH) for i, (s, d) in enumerate(pairs)]
        for cp in cps:
            cp.start()
        for cp in cps:
            cp.wait()

    return pl.pallas_call(
        body, name="grad_sibling_exchange",
        out_shape=[jax.ShapeDtypeStruct(_half_shape(t), F32) for t in range(n)],
        in_specs=[_ANY] * n, out_specs=[_ANY] * n,
        scratch_shapes=[pltpu.SemaphoreType.DMA((n_dma,)), pltpu.SemaphoreType.DMA((n_dma,))],
    )(*grads)


def _chip_sum(t, g, rs, place):
    shape, ax, (pr, pc), dt = _RED[t]
    W = shape[1]
    tr = min(pr, _RED_TR)
    nb = pr // tr

    def body(place_ref, g_ref, rs_ref, o_ref):
        o_ref[...] = (g_ref[...] + rs_ref[...]).astype(o_ref.dtype)

    return pl.pallas_call(
        body, name="grad_chip_sum_%d" % t,
        out_shape=jax.ShapeDtypeStruct(rs.shape, dt),
        grid_spec=pltpu.PrefetchScalarGridSpec(
            num_scalar_prefetch=1, grid=(1 if ax == 1 else N_CHIPS, nb),
            in_specs=[pl.BlockSpec((tr, W), lambda kk, i, pr_: ((2 * kk + pr_[0]) * nb + i, 0)),
                      pl.BlockSpec((tr, W), lambda kk, i, pr_: (kk * nb + i, 0))],
            out_specs=pl.BlockSpec((tr, W), lambda kk, i, pr_: (kk * nb + i, 0))),
        compiler_params=_params(("parallel", "parallel")),
    )(place, g, rs)


def _grad_chip_exchange(sums):
    n = len(_RED)

    def body(*refs):
        srcs, dsts, (send_sems, recv_sems) = refs[:n], refs[n:2 * n], refs[2 * n:]
        x, y, c, chips = _place()
        cps = [pltpu.make_async_remote_copy(src_ref=_piece_in_half(srcs[t], t, 2 * px + py), dst_ref=dsts[t].at[j],
                                            send_sem=send_sems.at[j, t], recv_sem=recv_sems.at[j, t],
                                            device_id=(px, py, c), device_id_type=_MESH)
               for t in range(n) for j, (px, py) in enumerate(chips)]
        for cp in cps:
            cp.start()
        for cp in cps:
            cp.wait()

    return pl.pallas_call(
        body, name="grad_chip_exchange",
        out_shape=[jax.ShapeDtypeStruct((3,) + piece, dt) for _, _, piece, dt in _RED],
        in_specs=[_ANY] * n, out_specs=[_ANY] * n,
        scratch_shapes=[pltpu.SemaphoreType.DMA((3, n)), pltpu.SemaphoreType.DMA((3, n))],
    )(*sums)


def _total_sum(t, g, rs, rc, place):
    shape, ax, (pr, pc), _ = _RED[t]
    tr = min(pr, _RED_TR)
    nb = pr // tr
    small = t == len(_RED) - 1

    def body(place_ref, g_ref, rs_ref, rc_ref, o_ref):
        o_ref[...] = (g_ref[...] + rs_ref[...]) + rc_ref[0].astype(F32) + rc_ref[1].astype(F32) + rc_ref[2].astype(F32)

    if ax == 1:
        g_map = lambda i, pr_: (pr_[0] * nb + i, pr_[1])
        rs_map = lambda i, pr_: (i, pr_[1])
    else:
        g_map = lambda i, pr_: ((2 * pr_[1] + pr_[0]) * nb + i, 0)
        rs_map = lambda i, pr_: (pr_[1] * nb + i, 0)
    o_map = (lambda i, pr_: ((2 * pr_[1] + pr_[0]) * nb + i, 0)) if small else (lambda i, pr_: (pr_[0] * nb + i, 0))
    return pl.pallas_call(
        body, name="grad_total_sum_%d" % t,
        out_shape=jax.ShapeDtypeStruct(((8 if small else 2) * pr, pc), F32),
        grid_spec=pltpu.PrefetchScalarGridSpec(
            num_scalar_prefetch=1, grid=(nb,),
            in_specs=[pl.BlockSpec((tr, pc), g_map), pl.BlockSpec((tr, pc), rs_map),
                      pl.BlockSpec((3, tr, pc), lambda i, pr_: (0, i, 0))],
            out_specs=pl.BlockSpec((tr, pc), o_map)),
        compiler_params=_params(("parallel",)),
    )(place, g, rs, rc)


def _grad_final_exchange(totals):
    n = len(_RED)
    nb = n - 1

    def body(*refs):
        srcs, dsts, (send_sems, recv_sems) = refs[:n], refs[n:2 * n], refs[2 * n:]
        x, y, c, chips = _place()
        me = 4 * x + 2 * y + c
        others = [(x, y, 1 - c)] + [(px, py, cc) for (px, py) in chips for cc in (c, 1 - c)]

        def half(ref, t, cc):
            pr = _RED[t][2][0]
            return ref.at[pl.ds(pl.multiple_of(cc * pr, pr), pr), :]

        def eighth(ref, dev):
            return ref.at[pl.ds(pl.multiple_of(dev * SMALL_ROWS, SMALL_ROWS), SMALL_ROWS), :]

        def big_copy(t, cc):
            return pltpu.make_async_remote_copy(src_ref=half(srcs[t], t, cc), dst_ref=half(dsts[t], t, cc), send_sem=send_sems.at[t],
                                                recv_sem=recv_sems.at[t], device_id=others[0], device_id_type=_MESH)

        def small_copy(i, dev):
            return pltpu.make_async_remote_copy(src_ref=eighth(srcs[nb], dev), dst_ref=eighth(dsts[nb], dev),
                                                send_sem=send_sems.at[nb + i], recv_sem=recv_sems.at[nb + i],
                                                device_id=others[i], device_id_type=_MESH)

        sends = [big_copy(t, c) for t in range(nb)] + [small_copy(i, me) for i in range(7)]
        for cp in sends:
            cp.start()
        for t in range(nb):
            big_copy(t, 1 - c).wait_recv()
        for i, (px, py, pc) in enumerate(others):
            small_copy(i, 4 * px + 2 * py + pc).wait_recv()
        for cp in sends:
            cp.wait_send()

    return pl.pallas_call(
        body, name="grad_final_exchange",
        out_shape=[jax.ShapeDtypeStruct(a.shape, F32) for a in totals],
        in_specs=[_ANY] * n, out_specs=[_ANY] * n,
        input_output_aliases={t: t for t in range(n)},
        scratch_shapes=[pltpu.SemaphoreType.DMA((nb + 7,)), pltpu.SemaphoreType.DMA((nb + 7,))],
    )(*totals)


def _reduce_gradients(big_grads, small_flat):
    c = lax.axis_index("c")
    k = 2 * lax.axis_index("x") + lax.axis_index("y")
    place = jnp.stack([c, k]).astype(jnp.int32)
    grads = list(big_grads) + [small_flat.reshape(8 * SMALL_ROWS, SMALL_W)]
    n = len(grads)
    from_sibling = _grad_sibling_exchange(grads)
    chip_sums = [_chip_sum(t, grads[t], from_sibling[t], place) for t in range(n)]
    from_chips = _grad_chip_exchange(chip_sums)
    totals = [_total_sum(t, grads[t], from_sibling[t], from_chips[t], place) for t in range(n)]
    *shards, small = _grad_final_exchange(totals)
    return shards, small.reshape(-1)


def _adamw(w, g, m, v, name):
    def fn(w, g, m, v):
        m = ADAM_B1 * m + (1.0 - ADAM_B1) * g
        v = ADAM_B2 * v + (1.0 - ADAM_B2) * (g * g)
        m_hat = m / (1.0 - ADAM_B1 ** ADAM_STEP)
        v_hat = v / (1.0 - ADAM_B2 ** ADAM_STEP)
        return -ADAM_LR * (m_hat / (jnp.sqrt(v_hat) + ADAM_EPS) + ADAM_WD * w), m, v
    W = w.shape[1]
    return _rowwise(fn, [(a, 0, W) for a in (w, g, m, v)], [], [(W, F32)] * 3, tr=128, name=name)


def _chunks(arr, off, width, w=512):
    return [(arr, off + i * w, w) for i in range(width // w)]


def _cat(vs):
    return jnp.concatenate(vs, axis=1)


def _forward_backward(x, p_b, tgt, full, small):
    L = x.shape[0]
    w_in, w_glu, w_out, w_pg, w_pp = full
    row = lambda v: v.reshape(1, -1)
    g_mix, g_ple, g_fin = row(small["norm_mix"]), row(small["norm_ple"]), row(small["norm_final"])
    gq, gk, b_glu = row(small["q_norm"]), row(small["k_norm"]), row(small["b_glu"])
    cos, sin = _rope_tables(L)

    hn_b, = _rowwise(lambda x, g: x * _rms(x) * g, [(x, 0, D_MODEL)], [g_mix], [(D_MODEL, BF16)], name="norm_mix")
    z = _matmul(hn_b, w_in, name="mm_in")
    qr, kr, vb = _attn_prep(z, gq, gk, cos, sin)
    o, lse = _attn_fwd(qr, kr, vb)

    ssm_names = ("ssm_a_re", "ssm_a_im", "ssm_log_dt", "ssm_b_re", "ssm_b_im")
    (lre, lim, bre, bim), disc_vjp = jax.vjp(_ssm_disc, *[small[n][0] for n in ssm_names])
    ssm = (_to_wb(bre, bim).astype(BF16), _to_wc(small["ssm_c_re"][0], small["ssm_c_im"][0]).astype(BF16),
           _to_lam(lre), _to_lam(lim), row(small["ssm_d"]))
    u_p = _seg_perm(z[:, Z_U:Z_U + D_SSM])
    y_s = _seg_unperm(_ssm_fwd(u_p, *ssm))
    ge_b, = _rowwise(_gelu, [(y_s, 0, D_SSM)], [], [(D_SSM, BF16)], name="gelu")
    glu = _matmul(ge_b, w_glu, name="mm_glu")

    def merge(ga0, ga1, a, b, gs0, gs1, o, bias):
        sa, _ = _silu_and_grad(_cat([ga0, ga1]))
        ss, _ = _silu_and_grad(_cat([gs0, gs1]))
        y2 = (a + bias[:, :D_SSM]) * _sig(b + bias[:, D_SSM:])
        return _cat([o * sa, y2 * ss])
    merge_rows = _chunks(z, Z_GA, D_ATTN) + [(glu, 0, D_SSM), (glu, D_SSM, D_SSM)] + _chunks(z, Z_GS, D_SSM) + [(o, 0, D_ATTN)]
    cat_b, = _rowwise(merge, merge_rows, [b_glu], [(D_MODEL, BF16)], name="merge")
    t_out = _matmul(cat_b, w_out, name="mm_out")

    def resid(x, t, g):
        h1 = x + t
        return h1, h1 * _rms(h1) * g
    h1, hp_b = _rowwise(resid, [(x, 0, D_MODEL), (t_out, 0, D_MODEL)], [g_ple], [(D_MODEL, F32), (D_MODEL, BF16)], name="resid_norm")
    gl = _matmul(hp_b, w_pg, name="mm_ple_gate")
    pp = _matmul(p_b, w_pp, name="mm_ple_proj")

    def head(h1, gl, pp, tgt, g):
        gate = _sig(gl)
        h2 = h1 + gate * pp
        r = _rms(h2)
        n = h2 * r
        err = n * g - tgt
        dy = err * (1.0 / D_MODEL)
        dn = dy * g
        dh2 = r * (dn - n * jnp.mean(dn * n, axis=-1, keepdims=True))
        dgate = dh2 * pp
        return dh2, dh2 * gate, dgate * gate * (1.0 - gate), _colsum(dy * n), _colsum(0.5 * err * err * (1.0 / D_MODEL))
    dh2, dpp_b, dgl_b, dg_fin, loss_cols = _rowwise(
        head, [(a, 0, D_MODEL) for a in (h1, gl, pp, tgt)], [g_fin],
        [(D_MODEL, F32), (D_MODEL, BF16), (D_MODEL, BF16)], [(1, D_MODEL), (1, D_MODEL)], name="loss_head")

    dw_pp = _matmul(p_b, dpp_b, ta=True, name="mm_d_w_ple_proj")
    dw_pg = _matmul(hp_b, dgl_b, ta=True, name="mm_d_w_ple_gate")
    dhp = _matmul(dgl_b, w_pg, tb=True, name="mm_d_hp")

    def resid_bwd(dhp, h1, dh2, g):
        dx, dg = _rms_bwd(h1, g, dhp)
        dh1 = dh2 + dx
        return dh1, dh1, _colsum(dg)
    dh1, dh1_b, dg_ple = _rowwise(resid_bwd, [(a, 0, D_MODEL) for a in (dhp, h1, dh2)], [g_ple],
                                  [(D_MODEL, F32), (D_MODEL, BF16)], [(1, D_MODEL)], name="resid_norm_bwd")
    dw_out = _matmul(cat_b, dh1_b, ta=True, name="mm_d_w_out")
    dcat = _matmul(dh1_b, w_out, tb=True, name="mm_d_cat")

    def merge_bwd(dya, dys, ga0, ga1, a, b, gs0, gs1, o, bias):
        ga, gs = _cat([ga0, ga1]), _cat([gs0, gs1])
        sa, dsa = _silu_and_grad(ga)
        ss, dss = _silu_and_grad(gs)
        a, sb = a + bias[:, :D_SSM], _sig(b + bias[:, D_SSM:])
        dy2 = dys * ss
        dglu = _cat([dy2 * sb, dy2 * a * sb * (1.0 - sb)])
        return dya * sa, dya * o * dsa, dys * (a * sb) * dss, dglu, _colsum(dglu)
    do_b, dga_b, dgs_b, dglu_b, db_glu = _rowwise(
        merge_bwd, [(dcat, 0, D_ATTN), (dcat, D_ATTN, D_SSM)] + merge_rows, [b_glu],
        [(D_ATTN, BF16), (D_ATTN, BF16), (D_SSM, BF16), (2 * D_SSM, BF16)], [(1, 2 * D_SSM)], name="merge_bwd")
    dw_glu = _matmul(ge_b, dglu_b, ta=True, name="mm_d_w_glu")
    dge = _matmul(dglu_b, w_glu, tb=True, name="mm_d_ge")
    dy_s, = _rowwise(lambda dge, y: dge * _gelu_grad(y), [(dge, 0, D_SSM), (y_s, 0, D_SSM)], [], [(D_SSM, F32)], name="gelu_bwd")
    du_p, dwb, dwc, dare, daim, d_ssm_d = _ssm_bwd(u_p, _seg_perm(dy_s), *ssm)
    dc_re, dc_im = _from_wc(dwc)
    da_re, da_im, dlog_dt, db_re, db_im = disc_vjp((_from_lam(dare), _from_lam(daim)) + _from_wb(dwb))

    dqr, dkr, dv = _attn_bwd(qr, kr, kr.T, vb, do_b, lse.reshape(N_HEADS, 1, L))
    dq_b, dk_b, dgq, dgk = _attn_prep_bwd(dqr, dkr, z, gq, gk, cos, sin)
    dz_b = _cat([dq_b, dk_b, dv.astype(BF16), dga_b, _seg_unperm(du_p).astype(BF16), dgs_b])
    dw_in = _matmul(hn_b, dz_b, ta=True, name="mm_d_w_in")
    dhn = _matmul(dz_b, w_in, tb=True, name="mm_d_hn")

    def norm_bwd(dhn, x, dh1, g):
        dx, dg = _rms_bwd(x, g, dhn)
        return dh1 + dx, _colsum(dg)
    grad_x, dg_mix = _rowwise(norm_bwd, [(a, 0, D_MODEL) for a in (dhn, x, dh1)], [g_mix], [(D_MODEL, F32)], [(1, D_MODEL)],
                              name="norm_mix_bwd")

    small_grads = {"norm_mix": dg_mix, "q_norm": dgq, "k_norm": dgk, "ssm_a_re": da_re, "ssm_a_im": da_im, "ssm_log_dt": dlog_dt,
                   "ssm_b_re": db_re, "ssm_b_im": db_im, "ssm_c_re": dc_re, "ssm_c_im": dc_im, "ssm_d": d_ssm_d,
                   "b_glu": db_glu, "norm_ple": dg_ple, "norm_final": dg_fin}
    return jnp.sum(loss_cols), grad_x, [dw_in, dw_glu, dw_out, dw_pg, dw_pp], small_grads


_SMALL = ("norm_mix", "q_norm", "k_norm", "ssm_a_re", "ssm_a_im", "ssm_log_dt", "ssm_b_re", "ssm_b_im", "ssm_c_re", "ssm_c_im",
          "ssm_d", "b_glu", "norm_ple", "norm_final")
_WEIGHTS = ("norm_mix", "w_in", "q_norm", "k_norm", "ssm_a_re", "ssm_a_im", "ssm_log_dt", "ssm_b_re", "ssm_b_im", "ssm_c_re",
            "ssm_c_im", "ssm_d", "w_glu", "b_glu", "w_out", "norm_ple", "w_ple_gate", "w_ple_proj", "norm_final")


def _flat_small(d):
    flat = jnp.concatenate([d[n].reshape(-1).astype(F32) for n in _SMALL])
    return jnp.pad(flat, (0, N_SMALL - flat.shape[0]))


def _split_small(flat, like):
    out, off = {}, 0
    for n in _SMALL:
        sz = math.prod(like[n].shape)
        out[n] = flat[off:off + sz].reshape(like[n].shape)
        off += sz
    return out


def kernel(x, p, norm_mix, w_in, q_norm, k_norm, ssm_a_re, ssm_a_im, ssm_log_dt, ssm_b_re, ssm_b_im, ssm_c_re, ssm_c_im, ssm_d, w_glu, b_glu, w_out, norm_ple, w_ple_gate, w_ple_proj, norm_final, loss_target, m_norm_mix, m_w_in, m_q_norm, m_k_norm, m_ssm_a_re, m_ssm_a_im, m_ssm_log_dt, m_ssm_b_re, m_ssm_b_im, m_ssm_c_re, m_ssm_c_im, m_ssm_d, m_w_glu, m_b_glu, m_w_out, m_norm_ple, m_w_ple_gate, m_w_ple_proj, m_norm_final, v_norm_mix, v_w_in, v_q_norm, v_k_norm, v_ssm_a_re, v_ssm_a_im, v_ssm_log_dt, v_ssm_b_re, v_ssm_b_im, v_ssm_c_re, v_ssm_c_im, v_ssm_d, v_w_glu, v_b_glu, v_w_out, v_norm_ple, v_w_ple_gate, v_w_ple_proj, v_norm_final):
    w = dict(norm_mix=norm_mix, w_in=w_in, q_norm=q_norm, k_norm=k_norm, ssm_a_re=ssm_a_re, ssm_a_im=ssm_a_im,
             ssm_log_dt=ssm_log_dt, ssm_b_re=ssm_b_re, ssm_b_im=ssm_b_im, ssm_c_re=ssm_c_re, ssm_c_im=ssm_c_im, ssm_d=ssm_d,
             w_glu=w_glu, b_glu=b_glu, w_out=w_out, norm_ple=norm_ple, w_ple_gate=w_ple_gate, w_ple_proj=w_ple_proj,
             norm_final=norm_final)
    m = dict(norm_mix=m_norm_mix, w_in=m_w_in, q_norm=m_q_norm, k_norm=m_k_norm, ssm_a_re=m_ssm_a_re, ssm_a_im=m_ssm_a_im,
             ssm_log_dt=m_ssm_log_dt, ssm_b_re=m_ssm_b_re, ssm_b_im=m_ssm_b_im, ssm_c_re=m_ssm_c_re, ssm_c_im=m_ssm_c_im,
             ssm_d=m_ssm_d, w_glu=m_w_glu, b_glu=m_b_glu, w_out=m_w_out, norm_ple=m_norm_ple, w_ple_gate=m_w_ple_gate,
             w_ple_proj=m_w_ple_proj, norm_final=m_norm_final)
    v = dict(norm_mix=v_norm_mix, w_in=v_w_in, q_norm=v_q_norm, k_norm=v_k_norm, ssm_a_re=v_ssm_a_re, ssm_a_im=v_ssm_a_im,
             ssm_log_dt=v_ssm_log_dt, ssm_b_re=v_ssm_b_re, ssm_b_im=v_ssm_b_im, ssm_c_re=v_ssm_c_re, ssm_c_im=v_ssm_c_im,
             ssm_d=v_ssm_d, w_glu=v_w_glu, b_glu=v_b_glu, w_out=v_w_out, norm_ple=v_norm_ple, w_ple_gate=v_w_ple_gate,
             w_ple_proj=v_w_ple_proj, norm_final=v_norm_final)
    big_names = [n for n, _, _, _ in _BIG]

    full = _gather_weights([w[n][0].astype(BF16) for n in big_names])
    small = {n: w[n] for n in _SMALL}
    loss_part, grad_x, big_grads, small_grads = _forward_backward(
        x[0], p[0, 0].astype(BF16), loss_target[0], full, small)
    loss = lax.psum(loss_part, ("x", "y", "c"))

    big_red, small_red = _reduce_gradients(big_grads, _flat_small(small_grads))
    grads = _split_small(small_red, w)
    delta, new_m, new_v = {}, {}, {}
    for n, g in zip(big_names, big_red):
        grads[n] = g[None]
        d_, m_, v_ = _adamw(w[n][0], g, m[n][0], v[n][0], "adamw_" + n)
        delta[n], new_m[n], new_v[n] = d_[None], m_[None], v_[None]
    d_, m_, v_ = _adamw(*[a.reshape(-1, SMALL_W) for a in (_flat_small(w), small_red, _flat_small(m), _flat_small(v))], "adamw_small")
    delta.update(_split_small(d_.reshape(-1), w))
    new_m.update(_split_small(m_.reshape(-1), w))
    new_v.update(_split_small(v_.reshape(-1), w))
    return (loss, grad_x[None], *[grads[n] for n in _WEIGHTS], *[delta[n] for n in _WEIGHTS],
            *[new_m[n] for n in _WEIGHTS], *[new_v[n] for n in _WEIGHTS])
```

```python
import functools
import math

import jax
import jax.numpy as jnp
import numpy as np
from jax import lax
from jax.experimental import pallas as pl
from jax.experimental.pallas import tpu as pltpu

D_MODEL = 2048
GRID_W = 64
PLE_DIM = 256
D_ATTN = 1024
N_HEADS = 8
N_KV = 2
HEAD_DIM = 128
ROPE_THETA = 10000.0
D_SSM = 1024
SSM_H = 16
SSM_G = 64
SSM_P = 64
D_KV = N_KV * HEAD_DIM
D_IN = 2 * D_ATTN + 2 * D_KV + 2 * D_SSM
EPS = 1e-6
Z_Q, Z_K, Z_V, Z_GA, Z_U, Z_GS = 0, 1024, 1280, 1536, 2560, 3584

ADAM_LR, ADAM_B1, ADAM_B2, ADAM_EPS, ADAM_WD, ADAM_STEP = 0.001, 0.9, 0.999, 1e-08, 0.01, 10

N_CHIPS = 4
VMEM_LIMIT_V7X = 56 * 1024 * 1024
F32 = jnp.float32
BF16 = jnp.bfloat16


def _params(sem, vmem=VMEM_LIMIT_V7X):
    return pltpu.CompilerParams(dimension_semantics=sem, vmem_limit_bytes=vmem)


def _matmul(a, b, *, ta=False, tb=False, out_dtype=F32, tm=1024, tn=512, name):
    M, K = (a.shape[1], a.shape[0]) if ta else a.shape
    N = b.shape[0] if tb else b.shape[1]
    tm, tn = min(tm, M), min(tn, N)
    assert M % tm == 0 and N % tn == 0, (name, M, N, K)
    dims = (((0 if ta else 1,), (1 if tb else 0,)), ((), ()))

    def body(a_ref, b_ref, o_ref):
        o_ref[...] = lax.dot_general(a_ref[...], b_ref[...], dims, preferred_element_type=F32).astype(o_ref.dtype)

    a_spec = pl.BlockSpec((K, tm), lambda i, j: (0, i)) if ta else pl.BlockSpec((tm, K), lambda i, j: (i, 0))
    b_spec = pl.BlockSpec((tn, K), lambda i, j: (j, 0)) if tb else pl.BlockSpec((K, tn), lambda i, j: (0, j))
    return pl.pallas_call(
        body, name=name,
        out_shape=jax.ShapeDtypeStruct((M, N), out_dtype),
        grid=(M // tm, N // tn),
        in_specs=[a_spec, b_spec],
        out_specs=pl.BlockSpec((tm, tn), lambda i, j: (i, j)),
        compiler_params=_params(("parallel", "parallel")),
    )(a, b)


def _rowwise(fn, rows, consts, outs, accs=(), *, tr=256, name):
    L = rows[0][0].shape[0]
    tr = math.gcd(tr, L)
    assert tr % 8 == 0 or tr == L, (name, L, tr)
    n_in, n_c, n_o, n_a = len(rows), len(consts), len(outs), len(accs)

    def body(*refs):
        ins = [r[...] for r in refs[:n_in + n_c]]
        res = fn(*ins)
        if not isinstance(res, (tuple, list)):
            res = (res,)
        o_refs = refs[n_in + n_c:n_in + n_c + n_o]
        a_refs = refs[n_in + n_c + n_o:]
        for r, v in zip(o_refs, res[:n_o]):
            r[...] = v.astype(r.dtype)
        if n_a:
            first = pl.program_id(0) == 0

            @pl.when(first)
            def _():
                for r, v in zip(a_refs, res[n_o:]):
                    r[...] = v.astype(F32)

            @pl.when(jnp.logical_not(first))
            def _():
                for r, v in zip(a_refs, res[n_o:]):
                    r[...] += v.astype(F32)

    in_specs = []
    for arr, off, w in rows:
        assert off % w == 0, (name, off, w)
        in_specs.append(pl.BlockSpec((tr, w), functools.partial(lambda i, c: (i, c), c=off // w)))
    for c in consts:
        in_specs.append(pl.BlockSpec(c.shape, lambda i: (0, 0)))
    out_shape = [jax.ShapeDtypeStruct((L, w), dt) for w, dt in outs] + [jax.ShapeDtypeStruct(s, F32) for s in accs]
    out_specs = [pl.BlockSpec((tr, w), lambda i: (i, 0)) for w, _ in outs] + [pl.BlockSpec(s, lambda i: (0, 0)) for s in accs]
    res = pl.pallas_call(
        body, name=name,
        out_shape=out_shape,
        grid=(L // tr,),
        in_specs=in_specs,
        out_specs=out_specs,
        compiler_params=_params(("arbitrary",) if n_a else ("parallel",)),
    )(*[r[0] for r in rows], *consts)
    return res


def _sig(x):
    return jax.nn.sigmoid(x)


def _silu_and_grad(x):
    s = _sig(x)
    return x * s, s * (1.0 + x * (1.0 - s))


_GELU_C = math.sqrt(2.0 / math.pi)


def _gelu(x):
    return 0.5 * x * (1.0 + jnp.tanh(_GELU_C * (x + 0.044715 * x * x * x)))


def _gelu_grad(x):
    t = jnp.tanh(_GELU_C * (x + 0.044715 * x * x * x))
    return 0.5 * (1.0 + t) + 0.5 * x * (1.0 - t * t) * _GELU_C * (1.0 + 3.0 * 0.044715 * x * x)


def _rms(x):
    return lax.rsqrt(jnp.mean(x * x, axis=-1, keepdims=True) + EPS)


def _rms_bwd(x, g, dy):
    r = _rms(x)
    n = x * r
    dn = dy * g
    return r * (dn - n * jnp.mean(dn * n, axis=-1, keepdims=True)), dy * n


def _colsum(v):
    return jnp.sum(v, axis=0, keepdims=True)


def _rope_partner(x):
    lane = lax.broadcasted_iota(jnp.int32, x.shape, x.ndim - 1)
    return jnp.where(lane % 64 < 32, pltpu.roll(x, 96, x.ndim - 1), pltpu.roll(x, 32, x.ndim - 1))


def _rope_tables(L):
    rows_n = L // GRID_W
    rows = jnp.repeat(jnp.arange(rows_n), GRID_W).astype(F32)
    cols = jnp.tile(jnp.arange(GRID_W), rows_n).astype(F32)
    n_freq = HEAD_DIM // 4
    inv_freq = ROPE_THETA ** (-jnp.arange(n_freq, dtype=F32) / n_freq)
    ar, ac = rows[:, None] * inv_freq[None, :], cols[:, None] * inv_freq[None, :]
    cos = jnp.concatenate([jnp.cos(ar), jnp.cos(ar), jnp.cos(ac), jnp.cos(ac)], axis=-1)
    sin = jnp.concatenate([-jnp.sin(ar), jnp.sin(ar), -jnp.sin(ac), jnp.sin(ac)], axis=-1)
    return cos, sin


def _heads(v):
    return [v[:, h * HEAD_DIM:(h + 1) * HEAD_DIM] for h in range(v.shape[1] // HEAD_DIM)]


def _attn_prep(z, q_norm, k_norm, cos, sin):
    def fn(q, k, v, cos, sin, gq, gk):
        def one(xh, g):
            xn = xh * _rms(xh) * g
            return xn * cos + _rope_partner(xn) * sin
        qr = jnp.concatenate([one(h, gq) for h in _heads(q)], axis=1)
        kr = jnp.concatenate([one(h, gk) for h in _heads(k)], axis=1)
        return qr, kr, v
    return _rowwise(fn, [(z, Z_Q, D_ATTN), (z, Z_K, D_KV), (z, Z_V, D_KV), (cos, 0, HEAD_DIM), (sin, 0, HEAD_DIM)],
                    [q_norm, k_norm], [(D_ATTN, BF16), (D_KV, BF16), (D_KV, BF16)], name="attn_prep")


def _attn_prep_bwd(dqr, dkr, z, q_norm, k_norm, cos, sin):
    def fn(dqr, dkr, q, k, cos, sin, gq, gk):
        def one(dyh, xh, g):
            dn = dyh * cos + _rope_partner(dyh * sin)
            return _rms_bwd(xh, g, dn)
        rq = [one(a, b, gq) for a, b in zip(_heads(dqr), _heads(q))]
        rk = [one(a, b, gk) for a, b in zip(_heads(dkr), _heads(k))]
        dq = jnp.concatenate([r[0] for r in rq], axis=1)
        dk = jnp.concatenate([r[0] for r in rk], axis=1)
        return dq, dk, _colsum(sum(r[1] for r in rq)), _colsum(sum(r[1] for r in rk))
    return _rowwise(fn, [(dqr, 0, D_ATTN), (dkr, 0, D_KV), (z, Z_Q, D_ATTN), (z, Z_K, D_KV), (cos, 0, HEAD_DIM), (sin, 0, HEAD_DIM)],
                    [q_norm, k_norm], [(D_ATTN, BF16), (D_KV, BF16)], [(1, HEAD_DIM), (1, HEAD_DIM)], name="attn_prep_bwd")


_QK_T = (((1,), (1,)), ((), ()))
_TA = (((0,), (0,)), ((), ()))
_REP = N_HEADS // N_KV


_EXP2_SCALE = HEAD_DIM ** -0.5 * math.log2(math.e)
ATTN_FWD_KEY_CHUNKS = 4
ATTN_BWD_KEY_CHUNKS = 8


def _attn_fwd(qr, kr, vb, *, tq=1024):
    L = qr.shape[0]
    tq = min(tq, L)
    kc = L // ATTN_FWD_KEY_CHUNKS

    def body(q_ref, k_ref, v_ref, o_ref, lse_ref):
        q = q_ref[...]
        m = jnp.full((tq, 1), -jnp.inf, F32)
        l = jnp.zeros((tq, 1), F32)
        o = jnp.zeros((tq, HEAD_DIM), F32)
        for c in range(ATTN_FWD_KEY_CHUNKS):
            ks = slice(c * kc, (c + 1) * kc)
            s = lax.dot_general(q, k_ref[ks, :], _QK_T, preferred_element_type=F32)
            m_new = jnp.maximum(m, jnp.max(s, axis=1, keepdims=True))
            a = jnp.exp2((m - m_new) * _EXP2_SCALE)
            p = jnp.exp2((s - m_new) * _EXP2_SCALE)
            l = a * l + jnp.sum(p, axis=1, keepdims=True)
            o = a * o + jnp.dot(p.astype(BF16), v_ref[ks, :], preferred_element_type=F32)
            m = m_new
        o_ref[...] = o * (1.0 / l)
        lse_ref[...] = m * _EXP2_SCALE + jnp.log2(l)

    kv = pl.BlockSpec((L, HEAD_DIM), lambda h, i: (0, h // _REP))
    return pl.pallas_call(
        body, name="attn_fwd",
        out_shape=[jax.ShapeDtypeStruct((L, D_ATTN), F32), jax.ShapeDtypeStruct((N_HEADS, L, 1), F32)],
        grid=(N_HEADS, L // tq),
        in_specs=[pl.BlockSpec((tq, HEAD_DIM), lambda h, i: (i, h)), kv, kv],
        out_specs=[pl.BlockSpec((tq, HEAD_DIM), lambda h, i: (i, h)),
                   pl.BlockSpec((None, tq, 1), lambda h, i: (h, i, 0))],
        compiler_params=_params(("parallel", "parallel")),
    )(qr, kr, vb)


def _attn_bwd(qr, kr, k_t, vb, do, lse, ex_ts, ex_sums, *, tq=512):
    L = qr.shape[0]
    tq = min(tq, L)
    scale = HEAD_DIM ** -0.5
    kc = L // ATTN_BWD_KEY_CHUNKS
    n_ex = len(ex_ts)
    grid = (N_KV, _REP, L // tq)

    def body(q_ref, k_ref, kt_ref, v_ref, do_ref, lse_ref, *rest):
        dq_ref, dk_ref, dv_ref = rest[n_ex:n_ex + 3]
        ex = _ChipExchange(ex_ts, rest[:n_ex], rest[n_ex + 3:2 * n_ex + 3], rest[2 * n_ex + 3:])
        step = (pl.program_id(0) * grid[1] + pl.program_id(1)) * grid[2] + pl.program_id(2)
        pl.when(step == 0)(ex.start)

        @pl.when((pl.program_id(1) == 0) & (pl.program_id(2) == 0))
        def _():
            dk_ref[...] = jnp.zeros_like(dk_ref)
            dv_ref[...] = jnp.zeros_like(dv_ref)

        q, do, lse = q_ref[...], do_ref[...], lse_ref[...]
        keys = [slice(c * kc, (c + 1) * kc) for c in range(ATTN_BWD_KEY_CHUNKS)]
        ps, dps = [], []
        for ks in keys:
            st = lax.dot_general(k_ref[ks, :], q, _QK_T, preferred_element_type=F32)
            p = jnp.exp2(st * _EXP2_SCALE - lse)
            dv_ref[ks, :] += jnp.dot(p.astype(BF16), do, preferred_element_type=F32)
            ps.append(p)
            dps.append(lax.dot_general(v_ref[ks, :], do, _QK_T, preferred_element_type=F32))
        delta = sum(jnp.sum(p * dp, axis=0, keepdims=True) for p, dp in zip(ps, dps))
        dq_t = 0.0
        for ks, p, dp in zip(keys, ps, dps):
            ds = (p * (dp - delta) * scale).astype(BF16)
            dk_ref[ks, :] += jnp.dot(ds, q, preferred_element_type=F32)
            dq_t = dq_t + jnp.dot(kt_ref[:, ks], ds, preferred_element_type=F32)
        dq_ref[...] = dq_t.T
        pl.when(step == grid[0] * grid[1] * grid[2] - 1)(ex.finish)

    head = lambda g, r, i: (i, g * _REP + r)
    kv = pl.BlockSpec((L, HEAD_DIM), lambda g, r, i: (0, g))
    return pl.pallas_call(
        body, name="attn_bwd",
        out_shape=[jax.ShapeDtypeStruct((L, D_ATTN), F32), jax.ShapeDtypeStruct((L, D_KV), F32), jax.ShapeDtypeStruct((L, D_KV), F32)] +
                  _ChipExchange.out_shape(ex_ts),
        grid=grid,
        in_specs=[pl.BlockSpec((tq, HEAD_DIM), head), kv,
                  pl.BlockSpec((HEAD_DIM, L), lambda g, r, i: (g, 0)), kv,
                  pl.BlockSpec((tq, HEAD_DIM), head),
                  pl.BlockSpec((None, 1, tq), lambda g, r, i: (g * _REP + r, 0, i))] + [_ANY] * n_ex,
        out_specs=[pl.BlockSpec((tq, HEAD_DIM), head), kv, kv] + [_ANY] * n_ex,
        scratch_shapes=_ChipExchange.scratch(ex_ts),
        compiler_params=_params(("arbitrary", "arbitrary", "arbitrary")),
    )(qr, kr, k_t, vb, do, lse, *ex_sums)


SSM_BLK = 8
SSM_NB = SSM_G // SSM_BLK
SSM_SEG = 8
SSM_UNROLL = 4


def _unrolled_loop(n, step, carry):
    u = SSM_UNROLL

    def trip(i, c):
        for j in range(u):
            c = step(i * u + j, c)
        return c
    carry = lax.fori_loop(0, n // u, trip, carry)
    for t in range(n - n % u, n):
        carry = step(jnp.int32(t), carry)
    return carry


def _cplx_pow2(a, b, n):
    for _ in range(int(math.log2(n))):
        a, b = a * a - b * b, 2.0 * a * b
    return a, b


def _seg_scan(ref, a, b, T, reverse):
    npair = len(a)
    zero = jnp.zeros((SSM_SEG, 128), F32)

    def make_step(store):
        def step(t, carry):
            lt = (T - 1 - t) if reverse else t
            row = pl.multiple_of(lt * SSM_SEG, SSM_SEG)
            blk = ref[pl.ds(row, SSM_SEG), :]
            new = []
            for q in range(npair):
                re, im = carry[2 * q], carry[2 * q + 1]
                nre = a[q] * re - b[q] * im + blk[:, q * 256:q * 256 + 128]
                nim = a[q] * im + b[q] * re + blk[:, q * 256 + 128:q * 256 + 256]
                new += [nre, nim]
            if store:
                ref[pl.ds(row, SSM_SEG), :] = jnp.concatenate(new, axis=1)
            return tuple(new)
        return step

    ends = _unrolled_loop(T, make_step(False), (zero,) * (2 * npair))
    sub = lax.broadcasted_iota(jnp.int32, (SSM_SEG, 128), 0)
    keep = (sub != SSM_SEG - 1) if reverse else (sub != 0)
    shift = (SSM_SEG - 1) if reverse else 1
    init = []
    for q in range(npair):
        pa, pb = _cplx_pow2(a[q], b[q], T)
        xr, xi = zero, zero
        for _ in range(SSM_SEG - 1):
            fr = ends[2 * q] + pa * xr - pb * xi
            fi = ends[2 * q + 1] + pa * xi + pb * xr
            xr = jnp.where(keep, pltpu.roll(fr, shift, 0), 0.0)
            xi = jnp.where(keep, pltpu.roll(fi, shift, 0), 0.0)
        init += [xr, xi]
    _unrolled_loop(T, make_step(True), tuple(init))
    return init


SSM_BW = SSM_BLK * SSM_H
SSM_SW = SSM_BLK * 2 * SSM_P
SSM_NPAIR = SSM_BLK // 2


def _seg_perm(a):
    L, C = a.shape
    return a.reshape(SSM_SEG, L // SSM_SEG, C).transpose(1, 0, 2).reshape(L, C)


def _seg_unperm(a):
    L, C = a.shape
    return a.reshape(L // SSM_SEG, SSM_SEG, C).transpose(1, 0, 2).reshape(L, C)


def _lam_rows(are_ref, aim_ref, d):
    a = [jnp.broadcast_to(are_ref[d, j:j + 1, :], (SSM_SEG, 128)) for j in range(SSM_NPAIR)]
    b = [jnp.broadcast_to(aim_ref[d, j:j + 1, :], (SSM_SEG, 128)) for j in range(SSM_NPAIR)]
    return a, b


def _ssm_fwd(u_p, wb, wc, are, aim, dvec):
    L = u_p.shape[0]
    T = L // SSM_SEG
    RC = min(512, L)

    def body(u_ref, wb_ref, wc_ref, are_ref, aim_ref, d_ref, y_ref, x_scr):
        y_ref[...] = u_ref[...] * d_ref[...]
        for d in range(2):
            def bu_chunk(c, _):
                rows = pl.ds(pl.multiple_of(c * RC, RC), RC)
                x_scr[rows, :] = jnp.dot(u_ref[rows, :].astype(BF16), wb_ref[d], preferred_element_type=F32)
                return 0
            lax.fori_loop(0, L // RC, bu_chunk, 0)
            a, b = _lam_rows(are_ref, aim_ref, d)
            _seg_scan(x_scr, a, b, T, reverse=(d == 1))

            def y_chunk(c, _):
                rows = pl.ds(pl.multiple_of(c * RC, RC), RC)
                y_ref[rows, :] += jnp.dot(x_scr[rows, :].astype(BF16), wc_ref[d], preferred_element_type=F32)
                return 0
            lax.fori_loop(0, L // RC, y_chunk, 0)

    blk4 = lambda g: (g, 0, 0, 0)
    return pl.pallas_call(
        body, name="ssm_fwd",
        out_shape=jax.ShapeDtypeStruct((L, D_SSM), F32),
        grid=(SSM_NB,),
        in_specs=[pl.BlockSpec((L, SSM_BW), lambda g: (0, g)),
                  pl.BlockSpec((None, 2, SSM_BW, SSM_SW), blk4),
                  pl.BlockSpec((None, 2, SSM_SW, SSM_BW), blk4),
                  pl.BlockSpec((None, 2, SSM_NPAIR, 128), blk4),
                  pl.BlockSpec((None, 2, SSM_NPAIR, 128), blk4),
                  pl.BlockSpec((1, SSM_BW), lambda g: (0, g))],
        out_specs=pl.BlockSpec((L, SSM_BW), lambda g: (0, g)),
        scratch_shapes=[pltpu.VMEM((L, SSM_SW), F32)],
        compiler_params=_params(("parallel",)),
    )(u_p, wb, wc, are, aim, dvec)


def _ssm_bwd(u_p, dy_p, wb, wc, are, aim, dvec):
    L = u_p.shape[0]
    T = L // SSM_SEG
    RC = min(512, L)

    def lam_acc(acc, sb, xb):
        new = []
        for q in range(SSM_NPAIR):
            sr, si = sb[:, q * 256:q * 256 + 128], sb[:, q * 256 + 128:q * 256 + 256]
            xr, xi = xb[:, q * 256:q * 256 + 128], xb[:, q * 256 + 128:q * 256 + 256]
            new += [acc[2 * q] + sr * xr + si * xi, acc[2 * q + 1] + si * xr - sr * xi]
        return tuple(new)

    def body(u_ref, dy_ref, wb_ref, wc_ref, are_ref, aim_ref, d_ref,
             du_ref, dwb_ref, dwc_ref, dare_ref, daim_ref, dd_ref, x_scr, s_scr):
        du_ref[...] = dy_ref[...] * d_ref[...]
        dd_ref[...] = _colsum(dy_ref[...] * u_ref[...])
        dwb_ref[...] = jnp.zeros_like(dwb_ref)
        dwc_ref[...] = jnp.zeros_like(dwc_ref)
        for d in range(2):
            rev = d == 1

            def in_chunk(c, _):
                rows = pl.ds(pl.multiple_of(c * RC, RC), RC)
                x_scr[rows, :] = jnp.dot(u_ref[rows, :].astype(BF16), wb_ref[d], preferred_element_type=F32)
                s_scr[rows, :] = lax.dot_general(dy_ref[rows, :].astype(BF16), wc_ref[d], _QK_T, preferred_element_type=F32)
                return 0
            lax.fori_loop(0, L // RC, in_chunk, 0)
            a, b = _lam_rows(are_ref, aim_ref, d)
            x_in = _seg_scan(x_scr, a, b, T, reverse=rev)
            _seg_scan(s_scr, a, [-v for v in b], T, reverse=not rev)

            def lam_step(t, acc):
                lt = (T - 2 - t) if rev else (t + 1)
                srow = pl.multiple_of(lt * SSM_SEG, SSM_SEG)
                xrow = pl.multiple_of((lt + 1 if rev else lt - 1) * SSM_SEG, SSM_SEG)
                return lam_acc(acc, s_scr[pl.ds(srow, SSM_SEG), :], x_scr[pl.ds(xrow, SSM_SEG), :])

            edge = pl.ds(((T - 1) if rev else 0) * SSM_SEG, SSM_SEG)
            acc0 = lam_acc((jnp.zeros((SSM_SEG, 128), F32),) * (2 * SSM_NPAIR), s_scr[edge, :], jnp.concatenate(x_in, axis=1))
            acc = _unrolled_loop(T - 1, lam_step, acc0)
            for q in range(SSM_NPAIR):
                dare_ref[d, q:q + 1, :] = _colsum(acc[2 * q])
                daim_ref[d, q:q + 1, :] = _colsum(acc[2 * q + 1])

            def out_chunk(c, _):
                rows = pl.ds(pl.multiple_of(c * RC, RC), RC)
                xs, ss = x_scr[rows, :].astype(BF16), s_scr[rows, :].astype(BF16)
                uu, dd = u_ref[rows, :].astype(BF16), dy_ref[rows, :].astype(BF16)
                dwc_ref[d] += lax.dot_general(xs, dd, _TA, preferred_element_type=F32)
                dwb_ref[d] += lax.dot_general(uu, ss, _TA, preferred_element_type=F32)
                du_ref[rows, :] += lax.dot_general(ss, wb_ref[d], _QK_T, preferred_element_type=F32)
                return 0
            lax.fori_loop(0, L // RC, out_chunk, 0)

    blk4 = lambda g: (g, 0, 0, 0)
    chan = pl.BlockSpec((L, SSM_BW), lambda g: (0, g))
    par_specs = [pl.BlockSpec((None, 2, SSM_BW, SSM_SW), blk4),
                 pl.BlockSpec((None, 2, SSM_SW, SSM_BW), blk4),
                 pl.BlockSpec((None, 2, SSM_NPAIR, 128), blk4),
                 pl.BlockSpec((None, 2, SSM_NPAIR, 128), blk4),
                 pl.BlockSpec((1, SSM_BW), lambda g: (0, g))]
    return pl.pallas_call(
        body, name="ssm_bwd",
        out_shape=[jax.ShapeDtypeStruct((L, D_SSM), F32),
                   jax.ShapeDtypeStruct((SSM_NB, 2, SSM_BW, SSM_SW), F32),
                   jax.ShapeDtypeStruct((SSM_NB, 2, SSM_SW, SSM_BW), F32),
                   jax.ShapeDtypeStruct((SSM_NB, 2, SSM_NPAIR, 128), F32),
                   jax.ShapeDtypeStruct((SSM_NB, 2, SSM_NPAIR, 128), F32),
                   jax.ShapeDtypeStruct((1, D_SSM), F32)],
        grid=(SSM_NB,),
        in_specs=[chan, chan] + par_specs,
        out_specs=[chan] + par_specs,
        scratch_shapes=[pltpu.VMEM((L, SSM_SW), F32), pltpu.VMEM((L, SSM_SW), F32)],
        compiler_params=_params(("parallel",)),
    )(u_p, dy_p, wb, wc, are, aim, dvec)


def _ssm_disc(a_re, a_im, log_dt, b_re, b_im):
    lam = lax.complex(jnp.minimum(a_re, -1e-4), a_im)
    dt = jnp.exp(log_dt)[..., None]
    lam_bar = jnp.exp(lam * dt)
    b_bar = ((lam_bar - 1.0) / lam)[..., None] * lax.complex(b_re, b_im)
    return jnp.real(lam_bar), jnp.imag(lam_bar), jnp.real(b_bar), jnp.imag(b_bar)


_EYE8 = np.eye(SSM_BLK, dtype=np.float32)


def _to_wb(bb_re, bb_im):
    bb = jnp.stack([bb_re, bb_im], axis=2).reshape(2, SSM_NB, SSM_BLK, 2, SSM_P, SSM_H)
    t = bb.transpose(1, 0, 5, 2, 3, 4)[:, :, None] * _EYE8[None, None, :, None, :, None, None]
    t = t.reshape(SSM_NB, 2, SSM_BLK, SSM_H, 4, 2, 2, SSM_P).transpose(0, 1, 2, 3, 4, 6, 5, 7)
    return t.reshape(SSM_NB, 2, SSM_BW, SSM_SW)


def _from_wb(dwb):
    t = dwb.reshape(SSM_NB, 2, SSM_BLK, SSM_H, 4, 2, 2, SSM_P).transpose(0, 1, 2, 3, 4, 6, 5, 7)
    t = t.reshape(SSM_NB, 2, SSM_BLK, SSM_H, SSM_BLK, 2, SSM_P)
    bb = jnp.sum(t * _EYE8[None, None, :, None, :, None, None], axis=2)
    bb = bb.transpose(1, 0, 3, 4, 5, 2).reshape(2, SSM_G, 2, SSM_P, SSM_H)
    return bb[:, :, 0], bb[:, :, 1]


def _to_wc(c_re, c_im):
    cc = jnp.stack([c_re, -c_im], axis=2).reshape(2, SSM_NB, SSM_BLK, 2, SSM_H, SSM_P)
    t = cc.transpose(1, 0, 2, 3, 5, 4)[:, :, :, :, :, None] * _EYE8.T[None, None, :, None, None, :, None]
    t = t.reshape(SSM_NB, 2, 4, 2, 2, SSM_P, SSM_BLK, SSM_H).transpose(0, 1, 2, 4, 3, 5, 6, 7)
    return t.reshape(SSM_NB, 2, SSM_SW, SSM_BW)


def _from_wc(dwc):
    t = dwc.reshape(SSM_NB, 2, 4, 2, 2, SSM_P, SSM_BLK, SSM_H).transpose(0, 1, 2, 4, 3, 5, 6, 7)
    t = t.reshape(SSM_NB, 2, SSM_BLK, 2, SSM_P, SSM_BLK, SSM_H)
    cc = jnp.sum(t * _EYE8.T[None, None, :, None, None, :, None], axis=5)
    cc = cc.transpose(1, 0, 2, 3, 5, 4).reshape(2, SSM_G, 2, SSM_H, SSM_P)
    return cc[:, :, 0], -cc[:, :, 1]


def _to_lam(v):
    return v.reshape(2, SSM_NB, SSM_NPAIR, 128).transpose(1, 0, 2, 3)


def _from_lam(v):
    return v.transpose(1, 0, 2, 3).reshape(2, SSM_G, SSM_P)


_MESH = pl.DeviceIdType.MESH
_ANY = pl.BlockSpec(memory_space=pl.ANY)
_BIG = (("w_in", (D_MODEL, D_IN), 1, D_IN // N_CHIPS),
        ("w_glu", (D_SSM, 2 * D_SSM), 1, 2 * D_SSM // N_CHIPS),
        ("w_out", (D_ATTN + D_SSM, D_MODEL), 0, (D_ATTN + D_SSM) // N_CHIPS),
        ("w_ple_gate", (D_MODEL, D_MODEL), 0, D_MODEL // N_CHIPS),
        ("w_ple_proj", (PLE_DIM, D_MODEL), 1, D_MODEL // N_CHIPS))


def _place():
    x, y, c = lax.axis_index("x"), lax.axis_index("y"), lax.axis_index("c")
    return x, y, c, [(1 - x, y), (x, 1 - y), (1 - x, 1 - y)]


class _Gather:
    def __init__(self, ts, srcs, dsts, stage, sems):
        self.ts, self.srcs, self.dsts, self.stage = ts, srcs, dsts, stage
        self.send_sems, self.recv_sems, self.fwd_send_sems, self.fwd_recv_sems, self.loc_sems = sems
        self.x, self.y, self.c, self.chips = _place()
        self.n = len(ts)

    @staticmethod
    def scratch(shards):
        sems = pltpu.SemaphoreType.DMA((3, len(shards)))
        return [pltpu.VMEM(s.shape, BF16) for s in shards] + [sems, sems, sems, sems, pltpu.SemaphoreType.DMA((len(shards),))]

    def _shard_of(self, i, kk):
        _, _, axis, sz = _BIG[self.ts[i]]
        sl = pl.ds(pl.multiple_of(kk * sz, sz), sz)
        return self.dsts[i].at[:, sl] if axis == 1 else self.dsts[i].at[sl, :]

    @staticmethod
    def _half_of(ref, cc):
        n = ref.shape[0] // 2
        return ref.at[pl.ds(pl.multiple_of(cc * n, n), n), :]

    def _ici(self, j, i, kk):
        px, py = self.chips[j]
        return pltpu.make_async_remote_copy(
            src_ref=self._half_of(self.srcs[i], self.c), dst_ref=self._half_of(self._shard_of(i, kk), self.c),
            send_sem=self.send_sems.at[j, i], recv_sem=self.recv_sems.at[j, i],
            device_id=(px, py, self.c), device_id_type=_MESH)

    def _forward(self, j, i, kk, cc):
        part = self._half_of(self._shard_of(i, kk), cc)
        return pltpu.make_async_remote_copy(
            src_ref=part, dst_ref=part, send_sem=self.fwd_send_sems.at[j, i], recv_sem=self.fwd_recv_sems.at[j, i],
            device_id=(self.x, self.y, 1 - self.c), device_id_type=_MESH)

    def _load(self, i):
        return pltpu.make_async_copy(self.srcs[i], self.stage[i], self.loc_sems.at[i])

    def _place_own(self, i):
        return pltpu.make_async_copy(self.stage[i], self._shard_of(i, 2 * self.x + self.y), self.loc_sems.at[i])

    def _peers(self):
        return [(i, j, 2 * px + py) for i in range(self.n) for j, (px, py) in enumerate(self.chips)]

    def start(self):
        for i in range(self.n):
            self._load(i).start()
        for i, j, _ in self._peers():
            self._ici(j, i, 2 * self.x + self.y).start()

    def forward(self):
        for i in range(self.n):
            self._load(i).wait()
            self._place_own(i).start()
        for i, j, kk in self._peers():
            self._ici(j, i, kk).wait_recv()
            self._forward(j, i, kk, self.c).start()

    def finish(self):
        for i, j, kk in self._peers():
            self._forward(j, i, kk, 1 - self.c).wait_recv()
        for i, j, kk in self._peers():
            self._ici(j, i, kk).wait_send()
            self._forward(j, i, kk, self.c).wait_send()
        for i in range(self.n):
            self._place_own(i).wait()


def _gather_weights(ts, shards):
    n = len(ts)

    def body(*refs):
        g = _Gather(ts, refs[:n], refs[n:2 * n], refs[2 * n:3 * n], refs[3 * n:])
        g.start()
        g.forward()
        g.finish()

    return pl.pallas_call(
        body, name="gather_weights",
        out_shape=[jax.ShapeDtypeStruct(_BIG[t][1], BF16) for t in ts],
        in_specs=[_ANY] * n, out_specs=[_ANY] * n,
        scratch_shapes=_Gather.scratch(shards),
        compiler_params=pltpu.CompilerParams(vmem_limit_bytes=VMEM_LIMIT_V7X),
    )(*shards)


def _matmul_gather(a, b, ts, shards, *, tm=1024, tn=512, name):
    M, K = a.shape
    N = b.shape[1]
    tm, tn = min(tm, M), min(tn, N)
    assert M % tm == 0 and N % tn == 0, (name, M, N)
    gm, gn = M // tm, N // tn
    n = len(ts)

    def body(a_ref, b_ref, *refs):
        o_ref = refs[n]
        g = _Gather(ts, refs[:n], refs[n + 1:2 * n + 1], refs[2 * n + 1:3 * n + 1], refs[3 * n + 1:])
        step = pl.program_id(0) * gn + pl.program_id(1)
        pl.when(step == 0)(g.start)
        o_ref[...] = jnp.dot(a_ref[...], b_ref[...], preferred_element_type=F32).astype(o_ref.dtype)
        pl.when(step == gm * gn // 2)(g.forward)
        pl.when(step == gm * gn - 1)(g.finish)

    return pl.pallas_call(
        body, name=name,
        out_shape=[jax.ShapeDtypeStruct((M, N), F32)] + [jax.ShapeDtypeStruct(_BIG[t][1], BF16) for t in ts],
        grid=(gm, gn),
        in_specs=[pl.BlockSpec((tm, K), lambda i, j: (i, 0)), pl.BlockSpec((K, tn), lambda i, j: (0, j))] + [_ANY] * n,
        out_specs=[pl.BlockSpec((tm, tn), lambda i, j: (i, j))] + [_ANY] * n,
        scratch_shapes=_Gather.scratch(shards),
        compiler_params=_params(("arbitrary", "arbitrary")),
    )(a, b, *shards)


SMALL_W = 1024
SMALL_ROWS = 72
N_SMALL = 8 * SMALL_ROWS * SMALL_W
_RED = tuple((shape, ax, (shape[0] // 2, sz) if ax == 1 else (sz // 2, shape[1]), BF16) for _, shape, ax, sz in _BIG) + \
    (((8 * SMALL_ROWS, SMALL_W), 0, (SMALL_ROWS, SMALL_W), F32),)
_RED_TR = 128


def _piece(ref, t, kk, cc):
    _, ax, (pr, pc), _ = _RED[t]
    if ax == 1:
        return ref.at[pl.ds(pl.multiple_of(cc * pr, pr), pr), pl.ds(pl.multiple_of(kk * pc, pc), pc)]
    return ref.at[pl.ds(pl.multiple_of((2 * kk + cc) * pr, pr), pr), :]


def _half_shape(t):
    shape, ax, (pr, pc), _ = _RED[t]
    return (pr, shape[1]) if ax == 1 else (N_CHIPS * pr, pc)


def _piece_in_half(ref, t, kk):
    _, ax, (pr, pc), _ = _RED[t]
    return ref.at[:, pl.ds(pl.multiple_of(kk * pc, pc), pc)] if ax == 1 else ref.at[pl.ds(pl.multiple_of(kk * pr, pr), pr), :]


def _grad_sibling_exchange(ts, grads, name):
    n = len(ts)
    n_dma = sum(1 if _RED[t][1] == 1 else N_CHIPS for t in ts)

    def body(*refs):
        srcs, dsts, (send_sems, recv_sems) = refs[:n], refs[n:2 * n], refs[2 * n:]
        x, y, c, _ = _place()
        pairs = []
        for i, t in enumerate(ts):
            _, ax, (pr, _), _ = _RED[t]
            if ax == 1:
                pairs.append((srcs[i].at[pl.ds(pl.multiple_of((1 - c) * pr, pr), pr), :], dsts[i]))
            else:
                pairs += [(_piece(srcs[i], t, kk, 1 - c), _piece_in_half(dsts[i], t, kk)) for kk in range(N_CHIPS)]
        cps = [pltpu.make_async_remote_copy(src_ref=s, dst_ref=d, send_sem=send_sems.at[i], recv_sem=recv_sems.at[i],
                                            device_id=(x, y, 1 - c), device_id_type=_MESH) for i, (s, d) in enumerate(pairs)]
        for cp in cps:
            cp.start()
        for cp in cps:
            cp.wait()

    return pl.pallas_call(
        body, name=name,
        out_shape=[jax.ShapeDtypeStruct(_half_shape(t), F32) for t in ts],
        in_specs=[_ANY] * n, out_specs=[_ANY] * n,
        scratch_shapes=[pltpu.SemaphoreType.DMA((n_dma,)), pltpu.SemaphoreType.DMA((n_dma,))],
    )(*grads)


def _chip_sum(t, g, rs, place):
    shape, ax, (pr, pc), dt = _RED[t]
    W = shape[1]
    tr = min(pr, _RED_TR)
    nb = pr // tr

    def body(place_ref, g_ref, rs_ref, o_ref):
        o_ref[...] = (g_ref[...] + rs_ref[...]).astype(o_ref.dtype)

    return pl.pallas_call(
        body, name="grad_chip_sum_%d" % t,
        out_shape=jax.ShapeDtypeStruct(rs.shape, dt),
        grid_spec=pltpu.PrefetchScalarGridSpec(
            num_scalar_prefetch=1, grid=(1 if ax == 1 else N_CHIPS, nb),
            in_specs=[pl.BlockSpec((tr, W), lambda kk, i, pr_: ((2 * kk + pr_[0]) * nb + i, 0)),
                      pl.BlockSpec((tr, W), lambda kk, i, pr_: (kk * nb + i, 0))],
            out_specs=pl.BlockSpec((tr, W), lambda kk, i, pr_: (kk * nb + i, 0))),
        compiler_params=_params(("parallel", "parallel")),
    )(place, g, rs)


class _ChipExchange:
    def __init__(self, ts, srcs, dsts, sems):
        self.send_sems, self.recv_sems = sems
        x, y, c, chips = _place()
        self.copies = lambda: [
            pltpu.make_async_remote_copy(src_ref=_piece_in_half(srcs[i], t, 2 * px + py), dst_ref=dsts[i].at[j],
                                         send_sem=self.send_sems.at[j, i], recv_sem=self.recv_sems.at[j, i],
                                         device_id=(px, py, c), device_id_type=_MESH)
            for i, t in enumerate(ts) for j, (px, py) in enumerate(chips)]

    @staticmethod
    def scratch(ts):
        return [pltpu.SemaphoreType.DMA((3, len(ts))), pltpu.SemaphoreType.DMA((3, len(ts)))]

    @staticmethod
    def out_shape(ts):
        return [jax.ShapeDtypeStruct((3,) + _RED[t][2], _RED[t][3]) for t in ts]

    def start(self):
        for cp in self.copies():
            cp.start()

    def finish(self):
        for cp in self.copies():
            cp.wait()


def _grad_chip_exchange(ts, sums):
    n = len(ts)

    def body(*refs):
        ex = _ChipExchange(ts, refs[:n], refs[n:2 * n], refs[2 * n:])
        ex.start()
        ex.finish()

    return pl.pallas_call(
        body, name="grad_chip_exchange",
        out_shape=_ChipExchange.out_shape(ts),
        in_specs=[_ANY] * n, out_specs=[_ANY] * n,
        scratch_shapes=_ChipExchange.scratch(ts),
    )(*sums)


def _total_sum(t, g, rs, rc, place):
    shape, ax, (pr, pc), _ = _RED[t]
    tr = min(pr, _RED_TR)
    nb = pr // tr
    small = t == len(_RED) - 1

    def body(place_ref, g_ref, rs_ref, rc_ref, o_ref):
        o_ref[...] = (g_ref[...] + rs_ref[...]) + rc_ref[0].astype(F32) + rc_ref[1].astype(F32) + rc_ref[2].astype(F32)

    if ax == 1:
        g_map = lambda i, pr_: (pr_[0] * nb + i, pr_[1])
        rs_map = lambda i, pr_: (i, pr_[1])
    else:
        g_map = lambda i, pr_: ((2 * pr_[1] + pr_[0]) * nb + i, 0)
        rs_map = lambda i, pr_: (pr_[1] * nb + i, 0)
    o_map = (lambda i, pr_: ((2 * pr_[1] + pr_[0]) * nb + i, 0)) if small else (lambda i, pr_: (pr_[0] * nb + i, 0))
    return pl.pallas_call(
        body, name="grad_total_sum_%d" % t,
        out_shape=jax.ShapeDtypeStruct(((8 if small else 2) * pr, pc), F32),
        grid_spec=pltpu.PrefetchScalarGridSpec(
            num_scalar_prefetch=1, grid=(nb,),
            in_specs=[pl.BlockSpec((tr, pc), g_map), pl.BlockSpec((tr, pc), rs_map),
                      pl.BlockSpec((3, tr, pc), lambda i, pr_: (0, i, 0))],
            out_specs=pl.BlockSpec((tr, pc), o_map)),
        compiler_params=_params(("parallel",)),
    )(place, g, rs, rc)


def _grad_final_exchange(totals):
    n = len(_RED)
    nb = n - 1

    def body(*refs):
        srcs, dsts, (send_sems, recv_sems) = refs[:n], refs[n:2 * n], refs[2 * n:]
        x, y, c, chips = _place()
        me = 4 * x + 2 * y + c
        others = [(x, y, 1 - c)] + [(px, py, cc) for (px, py) in chips for cc in (c, 1 - c)]

        def half(ref, t, cc):
            pr = _RED[t][2][0]
            return ref.at[pl.ds(pl.multiple_of(cc * pr, pr), pr), :]

        def eighth(ref, dev):
            return ref.at[pl.ds(pl.multiple_of(dev * SMALL_ROWS, SMALL_ROWS), SMALL_ROWS), :]

        def big_copy(t, cc):
            return pltpu.make_async_remote_copy(src_ref=half(srcs[t], t, cc), dst_ref=half(dsts[t], t, cc), send_sem=send_sems.at[t],
                                                recv_sem=recv_sems.at[t], device_id=others[0], device_id_type=_MESH)

        def small_copy(i, dev):
            return pltpu.make_async_remote_copy(src_ref=eighth(srcs[nb], dev), dst_ref=eighth(dsts[nb], dev),
                                                send_sem=send_sems.at[nb + i], recv_sem=recv_sems.at[nb + i],
                                                device_id=others[i], device_id_type=_MESH)

        sends = [big_copy(t, c) for t in range(nb)] + [small_copy(i, me) for i in range(7)]
        for cp in sends:
            cp.start()
        for t in range(nb):
            big_copy(t, 1 - c).wait_recv()
        for i, (px, py, pc) in enumerate(others):
            small_copy(i, 4 * px + 2 * py + pc).wait_recv()
        for cp in sends:
            cp.wait_send()

    return pl.pallas_call(
        body, name="grad_final_exchange",
        out_shape=[jax.ShapeDtypeStruct(a.shape, F32) for a in totals],
        in_specs=[_ANY] * n, out_specs=[_ANY] * n,
        input_output_aliases={t: t for t in range(n)},
        scratch_shapes=[pltpu.SemaphoreType.DMA((nb + 7,)), pltpu.SemaphoreType.DMA((nb + 7,))],
    )(*totals)


def _grad_place():
    return jnp.stack([lax.axis_index("c"), 2 * lax.axis_index("x") + lax.axis_index("y")]).astype(jnp.int32)


def _reduce_begin(ts, grads, place, tag):
    from_sibling = _grad_sibling_exchange(ts, grads, "grad_sibling_exchange_" + tag)
    return from_sibling, [_chip_sum(t, g, r, place) for t, g, r in zip(ts, grads, from_sibling)]


def _reduce_end(ts, grads, from_sibling, from_chips, place):
    return [_total_sum(t, g, r, q, place) for t, g, r, q in zip(ts, grads, from_sibling, from_chips)]


_EARLY = (1, 2, 3, 4)
_LATE = (0, 5)


def _adamw(w, g, m, v, name):
    def fn(w, g, m, v):
        m = ADAM_B1 * m + (1.0 - ADAM_B1) * g
        v = ADAM_B2 * v + (1.0 - ADAM_B2) * (g * g)
        m_hat = m / (1.0 - ADAM_B1 ** ADAM_STEP)
        v_hat = v / (1.0 - ADAM_B2 ** ADAM_STEP)
        return -ADAM_LR * (m_hat / (jnp.sqrt(v_hat) + ADAM_EPS) + ADAM_WD * w), m, v
    W = w.shape[1]
    return _rowwise(fn, [(a, 0, W) for a in (w, g, m, v)], [], [(W, F32)] * 3, tr=128, name=name)


def _chunks(arr, off, width, w=512):
    return [(arr, off + i * w, w) for i in range(width // w)]


def _cat(vs):
    return jnp.concatenate(vs, axis=1)


def _forward_backward(x, p_b, tgt, shards, small):
    L = x.shape[0]
    w_in, = _gather_weights([0], shards[:1])
    row = lambda v: v.reshape(1, -1)
    g_mix, g_ple, g_fin = row(small["norm_mix"]), row(small["norm_ple"]), row(small["norm_final"])
    gq, gk, b_glu = row(small["q_norm"]), row(small["k_norm"]), row(small["b_glu"])
    cos, sin = _rope_tables(L)

    hn_b, = _rowwise(lambda x, g: x * _rms(x) * g, [(x, 0, D_MODEL)], [g_mix], [(D_MODEL, BF16)], name="norm_mix")
    z, w_glu, w_out, w_pg, w_pp = _matmul_gather(hn_b, w_in, [1, 2, 3, 4], shards[1:], name="mm_in")
    qr, kr, vb = _attn_prep(z, gq, gk, cos, sin)
    o, lse = _attn_fwd(qr, kr, vb)

    ssm_names = ("ssm_a_re", "ssm_a_im", "ssm_log_dt", "ssm_b_re", "ssm_b_im")
    (lre, lim, bre, bim), disc_vjp = jax.vjp(_ssm_disc, *[small[n][0] for n in ssm_names])
    ssm = (_to_wb(bre, bim).astype(BF16), _to_wc(small["ssm_c_re"][0], small["ssm_c_im"][0]).astype(BF16),
           _to_lam(lre), _to_lam(lim), row(small["ssm_d"]))
    u_p = _seg_perm(z[:, Z_U:Z_U + D_SSM])
    y_s = _seg_unperm(_ssm_fwd(u_p, *ssm))
    ge_b, = _rowwise(_gelu, [(y_s, 0, D_SSM)], [], [(D_SSM, BF16)], name="gelu")
    glu = _matmul(ge_b, w_glu, name="mm_glu")

    def merge(ga0, ga1, a, b, gs0, gs1, o, bias):
        sa, _ = _silu_and_grad(_cat([ga0, ga1]))
        ss, _ = _silu_and_grad(_cat([gs0, gs1]))
        y2 = (a + bias[:, :D_SSM]) * _sig(b + bias[:, D_SSM:])
        return _cat([o * sa, y2 * ss])
    merge_rows = _chunks(z, Z_GA, D_ATTN) + [(glu, 0, D_SSM), (glu, D_SSM, D_SSM)] + _chunks(z, Z_GS, D_SSM) + [(o, 0, D_ATTN)]
    cat_b, = _rowwise(merge, merge_rows, [b_glu], [(D_MODEL, BF16)], name="merge")
    t_out = _matmul(cat_b, w_out, name="mm_out")

    def resid(x, t, g):
        h1 = x + t
        return h1, h1 * _rms(h1) * g
    h1, hp_b = _rowwise(resid, [(x, 0, D_MODEL), (t_out, 0, D_MODEL)], [g_ple], [(D_MODEL, F32), (D_MODEL, BF16)], name="resid_norm")
    gl = _matmul(hp_b, w_pg, name="mm_ple_gate")
    pp = _matmul(p_b, w_pp, name="mm_ple_proj")

    def head(h1, gl, pp, tgt, g):
        gate = _sig(gl)
        h2 = h1 + gate * pp
        r = _rms(h2)
        n = h2 * r
        err = n * g - tgt
        dy = err * (1.0 / D_MODEL)
        dn = dy * g
        dh2 = r * (dn - n * jnp.mean(dn * n, axis=-1, keepdims=True))
        dgate = dh2 * pp
        return dh2, dh2 * gate, dgate * gate * (1.0 - gate), _colsum(dy * n), _colsum(0.5 * err * err * (1.0 / D_MODEL))
    dh2, dpp_b, dgl_b, dg_fin, loss_cols = _rowwise(
        head, [(a, 0, D_MODEL) for a in (h1, gl, pp, tgt)], [g_fin],
        [(D_MODEL, F32), (D_MODEL, BF16), (D_MODEL, BF16)], [(1, D_MODEL), (1, D_MODEL)], name="loss_head")

    dw_pp = _matmul(p_b, dpp_b, ta=True, name="mm_d_w_ple_proj")
    dw_pg = _matmul(hp_b, dgl_b, ta=True, name="mm_d_w_ple_gate")
    dhp = _matmul(dgl_b, w_pg, tb=True, name="mm_d_hp")

    def resid_bwd(dhp, h1, dh2, g):
        dx, dg = _rms_bwd(h1, g, dhp)
        dh1 = dh2 + dx
        return dh1, dh1, _colsum(dg)
    dh1, dh1_b, dg_ple = _rowwise(resid_bwd, [(a, 0, D_MODEL) for a in (dhp, h1, dh2)], [g_ple],
                                  [(D_MODEL, F32), (D_MODEL, BF16)], [(1, D_MODEL)], name="resid_norm_bwd")
    dw_out = _matmul(cat_b, dh1_b, ta=True, name="mm_d_w_out")
    dcat = _matmul(dh1_b, w_out, tb=True, name="mm_d_cat")

    def merge_bwd(dya, dys, ga0, ga1, a, b, gs0, gs1, o, bias):
        ga, gs = _cat([ga0, ga1]), _cat([gs0, gs1])
        sa, dsa = _silu_and_grad(ga)
        ss, dss = _silu_and_grad(gs)
        a, sb = a + bias[:, :D_SSM], _sig(b + bias[:, D_SSM:])
        dy2 = dys * ss
        dglu = _cat([dy2 * sb, dy2 * a * sb * (1.0 - sb)])
        return dya * sa, dya * o * dsa, dys * (a * sb) * dss, dglu, _colsum(dglu)
    do_b, dga_b, dgs_b, dglu_b, db_glu = _rowwise(
        merge_bwd, [(dcat, 0, D_ATTN), (dcat, D_ATTN, D_SSM)] + merge_rows, [b_glu],
        [(D_ATTN, BF16), (D_ATTN, BF16), (D_SSM, BF16), (2 * D_SSM, BF16)], [(1, 2 * D_SSM)], name="merge_bwd")
    dw_glu = _matmul(ge_b, dglu_b, ta=True, name="mm_d_w_glu")
    dge = _matmul(dglu_b, w_glu, tb=True, name="mm_d_ge")
    dy_s, = _rowwise(lambda dge, y: dge * _gelu_grad(y), [(dge, 0, D_SSM), (y_s, 0, D_SSM)], [], [(D_SSM, F32)], name="gelu_bwd")
    du_p, dwb, dwc, dare, daim, d_ssm_d = _ssm_bwd(u_p, _seg_perm(dy_s), *ssm)
    dc_re, dc_im = _from_wc(dwc)
    da_re, da_im, dlog_dt, db_re, db_im = disc_vjp((_from_lam(dare), _from_lam(daim)) + _from_wb(dwb))

    place = _grad_place()
    early_grads = [dw_glu, dw_out, dw_pg, dw_pp]
    early_sib, early_sums = _reduce_begin(_EARLY, early_grads, place, "early")
    dqr, dkr, dv, *early_chips = _attn_bwd(qr, kr, kr.T, vb, do_b, lse.reshape(N_HEADS, 1, L), _EARLY, early_sums)
    early_totals = _reduce_end(_EARLY, early_grads, early_sib, early_chips, place)
    dq_b, dk_b, dgq, dgk = _attn_prep_bwd(dqr, dkr, z, gq, gk, cos, sin)
    dz_b = _cat([dq_b, dk_b, dv.astype(BF16), dga_b, _seg_unperm(du_p).astype(BF16), dgs_b])
    dw_in = _matmul(hn_b, dz_b, ta=True, name="mm_d_w_in")
    dhn = _matmul(dz_b, w_in, tb=True, name="mm_d_hn")

    def norm_bwd(dhn, x, dh1, g):
        dx, dg = _rms_bwd(x, g, dhn)
        return dh1 + dx, _colsum(dg)
    grad_x, dg_mix = _rowwise(norm_bwd, [(a, 0, D_MODEL) for a in (dhn, x, dh1)], [g_mix], [(D_MODEL, F32)], [(1, D_MODEL)],
                              name="norm_mix_bwd")

    small_grads = {"norm_mix": dg_mix, "q_norm": dgq, "k_norm": dgk, "ssm_a_re": da_re, "ssm_a_im": da_im, "ssm_log_dt": dlog_dt,
                   "ssm_b_re": db_re, "ssm_b_im": db_im, "ssm_c_re": dc_re, "ssm_c_im": dc_im, "ssm_d": d_ssm_d,
                   "b_glu": db_glu, "norm_ple": dg_ple, "norm_final": dg_fin}
    return jnp.sum(loss_cols), grad_x, dw_in, early_totals, small_grads, place


_SMALL = ("norm_mix", "q_norm", "k_norm", "ssm_a_re", "ssm_a_im", "ssm_log_dt", "ssm_b_re", "ssm_b_im", "ssm_c_re", "ssm_c_im",
          "ssm_d", "b_glu", "norm_ple", "norm_final")
_WEIGHTS = ("norm_mix", "w_in", "q_norm", "k_norm", "ssm_a_re", "ssm_a_im", "ssm_log_dt", "ssm_b_re", "ssm_b_im", "ssm_c_re",
            "ssm_c_im", "ssm_d", "w_glu", "b_glu", "w_out", "norm_ple", "w_ple_gate", "w_ple_proj", "norm_final")


def _flat_small(d):
    flat = jnp.concatenate([d[n].reshape(-1).astype(F32) for n in _SMALL])
    return jnp.pad(flat, (0, N_SMALL - flat.shape[0]))


def _split_small(flat, like):
    out, off = {}, 0
    for n in _SMALL:
        sz = math.prod(like[n].shape)
        out[n] = flat[off:off + sz].reshape(like[n].shape)
        off += sz
    return out


def kernel(x, p, norm_mix, w_in, q_norm, k_norm, ssm_a_re, ssm_a_im, ssm_log_dt, ssm_b_re, ssm_b_im, ssm_c_re, ssm_c_im, ssm_d, w_glu, b_glu, w_out, norm_ple, w_ple_gate, w_ple_proj, norm_final, loss_target, m_norm_mix, m_w_in, m_q_norm, m_k_norm, m_ssm_a_re, m_ssm_a_im, m_ssm_log_dt, m_ssm_b_re, m_ssm_b_im, m_ssm_c_re, m_ssm_c_im, m_ssm_d, m_w_glu, m_b_glu, m_w_out, m_norm_ple, m_w_ple_gate, m_w_ple_proj, m_norm_final, v_norm_mix, v_w_in, v_q_norm, v_k_norm, v_ssm_a_re, v_ssm_a_im, v_ssm_log_dt, v_ssm_b_re, v_ssm_b_im, v_ssm_c_re, v_ssm_c_im, v_ssm_d, v_w_glu, v_b_glu, v_w_out, v_norm_ple, v_w_ple_gate, v_w_ple_proj, v_norm_final):
    w = dict(norm_mix=norm_mix, w_in=w_in, q_norm=q_norm, k_norm=k_norm, ssm_a_re=ssm_a_re, ssm_a_im=ssm_a_im,
             ssm_log_dt=ssm_log_dt, ssm_b_re=ssm_b_re, ssm_b_im=ssm_b_im, ssm_c_re=ssm_c_re, ssm_c_im=ssm_c_im, ssm_d=ssm_d,
             w_glu=w_glu, b_glu=b_glu, w_out=w_out, norm_ple=norm_ple, w_ple_gate=w_ple_gate, w_ple_proj=w_ple_proj,
             norm_final=norm_final)
    m = dict(norm_mix=m_norm_mix, w_in=m_w_in, q_norm=m_q_norm, k_norm=m_k_norm, ssm_a_re=m_ssm_a_re, ssm_a_im=m_ssm_a_im,
             ssm_log_dt=m_ssm_log_dt, ssm_b_re=m_ssm_b_re, ssm_b_im=m_ssm_b_im, ssm_c_re=m_ssm_c_re, ssm_c_im=m_ssm_c_im,
             ssm_d=m_ssm_d, w_glu=m_w_glu, b_glu=m_b_glu, w_out=m_w_out, norm_ple=m_norm_ple, w_ple_gate=m_w_ple_gate,
             w_ple_proj=m_w_ple_proj, norm_final=m_norm_final)
    v = dict(norm_mix=v_norm_mix, w_in=v_w_in, q_norm=v_q_norm, k_norm=v_k_norm, ssm_a_re=v_ssm_a_re, ssm_a_im=v_ssm_a_im,
             ssm_log_dt=v_ssm_log_dt, ssm_b_re=v_ssm_b_re, ssm_b_im=v_ssm_b_im, ssm_c_re=v_ssm_c_re, ssm_c_im=v_ssm_c_im,
             ssm_d=v_ssm_d, w_glu=v_w_glu, b_glu=v_b_glu, w_out=v_w_out, norm_ple=v_norm_ple, w_ple_gate=v_w_ple_gate,
             w_ple_proj=v_w_ple_proj, norm_final=v_norm_final)
    big_names = [n for n, _, _, _ in _BIG]

    small = {n: w[n] for n in _SMALL}
    loss_part, grad_x, dw_in, early_totals, small_grads, place = _forward_backward(
        x[0], p[0, 0].astype(BF16), loss_target[0], [w[n][0].astype(BF16) for n in big_names], small)
    loss = lax.psum(loss_part, ("x", "y", "c"))

    late_grads = [dw_in, _flat_small(small_grads).reshape(8 * SMALL_ROWS, SMALL_W)]
    late_sib, late_sums = _reduce_begin(_LATE, late_grads, place, "late")
    late_totals = _reduce_end(_LATE, late_grads, late_sib, _grad_chip_exchange(_LATE, late_sums), place)
    *big_red, small_red = _grad_final_exchange([late_totals[0]] + early_totals + [late_totals[1]])
    grads = _split_small(small_red.reshape(-1), w)
    delta, new_m, new_v = {}, {}, {}
    for n, g in zip(big_names, big_red):
        grads[n] = g[None]
        d_, m_, v_ = _adamw(w[n][0], g, m[n][0], v[n][0], "adamw_" + n)
        delta[n], new_m[n], new_v[n] = d_[None], m_[None], v_[None]
    d_, m_, v_ = _adamw(*[a.reshape(-1, SMALL_W) for a in (_flat_small(w), small_red, _flat_small(m), _flat_small(v))], "adamw_small")
    delta.update(_split_small(d_.reshape(-1), w))
    new_m.update(_split_small(m_.reshape(-1), w))
    new_v.update(_split_small(v_.reshape(-1), w))
    return (loss, grad_x[None], *[grads[n] for n in _WEIGHTS], *[delta[n] for n in _WEIGHTS],
            *[new_m[n] for n in _WEIGHTS], *[new_v[n] for n in _WEIGHTS])
```

```python
import functools
import math

import jax
import jax.numpy as jnp
import numpy as np
from jax import lax
from jax.experimental import pallas as pl
from jax.experimental.pallas import tpu as pltpu

D_MODEL = 2048
GRID_W = 64
PLE_DIM = 256
D_ATTN = 1024
N_HEADS = 8
N_KV = 2
HEAD_DIM = 128
ROPE_THETA = 10000.0
D_SSM = 1024
SSM_H = 16
SSM_G = 64
SSM_P = 64
D_KV = N_KV * HEAD_DIM
D_IN = 2 * D_ATTN + 2 * D_KV + 2 * D_SSM
EPS = 1e-6
Z_Q, Z_K, Z_V, Z_GA, Z_U, Z_GS = 0, 1024, 1280, 1536, 2560, 3584

ADAM_LR, ADAM_B1, ADAM_B2, ADAM_EPS, ADAM_WD, ADAM_STEP = 0.001, 0.9, 0.999, 1e-08, 0.01, 10

N_CHIPS = 4
VMEM_LIMIT_V7X = 56 * 1024 * 1024
F32 = jnp.float32
BF16 = jnp.bfloat16


def _params(sem, vmem=VMEM_LIMIT_V7X):
    return pltpu.CompilerParams(dimension_semantics=sem, vmem_limit_bytes=vmem)


def _matmul(a, b, *, ta=False, tb=False, out_dtype=F32, tm=1024, tn=512, name, exchange=None):
    M, K = (a.shape[1], a.shape[0]) if ta else a.shape
    N = b.shape[0] if tb else b.shape[1]
    tm, tn = min(tm, M), min(tn, N)
    assert M % tm == 0 and N % tn == 0, (name, M, N, K)
    dims = (((0 if ta else 1,), (1 if tb else 0,)), ((), ()))
    ex_ts, ex_sums = exchange if exchange else ((), ())
    n_ex = len(ex_ts)
    gm, gn = M // tm, N // tn

    def body(a_ref, b_ref, *rest):
        o_ref = rest[n_ex]
        if n_ex:
            ex = _ChipExchange(ex_ts, rest[:n_ex], rest[n_ex + 1:2 * n_ex + 1], rest[2 * n_ex + 1:])
            step = pl.program_id(0) * gn + pl.program_id(1)
            pl.when(step == 0)(ex.start)
        o_ref[...] = lax.dot_general(a_ref[...], b_ref[...], dims, preferred_element_type=F32).astype(o_ref.dtype)
        if n_ex:
            pl.when(step == gm * gn - 1)(ex.finish)

    a_spec = pl.BlockSpec((K, tm), lambda i, j: (0, i)) if ta else pl.BlockSpec((tm, K), lambda i, j: (i, 0))
    b_spec = pl.BlockSpec((tn, K), lambda i, j: (j, 0)) if tb else pl.BlockSpec((K, tn), lambda i, j: (0, j))
    res = pl.pallas_call(
        body, name=name,
        out_shape=[jax.ShapeDtypeStruct((M, N), out_dtype)] + (_ChipExchange.out_shape(ex_ts) if n_ex else []),
        grid=(gm, gn),
        in_specs=[a_spec, b_spec] + [_ANY] * n_ex,
        out_specs=[pl.BlockSpec((tm, tn), lambda i, j: (i, j))] + [_ANY] * n_ex,
        scratch_shapes=_ChipExchange.scratch(ex_ts) if n_ex else [],
        compiler_params=_params(("arbitrary", "arbitrary") if n_ex else ("parallel", "parallel")),
    )(a, b, *ex_sums)
    return res if n_ex else res[0]


def _rowwise(fn, rows, consts, outs, accs=(), *, tr=256, name):
    L = rows[0][0].shape[0]
    tr = math.gcd(tr, L)
    assert tr % 8 == 0 or tr == L, (name, L, tr)
    n_in, n_c, n_o, n_a = len(rows), len(consts), len(outs), len(accs)

    def body(*refs):
        ins = [r[...] for r in refs[:n_in + n_c]]
        res = fn(*ins)
        if not isinstance(res, (tuple, list)):
            res = (res,)
        o_refs = refs[n_in + n_c:n_in + n_c + n_o]
        a_refs = refs[n_in + n_c + n_o:]
        for r, v in zip(o_refs, res[:n_o]):
            r[...] = v.astype(r.dtype)
        if n_a:
            first = pl.program_id(0) == 0

            @pl.when(first)
            def _():
                for r, v in zip(a_refs, res[n_o:]):
                    r[...] = v.astype(F32)

            @pl.when(jnp.logical_not(first))
            def _():
                for r, v in zip(a_refs, res[n_o:]):
                    r[...] += v.astype(F32)

    in_specs = []
    for arr, off, w in rows:
        assert off % w == 0, (name, off, w)
        in_specs.append(pl.BlockSpec((tr, w), functools.partial(lambda i, c: (i, c), c=off // w)))
    for c in consts:
        in_specs.append(pl.BlockSpec(c.shape, lambda i: (0, 0)))
    out_shape = [jax.ShapeDtypeStruct((L, w), dt) for w, dt in outs] + [jax.ShapeDtypeStruct(s, F32) for s in accs]
    out_specs = [pl.BlockSpec((tr, w), lambda i: (i, 0)) for w, _ in outs] + [pl.BlockSpec(s, lambda i: (0, 0)) for s in accs]
    res = pl.pallas_call(
        body, name=name,
        out_shape=out_shape,
        grid=(L // tr,),
        in_specs=in_specs,
        out_specs=out_specs,
        compiler_params=_params(("arbitrary",) if n_a else ("parallel",)),
    )(*[r[0] for r in rows], *consts)
    return res


def _sig(x):
    return jax.nn.sigmoid(x)


def _silu_and_grad(x):
    s = _sig(x)
    return x * s, s * (1.0 + x * (1.0 - s))


_GELU_C = math.sqrt(2.0 / math.pi)


def _gelu(x):
    return 0.5 * x * (1.0 + jnp.tanh(_GELU_C * (x + 0.044715 * x * x * x)))


def _gelu_grad(x):
    t = jnp.tanh(_GELU_C * (x + 0.044715 * x * x * x))
    return 0.5 * (1.0 + t) + 0.5 * x * (1.0 - t * t) * _GELU_C * (1.0 + 3.0 * 0.044715 * x * x)


def _rms(x):
    return lax.rsqrt(jnp.mean(x * x, axis=-1, keepdims=True) + EPS)


def _rms_bwd(x, g, dy):
    r = _rms(x)
    n = x * r
    dn = dy * g
    return r * (dn - n * jnp.mean(dn * n, axis=-1, keepdims=True)), dy * n


def _colsum(v):
    return jnp.sum(v, axis=0, keepdims=True)


def _rope_partner(x):
    lane = lax.broadcasted_iota(jnp.int32, x.shape, x.ndim - 1)
    return jnp.where(lane % 64 < 32, pltpu.roll(x, 96, x.ndim - 1), pltpu.roll(x, 32, x.ndim - 1))


def _rope_tables(L):
    rows_n = L // GRID_W
    rows = jnp.repeat(jnp.arange(rows_n), GRID_W).astype(F32)
    cols = jnp.tile(jnp.arange(GRID_W), rows_n).astype(F32)
    n_freq = HEAD_DIM // 4
    inv_freq = ROPE_THETA ** (-jnp.arange(n_freq, dtype=F32) / n_freq)
    ar, ac = rows[:, None] * inv_freq[None, :], cols[:, None] * inv_freq[None, :]
    cos = jnp.concatenate([jnp.cos(ar), jnp.cos(ar), jnp.cos(ac), jnp.cos(ac)], axis=-1)
    sin = jnp.concatenate([-jnp.sin(ar), jnp.sin(ar), -jnp.sin(ac), jnp.sin(ac)], axis=-1)
    return cos, sin


def _heads(v):
    return [v[:, h * HEAD_DIM:(h + 1) * HEAD_DIM] for h in range(v.shape[1] // HEAD_DIM)]


def _attn_prep(z, q_norm, k_norm, cos, sin):
    def fn(q, k, v, cos, sin, gq, gk):
        def one(xh, g):
            xn = xh * _rms(xh) * g
            return xn * cos + _rope_partner(xn) * sin
        qr = jnp.concatenate([one(h, gq) for h in _heads(q)], axis=1)
        kr = jnp.concatenate([one(h, gk) for h in _heads(k)], axis=1)
        return qr, kr, v
    return _rowwise(fn, [(z, Z_Q, D_ATTN), (z, Z_K, D_KV), (z, Z_V, D_KV), (cos, 0, HEAD_DIM), (sin, 0, HEAD_DIM)],
                    [q_norm, k_norm], [(D_ATTN, BF16), (D_KV, BF16), (D_KV, BF16)], name="attn_prep")


def _attn_prep_bwd(dqr, dkr, z, q_norm, k_norm, cos, sin):
    def fn(dqr, dkr, q, k, cos, sin, gq, gk):
        def one(dyh, xh, g):
            dn = dyh * cos + _rope_partner(dyh * sin)
            return _rms_bwd(xh, g, dn)
        rq = [one(a, b, gq) for a, b in zip(_heads(dqr), _heads(q))]
        rk = [one(a, b, gk) for a, b in zip(_heads(dkr), _heads(k))]
        dq = jnp.concatenate([r[0] for r in rq], axis=1)
        dk = jnp.concatenate([r[0] for r in rk], axis=1)
        return dq, dk, _colsum(sum(r[1] for r in rq)), _colsum(sum(r[1] for r in rk))
    return _rowwise(fn, [(dqr, 0, D_ATTN), (dkr, 0, D_KV), (z, Z_Q, D_ATTN), (z, Z_K, D_KV), (cos, 0, HEAD_DIM), (sin, 0, HEAD_DIM)],
                    [q_norm, k_norm], [(D_ATTN, BF16), (D_KV, BF16)], [(1, HEAD_DIM), (1, HEAD_DIM)], name="attn_prep_bwd")


_QK_T = (((1,), (1,)), ((), ()))
_TA = (((0,), (0,)), ((), ()))
_REP = N_HEADS // N_KV


_EXP2_SCALE = HEAD_DIM ** -0.5 * math.log2(math.e)
ATTN_FWD_KEY_CHUNKS = 4
ATTN_BWD_KEY_CHUNKS = 8


def _attn_fwd(qr, kr, vb, g_ts, g_shards, *, tq=1024):
    L = qr.shape[0]
    tq = min(tq, L)
    kc = L // ATTN_FWD_KEY_CHUNKS
    n_g = len(g_ts)
    grid = (N_HEADS, L // tq)
    steps = grid[0] * grid[1]

    def body(q_ref, k_ref, v_ref, *rest):
        o_ref, lse_ref = rest[n_g:n_g + 2]
        g = _Gather(g_ts, rest[:n_g], rest[n_g + 2:2 * n_g + 2], rest[2 * n_g + 2:3 * n_g + 2], rest[3 * n_g + 2:])
        step = pl.program_id(0) * grid[1] + pl.program_id(1)
        pl.when(step == 0)(g.start)
        pl.when(step == (3 * steps) // 4)(g.forward)
        q = q_ref[...]
        m = jnp.full((tq, 1), -jnp.inf, F32)
        l = jnp.zeros((tq, 1), F32)
        o = jnp.zeros((tq, HEAD_DIM), F32)
        for c in range(ATTN_FWD_KEY_CHUNKS):
            ks = slice(c * kc, (c + 1) * kc)
            s = lax.dot_general(q, k_ref[ks, :], _QK_T, preferred_element_type=F32)
            m_new = jnp.maximum(m, jnp.max(s, axis=1, keepdims=True))
            a = jnp.exp2((m - m_new) * _EXP2_SCALE)
            p = jnp.exp2((s - m_new) * _EXP2_SCALE)
            l = a * l + jnp.sum(p, axis=1, keepdims=True)
            o = a * o + jnp.dot(p.astype(BF16), v_ref[ks, :], preferred_element_type=F32)
            m = m_new
        o_ref[...] = o * (1.0 / l)
        lse_ref[...] = m * _EXP2_SCALE + jnp.log2(l)
        pl.when(step == steps - 1)(g.finish)

    kv = pl.BlockSpec((L, HEAD_DIM), lambda h, i: (0, h // _REP))
    return pl.pallas_call(
        body, name="attn_fwd",
        out_shape=[jax.ShapeDtypeStruct((L, D_ATTN), F32), jax.ShapeDtypeStruct((N_HEADS, L, 1), F32)] +
                  [jax.ShapeDtypeStruct(_BIG[t][1], BF16) for t in g_ts],
        grid=grid,
        in_specs=[pl.BlockSpec((tq, HEAD_DIM), lambda h, i: (i, h)), kv, kv] + [_ANY] * n_g,
        out_specs=[pl.BlockSpec((tq, HEAD_DIM), lambda h, i: (i, h)),
                   pl.BlockSpec((None, tq, 1), lambda h, i: (h, i, 0))] + [_ANY] * n_g,
        scratch_shapes=_Gather.scratch(g_shards),
        compiler_params=_params(("arbitrary", "arbitrary")),
    )(qr, kr, vb, *g_shards)


def _attn_bwd(qr, kr, k_t, vb, do, lse, ex_ts, ex_sums, *, tq=512):
    L = qr.shape[0]
    tq = min(tq, L)
    scale = HEAD_DIM ** -0.5
    kc = L // ATTN_BWD_KEY_CHUNKS
    n_ex = len(ex_ts)
    grid = (N_KV, _REP, L // tq)

    def body(q_ref, k_ref, kt_ref, v_ref, do_ref, lse_ref, *rest):
        dq_ref, dk_ref, dv_ref = rest[n_ex:n_ex + 3]
        ex = _ChipExchange(ex_ts, rest[:n_ex], rest[n_ex + 3:2 * n_ex + 3], rest[2 * n_ex + 3:])
        step = (pl.program_id(0) * grid[1] + pl.program_id(1)) * grid[2] + pl.program_id(2)
        pl.when(step == 0)(ex.start)

        @pl.when((pl.program_id(1) == 0) & (pl.program_id(2) == 0))
        def _():
            dk_ref[...] = jnp.zeros_like(dk_ref)
            dv_ref[...] = jnp.zeros_like(dv_ref)

        q, do, lse = q_ref[...], do_ref[...], lse_ref[...]
        keys = [slice(c * kc, (c + 1) * kc) for c in range(ATTN_BWD_KEY_CHUNKS)]
        ps, dps = [], []
        for ks in keys:
            st = lax.dot_general(k_ref[ks, :], q, _QK_T, preferred_element_type=F32)
            p = jnp.exp2(st * _EXP2_SCALE - lse)
            dv_ref[ks, :] += jnp.dot(p.astype(BF16), do, preferred_element_type=F32)
            ps.append(p)
            dps.append(lax.dot_general(v_ref[ks, :], do, _QK_T, preferred_element_type=F32))
        delta = sum(jnp.sum(p * dp, axis=0, keepdims=True) for p, dp in zip(ps, dps))
        dq_t = 0.0
        for ks, p, dp in zip(keys, ps, dps):
            ds = (p * (dp - delta) * scale).astype(BF16)
            dk_ref[ks, :] += jnp.dot(ds, q, preferred_element_type=F32)
            dq_t = dq_t + jnp.dot(kt_ref[:, ks], ds, preferred_element_type=F32)
        dq_ref[...] = dq_t.T
        pl.when(step == grid[0] * grid[1] * grid[2] - 1)(ex.finish)

    head = lambda g, r, i: (i, g * _REP + r)
    kv = pl.BlockSpec((L, HEAD_DIM), lambda g, r, i: (0, g))
    return pl.pallas_call(
        body, name="attn_bwd",
        out_shape=[jax.ShapeDtypeStruct((L, D_ATTN), F32), jax.ShapeDtypeStruct((L, D_KV), F32), jax.ShapeDtypeStruct((L, D_KV), F32)] +
                  _ChipExchange.out_shape(ex_ts),
        grid=grid,
        in_specs=[pl.BlockSpec((tq, HEAD_DIM), head), kv,
                  pl.BlockSpec((HEAD_DIM, L), lambda g, r, i: (g, 0)), kv,
                  pl.BlockSpec((tq, HEAD_DIM), head),
                  pl.BlockSpec((None, 1, tq), lambda g, r, i: (g * _REP + r, 0, i))] + [_ANY] * n_ex,
        out_specs=[pl.BlockSpec((tq, HEAD_DIM), head), kv, kv] + [_ANY] * n_ex,
        scratch_shapes=_ChipExchange.scratch(ex_ts),
        compiler_params=_params(("arbitrary", "arbitrary", "arbitrary")),
    )(qr, kr, k_t, vb, do, lse, *ex_sums)


SSM_BLK = 8
SSM_NB = SSM_G // SSM_BLK
SSM_SEG = 8
SSM_UNROLL = 4


def _unrolled_loop(n, step, carry):
    u = SSM_UNROLL

    def trip(i, c):
        for j in range(u):
            c = step(i * u + j, c)
        return c
    carry = lax.fori_loop(0, n // u, trip, carry)
    for t in range(n - n % u, n):
        carry = step(jnp.int32(t), carry)
    return carry


def _cplx_pow2(a, b, n):
    for _ in range(int(math.log2(n))):
        a, b = a * a - b * b, 2.0 * a * b
    return a, b


def _seg_scan(ref, a, b, T, reverse):
    npair = len(a)
    zero = jnp.zeros((SSM_SEG, 128), F32)

    def make_step(store):
        def step(t, carry):
            lt = (T - 1 - t) if reverse else t
            row = pl.multiple_of(lt * SSM_SEG, SSM_SEG)
            blk = ref[pl.ds(row, SSM_SEG), :]
            new = []
            for q in range(npair):
                re, im = carry[2 * q], carry[2 * q + 1]
                nre = a[q] * re - b[q] * im + blk[:, q * 256:q * 256 + 128]
                nim = a[q] * im + b[q] * re + blk[:, q * 256 + 128:q * 256 + 256]
                new += [nre, nim]
            if store:
                ref[pl.ds(row, SSM_SEG), :] = jnp.concatenate(new, axis=1)
            return tuple(new)
        return step

    ends = _unrolled_loop(T, make_step(False), (zero,) * (2 * npair))
    sub = lax.broadcasted_iota(jnp.int32, (SSM_SEG, 128), 0)
    keep = (sub != SSM_SEG - 1) if reverse else (sub != 0)
    shift = (SSM_SEG - 1) if reverse else 1
    init = []
    for q in range(npair):
        pa, pb = _cplx_pow2(a[q], b[q], T)
        xr, xi = zero, zero
        for _ in range(SSM_SEG - 1):
            fr = ends[2 * q] + pa * xr - pb * xi
            fi = ends[2 * q + 1] + pa * xi + pb * xr
            xr = jnp.where(keep, pltpu.roll(fr, shift, 0), 0.0)
            xi = jnp.where(keep, pltpu.roll(fi, shift, 0), 0.0)
        init += [xr, xi]
    _unrolled_loop(T, make_step(True), tuple(init))
    return init


SSM_BW = SSM_BLK * SSM_H
SSM_SW = SSM_BLK * 2 * SSM_P
SSM_NPAIR = SSM_BLK // 2


def _seg_perm(a):
    L, C = a.shape
    return a.reshape(SSM_SEG, L // SSM_SEG, C).transpose(1, 0, 2).reshape(L, C)


def _seg_unperm(a):
    L, C = a.shape
    return a.reshape(L // SSM_SEG, SSM_SEG, C).transpose(1, 0, 2).reshape(L, C)


def _lam_rows(are_ref, aim_ref, d):
    a = [jnp.broadcast_to(are_ref[d, j:j + 1, :], (SSM_SEG, 128)) for j in range(SSM_NPAIR)]
    b = [jnp.broadcast_to(aim_ref[d, j:j + 1, :], (SSM_SEG, 128)) for j in range(SSM_NPAIR)]
    return a, b


def _ssm_fwd(u_p, wb, wc, are, aim, dvec):
    L = u_p.shape[0]
    T = L // SSM_SEG
    RC = min(512, L)

    def body(u_ref, wb_ref, wc_ref, are_ref, aim_ref, d_ref, y_ref, x_scr):
        y_ref[...] = u_ref[...] * d_ref[...]
        for d in range(2):
            def bu_chunk(c, _):
                rows = pl.ds(pl.multiple_of(c * RC, RC), RC)
                x_scr[rows, :] = jnp.dot(u_ref[rows, :].astype(BF16), wb_ref[d], preferred_element_type=F32)
                return 0
            lax.fori_loop(0, L // RC, bu_chunk, 0)
            a, b = _lam_rows(are_ref, aim_ref, d)
            _seg_scan(x_scr, a, b, T, reverse=(d == 1))

            def y_chunk(c, _):
                rows = pl.ds(pl.multiple_of(c * RC, RC), RC)
                y_ref[rows, :] += jnp.dot(x_scr[rows, :].astype(BF16), wc_ref[d], preferred_element_type=F32)
                return 0
            lax.fori_loop(0, L // RC, y_chunk, 0)

    blk4 = lambda g: (g, 0, 0, 0)
    return pl.pallas_call(
        body, name="ssm_fwd",
        out_shape=jax.ShapeDtypeStruct((L, D_SSM), F32),
        grid=(SSM_NB,),
        in_specs=[pl.BlockSpec((L, SSM_BW), lambda g: (0, g)),
                  pl.BlockSpec((None, 2, SSM_BW, SSM_SW), blk4),
                  pl.BlockSpec((None, 2, SSM_SW, SSM_BW), blk4),
                  pl.BlockSpec((None, 2, SSM_NPAIR, 128), blk4),
                  pl.BlockSpec((None, 2, SSM_NPAIR, 128), blk4),
                  pl.BlockSpec((1, SSM_BW), lambda g: (0, g))],
        out_specs=pl.BlockSpec((L, SSM_BW), lambda g: (0, g)),
        scratch_shapes=[pltpu.VMEM((L, SSM_SW), F32)],
        compiler_params=_params(("parallel",)),
    )(u_p, wb, wc, are, aim, dvec)


def _ssm_bwd(u_p, dy_p, wb, wc, are, aim, dvec):
    L = u_p.shape[0]
    T = L // SSM_SEG
    RC = min(512, L)

    def lam_acc(acc, sb, xb):
        new = []
        for q in range(SSM_NPAIR):
            sr, si = sb[:, q * 256:q * 256 + 128], sb[:, q * 256 + 128:q * 256 + 256]
            xr, xi = xb[:, q * 256:q * 256 + 128], xb[:, q * 256 + 128:q * 256 + 256]
            new += [acc[2 * q] + sr * xr + si * xi, acc[2 * q + 1] + si * xr - sr * xi]
        return tuple(new)

    def body(u_ref, dy_ref, wb_ref, wc_ref, are_ref, aim_ref, d_ref,
             du_ref, dwb_ref, dwc_ref, dare_ref, daim_ref, dd_ref, x_scr, s_scr):
        du_ref[...] = dy_ref[...] * d_ref[...]
        dd_ref[...] = _colsum(dy_ref[...] * u_ref[...])
        dwb_ref[...] = jnp.zeros_like(dwb_ref)
        dwc_ref[...] = jnp.zeros_like(dwc_ref)
        for d in range(2):
            rev = d == 1

            def in_chunk(c, _):
                rows = pl.ds(pl.multiple_of(c * RC, RC), RC)
                x_scr[rows, :] = jnp.dot(u_ref[rows, :].astype(BF16), wb_ref[d], preferred_element_type=F32)
                s_scr[rows, :] = lax.dot_general(dy_ref[rows, :].astype(BF16), wc_ref[d], _QK_T, preferred_element_type=F32)
                return 0
            lax.fori_loop(0, L // RC, in_chunk, 0)
            a, b = _lam_rows(are_ref, aim_ref, d)
            x_in = _seg_scan(x_scr, a, b, T, reverse=rev)
            _seg_scan(s_scr, a, [-v for v in b], T, reverse=not rev)

            def lam_step(t, acc):
                lt = (T - 2 - t) if rev else (t + 1)
                srow = pl.multiple_of(lt * SSM_SEG, SSM_SEG)
                xrow = pl.multiple_of((lt + 1 if rev else lt - 1) * SSM_SEG, SSM_SEG)
                return lam_acc(acc, s_scr[pl.ds(srow, SSM_SEG), :], x_scr[pl.ds(xrow, SSM_SEG), :])

            edge = pl.ds(((T - 1) if rev else 0) * SSM_SEG, SSM_SEG)
            acc0 = lam_acc((jnp.zeros((SSM_SEG, 128), F32),) * (2 * SSM_NPAIR), s_scr[edge, :], jnp.concatenate(x_in, axis=1))
            acc = _unrolled_loop(T - 1, lam_step, acc0)
            for q in range(SSM_NPAIR):
                dare_ref[d, q:q + 1, :] = _colsum(acc[2 * q])
                daim_ref[d, q:q + 1, :] = _colsum(acc[2 * q + 1])

            def out_chunk(c, _):
                rows = pl.ds(pl.multiple_of(c * RC, RC), RC)
                xs, ss = x_scr[rows, :].astype(BF16), s_scr[rows, :].astype(BF16)
                uu, dd = u_ref[rows, :].astype(BF16), dy_ref[rows, :].astype(BF16)
                dwc_ref[d] += lax.dot_general(xs, dd, _TA, preferred_element_type=F32)
                dwb_ref[d] += lax.dot_general(uu, ss, _TA, preferred_element_type=F32)
                du_ref[rows, :] += lax.dot_general(ss, wb_ref[d], _QK_T, preferred_element_type=F32)
                return 0
            lax.fori_loop(0, L // RC, out_chunk, 0)

    blk4 = lambda g: (g, 0, 0, 0)
    chan = pl.BlockSpec((L, SSM_BW), lambda g: (0, g))
    par_specs = [pl.BlockSpec((None, 2, SSM_BW, SSM_SW), blk4),
                 pl.BlockSpec((None, 2, SSM_SW, SSM_BW), blk4),
                 pl.BlockSpec((None, 2, SSM_NPAIR, 128), blk4),
                 pl.BlockSpec((None, 2, SSM_NPAIR, 128), blk4),
                 pl.BlockSpec((1, SSM_BW), lambda g: (0, g))]
    return pl.pallas_call(
        body, name="ssm_bwd",
        out_shape=[jax.ShapeDtypeStruct((L, D_SSM), F32),
                   jax.ShapeDtypeStruct((SSM_NB, 2, SSM_BW, SSM_SW), F32),
                   jax.ShapeDtypeStruct((SSM_NB, 2, SSM_SW, SSM_BW), F32),
                   jax.ShapeDtypeStruct((SSM_NB, 2, SSM_NPAIR, 128), F32),
                   jax.ShapeDtypeStruct((SSM_NB, 2, SSM_NPAIR, 128), F32),
                   jax.ShapeDtypeStruct((1, D_SSM), F32)],
        grid=(SSM_NB,),
        in_specs=[chan, chan] + par_specs,
        out_specs=[chan] + par_specs,
        scratch_shapes=[pltpu.VMEM((L, SSM_SW), F32), pltpu.VMEM((L, SSM_SW), F32)],
        compiler_params=_params(("parallel",)),
    )(u_p, dy_p, wb, wc, are, aim, dvec)


def _ssm_disc(a_re, a_im, log_dt, b_re, b_im):
    lam = lax.complex(jnp.minimum(a_re, -1e-4), a_im)
    dt = jnp.exp(log_dt)[..., None]
    lam_bar = jnp.exp(lam * dt)
    b_bar = ((lam_bar - 1.0) / lam)[..., None] * lax.complex(b_re, b_im)
    return jnp.real(lam_bar), jnp.imag(lam_bar), jnp.real(b_bar), jnp.imag(b_bar)


_EYE8 = np.eye(SSM_BLK, dtype=np.float32)


def _to_wb(bb_re, bb_im):
    bb = jnp.stack([bb_re, bb_im], axis=2).reshape(2, SSM_NB, SSM_BLK, 2, SSM_P, SSM_H)
    t = bb.transpose(1, 0, 5, 2, 3, 4)[:, :, None] * _EYE8[None, None, :, None, :, None, None]
    t = t.reshape(SSM_NB, 2, SSM_BLK, SSM_H, 4, 2, 2, SSM_P).transpose(0, 1, 2, 3, 4, 6, 5, 7)
    return t.reshape(SSM_NB, 2, SSM_BW, SSM_SW)


def _from_wb(dwb):
    t = dwb.reshape(SSM_NB, 2, SSM_BLK, SSM_H, 4, 2, 2, SSM_P).transpose(0, 1, 2, 3, 4, 6, 5, 7)
    t = t.reshape(SSM_NB, 2, SSM_BLK, SSM_H, SSM_BLK, 2, SSM_P)
    bb = jnp.sum(t * _EYE8[None, None, :, None, :, None, None], axis=2)
    bb = bb.transpose(1, 0, 3, 4, 5, 2).reshape(2, SSM_G, 2, SSM_P, SSM_H)
    return bb[:, :, 0], bb[:, :, 1]


def _to_wc(c_re, c_im):
    cc = jnp.stack([c_re, -c_im], axis=2).reshape(2, SSM_NB, SSM_BLK, 2, SSM_H, SSM_P)
    t = cc.transpose(1, 0, 2, 3, 5, 4)[:, :, :, :, :, None] * _EYE8.T[None, None, :, None, None, :, None]
    t = t.reshape(SSM_NB, 2, 4, 2, 2, SSM_P, SSM_BLK, SSM_H).transpose(0, 1, 2, 4, 3, 5, 6, 7)
    return t.reshape(SSM_NB, 2, SSM_SW, SSM_BW)


def _from_wc(dwc):
    t = dwc.reshape(SSM_NB, 2, 4, 2, 2, SSM_P, SSM_BLK, SSM_H).transpose(0, 1, 2, 4, 3, 5, 6, 7)
    t = t.reshape(SSM_NB, 2, SSM_BLK, 2, SSM_P, SSM_BLK, SSM_H)
    cc = jnp.sum(t * _EYE8.T[None, None, :, None, None, :, None], axis=5)
    cc = cc.transpose(1, 0, 2, 3, 5, 4).reshape(2, SSM_G, 2, SSM_H, SSM_P)
    return cc[:, :, 0], -cc[:, :, 1]


def _to_lam(v):
    return v.reshape(2, SSM_NB, SSM_NPAIR, 128).transpose(1, 0, 2, 3)


def _from_lam(v):
    return v.transpose(1, 0, 2, 3).reshape(2, SSM_G, SSM_P)


_MESH = pl.DeviceIdType.MESH
_ANY = pl.BlockSpec(memory_space=pl.ANY)
_BIG = (("w_in", (D_MODEL, D_IN), 1, D_IN // N_CHIPS),
        ("w_glu", (D_SSM, 2 * D_SSM), 1, 2 * D_SSM // N_CHIPS),
        ("w_out", (D_ATTN + D_SSM, D_MODEL), 0, (D_ATTN + D_SSM) // N_CHIPS),
        ("w_ple_gate", (D_MODEL, D_MODEL), 0, D_MODEL // N_CHIPS),
        ("w_ple_proj", (PLE_DIM, D_MODEL), 1, D_MODEL // N_CHIPS))


def _place():
    x, y, c = lax.axis_index("x"), lax.axis_index("y"), lax.axis_index("c")
    return x, y, c, [(1 - x, y), (x, 1 - y), (1 - x, 1 - y)]


class _Gather:
    def __init__(self, ts, srcs, dsts, stage, sems):
        self.ts, self.srcs, self.dsts, self.stage = ts, srcs, dsts, stage
        self.send_sems, self.recv_sems, self.fwd_send_sems, self.fwd_recv_sems, self.loc_sems = sems
        self.x, self.y, self.c, self.chips = _place()
        self.n = len(ts)

    @staticmethod
    def scratch(shards):
        sems = pltpu.SemaphoreType.DMA((3, len(shards)))
        return [pltpu.VMEM(s.shape, BF16) for s in shards] + [sems, sems, sems, sems, pltpu.SemaphoreType.DMA((len(shards),))]

    def _shard_of(self, i, kk):
        _, _, axis, sz = _BIG[self.ts[i]]
        sl = pl.ds(pl.multiple_of(kk * sz, sz), sz)
        return self.dsts[i].at[:, sl] if axis == 1 else self.dsts[i].at[sl, :]

    @staticmethod
    def _half_of(ref, cc):
        n = ref.shape[0] // 2
        return ref.at[pl.ds(pl.multiple_of(cc * n, n), n), :]

    def _ici(self, j, i, kk):
        px, py = self.chips[j]
        return pltpu.make_async_remote_copy(
            src_ref=self._half_of(self.srcs[i], self.c), dst_ref=self._half_of(self._shard_of(i, kk), self.c),
            send_sem=self.send_sems.at[j, i], recv_sem=self.recv_sems.at[j, i],
            device_id=(px, py, self.c), device_id_type=_MESH)

    def _forward(self, j, i, kk, cc):
        part = self._half_of(self._shard_of(i, kk), cc)
        return pltpu.make_async_remote_copy(
            src_ref=part, dst_ref=part, send_sem=self.fwd_send_sems.at[j, i], recv_sem=self.fwd_recv_sems.at[j, i],
            device_id=(self.x, self.y, 1 - self.c), device_id_type=_MESH)

    def _load(self, i):
        return pltpu.make_async_copy(self.srcs[i], self.stage[i], self.loc_sems.at[i])

    def _place_own(self, i):
        return pltpu.make_async_copy(self.stage[i], self._shard_of(i, 2 * self.x + self.y), self.loc_sems.at[i])

    def _peers(self):
        return [(i, j, 2 * px + py) for i in range(self.n) for j, (px, py) in enumerate(self.chips)]

    def start(self):
        for i in range(self.n):
            self._load(i).start()
        for i, j, _ in self._peers():
            self._ici(j, i, 2 * self.x + self.y).start()

    def forward(self):
        for i in range(self.n):
            self._load(i).wait()
            self._place_own(i).start()
        for i, j, kk in self._peers():
            self._ici(j, i, kk).wait_recv()
            self._forward(j, i, kk, self.c).start()

    def finish(self):
        for i, j, kk in self._peers():
            self._forward(j, i, kk, 1 - self.c).wait_recv()
        for i, j, kk in self._peers():
            self._ici(j, i, kk).wait_send()
            self._forward(j, i, kk, self.c).wait_send()
        for i in range(self.n):
            self._place_own(i).wait()


def _gather_weights(ts, shards):
    n = len(ts)

    def body(*refs):
        g = _Gather(ts, refs[:n], refs[n:2 * n], refs[2 * n:3 * n], refs[3 * n:])
        g.start()
        g.forward()
        g.finish()

    return pl.pallas_call(
        body, name="gather_weights",
        out_shape=[jax.ShapeDtypeStruct(_BIG[t][1], BF16) for t in ts],
        in_specs=[_ANY] * n, out_specs=[_ANY] * n,
        scratch_shapes=_Gather.scratch(shards),
        compiler_params=pltpu.CompilerParams(vmem_limit_bytes=VMEM_LIMIT_V7X),
    )(*shards)


SMALL_W = 1024
SMALL_ROWS = 72
N_SMALL = 8 * SMALL_ROWS * SMALL_W
_RED = tuple((shape, ax, (shape[0] // 2, sz) if ax == 1 else (sz // 2, shape[1]), BF16) for _, shape, ax, sz in _BIG) + \
    (((8 * SMALL_ROWS, SMALL_W), 0, (SMALL_ROWS, SMALL_W), F32),)
_RED_TR = 128


def _piece(ref, t, kk, cc):
    _, ax, (pr, pc), _ = _RED[t]
    if ax == 1:
        return ref.at[pl.ds(pl.multiple_of(cc * pr, pr), pr), pl.ds(pl.multiple_of(kk * pc, pc), pc)]
    return ref.at[pl.ds(pl.multiple_of((2 * kk + cc) * pr, pr), pr), :]


def _half_shape(t):
    shape, ax, (pr, pc), _ = _RED[t]
    return (pr, shape[1]) if ax == 1 else (N_CHIPS * pr, pc)


def _piece_in_half(ref, t, kk):
    _, ax, (pr, pc), _ = _RED[t]
    return ref.at[:, pl.ds(pl.multiple_of(kk * pc, pc), pc)] if ax == 1 else ref.at[pl.ds(pl.multiple_of(kk * pr, pr), pr), :]


def _grad_sibling_exchange(ts, grads, name):
    n = len(ts)
    n_dma = sum(1 if _RED[t][1] == 1 else N_CHIPS for t in ts)

    def body(*refs):
        srcs, dsts, (send_sems, recv_sems) = refs[:n], refs[n:2 * n], refs[2 * n:]
        x, y, c, _ = _place()
        pairs = []
        for i, t in enumerate(ts):
            _, ax, (pr, _), _ = _RED[t]
            if ax == 1:
                pairs.append((srcs[i].at[pl.ds(pl.multiple_of((1 - c) * pr, pr), pr), :], dsts[i]))
            else:
                pairs += [(_piece(srcs[i], t, kk, 1 - c), _piece_in_half(dsts[i], t, kk)) for kk in range(N_CHIPS)]
        cps = [pltpu.make_async_remote_copy(src_ref=s, dst_ref=d, send_sem=send_sems.at[i], recv_sem=recv_sems.at[i],
                                            device_id=(x, y, 1 - c), device_id_type=_MESH) for i, (s, d) in enumerate(pairs)]
        for cp in cps:
            cp.start()
        for cp in cps:
            cp.wait()

    return pl.pallas_call(
        body, name=name,
        out_shape=[jax.ShapeDtypeStruct(_half_shape(t), F32) for t in ts],
        in_specs=[_ANY] * n, out_specs=[_ANY] * n,
        scratch_shapes=[pltpu.SemaphoreType.DMA((n_dma,)), pltpu.SemaphoreType.DMA((n_dma,))],
    )(*grads)


def _chip_sum(t, g, rs, place):
    shape, ax, (pr, pc), dt = _RED[t]
    W = shape[1]
    tr = min(pr, _RED_TR)
    nb = pr // tr

    def body(place_ref, g_ref, rs_ref, o_ref):
        o_ref[...] = (g_ref[...] + rs_ref[...]).astype(o_ref.dtype)

    return pl.pallas_call(
        body, name="grad_chip_sum_%d" % t,
        out_shape=jax.ShapeDtypeStruct(rs.shape, dt),
        grid_spec=pltpu.PrefetchScalarGridSpec(
            num_scalar_prefetch=1, grid=(1 if ax == 1 else N_CHIPS, nb),
            in_specs=[pl.BlockSpec((tr, W), lambda kk, i, pr_: ((2 * kk + pr_[0]) * nb + i, 0)),
                      pl.BlockSpec((tr, W), lambda kk, i, pr_: (kk * nb + i, 0))],
            out_specs=pl.BlockSpec((tr, W), lambda kk, i, pr_: (kk * nb + i, 0))),
        compiler_params=_params(("parallel", "parallel")),
    )(place, g, rs)


class _ChipExchange:
    def __init__(self, ts, srcs, dsts, sems):
        self.send_sems, self.recv_sems = sems
        x, y, c, chips = _place()
        self.copies = lambda: [
            pltpu.make_async_remote_copy(src_ref=_piece_in_half(srcs[i], t, 2 * px + py), dst_ref=dsts[i].at[j],
                                         send_sem=self.send_sems.at[j, i], recv_sem=self.recv_sems.at[j, i],
                                         device_id=(px, py, c), device_id_type=_MESH)
            for i, t in enumerate(ts) for j, (px, py) in enumerate(chips)]

    @staticmethod
    def scratch(ts):
        return [pltpu.SemaphoreType.DMA((3, len(ts))), pltpu.SemaphoreType.DMA((3, len(ts)))]

    @staticmethod
    def out_shape(ts):
        return [jax.ShapeDtypeStruct((3,) + _RED[t][2], _RED[t][3]) for t in ts]

    def start(self):
        for cp in self.copies():
            cp.start()

    def finish(self):
        for cp in self.copies():
            cp.wait()


def _grad_chip_exchange(ts, sums):
    n = len(ts)

    def body(*refs):
        ex = _ChipExchange(ts, refs[:n], refs[n:2 * n], refs[2 * n:])
        ex.start()
        ex.finish()

    return pl.pallas_call(
        body, name="grad_chip_exchange",
        out_shape=_ChipExchange.out_shape(ts),
        in_specs=[_ANY] * n, out_specs=[_ANY] * n,
        scratch_shapes=_ChipExchange.scratch(ts),
    )(*sums)


def _total_sum(t, g, rs, rc, place):
    shape, ax, (pr, pc), _ = _RED[t]
    tr = min(pr, _RED_TR)
    nb = pr // tr
    small = t == len(_RED) - 1

    def body(place_ref, g_ref, rs_ref, rc_ref, o_ref):
        o_ref[...] = (g_ref[...] + rs_ref[...]) + rc_ref[0].astype(F32) + rc_ref[1].astype(F32) + rc_ref[2].astype(F32)

    if ax == 1:
        g_map = lambda i, pr_: (pr_[0] * nb + i, pr_[1])
        rs_map = lambda i, pr_: (i, pr_[1])
    else:
        g_map = lambda i, pr_: ((2 * pr_[1] + pr_[0]) * nb + i, 0)
        rs_map = lambda i, pr_: (pr_[1] * nb + i, 0)
    o_map = (lambda i, pr_: ((2 * pr_[1] + pr_[0]) * nb + i, 0)) if small else (lambda i, pr_: (pr_[0] * nb + i, 0))
    return pl.pallas_call(
        body, name="grad_total_sum_%d" % t,
        out_shape=jax.ShapeDtypeStruct(((8 if small else 2) * pr, pc), F32),
        grid_spec=pltpu.PrefetchScalarGridSpec(
            num_scalar_prefetch=1, grid=(nb,),
            in_specs=[pl.BlockSpec((tr, pc), g_map), pl.BlockSpec((tr, pc), rs_map),
                      pl.BlockSpec((3, tr, pc), lambda i, pr_: (0, i, 0))],
            out_specs=pl.BlockSpec((tr, pc), o_map)),
        compiler_params=_params(("parallel",)),
    )(place, g, rs, rc)


def _grad_final_exchange(totals):
    n = len(_RED)
    nb = n - 1

    def body(*refs):
        srcs, dsts, (send_sems, recv_sems) = refs[:n], refs[n:2 * n], refs[2 * n:]
        x, y, c, chips = _place()
        me = 4 * x + 2 * y + c
        others = [(x, y, 1 - c)] + [(px, py, cc) for (px, py) in chips for cc in (c, 1 - c)]

        def half(ref, t, cc):
            pr = _RED[t][2][0]
            return ref.at[pl.ds(pl.multiple_of(cc * pr, pr), pr), :]

        def eighth(ref, dev):
            return ref.at[pl.ds(pl.multiple_of(dev * SMALL_ROWS, SMALL_ROWS), SMALL_ROWS), :]

        def big_copy(t, cc):
            return pltpu.make_async_remote_copy(src_ref=half(srcs[t], t, cc), dst_ref=half(dsts[t], t, cc), send_sem=send_sems.at[t],
                                                recv_sem=recv_sems.at[t], device_id=others[0], device_id_type=_MESH)

        def small_copy(i, dev):
            return pltpu.make_async_remote_copy(src_ref=eighth(srcs[nb], dev), dst_ref=eighth(dsts[nb], dev),
                                                send_sem=send_sems.at[nb + i], recv_sem=recv_sems.at[nb + i],
                                                device_id=others[i], device_id_type=_MESH)

        sends = [big_copy(t, c) for t in range(nb)] + [small_copy(i, me) for i in range(7)]
        for cp in sends:
            cp.start()
        for t in range(nb):
            big_copy(t, 1 - c).wait_recv()
        for i, (px, py, pc) in enumerate(others):
            small_copy(i, 4 * px + 2 * py + pc).wait_recv()
        for cp in sends:
            cp.wait_send()

    return pl.pallas_call(
        body, name="grad_final_exchange",
        out_shape=[jax.ShapeDtypeStruct(a.shape, F32) for a in totals],
        in_specs=[_ANY] * n, out_specs=[_ANY] * n,
        input_output_aliases={t: t for t in range(n)},
        scratch_shapes=[pltpu.SemaphoreType.DMA((nb + 7,)), pltpu.SemaphoreType.DMA((nb + 7,))],
    )(*totals)


def _grad_place():
    return jnp.stack([lax.axis_index("c"), 2 * lax.axis_index("x") + lax.axis_index("y")]).astype(jnp.int32)


def _reduce_begin(ts, grads, place, tag):
    from_sibling = _grad_sibling_exchange(ts, grads, "grad_sibling_exchange_" + tag)
    return from_sibling, [_chip_sum(t, g, r, place) for t, g, r in zip(ts, grads, from_sibling)]


def _reduce_end(ts, grads, from_sibling, from_chips, place):
    return [_total_sum(t, g, r, q, place) for t, g, r, q in zip(ts, grads, from_sibling, from_chips)]


_EARLY = (1, 2, 3, 4)
_W_IN = (0,)
_SMALL_RED = (5,)


def _adamw(w, g, m, v, name):
    def fn(w, g, m, v):
        m = ADAM_B1 * m + (1.0 - ADAM_B1) * g
        v = ADAM_B2 * v + (1.0 - ADAM_B2) * (g * g)
        m_hat = m / (1.0 - ADAM_B1 ** ADAM_STEP)
        v_hat = v / (1.0 - ADAM_B2 ** ADAM_STEP)
        return -ADAM_LR * (m_hat / (jnp.sqrt(v_hat) + ADAM_EPS) + ADAM_WD * w), m, v
    W = w.shape[1]
    return _rowwise(fn, [(a, 0, W) for a in (w, g, m, v)], [], [(W, F32)] * 3, tr=128, name=name)


def _chunks(arr, off, width, w=512):
    return [(arr, off + i * w, w) for i in range(width // w)]


def _cat(vs):
    return jnp.concatenate(vs, axis=1)


def _forward_backward(x, p_b, tgt, shards, small):
    L = x.shape[0]
    w_in, = _gather_weights([0], shards[:1])
    row = lambda v: v.reshape(1, -1)
    g_mix, g_ple, g_fin = row(small["norm_mix"]), row(small["norm_ple"]), row(small["norm_final"])
    gq, gk, b_glu = row(small["q_norm"]), row(small["k_norm"]), row(small["b_glu"])
    cos, sin = _rope_tables(L)

    hn_b, = _rowwise(lambda x, g: x * _rms(x) * g, [(x, 0, D_MODEL)], [g_mix], [(D_MODEL, BF16)], name="norm_mix")
    z = _matmul(hn_b, w_in, name="mm_in")
    qr, kr, vb = _attn_prep(z, gq, gk, cos, sin)
    o, lse, w_glu, w_out, w_pg, w_pp = _attn_fwd(qr, kr, vb, [1, 2, 3, 4], shards[1:])

    ssm_names = ("ssm_a_re", "ssm_a_im", "ssm_log_dt", "ssm_b_re", "ssm_b_im")
    (lre, lim, bre, bim), disc_vjp = jax.vjp(_ssm_disc, *[small[n][0] for n in ssm_names])
    ssm = (_to_wb(bre, bim).astype(BF16), _to_wc(small["ssm_c_re"][0], small["ssm_c_im"][0]).astype(BF16),
           _to_lam(lre), _to_lam(lim), row(small["ssm_d"]))
    u_p = _seg_perm(z[:, Z_U:Z_U + D_SSM])
    y_s = _seg_unperm(_ssm_fwd(u_p, *ssm))
    ge_b, = _rowwise(_gelu, [(y_s, 0, D_SSM)], [], [(D_SSM, BF16)], name="gelu")
    glu = _matmul(ge_b, w_glu, name="mm_glu")

    def merge(ga0, ga1, a, b, gs0, gs1, o, bias):
        sa, _ = _silu_and_grad(_cat([ga0, ga1]))
        ss, _ = _silu_and_grad(_cat([gs0, gs1]))
        y2 = (a + bias[:, :D_SSM]) * _sig(b + bias[:, D_SSM:])
        return _cat([o * sa, y2 * ss])
    merge_rows = _chunks(z, Z_GA, D_ATTN) + [(glu, 0, D_SSM), (glu, D_SSM, D_SSM)] + _chunks(z, Z_GS, D_SSM) + [(o, 0, D_ATTN)]
    cat_b, = _rowwise(merge, merge_rows, [b_glu], [(D_MODEL, BF16)], name="merge")
    t_out = _matmul(cat_b, w_out, name="mm_out")

    def resid(x, t, g):
        h1 = x + t
        return h1, h1 * _rms(h1) * g
    h1, hp_b = _rowwise(resid, [(x, 0, D_MODEL), (t_out, 0, D_MODEL)], [g_ple], [(D_MODEL, F32), (D_MODEL, BF16)], name="resid_norm")
    gl = _matmul(hp_b, w_pg, name="mm_ple_gate")
    pp = _matmul(p_b, w_pp, name="mm_ple_proj")

    def head(h1, gl, pp, tgt, g):
        gate = _sig(gl)
        h2 = h1 + gate * pp
        r = _rms(h2)
        n = h2 * r
        err = n * g - tgt
        dy = err * (1.0 / D_MODEL)
        dn = dy * g
        dh2 = r * (dn - n * jnp.mean(dn * n, axis=-1, keepdims=True))
        dgate = dh2 * pp
        return dh2, dh2 * gate, dgate * gate * (1.0 - gate), _colsum(dy * n), _colsum(0.5 * err * err * (1.0 / D_MODEL))
    dh2, dpp_b, dgl_b, dg_fin, loss_cols = _rowwise(
        head, [(a, 0, D_MODEL) for a in (h1, gl, pp, tgt)], [g_fin],
        [(D_MODEL, F32), (D_MODEL, BF16), (D_MODEL, BF16)], [(1, D_MODEL), (1, D_MODEL)], name="loss_head")

    dw_pp = _matmul(p_b, dpp_b, ta=True, name="mm_d_w_ple_proj")
    dw_pg = _matmul(hp_b, dgl_b, ta=True, name="mm_d_w_ple_gate")
    dhp = _matmul(dgl_b, w_pg, tb=True, name="mm_d_hp")

    def resid_bwd(dhp, h1, dh2, g):
        dx, dg = _rms_bwd(h1, g, dhp)
        dh1 = dh2 + dx
        return dh1, dh1, _colsum(dg)
    dh1, dh1_b, dg_ple = _rowwise(resid_bwd, [(a, 0, D_MODEL) for a in (dhp, h1, dh2)], [g_ple],
                                  [(D_MODEL, F32), (D_MODEL, BF16)], [(1, D_MODEL)], name="resid_norm_bwd")
    dw_out = _matmul(cat_b, dh1_b, ta=True, name="mm_d_w_out")
    dcat = _matmul(dh1_b, w_out, tb=True, name="mm_d_cat")

    def merge_bwd(dya, dys, ga0, ga1, a, b, gs0, gs1, o, bias):
        ga, gs = _cat([ga0, ga1]), _cat([gs0, gs1])
        sa, dsa = _silu_and_grad(ga)
        ss, dss = _silu_and_grad(gs)
        a, sb = a + bias[:, :D_SSM], _sig(b + bias[:, D_SSM:])
        dy2 = dys * ss
        dglu = _cat([dy2 * sb, dy2 * a * sb * (1.0 - sb)])
        return dya * sa, dya * o * dsa, dys * (a * sb) * dss, dglu, _colsum(dglu)
    do_b, dga_b, dgs_b, dglu_b, db_glu = _rowwise(
        merge_bwd, [(dcat, 0, D_ATTN), (dcat, D_ATTN, D_SSM)] + merge_rows, [b_glu],
        [(D_ATTN, BF16), (D_ATTN, BF16), (D_SSM, BF16), (2 * D_SSM, BF16)], [(1, 2 * D_SSM)], name="merge_bwd")
    dw_glu = _matmul(ge_b, dglu_b, ta=True, name="mm_d_w_glu")
    dge = _matmul(dglu_b, w_glu, tb=True, name="mm_d_ge")
    dy_s, = _rowwise(lambda dge, y: dge * _gelu_grad(y), [(dge, 0, D_SSM), (y_s, 0, D_SSM)], [], [(D_SSM, F32)], name="gelu_bwd")
    du_p, dwb, dwc, dare, daim, d_ssm_d = _ssm_bwd(u_p, _seg_perm(dy_s), *ssm)
    dc_re, dc_im = _from_wc(dwc)
    da_re, da_im, dlog_dt, db_re, db_im = disc_vjp((_from_lam(dare), _from_lam(daim)) + _from_wb(dwb))

    place = _grad_place()
    early_grads = [dw_glu, dw_out, dw_pg, dw_pp]
    early_sib, early_sums = _reduce_begin(_EARLY, early_grads, place, "early")
    dqr, dkr, dv, *early_chips = _attn_bwd(qr, kr, kr.T, vb, do_b, lse.reshape(N_HEADS, 1, L), _EARLY, early_sums)
    early_totals = _reduce_end(_EARLY, early_grads, early_sib, early_chips, place)
    dq_b, dk_b, dgq, dgk = _attn_prep_bwd(dqr, dkr, z, gq, gk, cos, sin)
    dz_b = _cat([dq_b, dk_b, dv.astype(BF16), dga_b, _seg_unperm(du_p).astype(BF16), dgs_b])
    dw_in = _matmul(hn_b, dz_b, ta=True, name="mm_d_w_in")
    w_in_sib, w_in_sums = _reduce_begin(_W_IN, [dw_in], place, "w_in")
    dhn, *w_in_chips = _matmul(dz_b, w_in, tb=True, name="mm_d_hn", exchange=(_W_IN, w_in_sums))
    w_in_total, = _reduce_end(_W_IN, [dw_in], w_in_sib, w_in_chips, place)

    def norm_bwd(dhn, x, dh1, g):
        dx, dg = _rms_bwd(x, g, dhn)
        return dh1 + dx, _colsum(dg)
    grad_x, dg_mix = _rowwise(norm_bwd, [(a, 0, D_MODEL) for a in (dhn, x, dh1)], [g_mix], [(D_MODEL, F32)], [(1, D_MODEL)],
                              name="norm_mix_bwd")

    small_grads = {"norm_mix": dg_mix, "q_norm": dgq, "k_norm": dgk, "ssm_a_re": da_re, "ssm_a_im": da_im, "ssm_log_dt": dlog_dt,
                   "ssm_b_re": db_re, "ssm_b_im": db_im, "ssm_c_re": dc_re, "ssm_c_im": dc_im, "ssm_d": d_ssm_d,
                   "b_glu": db_glu, "norm_ple": dg_ple, "norm_final": dg_fin}
    return jnp.sum(loss_cols), grad_x, [w_in_total] + early_totals, small_grads, place


_SMALL = ("norm_mix", "q_norm", "k_norm", "ssm_a_re", "ssm_a_im", "ssm_log_dt", "ssm_b_re", "ssm_b_im", "ssm_c_re", "ssm_c_im",
          "ssm_d", "b_glu", "norm_ple", "norm_final")
_WEIGHTS = ("norm_mix", "w_in", "q_norm", "k_norm", "ssm_a_re", "ssm_a_im", "ssm_log_dt", "ssm_b_re", "ssm_b_im", "ssm_c_re",
            "ssm_c_im", "ssm_d", "w_glu", "b_glu", "w_out", "norm_ple", "w_ple_gate", "w_ple_proj", "norm_final")


def _flat_small(d):
    flat = jnp.concatenate([d[n].reshape(-1).astype(F32) for n in _SMALL])
    return jnp.pad(flat, (0, N_SMALL - flat.shape[0]))


def _split_small(flat, like):
    out, off = {}, 0
    for n in _SMALL:
        sz = math.prod(like[n].shape)
        out[n] = flat[off:off + sz].reshape(like[n].shape)
        off += sz
    return out


def kernel(x, p, norm_mix, w_in, q_norm, k_norm, ssm_a_re, ssm_a_im, ssm_log_dt, ssm_b_re, ssm_b_im, ssm_c_re, ssm_c_im, ssm_d, w_glu, b_glu, w_out, norm_ple, w_ple_gate, w_ple_proj, norm_final, loss_target, m_norm_mix, m_w_in, m_q_norm, m_k_norm, m_ssm_a_re, m_ssm_a_im, m_ssm_log_dt, m_ssm_b_re, m_ssm_b_im, m_ssm_c_re, m_ssm_c_im, m_ssm_d, m_w_glu, m_b_glu, m_w_out, m_norm_ple, m_w_ple_gate, m_w_ple_proj, m_norm_final, v_norm_mix, v_w_in, v_q_norm, v_k_norm, v_ssm_a_re, v_ssm_a_im, v_ssm_log_dt, v_ssm_b_re, v_ssm_b_im, v_ssm_c_re, v_ssm_c_im, v_ssm_d, v_w_glu, v_b_glu, v_w_out, v_norm_ple, v_w_ple_gate, v_w_ple_proj, v_norm_final):
    w = dict(norm_mix=norm_mix, w_in=w_in, q_norm=q_norm, k_norm=k_norm, ssm_a_re=ssm_a_re, ssm_a_im=ssm_a_im,
             ssm_log_dt=ssm_log_dt, ssm_b_re=ssm_b_re, ssm_b_im=ssm_b_im, ssm_c_re=ssm_c_re, ssm_c_im=ssm_c_im, ssm_d=ssm_d,
             w_glu=w_glu, b_glu=b_glu, w_out=w_out, norm_ple=norm_ple, w_ple_gate=w_ple_gate, w_ple_proj=w_ple_proj,
             norm_final=norm_final)
    m = dict(norm_mix=m_norm_mix, w_in=m_w_in, q_norm=m_q_norm, k_norm=m_k_norm, ssm_a_re=m_ssm_a_re, ssm_a_im=m_ssm_a_im,
             ssm_log_dt=m_ssm_log_dt, ssm_b_re=m_ssm_b_re, ssm_b_im=m_ssm_b_im, ssm_c_re=m_ssm_c_re, ssm_c_im=m_ssm_c_im,
             ssm_d=m_ssm_d, w_glu=m_w_glu, b_glu=m_b_glu, w_out=m_w_out, norm_ple=m_norm_ple, w_ple_gate=m_w_ple_gate,
             w_ple_proj=m_w_ple_proj, norm_final=m_norm_final)
    v = dict(norm_mix=v_norm_mix, w_in=v_w_in, q_norm=v_q_norm, k_norm=v_k_norm, ssm_a_re=v_ssm_a_re, ssm_a_im=v_ssm_a_im,
             ssm_log_dt=v_ssm_log_dt, ssm_b_re=v_ssm_b_re, ssm_b_im=v_ssm_b_im, ssm_c_re=v_ssm_c_re, ssm_c_im=v_ssm_c_im,
             ssm_d=v_ssm_d, w_glu=v_w_glu, b_glu=v_b_glu, w_out=v_w_out, norm_ple=v_norm_ple, w_ple_gate=v_w_ple_gate,
             w_ple_proj=v_w_ple_proj, norm_final=v_norm_final)
    big_names = [n for n, _, _, _ in _BIG]

    small = {n: w[n] for n in _SMALL}
    loss_part, grad_x, big_totals, small_grads, place = _forward_backward(
        x[0], p[0, 0].astype(BF16), loss_target[0], [w[n][0].astype(BF16) for n in big_names], small)
    loss = lax.psum(loss_part, ("x", "y", "c"))

    small_flat = [_flat_small(small_grads).reshape(8 * SMALL_ROWS, SMALL_W)]
    small_sib, small_sums = _reduce_begin(_SMALL_RED, small_flat, place, "small")
    small_total = _reduce_end(_SMALL_RED, small_flat, small_sib, _grad_chip_exchange(_SMALL_RED, small_sums), place)
    *big_red, small_red = _grad_final_exchange(big_totals + small_total)
    grads = _split_small(small_red.reshape(-1), w)
    delta, new_m, new_v = {}, {}, {}
    for n, g in zip(big_names, big_red):
        grads[n] = g[None]
        d_, m_, v_ = _adamw(w[n][0], g, m[n][0], v[n][0], "adamw_" + n)
        delta[n], new_m[n], new_v[n] = d_[None], m_[None], v_[None]
    d_, m_, v_ = _adamw(*[a.reshape(-1, SMALL_W) for a in (_flat_small(w), small_red, _flat_small(m), _flat_small(v))], "adamw_small")
    delta.update(_split_small(d_.reshape(-1), w))
    new_m.update(_split_small(m_.reshape(-1), w))
    new_v.update(_split_small(v_.reshape(-1), w))
    return (loss, grad_x[None], *[grads[n] for n in _WEIGHTS], *[delta[n] for n in _WEIGHTS],
            *[new_m[n] for n in _WEIGHTS], *[new_v[n] for n in _WEIGHTS])
```

```python
import functools
import math

import jax
import jax.numpy as jnp
import numpy as np
from jax import lax
from jax.experimental import pallas as pl
from jax.experimental.pallas import tpu as pltpu

D_MODEL = 2048
GRID_W = 64
PLE_DIM = 256
D_ATTN = 1024
N_HEADS = 8
N_KV = 2
HEAD_DIM = 128
ROPE_THETA = 10000.0
D_SSM = 1024
SSM_H = 16
SSM_G = 64
SSM_P = 64
D_KV = N_KV * HEAD_DIM
D_IN = 2 * D_ATTN + 2 * D_KV + 2 * D_SSM
EPS = 1e-6
Z_Q, Z_K, Z_V, Z_GA, Z_U, Z_GS = 0, 1024, 1280, 1536, 2560, 3584

ADAM_LR, ADAM_B1, ADAM_B2, ADAM_EPS, ADAM_WD, ADAM_STEP = 0.001, 0.9, 0.999, 1e-08, 0.01, 10

N_CHIPS = 4
VMEM_LIMIT_V7X = 56 * 1024 * 1024
F32 = jnp.float32
BF16 = jnp.bfloat16


def _params(sem, vmem=VMEM_LIMIT_V7X):
    return pltpu.CompilerParams(dimension_semantics=sem, vmem_limit_bytes=vmem)


def _matmul(a, b, *, ta=False, tb=False, out_dtype=F32, tm=1024, tn=512, name, exchange=None):
    M, K = (a.shape[1], a.shape[0]) if ta else a.shape
    N = b.shape[0] if tb else b.shape[1]
    tm, tn = min(tm, M), min(tn, N)
    assert M % tm == 0 and N % tn == 0, (name, M, N, K)
    dims = (((0 if ta else 1,), (1 if tb else 0,)), ((), ()))
    ex_ts, ex_sums = exchange if exchange else ((), ())
    n_ex = len(ex_ts)
    gm, gn = M // tm, N // tn

    def body(a_ref, b_ref, *rest):
        o_ref = rest[n_ex]
        if n_ex:
            ex = _ChipExchange(ex_ts, rest[:n_ex], rest[n_ex + 1:2 * n_ex + 1], rest[2 * n_ex + 1:])
            step = pl.program_id(0) * gn + pl.program_id(1)
            pl.when(step == 0)(ex.start)
        o_ref[...] = lax.dot_general(a_ref[...], b_ref[...], dims, preferred_element_type=F32).astype(o_ref.dtype)
        if n_ex:
            pl.when(step == gm * gn - 1)(ex.finish)

    a_spec = pl.BlockSpec((K, tm), lambda i, j: (0, i)) if ta else pl.BlockSpec((tm, K), lambda i, j: (i, 0))
    b_spec = pl.BlockSpec((tn, K), lambda i, j: (j, 0)) if tb else pl.BlockSpec((K, tn), lambda i, j: (0, j))
    res = pl.pallas_call(
        body, name=name,
        out_shape=[jax.ShapeDtypeStruct((M, N), out_dtype)] + (_ChipExchange.out_shape(ex_ts) if n_ex else []),
        grid=(gm, gn),
        in_specs=[a_spec, b_spec] + [_ANY] * n_ex,
        out_specs=[pl.BlockSpec((tm, tn), lambda i, j: (i, j))] + [_ANY] * n_ex,
        scratch_shapes=_ChipExchange.scratch(ex_ts) if n_ex else [],
        compiler_params=_params(("arbitrary", "arbitrary") if n_ex else ("parallel", "parallel")),
    )(a, b, *ex_sums)
    return res if n_ex else res[0]


def _rowwise(fn, rows, consts, outs, accs=(), *, tr=256, name):
    L = rows[0][0].shape[0]
    tr = math.gcd(tr, L)
    assert tr % 8 == 0 or tr == L, (name, L, tr)
    n_in, n_c, n_o, n_a = len(rows), len(consts), len(outs), len(accs)

    def body(*refs):
        ins = [r[...] for r in refs[:n_in + n_c]]
        res = fn(*ins)
        if not isinstance(res, (tuple, list)):
            res = (res,)
        o_refs = refs[n_in + n_c:n_in + n_c + n_o]
        a_refs = refs[n_in + n_c + n_o:]
        for r, v in zip(o_refs, res[:n_o]):
            r[...] = v.astype(r.dtype)
        if n_a:
            first = pl.program_id(0) == 0

            @pl.when(first)
            def _():
                for r, v in zip(a_refs, res[n_o:]):
                    r[...] = v.astype(F32)

            @pl.when(jnp.logical_not(first))
            def _():
                for r, v in zip(a_refs, res[n_o:]):
                    r[...] += v.astype(F32)

    in_specs = []
    for arr, off, w in rows:
        assert off % w == 0, (name, off, w)
        in_specs.append(pl.BlockSpec((tr, w), functools.partial(lambda i, c: (i, c), c=off // w)))
    for c in consts:
        in_specs.append(pl.BlockSpec(c.shape, lambda i: (0, 0)))
    out_shape = [jax.ShapeDtypeStruct((L, w), dt) for w, dt in outs] + [jax.ShapeDtypeStruct(s, F32) for s in accs]
    out_specs = [pl.BlockSpec((tr, w), lambda i: (i, 0)) for w, _ in outs] + [pl.BlockSpec(s, lambda i: (0, 0)) for s in accs]
    res = pl.pallas_call(
        body, name=name,
        out_shape=out_shape,
        grid=(L // tr,),
        in_specs=in_specs,
        out_specs=out_specs,
        compiler_params=_params(("arbitrary",) if n_a else ("parallel",)),
    )(*[r[0] for r in rows], *consts)
    return res


def _sig(x):
    return jax.nn.sigmoid(x)


def _silu_and_grad(x):
    s = _sig(x)
    return x * s, s * (1.0 + x * (1.0 - s))


_GELU_C = math.sqrt(2.0 / math.pi)


def _gelu(x):
    return 0.5 * x * (1.0 + jnp.tanh(_GELU_C * (x + 0.044715 * x * x * x)))


def _gelu_grad(x):
    t = jnp.tanh(_GELU_C * (x + 0.044715 * x * x * x))
    return 0.5 * (1.0 + t) + 0.5 * x * (1.0 - t * t) * _GELU_C * (1.0 + 3.0 * 0.044715 * x * x)


def _rms(x):
    return lax.rsqrt(jnp.mean(x * x, axis=-1, keepdims=True) + EPS)


def _rms_bwd(x, g, dy):
    r = _rms(x)
    n = x * r
    dn = dy * g
    return r * (dn - n * jnp.mean(dn * n, axis=-1, keepdims=True)), dy * n


def _colsum(v):
    return jnp.sum(v, axis=0, keepdims=True)


def _rope_partner(x):
    lane = lax.broadcasted_iota(jnp.int32, x.shape, x.ndim - 1)
    return jnp.where(lane % 64 < 32, pltpu.roll(x, 96, x.ndim - 1), pltpu.roll(x, 32, x.ndim - 1))


def _rope_tables(L):
    rows_n = L // GRID_W
    rows = jnp.repeat(jnp.arange(rows_n), GRID_W).astype(F32)
    cols = jnp.tile(jnp.arange(GRID_W), rows_n).astype(F32)
    n_freq = HEAD_DIM // 4
    inv_freq = ROPE_THETA ** (-jnp.arange(n_freq, dtype=F32) / n_freq)
    ar, ac = rows[:, None] * inv_freq[None, :], cols[:, None] * inv_freq[None, :]
    cos = jnp.concatenate([jnp.cos(ar), jnp.cos(ar), jnp.cos(ac), jnp.cos(ac)], axis=-1)
    sin = jnp.concatenate([-jnp.sin(ar), jnp.sin(ar), -jnp.sin(ac), jnp.sin(ac)], axis=-1)
    return cos, sin


def _heads(v):
    return [v[:, h * HEAD_DIM:(h + 1) * HEAD_DIM] for h in range(v.shape[1] // HEAD_DIM)]


def _attn_prep(z, q_norm, k_norm, cos, sin):
    def fn(q, k, v, cos, sin, gq, gk):
        def one(xh, g):
            xn = xh * _rms(xh) * g
            return xn * cos + _rope_partner(xn) * sin
        qr = jnp.concatenate([one(h, gq) for h in _heads(q)], axis=1)
        kr = jnp.concatenate([one(h, gk) for h in _heads(k)], axis=1)
        return qr, kr, v
    return _rowwise(fn, [(z, Z_Q, D_ATTN), (z, Z_K, D_KV), (z, Z_V, D_KV), (cos, 0, HEAD_DIM), (sin, 0, HEAD_DIM)],
                    [q_norm, k_norm], [(D_ATTN, BF16), (D_KV, BF16), (D_KV, BF16)], name="attn_prep")


def _attn_prep_bwd(dqr, dkr, z, q_norm, k_norm, cos, sin):
    def fn(dqr, dkr, q, k, cos, sin, gq, gk):
        def one(dyh, xh, g):
            dn = dyh * cos + _rope_partner(dyh * sin)
            return _rms_bwd(xh, g, dn)
        rq = [one(a, b, gq) for a, b in zip(_heads(dqr), _heads(q))]
        rk = [one(a, b, gk) for a, b in zip(_heads(dkr), _heads(k))]
        dq = jnp.concatenate([r[0] for r in rq], axis=1)
        dk = jnp.concatenate([r[0] for r in rk], axis=1)
        return dq, dk, _colsum(sum(r[1] for r in rq)), _colsum(sum(r[1] for r in rk))
    return _rowwise(fn, [(dqr, 0, D_ATTN), (dkr, 0, D_KV), (z, Z_Q, D_ATTN), (z, Z_K, D_KV), (cos, 0, HEAD_DIM), (sin, 0, HEAD_DIM)],
                    [q_norm, k_norm], [(D_ATTN, BF16), (D_KV, BF16)], [(1, HEAD_DIM), (1, HEAD_DIM)], name="attn_prep_bwd")


_QK_T = (((1,), (1,)), ((), ()))
_TA = (((0,), (0,)), ((), ()))
_REP = N_HEADS // N_KV


_EXP2_SCALE = HEAD_DIM ** -0.5 * math.log2(math.e)
ATTN_FWD_KEY_CHUNKS = 4
ATTN_BWD_KEY_CHUNKS = 8


def _attn_fwd(qr, kr, vb, g_ts, g_shards, *, tq=1024):
    L = qr.shape[0]
    tq = min(tq, L)
    kc = L // ATTN_FWD_KEY_CHUNKS
    n_g = len(g_ts)
    grid = (N_HEADS, L // tq)
    steps = grid[0] * grid[1]

    def body(q_ref, k_ref, v_ref, *rest):
        o_ref, lse_ref = rest[n_g:n_g + 2]
        g = _Gather(g_ts, rest[:n_g], rest[n_g + 2:2 * n_g + 2], rest[2 * n_g + 2:3 * n_g + 2], rest[3 * n_g + 2:])
        step = pl.program_id(0) * grid[1] + pl.program_id(1)
        pl.when(step == 0)(g.start)
        pl.when(step == (3 * steps) // 4)(g.forward)
        q = q_ref[...]
        m = jnp.full((tq, 1), -jnp.inf, F32)
        l = jnp.zeros((tq, 1), F32)
        o = jnp.zeros((tq, HEAD_DIM), F32)
        for c in range(ATTN_FWD_KEY_CHUNKS):
            ks = slice(c * kc, (c + 1) * kc)
            s = lax.dot_general(q, k_ref[ks, :], _QK_T, preferred_element_type=F32)
            m_new = jnp.maximum(m, jnp.max(s, axis=1, keepdims=True))
            a = jnp.exp2((m - m_new) * _EXP2_SCALE)
            p = jnp.exp2((s - m_new) * _EXP2_SCALE)
            l = a * l + jnp.sum(p, axis=1, keepdims=True)
            o = a * o + jnp.dot(p.astype(BF16), v_ref[ks, :], preferred_element_type=F32)
            m = m_new
        o_ref[...] = o * (1.0 / l)
        lse_ref[...] = m * _EXP2_SCALE + jnp.log2(l)
        pl.when(step == steps - 1)(g.finish)

    kv = pl.BlockSpec((L, HEAD_DIM), lambda h, i: (0, h // _REP))
    return pl.pallas_call(
        body, name="attn_fwd",
        out_shape=[jax.ShapeDtypeStruct((L, D_ATTN), F32), jax.ShapeDtypeStruct((N_HEADS, L, 1), F32)] +
                  [jax.ShapeDtypeStruct(_BIG[t][1], BF16) for t in g_ts],
        grid=grid,
        in_specs=[pl.BlockSpec((tq, HEAD_DIM), lambda h, i: (i, h)), kv, kv] + [_ANY] * n_g,
        out_specs=[pl.BlockSpec((tq, HEAD_DIM), lambda h, i: (i, h)),
                   pl.BlockSpec((None, tq, 1), lambda h, i: (h, i, 0))] + [_ANY] * n_g,
        scratch_shapes=_Gather.scratch(g_shards),
        compiler_params=_params(("arbitrary", "arbitrary")),
    )(qr, kr, vb, *g_shards)


def _attn_bwd(qr, kr, k_t, vb, do, lse, ex_ts, ex_sums, *, tq=512):
    L = qr.shape[0]
    tq = min(tq, L)
    scale = HEAD_DIM ** -0.5
    kc = L // ATTN_BWD_KEY_CHUNKS
    n_ex = len(ex_ts)
    grid = (N_KV, _REP, L // tq)

    def body(q_ref, k_ref, kt_ref, v_ref, do_ref, lse_ref, *rest):
        dq_ref, dk_ref, dv_ref = rest[n_ex:n_ex + 3]
        ex = _ChipExchange(ex_ts, rest[:n_ex], rest[n_ex + 3:2 * n_ex + 3], rest[2 * n_ex + 3:])
        step = (pl.program_id(0) * grid[1] + pl.program_id(1)) * grid[2] + pl.program_id(2)
        pl.when(step == 0)(ex.start)

        @pl.when((pl.program_id(1) == 0) & (pl.program_id(2) == 0))
        def _():
            dk_ref[...] = jnp.zeros_like(dk_ref)
            dv_ref[...] = jnp.zeros_like(dv_ref)

        q, do, lse = q_ref[...], do_ref[...], lse_ref[...]
        keys = [slice(c * kc, (c + 1) * kc) for c in range(ATTN_BWD_KEY_CHUNKS)]
        ps, dps = [], []
        for ks in keys:
            st = lax.dot_general(k_ref[ks, :], q, _QK_T, preferred_element_type=F32)
            p = jnp.exp2(st * _EXP2_SCALE - lse)
            dv_ref[ks, :] += jnp.dot(p.astype(BF16), do, preferred_element_type=F32)
            ps.append(p)
            dps.append(lax.dot_general(v_ref[ks, :], do, _QK_T, preferred_element_type=F32))
        delta = sum(jnp.sum(p * dp, axis=0, keepdims=True) for p, dp in zip(ps, dps))
        dq_t = 0.0
        for ks, p, dp in zip(keys, ps, dps):
            ds = (p * (dp - delta) * scale).astype(BF16)
            dk_ref[ks, :] += jnp.dot(ds, q, preferred_element_type=F32)
            dq_t = dq_t + jnp.dot(kt_ref[:, ks], ds, preferred_element_type=F32)
        dq_ref[...] = dq_t.T
        pl.when(step == grid[0] * grid[1] * grid[2] - 1)(ex.finish)

    head = lambda g, r, i: (i, g * _REP + r)
    kv = pl.BlockSpec((L, HEAD_DIM), lambda g, r, i: (0, g))
    return pl.pallas_call(
        body, name="attn_bwd",
        out_shape=[jax.ShapeDtypeStruct((L, D_ATTN), F32), jax.ShapeDtypeStruct((L, D_KV), F32), jax.ShapeDtypeStruct((L, D_KV), F32)] +
                  _ChipExchange.out_shape(ex_ts),
        grid=grid,
        in_specs=[pl.BlockSpec((tq, HEAD_DIM), head), kv,
                  pl.BlockSpec((HEAD_DIM, L), lambda g, r, i: (g, 0)), kv,
                  pl.BlockSpec((tq, HEAD_DIM), head),
                  pl.BlockSpec((None, 1, tq), lambda g, r, i: (g * _REP + r, 0, i))] + [_ANY] * n_ex,
        out_specs=[pl.BlockSpec((tq, HEAD_DIM), head), kv, kv] + [_ANY] * n_ex,
        scratch_shapes=_ChipExchange.scratch(ex_ts),
        compiler_params=_params(("arbitrary", "arbitrary", "arbitrary")),
    )(qr, kr, k_t, vb, do, lse, *ex_sums)


SSM_BLK = 8
SSM_NB = SSM_G // SSM_BLK
SSM_SEG = 8
SSM_UNROLL = 4


def _unrolled_loop(n, step, carry):
    u = SSM_UNROLL

    def trip(i, c):
        for j in range(u):
            c = step(i * u + j, c)
        return c
    carry = lax.fori_loop(0, n // u, trip, carry)
    for t in range(n - n % u, n):
        carry = step(jnp.int32(t), carry)
    return carry


def _cplx_pow2(a, b, n):
    for _ in range(int(math.log2(n))):
        a, b = a * a - b * b, 2.0 * a * b
    return a, b


def _seg_scan(ref, a, b, T, reverse):
    npair = len(a)
    zero = jnp.zeros((SSM_SEG, 128), F32)

    def make_step(store):
        def step(t, carry):
            lt = (T - 1 - t) if reverse else t
            row = pl.multiple_of(lt * SSM_SEG, SSM_SEG)
            blk = ref[pl.ds(row, SSM_SEG), :]
            new = []
            for q in range(npair):
                re, im = carry[2 * q], carry[2 * q + 1]
                nre = a[q] * re - b[q] * im + blk[:, q * 256:q * 256 + 128]
                nim = a[q] * im + b[q] * re + blk[:, q * 256 + 128:q * 256 + 256]
                new += [nre, nim]
            if store:
                ref[pl.ds(row, SSM_SEG), :] = jnp.concatenate(new, axis=1)
            return tuple(new)
        return step

    ends = _unrolled_loop(T, make_step(False), (zero,) * (2 * npair))
    sub = lax.broadcasted_iota(jnp.int32, (SSM_SEG, 128), 0)
    keep = (sub != SSM_SEG - 1) if reverse else (sub != 0)
    shift = (SSM_SEG - 1) if reverse else 1
    init = []
    for q in range(npair):
        pa, pb = _cplx_pow2(a[q], b[q], T)
        xr, xi = zero, zero
        for _ in range(SSM_SEG - 1):
            fr = ends[2 * q] + pa * xr - pb * xi
            fi = ends[2 * q + 1] + pa * xi + pb * xr
            xr = jnp.where(keep, pltpu.roll(fr, shift, 0), 0.0)
            xi = jnp.where(keep, pltpu.roll(fi, shift, 0), 0.0)
        init += [xr, xi]
    _unrolled_loop(T, make_step(True), tuple(init))
    return init


SSM_BW = SSM_BLK * SSM_H
SSM_SW = SSM_BLK * 2 * SSM_P
SSM_NPAIR = SSM_BLK // 2


def _seg_perm(a):
    L, C = a.shape
    return a.reshape(SSM_SEG, L // SSM_SEG, C).transpose(1, 0, 2).reshape(L, C)


def _seg_unperm(a):
    L, C = a.shape
    return a.reshape(L // SSM_SEG, SSM_SEG, C).transpose(1, 0, 2).reshape(L, C)


def _lam_rows(are_ref, aim_ref, d):
    a = [jnp.broadcast_to(are_ref[d, j:j + 1, :], (SSM_SEG, 128)) for j in range(SSM_NPAIR)]
    b = [jnp.broadcast_to(aim_ref[d, j:j + 1, :], (SSM_SEG, 128)) for j in range(SSM_NPAIR)]
    return a, b


_PAIR_SPEC = pl.BlockSpec((None, 2, SSM_NPAIR, 2, 2 * SSM_H, 128), lambda g: (g, 0, 0, 0, 0, 0))


def _pair_window(j, r):
    return slice(j * 2 * SSM_H, (j + 1) * 2 * SSM_H), slice(j * 256 + r * 128, j * 256 + (r + 1) * 128)


def _expand_pairs(c_ref, dense_ref):
    dense_ref[...] = jnp.zeros_like(dense_ref)
    for d in range(2):
        for j in range(SSM_NPAIR):
            for r in range(2):
                rows, cols = _pair_window(j, r)
                dense_ref[d, rows, cols] = c_ref[d, j, r].astype(dense_ref.dtype)


def _ssm_fwd(u_p, wb, wc, are, aim, dvec):
    L = u_p.shape[0]
    T = L // SSM_SEG
    RC = min(512, L)

    def body(u_ref, wb_ref, wc_ref, are_ref, aim_ref, d_ref, y_ref, x_scr, wb_s, wc_s):
        _expand_pairs(wb_ref, wb_s)
        _expand_pairs(wc_ref, wc_s)
        y_ref[...] = u_ref[...] * d_ref[...]
        for d in range(2):
            def bu_chunk(c, _):
                rows = pl.ds(pl.multiple_of(c * RC, RC), RC)
                x_scr[rows, :] = jnp.dot(u_ref[rows, :].astype(BF16), wb_s[d], preferred_element_type=F32)
                return 0
            lax.fori_loop(0, L // RC, bu_chunk, 0)
            a, b = _lam_rows(are_ref, aim_ref, d)
            _seg_scan(x_scr, a, b, T, reverse=(d == 1))

            def y_chunk(c, _):
                rows = pl.ds(pl.multiple_of(c * RC, RC), RC)
                y_ref[rows, :] += lax.dot_general(x_scr[rows, :].astype(BF16), wc_s[d], _QK_T, preferred_element_type=F32)
                return 0
            lax.fori_loop(0, L // RC, y_chunk, 0)

    blk4 = lambda g: (g, 0, 0, 0)
    return pl.pallas_call(
        body, name="ssm_fwd",
        out_shape=jax.ShapeDtypeStruct((L, D_SSM), F32),
        grid=(SSM_NB,),
        in_specs=[pl.BlockSpec((L, SSM_BW), lambda g: (0, g)), _PAIR_SPEC, _PAIR_SPEC,
                  pl.BlockSpec((None, 2, SSM_NPAIR, 128), blk4),
                  pl.BlockSpec((None, 2, SSM_NPAIR, 128), blk4),
                  pl.BlockSpec((1, SSM_BW), lambda g: (0, g))],
        out_specs=pl.BlockSpec((L, SSM_BW), lambda g: (0, g)),
        scratch_shapes=[pltpu.VMEM((L, SSM_SW), F32), pltpu.VMEM((2, SSM_BW, SSM_SW), BF16), pltpu.VMEM((2, SSM_BW, SSM_SW), BF16)],
        compiler_params=_params(("parallel",)),
    )(u_p, wb, wc, are, aim, dvec)


def _ssm_bwd(u_p, dy_p, wb, wc, are, aim, dvec):
    L = u_p.shape[0]
    T = L // SSM_SEG
    RC = min(512, L)

    def lam_acc(acc, sb, xb):
        new = []
        for q in range(SSM_NPAIR):
            sr, si = sb[:, q * 256:q * 256 + 128], sb[:, q * 256 + 128:q * 256 + 256]
            xr, xi = xb[:, q * 256:q * 256 + 128], xb[:, q * 256 + 128:q * 256 + 256]
            new += [acc[2 * q] + sr * xr + si * xi, acc[2 * q + 1] + si * xr - sr * xi]
        return tuple(new)

    def body(u_ref, dy_ref, wb_ref, wc_ref, are_ref, aim_ref, d_ref,
             du_ref, dwb_ref, dwc_ref, dare_ref, daim_ref, dd_ref, x_scr, s_scr, wb_s, wc_s, dwb_s, dwc_s):
        _expand_pairs(wb_ref, wb_s)
        _expand_pairs(wc_ref, wc_s)
        du_ref[...] = dy_ref[...] * d_ref[...]
        dd_ref[...] = _colsum(dy_ref[...] * u_ref[...])
        dwb_s[...] = jnp.zeros_like(dwb_s)
        dwc_s[...] = jnp.zeros_like(dwc_s)
        for d in range(2):
            rev = d == 1

            def in_chunk(c, _):
                rows = pl.ds(pl.multiple_of(c * RC, RC), RC)
                x_scr[rows, :] = jnp.dot(u_ref[rows, :].astype(BF16), wb_s[d], preferred_element_type=F32)
                s_scr[rows, :] = jnp.dot(dy_ref[rows, :].astype(BF16), wc_s[d], preferred_element_type=F32)
                return 0
            lax.fori_loop(0, L // RC, in_chunk, 0)
            a, b = _lam_rows(are_ref, aim_ref, d)
            x_in = _seg_scan(x_scr, a, b, T, reverse=rev)
            _seg_scan(s_scr, a, [-v for v in b], T, reverse=not rev)

            def lam_step(t, acc):
                lt = (T - 2 - t) if rev else (t + 1)
                srow = pl.multiple_of(lt * SSM_SEG, SSM_SEG)
                xrow = pl.multiple_of((lt + 1 if rev else lt - 1) * SSM_SEG, SSM_SEG)
                return lam_acc(acc, s_scr[pl.ds(srow, SSM_SEG), :], x_scr[pl.ds(xrow, SSM_SEG), :])

            edge = pl.ds(((T - 1) if rev else 0) * SSM_SEG, SSM_SEG)
            acc0 = lam_acc((jnp.zeros((SSM_SEG, 128), F32),) * (2 * SSM_NPAIR), s_scr[edge, :], jnp.concatenate(x_in, axis=1))
            acc = _unrolled_loop(T - 1, lam_step, acc0)
            for q in range(SSM_NPAIR):
                dare_ref[d, q:q + 1, :] = _colsum(acc[2 * q])
                daim_ref[d, q:q + 1, :] = _colsum(acc[2 * q + 1])

            def out_chunk(c, _):
                rows = pl.ds(pl.multiple_of(c * RC, RC), RC)
                xs, ss = x_scr[rows, :].astype(BF16), s_scr[rows, :].astype(BF16)
                uu, dd = u_ref[rows, :].astype(BF16), dy_ref[rows, :].astype(BF16)
                dwc_s[d] += lax.dot_general(dd, xs, _TA, preferred_element_type=F32)
                dwb_s[d] += lax.dot_general(uu, ss, _TA, preferred_element_type=F32)
                du_ref[rows, :] += lax.dot_general(ss, wb_s[d], _QK_T, preferred_element_type=F32)
                return 0
            lax.fori_loop(0, L // RC, out_chunk, 0)
            for j in range(SSM_NPAIR):
                for r in range(2):
                    rows, cols = _pair_window(j, r)
                    dwb_ref[d, j, r] = dwb_s[d, rows, cols]
                    dwc_ref[d, j, r] = dwc_s[d, rows, cols]

    blk4 = lambda g: (g, 0, 0, 0)
    chan = pl.BlockSpec((L, SSM_BW), lambda g: (0, g))
    par_specs = [_PAIR_SPEC, _PAIR_SPEC,
                 pl.BlockSpec((None, 2, SSM_NPAIR, 128), blk4),
                 pl.BlockSpec((None, 2, SSM_NPAIR, 128), blk4),
                 pl.BlockSpec((1, SSM_BW), lambda g: (0, g))]
    dense = lambda dt: pltpu.VMEM((2, SSM_BW, SSM_SW), dt)
    return pl.pallas_call(
        body, name="ssm_bwd",
        out_shape=[jax.ShapeDtypeStruct((L, D_SSM), F32),
                   jax.ShapeDtypeStruct(wb.shape, F32),
                   jax.ShapeDtypeStruct(wc.shape, F32),
                   jax.ShapeDtypeStruct((SSM_NB, 2, SSM_NPAIR, 128), F32),
                   jax.ShapeDtypeStruct((SSM_NB, 2, SSM_NPAIR, 128), F32),
                   jax.ShapeDtypeStruct((1, D_SSM), F32)],
        grid=(SSM_NB,),
        in_specs=[chan, chan] + par_specs,
        out_specs=[chan] + par_specs,
        scratch_shapes=[pltpu.VMEM((L, SSM_SW), F32), pltpu.VMEM((L, SSM_SW), F32), dense(BF16), dense(BF16), dense(F32), dense(F32)],
        compiler_params=_params(("parallel",)),
    )(u_p, dy_p, wb, wc, are, aim, dvec)


def _ssm_disc(a_re, a_im, log_dt, b_re, b_im):
    lam = lax.complex(jnp.minimum(a_re, -1e-4), a_im)
    dt = jnp.exp(log_dt)[..., None]
    lam_bar = jnp.exp(lam * dt)
    b_bar = ((lam_bar - 1.0) / lam)[..., None] * lax.complex(b_re, b_im)
    return jnp.real(lam_bar), jnp.imag(lam_bar), jnp.real(b_bar), jnp.imag(b_bar)


_EYE2 = np.eye(2, dtype=np.float32)[:, None, :, None]


def _to_pairs(t):
    t = t.reshape(2, SSM_NB, SSM_NPAIR, 2, 2, SSM_H, SSM_P).transpose(1, 0, 2, 4, 3, 5, 6)
    return (t[..., None, :] * _EYE2).reshape(SSM_NB, 2, SSM_NPAIR, 2, 2 * SSM_H, 2 * SSM_P)


def _from_pairs(c):
    t = jnp.sum(c.reshape(SSM_NB, 2, SSM_NPAIR, 2, 2, SSM_H, 2, SSM_P) * _EYE2, axis=6)
    return t.transpose(1, 0, 2, 4, 3, 5, 6).reshape(2, SSM_G, 2, SSM_H, SSM_P)


def _to_lam(v):
    return v.reshape(2, SSM_NB, SSM_NPAIR, 128).transpose(1, 0, 2, 3)


def _from_lam(v):
    return v.transpose(1, 0, 2, 3).reshape(2, SSM_G, SSM_P)


_MESH = pl.DeviceIdType.MESH
_ANY = pl.BlockSpec(memory_space=pl.ANY)
_BIG = (("w_in", (D_MODEL, D_IN), 1, D_IN // N_CHIPS),
        ("w_glu", (D_SSM, 2 * D_SSM), 1, 2 * D_SSM // N_CHIPS),
        ("w_out", (D_ATTN + D_SSM, D_MODEL), 0, (D_ATTN + D_SSM) // N_CHIPS),
        ("w_ple_gate", (D_MODEL, D_MODEL), 0, D_MODEL // N_CHIPS),
        ("w_ple_proj", (PLE_DIM, D_MODEL), 1, D_MODEL // N_CHIPS))


def _place():
    x, y, c = lax.axis_index("x"), lax.axis_index("y"), lax.axis_index("c")
    return x, y, c, [(1 - x, y), (x, 1 - y), (1 - x, 1 - y)]


class _Gather:
    def __init__(self, ts, srcs, dsts, stage, sems):
        self.ts, self.srcs, self.dsts, self.stage = ts, srcs, dsts, stage
        self.send_sems, self.recv_sems, self.fwd_send_sems, self.fwd_recv_sems, self.loc_sems = sems
        self.x, self.y, self.c, self.chips = _place()
        self.n = len(ts)

    @staticmethod
    def scratch(shards):
        sems = pltpu.SemaphoreType.DMA((3, len(shards)))
        return [pltpu.VMEM(s.shape, BF16) for s in shards] + [sems, sems, sems, sems, pltpu.SemaphoreType.DMA((len(shards),))]

    def _shard_of(self, i, kk):
        _, _, axis, sz = _BIG[self.ts[i]]
        sl = pl.ds(pl.multiple_of(kk * sz, sz), sz)
        return self.dsts[i].at[:, sl] if axis == 1 else self.dsts[i].at[sl, :]

    @staticmethod
    def _half_of(ref, cc):
        n = ref.shape[0] // 2
        return ref.at[pl.ds(pl.multiple_of(cc * n, n), n), :]

    def _ici(self, j, i, kk):
        px, py = self.chips[j]
        return pltpu.make_async_remote_copy(
            src_ref=self._half_of(self.srcs[i], self.c), dst_ref=self._half_of(self._shard_of(i, kk), self.c),
            send_sem=self.send_sems.at[j, i], recv_sem=self.recv_sems.at[j, i],
            device_id=(px, py, self.c), device_id_type=_MESH)

    def _forward(self, j, i, kk, cc):
        part = self._half_of(self._shard_of(i, kk), cc)
        return pltpu.make_async_remote_copy(
            src_ref=part, dst_ref=part, send_sem=self.fwd_send_sems.at[j, i], recv_sem=self.fwd_recv_sems.at[j, i],
            device_id=(self.x, self.y, 1 - self.c), device_id_type=_MESH)

    def _load(self, i):
        return pltpu.make_async_copy(self.srcs[i], self.stage[i], self.loc_sems.at[i])

    def _place_own(self, i):
        return pltpu.make_async_copy(self.stage[i], self._shard_of(i, 2 * self.x + self.y), self.loc_sems.at[i])

    def _peers(self):
        return [(i, j, 2 * px + py) for i in range(self.n) for j, (px, py) in enumerate(self.chips)]

    def start(self):
        for i in range(self.n):
            self._load(i).start()
        for i, j, _ in self._peers():
            self._ici(j, i, 2 * self.x + self.y).start()

    def forward(self):
        for i in range(self.n):
            self._load(i).wait()
            self._place_own(i).start()
        for i, j, kk in self._peers():
            self._ici(j, i, kk).wait_recv()
            self._forward(j, i, kk, self.c).start()

    def finish(self):
        for i, j, kk in self._peers():
            self._forward(j, i, kk, 1 - self.c).wait_recv()
        for i, j, kk in self._peers():
            self._ici(j, i, kk).wait_send()
            self._forward(j, i, kk, self.c).wait_send()
        for i in range(self.n):
            self._place_own(i).wait()


def _gather_weights(ts, shards):
    n = len(ts)

    def body(*refs):
        g = _Gather(ts, refs[:n], refs[n:2 * n], refs[2 * n:3 * n], refs[3 * n:])
        g.start()
        g.forward()
        g.finish()

    return pl.pallas_call(
        body, name="gather_weights",
        out_shape=[jax.ShapeDtypeStruct(_BIG[t][1], BF16) for t in ts],
        in_specs=[_ANY] * n, out_specs=[_ANY] * n,
        scratch_shapes=_Gather.scratch(shards),
        compiler_params=pltpu.CompilerParams(vmem_limit_bytes=VMEM_LIMIT_V7X),
    )(*shards)


SMALL_W = 1024
SMALL_ROWS = 72
N_SMALL = 8 * SMALL_ROWS * SMALL_W
_RED = tuple((shape, ax, (shape[0] // 2, sz) if ax == 1 else (sz // 2, shape[1]), BF16) for _, shape, ax, sz in _BIG) + \
    (((8 * SMALL_ROWS, SMALL_W), 0, (SMALL_ROWS, SMALL_W), F32),)
_RED_TR = 128


def _piece(ref, t, kk, cc):
    _, ax, (pr, pc), _ = _RED[t]
    if ax == 1:
        return ref.at[pl.ds(pl.multiple_of(cc * pr, pr), pr), pl.ds(pl.multiple_of(kk * pc, pc), pc)]
    return ref.at[pl.ds(pl.multiple_of((2 * kk + cc) * pr, pr), pr), :]


def _half_shape(t):
    shape, ax, (pr, pc), _ = _RED[t]
    return (pr, shape[1]) if ax == 1 else (N_CHIPS * pr, pc)


def _piece_in_half(ref, t, kk):
    _, ax, (pr, pc), _ = _RED[t]
    return ref.at[:, pl.ds(pl.multiple_of(kk * pc, pc), pc)] if ax == 1 else ref.at[pl.ds(pl.multiple_of(kk * pr, pr), pr), :]


def _grad_sibling_exchange(ts, grads, name):
    n = len(ts)
    n_dma = sum(1 if _RED[t][1] == 1 else N_CHIPS for t in ts)

    def body(*refs):
        srcs, dsts, (send_sems, recv_sems) = refs[:n], refs[n:2 * n], refs[2 * n:]
        x, y, c, _ = _place()
        pairs = []
        for i, t in enumerate(ts):
            _, ax, (pr, _), _ = _RED[t]
            if ax == 1:
                pairs.append((srcs[i].at[pl.ds(pl.multiple_of((1 - c) * pr, pr), pr), :], dsts[i]))
            else:
                pairs += [(_piece(srcs[i], t, kk, 1 - c), _piece_in_half(dsts[i], t, kk)) for kk in range(N_CHIPS)]
        cps = [pltpu.make_async_remote_copy(src_ref=s, dst_ref=d, send_sem=send_sems.at[i], recv_sem=recv_sems.at[i],
                                            device_id=(x, y, 1 - c), device_id_type=_MESH) for i, (s, d) in enumerate(pairs)]
        for cp in cps:
            cp.start()
        for cp in cps:
            cp.wait()

    return pl.pallas_call(
        body, name=name,
        out_shape=[jax.ShapeDtypeStruct(_half_shape(t), F32) for t in ts],
        in_specs=[_ANY] * n, out_specs=[_ANY] * n,
        scratch_shapes=[pltpu.SemaphoreType.DMA((n_dma,)), pltpu.SemaphoreType.DMA((n_dma,))],
    )(*grads)


def _chip_sum(t, g, rs, place):
    shape, ax, (pr, pc), dt = _RED[t]
    W = shape[1]
    tr = min(pr, _RED_TR)
    nb = pr // tr

    def body(place_ref, g_ref, rs_ref, o_ref):
        o_ref[...] = (g_ref[...] + rs_ref[...]).astype(o_ref.dtype)

    return pl.pallas_call(
        body, name="grad_chip_sum_%d" % t,
        out_shape=jax.ShapeDtypeStruct(rs.shape, dt),
        grid_spec=pltpu.PrefetchScalarGridSpec(
            num_scalar_prefetch=1, grid=(1 if ax == 1 else N_CHIPS, nb),
            in_specs=[pl.BlockSpec((tr, W), lambda kk, i, pr_: ((2 * kk + pr_[0]) * nb + i, 0)),
                      pl.BlockSpec((tr, W), lambda kk, i, pr_: (kk * nb + i, 0))],
            out_specs=pl.BlockSpec((tr, W), lambda kk, i, pr_: (kk * nb + i, 0))),
        compiler_params=_params(("parallel", "parallel")),
    )(place, g, rs)


class _ChipExchange:
    def __init__(self, ts, srcs, dsts, sems):
        self.send_sems, self.recv_sems = sems
        x, y, c, chips = _place()
        self.copies = lambda: [
            pltpu.make_async_remote_copy(src_ref=_piece_in_half(srcs[i], t, 2 * px + py), dst_ref=dsts[i].at[j],
                                         send_sem=self.send_sems.at[j, i], recv_sem=self.recv_sems.at[j, i],
                                         device_id=(px, py, c), device_id_type=_MESH)
            for i, t in enumerate(ts) for j, (px, py) in enumerate(chips)]

    @staticmethod
    def scratch(ts):
        return [pltpu.SemaphoreType.DMA((3, len(ts))), pltpu.SemaphoreType.DMA((3, len(ts)))]

    @staticmethod
    def out_shape(ts):
        return [jax.ShapeDtypeStruct((3,) + _RED[t][2], _RED[t][3]) for t in ts]

    def start(self):
        for cp in self.copies():
            cp.start()

    def finish(self):
        for cp in self.copies():
            cp.wait()


def _grad_chip_exchange(ts, sums):
    n = len(ts)

    def body(*refs):
        ex = _ChipExchange(ts, refs[:n], refs[n:2 * n], refs[2 * n:])
        ex.start()
        ex.finish()

    return pl.pallas_call(
        body, name="grad_chip_exchange",
        out_shape=_ChipExchange.out_shape(ts),
        in_specs=[_ANY] * n, out_specs=[_ANY] * n,
        scratch_shapes=_ChipExchange.scratch(ts),
    )(*sums)


def _total_sum(t, g, rs, rc, place):
    shape, ax, (pr, pc), _ = _RED[t]
    tr = min(pr, _RED_TR)
    nb = pr // tr
    small = t == len(_RED) - 1

    def body(place_ref, g_ref, rs_ref, rc_ref, o_ref):
        o_ref[...] = (g_ref[...] + rs_ref[...]) + rc_ref[0].astype(F32) + rc_ref[1].astype(F32) + rc_ref[2].astype(F32)

    if ax == 1:
        g_map = lambda i, pr_: (pr_[0] * nb + i, pr_[1])
        rs_map = lambda i, pr_: (i, pr_[1])
    else:
        g_map = lambda i, pr_: ((2 * pr_[1] + pr_[0]) * nb + i, 0)
        rs_map = lambda i, pr_: (pr_[1] * nb + i, 0)
    o_map = (lambda i, pr_: ((2 * pr_[1] + pr_[0]) * nb + i, 0)) if small else (lambda i, pr_: (pr_[0] * nb + i, 0))
    return pl.pallas_call(
        body, name="grad_total_sum_%d" % t,
        out_shape=jax.ShapeDtypeStruct(((8 if small else 2) * pr, pc), F32),
        grid_spec=pltpu.PrefetchScalarGridSpec(
            num_scalar_prefetch=1, grid=(nb,),
            in_specs=[pl.BlockSpec((tr, pc), g_map), pl.BlockSpec((tr, pc), rs_map),
                      pl.BlockSpec((3, tr, pc), lambda i, pr_: (0, i, 0))],
            out_specs=pl.BlockSpec((tr, pc), o_map)),
        compiler_params=_params(("parallel",)),
    )(place, g, rs, rc)


def _grad_final_exchange(totals):
    n = len(_RED)
    nb = n - 1

    def body(*refs):
        srcs, dsts, (send_sems, recv_sems) = refs[:n], refs[n:2 * n], refs[2 * n:]
        x, y, c, chips = _place()
        me = 4 * x + 2 * y + c
        others = [(x, y, 1 - c)] + [(px, py, cc) for (px, py) in chips for cc in (c, 1 - c)]

        def half(ref, t, cc):
            pr = _RED[t][2][0]
            return ref.at[pl.ds(pl.multiple_of(cc * pr, pr), pr), :]

        def eighth(ref, dev):
            return ref.at[pl.ds(pl.multiple_of(dev * SMALL_ROWS, SMALL_ROWS), SMALL_ROWS), :]

        def big_copy(t, cc):
            return pltpu.make_async_remote_copy(src_ref=half(srcs[t], t, cc), dst_ref=half(dsts[t], t, cc), send_sem=send_sems.at[t],
                                                recv_sem=recv_sems.at[t], device_id=others[0], device_id_type=_MESH)

        def small_copy(i, dev):
            return pltpu.make_async_remote_copy(src_ref=eighth(srcs[nb], dev), dst_ref=eighth(dsts[nb], dev),
                                                send_sem=send_sems.at[nb + i], recv_sem=recv_sems.at[nb + i],
                                                device_id=others[i], device_id_type=_MESH)

        sends = [big_copy(t, c) for t in range(nb)] + [small_copy(i, me) for i in range(7)]
        for cp in sends:
            cp.start()
        for t in range(nb):
            big_copy(t, 1 - c).wait_recv()
        for i, (px, py, pc) in enumerate(others):
            small_copy(i, 4 * px + 2 * py + pc).wait_recv()
        for cp in sends:
            cp.wait_send()

    return pl.pallas_call(
        body, name="grad_final_exchange",
        out_shape=[jax.ShapeDtypeStruct(a.shape, F32) for a in totals],
        in_specs=[_ANY] * n, out_specs=[_ANY] * n,
        input_output_aliases={t: t for t in range(n)},
        scratch_shapes=[pltpu.SemaphoreType.DMA((nb + 7,)), pltpu.SemaphoreType.DMA((nb + 7,))],
    )(*totals)


def _grad_place():
    return jnp.stack([lax.axis_index("c"), 2 * lax.axis_index("x") + lax.axis_index("y")]).astype(jnp.int32)


def _reduce_begin(ts, grads, place, tag):
    from_sibling = _grad_sibling_exchange(ts, grads, "grad_sibling_exchange_" + tag)
    return from_sibling, [_chip_sum(t, g, r, place) for t, g, r in zip(ts, grads, from_sibling)]


def _reduce_end(ts, grads, from_sibling, from_chips, place):
    return [_total_sum(t, g, r, q, place) for t, g, r, q in zip(ts, grads, from_sibling, from_chips)]


_EARLY = (1, 2, 3, 4)
_W_IN = (0,)
_SMALL_RED = (5,)


def _adamw_math(w, g, m, v):
    m = ADAM_B1 * m + (1.0 - ADAM_B1) * g
    v = ADAM_B2 * v + (1.0 - ADAM_B2) * (g * g)
    m_hat = m / (1.0 - ADAM_B1 ** ADAM_STEP)
    v_hat = v / (1.0 - ADAM_B2 ** ADAM_STEP)
    return -ADAM_LR * (m_hat / (jnp.sqrt(v_hat) + ADAM_EPS) + ADAM_WD * w), m, v


def _adamw(w, g, m, v, name):
    W = w.shape[1]
    return _rowwise(_adamw_math, [(a, 0, W) for a in (w, g, m, v)], [], [(W, F32)] * 3, tr=128, name=name)


def _adamw_whole(ws, gs, ms, vs, name):
    n = len(ws)

    def body(*refs):
        ins, outs = refs[:4 * n], refs[4 * n:]
        for i in range(n):
            res = _adamw_math(*[ins[j * n + i][...] for j in range(4)])
            for j in range(3):
                outs[j * n + i][...] = res[j]

    res = pl.pallas_call(
        body, name=name,
        out_shape=[jax.ShapeDtypeStruct(a.shape, F32) for a in ws] * 3,
        compiler_params=pltpu.CompilerParams(vmem_limit_bytes=VMEM_LIMIT_V7X),
    )(*ws, *gs, *ms, *vs)
    return res[:n], res[n:2 * n], res[2 * n:]


def _chunks(arr, off, width, w=512):
    return [(arr, off + i * w, w) for i in range(width // w)]


def _cat(vs):
    return jnp.concatenate(vs, axis=1)


def _forward_backward(x, p_b, tgt, shards, small):
    L = x.shape[0]
    w_in, = _gather_weights([0], shards[:1])
    row = lambda v: v.reshape(1, -1)
    g_mix, g_ple, g_fin = row(small["norm_mix"]), row(small["norm_ple"]), row(small["norm_final"])
    gq, gk, b_glu = row(small["q_norm"]), row(small["k_norm"]), row(small["b_glu"])
    cos, sin = _rope_tables(L)

    hn_b, = _rowwise(lambda x, g: x * _rms(x) * g, [(x, 0, D_MODEL)], [g_mix], [(D_MODEL, BF16)], name="norm_mix")
    z = _matmul(hn_b, w_in, name="mm_in")
    qr, kr, vb = _attn_prep(z, gq, gk, cos, sin)
    o, lse, w_glu, w_out, w_pg, w_pp = _attn_fwd(qr, kr, vb, [1, 2, 3, 4], shards[1:])

    ssm_names = ("ssm_a_re", "ssm_a_im", "ssm_log_dt", "ssm_b_re", "ssm_b_im")
    (lre, lim, bre, bim), disc_vjp = jax.vjp(_ssm_disc, *[small[n][0] for n in ssm_names])
    ssm = (_to_pairs(jnp.swapaxes(jnp.stack([bre, bim], axis=2), -1, -2)),
           _to_pairs(jnp.stack([small["ssm_c_re"][0], -small["ssm_c_im"][0]], axis=2)),
           _to_lam(lre), _to_lam(lim), row(small["ssm_d"]))
    u_p = _seg_perm(z[:, Z_U:Z_U + D_SSM])
    y_s = _seg_unperm(_ssm_fwd(u_p, *ssm))
    ge_b, = _rowwise(_gelu, [(y_s, 0, D_SSM)], [], [(D_SSM, BF16)], name="gelu")
    glu = _matmul(ge_b, w_glu, name="mm_glu")

    def merge(ga0, ga1, a, b, gs0, gs1, o, bias):
        sa, _ = _silu_and_grad(_cat([ga0, ga1]))
        ss, _ = _silu_and_grad(_cat([gs0, gs1]))
        y2 = (a + bias[:, :D_SSM]) * _sig(b + bias[:, D_SSM:])
        return _cat([o * sa, y2 * ss])
    merge_rows = _chunks(z, Z_GA, D_ATTN) + [(glu, 0, D_SSM), (glu, D_SSM, D_SSM)] + _chunks(z, Z_GS, D_SSM) + [(o, 0, D_ATTN)]
    cat_b, = _rowwise(merge, merge_rows, [b_glu], [(D_MODEL, BF16)], name="merge")
    t_out = _matmul(cat_b, w_out, name="mm_out")

    def resid(x, t, g):
        h1 = x + t
        return h1, h1 * _rms(h1) * g
    h1, hp_b = _rowwise(resid, [(x, 0, D_MODEL), (t_out, 0, D_MODEL)], [g_ple], [(D_MODEL, F32), (D_MODEL, BF16)], name="resid_norm")
    gl = _matmul(hp_b, w_pg, name="mm_ple_gate")
    pp = _matmul(p_b, w_pp, name="mm_ple_proj")

    def head(h1, gl, pp, tgt, g):
        gate = _sig(gl)
        h2 = h1 + gate * pp
        r = _rms(h2)
        n = h2 * r
        err = n * g - tgt
        dy = err * (1.0 / D_MODEL)
        dn = dy * g
        dh2 = r * (dn - n * jnp.mean(dn * n, axis=-1, keepdims=True))
        dgate = dh2 * pp
        return dh2, dh2 * gate, dgate * gate * (1.0 - gate), _colsum(dy * n), _colsum(0.5 * err * err * (1.0 / D_MODEL))
    dh2, dpp_b, dgl_b, dg_fin, loss_cols = _rowwise(
        head, [(a, 0, D_MODEL) for a in (h1, gl, pp, tgt)], [g_fin],
        [(D_MODEL, F32), (D_MODEL, BF16), (D_MODEL, BF16)], [(1, D_MODEL), (1, D_MODEL)], name="loss_head")

    dw_pp = _matmul(p_b, dpp_b, ta=True, name="mm_d_w_ple_proj")
    dw_pg = _matmul(hp_b, dgl_b, ta=True, name="mm_d_w_ple_gate")
    dhp = _matmul(dgl_b, w_pg, tb=True, name="mm_d_hp")

    def resid_bwd(dhp, h1, dh2, g):
        dx, dg = _rms_bwd(h1, g, dhp)
        dh1 = dh2 + dx
        return dh1, dh1, _colsum(dg)
    dh1, dh1_b, dg_ple = _rowwise(resid_bwd, [(a, 0, D_MODEL) for a in (dhp, h1, dh2)], [g_ple],
                                  [(D_MODEL, F32), (D_MODEL, BF16)], [(1, D_MODEL)], name="resid_norm_bwd")
    dw_out = _matmul(cat_b, dh1_b, ta=True, name="mm_d_w_out")
    dcat = _matmul(dh1_b, w_out, tb=True, name="mm_d_cat")

    def merge_bwd(dya, dys, ga0, ga1, a, b, gs0, gs1, o, bias):
        ga, gs = _cat([ga0, ga1]), _cat([gs0, gs1])
        sa, dsa = _silu_and_grad(ga)
        ss, dss = _silu_and_grad(gs)
        a, sb = a + bias[:, :D_SSM], _sig(b + bias[:, D_SSM:])
        dy2 = dys * ss
        dglu = _cat([dy2 * sb, dy2 * a * sb * (1.0 - sb)])
        return dya * sa, dya * o * dsa, dys * (a * sb) * dss, dglu, _colsum(dglu)
    do_b, dga_b, dgs_b, dglu_b, db_glu = _rowwise(
        merge_bwd, [(dcat, 0, D_ATTN), (dcat, D_ATTN, D_SSM)] + merge_rows, [b_glu],
        [(D_ATTN, BF16), (D_ATTN, BF16), (D_SSM, BF16), (2 * D_SSM, BF16)], [(1, 2 * D_SSM)], name="merge_bwd")
    dw_glu = _matmul(ge_b, dglu_b, ta=True, name="mm_d_w_glu")
    dge = _matmul(dglu_b, w_glu, tb=True, name="mm_d_ge")
    dy_s, = _rowwise(lambda dge, y: dge * _gelu_grad(y), [(dge, 0, D_SSM), (y_s, 0, D_SSM)], [], [(D_SSM, F32)], name="gelu_bwd")
    du_p, dwb, dwc, dare, daim, d_ssm_d = _ssm_bwd(u_p, _seg_perm(dy_s), *ssm)
    dcc, dbb = _from_pairs(dwc), jnp.swapaxes(_from_pairs(dwb), -1, -2)
    dc_re, dc_im = dcc[:, :, 0], -dcc[:, :, 1]
    da_re, da_im, dlog_dt, db_re, db_im = disc_vjp((_from_lam(dare), _from_lam(daim), dbb[:, :, 0], dbb[:, :, 1]))

    place = _grad_place()
    early_grads = [dw_glu, dw_out, dw_pg, dw_pp]
    early_sib, early_sums = _reduce_begin(_EARLY, early_grads, place, "early")
    dqr, dkr, dv, *early_chips = _attn_bwd(qr, kr, kr.T, vb, do_b, lse.reshape(N_HEADS, 1, L), _EARLY, early_sums)
    early_totals = _reduce_end(_EARLY, early_grads, early_sib, early_chips, place)
    dq_b, dk_b, dgq, dgk = _attn_prep_bwd(dqr, dkr, z, gq, gk, cos, sin)
    dz_b = _cat([dq_b, dk_b, dv.astype(BF16), dga_b, _seg_unperm(du_p).astype(BF16), dgs_b])
    dw_in = _matmul(hn_b, dz_b, ta=True, name="mm_d_w_in")
    w_in_sib, w_in_sums = _reduce_begin(_W_IN, [dw_in], place, "w_in")
    dhn, *w_in_chips = _matmul(dz_b, w_in, tb=True, name="mm_d_hn", exchange=(_W_IN, w_in_sums))
    w_in_total, = _reduce_end(_W_IN, [dw_in], w_in_sib, w_in_chips, place)

    def norm_bwd(dhn, x, dh1, g):
        dx, dg = _rms_bwd(x, g, dhn)
        return dh1 + dx, _colsum(dg)
    grad_x, dg_mix = _rowwise(norm_bwd, [(a, 0, D_MODEL) for a in (dhn, x, dh1)], [g_mix], [(D_MODEL, F32)], [(1, D_MODEL)],
                              name="norm_mix_bwd")

    small_grads = {"norm_mix": dg_mix, "q_norm": dgq, "k_norm": dgk, "ssm_a_re": da_re, "ssm_a_im": da_im, "ssm_log_dt": dlog_dt,
                   "ssm_b_re": db_re, "ssm_b_im": db_im, "ssm_c_re": dc_re, "ssm_c_im": dc_im, "ssm_d": d_ssm_d,
                   "b_glu": db_glu, "norm_ple": dg_ple, "norm_final": dg_fin}
    return jnp.sum(loss_cols), grad_x, [w_in_total] + early_totals, small_grads, place


_SMALL = ("norm_mix", "q_norm", "k_norm", "ssm_a_re", "ssm_a_im", "ssm_log_dt", "ssm_b_re", "ssm_b_im", "ssm_c_re", "ssm_c_im",
          "ssm_d", "b_glu", "norm_ple", "norm_final")
_WEIGHTS = ("norm_mix", "w_in", "q_norm", "k_norm", "ssm_a_re", "ssm_a_im", "ssm_log_dt", "ssm_b_re", "ssm_b_im", "ssm_c_re",
            "ssm_c_im", "ssm_d", "w_glu", "b_glu", "w_out", "norm_ple", "w_ple_gate", "w_ple_proj", "norm_final")


_SMALL_ADAMW_GROUPS = (("ssm_b_re",), ("ssm_b_im",), ("ssm_c_re", "ssm_c_im"),
                       ("norm_mix", "q_norm", "k_norm", "ssm_a_re", "ssm_a_im", "ssm_log_dt", "ssm_d", "b_glu", "norm_ple",
                        "norm_final"))


def _flat_small(d):
    flat = jnp.concatenate([d[n].reshape(-1).astype(F32) for n in _SMALL])
    return jnp.pad(flat, (0, N_SMALL - flat.shape[0]))


def _split_small(flat, like):
    out, off = {}, 0
    for n in _SMALL:
        sz = math.prod(like[n].shape)
        out[n] = flat[off:off + sz].reshape(like[n].shape)
        off += sz
    return out


def kernel(x, p, norm_mix, w_in, q_norm, k_norm, ssm_a_re, ssm_a_im, ssm_log_dt, ssm_b_re, ssm_b_im, ssm_c_re, ssm_c_im, ssm_d, w_glu, b_glu, w_out, norm_ple, w_ple_gate, w_ple_proj, norm_final, loss_target, m_norm_mix, m_w_in, m_q_norm, m_k_norm, m_ssm_a_re, m_ssm_a_im, m_ssm_log_dt, m_ssm_b_re, m_ssm_b_im, m_ssm_c_re, m_ssm_c_im, m_ssm_d, m_w_glu, m_b_glu, m_w_out, m_norm_ple, m_w_ple_gate, m_w_ple_proj, m_norm_final, v_norm_mix, v_w_in, v_q_norm, v_k_norm, v_ssm_a_re, v_ssm_a_im, v_ssm_log_dt, v_ssm_b_re, v_ssm_b_im, v_ssm_c_re, v_ssm_c_im, v_ssm_d, v_w_glu, v_b_glu, v_w_out, v_norm_ple, v_w_ple_gate, v_w_ple_proj, v_norm_final):
    w = dict(norm_mix=norm_mix, w_in=w_in, q_norm=q_norm, k_norm=k_norm, ssm_a_re=ssm_a_re, ssm_a_im=ssm_a_im,
             ssm_log_dt=ssm_log_dt, ssm_b_re=ssm_b_re, ssm_b_im=ssm_b_im, ssm_c_re=ssm_c_re, ssm_c_im=ssm_c_im, ssm_d=ssm_d,
             w_glu=w_glu, b_glu=b_glu, w_out=w_out, norm_ple=norm_ple, w_ple_gate=w_ple_gate, w_ple_proj=w_ple_proj,
             norm_final=norm_final)
    m = dict(norm_mix=m_norm_mix, w_in=m_w_in, q_norm=m_q_norm, k_norm=m_k_norm, ssm_a_re=m_ssm_a_re, ssm_a_im=m_ssm_a_im,
             ssm_log_dt=m_ssm_log_dt, ssm_b_re=m_ssm_b_re, ssm_b_im=m_ssm_b_im, ssm_c_re=m_ssm_c_re, ssm_c_im=m_ssm_c_im,
             ssm_d=m_ssm_d, w_glu=m_w_glu, b_glu=m_b_glu, w_out=m_w_out, norm_ple=m_norm_ple, w_ple_gate=m_w_ple_gate,
             w_ple_proj=m_w_ple_proj, norm_final=m_norm_final)
    v = dict(norm_mix=v_norm_mix, w_in=v_w_in, q_norm=v_q_norm, k_norm=v_k_norm, ssm_a_re=v_ssm_a_re, ssm_a_im=v_ssm_a_im,
             ssm_log_dt=v_ssm_log_dt, ssm_b_re=v_ssm_b_re, ssm_b_im=v_ssm_b_im, ssm_c_re=v_ssm_c_re, ssm_c_im=v_ssm_c_im,
             ssm_d=v_ssm_d, w_glu=v_w_glu, b_glu=v_b_glu, w_out=v_w_out, norm_ple=v_norm_ple, w_ple_gate=v_w_ple_gate,
             w_ple_proj=v_w_ple_proj, norm_final=v_norm_final)
    big_names = [n for n, _, _, _ in _BIG]

    small = {n: w[n] for n in _SMALL}
    loss_part, grad_x, big_totals, small_grads, place = _forward_backward(
        x[0], p[0, 0].astype(BF16), loss_target[0], [w[n][0].astype(BF16) for n in big_names], small)
    loss = lax.psum(loss_part, ("x", "y", "c"))

    small_flat = [_flat_small(small_grads).reshape(8 * SMALL_ROWS, SMALL_W)]
    small_sib, small_sums = _reduce_begin(_SMALL_RED, small_flat, place, "small")
    small_total = _reduce_end(_SMALL_RED, small_flat, small_sib, _grad_chip_exchange(_SMALL_RED, small_sums), place)
    *big_red, small_red = _grad_final_exchange(big_totals + small_total)
    grads = _split_small(small_red.reshape(-1), w)
    delta, new_m, new_v = {}, {}, {}
    for n, g in zip(big_names, big_red):
        grads[n] = g[None]
        d_, m_, v_ = _adamw(w[n][0], g, m[n][0], v[n][0], "adamw_" + n)
        delta[n], new_m[n], new_v[n] = d_[None], m_[None], v_[None]
    at_least_2d = lambda a: a.reshape(1, -1) if a.ndim == 1 else a
    for i, names in enumerate(_SMALL_ADAMW_GROUPS):
        d_, m_, v_ = _adamw_whole(*[[at_least_2d(src[n]) for n in names] for src in (w, grads, m, v)], "adamw_small_%d" % i)
        for j, n in enumerate(names):
            delta[n], new_m[n], new_v[n] = (a[j].reshape(w[n].shape) for a in (d_, m_, v_))
    return (loss, grad_x[None], *[grads[n] for n in _WEIGHTS], *[delta[n] for n in _WEIGHTS],
            *[new_m[n] for n in _WEIGHTS], *[new_v[n] for n in _WEIGHTS])
```

```python
import functools
import math

import jax
import jax.numpy as jnp
import numpy as np
from jax import lax
from jax.experimental import pallas as pl
from jax.experimental.pallas import tpu as pltpu

D_MODEL = 2048
GRID_W = 64
PLE_DIM = 256
D_ATTN = 1024
N_HEADS = 8
N_KV = 2
HEAD_DIM = 128
ROPE_THETA = 10000.0
D_SSM = 1024
SSM_H = 16
SSM_G = 64
SSM_P = 64
D_KV = N_KV * HEAD_DIM
D_IN = 2 * D_ATTN + 2 * D_KV + 2 * D_SSM
EPS = 1e-6
Z_Q, Z_K, Z_V, Z_GA, Z_U, Z_GS = 0, 1024, 1280, 1536, 2560, 3584

ADAM_LR, ADAM_B1, ADAM_B2, ADAM_EPS, ADAM_WD, ADAM_STEP = 0.001, 0.9, 0.999, 1e-08, 0.01, 10

N_CHIPS = 4
VMEM_LIMIT_V7X = 56 * 1024 * 1024
F32 = jnp.float32
BF16 = jnp.bfloat16


def _params(sem, vmem=VMEM_LIMIT_V7X):
    return pltpu.CompilerParams(dimension_semantics=sem, vmem_limit_bytes=vmem)


def _matmul(a, b, *, ta=False, tb=False, out_dtype=F32, tm=1024, tn=None, name, exchange=None):
    M, K = (a.shape[1], a.shape[0]) if ta else a.shape
    N = b.shape[0] if tb else b.shape[1]
    if tn is None:
        tn = 1024 if (N % 1024 == 0 and K <= 4096) else 512
    tm, tn = min(tm, M), min(tn, N)
    assert M % tm == 0 and N % tn == 0, (name, M, N, K)
    dims = (((0 if ta else 1,), (1 if tb else 0,)), ((), ()))
    ex_ts, ex_sums = exchange if exchange else ((), ())
    n_ex = len(ex_ts)
    gm, gn = M // tm, N // tn

    def body(a_ref, b_ref, *rest):
        o_ref = rest[n_ex]
        if n_ex:
            ex = _ChipExchange(ex_ts, rest[:n_ex], rest[n_ex + 1:2 * n_ex + 1], rest[2 * n_ex + 1:])
            step = pl.program_id(0) * gn + pl.program_id(1)
            pl.when(step == 0)(ex.start)
        o_ref[...] = lax.dot_general(a_ref[...], b_ref[...], dims, preferred_element_type=F32).astype(o_ref.dtype)
        if n_ex:
            pl.when(step == gm * gn - 1)(ex.finish)

    a_spec = pl.BlockSpec((K, tm), lambda i, j: (0, i)) if ta else pl.BlockSpec((tm, K), lambda i, j: (i, 0))
    b_spec = pl.BlockSpec((tn, K), lambda i, j: (j, 0)) if tb else pl.BlockSpec((K, tn), lambda i, j: (0, j))
    res = pl.pallas_call(
        body, name=name,
        out_shape=[jax.ShapeDtypeStruct((M, N), out_dtype)] + (_ChipExchange.out_shape(ex_ts) if n_ex else []),
        grid=(gm, gn),
        in_specs=[a_spec, b_spec] + [_ANY] * n_ex,
        out_specs=[pl.BlockSpec((tm, tn), lambda i, j: (i, j))] + [_ANY] * n_ex,
        scratch_shapes=_ChipExchange.scratch(ex_ts) if n_ex else [],
        compiler_params=_params(("arbitrary", "arbitrary") if n_ex else ("parallel", "parallel")),
    )(a, b, *ex_sums)
    return res if n_ex else res[0]


def _rowwise(fn, rows, consts, outs, accs=(), *, tr=256, name):
    L = rows[0][0].shape[0]
    tr = math.gcd(tr, L)
    assert tr % 8 == 0 or tr == L, (name, L, tr)
    n_in, n_c, n_o, n_a = len(rows), len(consts), len(outs), len(accs)

    def body(*refs):
        ins = [r[...] for r in refs[:n_in + n_c]]
        res = fn(*ins)
        if not isinstance(res, (tuple, list)):
            res = (res,)
        o_refs = refs[n_in + n_c:n_in + n_c + n_o]
        a_refs = refs[n_in + n_c + n_o:]
        for r, v in zip(o_refs, res[:n_o]):
            r[...] = v.astype(r.dtype)
        if n_a:
            first = pl.program_id(0) == 0

            @pl.when(first)
            def _():
                for r, v in zip(a_refs, res[n_o:]):
                    r[...] = v.astype(F32)

            @pl.when(jnp.logical_not(first))
            def _():
                for r, v in zip(a_refs, res[n_o:]):
                    r[...] += v.astype(F32)

    in_specs = []
    for arr, off, w in rows:
        assert off % w == 0, (name, off, w)
        in_specs.append(pl.BlockSpec((tr, w), functools.partial(lambda i, c: (i, c), c=off // w)))
    for c in consts:
        in_specs.append(pl.BlockSpec(c.shape, lambda i: (0, 0)))
    out_shape = [jax.ShapeDtypeStruct((L, w), dt) for w, dt in outs] + [jax.ShapeDtypeStruct(s, F32) for s in accs]
    out_specs = [pl.BlockSpec((tr, w), lambda i: (i, 0)) for w, _ in outs] + [pl.BlockSpec(s, lambda i: (0, 0)) for s in accs]
    res = pl.pallas_call(
        body, name=name,
        out_shape=out_shape,
        grid=(L // tr,),
        in_specs=in_specs,
        out_specs=out_specs,
        compiler_params=_params(("arbitrary",) if n_a else ("parallel",)),
    )(*[r[0] for r in rows], *consts)
    return res


def _sig(x):
    return jax.nn.sigmoid(x)


def _silu_and_grad(x):
    s = _sig(x)
    return x * s, s * (1.0 + x * (1.0 - s))


_GELU_C = math.sqrt(2.0 / math.pi)


def _gelu(x):
    return 0.5 * x * (1.0 + jnp.tanh(_GELU_C * (x + 0.044715 * x * x * x)))


def _gelu_grad(x):
    t = jnp.tanh(_GELU_C * (x + 0.044715 * x * x * x))
    return 0.5 * (1.0 + t) + 0.5 * x * (1.0 - t * t) * _GELU_C * (1.0 + 3.0 * 0.044715 * x * x)


def _rms(x):
    return lax.rsqrt(jnp.mean(x * x, axis=-1, keepdims=True) + EPS)


def _rms_bwd(x, g, dy):
    r = _rms(x)
    n = x * r
    dn = dy * g
    return r * (dn - n * jnp.mean(dn * n, axis=-1, keepdims=True)), dy * n


def _colsum(v):
    return jnp.sum(v, axis=0, keepdims=True)


def _rope_partner(x):
    lane = lax.broadcasted_iota(jnp.int32, x.shape, x.ndim - 1)
    return jnp.where(lane % 64 < 32, pltpu.roll(x, 96, x.ndim - 1), pltpu.roll(x, 32, x.ndim - 1))


def _rope_tables(L):
    rows_n = L // GRID_W
    rows = jnp.repeat(jnp.arange(rows_n), GRID_W).astype(F32)
    cols = jnp.tile(jnp.arange(GRID_W), rows_n).astype(F32)
    n_freq = HEAD_DIM // 4
    inv_freq = ROPE_THETA ** (-jnp.arange(n_freq, dtype=F32) / n_freq)
    ar, ac = rows[:, None] * inv_freq[None, :], cols[:, None] * inv_freq[None, :]
    cos = jnp.concatenate([jnp.cos(ar), jnp.cos(ar), jnp.cos(ac), jnp.cos(ac)], axis=-1)
    sin = jnp.concatenate([-jnp.sin(ar), jnp.sin(ar), -jnp.sin(ac), jnp.sin(ac)], axis=-1)
    return cos, sin


def _heads(v):
    return [v[:, h * HEAD_DIM:(h + 1) * HEAD_DIM] for h in range(v.shape[1] // HEAD_DIM)]


def _attn_prep(z, q_norm, k_norm, cos, sin):
    def fn(q, k, v, cos, sin, gq, gk):
        def one(xh, g):
            xn = xh * _rms(xh) * g
            return xn * cos + _rope_partner(xn) * sin
        qr = jnp.concatenate([one(h, gq) for h in _heads(q)], axis=1)
        kr = jnp.concatenate([one(h, gk) for h in _heads(k)], axis=1)
        return qr, kr, v
    return _rowwise(fn, [(z, Z_Q, D_ATTN), (z, Z_K, D_KV), (z, Z_V, D_KV), (cos, 0, HEAD_DIM), (sin, 0, HEAD_DIM)],
                    [q_norm, k_norm], [(D_ATTN, BF16), (D_KV, BF16), (D_KV, BF16)], name="attn_prep")


def _attn_prep_bwd(dqr, dkr, z, q_norm, k_norm, cos, sin):
    def fn(dqr, dkr, q, k, cos, sin, gq, gk):
        def one(dyh, xh, g):
            dn = dyh * cos + _rope_partner(dyh * sin)
            return _rms_bwd(xh, g, dn)
        rq = [one(a, b, gq) for a, b in zip(_heads(dqr), _heads(q))]
        rk = [one(a, b, gk) for a, b in zip(_heads(dkr), _heads(k))]
        dq = jnp.concatenate([r[0] for r in rq], axis=1)
        dk = jnp.concatenate([r[0] for r in rk], axis=1)
        return dq, dk, _colsum(sum(r[1] for r in rq)), _colsum(sum(r[1] for r in rk))
    return _rowwise(fn, [(dqr, 0, D_ATTN), (dkr, 0, D_KV), (z, Z_Q, D_ATTN), (z, Z_K, D_KV), (cos, 0, HEAD_DIM), (sin, 0, HEAD_DIM)],
                    [q_norm, k_norm], [(D_ATTN, BF16), (D_KV, BF16)], [(1, HEAD_DIM), (1, HEAD_DIM)], name="attn_prep_bwd")


_QK_T = (((1,), (1,)), ((), ()))
_TA = (((0,), (0,)), ((), ()))
_REP = N_HEADS // N_KV


_EXP2_SCALE = HEAD_DIM ** -0.5 * math.log2(math.e)
ATTN_FWD_KEY_CHUNKS = 4
ATTN_BWD_KEY_CHUNKS = 8


def _attn_fwd(qr, kr, vb, g_ts, g_shards, *, tq=1024):
    L = qr.shape[0]
    tq = min(tq, L)
    kc = L // ATTN_FWD_KEY_CHUNKS
    n_g = len(g_ts)
    grid = (N_HEADS, L // tq)
    steps = grid[0] * grid[1]

    def body(q_ref, k_ref, v_ref, *rest):
        o_ref, lse_ref = rest[n_g:n_g + 2]
        g = _Gather(g_ts, rest[:n_g], rest[n_g + 2:2 * n_g + 2], rest[2 * n_g + 2:3 * n_g + 2], rest[3 * n_g + 2:])
        step = pl.program_id(0) * grid[1] + pl.program_id(1)
        pl.when(step == 0)(g.start)
        pl.when(step == (3 * steps) // 4)(g.forward)
        q = q_ref[...]
        m = jnp.full((tq, 1), -jnp.inf, F32)
        l = jnp.zeros((tq, 1), F32)
        o = jnp.zeros((tq, HEAD_DIM), F32)
        for c in range(ATTN_FWD_KEY_CHUNKS):
            ks = slice(c * kc, (c + 1) * kc)
            s = lax.dot_general(q, k_ref[ks, :], _QK_T, preferred_element_type=F32)
            m_new = jnp.maximum(m, jnp.max(s, axis=1, keepdims=True))
            a = jnp.exp2((m - m_new) * _EXP2_SCALE)
            p = jnp.exp2((s - m_new) * _EXP2_SCALE)
            l = a * l + jnp.sum(p, axis=1, keepdims=True)
            o = a * o + jnp.dot(p.astype(BF16), v_ref[ks, :], preferred_element_type=F32)
            m = m_new
        o_ref[...] = o * (1.0 / l)
        lse_ref[...] = m * _EXP2_SCALE + jnp.log2(l)
        pl.when(step == steps - 1)(g.finish)

    kv = pl.BlockSpec((L, HEAD_DIM), lambda h, i: (0, h // _REP))
    return pl.pallas_call(
        body, name="attn_fwd",
        out_shape=[jax.ShapeDtypeStruct((L, D_ATTN), F32), jax.ShapeDtypeStruct((N_HEADS, L, 1), F32)] +
                  [jax.ShapeDtypeStruct(_BIG[t][1], BF16) for t in g_ts],
        grid=grid,
        in_specs=[pl.BlockSpec((tq, HEAD_DIM), lambda h, i: (i, h)), kv, kv] + [_ANY] * n_g,
        out_specs=[pl.BlockSpec((tq, HEAD_DIM), lambda h, i: (i, h)),
                   pl.BlockSpec((None, tq, 1), lambda h, i: (h, i, 0))] + [_ANY] * n_g,
        scratch_shapes=_Gather.scratch(g_shards),
        compiler_params=_params(("arbitrary", "arbitrary")),
    )(qr, kr, vb, *g_shards)


def _attn_bwd(qr, kr, k_t, vb, do, lse, ex_ts, ex_sums, *, tq=512):
    L = qr.shape[0]
    tq = min(tq, L)
    scale = HEAD_DIM ** -0.5
    kc = L // ATTN_BWD_KEY_CHUNKS
    n_ex = len(ex_ts)
    grid = (N_KV, _REP, L // tq)

    def body(q_ref, k_ref, kt_ref, v_ref, do_ref, lse_ref, *rest):
        dq_ref, dk_ref, dv_ref = rest[n_ex:n_ex + 3]
        ex = _ChipExchange(ex_ts, rest[:n_ex], rest[n_ex + 3:2 * n_ex + 3], rest[2 * n_ex + 3:])
        step = (pl.program_id(0) * grid[1] + pl.program_id(1)) * grid[2] + pl.program_id(2)
        pl.when(step == 0)(ex.start)

        @pl.when((pl.program_id(1) == 0) & (pl.program_id(2) == 0))
        def _():
            dk_ref[...] = jnp.zeros_like(dk_ref)
            dv_ref[...] = jnp.zeros_like(dv_ref)

        q, do, lse = q_ref[...], do_ref[...], lse_ref[...]
        keys = [slice(c * kc, (c + 1) * kc) for c in range(ATTN_BWD_KEY_CHUNKS)]
        ps, dps = [], []
        for ks in keys:
            st = lax.dot_general(k_ref[ks, :], q, _QK_T, preferred_element_type=F32)
            p = jnp.exp2(st * _EXP2_SCALE - lse)
            dv_ref[ks, :] += jnp.dot(p.astype(BF16), do, preferred_element_type=F32)
            ps.append(p)
            dps.append(lax.dot_general(v_ref[ks, :], do, _QK_T, preferred_element_type=F32))
        delta = sum(jnp.sum(p * dp, axis=0, keepdims=True) for p, dp in zip(ps, dps))
        dq_t = 0.0
        for ks, p, dp in zip(keys, ps, dps):
            ds = (p * (dp - delta) * scale).astype(BF16)
            dk_ref[ks, :] += jnp.dot(ds, q, preferred_element_type=F32)
            dq_t = dq_t + jnp.dot(kt_ref[:, ks], ds, preferred_element_type=F32)
        dq_ref[...] = dq_t.T
        pl.when(step == grid[0] * grid[1] * grid[2] - 1)(ex.finish)

    head = lambda g, r, i: (i, g * _REP + r)
    kv = pl.BlockSpec((L, HEAD_DIM), lambda g, r, i: (0, g))
    return pl.pallas_call(
        body, name="attn_bwd",
        out_shape=[jax.ShapeDtypeStruct((L, D_ATTN), F32), jax.ShapeDtypeStruct((L, D_KV), F32), jax.ShapeDtypeStruct((L, D_KV), F32)] +
                  _ChipExchange.out_shape(ex_ts),
        grid=grid,
        in_specs=[pl.BlockSpec((tq, HEAD_DIM), head), kv,
                  pl.BlockSpec((HEAD_DIM, L), lambda g, r, i: (g, 0)), kv,
                  pl.BlockSpec((tq, HEAD_DIM), head),
                  pl.BlockSpec((None, 1, tq), lambda g, r, i: (g * _REP + r, 0, i))] + [_ANY] * n_ex,
        out_specs=[pl.BlockSpec((tq, HEAD_DIM), head), kv, kv] + [_ANY] * n_ex,
        scratch_shapes=_ChipExchange.scratch(ex_ts),
        compiler_params=_params(("arbitrary", "arbitrary", "arbitrary")),
    )(qr, kr, k_t, vb, do, lse, *ex_sums)


SSM_BLK = 8
SSM_NB = SSM_G // SSM_BLK
SSM_SEG = 8
SSM_UNROLL = 4


def _unrolled_loop(n, step, carry):
    u = SSM_UNROLL

    def trip(i, c):
        for j in range(u):
            c = step(i * u + j, c)
        return c
    carry = lax.fori_loop(0, n // u, trip, carry)
    for t in range(n - n % u, n):
        carry = step(jnp.int32(t), carry)
    return carry


def _cplx_pow2(a, b, n):
    for _ in range(int(math.log2(n))):
        a, b = a * a - b * b, 2.0 * a * b
    return a, b


def _seg_scan(ref, a, b, T, reverse, entry=None):
    npair = len(a)
    zero = jnp.zeros((SSM_SEG, 128), F32)

    def make_step(store):
        def step(t, carry):
            lt = (T - 1 - t) if reverse else t
            row = pl.multiple_of(lt * SSM_SEG, SSM_SEG)
            blk = ref[pl.ds(row, SSM_SEG), :]
            new = []
            for q in range(npair):
                re, im = carry[2 * q], carry[2 * q + 1]
                nre = a[q] * re - b[q] * im + blk[:, q * 256:q * 256 + 128]
                nim = a[q] * im + b[q] * re + blk[:, q * 256 + 128:q * 256 + 256]
                new += [nre, nim]
            if store:
                ref[pl.ds(row, SSM_SEG), :] = jnp.concatenate(new, axis=1)
            return tuple(new)
        return step

    if entry is not None:
        _unrolled_loop(T, make_step(True), tuple(entry))
        return entry
    ends = _unrolled_loop(T, make_step(False), (zero,) * (2 * npair))
    sub = lax.broadcasted_iota(jnp.int32, (SSM_SEG, 128), 0)
    keep = (sub != SSM_SEG - 1) if reverse else (sub != 0)
    shift = (SSM_SEG - 1) if reverse else 1
    init = []
    for q in range(npair):
        pa, pb = _cplx_pow2(a[q], b[q], T)
        xr, xi = zero, zero
        for _ in range(SSM_SEG - 1):
            fr = ends[2 * q] + pa * xr - pb * xi
            fi = ends[2 * q + 1] + pa * xi + pb * xr
            xr = jnp.where(keep, pltpu.roll(fr, shift, 0), 0.0)
            xi = jnp.where(keep, pltpu.roll(fi, shift, 0), 0.0)
        init += [xr, xi]
    _unrolled_loop(T, make_step(True), tuple(init))
    return init


SSM_BW = SSM_BLK * SSM_H
SSM_SW = SSM_BLK * 2 * SSM_P
SSM_NPAIR = SSM_BLK // 2


def _seg_perm(a):
    L, C = a.shape
    return a.reshape(SSM_SEG, L // SSM_SEG, C).transpose(1, 0, 2).reshape(L, C)


def _seg_unperm(a):
    L, C = a.shape
    return a.reshape(L // SSM_SEG, SSM_SEG, C).transpose(1, 0, 2).reshape(L, C)


def _lam_rows(are_ref, aim_ref, d):
    a = [jnp.broadcast_to(are_ref[d, j:j + 1, :], (SSM_SEG, 128)) for j in range(SSM_NPAIR)]
    b = [jnp.broadcast_to(aim_ref[d, j:j + 1, :], (SSM_SEG, 128)) for j in range(SSM_NPAIR)]
    return a, b


_PAIR_SPEC = pl.BlockSpec((None, 2, SSM_NPAIR, 2, 2 * SSM_H, 128), lambda g: (g, 0, 0, 0, 0, 0))


def _pair_window(j, r):
    return slice(j * 2 * SSM_H, (j + 1) * 2 * SSM_H), slice(j * 256 + r * 128, j * 256 + (r + 1) * 128)


def _expand_pairs(c_ref, dense_ref):
    dense_ref[...] = jnp.zeros_like(dense_ref)
    for d in range(2):
        for j in range(SSM_NPAIR):
            for r in range(2):
                rows, cols = _pair_window(j, r)
                dense_ref[d, rows, cols] = c_ref[d, j, r].astype(dense_ref.dtype)


def _ssm_fwd(u_p, wb, wc, are, aim, dvec):
    L = u_p.shape[0]
    T = L // SSM_SEG
    RC = min(512, L)

    def body(u_ref, wb_ref, wc_ref, are_ref, aim_ref, d_ref, y_ref, entry_ref, x_scr, wb_s, wc_s):
        _expand_pairs(wb_ref, wb_s)
        _expand_pairs(wc_ref, wc_s)
        y_ref[...] = u_ref[...] * d_ref[...]
        for d in range(2):
            def bu_chunk(c, _):
                rows = pl.ds(pl.multiple_of(c * RC, RC), RC)
                x_scr[rows, :] = jnp.dot(u_ref[rows, :].astype(BF16), wb_s[d], preferred_element_type=F32)
                return 0
            lax.fori_loop(0, L // RC, bu_chunk, 0)
            a, b = _lam_rows(are_ref, aim_ref, d)
            entry_ref[d] = jnp.concatenate(_seg_scan(x_scr, a, b, T, reverse=(d == 1)), axis=1)

            def y_chunk(c, _):
                rows = pl.ds(pl.multiple_of(c * RC, RC), RC)
                y_ref[rows, :] += lax.dot_general(x_scr[rows, :].astype(BF16), wc_s[d], _QK_T, preferred_element_type=F32)
                return 0
            lax.fori_loop(0, L // RC, y_chunk, 0)

    blk4 = lambda g: (g, 0, 0, 0)
    return pl.pallas_call(
        body, name="ssm_fwd",
        out_shape=[jax.ShapeDtypeStruct((L, D_SSM), F32), jax.ShapeDtypeStruct((SSM_NB, 2, SSM_SEG, SSM_SW), F32)],
        grid=(SSM_NB,),
        in_specs=[pl.BlockSpec((L, SSM_BW), lambda g: (0, g)), _PAIR_SPEC, _PAIR_SPEC,
                  pl.BlockSpec((None, 2, SSM_NPAIR, 128), blk4),
                  pl.BlockSpec((None, 2, SSM_NPAIR, 128), blk4),
                  pl.BlockSpec((1, SSM_BW), lambda g: (0, g))],
        out_specs=[pl.BlockSpec((L, SSM_BW), lambda g: (0, g)), pl.BlockSpec((None, 2, SSM_SEG, SSM_SW), blk4)],
        scratch_shapes=[pltpu.VMEM((L, SSM_SW), F32), pltpu.VMEM((2, SSM_BW, SSM_SW), BF16), pltpu.VMEM((2, SSM_BW, SSM_SW), BF16)],
        compiler_params=_params(("parallel",)),
    )(u_p, wb, wc, are, aim, dvec)


def _ssm_bwd(u_p, dy_p, entry, wb, wc, are, aim, dvec):
    L = u_p.shape[0]
    T = L // SSM_SEG
    RC = min(512, L)

    def lam_acc(acc, sb, xb):
        new = []
        for q in range(SSM_NPAIR):
            sr, si = sb[:, q * 256:q * 256 + 128], sb[:, q * 256 + 128:q * 256 + 256]
            xr, xi = xb[:, q * 256:q * 256 + 128], xb[:, q * 256 + 128:q * 256 + 256]
            new += [acc[2 * q] + sr * xr + si * xi, acc[2 * q + 1] + si * xr - sr * xi]
        return tuple(new)

    def body(u_ref, dy_ref, entry_ref, wb_ref, wc_ref, are_ref, aim_ref, d_ref,
             du_ref, dwb_ref, dwc_ref, dare_ref, daim_ref, dd_ref, x_scr, s_scr, wb_s, wc_s, dwb_s, dwc_s):
        _expand_pairs(wb_ref, wb_s)
        _expand_pairs(wc_ref, wc_s)
        du_ref[...] = dy_ref[...] * d_ref[...]
        dd_ref[...] = _colsum(dy_ref[...] * u_ref[...])
        dwb_s[...] = jnp.zeros_like(dwb_s)
        dwc_s[...] = jnp.zeros_like(dwc_s)
        for d in range(2):
            rev = d == 1

            def in_chunk(c, _):
                rows = pl.ds(pl.multiple_of(c * RC, RC), RC)
                x_scr[rows, :] = jnp.dot(u_ref[rows, :].astype(BF16), wb_s[d], preferred_element_type=F32)
                s_scr[rows, :] = jnp.dot(dy_ref[rows, :].astype(BF16), wc_s[d], preferred_element_type=F32)
                return 0
            lax.fori_loop(0, L // RC, in_chunk, 0)
            a, b = _lam_rows(are_ref, aim_ref, d)
            x_in = _seg_scan(x_scr, a, b, T, reverse=rev, entry=[entry_ref[d, :, q * 128:(q + 1) * 128] for q in range(2 * SSM_NPAIR)])
            _seg_scan(s_scr, a, [-v for v in b], T, reverse=not rev)

            def lam_step(t, acc):
                lt = (T - 2 - t) if rev else (t + 1)
                srow = pl.multiple_of(lt * SSM_SEG, SSM_SEG)
                xrow = pl.multiple_of((lt + 1 if rev else lt - 1) * SSM_SEG, SSM_SEG)
                return lam_acc(acc, s_scr[pl.ds(srow, SSM_SEG), :], x_scr[pl.ds(xrow, SSM_SEG), :])

            edge = pl.ds(((T - 1) if rev else 0) * SSM_SEG, SSM_SEG)
            acc0 = lam_acc((jnp.zeros((SSM_SEG, 128), F32),) * (2 * SSM_NPAIR), s_scr[edge, :], jnp.concatenate(x_in, axis=1))
            acc = _unrolled_loop(T - 1, lam_step, acc0)
            for q in range(SSM_NPAIR):
                dare_ref[d, q:q + 1, :] = _colsum(acc[2 * q])
                daim_ref[d, q:q + 1, :] = _colsum(acc[2 * q + 1])

            def out_chunk(c, _):
                rows = pl.ds(pl.multiple_of(c * RC, RC), RC)
                xs, ss = x_scr[rows, :].astype(BF16), s_scr[rows, :].astype(BF16)
                uu, dd = u_ref[rows, :].astype(BF16), dy_ref[rows, :].astype(BF16)
                dwc_s[d] += lax.dot_general(dd, xs, _TA, preferred_element_type=F32)
                dwb_s[d] += lax.dot_general(uu, ss, _TA, preferred_element_type=F32)
                du_ref[rows, :] += lax.dot_general(ss, wb_s[d], _QK_T, preferred_element_type=F32)
                return 0
            lax.fori_loop(0, L // RC, out_chunk, 0)
            for j in range(SSM_NPAIR):
                for r in range(2):
                    rows, cols = _pair_window(j, r)
                    dwb_ref[d, j, r] = dwb_s[d, rows, cols]
                    dwc_ref[d, j, r] = dwc_s[d, rows, cols]

    blk4 = lambda g: (g, 0, 0, 0)
    chan = pl.BlockSpec((L, SSM_BW), lambda g: (0, g))
    par_specs = [_PAIR_SPEC, _PAIR_SPEC,
                 pl.BlockSpec((None, 2, SSM_NPAIR, 128), blk4),
                 pl.BlockSpec((None, 2, SSM_NPAIR, 128), blk4),
                 pl.BlockSpec((1, SSM_BW), lambda g: (0, g))]
    dense = lambda dt: pltpu.VMEM((2, SSM_BW, SSM_SW), dt)
    return pl.pallas_call(
        body, name="ssm_bwd",
        out_shape=[jax.ShapeDtypeStruct((L, D_SSM), F32),
                   jax.ShapeDtypeStruct(wb.shape, F32),
                   jax.ShapeDtypeStruct(wc.shape, F32),
                   jax.ShapeDtypeStruct((SSM_NB, 2, SSM_NPAIR, 128), F32),
                   jax.ShapeDtypeStruct((SSM_NB, 2, SSM_NPAIR, 128), F32),
                   jax.ShapeDtypeStruct((1, D_SSM), F32)],
        grid=(SSM_NB,),
        in_specs=[chan, chan, pl.BlockSpec((None, 2, SSM_SEG, SSM_SW), blk4)] + par_specs,
        out_specs=[chan] + par_specs,
        scratch_shapes=[pltpu.VMEM((L, SSM_SW), F32), pltpu.VMEM((L, SSM_SW), F32), dense(BF16), dense(BF16), dense(F32), dense(F32)],
        compiler_params=_params(("parallel",)),
    )(u_p, dy_p, entry, wb, wc, are, aim, dvec)


def _ssm_disc(a_re, a_im, log_dt, b_re, b_im):
    lam = lax.complex(jnp.minimum(a_re, -1e-4), a_im)
    dt = jnp.exp(log_dt)[..., None]
    lam_bar = jnp.exp(lam * dt)
    b_bar = ((lam_bar - 1.0) / lam)[..., None] * lax.complex(b_re, b_im)
    return jnp.real(lam_bar), jnp.imag(lam_bar), jnp.real(b_bar), jnp.imag(b_bar)


_EYE2 = np.eye(2, dtype=np.float32)[:, None, :, None]


def _to_pairs(t):
    t = t.reshape(2, SSM_NB, SSM_NPAIR, 2, 2, SSM_H, SSM_P).transpose(1, 0, 2, 4, 3, 5, 6)
    return (t[..., None, :] * _EYE2).reshape(SSM_NB, 2, SSM_NPAIR, 2, 2 * SSM_H, 2 * SSM_P)


def _from_pairs(c):
    t = jnp.sum(c.reshape(SSM_NB, 2, SSM_NPAIR, 2, 2, SSM_H, 2, SSM_P) * _EYE2, axis=6)
    return t.transpose(1, 0, 2, 4, 3, 5, 6).reshape(2, SSM_G, 2, SSM_H, SSM_P)


def _to_lam(v):
    return v.reshape(2, SSM_NB, SSM_NPAIR, 128).transpose(1, 0, 2, 3)


def _from_lam(v):
    return v.transpose(1, 0, 2, 3).reshape(2, SSM_G, SSM_P)


_MESH = pl.DeviceIdType.MESH
_ANY = pl.BlockSpec(memory_space=pl.ANY)
_BIG = (("w_in", (D_MODEL, D_IN), 1, D_IN // N_CHIPS),
        ("w_glu", (D_SSM, 2 * D_SSM), 1, 2 * D_SSM // N_CHIPS),
        ("w_out", (D_ATTN + D_SSM, D_MODEL), 0, (D_ATTN + D_SSM) // N_CHIPS),
        ("w_ple_gate", (D_MODEL, D_MODEL), 0, D_MODEL // N_CHIPS),
        ("w_ple_proj", (PLE_DIM, D_MODEL), 1, D_MODEL // N_CHIPS))


def _place():
    x, y, c = lax.axis_index("x"), lax.axis_index("y"), lax.axis_index("c")
    return x, y, c, [(1 - x, y), (x, 1 - y), (1 - x, 1 - y)]


class _Gather:
    def __init__(self, ts, srcs, dsts, stage, sems):
        self.ts, self.srcs, self.dsts, self.stage = ts, srcs, dsts, stage
        self.send_sems, self.recv_sems, self.fwd_send_sems, self.fwd_recv_sems, self.loc_sems = sems
        self.x, self.y, self.c, self.chips = _place()
        self.n = len(ts)

    @staticmethod
    def scratch(shards):
        sems = pltpu.SemaphoreType.DMA((3, len(shards)))
        return [pltpu.VMEM(s.shape, BF16) for s in shards] + [sems, sems, sems, sems, pltpu.SemaphoreType.DMA((len(shards),))]

    def _shard_of(self, i, kk):
        _, _, axis, sz = _BIG[self.ts[i]]
        sl = pl.ds(pl.multiple_of(kk * sz, sz), sz)
        return self.dsts[i].at[:, sl] if axis == 1 else self.dsts[i].at[sl, :]

    @staticmethod
    def _half_of(ref, cc):
        n = ref.shape[0] // 2
        return ref.at[pl.ds(pl.multiple_of(cc * n, n), n), :]

    def _ici(self, j, i, kk):
        px, py = self.chips[j]
        return pltpu.make_async_remote_copy(
            src_ref=self._half_of(self.srcs[i], self.c), dst_ref=self._half_of(self._shard_of(i, kk), self.c),
            send_sem=self.send_sems.at[j, i], recv_sem=self.recv_sems.at[j, i],
            device_id=(px, py, self.c), device_id_type=_MESH)

    def _forward(self, j, i, kk, cc):
        part = self._half_of(self._shard_of(i, kk), cc)
        return pltpu.make_async_remote_copy(
            src_ref=part, dst_ref=part, send_sem=self.fwd_send_sems.at[j, i], recv_sem=self.fwd_recv_sems.at[j, i],
            device_id=(self.x, self.y, 1 - self.c), device_id_type=_MESH)

    def _load(self, i):
        return pltpu.make_async_copy(self.srcs[i], self.stage[i], self.loc_sems.at[i])

    def _place_own(self, i):
        return pltpu.make_async_copy(self.stage[i], self._shard_of(i, 2 * self.x + self.y), self.loc_sems.at[i])

    def _peers(self):
        return [(i, j, 2 * px + py) for i in range(self.n) for j, (px, py) in enumerate(self.chips)]

    def start(self):
        for i in range(self.n):
            self._load(i).start()
        for i, j, _ in self._peers():
            self._ici(j, i, 2 * self.x + self.y).start()

    def forward(self):
        for i in range(self.n):
            self._load(i).wait()
            self._place_own(i).start()
        for i, j, kk in self._peers():
            self._ici(j, i, kk).wait_recv()
            self._forward(j, i, kk, self.c).start()

    def finish(self):
        for i, j, kk in self._peers():
            self._forward(j, i, kk, 1 - self.c).wait_recv()
        for i, j, kk in self._peers():
            self._ici(j, i, kk).wait_send()
            self._forward(j, i, kk, self.c).wait_send()
        for i in range(self.n):
            self._place_own(i).wait()


def _gather_weights(ts, shards):
    n = len(ts)

    def body(*refs):
        g = _Gather(ts, refs[:n], refs[n:2 * n], refs[2 * n:3 * n], refs[3 * n:])
        g.start()
        g.forward()
        g.finish()

    return pl.pallas_call(
        body, name="gather_weights",
        out_shape=[jax.ShapeDtypeStruct(_BIG[t][1], BF16) for t in ts],
        in_specs=[_ANY] * n, out_specs=[_ANY] * n,
        scratch_shapes=_Gather.scratch(shards),
        compiler_params=pltpu.CompilerParams(vmem_limit_bytes=VMEM_LIMIT_V7X),
    )(*shards)


SMALL_W = 1024
SMALL_ROWS = 72
N_SMALL = 8 * SMALL_ROWS * SMALL_W
_RED = tuple((shape, ax, (shape[0] // 2, sz) if ax == 1 else (sz // 2, shape[1]), BF16) for _, shape, ax, sz in _BIG) + \
    (((8 * SMALL_ROWS, SMALL_W), 0, (SMALL_ROWS, SMALL_W), F32),)
_RED_TR = 128


def _piece(ref, t, kk, cc):
    _, ax, (pr, pc), _ = _RED[t]
    if ax == 1:
        return ref.at[pl.ds(pl.multiple_of(cc * pr, pr), pr), pl.ds(pl.multiple_of(kk * pc, pc), pc)]
    return ref.at[pl.ds(pl.multiple_of((2 * kk + cc) * pr, pr), pr), :]


def _half_shape(t):
    shape, ax, (pr, pc), _ = _RED[t]
    return (pr, shape[1]) if ax == 1 else (N_CHIPS * pr, pc)


def _piece_in_half(ref, t, kk):
    _, ax, (pr, pc), _ = _RED[t]
    return ref.at[:, pl.ds(pl.multiple_of(kk * pc, pc), pc)] if ax == 1 else ref.at[pl.ds(pl.multiple_of(kk * pr, pr), pr), :]


def _grad_sibling_exchange(ts, grads, name):
    n = len(ts)
    n_dma = sum(1 if _RED[t][1] == 1 else N_CHIPS for t in ts)

    def body(*refs):
        srcs, dsts, (send_sems, recv_sems) = refs[:n], refs[n:2 * n], refs[2 * n:]
        x, y, c, _ = _place()
        pairs = []
        for i, t in enumerate(ts):
            _, ax, (pr, _), _ = _RED[t]
            if ax == 1:
                pairs.append((srcs[i].at[pl.ds(pl.multiple_of((1 - c) * pr, pr), pr), :], dsts[i]))
            else:
                pairs += [(_piece(srcs[i], t, kk, 1 - c), _piece_in_half(dsts[i], t, kk)) for kk in range(N_CHIPS)]
        cps = [pltpu.make_async_remote_copy(src_ref=s, dst_ref=d, send_sem=send_sems.at[i], recv_sem=recv_sems.at[i],
                                            device_id=(x, y, 1 - c), device_id_type=_MESH) for i, (s, d) in enumerate(pairs)]
        for cp in cps:
            cp.start()
        for cp in cps:
            cp.wait()

    return pl.pallas_call(
        body, name=name,
        out_shape=[jax.ShapeDtypeStruct(_half_shape(t), F32) for t in ts],
        in_specs=[_ANY] * n, out_specs=[_ANY] * n,
        scratch_shapes=[pltpu.SemaphoreType.DMA((n_dma,)), pltpu.SemaphoreType.DMA((n_dma,))],
    )(*grads)


def _chip_sum(t, g, rs, place):
    shape, ax, (pr, pc), dt = _RED[t]
    W = shape[1]
    tr = min(pr, _RED_TR)
    nb = pr // tr

    def body(place_ref, g_ref, rs_ref, o_ref):
        o_ref[...] = (g_ref[...] + rs_ref[...]).astype(o_ref.dtype)

    return pl.pallas_call(
        body, name="grad_chip_sum_%d" % t,
        out_shape=jax.ShapeDtypeStruct(rs.shape, dt),
        grid_spec=pltpu.PrefetchScalarGridSpec(
            num_scalar_prefetch=1, grid=(1 if ax == 1 else N_CHIPS, nb),
            in_specs=[pl.BlockSpec((tr, W), lambda kk, i, pr_: ((2 * kk + pr_[0]) * nb + i, 0)),
                      pl.BlockSpec((tr, W), lambda kk, i, pr_: (kk * nb + i, 0))],
            out_specs=pl.BlockSpec((tr, W), lambda kk, i, pr_: (kk * nb + i, 0))),
        compiler_params=_params(("parallel", "parallel")),
    )(place, g, rs)


class _ChipExchange:
    def __init__(self, ts, srcs, dsts, sems):
        self.send_sems, self.recv_sems = sems
        x, y, c, chips = _place()
        self.copies = lambda: [
            pltpu.make_async_remote_copy(src_ref=_piece_in_half(srcs[i], t, 2 * px + py), dst_ref=dsts[i].at[j],
                                         send_sem=self.send_sems.at[j, i], recv_sem=self.recv_sems.at[j, i],
                                         device_id=(px, py, c), device_id_type=_MESH)
            for i, t in enumerate(ts) for j, (px, py) in enumerate(chips)]

    @staticmethod
    def scratch(ts):
        return [pltpu.SemaphoreType.DMA((3, len(ts))), pltpu.SemaphoreType.DMA((3, len(ts)))]

    @staticmethod
    def out_shape(ts):
        return [jax.ShapeDtypeStruct((3,) + _RED[t][2], _RED[t][3]) for t in ts]

    def start(self):
        for cp in self.copies():
            cp.start()

    def finish(self):
        for cp in self.copies():
            cp.wait()


def _grad_chip_exchange(ts, sums):
    n = len(ts)

    def body(*refs):
        ex = _ChipExchange(ts, refs[:n], refs[n:2 * n], refs[2 * n:])
        ex.start()
        ex.finish()

    return pl.pallas_call(
        body, name="grad_chip_exchange",
        out_shape=_ChipExchange.out_shape(ts),
        in_specs=[_ANY] * n, out_specs=[_ANY] * n,
        scratch_shapes=_ChipExchange.scratch(ts),
    )(*sums)


def _total_sum(t, g, rs, rc, place):
    shape, ax, (pr, pc), _ = _RED[t]
    tr = min(pr, _RED_TR)
    nb = pr // tr
    small = t == len(_RED) - 1

    def body(place_ref, g_ref, rs_ref, rc_ref, o_ref):
        o_ref[...] = (g_ref[...] + rs_ref[...]) + rc_ref[0].astype(F32) + rc_ref[1].astype(F32) + rc_ref[2].astype(F32)

    if ax == 1:
        g_map = lambda i, pr_: (pr_[0] * nb + i, pr_[1])
        rs_map = lambda i, pr_: (i, pr_[1])
    else:
        g_map = lambda i, pr_: ((2 * pr_[1] + pr_[0]) * nb + i, 0)
        rs_map = lambda i, pr_: (pr_[1] * nb + i, 0)
    o_map = (lambda i, pr_: ((2 * pr_[1] + pr_[0]) * nb + i, 0)) if small else (lambda i, pr_: (pr_[0] * nb + i, 0))
    return pl.pallas_call(
        body, name="grad_total_sum_%d" % t,
        out_shape=jax.ShapeDtypeStruct(((8 if small else 2) * pr, pc), F32),
        grid_spec=pltpu.PrefetchScalarGridSpec(
            num_scalar_prefetch=1, grid=(nb,),
            in_specs=[pl.BlockSpec((tr, pc), g_map), pl.BlockSpec((tr, pc), rs_map),
                      pl.BlockSpec((3, tr, pc), lambda i, pr_: (0, i, 0))],
            out_specs=pl.BlockSpec((tr, pc), o_map)),
        compiler_params=_params(("parallel",)),
    )(place, g, rs, rc)


def _grad_final_exchange(totals):
    n = len(_RED)
    nb = n - 1

    def body(*refs):
        srcs, dsts, (send_sems, recv_sems) = refs[:n], refs[n:2 * n], refs[2 * n:]
        x, y, c, chips = _place()
        me = 4 * x + 2 * y + c
        others = [(x, y, 1 - c)] + [(px, py, cc) for (px, py) in chips for cc in (c, 1 - c)]

        def half(ref, t, cc):
            pr = _RED[t][2][0]
            return ref.at[pl.ds(pl.multiple_of(cc * pr, pr), pr), :]

        def eighth(ref, dev):
            return ref.at[pl.ds(pl.multiple_of(dev * SMALL_ROWS, SMALL_ROWS), SMALL_ROWS), :]

        def big_copy(t, cc):
            return pltpu.make_async_remote_copy(src_ref=half(srcs[t], t, cc), dst_ref=half(dsts[t], t, cc), send_sem=send_sems.at[t],
                                                recv_sem=recv_sems.at[t], device_id=others[0], device_id_type=_MESH)

        def small_copy(i, dev):
            return pltpu.make_async_remote_copy(src_ref=eighth(srcs[nb], dev), dst_ref=eighth(dsts[nb], dev),
                                                send_sem=send_sems.at[nb + i], recv_sem=recv_sems.at[nb + i],
                                                device_id=others[i], device_id_type=_MESH)

        sends = [big_copy(t, c) for t in range(nb)] + [small_copy(i, me) for i in range(7)]
        for cp in sends:
            cp.start()
        for t in range(nb):
            big_copy(t, 1 - c).wait_recv()
        for i, (px, py, pc) in enumerate(others):
            small_copy(i, 4 * px + 2 * py + pc).wait_recv()
        for cp in sends:
            cp.wait_send()

    return pl.pallas_call(
        body, name="grad_final_exchange",
        out_shape=[jax.ShapeDtypeStruct(a.shape, F32) for a in totals],
        in_specs=[_ANY] * n, out_specs=[_ANY] * n,
        input_output_aliases={t: t for t in range(n)},
        scratch_shapes=[pltpu.SemaphoreType.DMA((nb + 7,)), pltpu.SemaphoreType.DMA((nb + 7,))],
    )(*totals)


def _grad_place():
    return jnp.stack([lax.axis_index("c"), 2 * lax.axis_index("x") + lax.axis_index("y")]).astype(jnp.int32)


def _reduce_begin(ts, grads, place, tag):
    from_sibling = _grad_sibling_exchange(ts, grads, "grad_sibling_exchange_" + tag)
    return from_sibling, [_chip_sum(t, g, r, place) for t, g, r in zip(ts, grads, from_sibling)]


def _reduce_end(ts, grads, from_sibling, from_chips, place):
    return [_total_sum(t, g, r, q, place) for t, g, r, q in zip(ts, grads, from_sibling, from_chips)]


_EARLY = (1, 2, 3, 4)
_W_IN = (0,)
_SMALL_RED = (5,)


def _adamw_math(w, g, m, v):
    m = ADAM_B1 * m + (1.0 - ADAM_B1) * g
    v = ADAM_B2 * v + (1.0 - ADAM_B2) * (g * g)
    m_hat = m / (1.0 - ADAM_B1 ** ADAM_STEP)
    v_hat = v / (1.0 - ADAM_B2 ** ADAM_STEP)
    return -ADAM_LR * (m_hat / (jnp.sqrt(v_hat) + ADAM_EPS) + ADAM_WD * w), m, v


def _adamw(w, g, m, v, name):
    W = w.shape[1]
    return _rowwise(_adamw_math, [(a, 0, W) for a in (w, g, m, v)], [], [(W, F32)] * 3, tr=128, name=name)


def _adamw_whole(ws, gs, ms, vs, name):
    n = len(ws)

    def body(*refs):
        ins, outs = refs[:4 * n], refs[4 * n:]
        for i in range(n):
            res = _adamw_math(*[ins[j * n + i][...] for j in range(4)])
            for j in range(3):
                outs[j * n + i][...] = res[j]

    res = pl.pallas_call(
        body, name=name,
        out_shape=[jax.ShapeDtypeStruct(a.shape, F32) for a in ws] * 3,
        compiler_params=pltpu.CompilerParams(vmem_limit_bytes=VMEM_LIMIT_V7X),
    )(*ws, *gs, *ms, *vs)
    return res[:n], res[n:2 * n], res[2 * n:]


def _chunks(arr, off, width, w=512):
    return [(arr, off + i * w, w) for i in range(width // w)]


def _cat(vs):
    return jnp.concatenate(vs, axis=1)


def _forward_backward(x, p_b, tgt, shards, small):
    L = x.shape[0]
    w_in, = _gather_weights([0], shards[:1])
    row = lambda v: v.reshape(1, -1)
    g_mix, g_ple, g_fin = row(small["norm_mix"]), row(small["norm_ple"]), row(small["norm_final"])
    gq, gk, b_glu = row(small["q_norm"]), row(small["k_norm"]), row(small["b_glu"])
    cos, sin = _rope_tables(L)

    hn_b, = _rowwise(lambda x, g: x * _rms(x) * g, [(x, 0, D_MODEL)], [g_mix], [(D_MODEL, BF16)], name="norm_mix")
    z = _matmul(hn_b, w_in, name="mm_in")
    qr, kr, vb = _attn_prep(z, gq, gk, cos, sin)
    o, lse, w_glu, w_out, w_pg, w_pp = _attn_fwd(qr, kr, vb, [1, 2, 3, 4], shards[1:])

    ssm_names = ("ssm_a_re", "ssm_a_im", "ssm_log_dt", "ssm_b_re", "ssm_b_im")
    (lre, lim, bre, bim), disc_vjp = jax.vjp(_ssm_disc, *[small[n][0] for n in ssm_names])
    ssm = (_to_pairs(jnp.swapaxes(jnp.stack([bre, bim], axis=2), -1, -2)),
           _to_pairs(jnp.stack([small["ssm_c_re"][0], -small["ssm_c_im"][0]], axis=2)),
           _to_lam(lre), _to_lam(lim), row(small["ssm_d"]))
    u_p = _seg_perm(z[:, Z_U:Z_U + D_SSM])
    y_p, ssm_entry = _ssm_fwd(u_p, *ssm)
    y_s = _seg_unperm(y_p)
    ge_b, = _rowwise(_gelu, [(y_s, 0, D_SSM)], [], [(D_SSM, BF16)], name="gelu")
    glu = _matmul(ge_b, w_glu, name="mm_glu")

    def merge(ga0, ga1, a, b, gs0, gs1, o, bias):
        sa, _ = _silu_and_grad(_cat([ga0, ga1]))
        ss, _ = _silu_and_grad(_cat([gs0, gs1]))
        y2 = (a + bias[:, :D_SSM]) * _sig(b + bias[:, D_SSM:])
        return _cat([o * sa, y2 * ss])
    merge_rows = _chunks(z, Z_GA, D_ATTN) + [(glu, 0, D_SSM), (glu, D_SSM, D_SSM)] + _chunks(z, Z_GS, D_SSM) + [(o, 0, D_ATTN)]
    cat_b, = _rowwise(merge, merge_rows, [b_glu], [(D_MODEL, BF16)], name="merge")
    t_out = _matmul(cat_b, w_out, name="mm_out")

    def resid(x, t, g):
        h1 = x + t
        return h1, h1 * _rms(h1) * g
    h1, hp_b = _rowwise(resid, [(x, 0, D_MODEL), (t_out, 0, D_MODEL)], [g_ple], [(D_MODEL, F32), (D_MODEL, BF16)], name="resid_norm")
    gl = _matmul(hp_b, w_pg, name="mm_ple_gate")
    pp = _matmul(p_b, w_pp, name="mm_ple_proj")

    def head(h1, gl, pp, tgt, g):
        gate = _sig(gl)
        h2 = h1 + gate * pp
        r = _rms(h2)
        n = h2 * r
        err = n * g - tgt
        dy = err * (1.0 / D_MODEL)
        dn = dy * g
        dh2 = r * (dn - n * jnp.mean(dn * n, axis=-1, keepdims=True))
        dgate = dh2 * pp
        return dh2, dh2 * gate, dgate * gate * (1.0 - gate), _colsum(dy * n), _colsum(0.5 * err * err * (1.0 / D_MODEL))
    dh2, dpp_b, dgl_b, dg_fin, loss_cols = _rowwise(
        head, [(a, 0, D_MODEL) for a in (h1, gl, pp, tgt)], [g_fin],
        [(D_MODEL, F32), (D_MODEL, BF16), (D_MODEL, BF16)], [(1, D_MODEL), (1, D_MODEL)], name="loss_head")

    dw_pp = _matmul(p_b, dpp_b, ta=True, name="mm_d_w_ple_proj")
    dw_pg = _matmul(hp_b, dgl_b, ta=True, name="mm_d_w_ple_gate")
    dhp = _matmul(dgl_b, w_pg, tb=True, name="mm_d_hp")

    def resid_bwd(dhp, h1, dh2, g):
        dx, dg = _rms_bwd(h1, g, dhp)
        dh1 = dh2 + dx
        return dh1, dh1, _colsum(dg)
    dh1, dh1_b, dg_ple = _rowwise(resid_bwd, [(a, 0, D_MODEL) for a in (dhp, h1, dh2)], [g_ple],
                                  [(D_MODEL, F32), (D_MODEL, BF16)], [(1, D_MODEL)], name="resid_norm_bwd")
    dw_out = _matmul(cat_b, dh1_b, ta=True, name="mm_d_w_out")
    dcat = _matmul(dh1_b, w_out, tb=True, name="mm_d_cat")

    def merge_bwd(dya, dys, ga0, ga1, a, b, gs0, gs1, o, bias):
        ga, gs = _cat([ga0, ga1]), _cat([gs0, gs1])
        sa, dsa = _silu_and_grad(ga)
        ss, dss = _silu_and_grad(gs)
        a, sb = a + bias[:, :D_SSM], _sig(b + bias[:, D_SSM:])
        dy2 = dys * ss
        dglu = _cat([dy2 * sb, dy2 * a * sb * (1.0 - sb)])
        return dya * sa, dya * o * dsa, dys * (a * sb) * dss, dglu, _colsum(dglu)
    do_b, dga_b, dgs_b, dglu_b, db_glu = _rowwise(
        merge_bwd, [(dcat, 0, D_ATTN), (dcat, D_ATTN, D_SSM)] + merge_rows, [b_glu],
        [(D_ATTN, BF16), (D_ATTN, BF16), (D_SSM, BF16), (2 * D_SSM, BF16)], [(1, 2 * D_SSM)], name="merge_bwd")
    dw_glu = _matmul(ge_b, dglu_b, ta=True, name="mm_d_w_glu")
    dge = _matmul(dglu_b, w_glu, tb=True, name="mm_d_ge")
    dy_s, = _rowwise(lambda dge, y: dge * _gelu_grad(y), [(dge, 0, D_SSM), (y_s, 0, D_SSM)], [], [(D_SSM, F32)], name="gelu_bwd")
    du_p, dwb, dwc, dare, daim, d_ssm_d = _ssm_bwd(u_p, _seg_perm(dy_s), ssm_entry, *ssm)
    dcc, dbb = _from_pairs(dwc), jnp.swapaxes(_from_pairs(dwb), -1, -2)
    dc_re, dc_im = dcc[:, :, 0], -dcc[:, :, 1]
    da_re, da_im, dlog_dt, db_re, db_im = disc_vjp((_from_lam(dare), _from_lam(daim), dbb[:, :, 0], dbb[:, :, 1]))

    place = _grad_place()
    early_grads = [dw_glu, dw_out, dw_pg, dw_pp]
    early_sib, early_sums = _reduce_begin(_EARLY, early_grads, place, "early")
    dqr, dkr, dv, *early_chips = _attn_bwd(qr, kr, kr.T, vb, do_b, lse.reshape(N_HEADS, 1, L), _EARLY, early_sums)
    early_totals = _reduce_end(_EARLY, early_grads, early_sib, early_chips, place)
    dq_b, dk_b, dgq, dgk = _attn_prep_bwd(dqr, dkr, z, gq, gk, cos, sin)
    dz_b = _cat([dq_b, dk_b, dv.astype(BF16), dga_b, _seg_unperm(du_p).astype(BF16), dgs_b])
    dw_in = _matmul(hn_b, dz_b, ta=True, name="mm_d_w_in")
    w_in_sib, w_in_sums = _reduce_begin(_W_IN, [dw_in], place, "w_in")
    dhn, *w_in_chips = _matmul(dz_b, w_in, tb=True, name="mm_d_hn", exchange=(_W_IN, w_in_sums))
    w_in_total, = _reduce_end(_W_IN, [dw_in], w_in_sib, w_in_chips, place)

    def norm_bwd(dhn, x, dh1, g):
        dx, dg = _rms_bwd(x, g, dhn)
        return dh1 + dx, _colsum(dg)
    grad_x, dg_mix = _rowwise(norm_bwd, [(a, 0, D_MODEL) for a in (dhn, x, dh1)], [g_mix], [(D_MODEL, F32)], [(1, D_MODEL)],
                              name="norm_mix_bwd")

    small_grads = {"norm_mix": dg_mix, "q_norm": dgq, "k_norm": dgk, "ssm_a_re": da_re, "ssm_a_im": da_im, "ssm_log_dt": dlog_dt,
                   "ssm_b_re": db_re, "ssm_b_im": db_im, "ssm_c_re": dc_re, "ssm_c_im": dc_im, "ssm_d": d_ssm_d,
                   "b_glu": db_glu, "norm_ple": dg_ple, "norm_final": dg_fin}
    return jnp.sum(loss_cols), grad_x, [w_in_total] + early_totals, small_grads, place


_SMALL = ("norm_mix", "q_norm", "k_norm", "ssm_a_re", "ssm_a_im", "ssm_log_dt", "ssm_b_re", "ssm_b_im", "ssm_c_re", "ssm_c_im",
          "ssm_d", "b_glu", "norm_ple", "norm_final")
_WEIGHTS = ("norm_mix", "w_in", "q_norm", "k_norm", "ssm_a_re", "ssm_a_im", "ssm_log_dt", "ssm_b_re", "ssm_b_im", "ssm_c_re",
            "ssm_c_im", "ssm_d", "w_glu", "b_glu", "w_out", "norm_ple", "w_ple_gate", "w_ple_proj", "norm_final")


_SMALL_ADAMW_GROUPS = (("ssm_b_re",), ("ssm_b_im",), ("ssm_c_re", "ssm_c_im"),
                       ("norm_mix", "q_norm", "k_norm", "ssm_a_re", "ssm_a_im", "ssm_log_dt", "ssm_d", "b_glu", "norm_ple",
                        "norm_final"))


def _flat_small(d):
    flat = jnp.concatenate([d[n].reshape(-1).astype(F32) for n in _SMALL])
    return jnp.pad(flat, (0, N_SMALL - flat.shape[0]))


def _split_small(flat, like):
    out, off = {}, 0
    for n in _SMALL:
        sz = math.prod(like[n].shape)
        out[n] = flat[off:off + sz].reshape(like[n].shape)
        off += sz
    return out


def kernel(x, p, norm_mix, w_in, q_norm, k_norm, ssm_a_re, ssm_a_im, ssm_log_dt, ssm_b_re, ssm_b_im, ssm_c_re, ssm_c_im, ssm_d, w_glu, b_glu, w_out, norm_ple, w_ple_gate, w_ple_proj, norm_final, loss_target, m_norm_mix, m_w_in, m_q_norm, m_k_norm, m_ssm_a_re, m_ssm_a_im, m_ssm_log_dt, m_ssm_b_re, m_ssm_b_im, m_ssm_c_re, m_ssm_c_im, m_ssm_d, m_w_glu, m_b_glu, m_w_out, m_norm_ple, m_w_ple_gate, m_w_ple_proj, m_norm_final, v_norm_mix, v_w_in, v_q_norm, v_k_norm, v_ssm_a_re, v_ssm_a_im, v_ssm_log_dt, v_ssm_b_re, v_ssm_b_im, v_ssm_c_re, v_ssm_c_im, v_ssm_d, v_w_glu, v_b_glu, v_w_out, v_norm_ple, v_w_ple_gate, v_w_ple_proj, v_norm_final):
    w = dict(norm_mix=norm_mix, w_in=w_in, q_norm=q_norm, k_norm=k_norm, ssm_a_re=ssm_a_re, ssm_a_im=ssm_a_im,
             ssm_log_dt=ssm_log_dt, ssm_b_re=ssm_b_re, ssm_b_im=ssm_b_im, ssm_c_re=ssm_c_re, ssm_c_im=ssm_c_im, ssm_d=ssm_d,
             w_glu=w_glu, b_glu=b_glu, w_out=w_out, norm_ple=norm_ple, w_ple_gate=w_ple_gate, w_ple_proj=w_ple_proj,
             norm_final=norm_final)
    m = dict(norm_mix=m_norm_mix, w_in=m_w_in, q_norm=m_q_norm, k_norm=m_k_norm, ssm_a_re=m_ssm_a_re, ssm_a_im=m_ssm_a_im,
             ssm_log_dt=m_ssm_log_dt, ssm_b_re=m_ssm_b_re, ssm_b_im=m_ssm_b_im, ssm_c_re=m_ssm_c_re, ssm_c_im=m_ssm_c_im,
             ssm_d=m_ssm_d, w_glu=m_w_glu, b_glu=m_b_glu, w_out=m_w_out, norm_ple=m_norm_ple, w_ple_gate=m_w_ple_gate,
             w_ple_proj=m_w_ple_proj, norm_final=m_norm_final)
    v = dict(norm_mix=v_norm_mix, w_in=v_w_in, q_norm=v_q_norm, k_norm=v_k_norm, ssm_a_re=v_ssm_a_re, ssm_a_im=v_ssm_a_im,
             ssm_log_dt=v_ssm_log_dt, ssm_b_re=v_ssm_b_re, ssm_b_im=v_ssm_b_im, ssm_c_re=v_ssm_c_re, ssm_c_im=v_ssm_c_im,
             ssm_d=v_ssm_d, w_glu=v_w_glu, b_glu=v_b_glu, w_out=v_w_out, norm_ple=v_norm_ple, w_ple_gate=v_w_ple_gate,
             w_ple_proj=v_w_ple_proj, norm_final=v_norm_final)
    big_names = [n for n, _, _, _ in _BIG]

    small = {n: w[n] for n in _SMALL}
    loss_part, grad_x, big_totals, small_grads, place = _forward_backward(
        x[0], p[0, 0].astype(BF16), loss_target[0], [w[n][0].astype(BF16) for n in big_names], small)
    loss = lax.psum(loss_part, ("x", "y", "c"))

    small_flat = [_flat_small(small_grads).reshape(8 * SMALL_ROWS, SMALL_W)]
    small_sib, small_sums = _reduce_begin(_SMALL_RED, small_flat, place, "small")
    small_total = _reduce_end(_SMALL_RED, small_flat, small_sib, _grad_chip_exchange(_SMALL_RED, small_sums), place)
    *big_red, small_red = _grad_final_exchange(big_totals + small_total)
    grads = _split_small(small_red.reshape(-1), w)
    delta, new_m, new_v = {}, {}, {}
    for n, g in zip(big_names, big_red):
        grads[n] = g[None]
        d_, m_, v_ = _adamw(w[n][0], g, m[n][0], v[n][0], "adamw_" + n)
        delta[n], new_m[n], new_v[n] = d_[None], m_[None], v_[None]
    at_least_2d = lambda a: a.reshape(1, -1) if a.ndim == 1 else a
    for i, names in enumerate(_SMALL_ADAMW_GROUPS):
        d_, m_, v_ = _adamw_whole(*[[at_least_2d(src[n]) for n in names] for src in (w, grads, m, v)], "adamw_small_%d" % i)
        for j, n in enumerate(names):
            delta[n], new_m[n], new_v[n] = (a[j].reshape(w[n].shape) for a in (d_, m_, v_))
    return (loss, grad_x[None], *[grads[n] for n in _WEIGHTS], *[delta[n] for n in _WEIGHTS],
            *[new_m[n] for n in _WEIGHTS], *[new_v[n] for n in _WEIGHTS])
```

```python
import functools
import math

import jax
import jax.numpy as jnp
import numpy as np
from jax import lax
from jax.experimental import pallas as pl
from jax.experimental.pallas import tpu as pltpu

D_MODEL = 2048
GRID_W = 64
PLE_DIM = 256
D_ATTN = 1024
N_HEADS = 8
N_KV = 2
HEAD_DIM = 128
ROPE_THETA = 10000.0
D_SSM = 1024
SSM_H = 16
SSM_G = 64
SSM_P = 64
D_KV = N_KV * HEAD_DIM
D_IN = 2 * D_ATTN + 2 * D_KV + 2 * D_SSM
EPS = 1e-6
Z_Q, Z_K, Z_V, Z_GA, Z_U, Z_GS = 0, 1024, 1280, 1536, 2560, 3584

ADAM_LR, ADAM_B1, ADAM_B2, ADAM_EPS, ADAM_WD, ADAM_STEP = 0.001, 0.9, 0.999, 1e-08, 0.01, 10

N_CHIPS = 4
VMEM_LIMIT_V7X = 56 * 1024 * 1024
F32 = jnp.float32
BF16 = jnp.bfloat16


def _params(sem, vmem=VMEM_LIMIT_V7X):
    return pltpu.CompilerParams(dimension_semantics=sem, vmem_limit_bytes=vmem)


def _matmul(a, b, *, ta=False, tb=False, out_dtype=F32, tm=1024, tn=None, name, exchange=None):
    M, K = (a.shape[1], a.shape[0]) if ta else a.shape
    N = b.shape[0] if tb else b.shape[1]
    if tn is None:
        tn = 1024 if (N % 1024 == 0 and K <= 4096) else 512
    tm, tn = min(tm, M), min(tn, N)
    assert M % tm == 0 and N % tn == 0, (name, M, N, K)
    dims = (((0 if ta else 1,), (1 if tb else 0,)), ((), ()))
    ex_ts, ex_sums = exchange if exchange else ((), ())
    n_ex = len(ex_ts)
    gm, gn = M // tm, N // tn

    def body(a_ref, b_ref, *rest):
        o_ref = rest[n_ex]
        if n_ex:
            ex = _ChipExchange(ex_ts, rest[:n_ex], rest[n_ex + 1:2 * n_ex + 1], rest[2 * n_ex + 1:])
            step = pl.program_id(0) * gn + pl.program_id(1)
            pl.when(step == 0)(ex.start)
        o_ref[...] = lax.dot_general(a_ref[...], b_ref[...], dims, preferred_element_type=F32).astype(o_ref.dtype)
        if n_ex:
            pl.when(step == gm * gn - 1)(ex.finish)

    a_spec = pl.BlockSpec((K, tm), lambda i, j: (0, i)) if ta else pl.BlockSpec((tm, K), lambda i, j: (i, 0))
    b_spec = pl.BlockSpec((tn, K), lambda i, j: (j, 0)) if tb else pl.BlockSpec((K, tn), lambda i, j: (0, j))
    res = pl.pallas_call(
        body, name=name,
        out_shape=[jax.ShapeDtypeStruct((M, N), out_dtype)] + (_ChipExchange.out_shape(ex_ts) if n_ex else []),
        grid=(gm, gn),
        in_specs=[a_spec, b_spec] + [_ANY] * n_ex,
        out_specs=[pl.BlockSpec((tm, tn), lambda i, j: (i, j))] + [_ANY] * n_ex,
        scratch_shapes=_ChipExchange.scratch(ex_ts) if n_ex else [],
        compiler_params=_params(("arbitrary", "arbitrary") if n_ex else ("parallel", "parallel")),
    )(a, b, *ex_sums)
    return res if n_ex else res[0]


def _rowwise(fn, rows, consts, outs, accs=(), *, tr=256, name):
    L = rows[0][0].shape[0]
    tr = math.gcd(tr, L)
    assert tr % 8 == 0 or tr == L, (name, L, tr)
    n_in, n_c, n_o, n_a = len(rows), len(consts), len(outs), len(accs)

    def body(*refs):
        ins = [r[...] for r in refs[:n_in + n_c]]
        res = fn(*ins)
        if not isinstance(res, (tuple, list)):
            res = (res,)
        o_refs = refs[n_in + n_c:n_in + n_c + n_o]
        a_refs = refs[n_in + n_c + n_o:]
        for r, v in zip(o_refs, res[:n_o]):
            r[...] = v.astype(r.dtype)
        if n_a:
            first = pl.program_id(0) == 0

            @pl.when(first)
            def _():
                for r, v in zip(a_refs, res[n_o:]):
                    r[...] = v.astype(F32)

            @pl.when(jnp.logical_not(first))
            def _():
                for r, v in zip(a_refs, res[n_o:]):
                    r[...] += v.astype(F32)

    in_specs = []
    for arr, off, w in rows:
        assert off % w == 0, (name, off, w)
        in_specs.append(pl.BlockSpec((tr, w), functools.partial(lambda i, c: (i, c), c=off // w)))
    for c in consts:
        in_specs.append(pl.BlockSpec(c.shape, lambda i: (0, 0)))
    out_shape = [jax.ShapeDtypeStruct((L, w), dt) for w, dt in outs] + [jax.ShapeDtypeStruct(s, F32) for s in accs]
    out_specs = [pl.BlockSpec((tr, w), lambda i: (i, 0)) for w, _ in outs] + [pl.BlockSpec(s, lambda i: (0, 0)) for s in accs]
    res = pl.pallas_call(
        body, name=name,
        out_shape=out_shape,
        grid=(L // tr,),
        in_specs=in_specs,
        out_specs=out_specs,
        compiler_params=_params(("arbitrary",) if n_a else ("parallel",)),
    )(*[r[0] for r in rows], *consts)
    return res


def _sig(x):
    return jax.nn.sigmoid(x)


def _silu_and_grad(x):
    s = _sig(x)
    return x * s, s * (1.0 + x * (1.0 - s))


_GELU_C = math.sqrt(2.0 / math.pi)


def _gelu(x):
    return 0.5 * x * (1.0 + jnp.tanh(_GELU_C * (x + 0.044715 * x * x * x)))


def _gelu_grad(x):
    t = jnp.tanh(_GELU_C * (x + 0.044715 * x * x * x))
    return 0.5 * (1.0 + t) + 0.5 * x * (1.0 - t * t) * _GELU_C * (1.0 + 3.0 * 0.044715 * x * x)


def _rms(x):
    return lax.rsqrt(jnp.mean(x * x, axis=-1, keepdims=True) + EPS)


def _rms_bwd(x, g, dy):
    r = _rms(x)
    n = x * r
    dn = dy * g
    return r * (dn - n * jnp.mean(dn * n, axis=-1, keepdims=True)), dy * n


def _colsum(v):
    return jnp.sum(v, axis=0, keepdims=True)


def _rope_partner(x):
    lane = lax.broadcasted_iota(jnp.int32, x.shape, x.ndim - 1)
    return jnp.where(lane % 64 < 32, pltpu.roll(x, 96, x.ndim - 1), pltpu.roll(x, 32, x.ndim - 1))


def _rope_tables(L):
    rows_n = L // GRID_W
    rows = jnp.repeat(jnp.arange(rows_n), GRID_W).astype(F32)
    cols = jnp.tile(jnp.arange(GRID_W), rows_n).astype(F32)
    n_freq = HEAD_DIM // 4
    inv_freq = ROPE_THETA ** (-jnp.arange(n_freq, dtype=F32) / n_freq)
    ar, ac = rows[:, None] * inv_freq[None, :], cols[:, None] * inv_freq[None, :]
    cos = jnp.concatenate([jnp.cos(ar), jnp.cos(ar), jnp.cos(ac), jnp.cos(ac)], axis=-1)
    sin = jnp.concatenate([-jnp.sin(ar), jnp.sin(ar), -jnp.sin(ac), jnp.sin(ac)], axis=-1)
    return cos, sin


def _heads(v):
    return [v[:, h * HEAD_DIM:(h + 1) * HEAD_DIM] for h in range(v.shape[1] // HEAD_DIM)]


def _attn_prep(z, q_norm, k_norm, cos, sin):
    def fn(q, k, v, cos, sin, gq, gk):
        def one(xh, g):
            xn = xh * _rms(xh) * g
            return xn * cos + _rope_partner(xn) * sin
        qr = jnp.concatenate([one(h, gq) for h in _heads(q)], axis=1)
        kr = jnp.concatenate([one(h, gk) for h in _heads(k)], axis=1)
        return qr, kr, v
    return _rowwise(fn, [(z, Z_Q, D_ATTN), (z, Z_K, D_KV), (z, Z_V, D_KV), (cos, 0, HEAD_DIM), (sin, 0, HEAD_DIM)],
                    [q_norm, k_norm], [(D_ATTN, BF16), (D_KV, BF16), (D_KV, BF16)], name="attn_prep")


def _attn_prep_bwd(dqr, dkr, z, q_norm, k_norm, cos, sin):
    def fn(dqr, dkr, q, k, cos, sin, gq, gk):
        def one(dyh, xh, g):
            dn = dyh * cos + _rope_partner(dyh * sin)
            return _rms_bwd(xh, g, dn)
        rq = [one(a, b, gq) for a, b in zip(_heads(dqr), _heads(q))]
        rk = [one(a, b, gk) for a, b in zip(_heads(dkr), _heads(k))]
        dq = jnp.concatenate([r[0] for r in rq], axis=1)
        dk = jnp.concatenate([r[0] for r in rk], axis=1)
        return dq, dk, _colsum(sum(r[1] for r in rq)), _colsum(sum(r[1] for r in rk))
    return _rowwise(fn, [(dqr, 0, D_ATTN), (dkr, 0, D_KV), (z, Z_Q, D_ATTN), (z, Z_K, D_KV), (cos, 0, HEAD_DIM), (sin, 0, HEAD_DIM)],
                    [q_norm, k_norm], [(D_ATTN, BF16), (D_KV, BF16)], [(1, HEAD_DIM), (1, HEAD_DIM)], name="attn_prep_bwd")


_QK_T = (((1,), (1,)), ((), ()))
_TA = (((0,), (0,)), ((), ()))
_REP = N_HEADS // N_KV


_EXP2_SCALE = HEAD_DIM ** -0.5 * math.log2(math.e)
ATTN_FWD_KEY_CHUNKS = 4
ATTN_BWD_KEY_CHUNKS = 8


def _attn_fwd(qr, kr, vb, g_ts, g_shards, *, tq=1024):
    L = qr.shape[0]
    tq = min(tq, L)
    kc = L // ATTN_FWD_KEY_CHUNKS
    n_g = len(g_ts)
    grid = (N_HEADS, L // tq)
    steps = grid[0] * grid[1]

    def body(q_ref, k_ref, v_ref, *rest):
        o_ref, lse_ref = rest[n_g:n_g + 2]
        g = _Gather(g_ts, rest[:n_g], rest[n_g + 2:2 * n_g + 2], rest[2 * n_g + 2:3 * n_g + 2], rest[3 * n_g + 2:])
        step = pl.program_id(0) * grid[1] + pl.program_id(1)
        pl.when(step == 0)(g.start)
        pl.when(step == (3 * steps) // 4)(g.forward)
        q = q_ref[...]
        m = jnp.full((tq, 1), -jnp.inf, F32)
        l = jnp.zeros((tq, 1), F32)
        o = jnp.zeros((tq, HEAD_DIM), F32)
        for c in range(ATTN_FWD_KEY_CHUNKS):
            ks = slice(c * kc, (c + 1) * kc)
            s = lax.dot_general(q, k_ref[ks, :], _QK_T, preferred_element_type=F32)
            m_new = jnp.maximum(m, jnp.max(s, axis=1, keepdims=True))
            a = jnp.exp2((m - m_new) * _EXP2_SCALE)
            p = jnp.exp2((s - m_new) * _EXP2_SCALE)
            l = a * l + jnp.sum(p, axis=1, keepdims=True)
            o = a * o + jnp.dot(p.astype(BF16), v_ref[ks, :], preferred_element_type=F32)
            m = m_new
        o_ref[...] = o * (1.0 / l)
        lse_ref[...] = m * _EXP2_SCALE + jnp.log2(l)
        pl.when(step == steps - 1)(g.finish)

    kv = pl.BlockSpec((L, HEAD_DIM), lambda h, i: (0, h // _REP))
    return pl.pallas_call(
        body, name="attn_fwd",
        out_shape=[jax.ShapeDtypeStruct((L, D_ATTN), F32), jax.ShapeDtypeStruct((N_HEADS, L, 1), F32)] +
                  [jax.ShapeDtypeStruct(_BIG[t][1], BF16) for t in g_ts],
        grid=grid,
        in_specs=[pl.BlockSpec((tq, HEAD_DIM), lambda h, i: (i, h)), kv, kv] + [_ANY] * n_g,
        out_specs=[pl.BlockSpec((tq, HEAD_DIM), lambda h, i: (i, h)),
                   pl.BlockSpec((None, tq, 1), lambda h, i: (h, i, 0))] + [_ANY] * n_g,
        scratch_shapes=_Gather.scratch(g_shards),
        compiler_params=_params(("arbitrary", "arbitrary")),
    )(qr, kr, vb, *g_shards)


def _attn_bwd(qr, kr, k_t, vb, do, lse, delta, ex_ts, ex_sums, *, tq=1024):
    L = qr.shape[0]
    tq = min(tq, L)
    scale = HEAD_DIM ** -0.5
    kc = L // ATTN_BWD_KEY_CHUNKS
    n_ex = len(ex_ts)
    grid = (N_KV, _REP, L // tq)

    def body(q_ref, k_ref, kt_ref, v_ref, do_ref, lse_ref, delta_ref, *rest):
        dq_ref, dk_ref, dv_ref = rest[n_ex:n_ex + 3]
        ex = _ChipExchange(ex_ts, rest[:n_ex], rest[n_ex + 3:2 * n_ex + 3], rest[2 * n_ex + 3:])
        step = (pl.program_id(0) * grid[1] + pl.program_id(1)) * grid[2] + pl.program_id(2)
        pl.when(step == 0)(ex.start)

        @pl.when((pl.program_id(1) == 0) & (pl.program_id(2) == 0))
        def _():
            dk_ref[...] = jnp.zeros_like(dk_ref)
            dv_ref[...] = jnp.zeros_like(dv_ref)

        q, do, lse, delta = q_ref[...], do_ref[...], lse_ref[...], delta_ref[...]
        dq_t = 0.0
        for c in range(ATTN_BWD_KEY_CHUNKS):
            ks = slice(c * kc, (c + 1) * kc)
            st = lax.dot_general(k_ref[ks, :], q, _QK_T, preferred_element_type=F32)
            p = jnp.exp2(st * _EXP2_SCALE - lse)
            dv_ref[ks, :] += jnp.dot(p.astype(BF16), do, preferred_element_type=F32)
            dp = lax.dot_general(v_ref[ks, :], do, _QK_T, preferred_element_type=F32)
            ds = (p * (dp - delta) * scale).astype(BF16)
            dk_ref[ks, :] += jnp.dot(ds, q, preferred_element_type=F32)
            dq_t = dq_t + jnp.dot(kt_ref[:, ks], ds, preferred_element_type=F32)
        dq_ref[...] = dq_t.T
        pl.when(step == grid[0] * grid[1] * grid[2] - 1)(ex.finish)

    head = lambda g, r, i: (i, g * _REP + r)
    kv = pl.BlockSpec((L, HEAD_DIM), lambda g, r, i: (0, g))
    per_query = pl.BlockSpec((None, 1, tq), lambda g, r, i: (g * _REP + r, 0, i))
    return pl.pallas_call(
        body, name="attn_bwd",
        out_shape=[jax.ShapeDtypeStruct((L, D_ATTN), F32), jax.ShapeDtypeStruct((L, D_KV), F32), jax.ShapeDtypeStruct((L, D_KV), F32)] +
                  _ChipExchange.out_shape(ex_ts),
        grid=grid,
        in_specs=[pl.BlockSpec((tq, HEAD_DIM), head), kv,
                  pl.BlockSpec((HEAD_DIM, L), lambda g, r, i: (g, 0)), kv,
                  pl.BlockSpec((tq, HEAD_DIM), head), per_query, per_query] + [_ANY] * n_ex,
        out_specs=[pl.BlockSpec((tq, HEAD_DIM), head), kv, kv] + [_ANY] * n_ex,
        scratch_shapes=_ChipExchange.scratch(ex_ts),
        compiler_params=_params(("arbitrary", "arbitrary", "arbitrary")),
    )(qr, kr, k_t, vb, do, lse, delta, *ex_sums)


SSM_BLK = 8
SSM_NB = SSM_G // SSM_BLK
SSM_SEG = 8
SSM_UNROLL = 4


def _unrolled_loop(n, step, carry):
    u = SSM_UNROLL

    def trip(i, c):
        for j in range(u):
            c = step(i * u + j, c)
        return c
    carry = lax.fori_loop(0, n // u, trip, carry)
    for t in range(n - n % u, n):
        carry = step(jnp.int32(t), carry)
    return carry


def _cplx_pow2(a, b, n):
    for _ in range(int(math.log2(n))):
        a, b = a * a - b * b, 2.0 * a * b
    return a, b


def _seg_scan(ref, a, b, T, reverse, entry=None, tap=None):
    npair = len(a)
    zero = jnp.zeros((SSM_SEG, 128), F32)

    def make_step(store, tap_fn=None):
        def step(t, carry):
            lt = (T - 1 - t) if reverse else t
            row = pl.multiple_of(lt * SSM_SEG, SSM_SEG)
            blk = ref[pl.ds(row, SSM_SEG), :]
            new = []
            for q in range(npair):
                re, im = carry[2 * q], carry[2 * q + 1]
                nre = a[q] * re - b[q] * im + blk[:, q * 256:q * 256 + 128]
                nim = a[q] * im + b[q] * re + blk[:, q * 256 + 128:q * 256 + 256]
                new += [nre, nim]
            if store:
                ref[pl.ds(row, SSM_SEG), :] = jnp.concatenate(new, axis=1)
            extra = carry[2 * npair:]
            return tuple(new) + (tuple(tap_fn(lt, new, extra)) if tap_fn else tuple(extra))
        return step

    def second_pass(init):
        if tap is None:
            _unrolled_loop(T, make_step(True), tuple(init))
            return init
        carry = _unrolled_loop(T - 1, make_step(True, tap[0]), tuple(init) + tuple(tap[2]))
        carry = make_step(True, tap[1])(jnp.int32(T - 1), carry)
        return init, carry[2 * npair:]

    if entry is not None:
        return second_pass(entry)
    ends = _unrolled_loop(T, make_step(False), (zero,) * (2 * npair))
    sub = lax.broadcasted_iota(jnp.int32, (SSM_SEG, 128), 0)
    keep = (sub != SSM_SEG - 1) if reverse else (sub != 0)
    shift = (SSM_SEG - 1) if reverse else 1
    init = []
    for q in range(npair):
        pa, pb = _cplx_pow2(a[q], b[q], T)
        xr, xi = zero, zero
        for _ in range(SSM_SEG - 1):
            fr = ends[2 * q] + pa * xr - pb * xi
            fi = ends[2 * q + 1] + pa * xi + pb * xr
            xr = jnp.where(keep, pltpu.roll(fr, shift, 0), 0.0)
            xi = jnp.where(keep, pltpu.roll(fi, shift, 0), 0.0)
        init += [xr, xi]
    return second_pass(init)


SSM_BW = SSM_BLK * SSM_H
SSM_SW = SSM_BLK * 2 * SSM_P
SSM_NPAIR = SSM_BLK // 2


def _seg_perm(a):
    L, C = a.shape
    return a.reshape(SSM_SEG, L // SSM_SEG, C).transpose(1, 0, 2).reshape(L, C)


def _seg_unperm(a):
    L, C = a.shape
    return a.reshape(L // SSM_SEG, SSM_SEG, C).transpose(1, 0, 2).reshape(L, C)


def _lam_rows(are_ref, aim_ref, d):
    a = [jnp.broadcast_to(are_ref[d, j:j + 1, :], (SSM_SEG, 128)) for j in range(SSM_NPAIR)]
    b = [jnp.broadcast_to(aim_ref[d, j:j + 1, :], (SSM_SEG, 128)) for j in range(SSM_NPAIR)]
    return a, b


_PAIR_SPEC = pl.BlockSpec((None, 2, SSM_NPAIR, 2, 2 * SSM_H, 128), lambda g: (g, 0, 0, 0, 0, 0))


def _pair_window(j, r):
    return slice(j * 2 * SSM_H, (j + 1) * 2 * SSM_H), slice(j * 256 + r * 128, j * 256 + (r + 1) * 128)


def _expand_pairs(c_ref, dense_ref):
    dense_ref[...] = jnp.zeros_like(dense_ref)
    for d in range(2):
        for j in range(SSM_NPAIR):
            for r in range(2):
                rows, cols = _pair_window(j, r)
                dense_ref[d, rows, cols] = c_ref[d, j, r].astype(dense_ref.dtype)


def _ssm_fwd(u_p, wb, wc, are, aim, dvec):
    L = u_p.shape[0]
    T = L // SSM_SEG
    RC = min(512, L)

    def body(u_ref, wb_ref, wc_ref, are_ref, aim_ref, d_ref, y_ref, entry_ref, x_scr, wb_s, wc_s):
        _expand_pairs(wb_ref, wb_s)
        _expand_pairs(wc_ref, wc_s)
        y_ref[...] = u_ref[...] * d_ref[...]
        for d in range(2):
            def bu_chunk(c, _):
                rows = pl.ds(pl.multiple_of(c * RC, RC), RC)
                x_scr[rows, :] = jnp.dot(u_ref[rows, :].astype(BF16), wb_s[d], preferred_element_type=F32)
                return 0
            lax.fori_loop(0, L // RC, bu_chunk, 0)
            a, b = _lam_rows(are_ref, aim_ref, d)
            entry_ref[d] = jnp.concatenate(_seg_scan(x_scr, a, b, T, reverse=(d == 1)), axis=1)

            def y_chunk(c, _):
                rows = pl.ds(pl.multiple_of(c * RC, RC), RC)
                y_ref[rows, :] += lax.dot_general(x_scr[rows, :].astype(BF16), wc_s[d], _QK_T, preferred_element_type=F32)
                return 0
            lax.fori_loop(0, L // RC, y_chunk, 0)

    blk4 = lambda g: (g, 0, 0, 0)
    return pl.pallas_call(
        body, name="ssm_fwd",
        out_shape=[jax.ShapeDtypeStruct((L, D_SSM), F32), jax.ShapeDtypeStruct((SSM_NB, 2, SSM_SEG, SSM_SW), F32)],
        grid=(SSM_NB,),
        in_specs=[pl.BlockSpec((L, SSM_BW), lambda g: (0, g)), _PAIR_SPEC, _PAIR_SPEC,
                  pl.BlockSpec((None, 2, SSM_NPAIR, 128), blk4),
                  pl.BlockSpec((None, 2, SSM_NPAIR, 128), blk4),
                  pl.BlockSpec((1, SSM_BW), lambda g: (0, g))],
        out_specs=[pl.BlockSpec((L, SSM_BW), lambda g: (0, g)), pl.BlockSpec((None, 2, SSM_SEG, SSM_SW), blk4)],
        scratch_shapes=[pltpu.VMEM((L, SSM_SW), F32), pltpu.VMEM((2, SSM_BW, SSM_SW), BF16), pltpu.VMEM((2, SSM_BW, SSM_SW), BF16)],
        compiler_params=_params(("parallel",)),
    )(u_p, wb, wc, are, aim, dvec)


def _ssm_bwd(u_p, dy_p, entry, wb, wc, are, aim, dvec):
    L = u_p.shape[0]
    T = L // SSM_SEG
    RC = min(512, L)

    def lam_acc(acc, s, x):
        new = []
        for q in range(SSM_NPAIR):
            sr, si, xr, xi = s[2 * q], s[2 * q + 1], x[2 * q], x[2 * q + 1]
            new += [acc[2 * q] + sr * xr + si * xi, acc[2 * q + 1] + si * xr - sr * xi]
        return tuple(new)

    def body(u_ref, dy_ref, entry_ref, wb_ref, wc_ref, are_ref, aim_ref, d_ref,
             du_ref, dwb_ref, dwc_ref, dare_ref, daim_ref, dd_ref, x_scr, s_scr, wb_s, wc_s, dwb_s, dwc_s):
        _expand_pairs(wb_ref, wb_s)
        _expand_pairs(wc_ref, wc_s)
        du_ref[...] = dy_ref[...] * d_ref[...]
        dd_ref[...] = _colsum(dy_ref[...] * u_ref[...])
        dwb_s[...] = jnp.zeros_like(dwb_s)
        dwc_s[...] = jnp.zeros_like(dwc_s)
        for d in range(2):
            rev = d == 1

            def in_chunk(c, _):
                rows = pl.ds(pl.multiple_of(c * RC, RC), RC)
                x_scr[rows, :] = jnp.dot(u_ref[rows, :].astype(BF16), wb_s[d], preferred_element_type=F32)
                s_scr[rows, :] = jnp.dot(dy_ref[rows, :].astype(BF16), wc_s[d], preferred_element_type=F32)
                return 0
            lax.fori_loop(0, L // RC, in_chunk, 0)
            a, b = _lam_rows(are_ref, aim_ref, d)
            x_in = _seg_scan(x_scr, a, b, T, reverse=rev, entry=[entry_ref[d, :, q * 128:(q + 1) * 128] for q in range(2 * SSM_NPAIR)])

            def pair_with_row(lt, s_new, acc):
                xrow = pl.multiple_of((lt + 1 if rev else lt - 1) * SSM_SEG, SSM_SEG)
                xb = x_scr[pl.ds(xrow, SSM_SEG), :]
                return lam_acc(acc, s_new, [xb[:, i * 128:(i + 1) * 128] for i in range(2 * SSM_NPAIR)])

            zeros = (jnp.zeros((SSM_SEG, 128), F32),) * (2 * SSM_NPAIR)
            _, acc = _seg_scan(s_scr, a, [-v for v in b], T, reverse=not rev,
                               tap=(pair_with_row, lambda lt, s_new, acc: lam_acc(acc, s_new, x_in), zeros))
            for q in range(SSM_NPAIR):
                dare_ref[d, q:q + 1, :] = _colsum(acc[2 * q])
                daim_ref[d, q:q + 1, :] = _colsum(acc[2 * q + 1])

            def out_chunk(c, _):
                rows = pl.ds(pl.multiple_of(c * RC, RC), RC)
                xs, ss = x_scr[rows, :].astype(BF16), s_scr[rows, :].astype(BF16)
                uu, dd = u_ref[rows, :].astype(BF16), dy_ref[rows, :].astype(BF16)
                dwc_s[d] += lax.dot_general(dd, xs, _TA, preferred_element_type=F32)
                dwb_s[d] += lax.dot_general(uu, ss, _TA, preferred_element_type=F32)
                du_ref[rows, :] += lax.dot_general(ss, wb_s[d], _QK_T, preferred_element_type=F32)
                return 0
            lax.fori_loop(0, L // RC, out_chunk, 0)
            for j in range(SSM_NPAIR):
                for r in range(2):
                    rows, cols = _pair_window(j, r)
                    dwb_ref[d, j, r] = dwb_s[d, rows, cols]
                    dwc_ref[d, j, r] = dwc_s[d, rows, cols]

    blk4 = lambda g: (g, 0, 0, 0)
    chan = pl.BlockSpec((L, SSM_BW), lambda g: (0, g))
    par_specs = [_PAIR_SPEC, _PAIR_SPEC,
                 pl.BlockSpec((None, 2, SSM_NPAIR, 128), blk4),
                 pl.BlockSpec((None, 2, SSM_NPAIR, 128), blk4),
                 pl.BlockSpec((1, SSM_BW), lambda g: (0, g))]
    dense = lambda dt: pltpu.VMEM((2, SSM_BW, SSM_SW), dt)
    return pl.pallas_call(
        body, name="ssm_bwd",
        out_shape=[jax.ShapeDtypeStruct((L, D_SSM), F32),
                   jax.ShapeDtypeStruct(wb.shape, F32),
                   jax.ShapeDtypeStruct(wc.shape, F32),
                   jax.ShapeDtypeStruct((SSM_NB, 2, SSM_NPAIR, 128), F32),
                   jax.ShapeDtypeStruct((SSM_NB, 2, SSM_NPAIR, 128), F32),
                   jax.ShapeDtypeStruct((1, D_SSM), F32)],
        grid=(SSM_NB,),
        in_specs=[chan, chan, pl.BlockSpec((None, 2, SSM_SEG, SSM_SW), blk4)] + par_specs,
        out_specs=[chan] + par_specs,
        scratch_shapes=[pltpu.VMEM((L, SSM_SW), F32), pltpu.VMEM((L, SSM_SW), F32), dense(BF16), dense(BF16), dense(F32), dense(F32)],
        compiler_params=_params(("parallel",)),
    )(u_p, dy_p, entry, wb, wc, are, aim, dvec)


def _ssm_disc(a_re, a_im, log_dt, b_re, b_im):
    lam = lax.complex(jnp.minimum(a_re, -1e-4), a_im)
    dt = jnp.exp(log_dt)[..., None]
    lam_bar = jnp.exp(lam * dt)
    b_bar = ((lam_bar - 1.0) / lam)[..., None] * lax.complex(b_re, b_im)
    return jnp.real(lam_bar), jnp.imag(lam_bar), jnp.real(b_bar), jnp.imag(b_bar)


_EYE2 = np.eye(2, dtype=np.float32)[:, None, :, None]


def _to_pairs(t):
    t = t.reshape(2, SSM_NB, SSM_NPAIR, 2, 2, SSM_H, SSM_P).transpose(1, 0, 2, 4, 3, 5, 6)
    return (t[..., None, :] * _EYE2).reshape(SSM_NB, 2, SSM_NPAIR, 2, 2 * SSM_H, 2 * SSM_P)


def _from_pairs(c):
    t = jnp.sum(c.reshape(SSM_NB, 2, SSM_NPAIR, 2, 2, SSM_H, 2, SSM_P) * _EYE2, axis=6)
    return t.transpose(1, 0, 2, 4, 3, 5, 6).reshape(2, SSM_G, 2, SSM_H, SSM_P)


def _to_lam(v):
    return v.reshape(2, SSM_NB, SSM_NPAIR, 128).transpose(1, 0, 2, 3)


def _from_lam(v):
    return v.transpose(1, 0, 2, 3).reshape(2, SSM_G, SSM_P)


_MESH = pl.DeviceIdType.MESH
_ANY = pl.BlockSpec(memory_space=pl.ANY)
_BIG = (("w_in", (D_MODEL, D_IN), 1, D_IN // N_CHIPS),
        ("w_glu", (D_SSM, 2 * D_SSM), 1, 2 * D_SSM // N_CHIPS),
        ("w_out", (D_ATTN + D_SSM, D_MODEL), 0, (D_ATTN + D_SSM) // N_CHIPS),
        ("w_ple_gate", (D_MODEL, D_MODEL), 0, D_MODEL // N_CHIPS),
        ("w_ple_proj", (PLE_DIM, D_MODEL), 1, D_MODEL // N_CHIPS))


def _place():
    x, y, c = lax.axis_index("x"), lax.axis_index("y"), lax.axis_index("c")
    return x, y, c, [(1 - x, y), (x, 1 - y), (1 - x, 1 - y)]


class _Gather:
    def __init__(self, ts, srcs, dsts, stage, sems):
        self.ts, self.srcs, self.dsts, self.stage = ts, srcs, dsts, stage
        self.send_sems, self.recv_sems, self.fwd_send_sems, self.fwd_recv_sems, self.loc_sems = sems
        self.x, self.y, self.c, self.chips = _place()
        self.n = len(ts)

    @staticmethod
    def scratch(shards):
        sems = pltpu.SemaphoreType.DMA((3, len(shards)))
        return [pltpu.VMEM(s.shape, BF16) for s in shards] + [sems, sems, sems, sems, pltpu.SemaphoreType.DMA((len(shards),))]

    def _shard_of(self, i, kk):
        _, _, axis, sz = _BIG[self.ts[i]]
        sl = pl.ds(pl.multiple_of(kk * sz, sz), sz)
        return self.dsts[i].at[:, sl] if axis == 1 else self.dsts[i].at[sl, :]

    @staticmethod
    def _half_of(ref, cc):
        n = ref.shape[0] // 2
        return ref.at[pl.ds(pl.multiple_of(cc * n, n), n), :]

    def _ici(self, j, i, kk):
        px, py = self.chips[j]
        return pltpu.make_async_remote_copy(
            src_ref=self._half_of(self.srcs[i], self.c), dst_ref=self._half_of(self._shard_of(i, kk), self.c),
            send_sem=self.send_sems.at[j, i], recv_sem=self.recv_sems.at[j, i],
            device_id=(px, py, self.c), device_id_type=_MESH)

    def _forward(self, j, i, kk, cc):
        part = self._half_of(self._shard_of(i, kk), cc)
        return pltpu.make_async_remote_copy(
            src_ref=part, dst_ref=part, send_sem=self.fwd_send_sems.at[j, i], recv_sem=self.fwd_recv_sems.at[j, i],
            device_id=(self.x, self.y, 1 - self.c), device_id_type=_MESH)

    def _load(self, i):
        return pltpu.make_async_copy(self.srcs[i], self.stage[i], self.loc_sems.at[i])

    def _place_own(self, i):
        return pltpu.make_async_copy(self.stage[i], self._shard_of(i, 2 * self.x + self.y), self.loc_sems.at[i])

    def _peers(self):
        return [(i, j, 2 * px + py) for i in range(self.n) for j, (px, py) in enumerate(self.chips)]

    def start(self):
        for i in range(self.n):
            self._load(i).start()
        for i, j, _ in self._peers():
            self._ici(j, i, 2 * self.x + self.y).start()

    def forward(self):
        for i in range(self.n):
            self._load(i).wait()
            self._place_own(i).start()
        for i, j, kk in self._peers():
            self._ici(j, i, kk).wait_recv()
            self._forward(j, i, kk, self.c).start()

    def finish(self):
        for i, j, kk in self._peers():
            self._forward(j, i, kk, 1 - self.c).wait_recv()
        for i, j, kk in self._peers():
            self._ici(j, i, kk).wait_send()
            self._forward(j, i, kk, self.c).wait_send()
        for i in range(self.n):
            self._place_own(i).wait()


def _gather_weights(ts, shards):
    n = len(ts)

    def body(*refs):
        g = _Gather(ts, refs[:n], refs[n:2 * n], refs[2 * n:3 * n], refs[3 * n:])
        g.start()
        g.forward()
        g.finish()

    return pl.pallas_call(
        body, name="gather_weights",
        out_shape=[jax.ShapeDtypeStruct(_BIG[t][1], BF16) for t in ts],
        in_specs=[_ANY] * n, out_specs=[_ANY] * n,
        scratch_shapes=_Gather.scratch(shards),
        compiler_params=pltpu.CompilerParams(vmem_limit_bytes=VMEM_LIMIT_V7X),
    )(*shards)


SMALL_W = 1024
SMALL_ROWS = 72
N_SMALL = 8 * SMALL_ROWS * SMALL_W
_RED = tuple((shape, ax, (shape[0] // 2, sz) if ax == 1 else (sz // 2, shape[1]), BF16) for _, shape, ax, sz in _BIG) + \
    (((8 * SMALL_ROWS, SMALL_W), 0, (SMALL_ROWS, SMALL_W), F32),)
_RED_TR = 128


def _piece(ref, t, kk, cc):
    _, ax, (pr, pc), _ = _RED[t]
    if ax == 1:
        return ref.at[pl.ds(pl.multiple_of(cc * pr, pr), pr), pl.ds(pl.multiple_of(kk * pc, pc), pc)]
    return ref.at[pl.ds(pl.multiple_of((2 * kk + cc) * pr, pr), pr), :]


def _half_shape(t):
    shape, ax, (pr, pc), _ = _RED[t]
    return (pr, shape[1]) if ax == 1 else (N_CHIPS * pr, pc)


def _piece_in_half(ref, t, kk):
    _, ax, (pr, pc), _ = _RED[t]
    return ref.at[:, pl.ds(pl.multiple_of(kk * pc, pc), pc)] if ax == 1 else ref.at[pl.ds(pl.multiple_of(kk * pr, pr), pr), :]


def _grad_sibling_exchange(ts, grads, name):
    n = len(ts)
    n_dma = sum(1 if _RED[t][1] == 1 else N_CHIPS for t in ts)

    def body(*refs):
        srcs, dsts, (send_sems, recv_sems) = refs[:n], refs[n:2 * n], refs[2 * n:]
        x, y, c, _ = _place()
        pairs = []
        for i, t in enumerate(ts):
            _, ax, (pr, _), _ = _RED[t]
            if ax == 1:
                pairs.append((srcs[i].at[pl.ds(pl.multiple_of((1 - c) * pr, pr), pr), :], dsts[i]))
            else:
                pairs += [(_piece(srcs[i], t, kk, 1 - c), _piece_in_half(dsts[i], t, kk)) for kk in range(N_CHIPS)]
        cps = [pltpu.make_async_remote_copy(src_ref=s, dst_ref=d, send_sem=send_sems.at[i], recv_sem=recv_sems.at[i],
                                            device_id=(x, y, 1 - c), device_id_type=_MESH) for i, (s, d) in enumerate(pairs)]
        for cp in cps:
            cp.start()
        for cp in cps:
            cp.wait()

    return pl.pallas_call(
        body, name=name,
        out_shape=[jax.ShapeDtypeStruct(_half_shape(t), F32) for t in ts],
        in_specs=[_ANY] * n, out_specs=[_ANY] * n,
        scratch_shapes=[pltpu.SemaphoreType.DMA((n_dma,)), pltpu.SemaphoreType.DMA((n_dma,))],
    )(*grads)


def _chip_sum(t, g, rs, place):
    shape, ax, (pr, pc), dt = _RED[t]
    W = shape[1]
    tr = min(pr, _RED_TR)
    nb = pr // tr

    def body(place_ref, g_ref, rs_ref, o_ref):
        o_ref[...] = (g_ref[...] + rs_ref[...]).astype(o_ref.dtype)

    return pl.pallas_call(
        body, name="grad_chip_sum_%d" % t,
        out_shape=jax.ShapeDtypeStruct(rs.shape, dt),
        grid_spec=pltpu.PrefetchScalarGridSpec(
            num_scalar_prefetch=1, grid=(1 if ax == 1 else N_CHIPS, nb),
            in_specs=[pl.BlockSpec((tr, W), lambda kk, i, pr_: ((2 * kk + pr_[0]) * nb + i, 0)),
                      pl.BlockSpec((tr, W), lambda kk, i, pr_: (kk * nb + i, 0))],
            out_specs=pl.BlockSpec((tr, W), lambda kk, i, pr_: (kk * nb + i, 0))),
        compiler_params=_params(("parallel", "parallel")),
    )(place, g, rs)


class _ChipExchange:
    def __init__(self, ts, srcs, dsts, sems):
        self.send_sems, self.recv_sems = sems
        x, y, c, chips = _place()
        self.copies = lambda: [
            pltpu.make_async_remote_copy(src_ref=_piece_in_half(srcs[i], t, 2 * px + py), dst_ref=dsts[i].at[j],
                                         send_sem=self.send_sems.at[j, i], recv_sem=self.recv_sems.at[j, i],
                                         device_id=(px, py, c), device_id_type=_MESH)
            for i, t in enumerate(ts) for j, (px, py) in enumerate(chips)]

    @staticmethod
    def scratch(ts):
        return [pltpu.SemaphoreType.DMA((3, len(ts))), pltpu.SemaphoreType.DMA((3, len(ts)))]

    @staticmethod
    def out_shape(ts):
        return [jax.ShapeDtypeStruct((3,) + _RED[t][2], _RED[t][3]) for t in ts]

    def start(self):
        for cp in self.copies():
            cp.start()

    def finish(self):
        for cp in self.copies():
            cp.wait()


def _grad_chip_exchange(ts, sums):
    n = len(ts)

    def body(*refs):
        ex = _ChipExchange(ts, refs[:n], refs[n:2 * n], refs[2 * n:])
        ex.start()
        ex.finish()

    return pl.pallas_call(
        body, name="grad_chip_exchange",
        out_shape=_ChipExchange.out_shape(ts),
        in_specs=[_ANY] * n, out_specs=[_ANY] * n,
        scratch_shapes=_ChipExchange.scratch(ts),
    )(*sums)


def _total_sum(t, g, rs, rc, place):
    shape, ax, (pr, pc), _ = _RED[t]
    tr = min(pr, _RED_TR)
    nb = pr // tr
    small = t == len(_RED) - 1

    def body(place_ref, g_ref, rs_ref, rc_ref, o_ref):
        o_ref[...] = (g_ref[...] + rs_ref[...]) + rc_ref[0].astype(F32) + rc_ref[1].astype(F32) + rc_ref[2].astype(F32)

    if ax == 1:
        g_map = lambda i, pr_: (pr_[0] * nb + i, pr_[1])
        rs_map = lambda i, pr_: (i, pr_[1])
    else:
        g_map = lambda i, pr_: ((2 * pr_[1] + pr_[0]) * nb + i, 0)
        rs_map = lambda i, pr_: (pr_[1] * nb + i, 0)
    o_map = (lambda i, pr_: ((2 * pr_[1] + pr_[0]) * nb + i, 0)) if small else (lambda i, pr_: (pr_[0] * nb + i, 0))
    return pl.pallas_call(
        body, name="grad_total_sum_%d" % t,
        out_shape=jax.ShapeDtypeStruct(((8 if small else 2) * pr, pc), F32),
        grid_spec=pltpu.PrefetchScalarGridSpec(
            num_scalar_prefetch=1, grid=(nb,),
            in_specs=[pl.BlockSpec((tr, pc), g_map), pl.BlockSpec((tr, pc), rs_map),
                      pl.BlockSpec((3, tr, pc), lambda i, pr_: (0, i, 0))],
            out_specs=pl.BlockSpec((tr, pc), o_map)),
        compiler_params=_params(("parallel",)),
    )(place, g, rs, rc)


def _grad_final_exchange(totals):
    n = len(_RED)
    nb = n - 1

    def body(*refs):
        srcs, dsts, (send_sems, recv_sems) = refs[:n], refs[n:2 * n], refs[2 * n:]
        x, y, c, chips = _place()
        me = 4 * x + 2 * y + c
        others = [(x, y, 1 - c)] + [(px, py, cc) for (px, py) in chips for cc in (c, 1 - c)]

        def half(ref, t, cc):
            pr = _RED[t][2][0]
            return ref.at[pl.ds(pl.multiple_of(cc * pr, pr), pr), :]

        def eighth(ref, dev):
            return ref.at[pl.ds(pl.multiple_of(dev * SMALL_ROWS, SMALL_ROWS), SMALL_ROWS), :]

        def big_copy(t, cc):
            return pltpu.make_async_remote_copy(src_ref=half(srcs[t], t, cc), dst_ref=half(dsts[t], t, cc), send_sem=send_sems.at[t],
                                                recv_sem=recv_sems.at[t], device_id=others[0], device_id_type=_MESH)

        def small_copy(i, dev):
            return pltpu.make_async_remote_copy(src_ref=eighth(srcs[nb], dev), dst_ref=eighth(dsts[nb], dev),
                                                send_sem=send_sems.at[nb + i], recv_sem=recv_sems.at[nb + i],
                                                device_id=others[i], device_id_type=_MESH)

        sends = [big_copy(t, c) for t in range(nb)] + [small_copy(i, me) for i in range(7)]
        for cp in sends:
            cp.start()
        for t in range(nb):
            big_copy(t, 1 - c).wait_recv()
        for i, (px, py, pc) in enumerate(others):
            small_copy(i, 4 * px + 2 * py + pc).wait_recv()
        for cp in sends:
            cp.wait_send()

    return pl.pallas_call(
        body, name="grad_final_exchange",
        out_shape=[jax.ShapeDtypeStruct(a.shape, F32) for a in totals],
        in_specs=[_ANY] * n, out_specs=[_ANY] * n,
        input_output_aliases={t: t for t in range(n)},
        scratch_shapes=[pltpu.SemaphoreType.DMA((nb + 7,)), pltpu.SemaphoreType.DMA((nb + 7,))],
    )(*totals)


def _grad_place():
    return jnp.stack([lax.axis_index("c"), 2 * lax.axis_index("x") + lax.axis_index("y")]).astype(jnp.int32)


def _reduce_begin(ts, grads, place, tag):
    from_sibling = _grad_sibling_exchange(ts, grads, "grad_sibling_exchange_" + tag)
    return from_sibling, [_chip_sum(t, g, r, place) for t, g, r in zip(ts, grads, from_sibling)]


def _reduce_end(ts, grads, from_sibling, from_chips, place):
    return [_total_sum(t, g, r, q, place) for t, g, r, q in zip(ts, grads, from_sibling, from_chips)]


_EARLY = (1, 2, 3, 4)
_W_IN = (0,)
_SMALL_RED = (5,)


def _adamw_math(w, g, m, v):
    m = ADAM_B1 * m + (1.0 - ADAM_B1) * g
    v = ADAM_B2 * v + (1.0 - ADAM_B2) * (g * g)
    m_hat = m / (1.0 - ADAM_B1 ** ADAM_STEP)
    v_hat = v / (1.0 - ADAM_B2 ** ADAM_STEP)
    return -ADAM_LR * (m_hat / (jnp.sqrt(v_hat) + ADAM_EPS) + ADAM_WD * w), m, v


def _adamw(w, g, m, v, name):
    W = w.shape[1]
    return _rowwise(_adamw_math, [(a, 0, W) for a in (w, g, m, v)], [], [(W, F32)] * 3, tr=128, name=name)


def _adamw_whole(ws, gs, ms, vs, name):
    n = len(ws)

    def body(*refs):
        ins, outs = refs[:4 * n], refs[4 * n:]
        for i in range(n):
            res = _adamw_math(*[ins[j * n + i][...] for j in range(4)])
            for j in range(3):
                outs[j * n + i][...] = res[j]

    res = pl.pallas_call(
        body, name=name,
        out_shape=[jax.ShapeDtypeStruct(a.shape, F32) for a in ws] * 3,
        compiler_params=pltpu.CompilerParams(vmem_limit_bytes=VMEM_LIMIT_V7X),
    )(*ws, *gs, *ms, *vs)
    return res[:n], res[n:2 * n], res[2 * n:]


def _chunks(arr, off, width, w=512):
    return [(arr, off + i * w, w) for i in range(width // w)]


def _cat(vs):
    return jnp.concatenate(vs, axis=1)


def _forward_backward(x, p_b, tgt, shards, small):
    L = x.shape[0]
    w_in, = _gather_weights([0], shards[:1])
    row = lambda v: v.reshape(1, -1)
    g_mix, g_ple, g_fin = row(small["norm_mix"]), row(small["norm_ple"]), row(small["norm_final"])
    gq, gk, b_glu = row(small["q_norm"]), row(small["k_norm"]), row(small["b_glu"])
    cos, sin = _rope_tables(L)

    hn_b, = _rowwise(lambda x, g: x * _rms(x) * g, [(x, 0, D_MODEL)], [g_mix], [(D_MODEL, BF16)], name="norm_mix")
    z = _matmul(hn_b, w_in, name="mm_in")
    qr, kr, vb = _attn_prep(z, gq, gk, cos, sin)
    o, lse, w_glu, w_out, w_pg, w_pp = _attn_fwd(qr, kr, vb, [1, 2, 3, 4], shards[1:])

    ssm_names = ("ssm_a_re", "ssm_a_im", "ssm_log_dt", "ssm_b_re", "ssm_b_im")
    (lre, lim, bre, bim), disc_vjp = jax.vjp(_ssm_disc, *[small[n][0] for n in ssm_names])
    ssm = (_to_pairs(jnp.swapaxes(jnp.stack([bre, bim], axis=2), -1, -2)),
           _to_pairs(jnp.stack([small["ssm_c_re"][0], -small["ssm_c_im"][0]], axis=2)),
           _to_lam(lre), _to_lam(lim), row(small["ssm_d"]))
    u_p = _seg_perm(z[:, Z_U:Z_U + D_SSM])
    y_p, ssm_entry = _ssm_fwd(u_p, *ssm)
    y_s = _seg_unperm(y_p)
    ge_b, = _rowwise(_gelu, [(y_s, 0, D_SSM)], [], [(D_SSM, BF16)], name="gelu")
    glu = _matmul(ge_b, w_glu, name="mm_glu")

    def merge(ga0, ga1, a, b, gs0, gs1, o, bias):
        sa, _ = _silu_and_grad(_cat([ga0, ga1]))
        ss, _ = _silu_and_grad(_cat([gs0, gs1]))
        y2 = (a + bias[:, :D_SSM]) * _sig(b + bias[:, D_SSM:])
        return _cat([o * sa, y2 * ss])
    merge_rows = _chunks(z, Z_GA, D_ATTN) + [(glu, 0, D_SSM), (glu, D_SSM, D_SSM)] + _chunks(z, Z_GS, D_SSM) + [(o, 0, D_ATTN)]
    cat_b, = _rowwise(merge, merge_rows, [b_glu], [(D_MODEL, BF16)], name="merge")
    t_out = _matmul(cat_b, w_out, name="mm_out")

    def resid(x, t, g):
        h1 = x + t
        return h1, h1 * _rms(h1) * g
    h1, hp_b = _rowwise(resid, [(x, 0, D_MODEL), (t_out, 0, D_MODEL)], [g_ple], [(D_MODEL, F32), (D_MODEL, BF16)], name="resid_norm")
    gl = _matmul(hp_b, w_pg, name="mm_ple_gate")
    pp = _matmul(p_b, w_pp, name="mm_ple_proj")

    def head(h1, gl, pp, tgt, g):
        gate = _sig(gl)
        h2 = h1 + gate * pp
        r = _rms(h2)
        n = h2 * r
        err = n * g - tgt
        dy = err * (1.0 / D_MODEL)
        dn = dy * g
        dh2 = r * (dn - n * jnp.mean(dn * n, axis=-1, keepdims=True))
        dgate = dh2 * pp
        return dh2, dh2 * gate, dgate * gate * (1.0 - gate), _colsum(dy * n), _colsum(0.5 * err * err * (1.0 / D_MODEL))
    dh2, dpp_b, dgl_b, dg_fin, loss_cols = _rowwise(
        head, [(a, 0, D_MODEL) for a in (h1, gl, pp, tgt)], [g_fin],
        [(D_MODEL, F32), (D_MODEL, BF16), (D_MODEL, BF16)], [(1, D_MODEL), (1, D_MODEL)], name="loss_head")

    dw_pp = _matmul(p_b, dpp_b, ta=True, name="mm_d_w_ple_proj")
    dw_pg = _matmul(hp_b, dgl_b, ta=True, name="mm_d_w_ple_gate")
    dhp = _matmul(dgl_b, w_pg, tb=True, name="mm_d_hp")

    def resid_bwd(dhp, h1, dh2, g):
        dx, dg = _rms_bwd(h1, g, dhp)
        dh1 = dh2 + dx
        return dh1, dh1, _colsum(dg)
    dh1, dh1_b, dg_ple = _rowwise(resid_bwd, [(a, 0, D_MODEL) for a in (dhp, h1, dh2)], [g_ple],
                                  [(D_MODEL, F32), (D_MODEL, BF16)], [(1, D_MODEL)], name="resid_norm_bwd")
    dw_out = _matmul(cat_b, dh1_b, ta=True, name="mm_d_w_out")
    dcat = _matmul(dh1_b, w_out, tb=True, name="mm_d_cat")

    def merge_bwd(dya, dys, ga0, ga1, a, b, gs0, gs1, o, bias):
        ga, gs = _cat([ga0, ga1]), _cat([gs0, gs1])
        sa, dsa = _silu_and_grad(ga)
        ss, dss = _silu_and_grad(gs)
        a, sb = a + bias[:, :D_SSM], _sig(b + bias[:, D_SSM:])
        dy2 = dys * ss
        dglu = _cat([dy2 * sb, dy2 * a * sb * (1.0 - sb)])
        do = dya * sa
        lane = lax.broadcasted_iota(jnp.int32, (do.shape[0], HEAD_DIM), 1)
        delta = sum(jnp.where(lane == i, jnp.sum(h, axis=1, keepdims=True), 0.0)
                    for i, h in enumerate(_heads(do * o)))
        return do, dya * o * dsa, dys * (a * sb) * dss, dglu, delta, _colsum(dglu)
    do_b, dga_b, dgs_b, dglu_b, delta, db_glu = _rowwise(
        merge_bwd, [(dcat, 0, D_ATTN), (dcat, D_ATTN, D_SSM)] + merge_rows, [b_glu],
        [(D_ATTN, BF16), (D_ATTN, BF16), (D_SSM, BF16), (2 * D_SSM, BF16), (HEAD_DIM, F32)], [(1, 2 * D_SSM)], name="merge_bwd")
    dw_glu = _matmul(ge_b, dglu_b, ta=True, name="mm_d_w_glu")
    dge = _matmul(dglu_b, w_glu, tb=True, name="mm_d_ge")
    dy_s, = _rowwise(lambda dge, y: dge * _gelu_grad(y), [(dge, 0, D_SSM), (y_s, 0, D_SSM)], [], [(D_SSM, F32)], name="gelu_bwd")
    du_p, dwb, dwc, dare, daim, d_ssm_d = _ssm_bwd(u_p, _seg_perm(dy_s), ssm_entry, *ssm)
    dcc, dbb = _from_pairs(dwc), jnp.swapaxes(_from_pairs(dwb), -1, -2)
    dc_re, dc_im = dcc[:, :, 0], -dcc[:, :, 1]
    da_re, da_im, dlog_dt, db_re, db_im = disc_vjp((_from_lam(dare), _from_lam(daim), dbb[:, :, 0], dbb[:, :, 1]))

    place = _grad_place()
    early_grads = [dw_glu, dw_out, dw_pg, dw_pp]
    early_sib, early_sums = _reduce_begin(_EARLY, early_grads, place, "early")
    dqr, dkr, dv, *early_chips = _attn_bwd(qr, kr, kr.T, vb, do_b, lse.reshape(N_HEADS, 1, L), delta[:, :N_HEADS].T.reshape(N_HEADS, 1, L),
                                           _EARLY, early_sums)
    early_totals = _reduce_end(_EARLY, early_grads, early_sib, early_chips, place)
    dq_b, dk_b, dgq, dgk = _attn_prep_bwd(dqr, dkr, z, gq, gk, cos, sin)
    dz_b = _cat([dq_b, dk_b, dv.astype(BF16), dga_b, _seg_unperm(du_p).astype(BF16), dgs_b])
    dw_in = _matmul(hn_b, dz_b, ta=True, name="mm_d_w_in")
    w_in_sib, w_in_sums = _reduce_begin(_W_IN, [dw_in], place, "w_in")
    dhn, *w_in_chips = _matmul(dz_b, w_in, tb=True, name="mm_d_hn", exchange=(_W_IN, w_in_sums))
    w_in_total, = _reduce_end(_W_IN, [dw_in], w_in_sib, w_in_chips, place)

    def norm_bwd(dhn, x, dh1, g):
        dx, dg = _rms_bwd(x, g, dhn)
        return dh1 + dx, _colsum(dg)
    grad_x, dg_mix = _rowwise(norm_bwd, [(a, 0, D_MODEL) for a in (dhn, x, dh1)], [g_mix], [(D_MODEL, F32)], [(1, D_MODEL)],
                              name="norm_mix_bwd")

    small_grads = {"norm_mix": dg_mix, "q_norm": dgq, "k_norm": dgk, "ssm_a_re": da_re, "ssm_a_im": da_im, "ssm_log_dt": dlog_dt,
                   "ssm_b_re": db_re, "ssm_b_im": db_im, "ssm_c_re": dc_re, "ssm_c_im": dc_im, "ssm_d": d_ssm_d,
                   "b_glu": db_glu, "norm_ple": dg_ple, "norm_final": dg_fin}
    return jnp.sum(loss_cols), grad_x, [w_in_total] + early_totals, small_grads, place


_SMALL = ("norm_mix", "q_norm", "k_norm", "ssm_a_re", "ssm_a_im", "ssm_log_dt", "ssm_b_re", "ssm_b_im", "ssm_c_re", "ssm_c_im",
          "ssm_d", "b_glu", "norm_ple", "norm_final")
_WEIGHTS = ("norm_mix", "w_in", "q_norm", "k_norm", "ssm_a_re", "ssm_a_im", "ssm_log_dt", "ssm_b_re", "ssm_b_im", "ssm_c_re",
            "ssm_c_im", "ssm_d", "w_glu", "b_glu", "w_out", "norm_ple", "w_ple_gate", "w_ple_proj", "norm_final")


_SMALL_ADAMW_GROUPS = (("ssm_b_re",), ("ssm_b_im",), ("ssm_c_re", "ssm_c_im"),
                       ("norm_mix", "q_norm", "k_norm", "ssm_a_re", "ssm_a_im", "ssm_log_dt", "ssm_d", "b_glu", "norm_ple",
                        "norm_final"))


def _flat_small(d):
    flat = jnp.concatenate([d[n].reshape(-1).astype(F32) for n in _SMALL])
    return jnp.pad(flat, (0, N_SMALL - flat.shape[0]))


def _split_small(flat, like):
    out, off = {}, 0
    for n in _SMALL:
        sz = math.prod(like[n].shape)
        out[n] = flat[off:off + sz].reshape(like[n].shape)
        off += sz
    return out


def kernel(x, p, norm_mix, w_in, q_norm, k_norm, ssm_a_re, ssm_a_im, ssm_log_dt, ssm_b_re, ssm_b_im, ssm_c_re, ssm_c_im, ssm_d, w_glu, b_glu, w_out, norm_ple, w_ple_gate, w_ple_proj, norm_final, loss_target, m_norm_mix, m_w_in, m_q_norm, m_k_norm, m_ssm_a_re, m_ssm_a_im, m_ssm_log_dt, m_ssm_b_re, m_ssm_b_im, m_ssm_c_re, m_ssm_c_im, m_ssm_d, m_w_glu, m_b_glu, m_w_out, m_norm_ple, m_w_ple_gate, m_w_ple_proj, m_norm_final, v_norm_mix, v_w_in, v_q_norm, v_k_norm, v_ssm_a_re, v_ssm_a_im, v_ssm_log_dt, v_ssm_b_re, v_ssm_b_im, v_ssm_c_re, v_ssm_c_im, v_ssm_d, v_w_glu, v_b_glu, v_w_out, v_norm_ple, v_w_ple_gate, v_w_ple_proj, v_norm_final):
    w = dict(norm_mix=norm_mix, w_in=w_in, q_norm=q_norm, k_norm=k_norm, ssm_a_re=ssm_a_re, ssm_a_im=ssm_a_im,
             ssm_log_dt=ssm_log_dt, ssm_b_re=ssm_b_re, ssm_b_im=ssm_b_im, ssm_c_re=ssm_c_re, ssm_c_im=ssm_c_im, ssm_d=ssm_d,
             w_glu=w_glu, b_glu=b_glu, w_out=w_out, norm_ple=norm_ple, w_ple_gate=w_ple_gate, w_ple_proj=w_ple_proj,
             norm_final=norm_final)
    m = dict(norm_mix=m_norm_mix, w_in=m_w_in, q_norm=m_q_norm, k_norm=m_k_norm, ssm_a_re=m_ssm_a_re, ssm_a_im=m_ssm_a_im,
             ssm_log_dt=m_ssm_log_dt, ssm_b_re=m_ssm_b_re, ssm_b_im=m_ssm_b_im, ssm_c_re=m_ssm_c_re, ssm_c_im=m_ssm_c_im,
             ssm_d=m_ssm_d, w_glu=m_w_glu, b_glu=m_b_glu, w_out=m_w_out, norm_ple=m_norm_ple, w_ple_gate=m_w_ple_gate,
             w_ple_proj=m_w_ple_proj, norm_final=m_norm_final)
    v = dict(norm_mix=v_norm_mix, w_in=v_w_in, q_norm=v_q_norm, k_norm=v_k_norm, ssm_a_re=v_ssm_a_re, ssm_a_im=v_ssm_a_im,
             ssm_log_dt=v_ssm_log_dt, ssm_b_re=v_ssm_b_re, ssm_b_im=v_ssm_b_im, ssm_c_re=v_ssm_c_re, ssm_c_im=v_ssm_c_im,
             ssm_d=v_ssm_d, w_glu=v_w_glu, b_glu=v_b_glu, w_out=v_w_out, norm_ple=v_norm_ple, w_ple_gate=v_w_ple_gate,
             w_ple_proj=v_w_ple_proj, norm_final=v_norm_final)
    big_names = [n for n, _, _, _ in _BIG]

    small = {n: w[n] for n in _SMALL}
    loss_part, grad_x, big_totals, small_grads, place = _forward_backward(
        x[0], p[0, 0].astype(BF16), loss_target[0], [w[n][0].astype(BF16) for n in big_names], small)
    loss = lax.psum(loss_part, ("x", "y", "c"))

    small_flat = [_flat_small(small_grads).reshape(8 * SMALL_ROWS, SMALL_W)]
    small_sib, small_sums = _reduce_begin(_SMALL_RED, small_flat, place, "small")
    small_total = _reduce_end(_SMALL_RED, small_flat, small_sib, _grad_chip_exchange(_SMALL_RED, small_sums), place)
    *big_red, small_red = _grad_final_exchange(big_totals + small_total)
    grads = _split_small(small_red.reshape(-1), w)
    delta, new_m, new_v = {}, {}, {}
    for n, g in zip(big_names, big_red):
        grads[n] = g[None]
        d_, m_, v_ = _adamw(w[n][0], g, m[n][0], v[n][0], "adamw_" + n)
        delta[n], new_m[n], new_v[n] = d_[None], m_[None], v_[None]
    at_least_2d = lambda a: a.reshape(1, -1) if a.ndim == 1 else a
    for i, names in enumerate(_SMALL_ADAMW_GROUPS):
        d_, m_, v_ = _adamw_whole(*[[at_least_2d(src[n]) for n in names] for src in (w, grads, m, v)], "adamw_small_%d" % i)
        for j, n in enumerate(names):
            delta[n], new_m[n], new_v[n] = (a[j].reshape(w[n].shape) for a in (d_, m_, v_))
    return (loss, grad_x[None], *[grads[n] for n in _WEIGHTS], *[delta[n] for n in _WEIGHTS],
            *[new_m[n] for n in _WEIGHTS], *[new_v[n] for n in _WEIGHTS])
```

```python
import functools
import math

import jax
import jax.numpy as jnp
import numpy as np
from jax import lax
from jax.experimental import pallas as pl
from jax.experimental.pallas import tpu as pltpu

D_MODEL = 2048
GRID_W = 64
PLE_DIM = 256
D_ATTN = 1024
N_HEADS = 8
N_KV = 2
HEAD_DIM = 128
ROPE_THETA = 10000.0
D_SSM = 1024
SSM_H = 16
SSM_G = 64
SSM_P = 64
D_KV = N_KV * HEAD_DIM
D_IN = 2 * D_ATTN + 2 * D_KV + 2 * D_SSM
EPS = 1e-6
Z_Q, Z_K, Z_V, Z_GA, Z_U, Z_GS = 0, 1024, 1280, 1536, 2560, 3584

ADAM_LR, ADAM_B1, ADAM_B2, ADAM_EPS, ADAM_WD, ADAM_STEP = 0.001, 0.9, 0.999, 1e-08, 0.01, 10

N_CHIPS = 4
VMEM_LIMIT_V7X = 56 * 1024 * 1024
F32 = jnp.float32
BF16 = jnp.bfloat16


def _params(sem, vmem=VMEM_LIMIT_V7X):
    return pltpu.CompilerParams(dimension_semantics=sem, vmem_limit_bytes=vmem)


def _matmul(a, b, *, ta=False, tb=False, out_dtype=F32, tm=1024, tn=None, name, exchange=None):
    M, K = (a.shape[1], a.shape[0]) if ta else a.shape
    N = b.shape[0] if tb else b.shape[1]
    if tn is None:
        tn = 1024 if (N % 1024 == 0 and K <= 4096) else 512
    tm, tn = min(tm, M), min(tn, N)
    assert M % tm == 0 and N % tn == 0, (name, M, N, K)
    dims = (((0 if ta else 1,), (1 if tb else 0,)), ((), ()))
    ex_ts, ex_sums = exchange if exchange else ((), ())
    n_ex = len(ex_ts)
    gm, gn = M // tm, N // tn

    def body(a_ref, b_ref, *rest):
        o_ref = rest[n_ex]
        if n_ex:
            ex = _ChipExchange(ex_ts, rest[:n_ex], rest[n_ex + 1:2 * n_ex + 1], rest[2 * n_ex + 1:])
            step = pl.program_id(0) * gn + pl.program_id(1)
            pl.when(step == 0)(ex.start)
        o_ref[...] = lax.dot_general(a_ref[...], b_ref[...], dims, preferred_element_type=F32).astype(o_ref.dtype)
        if n_ex:
            pl.when(step == gm * gn - 1)(ex.finish)

    a_spec = pl.BlockSpec((K, tm), lambda i, j: (0, i)) if ta else pl.BlockSpec((tm, K), lambda i, j: (i, 0))
    b_spec = pl.BlockSpec((tn, K), lambda i, j: (j, 0)) if tb else pl.BlockSpec((K, tn), lambda i, j: (0, j))
    res = pl.pallas_call(
        body, name=name,
        out_shape=[jax.ShapeDtypeStruct((M, N), out_dtype)] + (_ChipExchange.out_shape(ex_ts) if n_ex else []),
        grid=(gm, gn),
        in_specs=[a_spec, b_spec] + [_ANY] * n_ex,
        out_specs=[pl.BlockSpec((tm, tn), lambda i, j: (i, j))] + [_ANY] * n_ex,
        scratch_shapes=_ChipExchange.scratch(ex_ts) if n_ex else [],
        compiler_params=_params(("arbitrary", "arbitrary") if n_ex else ("parallel", "parallel")),
    )(a, b, *ex_sums)
    return res if n_ex else res[0]


def _rowwise(fn, rows, consts, outs, accs=(), *, tr=256, name):
    L = rows[0][0].shape[0]
    tr = math.gcd(tr, L)
    assert tr % 8 == 0 or tr == L, (name, L, tr)
    n_in, n_c, n_o, n_a = len(rows), len(consts), len(outs), len(accs)

    def body(*refs):
        ins = [r[...] for r in refs[:n_in + n_c]]
        res = fn(*ins)
        if not isinstance(res, (tuple, list)):
            res = (res,)
        o_refs = refs[n_in + n_c:n_in + n_c + n_o]
        a_refs = refs[n_in + n_c + n_o:]
        for r, v in zip(o_refs, res[:n_o]):
            r[...] = v.astype(r.dtype)
        if n_a:
            first = pl.program_id(0) == 0

            @pl.when(first)
            def _():
                for r, v in zip(a_refs, res[n_o:]):
                    r[...] = v.astype(F32)

            @pl.when(jnp.logical_not(first))
            def _():
                for r, v in zip(a_refs, res[n_o:]):
                    r[...] += v.astype(F32)

    in_specs = []
    for arr, off, w in rows:
        assert off % w == 0, (name, off, w)
        in_specs.append(pl.BlockSpec((tr, w), functools.partial(lambda i, c: (i, c), c=off // w)))
    for c in consts:
        in_specs.append(pl.BlockSpec(c.shape, lambda i: (0, 0)))
    out_shape = [jax.ShapeDtypeStruct((L, w), dt) for w, dt in outs] + [jax.ShapeDtypeStruct(s, F32) for s in accs]
    out_specs = [pl.BlockSpec((tr, w), lambda i: (i, 0)) for w, _ in outs] + [pl.BlockSpec(s, lambda i: (0, 0)) for s in accs]
    res = pl.pallas_call(
        body, name=name,
        out_shape=out_shape,
        grid=(L // tr,),
        in_specs=in_specs,
        out_specs=out_specs,
        compiler_params=_params(("arbitrary",) if n_a else ("parallel",)),
    )(*[r[0] for r in rows], *consts)
    return res


def _sig(x):
    return jax.nn.sigmoid(x)


def _silu_and_grad(x):
    s = _sig(x)
    return x * s, s * (1.0 + x * (1.0 - s))


_GELU_C = math.sqrt(2.0 / math.pi)


def _gelu(x):
    return 0.5 * x * (1.0 + jnp.tanh(_GELU_C * (x + 0.044715 * x * x * x)))


def _gelu_grad(x):
    t = jnp.tanh(_GELU_C * (x + 0.044715 * x * x * x))
    return 0.5 * (1.0 + t) + 0.5 * x * (1.0 - t * t) * _GELU_C * (1.0 + 3.0 * 0.044715 * x * x)


def _rms(x):
    return lax.rsqrt(jnp.mean(x * x, axis=-1, keepdims=True) + EPS)


def _rms_bwd(x, g, dy):
    r = _rms(x)
    n = x * r
    dn = dy * g
    return r * (dn - n * jnp.mean(dn * n, axis=-1, keepdims=True)), dy * n


def _colsum(v):
    return jnp.sum(v, axis=0, keepdims=True)


def _rope_partner(x):
    lane = lax.broadcasted_iota(jnp.int32, x.shape, x.ndim - 1)
    return jnp.where(lane % 64 < 32, pltpu.roll(x, 96, x.ndim - 1), pltpu.roll(x, 32, x.ndim - 1))


def _rope_tables(L):
    rows_n = L // GRID_W
    rows = jnp.repeat(jnp.arange(rows_n), GRID_W).astype(F32)
    cols = jnp.tile(jnp.arange(GRID_W), rows_n).astype(F32)
    n_freq = HEAD_DIM // 4
    inv_freq = ROPE_THETA ** (-jnp.arange(n_freq, dtype=F32) / n_freq)
    ar, ac = rows[:, None] * inv_freq[None, :], cols[:, None] * inv_freq[None, :]
    cos = jnp.concatenate([jnp.cos(ar), jnp.cos(ar), jnp.cos(ac), jnp.cos(ac)], axis=-1)
    sin = jnp.concatenate([-jnp.sin(ar), jnp.sin(ar), -jnp.sin(ac), jnp.sin(ac)], axis=-1)
    return cos, sin


def _heads(v):
    return [v[:, h * HEAD_DIM:(h + 1) * HEAD_DIM] for h in range(v.shape[1] // HEAD_DIM)]


def _attn_prep(z, q_norm, k_norm, cos, sin):
    def fn(q, k, v, cos, sin, gq, gk):
        def one(xh, g):
            xn = xh * _rms(xh) * g
            return xn * cos + _rope_partner(xn) * sin
        qr = jnp.concatenate([one(h, gq) for h in _heads(q)], axis=1)
        kr = jnp.concatenate([one(h, gk) for h in _heads(k)], axis=1)
        return qr, kr, v
    return _rowwise(fn, [(z, Z_Q, D_ATTN), (z, Z_K, D_KV), (z, Z_V, D_KV), (cos, 0, HEAD_DIM), (sin, 0, HEAD_DIM)],
                    [q_norm, k_norm], [(D_ATTN, BF16), (D_KV, BF16), (D_KV, BF16)], name="attn_prep")


def _attn_prep_bwd(dqr, dkr, z, q_norm, k_norm, cos, sin):
    def fn(dqr, dkr, q, k, cos, sin, gq, gk):
        def one(dyh, xh, g):
            dn = dyh * cos + _rope_partner(dyh * sin)
            return _rms_bwd(xh, g, dn)
        rq = [one(a, b, gq) for a, b in zip(_heads(dqr), _heads(q))]
        rk = [one(a, b, gk) for a, b in zip(_heads(dkr), _heads(k))]
        dq = jnp.concatenate([r[0] for r in rq], axis=1)
        dk = jnp.concatenate([r[0] for r in rk], axis=1)
        return dq, dk, _colsum(sum(r[1] for r in rq)), _colsum(sum(r[1] for r in rk))
    return _rowwise(fn, [(dqr, 0, D_ATTN), (dkr, 0, D_KV), (z, Z_Q, D_ATTN), (z, Z_K, D_KV), (cos, 0, HEAD_DIM), (sin, 0, HEAD_DIM)],
                    [q_norm, k_norm], [(D_ATTN, BF16), (D_KV, BF16)], [(1, HEAD_DIM), (1, HEAD_DIM)], name="attn_prep_bwd")


_QK_T = (((1,), (1,)), ((), ()))
_TA = (((0,), (0,)), ((), ()))
_REP = N_HEADS // N_KV


_EXP2_SCALE = HEAD_DIM ** -0.5 * math.log2(math.e)
ATTN_FWD_KEY_CHUNKS = 4
ATTN_BWD_KEY_CHUNKS = 8


def _attn_fwd(qr, kr, vb, g_ts, g_shards, *, tq=1024):
    L = qr.shape[0]
    tq = min(tq, L)
    kc = L // ATTN_FWD_KEY_CHUNKS
    n_g = len(g_ts)
    grid = (N_HEADS, L // tq)
    steps = grid[0] * grid[1]

    def body(q_ref, k_ref, v_ref, *rest):
        o_ref, lse_ref = rest[n_g:n_g + 2]
        g = _Gather(g_ts, rest[:n_g], rest[n_g + 2:2 * n_g + 2], rest[2 * n_g + 2:3 * n_g + 2], rest[3 * n_g + 2:])
        step = pl.program_id(0) * grid[1] + pl.program_id(1)
        pl.when(step == 0)(g.start)
        pl.when(step == (3 * steps) // 4)(g.forward)
        q = q_ref[...]
        m = jnp.full((tq, 1), -jnp.inf, F32)
        l = jnp.zeros((tq, 1), F32)
        o = jnp.zeros((tq, HEAD_DIM), F32)
        for c in range(ATTN_FWD_KEY_CHUNKS):
            ks = slice(c * kc, (c + 1) * kc)
            s = lax.dot_general(q, k_ref[ks, :], _QK_T, preferred_element_type=F32)
            m_new = jnp.maximum(m, jnp.max(s, axis=1, keepdims=True))
            a = jnp.exp2((m - m_new) * _EXP2_SCALE)
            p = jnp.exp2((s - m_new) * _EXP2_SCALE)
            l = a * l + jnp.sum(p, axis=1, keepdims=True)
            o = a * o + jnp.dot(p.astype(BF16), v_ref[ks, :], preferred_element_type=F32)
            m = m_new
        o_ref[...] = o * (1.0 / l)
        lse_ref[...] = m * _EXP2_SCALE + jnp.log2(l)
        pl.when(step == steps - 1)(g.finish)

    kv = pl.BlockSpec((L, HEAD_DIM), lambda h, i: (0, h // _REP))
    return pl.pallas_call(
        body, name="attn_fwd",
        out_shape=[jax.ShapeDtypeStruct((L, D_ATTN), F32), jax.ShapeDtypeStruct((N_HEADS, L, 1), F32)] +
                  [jax.ShapeDtypeStruct(_BIG[t][1], BF16) for t in g_ts],
        grid=grid,
        in_specs=[pl.BlockSpec((tq, HEAD_DIM), lambda h, i: (i, h)), kv, kv] + [_ANY] * n_g,
        out_specs=[pl.BlockSpec((tq, HEAD_DIM), lambda h, i: (i, h)),
                   pl.BlockSpec((None, tq, 1), lambda h, i: (h, i, 0))] + [_ANY] * n_g,
        scratch_shapes=_Gather.scratch(g_shards),
        compiler_params=_params(("arbitrary", "arbitrary")),
    )(qr, kr, vb, *g_shards)


def _attn_bwd(qr, kr, k_t, vb, do, lse, delta, ex_ts, ex_sums, *, tq=1024):
    L = qr.shape[0]
    tq = min(tq, L)
    scale = HEAD_DIM ** -0.5
    kc = L // ATTN_BWD_KEY_CHUNKS
    n_ex = len(ex_ts)
    grid = (N_KV, _REP, L // tq)

    def body(q_ref, k_ref, kt_ref, v_ref, do_ref, lse_ref, delta_ref, *rest):
        dq_ref, dk_ref, dv_ref = rest[n_ex:n_ex + 3]
        ex = _ChipExchange(ex_ts, rest[:n_ex], rest[n_ex + 3:2 * n_ex + 3], rest[2 * n_ex + 3:])
        step = (pl.program_id(0) * grid[1] + pl.program_id(1)) * grid[2] + pl.program_id(2)
        pl.when(step == 0)(ex.start)

        @pl.when((pl.program_id(1) == 0) & (pl.program_id(2) == 0))
        def _():
            dk_ref[...] = jnp.zeros_like(dk_ref)
            dv_ref[...] = jnp.zeros_like(dv_ref)

        q, do, lse, delta = q_ref[...], do_ref[...], lse_ref[...], delta_ref[...]
        dq_t = 0.0
        for c in range(ATTN_BWD_KEY_CHUNKS):
            ks = slice(c * kc, (c + 1) * kc)
            st = lax.dot_general(k_ref[ks, :], q, _QK_T, preferred_element_type=F32)
            p = jnp.exp2(st * _EXP2_SCALE - lse)
            dv_ref[ks, :] += jnp.dot(p.astype(BF16), do, preferred_element_type=F32)
            dp = lax.dot_general(v_ref[ks, :], do, _QK_T, preferred_element_type=F32)
            ds = (p * (dp - delta) * scale).astype(BF16)
            dk_ref[ks, :] += jnp.dot(ds, q, preferred_element_type=F32)
            dq_t = dq_t + jnp.dot(kt_ref[:, ks], ds, preferred_element_type=F32)
        dq_ref[...] = dq_t.T
        pl.when(step == grid[0] * grid[1] * grid[2] - 1)(ex.finish)

    head = lambda g, r, i: (i, g * _REP + r)
    kv = pl.BlockSpec((L, HEAD_DIM), lambda g, r, i: (0, g))
    per_query = pl.BlockSpec((None, 1, tq), lambda g, r, i: (g * _REP + r, 0, i))
    return pl.pallas_call(
        body, name="attn_bwd",
        out_shape=[jax.ShapeDtypeStruct((L, D_ATTN), F32), jax.ShapeDtypeStruct((L, D_KV), F32), jax.ShapeDtypeStruct((L, D_KV), F32)] +
                  _ChipExchange.out_shape(ex_ts),
        grid=grid,
        in_specs=[pl.BlockSpec((tq, HEAD_DIM), head), kv,
                  pl.BlockSpec((HEAD_DIM, L), lambda g, r, i: (g, 0)), kv,
                  pl.BlockSpec((tq, HEAD_DIM), head), per_query, per_query] + [_ANY] * n_ex,
        out_specs=[pl.BlockSpec((tq, HEAD_DIM), head), kv, kv] + [_ANY] * n_ex,
        scratch_shapes=_ChipExchange.scratch(ex_ts),
        compiler_params=_params(("arbitrary", "arbitrary", "arbitrary")),
    )(qr, kr, k_t, vb, do, lse, delta, *ex_sums)


SSM_BLK = 8
SSM_NB = SSM_G // SSM_BLK
SSM_SEG = 8
SSM_UNROLL = 4


def _unrolled_loop(n, step, carry):
    u = SSM_UNROLL

    def trip(i, c):
        for j in range(u):
            c = step(i * u + j, c)
        return c
    carry = lax.fori_loop(0, n // u, trip, carry)
    for t in range(n - n % u, n):
        carry = step(jnp.int32(t), carry)
    return carry


def _cplx_pow2(a, b, n):
    for _ in range(int(math.log2(n))):
        a, b = a * a - b * b, 2.0 * a * b
    return a, b


def _seg_scan(ref, a, b, T, reverse, entry=None, tap=None):
    npair = len(a)
    zero = jnp.zeros((SSM_SEG, 128), F32)

    def make_step(store, tap_fn=None):
        def step(t, carry):
            lt = (T - 1 - t) if reverse else t
            row = pl.multiple_of(lt * SSM_SEG, SSM_SEG)
            blk = ref[pl.ds(row, SSM_SEG), :]
            new = []
            for q in range(npair):
                re, im = carry[2 * q], carry[2 * q + 1]
                nre = a[q] * re - b[q] * im + blk[:, q * 256:q * 256 + 128]
                nim = a[q] * im + b[q] * re + blk[:, q * 256 + 128:q * 256 + 256]
                new += [nre, nim]
            if store:
                ref[pl.ds(row, SSM_SEG), :] = jnp.concatenate(new, axis=1)
            extra = carry[2 * npair:]
            return tuple(new) + (tuple(tap_fn(lt, new, extra)) if tap_fn else tuple(extra))
        return step

    def second_pass(init):
        if tap is None:
            _unrolled_loop(T, make_step(True), tuple(init))
            return init
        carry = _unrolled_loop(T - 1, make_step(True, tap[0]), tuple(init) + tuple(tap[2]))
        carry = make_step(True, tap[1])(jnp.int32(T - 1), carry)
        return init, carry[2 * npair:]

    if entry is not None:
        return second_pass(entry)
    ends = _unrolled_loop(T, make_step(False), (zero,) * (2 * npair))
    sub = lax.broadcasted_iota(jnp.int32, (SSM_SEG, 128), 0)
    keep = (sub != SSM_SEG - 1) if reverse else (sub != 0)
    shift = (SSM_SEG - 1) if reverse else 1
    init = []
    for q in range(npair):
        pa, pb = _cplx_pow2(a[q], b[q], T)
        xr, xi = zero, zero
        for _ in range(SSM_SEG - 1):
            fr = ends[2 * q] + pa * xr - pb * xi
            fi = ends[2 * q + 1] + pa * xi + pb * xr
            xr = jnp.where(keep, pltpu.roll(fr, shift, 0), 0.0)
            xi = jnp.where(keep, pltpu.roll(fi, shift, 0), 0.0)
        init += [xr, xi]
    return second_pass(init)


SSM_BW = SSM_BLK * SSM_H
SSM_SW = SSM_BLK * 2 * SSM_P
SSM_NPAIR = SSM_BLK // 2


def _seg_perm(a):
    L, C = a.shape
    return a.reshape(SSM_SEG, L // SSM_SEG, C).transpose(1, 0, 2).reshape(L, C)


def _seg_unperm(a):
    L, C = a.shape
    return a.reshape(L // SSM_SEG, SSM_SEG, C).transpose(1, 0, 2).reshape(L, C)


def _lam_rows(are_ref, aim_ref, d):
    a = [jnp.broadcast_to(are_ref[d, j:j + 1, :], (SSM_SEG, 128)) for j in range(SSM_NPAIR)]
    b = [jnp.broadcast_to(aim_ref[d, j:j + 1, :], (SSM_SEG, 128)) for j in range(SSM_NPAIR)]
    return a, b


_PAIR_SPEC = pl.BlockSpec((None, 2, SSM_NPAIR, 2, 2 * SSM_H, 128), lambda g: (g, 0, 0, 0, 0, 0))


def _pair_window(j, r):
    return slice(j * 2 * SSM_H, (j + 1) * 2 * SSM_H), slice(j * 256 + r * 128, j * 256 + (r + 1) * 128)


def _expand_pairs(c_ref, dense_ref):
    dense_ref[...] = jnp.zeros_like(dense_ref)
    for d in range(2):
        for j in range(SSM_NPAIR):
            for r in range(2):
                rows, cols = _pair_window(j, r)
                dense_ref[d, rows, cols] = c_ref[d, j, r].astype(dense_ref.dtype)


def _ssm_fwd(u_p, wb, wc, are, aim, dvec):
    L = u_p.shape[0]
    T = L // SSM_SEG
    RC = min(512, L)

    def body(u_ref, wb_ref, wc_ref, are_ref, aim_ref, d_ref, y_ref, entry_ref, x_scr, wb_s, wc_s):
        _expand_pairs(wb_ref, wb_s)
        _expand_pairs(wc_ref, wc_s)
        y_ref[...] = u_ref[...] * d_ref[...]
        for d in range(2):
            def bu_chunk(c, _):
                rows = pl.ds(pl.multiple_of(c * RC, RC), RC)
                x_scr[rows, :] = jnp.dot(u_ref[rows, :].astype(BF16), wb_s[d], preferred_element_type=F32)
                return 0
            lax.fori_loop(0, L // RC, bu_chunk, 0)
            a, b = _lam_rows(are_ref, aim_ref, d)
            entry_ref[d] = jnp.concatenate(_seg_scan(x_scr, a, b, T, reverse=(d == 1)), axis=1)

            def y_chunk(c, _):
                rows = pl.ds(pl.multiple_of(c * RC, RC), RC)
                y_ref[rows, :] += lax.dot_general(x_scr[rows, :].astype(BF16), wc_s[d], _QK_T, preferred_element_type=F32)
                return 0
            lax.fori_loop(0, L // RC, y_chunk, 0)

    blk4 = lambda g: (g, 0, 0, 0)
    return pl.pallas_call(
        body, name="ssm_fwd",
        out_shape=[jax.ShapeDtypeStruct((L, D_SSM), F32), jax.ShapeDtypeStruct((SSM_NB, 2, SSM_SEG, SSM_SW), F32)],
        grid=(SSM_NB,),
        in_specs=[pl.BlockSpec((L, SSM_BW), lambda g: (0, g)), _PAIR_SPEC, _PAIR_SPEC,
                  pl.BlockSpec((None, 2, SSM_NPAIR, 128), blk4),
                  pl.BlockSpec((None, 2, SSM_NPAIR, 128), blk4),
                  pl.BlockSpec((1, SSM_BW), lambda g: (0, g))],
        out_specs=[pl.BlockSpec((L, SSM_BW), lambda g: (0, g)), pl.BlockSpec((None, 2, SSM_SEG, SSM_SW), blk4)],
        scratch_shapes=[pltpu.VMEM((L, SSM_SW), F32), pltpu.VMEM((2, SSM_BW, SSM_SW), BF16), pltpu.VMEM((2, SSM_BW, SSM_SW), BF16)],
        compiler_params=_params(("parallel",)),
    )(u_p, wb, wc, are, aim, dvec)


def _ssm_bwd(u_p, dy_p, entry, wb, wc, are, aim, dvec):
    L = u_p.shape[0]
    T = L // SSM_SEG
    RC = min(512, L)

    def lam_acc(acc, s, x):
        new = []
        for q in range(SSM_NPAIR):
            sr, si, xr, xi = s[2 * q], s[2 * q + 1], x[2 * q], x[2 * q + 1]
            new += [acc[2 * q] + sr * xr + si * xi, acc[2 * q + 1] + si * xr - sr * xi]
        return tuple(new)

    def body(u_ref, dy_ref, entry_ref, wb_ref, wc_ref, are_ref, aim_ref, d_ref,
             du_ref, dwb_ref, dwc_ref, dare_ref, daim_ref, dd_ref, x_scr, s_scr, wb_s, wc_s, dwb_s, dwc_s):
        _expand_pairs(wb_ref, wb_s)
        _expand_pairs(wc_ref, wc_s)
        du_ref[...] = dy_ref[...] * d_ref[...]
        dd_ref[...] = _colsum(dy_ref[...] * u_ref[...])
        dwb_s[...] = jnp.zeros_like(dwb_s)
        dwc_s[...] = jnp.zeros_like(dwc_s)
        for d in range(2):
            rev = d == 1

            def in_chunk(c, _):
                rows = pl.ds(pl.multiple_of(c * RC, RC), RC)
                x_scr[rows, :] = jnp.dot(u_ref[rows, :].astype(BF16), wb_s[d], preferred_element_type=F32)
                s_scr[rows, :] = jnp.dot(dy_ref[rows, :].astype(BF16), wc_s[d], preferred_element_type=F32)
                return 0
            lax.fori_loop(0, L // RC, in_chunk, 0)
            a, b = _lam_rows(are_ref, aim_ref, d)
            x_in = _seg_scan(x_scr, a, b, T, reverse=rev, entry=[entry_ref[d, :, q * 128:(q + 1) * 128] for q in range(2 * SSM_NPAIR)])

            def pair_with_row(lt, s_new, acc):
                xrow = pl.multiple_of((lt + 1 if rev else lt - 1) * SSM_SEG, SSM_SEG)
                xb = x_scr[pl.ds(xrow, SSM_SEG), :]
                return lam_acc(acc, s_new, [xb[:, i * 128:(i + 1) * 128] for i in range(2 * SSM_NPAIR)])

            zeros = (jnp.zeros((SSM_SEG, 128), F32),) * (2 * SSM_NPAIR)
            _, acc = _seg_scan(s_scr, a, [-v for v in b], T, reverse=not rev,
                               tap=(pair_with_row, lambda lt, s_new, acc: lam_acc(acc, s_new, x_in), zeros))
            for q in range(SSM_NPAIR):
                dare_ref[d, q:q + 1, :] = _colsum(acc[2 * q])
                daim_ref[d, q:q + 1, :] = _colsum(acc[2 * q + 1])

            def out_chunk(c, _):
                rows = pl.ds(pl.multiple_of(c * RC, RC), RC)
                xs, ss = x_scr[rows, :].astype(BF16), s_scr[rows, :].astype(BF16)
                uu, dd = u_ref[rows, :].astype(BF16), dy_ref[rows, :].astype(BF16)
                dwc_s[d] += lax.dot_general(dd, xs, _TA, preferred_element_type=F32)
                dwb_s[d] += lax.dot_general(uu, ss, _TA, preferred_element_type=F32)
                du_ref[rows, :] += lax.dot_general(ss, wb_s[d], _QK_T, preferred_element_type=F32)
                return 0
            lax.fori_loop(0, L // RC, out_chunk, 0)
            for j in range(SSM_NPAIR):
                for r in range(2):
                    rows, cols = _pair_window(j, r)
                    dwb_ref[d, j, r] = dwb_s[d, rows, cols]
                    dwc_ref[d, j, r] = dwc_s[d, rows, cols]

    blk4 = lambda g: (g, 0, 0, 0)
    chan = pl.BlockSpec((L, SSM_BW), lambda g: (0, g))
    par_specs = [_PAIR_SPEC, _PAIR_SPEC,
                 pl.BlockSpec((None, 2, SSM_NPAIR, 128), blk4),
                 pl.BlockSpec((None, 2, SSM_NPAIR, 128), blk4),
                 pl.BlockSpec((1, SSM_BW), lambda g: (0, g))]
    dense = lambda dt: pltpu.VMEM((2, SSM_BW, SSM_SW), dt)
    return pl.pallas_call(
        body, name="ssm_bwd",
        out_shape=[jax.ShapeDtypeStruct((L, D_SSM), F32),
                   jax.ShapeDtypeStruct(wb.shape, F32),
                   jax.ShapeDtypeStruct(wc.shape, F32),
                   jax.ShapeDtypeStruct((SSM_NB, 2, SSM_NPAIR, 128), F32),
                   jax.ShapeDtypeStruct((SSM_NB, 2, SSM_NPAIR, 128), F32),
                   jax.ShapeDtypeStruct((1, D_SSM), F32)],
        grid=(SSM_NB,),
        in_specs=[chan, chan, pl.BlockSpec((None, 2, SSM_SEG, SSM_SW), blk4)] + par_specs,
        out_specs=[chan] + par_specs,
        scratch_shapes=[pltpu.VMEM((L, SSM_SW), F32), pltpu.VMEM((L, SSM_SW), F32), dense(BF16), dense(BF16), dense(F32), dense(F32)],
        compiler_params=_params(("parallel",)),
    )(u_p, dy_p, entry, wb, wc, are, aim, dvec)


def _ssm_disc(a_re, a_im, log_dt, b_re, b_im):
    lam = lax.complex(jnp.minimum(a_re, -1e-4), a_im)
    dt = jnp.exp(log_dt)[..., None]
    lam_bar = jnp.exp(lam * dt)
    b_bar = ((lam_bar - 1.0) / lam)[..., None] * lax.complex(b_re, b_im)
    return jnp.real(lam_bar), jnp.imag(lam_bar), jnp.real(b_bar), jnp.imag(b_bar)


_EYE2 = np.eye(2, dtype=np.float32)[:, None, :, None]


def _to_pairs(t):
    t = t.reshape(2, SSM_NB, SSM_NPAIR, 2, 2, SSM_H, SSM_P).transpose(1, 0, 2, 4, 3, 5, 6)
    return (t[..., None, :] * _EYE2).reshape(SSM_NB, 2, SSM_NPAIR, 2, 2 * SSM_H, 2 * SSM_P)


def _from_pairs(c):
    t = jnp.sum(c.reshape(SSM_NB, 2, SSM_NPAIR, 2, 2, SSM_H, 2, SSM_P) * _EYE2, axis=6)
    return t.transpose(1, 0, 2, 4, 3, 5, 6).reshape(2, SSM_G, 2, SSM_H, SSM_P)


def _to_lam(v):
    return v.reshape(2, SSM_NB, SSM_NPAIR, 128).transpose(1, 0, 2, 3)


def _from_lam(v):
    return v.transpose(1, 0, 2, 3).reshape(2, SSM_G, SSM_P)


_MESH = pl.DeviceIdType.MESH
_ANY = pl.BlockSpec(memory_space=pl.ANY)
_BIG = (("w_in", (D_MODEL, D_IN), 1, D_IN // N_CHIPS),
        ("w_glu", (D_SSM, 2 * D_SSM), 1, 2 * D_SSM // N_CHIPS),
        ("w_out", (D_ATTN + D_SSM, D_MODEL), 0, (D_ATTN + D_SSM) // N_CHIPS),
        ("w_ple_gate", (D_MODEL, D_MODEL), 0, D_MODEL // N_CHIPS),
        ("w_ple_proj", (PLE_DIM, D_MODEL), 1, D_MODEL // N_CHIPS))


def _place():
    x, y, c = lax.axis_index("x"), lax.axis_index("y"), lax.axis_index("c")
    return x, y, c, [(1 - x, y), (x, 1 - y), (1 - x, 1 - y)]


class _Gather:
    def __init__(self, ts, srcs, dsts, stage, sems):
        self.ts, self.srcs, self.dsts, self.stage = ts, srcs, dsts, stage
        self.send_sems, self.recv_sems, self.fwd_send_sems, self.fwd_recv_sems, self.loc_sems = sems
        self.x, self.y, self.c, self.chips = _place()
        self.n = len(ts)

    @staticmethod
    def scratch(shards):
        sems = pltpu.SemaphoreType.DMA((3, len(shards)))
        return [pltpu.VMEM(s.shape, BF16) for s in shards] + [sems, sems, sems, sems, pltpu.SemaphoreType.DMA((len(shards),))]

    def _shard_of(self, i, kk):
        _, _, axis, sz = _BIG[self.ts[i]]
        sl = pl.ds(pl.multiple_of(kk * sz, sz), sz)
        return self.dsts[i].at[:, sl] if axis == 1 else self.dsts[i].at[sl, :]

    @staticmethod
    def _half_of(ref, cc):
        n = ref.shape[0] // 2
        return ref.at[pl.ds(pl.multiple_of(cc * n, n), n), :]

    def _ici(self, j, i, kk):
        px, py = self.chips[j]
        return pltpu.make_async_remote_copy(
            src_ref=self._half_of(self.srcs[i], self.c), dst_ref=self._half_of(self._shard_of(i, kk), self.c),
            send_sem=self.send_sems.at[j, i], recv_sem=self.recv_sems.at[j, i],
            device_id=(px, py, self.c), device_id_type=_MESH)

    def _forward(self, j, i, kk, cc):
        part = self._half_of(self._shard_of(i, kk), cc)
        return pltpu.make_async_remote_copy(
            src_ref=part, dst_ref=part, send_sem=self.fwd_send_sems.at[j, i], recv_sem=self.fwd_recv_sems.at[j, i],
            device_id=(self.x, self.y, 1 - self.c), device_id_type=_MESH)

    def _load(self, i):
        return pltpu.make_async_copy(self.srcs[i], self.stage[i], self.loc_sems.at[i])

    def _place_own(self, i):
        return pltpu.make_async_copy(self.stage[i], self._shard_of(i, 2 * self.x + self.y), self.loc_sems.at[i])

    def _peers(self):
        return [(i, j, 2 * px + py) for i in range(self.n) for j, (px, py) in enumerate(self.chips)]

    def start(self):
        for i in range(self.n):
            self._load(i).start()
        for i, j, _ in self._peers():
            self._ici(j, i, 2 * self.x + self.y).start()

    def forward(self):
        for i in range(self.n):
            self._load(i).wait()
            self._place_own(i).start()
        for i, j, kk in self._peers():
            self._ici(j, i, kk).wait_recv()
            self._forward(j, i, kk, self.c).start()

    def finish(self):
        for i, j, kk in self._peers():
            self._forward(j, i, kk, 1 - self.c).wait_recv()
        for i, j, kk in self._peers():
            self._ici(j, i, kk).wait_send()
            self._forward(j, i, kk, self.c).wait_send()
        for i in range(self.n):
            self._place_own(i).wait()


def _matmul_in_gather(a, shard, *, tm=1024):
    t = 0
    (_, (K, N), _, sz) = _BIG[t]
    M = a.shape[0]
    tm = min(tm, M)
    gm = M // tm
    x, y = lax.axis_index("x"), lax.axis_index("y")
    order = jnp.stack([2 * x + y, 2 * (1 - x) + y, 2 * x + 1 - y, 2 * (1 - x) + 1 - y]).astype(jnp.int32)

    def body(order_ref, a_ref, shard_ref, z_ref, full_ref, b_vm, *sems):
        g = _Gather([t], [shard_ref], [full_ref], [b_vm], sems[:5])
        load_sem = sems[5]
        s, i = pl.program_id(0), pl.program_id(1)

        @pl.when((s == 0) & (i == 0))
        def _():
            g.start()
            g._load(0).wait()
            g._place_own(0).start()

        for j, (px, py) in enumerate(g.chips):
            @pl.when((s == j + 1) & (i == 0))
            def _(j=j, kk=2 * px + py):
                if j == 0:
                    g._place_own(0).wait()
                g._ici(j, 0, kk).wait_recv()
                g._forward(j, 0, kk, g.c).start()
                g._forward(j, 0, kk, 1 - g.c).wait_recv()
                cp = pltpu.make_async_copy(g._shard_of(0, kk), b_vm, load_sem.at[0])
                cp.start()
                cp.wait()

        z_ref[...] = jnp.dot(a_ref[...], b_vm[...], preferred_element_type=F32)

        @pl.when((s == N_CHIPS - 1) & (i == gm - 1))
        def _():
            for j, (px, py) in enumerate(g.chips):
                g._ici(j, 0, 2 * px + py).wait_send()
                g._forward(j, 0, 2 * px + py, g.c).wait_send()

    return pl.pallas_call(
        body, name="mm_in",
        out_shape=[jax.ShapeDtypeStruct((M, N), F32), jax.ShapeDtypeStruct((K, N), BF16)],
        grid_spec=pltpu.PrefetchScalarGridSpec(
            num_scalar_prefetch=1, grid=(N_CHIPS, gm),
            in_specs=[pl.BlockSpec((tm, K), lambda s, i, o: (i, 0)), _ANY],
            out_specs=[pl.BlockSpec((tm, sz), lambda s, i, o: (i, o[s])), _ANY],
            scratch_shapes=_Gather.scratch([shard]) + [pltpu.SemaphoreType.DMA((1,))]),
        compiler_params=_params(("arbitrary", "arbitrary")),
    )(order, a, shard)


SMALL_W = 1024
SMALL_ROWS = 72
N_SMALL = 8 * SMALL_ROWS * SMALL_W
_RED = tuple((shape, ax, (shape[0] // 2, sz) if ax == 1 else (sz // 2, shape[1]), BF16) for _, shape, ax, sz in _BIG) + \
    (((8 * SMALL_ROWS, SMALL_W), 0, (SMALL_ROWS, SMALL_W), F32),)
_RED_TR = 128


def _piece(ref, t, kk, cc):
    _, ax, (pr, pc), _ = _RED[t]
    if ax == 1:
        return ref.at[pl.ds(pl.multiple_of(cc * pr, pr), pr), pl.ds(pl.multiple_of(kk * pc, pc), pc)]
    return ref.at[pl.ds(pl.multiple_of((2 * kk + cc) * pr, pr), pr), :]


def _half_shape(t):
    shape, ax, (pr, pc), _ = _RED[t]
    return (pr, shape[1]) if ax == 1 else (N_CHIPS * pr, pc)


def _piece_in_half(ref, t, kk):
    _, ax, (pr, pc), _ = _RED[t]
    return ref.at[:, pl.ds(pl.multiple_of(kk * pc, pc), pc)] if ax == 1 else ref.at[pl.ds(pl.multiple_of(kk * pr, pr), pr), :]


def _grad_sibling_exchange(ts, grads, name):
    n = len(ts)
    n_dma = sum(1 if _RED[t][1] == 1 else N_CHIPS for t in ts)

    def body(*refs):
        srcs, dsts, (send_sems, recv_sems) = refs[:n], refs[n:2 * n], refs[2 * n:]
        x, y, c, _ = _place()
        pairs = []
        for i, t in enumerate(ts):
            _, ax, (pr, _), _ = _RED[t]
            if ax == 1:
                pairs.append((srcs[i].at[pl.ds(pl.multiple_of((1 - c) * pr, pr), pr), :], dsts[i]))
            else:
                pairs += [(_piece(srcs[i], t, kk, 1 - c), _piece_in_half(dsts[i], t, kk)) for kk in range(N_CHIPS)]
        cps = [pltpu.make_async_remote_copy(src_ref=s, dst_ref=d, send_sem=send_sems.at[i], recv_sem=recv_sems.at[i],
                                            device_id=(x, y, 1 - c), device_id_type=_MESH) for i, (s, d) in enumerate(pairs)]
        for cp in cps:
            cp.start()
        for cp in cps:
            cp.wait()

    return pl.pallas_call(
        body, name=name,
        out_shape=[jax.ShapeDtypeStruct(_half_shape(t), F32) for t in ts],
        in_specs=[_ANY] * n, out_specs=[_ANY] * n,
        scratch_shapes=[pltpu.SemaphoreType.DMA((n_dma,)), pltpu.SemaphoreType.DMA((n_dma,))],
    )(*grads)


def _chip_sum(t, g, rs, place):
    shape, ax, (pr, pc), dt = _RED[t]
    W = shape[1]
    tr = min(pr, _RED_TR)
    nb = pr // tr

    def body(place_ref, g_ref, rs_ref, o_ref):
        o_ref[...] = (g_ref[...] + rs_ref[...]).astype(o_ref.dtype)

    return pl.pallas_call(
        body, name="grad_chip_sum_%d" % t,
        out_shape=jax.ShapeDtypeStruct(rs.shape, dt),
        grid_spec=pltpu.PrefetchScalarGridSpec(
            num_scalar_prefetch=1, grid=(1 if ax == 1 else N_CHIPS, nb),
            in_specs=[pl.BlockSpec((tr, W), lambda kk, i, pr_: ((2 * kk + pr_[0]) * nb + i, 0)),
                      pl.BlockSpec((tr, W), lambda kk, i, pr_: (kk * nb + i, 0))],
            out_specs=pl.BlockSpec((tr, W), lambda kk, i, pr_: (kk * nb + i, 0))),
        compiler_params=_params(("parallel", "parallel")),
    )(place, g, rs)


class _ChipExchange:
    def __init__(self, ts, srcs, dsts, sems):
        self.send_sems, self.recv_sems = sems
        x, y, c, chips = _place()
        self.copies = lambda: [
            pltpu.make_async_remote_copy(src_ref=_piece_in_half(srcs[i], t, 2 * px + py), dst_ref=dsts[i].at[j],
                                         send_sem=self.send_sems.at[j, i], recv_sem=self.recv_sems.at[j, i],
                                         device_id=(px, py, c), device_id_type=_MESH)
            for i, t in enumerate(ts) for j, (px, py) in enumerate(chips)]

    @staticmethod
    def scratch(ts):
        return [pltpu.SemaphoreType.DMA((3, len(ts))), pltpu.SemaphoreType.DMA((3, len(ts)))]

    @staticmethod
    def out_shape(ts):
        return [jax.ShapeDtypeStruct((3,) + _RED[t][2], _RED[t][3]) for t in ts]

    def start(self):
        for cp in self.copies():
            cp.start()

    def finish(self):
        for cp in self.copies():
            cp.wait()


def _grad_chip_exchange(ts, sums):
    n = len(ts)

    def body(*refs):
        ex = _ChipExchange(ts, refs[:n], refs[n:2 * n], refs[2 * n:])
        ex.start()
        ex.finish()

    return pl.pallas_call(
        body, name="grad_chip_exchange",
        out_shape=_ChipExchange.out_shape(ts),
        in_specs=[_ANY] * n, out_specs=[_ANY] * n,
        scratch_shapes=_ChipExchange.scratch(ts),
    )(*sums)


def _total_sum(t, g, rs, rc, place):
    shape, ax, (pr, pc), _ = _RED[t]
    tr = min(pr, _RED_TR)
    nb = pr // tr
    small = t == len(_RED) - 1

    def body(place_ref, g_ref, rs_ref, rc_ref, o_ref):
        o_ref[...] = (g_ref[...] + rs_ref[...]) + rc_ref[0].astype(F32) + rc_ref[1].astype(F32) + rc_ref[2].astype(F32)

    if ax == 1:
        g_map = lambda i, pr_: (pr_[0] * nb + i, pr_[1])
        rs_map = lambda i, pr_: (i, pr_[1])
    else:
        g_map = lambda i, pr_: ((2 * pr_[1] + pr_[0]) * nb + i, 0)
        rs_map = lambda i, pr_: (pr_[1] * nb + i, 0)
    o_map = (lambda i, pr_: ((2 * pr_[1] + pr_[0]) * nb + i, 0)) if small else (lambda i, pr_: (pr_[0] * nb + i, 0))
    return pl.pallas_call(
        body, name="grad_total_sum_%d" % t,
        out_shape=jax.ShapeDtypeStruct(((8 if small else 2) * pr, pc), F32),
        grid_spec=pltpu.PrefetchScalarGridSpec(
            num_scalar_prefetch=1, grid=(nb,),
            in_specs=[pl.BlockSpec((tr, pc), g_map), pl.BlockSpec((tr, pc), rs_map),
                      pl.BlockSpec((3, tr, pc), lambda i, pr_: (0, i, 0))],
            out_specs=pl.BlockSpec((tr, pc), o_map)),
        compiler_params=_params(("parallel",)),
    )(place, g, rs, rc)


def _grad_final_exchange(totals):
    n = len(_RED)
    nb = n - 1

    def body(*refs):
        srcs, dsts, (send_sems, recv_sems) = refs[:n], refs[n:2 * n], refs[2 * n:]
        x, y, c, chips = _place()
        me = 4 * x + 2 * y + c
        others = [(x, y, 1 - c)] + [(px, py, cc) for (px, py) in chips for cc in (c, 1 - c)]

        def half(ref, t, cc):
            pr = _RED[t][2][0]
            return ref.at[pl.ds(pl.multiple_of(cc * pr, pr), pr), :]

        def eighth(ref, dev):
            return ref.at[pl.ds(pl.multiple_of(dev * SMALL_ROWS, SMALL_ROWS), SMALL_ROWS), :]

        def big_copy(t, cc):
            return pltpu.make_async_remote_copy(src_ref=half(srcs[t], t, cc), dst_ref=half(dsts[t], t, cc), send_sem=send_sems.at[t],
                                                recv_sem=recv_sems.at[t], device_id=others[0], device_id_type=_MESH)

        def small_copy(i, dev):
            return pltpu.make_async_remote_copy(src_ref=eighth(srcs[nb], dev), dst_ref=eighth(dsts[nb], dev),
                                                send_sem=send_sems.at[nb + i], recv_sem=recv_sems.at[nb + i],
                                                device_id=others[i], device_id_type=_MESH)

        sends = [big_copy(t, c) for t in range(nb)] + [small_copy(i, me) for i in range(7)]
        for cp in sends:
            cp.start()
        for t in range(nb):
            big_copy(t, 1 - c).wait_recv()
        for i, (px, py, pc) in enumerate(others):
            small_copy(i, 4 * px + 2 * py + pc).wait_recv()
        for cp in sends:
            cp.wait_send()

    return pl.pallas_call(
        body, name="grad_final_exchange",
        out_shape=[jax.ShapeDtypeStruct(a.shape, F32) for a in totals],
        in_specs=[_ANY] * n, out_specs=[_ANY] * n,
        input_output_aliases={t: t for t in range(n)},
        scratch_shapes=[pltpu.SemaphoreType.DMA((nb + 7,)), pltpu.SemaphoreType.DMA((nb + 7,))],
    )(*totals)


def _grad_place():
    return jnp.stack([lax.axis_index("c"), 2 * lax.axis_index("x") + lax.axis_index("y")]).astype(jnp.int32)


def _reduce_begin(ts, grads, place, tag):
    from_sibling = _grad_sibling_exchange(ts, grads, "grad_sibling_exchange_" + tag)
    return from_sibling, [_chip_sum(t, g, r, place) for t, g, r in zip(ts, grads, from_sibling)]


def _reduce_end(ts, grads, from_sibling, from_chips, place):
    return [_total_sum(t, g, r, q, place) for t, g, r, q in zip(ts, grads, from_sibling, from_chips)]


_EARLY = (1, 2, 3, 4)
_W_IN = (0,)
_SMALL_RED = (5,)


def _adamw_math(w, g, m, v):
    m = ADAM_B1 * m + (1.0 - ADAM_B1) * g
    v = ADAM_B2 * v + (1.0 - ADAM_B2) * (g * g)
    m_hat = m / (1.0 - ADAM_B1 ** ADAM_STEP)
    v_hat = v / (1.0 - ADAM_B2 ** ADAM_STEP)
    return -ADAM_LR * (m_hat / (jnp.sqrt(v_hat) + ADAM_EPS) + ADAM_WD * w), m, v


def _adamw(w, g, m, v, name):
    W = w.shape[1]
    return _rowwise(_adamw_math, [(a, 0, W) for a in (w, g, m, v)], [], [(W, F32)] * 3, tr=128, name=name)


def _adamw_whole(ws, gs, ms, vs, name):
    n = len(ws)

    def body(*refs):
        ins, outs = refs[:4 * n], refs[4 * n:]
        for i in range(n):
            res = _adamw_math(*[ins[j * n + i][...] for j in range(4)])
            for j in range(3):
                outs[j * n + i][...] = res[j]

    res = pl.pallas_call(
        body, name=name,
        out_shape=[jax.ShapeDtypeStruct(a.shape, F32) for a in ws] * 3,
        compiler_params=pltpu.CompilerParams(vmem_limit_bytes=VMEM_LIMIT_V7X),
    )(*ws, *gs, *ms, *vs)
    return res[:n], res[n:2 * n], res[2 * n:]


def _chunks(arr, off, width, w=512):
    return [(arr, off + i * w, w) for i in range(width // w)]


def _cat(vs):
    return jnp.concatenate(vs, axis=1)


def _forward_backward(x, p_b, tgt, shards, small):
    L = x.shape[0]
    row = lambda v: v.reshape(1, -1)
    g_mix, g_ple, g_fin = row(small["norm_mix"]), row(small["norm_ple"]), row(small["norm_final"])
    gq, gk, b_glu = row(small["q_norm"]), row(small["k_norm"]), row(small["b_glu"])
    cos, sin = _rope_tables(L)

    hn_b, = _rowwise(lambda x, g: x * _rms(x) * g, [(x, 0, D_MODEL)], [g_mix], [(D_MODEL, BF16)], name="norm_mix")
    z, w_in = _matmul_in_gather(hn_b, shards[0])
    qr, kr, vb = _attn_prep(z, gq, gk, cos, sin)
    o, lse, w_glu, w_out, w_pg, w_pp = _attn_fwd(qr, kr, vb, [1, 2, 3, 4], shards[1:])

    ssm_names = ("ssm_a_re", "ssm_a_im", "ssm_log_dt", "ssm_b_re", "ssm_b_im")
    (lre, lim, bre, bim), disc_vjp = jax.vjp(_ssm_disc, *[small[n][0] for n in ssm_names])
    ssm = (_to_pairs(jnp.swapaxes(jnp.stack([bre, bim], axis=2), -1, -2)),
           _to_pairs(jnp.stack([small["ssm_c_re"][0], -small["ssm_c_im"][0]], axis=2)),
           _to_lam(lre), _to_lam(lim), row(small["ssm_d"]))
    u_p = _seg_perm(z[:, Z_U:Z_U + D_SSM])
    y_p, ssm_entry = _ssm_fwd(u_p, *ssm)
    y_s = _seg_unperm(y_p)
    ge_b, = _rowwise(_gelu, [(y_s, 0, D_SSM)], [], [(D_SSM, BF16)], name="gelu")
    glu = _matmul(ge_b, w_glu, name="mm_glu")

    def merge(ga0, ga1, a, b, gs0, gs1, o, bias):
        sa, _ = _silu_and_grad(_cat([ga0, ga1]))
        ss, _ = _silu_and_grad(_cat([gs0, gs1]))
        y2 = (a + bias[:, :D_SSM]) * _sig(b + bias[:, D_SSM:])
        return _cat([o * sa, y2 * ss])
    merge_rows = _chunks(z, Z_GA, D_ATTN) + [(glu, 0, D_SSM), (glu, D_SSM, D_SSM)] + _chunks(z, Z_GS, D_SSM) + [(o, 0, D_ATTN)]
    cat_b, = _rowwise(merge, merge_rows, [b_glu], [(D_MODEL, BF16)], name="merge")
    t_out = _matmul(cat_b, w_out, name="mm_out")

    def resid(x, t, g):
        h1 = x + t
        return h1, h1 * _rms(h1) * g
    h1, hp_b = _rowwise(resid, [(x, 0, D_MODEL), (t_out, 0, D_MODEL)], [g_ple], [(D_MODEL, F32), (D_MODEL, BF16)], name="resid_norm")
    gl = _matmul(hp_b, w_pg, name="mm_ple_gate")
    pp = _matmul(p_b, w_pp, name="mm_ple_proj")

    def head(h1, gl, pp, tgt, g):
        gate = _sig(gl)
        h2 = h1 + gate * pp
        r = _rms(h2)
        n = h2 * r
        err = n * g - tgt
        dy = err * (1.0 / D_MODEL)
        dn = dy * g
        dh2 = r * (dn - n * jnp.mean(dn * n, axis=-1, keepdims=True))
        dgate = dh2 * pp
        return dh2, dh2 * gate, dgate * gate * (1.0 - gate), _colsum(dy * n), _colsum(0.5 * err * err * (1.0 / D_MODEL))
    dh2, dpp_b, dgl_b, dg_fin, loss_cols = _rowwise(
        head, [(a, 0, D_MODEL) for a in (h1, gl, pp, tgt)], [g_fin],
        [(D_MODEL, F32), (D_MODEL, BF16), (D_MODEL, BF16)], [(1, D_MODEL), (1, D_MODEL)], name="loss_head")

    dw_pp = _matmul(p_b, dpp_b, ta=True, name="mm_d_w_ple_proj")
    dw_pg = _matmul(hp_b, dgl_b, ta=True, name="mm_d_w_ple_gate")
    dhp = _matmul(dgl_b, w_pg, tb=True, name="mm_d_hp")

    def resid_bwd(dhp, h1, dh2, g):
        dx, dg = _rms_bwd(h1, g, dhp)
        dh1 = dh2 + dx
        return dh1, dh1, _colsum(dg)
    dh1, dh1_b, dg_ple = _rowwise(resid_bwd, [(a, 0, D_MODEL) for a in (dhp, h1, dh2)], [g_ple],
                                  [(D_MODEL, F32), (D_MODEL, BF16)], [(1, D_MODEL)], name="resid_norm_bwd")
    dw_out = _matmul(cat_b, dh1_b, ta=True, name="mm_d_w_out")
    dcat = _matmul(dh1_b, w_out, tb=True, name="mm_d_cat")

    def merge_bwd(dya, dys, ga0, ga1, a, b, gs0, gs1, o, bias):
        ga, gs = _cat([ga0, ga1]), _cat([gs0, gs1])
        sa, dsa = _silu_and_grad(ga)
        ss, dss = _silu_and_grad(gs)
        a, sb = a + bias[:, :D_SSM], _sig(b + bias[:, D_SSM:])
        dy2 = dys * ss
        dglu = _cat([dy2 * sb, dy2 * a * sb * (1.0 - sb)])
        do = dya * sa
        lane = lax.broadcasted_iota(jnp.int32, (do.shape[0], HEAD_DIM), 1)
        delta = sum(jnp.where(lane == i, jnp.sum(h, axis=1, keepdims=True), 0.0)
                    for i, h in enumerate(_heads(do * o)))
        return do, dya * o * dsa, dys * (a * sb) * dss, dglu, delta, _colsum(dglu)
    do_b, dga_b, dgs_b, dglu_b, delta, db_glu = _rowwise(
        merge_bwd, [(dcat, 0, D_ATTN), (dcat, D_ATTN, D_SSM)] + merge_rows, [b_glu],
        [(D_ATTN, BF16), (D_ATTN, BF16), (D_SSM, BF16), (2 * D_SSM, BF16), (HEAD_DIM, F32)], [(1, 2 * D_SSM)], name="merge_bwd")
    dw_glu = _matmul(ge_b, dglu_b, ta=True, name="mm_d_w_glu")
    dge = _matmul(dglu_b, w_glu, tb=True, name="mm_d_ge")
    dy_s, = _rowwise(lambda dge, y: dge * _gelu_grad(y), [(dge, 0, D_SSM), (y_s, 0, D_SSM)], [], [(D_SSM, F32)], name="gelu_bwd")
    du_p, dwb, dwc, dare, daim, d_ssm_d = _ssm_bwd(u_p, _seg_perm(dy_s), ssm_entry, *ssm)
    dcc, dbb = _from_pairs(dwc), jnp.swapaxes(_from_pairs(dwb), -1, -2)
    dc_re, dc_im = dcc[:, :, 0], -dcc[:, :, 1]
    da_re, da_im, dlog_dt, db_re, db_im = disc_vjp((_from_lam(dare), _from_lam(daim), dbb[:, :, 0], dbb[:, :, 1]))

    place = _grad_place()
    early_grads = [dw_glu, dw_out, dw_pg, dw_pp]
    early_sib, early_sums = _reduce_begin(_EARLY, early_grads, place, "early")
    dqr, dkr, dv, *early_chips = _attn_bwd(qr, kr, kr.T, vb, do_b, lse.reshape(N_HEADS, 1, L), delta[:, :N_HEADS].T.reshape(N_HEADS, 1, L),
                                           _EARLY, early_sums)
    early_totals = _reduce_end(_EARLY, early_grads, early_sib, early_chips, place)
    dq_b, dk_b, dgq, dgk = _attn_prep_bwd(dqr, dkr, z, gq, gk, cos, sin)
    dz_b = _cat([dq_b, dk_b, dv.astype(BF16), dga_b, _seg_unperm(du_p).astype(BF16), dgs_b])
    dw_in = _matmul(hn_b, dz_b, ta=True, name="mm_d_w_in")
    w_in_sib, w_in_sums = _reduce_begin(_W_IN, [dw_in], place, "w_in")
    dhn, *w_in_chips = _matmul(dz_b, w_in, tb=True, name="mm_d_hn", exchange=(_W_IN, w_in_sums))
    w_in_total, = _reduce_end(_W_IN, [dw_in], w_in_sib, w_in_chips, place)

    def norm_bwd(dhn, x, dh1, g):
        dx, dg = _rms_bwd(x, g, dhn)
        return dh1 + dx, _colsum(dg)
    grad_x, dg_mix = _rowwise(norm_bwd, [(a, 0, D_MODEL) for a in (dhn, x, dh1)], [g_mix], [(D_MODEL, F32)], [(1, D_MODEL)],
                              name="norm_mix_bwd")

    small_grads = {"norm_mix": dg_mix, "q_norm": dgq, "k_norm": dgk, "ssm_a_re": da_re, "ssm_a_im": da_im, "ssm_log_dt": dlog_dt,
                   "ssm_b_re": db_re, "ssm_b_im": db_im, "ssm_c_re": dc_re, "ssm_c_im": dc_im, "ssm_d": d_ssm_d,
                   "b_glu": db_glu, "norm_ple": dg_ple, "norm_final": dg_fin}
    return jnp.sum(loss_cols), grad_x, [w_in_total] + early_totals, small_grads, place


_SMALL = ("norm_mix", "q_norm", "k_norm", "ssm_a_re", "ssm_a_im", "ssm_log_dt", "ssm_b_re", "ssm_b_im", "ssm_c_re", "ssm_c_im",
          "ssm_d", "b_glu", "norm_ple", "norm_final")
_WEIGHTS = ("norm_mix", "w_in", "q_norm", "k_norm", "ssm_a_re", "ssm_a_im", "ssm_log_dt", "ssm_b_re", "ssm_b_im", "ssm_c_re",
            "ssm_c_im", "ssm_d", "w_glu", "b_glu", "w_out", "norm_ple", "w_ple_gate", "w_ple_proj", "norm_final")


_SMALL_ADAMW_GROUPS = (("ssm_b_re",), ("ssm_b_im",), ("ssm_c_re", "ssm_c_im"),
                       ("norm_mix", "q_norm", "k_norm", "ssm_a_re", "ssm_a_im", "ssm_log_dt", "ssm_d", "b_glu", "norm_ple",
                        "norm_final"))


def _flat_small(d):
    flat = jnp.concatenate([d[n].reshape(-1).astype(F32) for n in _SMALL])
    return jnp.pad(flat, (0, N_SMALL - flat.shape[0]))


def _split_small(flat, like):
    out, off = {}, 0
    for n in _SMALL:
        sz = math.prod(like[n].shape)
        out[n] = flat[off:off + sz].reshape(like[n].shape)
        off += sz
    return out


def kernel(x, p, norm_mix, w_in, q_norm, k_norm, ssm_a_re, ssm_a_im, ssm_log_dt, ssm_b_re, ssm_b_im, ssm_c_re, ssm_c_im, ssm_d, w_glu, b_glu, w_out, norm_ple, w_ple_gate, w_ple_proj, norm_final, loss_target, m_norm_mix, m_w_in, m_q_norm, m_k_norm, m_ssm_a_re, m_ssm_a_im, m_ssm_log_dt, m_ssm_b_re, m_ssm_b_im, m_ssm_c_re, m_ssm_c_im, m_ssm_d, m_w_glu, m_b_glu, m_w_out, m_norm_ple, m_w_ple_gate, m_w_ple_proj, m_norm_final, v_norm_mix, v_w_in, v_q_norm, v_k_norm, v_ssm_a_re, v_ssm_a_im, v_ssm_log_dt, v_ssm_b_re, v_ssm_b_im, v_ssm_c_re, v_ssm_c_im, v_ssm_d, v_w_glu, v_b_glu, v_w_out, v_norm_ple, v_w_ple_gate, v_w_ple_proj, v_norm_final):
    w = dict(norm_mix=norm_mix, w_in=w_in, q_norm=q_norm, k_norm=k_norm, ssm_a_re=ssm_a_re, ssm_a_im=ssm_a_im,
             ssm_log_dt=ssm_log_dt, ssm_b_re=ssm_b_re, ssm_b_im=ssm_b_im, ssm_c_re=ssm_c_re, ssm_c_im=ssm_c_im, ssm_d=ssm_d,
             w_glu=w_glu, b_glu=b_glu, w_out=w_out, norm_ple=norm_ple, w_ple_gate=w_ple_gate, w_ple_proj=w_ple_proj,
             norm_final=norm_final)
    m = dict(norm_mix=m_norm_mix, w_in=m_w_in, q_norm=m_q_norm, k_norm=m_k_norm, ssm_a_re=m_ssm_a_re, ssm_a_im=m_ssm_a_im,
             ssm_log_dt=m_ssm_log_dt, ssm_b_re=m_ssm_b_re, ssm_b_im=m_ssm_b_im, ssm_c_re=m_ssm_c_re, ssm_c_im=m_ssm_c_im,
             ssm_d=m_ssm_d, w_glu=m_w_glu, b_glu=m_b_glu, w_out=m_w_out, norm_ple=m_norm_ple, w_ple_gate=m_w_ple_gate,
             w_ple_proj=m_w_ple_proj, norm_final=m_norm_final)
    v = dict(norm_mix=v_norm_mix, w_in=v_w_in, q_norm=v_q_norm, k_norm=v_k_norm, ssm_a_re=v_ssm_a_re, ssm_a_im=v_ssm_a_im,
             ssm_log_dt=v_ssm_log_dt, ssm_b_re=v_ssm_b_re, ssm_b_im=v_ssm_b_im, ssm_c_re=v_ssm_c_re, ssm_c_im=v_ssm_c_im,
             ssm_d=v_ssm_d, w_glu=v_w_glu, b_glu=v_b_glu, w_out=v_w_out, norm_ple=v_norm_ple, w_ple_gate=v_w_ple_gate,
             w_ple_proj=v_w_ple_proj, norm_final=v_norm_final)
    big_names = [n for n, _, _, _ in _BIG]

    small = {n: w[n] for n in _SMALL}
    loss_part, grad_x, big_totals, small_grads, place = _forward_backward(
        x[0], p[0, 0].astype(BF16), loss_target[0], [w[n][0].astype(BF16) for n in big_names], small)
    loss = lax.psum(loss_part, ("x", "y", "c"))

    small_flat = [_flat_small(small_grads).reshape(8 * SMALL_ROWS, SMALL_W)]
    small_sib, small_sums = _reduce_begin(_SMALL_RED, small_flat, place, "small")
    small_total = _reduce_end(_SMALL_RED, small_flat, small_sib, _grad_chip_exchange(_SMALL_RED, small_sums), place)
    *big_red, small_red = _grad_final_exchange(big_totals + small_total)
    grads = _split_small(small_red.reshape(-1), w)
    delta, new_m, new_v = {}, {}, {}
    for n, g in zip(big_names, big_red):
        grads[n] = g[None]
        d_, m_, v_ = _adamw(w[n][0], g, m[n][0], v[n][0], "adamw_" + n)
        delta[n], new_m[n], new_v[n] = d_[None], m_[None], v_[None]
    at_least_2d = lambda a: a.reshape(1, -1) if a.ndim == 1 else a
    for i, names in enumerate(_SMALL_ADAMW_GROUPS):
        d_, m_, v_ = _adamw_whole(*[[at_least_2d(src[n]) for n in names] for src in (w, grads, m, v)], "adamw_small_%d" % i)
        for j, n in enumerate(names):
            delta[n], new_m[n], new_v[n] = (a[j].reshape(w[n].shape) for a in (d_, m_, v_))
    return (loss, grad_x[None], *[grads[n] for n in _WEIGHTS], *[delta[n] for n in _WEIGHTS],
            *[new_m[n] for n in _WEIGHTS], *[new_v[n] for n in _WEIGHTS])
```

```python
import functools
import math

import jax
import jax.numpy as jnp
import numpy as np
from jax import lax
from jax.experimental import pallas as pl
from jax.experimental.pallas import tpu as pltpu

D_MODEL = 2048
GRID_W = 64
PLE_DIM = 256
D_ATTN = 1024
N_HEADS = 8
N_KV = 2
HEAD_DIM = 128
ROPE_THETA = 10000.0
D_SSM = 1024
SSM_H = 16
SSM_G = 64
SSM_P = 64
D_KV = N_KV * HEAD_DIM
D_IN = 2 * D_ATTN + 2 * D_KV + 2 * D_SSM
EPS = 1e-6
Z_Q, Z_K, Z_V, Z_GA, Z_U, Z_GS = 0, 1024, 1280, 1536, 2560, 3584

ADAM_LR, ADAM_B1, ADAM_B2, ADAM_EPS, ADAM_WD, ADAM_STEP = 0.001, 0.9, 0.999, 1e-08, 0.01, 10

N_CHIPS = 4
VMEM_LIMIT_V7X = 56 * 1024 * 1024
F32 = jnp.float32
BF16 = jnp.bfloat16


def _params(sem, vmem=VMEM_LIMIT_V7X):
    return pltpu.CompilerParams(dimension_semantics=sem, vmem_limit_bytes=vmem)


def _matmul(a, b, *, ta=False, tb=False, out_dtype=F32, tm=1024, tn=None, name, exchange=None):
    M, K = (a.shape[1], a.shape[0]) if ta else a.shape
    N = b.shape[0] if tb else b.shape[1]
    if tn is None:
        tn = 1024 if (N % 1024 == 0 and K <= 4096) else 512
    tm, tn = min(tm, M), min(tn, N)
    assert M % tm == 0 and N % tn == 0, (name, M, N, K)
    dims = (((0 if ta else 1,), (1 if tb else 0,)), ((), ()))
    ex_ts, ex_sums = exchange if exchange else ((), ())
    n_ex = len(ex_ts)
    gm, gn = M // tm, N // tn

    def body(a_ref, b_ref, *rest):
        o_ref = rest[n_ex]
        if n_ex:
            ex = _ChipExchange(ex_ts, rest[:n_ex], rest[n_ex + 1:2 * n_ex + 1], rest[2 * n_ex + 1:])
            step = pl.program_id(0) * gn + pl.program_id(1)
            pl.when(step == 0)(ex.start)
        o_ref[...] = lax.dot_general(a_ref[...], b_ref[...], dims, preferred_element_type=F32).astype(o_ref.dtype)
        if n_ex:
            pl.when(step == gm * gn - 1)(ex.finish)

    a_spec = pl.BlockSpec((K, tm), lambda i, j: (0, i)) if ta else pl.BlockSpec((tm, K), lambda i, j: (i, 0))
    b_spec = pl.BlockSpec((tn, K), lambda i, j: (j, 0)) if tb else pl.BlockSpec((K, tn), lambda i, j: (0, j))
    res = pl.pallas_call(
        body, name=name,
        out_shape=[jax.ShapeDtypeStruct((M, N), out_dtype)] + (_ChipExchange.out_shape(ex_ts) if n_ex else []),
        grid=(gm, gn),
        in_specs=[a_spec, b_spec] + [_ANY] * n_ex,
        out_specs=[pl.BlockSpec((tm, tn), lambda i, j: (i, j))] + [_ANY] * n_ex,
        scratch_shapes=_ChipExchange.scratch(ex_ts) if n_ex else [],
        compiler_params=_params(("arbitrary", "arbitrary") if n_ex else ("parallel", "parallel")),
    )(a, b, *ex_sums)
    return res if n_ex else res[0]


def _rowwise(fn, rows, consts, outs, accs=(), *, tr=256, name):
    L = rows[0][0].shape[0]
    tr = math.gcd(tr, L)
    assert tr % 8 == 0 or tr == L, (name, L, tr)
    n_in, n_c, n_o, n_a = len(rows), len(consts), len(outs), len(accs)

    def body(*refs):
        ins = [r[...] for r in refs[:n_in + n_c]]
        res = fn(*ins)
        if not isinstance(res, (tuple, list)):
            res = (res,)
        o_refs = refs[n_in + n_c:n_in + n_c + n_o]
        a_refs = refs[n_in + n_c + n_o:]
        for r, v in zip(o_refs, res[:n_o]):
            r[...] = v.astype(r.dtype)
        if n_a:
            first = pl.program_id(0) == 0

            @pl.when(first)
            def _():
                for r, v in zip(a_refs, res[n_o:]):
                    r[...] = v.astype(F32)

            @pl.when(jnp.logical_not(first))
            def _():
                for r, v in zip(a_refs, res[n_o:]):
                    r[...] += v.astype(F32)

    in_specs = []
    for arr, off, w in rows:
        assert off % w == 0, (name, off, w)
        in_specs.append(pl.BlockSpec((tr, w), functools.partial(lambda i, c: (i, c), c=off // w)))
    for c in consts:
        in_specs.append(pl.BlockSpec(c.shape, lambda i: (0, 0)))
    out_shape = [jax.ShapeDtypeStruct((L, w), dt) for w, dt in outs] + [jax.ShapeDtypeStruct(s, F32) for s in accs]
    out_specs = [pl.BlockSpec((tr, w), lambda i: (i, 0)) for w, _ in outs] + [pl.BlockSpec(s, lambda i: (0, 0)) for s in accs]
    res = pl.pallas_call(
        body, name=name,
        out_shape=out_shape,
        grid=(L // tr,),
        in_specs=in_specs,
        out_specs=out_specs,
        compiler_params=_params(("arbitrary",) if n_a else ("parallel",)),
    )(*[r[0] for r in rows], *consts)
    return res


def _sig(x):
    return jax.nn.sigmoid(x)


def _silu_and_grad(x):
    s = _sig(x)
    return x * s, s * (1.0 + x * (1.0 - s))


_GELU_C = math.sqrt(2.0 / math.pi)


def _gelu(x):
    return 0.5 * x * (1.0 + jnp.tanh(_GELU_C * (x + 0.044715 * x * x * x)))


def _gelu_grad(x):
    t = jnp.tanh(_GELU_C * (x + 0.044715 * x * x * x))
    return 0.5 * (1.0 + t) + 0.5 * x * (1.0 - t * t) * _GELU_C * (1.0 + 3.0 * 0.044715 * x * x)


def _rms(x):
    return lax.rsqrt(jnp.mean(x * x, axis=-1, keepdims=True) + EPS)


def _rms_bwd(x, g, dy):
    r = _rms(x)
    n = x * r
    dn = dy * g
    return r * (dn - n * jnp.mean(dn * n, axis=-1, keepdims=True)), dy * n


def _colsum(v):
    return jnp.sum(v, axis=0, keepdims=True)


def _rope_partner(x):
    lane = lax.broadcasted_iota(jnp.int32, x.shape, x.ndim - 1)
    return jnp.where(lane % 64 < 32, pltpu.roll(x, 96, x.ndim - 1), pltpu.roll(x, 32, x.ndim - 1))


def _rope_tables(L):
    rows_n = L // GRID_W
    rows = jnp.repeat(jnp.arange(rows_n), GRID_W).astype(F32)
    cols = jnp.tile(jnp.arange(GRID_W), rows_n).astype(F32)
    n_freq = HEAD_DIM // 4
    inv_freq = ROPE_THETA ** (-jnp.arange(n_freq, dtype=F32) / n_freq)
    ar, ac = rows[:, None] * inv_freq[None, :], cols[:, None] * inv_freq[None, :]
    cos = jnp.concatenate([jnp.cos(ar), jnp.cos(ar), jnp.cos(ac), jnp.cos(ac)], axis=-1)
    sin = jnp.concatenate([-jnp.sin(ar), jnp.sin(ar), -jnp.sin(ac), jnp.sin(ac)], axis=-1)
    return cos, sin


def _heads(v):
    return [v[:, h * HEAD_DIM:(h + 1) * HEAD_DIM] for h in range(v.shape[1] // HEAD_DIM)]


def _attn_prep(z, q_norm, k_norm, cos, sin):
    def fn(q, k, v, cos, sin, gq, gk):
        def one(xh, g):
            xn = xh * _rms(xh) * g
            return xn * cos + _rope_partner(xn) * sin
        qr = jnp.concatenate([one(h, gq) for h in _heads(q)], axis=1)
        kr = jnp.concatenate([one(h, gk) for h in _heads(k)], axis=1)
        return qr, kr, v
    return _rowwise(fn, [(z, Z_Q, D_ATTN), (z, Z_K, D_KV), (z, Z_V, D_KV), (cos, 0, HEAD_DIM), (sin, 0, HEAD_DIM)],
                    [q_norm, k_norm], [(D_ATTN, BF16), (D_KV, BF16), (D_KV, BF16)], name="attn_prep")


def _attn_prep_bwd(dqr, dkr, z, q_norm, k_norm, cos, sin):
    def fn(dqr, dkr, q, k, cos, sin, gq, gk):
        def one(dyh, xh, g):
            dn = dyh * cos + _rope_partner(dyh * sin)
            return _rms_bwd(xh, g, dn)
        rq = [one(a, b, gq) for a, b in zip(_heads(dqr), _heads(q))]
        rk = [one(a, b, gk) for a, b in zip(_heads(dkr), _heads(k))]
        dq = jnp.concatenate([r[0] for r in rq], axis=1)
        dk = jnp.concatenate([r[0] for r in rk], axis=1)
        return dq, dk, _colsum(sum(r[1] for r in rq)), _colsum(sum(r[1] for r in rk))
    return _rowwise(fn, [(dqr, 0, D_ATTN), (dkr, 0, D_KV), (z, Z_Q, D_ATTN), (z, Z_K, D_KV), (cos, 0, HEAD_DIM), (sin, 0, HEAD_DIM)],
                    [q_norm, k_norm], [(D_ATTN, BF16), (D_KV, BF16)], [(1, HEAD_DIM), (1, HEAD_DIM)], name="attn_prep_bwd")


_QK_T = (((1,), (1,)), ((), ()))
_TA = (((0,), (0,)), ((), ()))
_REP = N_HEADS // N_KV


_EXP2_SCALE = HEAD_DIM ** -0.5 * math.log2(math.e)
ATTN_FWD_KEY_CHUNKS = 4
ATTN_BWD_KEY_CHUNKS = 8


def _attn_fwd(qr, kr, vb, g_ts, g_shards, *, tq=1024):
    L = qr.shape[0]
    tq = min(tq, L)
    kc = L // ATTN_FWD_KEY_CHUNKS
    n_g = len(g_ts)
    grid = (N_HEADS, L // tq)
    steps = grid[0] * grid[1]

    def body(q_ref, k_ref, v_ref, *rest):
        o_ref, lse_ref = rest[n_g:n_g + 2]
        g = _Gather(g_ts, rest[:n_g], rest[n_g + 2:2 * n_g + 2], rest[2 * n_g + 2:3 * n_g + 2], rest[3 * n_g + 2:])
        step = pl.program_id(0) * grid[1] + pl.program_id(1)
        pl.when(step == 0)(g.start)
        pl.when(step == (3 * steps) // 4)(g.forward)
        q = q_ref[...]
        m = jnp.full((tq, 1), -jnp.inf, F32)
        l = jnp.zeros((tq, 1), F32)
        o = jnp.zeros((tq, HEAD_DIM), F32)
        for c in range(ATTN_FWD_KEY_CHUNKS):
            ks = slice(c * kc, (c + 1) * kc)
            s = lax.dot_general(q, k_ref[ks, :], _QK_T, preferred_element_type=F32)
            m_new = jnp.maximum(m, jnp.max(s, axis=1, keepdims=True))
            a = jnp.exp2((m - m_new) * _EXP2_SCALE)
            p = jnp.exp2((s - m_new) * _EXP2_SCALE)
            l = a * l + jnp.sum(p, axis=1, keepdims=True)
            o = a * o + jnp.dot(p.astype(BF16), v_ref[ks, :], preferred_element_type=F32)
            m = m_new
        o_ref[...] = o * (1.0 / l)
        lse_ref[...] = m * _EXP2_SCALE + jnp.log2(l)
        pl.when(step == steps - 1)(g.finish)

    kv = pl.BlockSpec((L, HEAD_DIM), lambda h, i: (0, h // _REP))
    return pl.pallas_call(
        body, name="attn_fwd",
        out_shape=[jax.ShapeDtypeStruct((L, D_ATTN), F32), jax.ShapeDtypeStruct((N_HEADS, L, 1), F32)] +
                  [jax.ShapeDtypeStruct(_BIG[t][1], BF16) for t in g_ts],
        grid=grid,
        in_specs=[pl.BlockSpec((tq, HEAD_DIM), lambda h, i: (i, h)), kv, kv] + [_ANY] * n_g,
        out_specs=[pl.BlockSpec((tq, HEAD_DIM), lambda h, i: (i, h)),
                   pl.BlockSpec((None, tq, 1), lambda h, i: (h, i, 0))] + [_ANY] * n_g,
        scratch_shapes=_Gather.scratch(g_shards),
        compiler_params=_params(("arbitrary", "arbitrary")),
    )(qr, kr, vb, *g_shards)


def _attn_bwd(qr, kr, k_t, vb, do, lse, delta, ex_ts, ex_sums, *, tq=1024):
    L = qr.shape[0]
    tq = min(tq, L)
    scale = HEAD_DIM ** -0.5
    kc = L // ATTN_BWD_KEY_CHUNKS
    n_ex = len(ex_ts)
    grid = (N_KV, _REP, L // tq)

    def body(q_ref, k_ref, kt_ref, v_ref, do_ref, lse_ref, delta_ref, *rest):
        dq_ref, dk_ref, dv_ref = rest[n_ex:n_ex + 3]
        ex = _ChipExchange(ex_ts, rest[:n_ex], rest[n_ex + 3:2 * n_ex + 3], rest[2 * n_ex + 3:])
        step = (pl.program_id(0) * grid[1] + pl.program_id(1)) * grid[2] + pl.program_id(2)
        pl.when(step == 0)(ex.start)

        @pl.when((pl.program_id(1) == 0) & (pl.program_id(2) == 0))
        def _():
            dk_ref[...] = jnp.zeros_like(dk_ref)
            dv_ref[...] = jnp.zeros_like(dv_ref)

        q, do, lse, delta = q_ref[...], do_ref[...], lse_ref[...], delta_ref[...]
        dq_t = 0.0
        for c in range(ATTN_BWD_KEY_CHUNKS):
            ks = slice(c * kc, (c + 1) * kc)
            st = lax.dot_general(k_ref[ks, :], q, _QK_T, preferred_element_type=F32)
            p = jnp.exp2(st * _EXP2_SCALE - lse)
            dv_ref[ks, :] += jnp.dot(p.astype(BF16), do, preferred_element_type=F32)
            dp = lax.dot_general(v_ref[ks, :], do, _QK_T, preferred_element_type=F32)
            ds = (p * (dp - delta) * scale).astype(BF16)
            dk_ref[ks, :] += jnp.dot(ds, q, preferred_element_type=F32)
            dq_t = dq_t + jnp.dot(kt_ref[:, ks], ds, preferred_element_type=F32)
        dq_ref[...] = dq_t.T
        pl.when(step == grid[0] * grid[1] * grid[2] - 1)(ex.finish)

    head = lambda g, r, i: (i, g * _REP + r)
    kv = pl.BlockSpec((L, HEAD_DIM), lambda g, r, i: (0, g))
    per_query = pl.BlockSpec((None, 1, tq), lambda g, r, i: (g * _REP + r, 0, i))
    return pl.pallas_call(
        body, name="attn_bwd",
        out_shape=[jax.ShapeDtypeStruct((L, D_ATTN), F32), jax.ShapeDtypeStruct((L, D_KV), F32), jax.ShapeDtypeStruct((L, D_KV), F32)] +
                  _ChipExchange.out_shape(ex_ts),
        grid=grid,
        in_specs=[pl.BlockSpec((tq, HEAD_DIM), head), kv,
                  pl.BlockSpec((HEAD_DIM, L), lambda g, r, i: (g, 0)), kv,
                  pl.BlockSpec((tq, HEAD_DIM), head), per_query, per_query] + [_ANY] * n_ex,
        out_specs=[pl.BlockSpec((tq, HEAD_DIM), head), kv, kv] + [_ANY] * n_ex,
        scratch_shapes=_ChipExchange.scratch(ex_ts),
        compiler_params=_params(("arbitrary", "arbitrary", "arbitrary")),
    )(qr, kr, k_t, vb, do, lse, delta, *ex_sums)


SSM_BLK = 8
SSM_NB = SSM_G // SSM_BLK
SSM_SEG = 8
SSM_UNROLL = 4


def _unrolled_loop(n, step, carry):
    u = SSM_UNROLL

    def trip(i, c):
        for j in range(u):
            c = step(i * u + j, c)
        return c
    carry = lax.fori_loop(0, n // u, trip, carry)
    for t in range(n - n % u, n):
        carry = step(jnp.int32(t), carry)
    return carry


def _cplx_pow2(a, b, n):
    for _ in range(int(math.log2(n))):
        a, b = a * a - b * b, 2.0 * a * b
    return a, b


def _seg_scan(ref, a, b, T, reverse, entry=None, tap=None):
    npair = len(a)
    zero = jnp.zeros((SSM_SEG, 128), F32)

    def make_step(store, tap_fn=None):
        def step(t, carry):
            lt = (T - 1 - t) if reverse else t
            row = pl.multiple_of(lt * SSM_SEG, SSM_SEG)
            blk = ref[pl.ds(row, SSM_SEG), :]
            new = []
            for q in range(npair):
                re, im = carry[2 * q], carry[2 * q + 1]
                nre = a[q] * re - b[q] * im + blk[:, q * 256:q * 256 + 128]
                nim = a[q] * im + b[q] * re + blk[:, q * 256 + 128:q * 256 + 256]
                new += [nre, nim]
            if store:
                ref[pl.ds(row, SSM_SEG), :] = jnp.concatenate(new, axis=1)
            extra = carry[2 * npair:]
            return tuple(new) + (tuple(tap_fn(lt, new, extra)) if tap_fn else tuple(extra))
        return step

    def second_pass(init):
        if tap is None:
            _unrolled_loop(T, make_step(True), tuple(init))
            return init
        carry = _unrolled_loop(T - 1, make_step(True, tap[0]), tuple(init) + tuple(tap[2]))
        carry = make_step(True, tap[1])(jnp.int32(T - 1), carry)
        return init, carry[2 * npair:]

    if entry is not None:
        return second_pass(entry)
    ends = _unrolled_loop(T, make_step(False), (zero,) * (2 * npair))
    sub = lax.broadcasted_iota(jnp.int32, (SSM_SEG, 128), 0)
    keep = (sub != SSM_SEG - 1) if reverse else (sub != 0)
    shift = (SSM_SEG - 1) if reverse else 1
    init = []
    for q in range(npair):
        pa, pb = _cplx_pow2(a[q], b[q], T)
        xr, xi = zero, zero
        for _ in range(SSM_SEG - 1):
            fr = ends[2 * q] + pa * xr - pb * xi
            fi = ends[2 * q + 1] + pa * xi + pb * xr
            xr = jnp.where(keep, pltpu.roll(fr, shift, 0), 0.0)
            xi = jnp.where(keep, pltpu.roll(fi, shift, 0), 0.0)
        init += [xr, xi]
    return second_pass(init)


SSM_BW = SSM_BLK * SSM_H
SSM_SW = SSM_BLK * 2 * SSM_P
SSM_NPAIR = SSM_BLK // 2


def _seg_perm(a):
    L, C = a.shape
    return a.reshape(SSM_SEG, L // SSM_SEG, C).transpose(1, 0, 2).reshape(L, C)


def _seg_unperm(a):
    L, C = a.shape
    return a.reshape(L // SSM_SEG, SSM_SEG, C).transpose(1, 0, 2).reshape(L, C)


def _lam_rows(are_ref, aim_ref, d):
    a = [jnp.broadcast_to(are_ref[d, j:j + 1, :], (SSM_SEG, 128)) for j in range(SSM_NPAIR)]
    b = [jnp.broadcast_to(aim_ref[d, j:j + 1, :], (SSM_SEG, 128)) for j in range(SSM_NPAIR)]
    return a, b


_PAIR_SPEC = pl.BlockSpec((None, 2, SSM_NPAIR, 2, 2 * SSM_H, 128), lambda g: (g, 0, 0, 0, 0, 0))


def _pair_window(j, r):
    return slice(j * 2 * SSM_H, (j + 1) * 2 * SSM_H), slice(j * 256 + r * 128, j * 256 + (r + 1) * 128)


def _expand_pairs(c_ref, dense_ref):
    dense_ref[...] = jnp.zeros_like(dense_ref)
    for d in range(2):
        for j in range(SSM_NPAIR):
            for r in range(2):
                rows, cols = _pair_window(j, r)
                dense_ref[d, rows, cols] = c_ref[d, j, r].astype(dense_ref.dtype)


def _ssm_fwd(u_p, wb, wc, are, aim, dvec):
    L = u_p.shape[0]
    T = L // SSM_SEG
    RC = min(512, L)

    def body(u_ref, wb_ref, wc_ref, are_ref, aim_ref, d_ref, y_ref, ge_ref, entry_ref, x_scr, wb_s, wc_s):
        _expand_pairs(wb_ref, wb_s)
        _expand_pairs(wc_ref, wc_s)
        y_ref[...] = u_ref[...] * d_ref[...]
        for d in range(2):
            def bu_chunk(c, _):
                rows = pl.ds(pl.multiple_of(c * RC, RC), RC)
                x_scr[rows, :] = jnp.dot(u_ref[rows, :].astype(BF16), wb_s[d], preferred_element_type=F32)
                return 0
            lax.fori_loop(0, L // RC, bu_chunk, 0)
            a, b = _lam_rows(are_ref, aim_ref, d)
            entry_ref[d] = jnp.concatenate(_seg_scan(x_scr, a, b, T, reverse=(d == 1)), axis=1)

            def y_chunk(c, _):
                rows = pl.ds(pl.multiple_of(c * RC, RC), RC)
                y_ref[rows, :] += lax.dot_general(x_scr[rows, :].astype(BF16), wc_s[d], _QK_T, preferred_element_type=F32)
                return 0
            lax.fori_loop(0, L // RC, y_chunk, 0)
        ge_ref[...] = _gelu(y_ref[...]).astype(ge_ref.dtype)

    blk4 = lambda g: (g, 0, 0, 0)
    chan = pl.BlockSpec((L, SSM_BW), lambda g: (0, g))
    return pl.pallas_call(
        body, name="ssm_fwd",
        out_shape=[jax.ShapeDtypeStruct((L, D_SSM), F32), jax.ShapeDtypeStruct((L, D_SSM), BF16),
                   jax.ShapeDtypeStruct((SSM_NB, 2, SSM_SEG, SSM_SW), F32)],
        grid=(SSM_NB,),
        in_specs=[chan, _PAIR_SPEC, _PAIR_SPEC,
                  pl.BlockSpec((None, 2, SSM_NPAIR, 128), blk4),
                  pl.BlockSpec((None, 2, SSM_NPAIR, 128), blk4),
                  pl.BlockSpec((1, SSM_BW), lambda g: (0, g))],
        out_specs=[chan, chan, pl.BlockSpec((None, 2, SSM_SEG, SSM_SW), blk4)],
        scratch_shapes=[pltpu.VMEM((L, SSM_SW), F32), pltpu.VMEM((2, SSM_BW, SSM_SW), BF16), pltpu.VMEM((2, SSM_BW, SSM_SW), BF16)],
        compiler_params=_params(("parallel",)),
    )(u_p, wb, wc, are, aim, dvec)


def _ssm_bwd(u_p, dge_p, y_p, entry, wb, wc, are, aim, dvec, sib_ts, sib_grads):
    L = u_p.shape[0]
    T = L // SSM_SEG
    RC = min(512, L)
    n_sib = len(sib_ts)

    def lam_acc(acc, s, x):
        new = []
        for q in range(SSM_NPAIR):
            sr, si, xr, xi = s[2 * q], s[2 * q + 1], x[2 * q], x[2 * q + 1]
            new += [acc[2 * q] + sr * xr + si * xi, acc[2 * q + 1] + si * xr - sr * xi]
        return tuple(new)

    def body(u_ref, dge_ref, y_ref, entry_ref, wb_ref, wc_ref, are_ref, aim_ref, d_ref, *rest):
        du_ref, dwb_ref, dwc_ref, dare_ref, daim_ref, dd_ref = rest[n_sib:n_sib + 6]
        x_scr, s_scr, wb_s, wc_s, dwb_s, dwc_s, dy_ref = rest[2 * n_sib + 6:2 * n_sib + 13]
        sib = _SiblingExchange(sib_ts, rest[:n_sib], rest[n_sib + 6:2 * n_sib + 6], rest[2 * n_sib + 13:])
        pl.when(pl.program_id(0) == 0)(sib.start)
        dy_ref[...] = dge_ref[...] * _gelu_grad(y_ref[...])
        _expand_pairs(wb_ref, wb_s)
        _expand_pairs(wc_ref, wc_s)
        du_ref[...] = dy_ref[...] * d_ref[...]
        dd_ref[...] = _colsum(dy_ref[...] * u_ref[...])
        dwb_s[...] = jnp.zeros_like(dwb_s)
        dwc_s[...] = jnp.zeros_like(dwc_s)
        for d in range(2):
            rev = d == 1

            def in_chunk(c, _):
                rows = pl.ds(pl.multiple_of(c * RC, RC), RC)
                x_scr[rows, :] = jnp.dot(u_ref[rows, :].astype(BF16), wb_s[d], preferred_element_type=F32)
                s_scr[rows, :] = jnp.dot(dy_ref[rows, :].astype(BF16), wc_s[d], preferred_element_type=F32)
                return 0
            lax.fori_loop(0, L // RC, in_chunk, 0)
            a, b = _lam_rows(are_ref, aim_ref, d)
            x_in = _seg_scan(x_scr, a, b, T, reverse=rev, entry=[entry_ref[d, :, q * 128:(q + 1) * 128] for q in range(2 * SSM_NPAIR)])

            def pair_with_row(lt, s_new, acc):
                xrow = pl.multiple_of((lt + 1 if rev else lt - 1) * SSM_SEG, SSM_SEG)
                xb = x_scr[pl.ds(xrow, SSM_SEG), :]
                return lam_acc(acc, s_new, [xb[:, i * 128:(i + 1) * 128] for i in range(2 * SSM_NPAIR)])

            zeros = (jnp.zeros((SSM_SEG, 128), F32),) * (2 * SSM_NPAIR)
            _, acc = _seg_scan(s_scr, a, [-v for v in b], T, reverse=not rev,
                               tap=(pair_with_row, lambda lt, s_new, acc: lam_acc(acc, s_new, x_in), zeros))
            for q in range(SSM_NPAIR):
                dare_ref[d, q:q + 1, :] = _colsum(acc[2 * q])
                daim_ref[d, q:q + 1, :] = _colsum(acc[2 * q + 1])

            def out_chunk(c, _):
                rows = pl.ds(pl.multiple_of(c * RC, RC), RC)
                xs, ss = x_scr[rows, :].astype(BF16), s_scr[rows, :].astype(BF16)
                uu, dd = u_ref[rows, :].astype(BF16), dy_ref[rows, :].astype(BF16)
                dwc_s[d] += lax.dot_general(dd, xs, _TA, preferred_element_type=F32)
                dwb_s[d] += lax.dot_general(uu, ss, _TA, preferred_element_type=F32)
                du_ref[rows, :] += lax.dot_general(ss, wb_s[d], _QK_T, preferred_element_type=F32)
                return 0
            lax.fori_loop(0, L // RC, out_chunk, 0)
            for j in range(SSM_NPAIR):
                for r in range(2):
                    rows, cols = _pair_window(j, r)
                    dwb_ref[d, j, r] = dwb_s[d, rows, cols]
                    dwc_ref[d, j, r] = dwc_s[d, rows, cols]
        pl.when(pl.program_id(0) == SSM_NB - 1)(sib.finish)

    blk4 = lambda g: (g, 0, 0, 0)
    chan = pl.BlockSpec((L, SSM_BW), lambda g: (0, g))
    par_specs = [_PAIR_SPEC, _PAIR_SPEC,
                 pl.BlockSpec((None, 2, SSM_NPAIR, 128), blk4),
                 pl.BlockSpec((None, 2, SSM_NPAIR, 128), blk4),
                 pl.BlockSpec((1, SSM_BW), lambda g: (0, g))]
    dense = lambda dt: pltpu.VMEM((2, SSM_BW, SSM_SW), dt)
    return pl.pallas_call(
        body, name="ssm_bwd",
        out_shape=[jax.ShapeDtypeStruct((L, D_SSM), F32),
                   jax.ShapeDtypeStruct(wb.shape, F32),
                   jax.ShapeDtypeStruct(wc.shape, F32),
                   jax.ShapeDtypeStruct((SSM_NB, 2, SSM_NPAIR, 128), F32),
                   jax.ShapeDtypeStruct((SSM_NB, 2, SSM_NPAIR, 128), F32),
                   jax.ShapeDtypeStruct((1, D_SSM), F32)] + _SiblingExchange.out_shape(sib_ts),
        grid=(SSM_NB,),
        in_specs=[pl.BlockSpec((L, SSM_BW), lambda g: (0, g), pipeline_mode=pl.Buffered(1))] * 3 +
                 [pl.BlockSpec((None, 2, SSM_SEG, SSM_SW), blk4)] + par_specs + [_ANY] * n_sib,
        out_specs=[chan] + par_specs + [_ANY] * n_sib,
        scratch_shapes=[pltpu.VMEM((L, SSM_SW), F32), pltpu.VMEM((L, SSM_SW), F32), dense(BF16), dense(BF16), dense(F32), dense(F32),
                        pltpu.VMEM((L, SSM_BW), F32)] + _SiblingExchange.scratch(sib_ts),
        compiler_params=_params(("arbitrary",), vmem=VMEM_LIMIT_V7X + 2 * 1024 * 1024),
    )(u_p, dge_p, y_p, entry, wb, wc, are, aim, dvec, *sib_grads)


def _ssm_disc(a_re, a_im, log_dt, b_re, b_im):
    lam = lax.complex(jnp.minimum(a_re, -1e-4), a_im)
    dt = jnp.exp(log_dt)[..., None]
    lam_bar = jnp.exp(lam * dt)
    b_bar = ((lam_bar - 1.0) / lam)[..., None] * lax.complex(b_re, b_im)
    return jnp.real(lam_bar), jnp.imag(lam_bar), jnp.real(b_bar), jnp.imag(b_bar)


_EYE2 = np.eye(2, dtype=np.float32)[:, None, :, None]


def _to_pairs(t):
    t = t.reshape(2, SSM_NB, SSM_NPAIR, 2, 2, SSM_H, SSM_P).transpose(1, 0, 2, 4, 3, 5, 6)
    return (t[..., None, :] * _EYE2).reshape(SSM_NB, 2, SSM_NPAIR, 2, 2 * SSM_H, 2 * SSM_P)


def _from_pairs(c):
    t = jnp.sum(c.reshape(SSM_NB, 2, SSM_NPAIR, 2, 2, SSM_H, 2, SSM_P) * _EYE2, axis=6)
    return t.transpose(1, 0, 2, 4, 3, 5, 6).reshape(2, SSM_G, 2, SSM_H, SSM_P)


def _to_lam(v):
    return v.reshape(2, SSM_NB, SSM_NPAIR, 128).transpose(1, 0, 2, 3)


def _from_lam(v):
    return v.transpose(1, 0, 2, 3).reshape(2, SSM_G, SSM_P)


_MESH = pl.DeviceIdType.MESH
_ANY = pl.BlockSpec(memory_space=pl.ANY)
_BIG = (("w_in", (D_MODEL, D_IN), 1, D_IN // N_CHIPS),
        ("w_glu", (D_SSM, 2 * D_SSM), 1, 2 * D_SSM // N_CHIPS),
        ("w_out", (D_ATTN + D_SSM, D_MODEL), 0, (D_ATTN + D_SSM) // N_CHIPS),
        ("w_ple_gate", (D_MODEL, D_MODEL), 0, D_MODEL // N_CHIPS),
        ("w_ple_proj", (PLE_DIM, D_MODEL), 1, D_MODEL // N_CHIPS))


def _place():
    x, y, c = lax.axis_index("x"), lax.axis_index("y"), lax.axis_index("c")
    return x, y, c, [(1 - x, y), (x, 1 - y), (1 - x, 1 - y)]


class _Gather:
    def __init__(self, ts, srcs, dsts, stage, sems):
        self.ts, self.srcs, self.dsts, self.stage = ts, srcs, dsts, stage
        self.send_sems, self.recv_sems, self.fwd_send_sems, self.fwd_recv_sems, self.loc_sems = sems
        self.x, self.y, self.c, self.chips = _place()
        self.n = len(ts)

    @staticmethod
    def scratch(shards):
        sems = pltpu.SemaphoreType.DMA((3, len(shards)))
        return [pltpu.VMEM(s.shape, BF16) for s in shards] + [sems, sems, sems, sems, pltpu.SemaphoreType.DMA((len(shards),))]

    def _shard_of(self, i, kk):
        _, _, axis, sz = _BIG[self.ts[i]]
        sl = pl.ds(pl.multiple_of(kk * sz, sz), sz)
        return self.dsts[i].at[:, sl] if axis == 1 else self.dsts[i].at[sl, :]

    @staticmethod
    def _half_of(ref, cc):
        n = ref.shape[0] // 2
        return ref.at[pl.ds(pl.multiple_of(cc * n, n), n), :]

    def _ici(self, j, i, kk):
        px, py = self.chips[j]
        return pltpu.make_async_remote_copy(
            src_ref=self._half_of(self.srcs[i], self.c), dst_ref=self._half_of(self._shard_of(i, kk), self.c),
            send_sem=self.send_sems.at[j, i], recv_sem=self.recv_sems.at[j, i],
            device_id=(px, py, self.c), device_id_type=_MESH)

    def _forward(self, j, i, kk, cc):
        part = self._half_of(self._shard_of(i, kk), cc)
        return pltpu.make_async_remote_copy(
            src_ref=part, dst_ref=part, send_sem=self.fwd_send_sems.at[j, i], recv_sem=self.fwd_recv_sems.at[j, i],
            device_id=(self.x, self.y, 1 - self.c), device_id_type=_MESH)

    def _load(self, i):
        return pltpu.make_async_copy(self.srcs[i], self.stage[i], self.loc_sems.at[i])

    def _place_own(self, i):
        return pltpu.make_async_copy(self.stage[i], self._shard_of(i, 2 * self.x + self.y), self.loc_sems.at[i])

    def _peers(self):
        return [(i, j, 2 * px + py) for i in range(self.n) for j, (px, py) in enumerate(self.chips)]

    def start(self):
        for i in range(self.n):
            self._load(i).start()
        for i, j, _ in self._peers():
            self._ici(j, i, 2 * self.x + self.y).start()

    def forward(self):
        for i in range(self.n):
            self._load(i).wait()
            self._place_own(i).start()
        for i, j, kk in self._peers():
            self._ici(j, i, kk).wait_recv()
            self._forward(j, i, kk, self.c).start()

    def finish(self):
        for i, j, kk in self._peers():
            self._forward(j, i, kk, 1 - self.c).wait_recv()
        for i, j, kk in self._peers():
            self._ici(j, i, kk).wait_send()
            self._forward(j, i, kk, self.c).wait_send()
        for i in range(self.n):
            self._place_own(i).wait()


def _matmul_in_gather(a, shard, *, tm=1024):
    t = 0
    (_, (K, N), _, sz) = _BIG[t]
    M = a.shape[0]
    tm = min(tm, M)
    gm = M // tm
    x, y = lax.axis_index("x"), lax.axis_index("y")
    order = jnp.stack([2 * x + y, 2 * (1 - x) + y, 2 * x + 1 - y, 2 * (1 - x) + 1 - y]).astype(jnp.int32)

    def body(order_ref, a_ref, shard_ref, z_ref, full_ref, b_vm, *sems):
        g = _Gather([t], [shard_ref], [full_ref], [b_vm], sems[:5])
        load_sem = sems[5]
        s, i = pl.program_id(0), pl.program_id(1)

        @pl.when((s == 0) & (i == 0))
        def _():
            g.start()
            g._load(0).wait()
            g._place_own(0).start()

        for j, (px, py) in enumerate(g.chips):
            @pl.when((s == j + 1) & (i == 0))
            def _(j=j, kk=2 * px + py):
                if j == 0:
                    g._place_own(0).wait()
                g._ici(j, 0, kk).wait_recv()
                g._forward(j, 0, kk, g.c).start()
                g._forward(j, 0, kk, 1 - g.c).wait_recv()
                cp = pltpu.make_async_copy(g._shard_of(0, kk), b_vm, load_sem.at[0])
                cp.start()
                cp.wait()

        z_ref[...] = jnp.dot(a_ref[...], b_vm[...], preferred_element_type=F32)

        @pl.when((s == N_CHIPS - 1) & (i == gm - 1))
        def _():
            for j, (px, py) in enumerate(g.chips):
                g._ici(j, 0, 2 * px + py).wait_send()
                g._forward(j, 0, 2 * px + py, g.c).wait_send()

    return pl.pallas_call(
        body, name="mm_in",
        out_shape=[jax.ShapeDtypeStruct((M, N), F32), jax.ShapeDtypeStruct((K, N), BF16)],
        grid_spec=pltpu.PrefetchScalarGridSpec(
            num_scalar_prefetch=1, grid=(N_CHIPS, gm),
            in_specs=[pl.BlockSpec((tm, K), lambda s, i, o: (i, 0)), _ANY],
            out_specs=[pl.BlockSpec((tm, sz), lambda s, i, o: (i, o[s])), _ANY],
            scratch_shapes=_Gather.scratch([shard]) + [pltpu.SemaphoreType.DMA((1,))]),
        compiler_params=_params(("arbitrary", "arbitrary")),
    )(order, a, shard)


SMALL_W = 1024
SMALL_ROWS = 72
N_SMALL = 8 * SMALL_ROWS * SMALL_W
_RED = tuple((shape, ax, (shape[0] // 2, sz) if ax == 1 else (sz // 2, shape[1]), BF16) for _, shape, ax, sz in _BIG) + \
    (((8 * SMALL_ROWS, SMALL_W), 0, (SMALL_ROWS, SMALL_W), F32),)
_RED_TR = 128


def _piece(ref, t, kk, cc):
    _, ax, (pr, pc), _ = _RED[t]
    if ax == 1:
        return ref.at[pl.ds(pl.multiple_of(cc * pr, pr), pr), pl.ds(pl.multiple_of(kk * pc, pc), pc)]
    return ref.at[pl.ds(pl.multiple_of((2 * kk + cc) * pr, pr), pr), :]


def _half_shape(t):
    shape, ax, (pr, pc), _ = _RED[t]
    return (pr, shape[1]) if ax == 1 else (N_CHIPS * pr, pc)


def _piece_in_half(ref, t, kk):
    _, ax, (pr, pc), _ = _RED[t]
    return ref.at[:, pl.ds(pl.multiple_of(kk * pc, pc), pc)] if ax == 1 else ref.at[pl.ds(pl.multiple_of(kk * pr, pr), pr), :]


class _SiblingExchange:
    def __init__(self, ts, srcs, dsts, sems):
        send_sems, recv_sems = sems
        x, y, c, _ = _place()

        def copies():
            pairs = []
            for i, t in enumerate(ts):
                _, ax, (pr, _), _ = _RED[t]
                if ax == 1:
                    pairs.append((srcs[i].at[pl.ds(pl.multiple_of((1 - c) * pr, pr), pr), :], dsts[i]))
                else:
                    pairs += [(_piece(srcs[i], t, kk, 1 - c), _piece_in_half(dsts[i], t, kk)) for kk in range(N_CHIPS)]
            return [pltpu.make_async_remote_copy(src_ref=s, dst_ref=d, send_sem=send_sems.at[i], recv_sem=recv_sems.at[i],
                                                 device_id=(x, y, 1 - c), device_id_type=_MESH) for i, (s, d) in enumerate(pairs)]
        self.copies = copies

    @staticmethod
    def scratch(ts):
        n_dma = sum(1 if _RED[t][1] == 1 else N_CHIPS for t in ts)
        return [pltpu.SemaphoreType.DMA((n_dma,)), pltpu.SemaphoreType.DMA((n_dma,))]

    @staticmethod
    def out_shape(ts):
        return [jax.ShapeDtypeStruct(_half_shape(t), F32) for t in ts]

    def start(self):
        for cp in self.copies():
            cp.start()

    def finish(self):
        for cp in self.copies():
            cp.wait()


def _grad_sibling_exchange(ts, grads, name):
    n = len(ts)

    def body(*refs):
        ex = _SiblingExchange(ts, refs[:n], refs[n:2 * n], refs[2 * n:])
        ex.start()
        ex.finish()

    return pl.pallas_call(
        body, name=name,
        out_shape=_SiblingExchange.out_shape(ts),
        in_specs=[_ANY] * n, out_specs=[_ANY] * n,
        scratch_shapes=_SiblingExchange.scratch(ts),
    )(*grads)


def _chip_sum(t, g, rs, place):
    shape, ax, (pr, pc), dt = _RED[t]
    W = shape[1]
    tr = min(pr, _RED_TR)
    nb = pr // tr

    def body(place_ref, g_ref, rs_ref, o_ref):
        o_ref[...] = (g_ref[...] + rs_ref[...]).astype(o_ref.dtype)

    return pl.pallas_call(
        body, name="grad_chip_sum_%d" % t,
        out_shape=jax.ShapeDtypeStruct(rs.shape, dt),
        grid_spec=pltpu.PrefetchScalarGridSpec(
            num_scalar_prefetch=1, grid=(1 if ax == 1 else N_CHIPS, nb),
            in_specs=[pl.BlockSpec((tr, W), lambda kk, i, pr_: ((2 * kk + pr_[0]) * nb + i, 0)),
                      pl.BlockSpec((tr, W), lambda kk, i, pr_: (kk * nb + i, 0))],
            out_specs=pl.BlockSpec((tr, W), lambda kk, i, pr_: (kk * nb + i, 0))),
        compiler_params=_params(("parallel", "parallel")),
    )(place, g, rs)


class _ChipExchange:
    def __init__(self, ts, srcs, dsts, sems):
        self.send_sems, self.recv_sems = sems
        x, y, c, chips = _place()
        self.copies = lambda: [
            pltpu.make_async_remote_copy(src_ref=_piece_in_half(srcs[i], t, 2 * px + py), dst_ref=dsts[i].at[j],
                                         send_sem=self.send_sems.at[j, i], recv_sem=self.recv_sems.at[j, i],
                                         device_id=(px, py, c), device_id_type=_MESH)
            for i, t in enumerate(ts) for j, (px, py) in enumerate(chips)]

    @staticmethod
    def scratch(ts):
        return [pltpu.SemaphoreType.DMA((3, len(ts))), pltpu.SemaphoreType.DMA((3, len(ts)))]

    @staticmethod
    def out_shape(ts):
        return [jax.ShapeDtypeStruct((3,) + _RED[t][2], _RED[t][3]) for t in ts]

    def start(self):
        for cp in self.copies():
            cp.start()

    def finish(self):
        for cp in self.copies():
            cp.wait()


def _grad_chip_exchange(ts, sums):
    n = len(ts)

    def body(*refs):
        ex = _ChipExchange(ts, refs[:n], refs[n:2 * n], refs[2 * n:])
        ex.start()
        ex.finish()

    return pl.pallas_call(
        body, name="grad_chip_exchange",
        out_shape=_ChipExchange.out_shape(ts),
        in_specs=[_ANY] * n, out_specs=[_ANY] * n,
        scratch_shapes=_ChipExchange.scratch(ts),
    )(*sums)


def _total_sum(t, g, rs, rc, place):
    shape, ax, (pr, pc), _ = _RED[t]
    tr = min(pr, _RED_TR)
    nb = pr // tr
    small = t == len(_RED) - 1

    def body(place_ref, g_ref, rs_ref, rc_ref, o_ref):
        o_ref[...] = (g_ref[...] + rs_ref[...]) + rc_ref[0].astype(F32) + rc_ref[1].astype(F32) + rc_ref[2].astype(F32)

    if ax == 1:
        g_map = lambda i, pr_: (pr_[0] * nb + i, pr_[1])
        rs_map = lambda i, pr_: (i, pr_[1])
    else:
        g_map = lambda i, pr_: ((2 * pr_[1] + pr_[0]) * nb + i, 0)
        rs_map = lambda i, pr_: (pr_[1] * nb + i, 0)
    o_map = (lambda i, pr_: ((2 * pr_[1] + pr_[0]) * nb + i, 0)) if small else (lambda i, pr_: (pr_[0] * nb + i, 0))
    return pl.pallas_call(
        body, name="grad_total_sum_%d" % t,
        out_shape=jax.ShapeDtypeStruct(((8 if small else 2) * pr, pc), F32),
        grid_spec=pltpu.PrefetchScalarGridSpec(
            num_scalar_prefetch=1, grid=(nb,),
            in_specs=[pl.BlockSpec((tr, pc), g_map), pl.BlockSpec((tr, pc), rs_map),
                      pl.BlockSpec((3, tr, pc), lambda i, pr_: (0, i, 0))],
            out_specs=pl.BlockSpec((tr, pc), o_map)),
        compiler_params=_params(("parallel",)),
    )(place, g, rs, rc)


def _grad_final_exchange(totals):
    n = len(_RED)
    nb = n - 1

    def body(*refs):
        srcs, dsts, (send_sems, recv_sems) = refs[:n], refs[n:2 * n], refs[2 * n:]
        x, y, c, chips = _place()
        me = 4 * x + 2 * y + c
        others = [(x, y, 1 - c)] + [(px, py, cc) for (px, py) in chips for cc in (c, 1 - c)]

        def half(ref, t, cc):
            pr = _RED[t][2][0]
            return ref.at[pl.ds(pl.multiple_of(cc * pr, pr), pr), :]

        def eighth(ref, dev):
            return ref.at[pl.ds(pl.multiple_of(dev * SMALL_ROWS, SMALL_ROWS), SMALL_ROWS), :]

        def big_copy(t, cc):
            return pltpu.make_async_remote_copy(src_ref=half(srcs[t], t, cc), dst_ref=half(dsts[t], t, cc), send_sem=send_sems.at[t],
                                                recv_sem=recv_sems.at[t], device_id=others[0], device_id_type=_MESH)

        def small_copy(i, dev):
            return pltpu.make_async_remote_copy(src_ref=eighth(srcs[nb], dev), dst_ref=eighth(dsts[nb], dev),
                                                send_sem=send_sems.at[nb + i], recv_sem=recv_sems.at[nb + i],
                                                device_id=others[i], device_id_type=_MESH)

        sends = [big_copy(t, c) for t in range(nb)] + [small_copy(i, me) for i in range(7)]
        for cp in sends:
            cp.start()
        for t in range(nb):
            big_copy(t, 1 - c).wait_recv()
        for i, (px, py, pc) in enumerate(others):
            small_copy(i, 4 * px + 2 * py + pc).wait_recv()
        for cp in sends:
            cp.wait_send()

    return pl.pallas_call(
        body, name="grad_final_exchange",
        out_shape=[jax.ShapeDtypeStruct(a.shape, F32) for a in totals],
        in_specs=[_ANY] * n, out_specs=[_ANY] * n,
        input_output_aliases={t: t for t in range(n)},
        scratch_shapes=[pltpu.SemaphoreType.DMA((nb + 7,)), pltpu.SemaphoreType.DMA((nb + 7,))],
    )(*totals)


def _grad_place():
    return jnp.stack([lax.axis_index("c"), 2 * lax.axis_index("x") + lax.axis_index("y")]).astype(jnp.int32)


def _reduce_begin(ts, grads, place, tag, from_sibling=None):
    if from_sibling is None:
        from_sibling = _grad_sibling_exchange(ts, grads, "grad_sibling_exchange_" + tag)
    return from_sibling, [_chip_sum(t, g, r, place) for t, g, r in zip(ts, grads, from_sibling)]


def _reduce_end(ts, grads, from_sibling, from_chips, place):
    return [_total_sum(t, g, r, q, place) for t, g, r, q in zip(ts, grads, from_sibling, from_chips)]


_EARLY = (1, 2, 3, 4)
_W_IN = (0,)
_SMALL_RED = (5,)


def _adamw_math(w, g, m, v):
    m = ADAM_B1 * m + (1.0 - ADAM_B1) * g
    v = ADAM_B2 * v + (1.0 - ADAM_B2) * (g * g)
    m_hat = m / (1.0 - ADAM_B1 ** ADAM_STEP)
    v_hat = v / (1.0 - ADAM_B2 ** ADAM_STEP)
    return -ADAM_LR * (m_hat / (jnp.sqrt(v_hat) + ADAM_EPS) + ADAM_WD * w), m, v


def _adamw(w, g, m, v, name):
    W = w.shape[1]
    return _rowwise(_adamw_math, [(a, 0, W) for a in (w, g, m, v)], [], [(W, F32)] * 3, tr=128, name=name)


def _adamw_whole(ws, gs, ms, vs, name):
    n = len(ws)

    def body(*refs):
        ins, outs = refs[:4 * n], refs[4 * n:]
        for i in range(n):
            res = _adamw_math(*[ins[j * n + i][...] for j in range(4)])
            for j in range(3):
                outs[j * n + i][...] = res[j]

    res = pl.pallas_call(
        body, name=name,
        out_shape=[jax.ShapeDtypeStruct(a.shape, F32) for a in ws] * 3,
        compiler_params=pltpu.CompilerParams(vmem_limit_bytes=VMEM_LIMIT_V7X),
    )(*ws, *gs, *ms, *vs)
    return res[:n], res[n:2 * n], res[2 * n:]


def _chunks(arr, off, width, w=512):
    return [(arr, off + i * w, w) for i in range(width // w)]


def _cat(vs):
    return jnp.concatenate(vs, axis=1)


def _forward_backward(x, p_b, tgt, shards, small):
    L = x.shape[0]
    row = lambda v: v.reshape(1, -1)
    g_mix, g_ple, g_fin = row(small["norm_mix"]), row(small["norm_ple"]), row(small["norm_final"])
    gq, gk, b_glu = row(small["q_norm"]), row(small["k_norm"]), row(small["b_glu"])
    cos, sin = _rope_tables(L)

    hn_b, = _rowwise(lambda x, g: x * _rms(x) * g, [(x, 0, D_MODEL)], [g_mix], [(D_MODEL, BF16)], name="norm_mix")
    z, w_in = _matmul_in_gather(hn_b, shards[0])
    qr, kr, vb = _attn_prep(z, gq, gk, cos, sin)
    o, lse, w_glu, w_out, w_pg, w_pp = _attn_fwd(qr, kr, vb, [1, 2, 3, 4], shards[1:])

    ssm_names = ("ssm_a_re", "ssm_a_im", "ssm_log_dt", "ssm_b_re", "ssm_b_im")
    (lre, lim, bre, bim), disc_vjp = jax.vjp(_ssm_disc, *[small[n][0] for n in ssm_names])
    ssm = (_to_pairs(jnp.swapaxes(jnp.stack([bre, bim], axis=2), -1, -2)),
           _to_pairs(jnp.stack([small["ssm_c_re"][0], -small["ssm_c_im"][0]], axis=2)),
           _to_lam(lre), _to_lam(lim), row(small["ssm_d"]))
    u_p = _seg_perm(z[:, Z_U:Z_U + D_SSM])
    y_p, ge_p, ssm_entry = _ssm_fwd(u_p, *ssm)
    ge_b = _seg_unperm(ge_p)
    glu = _matmul(ge_b, w_glu, name="mm_glu")

    def merge(ga0, ga1, a, b, gs0, gs1, o, bias):
        sa, _ = _silu_and_grad(_cat([ga0, ga1]))
        ss, _ = _silu_and_grad(_cat([gs0, gs1]))
        y2 = (a + bias[:, :D_SSM]) * _sig(b + bias[:, D_SSM:])
        return _cat([o * sa, y2 * ss])
    merge_rows = _chunks(z, Z_GA, D_ATTN) + [(glu, 0, D_SSM), (glu, D_SSM, D_SSM)] + _chunks(z, Z_GS, D_SSM) + [(o, 0, D_ATTN)]
    cat_b, = _rowwise(merge, merge_rows, [b_glu], [(D_MODEL, BF16)], name="merge")
    t_out = _matmul(cat_b, w_out, name="mm_out")

    def resid(x, t, g):
        h1 = x + t
        return h1, h1 * _rms(h1) * g
    h1, hp_b = _rowwise(resid, [(x, 0, D_MODEL), (t_out, 0, D_MODEL)], [g_ple], [(D_MODEL, F32), (D_MODEL, BF16)], name="resid_norm")
    gl = _matmul(hp_b, w_pg, name="mm_ple_gate")
    pp = _matmul(p_b, w_pp, name="mm_ple_proj")

    def head(h1, gl, pp, tgt, g):
        gate = _sig(gl)
        h2 = h1 + gate * pp
        r = _rms(h2)
        n = h2 * r
        err = n * g - tgt
        dy = err * (1.0 / D_MODEL)
        dn = dy * g
        dh2 = r * (dn - n * jnp.mean(dn * n, axis=-1, keepdims=True))
        dgate = dh2 * pp
        return dh2, dh2 * gate, dgate * gate * (1.0 - gate), _colsum(dy * n), _colsum(0.5 * err * err * (1.0 / D_MODEL))
    dh2, dpp_b, dgl_b, dg_fin, loss_cols = _rowwise(
        head, [(a, 0, D_MODEL) for a in (h1, gl, pp, tgt)], [g_fin],
        [(D_MODEL, F32), (D_MODEL, BF16), (D_MODEL, BF16)], [(1, D_MODEL), (1, D_MODEL)], name="loss_head")

    dw_pp = _matmul(p_b, dpp_b, ta=True, name="mm_d_w_ple_proj")
    dw_pg = _matmul(hp_b, dgl_b, ta=True, name="mm_d_w_ple_gate")
    dhp = _matmul(dgl_b, w_pg, tb=True, name="mm_d_hp")

    def resid_bwd(dhp, h1, dh2, g):
        dx, dg = _rms_bwd(h1, g, dhp)
        dh1 = dh2 + dx
        return dh1, dh1, _colsum(dg)
    dh1, dh1_b, dg_ple = _rowwise(resid_bwd, [(a, 0, D_MODEL) for a in (dhp, h1, dh2)], [g_ple],
                                  [(D_MODEL, F32), (D_MODEL, BF16)], [(1, D_MODEL)], name="resid_norm_bwd")
    dw_out = _matmul(cat_b, dh1_b, ta=True, name="mm_d_w_out")
    dcat = _matmul(dh1_b, w_out, tb=True, name="mm_d_cat")

    def merge_bwd(dya, dys, ga0, ga1, a, b, gs0, gs1, o, bias):
        ga, gs = _cat([ga0, ga1]), _cat([gs0, gs1])
        sa, dsa = _silu_and_grad(ga)
        ss, dss = _silu_and_grad(gs)
        a, sb = a + bias[:, :D_SSM], _sig(b + bias[:, D_SSM:])
        dy2 = dys * ss
        dglu = _cat([dy2 * sb, dy2 * a * sb * (1.0 - sb)])
        do = dya * sa
        lane = lax.broadcasted_iota(jnp.int32, (do.shape[0], HEAD_DIM), 1)
        delta = sum(jnp.where(lane == i, jnp.sum(h, axis=1, keepdims=True), 0.0)
                    for i, h in enumerate(_heads(do * o)))
        return do, dya * o * dsa, dys * (a * sb) * dss, dglu, delta, _colsum(dglu)
    do_b, dga_b, dgs_b, dglu_b, delta, db_glu = _rowwise(
        merge_bwd, [(dcat, 0, D_ATTN), (dcat, D_ATTN, D_SSM)] + merge_rows, [b_glu],
        [(D_ATTN, BF16), (D_ATTN, BF16), (D_SSM, BF16), (2 * D_SSM, BF16), (HEAD_DIM, F32)], [(1, 2 * D_SSM)], name="merge_bwd")
    dw_glu = _matmul(ge_b, dglu_b, ta=True, name="mm_d_w_glu")
    dge = _matmul(dglu_b, w_glu, tb=True, name="mm_d_ge")
    place = _grad_place()
    early_grads = [dw_glu, dw_out, dw_pg, dw_pp]
    du_p, dwb, dwc, dare, daim, d_ssm_d, *early_sib = _ssm_bwd(u_p, _seg_perm(dge), y_p, ssm_entry, *ssm, _EARLY, early_grads)
    dcc, dbb = _from_pairs(dwc), jnp.swapaxes(_from_pairs(dwb), -1, -2)
    dc_re, dc_im = dcc[:, :, 0], -dcc[:, :, 1]
    da_re, da_im, dlog_dt, db_re, db_im = disc_vjp((_from_lam(dare), _from_lam(daim), dbb[:, :, 0], dbb[:, :, 1]))

    early_sib, early_sums = _reduce_begin(_EARLY, early_grads, place, "early", from_sibling=early_sib)
    dqr, dkr, dv, *early_chips = _attn_bwd(qr, kr, kr.T, vb, do_b, lse.reshape(N_HEADS, 1, L), delta[:, :N_HEADS].T.reshape(N_HEADS, 1, L),
                                           _EARLY, early_sums)
    early_totals = _reduce_end(_EARLY, early_grads, early_sib, early_chips, place)
    dq_b, dk_b, dgq, dgk = _attn_prep_bwd(dqr, dkr, z, gq, gk, cos, sin)
    dz_b = _cat([dq_b, dk_b, dv.astype(BF16), dga_b, _seg_unperm(du_p).astype(BF16), dgs_b])
    dw_in = _matmul(hn_b, dz_b, ta=True, name="mm_d_w_in")
    w_in_sib, w_in_sums = _reduce_begin(_W_IN, [dw_in], place, "w_in")
    dhn, *w_in_chips = _matmul(dz_b, w_in, tb=True, name="mm_d_hn", exchange=(_W_IN, w_in_sums))
    w_in_total, = _reduce_end(_W_IN, [dw_in], w_in_sib, w_in_chips, place)

    def norm_bwd(dhn, x, dh1, g):
        dx, dg = _rms_bwd(x, g, dhn)
        return dh1 + dx, _colsum(dg)
    grad_x, dg_mix = _rowwise(norm_bwd, [(a, 0, D_MODEL) for a in (dhn, x, dh1)], [g_mix], [(D_MODEL, F32)], [(1, D_MODEL)],
                              name="norm_mix_bwd")

    small_grads = {"norm_mix": dg_mix, "q_norm": dgq, "k_norm": dgk, "ssm_a_re": da_re, "ssm_a_im": da_im, "ssm_log_dt": dlog_dt,
                   "ssm_b_re": db_re, "ssm_b_im": db_im, "ssm_c_re": dc_re, "ssm_c_im": dc_im, "ssm_d": d_ssm_d,
                   "b_glu": db_glu, "norm_ple": dg_ple, "norm_final": dg_fin}
    return jnp.sum(loss_cols), grad_x, [w_in_total] + early_totals, small_grads, place


_SMALL = ("norm_mix", "q_norm", "k_norm", "ssm_a_re", "ssm_a_im", "ssm_log_dt", "ssm_b_re", "ssm_b_im", "ssm_c_re", "ssm_c_im",
          "ssm_d", "b_glu", "norm_ple", "norm_final")
_WEIGHTS = ("norm_mix", "w_in", "q_norm", "k_norm", "ssm_a_re", "ssm_a_im", "ssm_log_dt", "ssm_b_re", "ssm_b_im", "ssm_c_re",
            "ssm_c_im", "ssm_d", "w_glu", "b_glu", "w_out", "norm_ple", "w_ple_gate", "w_ple_proj", "norm_final")


_SMALL_ADAMW_GROUPS = (("ssm_b_re",), ("ssm_b_im",), ("ssm_c_re", "ssm_c_im"),
                       ("norm_mix", "q_norm", "k_norm", "ssm_a_re", "ssm_a_im", "ssm_log_dt", "ssm_d", "b_glu", "norm_ple",
                        "norm_final"))


def _flat_small(d):
    flat = jnp.concatenate([d[n].reshape(-1).astype(F32) for n in _SMALL])
    return jnp.pad(flat, (0, N_SMALL - flat.shape[0]))


def _split_small(flat, like):
    out, off = {}, 0
    for n in _SMALL:
        sz = math.prod(like[n].shape)
        out[n] = flat[off:off + sz].reshape(like[n].shape)
        off += sz
    return out


def kernel(x, p, norm_mix, w_in, q_norm, k_norm, ssm_a_re, ssm_a_im, ssm_log_dt, ssm_b_re, ssm_b_im, ssm_c_re, ssm_c_im, ssm_d, w_glu, b_glu, w_out, norm_ple, w_ple_gate, w_ple_proj, norm_final, loss_target, m_norm_mix, m_w_in, m_q_norm, m_k_norm, m_ssm_a_re, m_ssm_a_im, m_ssm_log_dt, m_ssm_b_re, m_ssm_b_im, m_ssm_c_re, m_ssm_c_im, m_ssm_d, m_w_glu, m_b_glu, m_w_out, m_norm_ple, m_w_ple_gate, m_w_ple_proj, m_norm_final, v_norm_mix, v_w_in, v_q_norm, v_k_norm, v_ssm_a_re, v_ssm_a_im, v_ssm_log_dt, v_ssm_b_re, v_ssm_b_im, v_ssm_c_re, v_ssm_c_im, v_ssm_d, v_w_glu, v_b_glu, v_w_out, v_norm_ple, v_w_ple_gate, v_w_ple_proj, v_norm_final):
    w = dict(norm_mix=norm_mix, w_in=w_in, q_norm=q_norm, k_norm=k_norm, ssm_a_re=ssm_a_re, ssm_a_im=ssm_a_im,
             ssm_log_dt=ssm_log_dt, ssm_b_re=ssm_b_re, ssm_b_im=ssm_b_im, ssm_c_re=ssm_c_re, ssm_c_im=ssm_c_im, ssm_d=ssm_d,
             w_glu=w_glu, b_glu=b_glu, w_out=w_out, norm_ple=norm_ple, w_ple_gate=w_ple_gate, w_ple_proj=w_ple_proj,
             norm_final=norm_final)
    m = dict(norm_mix=m_norm_mix, w_in=m_w_in, q_norm=m_q_norm, k_norm=m_k_norm, ssm_a_re=m_ssm_a_re, ssm_a_im=m_ssm_a_im,
             ssm_log_dt=m_ssm_log_dt, ssm_b_re=m_ssm_b_re, ssm_b_im=m_ssm_b_im, ssm_c_re=m_ssm_c_re, ssm_c_im=m_ssm_c_im,
             ssm_d=m_ssm_d, w_glu=m_w_glu, b_glu=m_b_glu, w_out=m_w_out, norm_ple=m_norm_ple, w_ple_gate=m_w_ple_gate,
             w_ple_proj=m_w_ple_proj, norm_final=m_norm_final)
    v = dict(norm_mix=v_norm_mix, w_in=v_w_in, q_norm=v_q_norm, k_norm=v_k_norm, ssm_a_re=v_ssm_a_re, ssm_a_im=v_ssm_a_im,
             ssm_log_dt=v_ssm_log_dt, ssm_b_re=v_ssm_b_re, ssm_b_im=v_ssm_b_im, ssm_c_re=v_ssm_c_re, ssm_c_im=v_ssm_c_im,
             ssm_d=v_ssm_d, w_glu=v_w_glu, b_glu=v_b_glu, w_out=v_w_out, norm_ple=v_norm_ple, w_ple_gate=v_w_ple_gate,
             w_ple_proj=v_w_ple_proj, norm_final=v_norm_final)
    big_names = [n for n, _, _, _ in _BIG]

    small = {n: w[n] for n in _SMALL}
    loss_part, grad_x, big_totals, small_grads, place = _forward_backward(
        x[0], p[0, 0].astype(BF16), loss_target[0], [w[n][0].astype(BF16) for n in big_names], small)
    loss = lax.psum(loss_part, ("x", "y", "c"))

    small_flat = [_flat_small(small_grads).reshape(8 * SMALL_ROWS, SMALL_W)]
    small_sib, small_sums = _reduce_begin(_SMALL_RED, small_flat, place, "small")
    small_total = _reduce_end(_SMALL_RED, small_flat, small_sib, _grad_chip_exchange(_SMALL_RED, small_sums), place)
    *big_red, small_red = _grad_final_exchange(big_totals + small_total)
    grads = _split_small(small_red.reshape(-1), w)
    delta, new_m, new_v = {}, {}, {}
    for n, g in zip(big_names, big_red):
        grads[n] = g[None]
        d_, m_, v_ = _adamw(w[n][0], g, m[n][0], v[n][0], "adamw_" + n)
        delta[n], new_m[n], new_v[n] = d_[None], m_[None], v_[None]
    at_least_2d = lambda a: a.reshape(1, -1) if a.ndim == 1 else a
    for i, names in enumerate(_SMALL_ADAMW_GROUPS):
        d_, m_, v_ = _adamw_whole(*[[at_least_2d(src[n]) for n in names] for src in (w, grads, m, v)], "adamw_small_%d" % i)
        for j, n in enumerate(names):
            delta[n], new_m[n], new_v[n] = (a[j].reshape(w[n].shape) for a in (d_, m_, v_))
    return (loss, grad_x[None], *[grads[n] for n in _WEIGHTS], *[delta[n] for n in _WEIGHTS],
            *[new_m[n] for n in _WEIGHTS], *[new_v[n] for n in _WEIGHTS])
```

```python
import functools
import math

import jax
import jax.numpy as jnp
import numpy as np
from jax import lax
from jax.experimental import pallas as pl
from jax.experimental.pallas import tpu as pltpu

D_MODEL = 2048
GRID_W = 64
PLE_DIM = 256
D_ATTN = 1024
N_HEADS = 8
N_KV = 2
HEAD_DIM = 128
ROPE_THETA = 10000.0
D_SSM = 1024
SSM_H = 16
SSM_G = 64
SSM_P = 64
D_KV = N_KV * HEAD_DIM
D_IN = 2 * D_ATTN + 2 * D_KV + 2 * D_SSM
EPS = 1e-6
Z_Q, Z_K, Z_V, Z_GA, Z_U, Z_GS = 0, 1024, 1280, 1536, 2560, 3584

ADAM_LR, ADAM_B1, ADAM_B2, ADAM_EPS, ADAM_WD, ADAM_STEP = 0.001, 0.9, 0.999, 1e-08, 0.01, 10

N_CHIPS = 4
VMEM_LIMIT_V7X = 56 * 1024 * 1024
F32 = jnp.float32
BF16 = jnp.bfloat16


def _params(sem, vmem=VMEM_LIMIT_V7X):
    return pltpu.CompilerParams(dimension_semantics=sem, vmem_limit_bytes=vmem)


def _matmul(a, b, *, ta=False, tb=False, out_dtype=F32, tm=1024, tn=None, name, exchange=None):
    M, K = (a.shape[1], a.shape[0]) if ta else a.shape
    N = b.shape[0] if tb else b.shape[1]
    if tn is None:
        tn = 1024 if (N % 1024 == 0 and K <= 4096) else 512
    tm, tn = min(tm, M), min(tn, N)
    assert M % tm == 0 and N % tn == 0, (name, M, N, K)
    dims = (((0 if ta else 1,), (1 if tb else 0,)), ((), ()))
    ex_ts, ex_sums = exchange if exchange else ((), ())
    n_ex = len(ex_ts)
    gm, gn = M // tm, N // tn

    def body(a_ref, b_ref, *rest):
        o_ref = rest[n_ex]
        if n_ex:
            ex = _ChipExchange(ex_ts, rest[:n_ex], rest[n_ex + 1:2 * n_ex + 1], rest[2 * n_ex + 1:])
            step = pl.program_id(0) * gn + pl.program_id(1)
            pl.when(step == 0)(ex.start)
        o_ref[...] = lax.dot_general(a_ref[...], b_ref[...], dims, preferred_element_type=F32).astype(o_ref.dtype)
        if n_ex:
            pl.when(step == gm * gn - 1)(ex.finish)

    a_spec = pl.BlockSpec((K, tm), lambda i, j: (0, i)) if ta else pl.BlockSpec((tm, K), lambda i, j: (i, 0))
    b_spec = pl.BlockSpec((tn, K), lambda i, j: (j, 0)) if tb else pl.BlockSpec((K, tn), lambda i, j: (0, j))
    res = pl.pallas_call(
        body, name=name,
        out_shape=[jax.ShapeDtypeStruct((M, N), out_dtype)] + (_ChipExchange.out_shape(ex_ts) if n_ex else []),
        grid=(gm, gn),
        in_specs=[a_spec, b_spec] + [_ANY] * n_ex,
        out_specs=[pl.BlockSpec((tm, tn), lambda i, j: (i, j))] + [_ANY] * n_ex,
        scratch_shapes=_ChipExchange.scratch(ex_ts) if n_ex else [],
        compiler_params=_params(("arbitrary", "arbitrary") if n_ex else ("parallel", "parallel")),
    )(a, b, *ex_sums)
    return res if n_ex else res[0]


def _rowwise(fn, rows, consts, outs, accs=(), *, tr=256, name):
    L = rows[0][0].shape[0]
    tr = math.gcd(tr, L)
    assert tr % 8 == 0 or tr == L, (name, L, tr)
    n_in, n_c, n_o, n_a = len(rows), len(consts), len(outs), len(accs)

    def body(*refs):
        ins = [r[...] for r in refs[:n_in + n_c]]
        res = fn(*ins)
        if not isinstance(res, (tuple, list)):
            res = (res,)
        o_refs = refs[n_in + n_c:n_in + n_c + n_o]
        a_refs = refs[n_in + n_c + n_o:]
        for r, v in zip(o_refs, res[:n_o]):
            r[...] = v.astype(r.dtype)
        if n_a:
            first = pl.program_id(0) == 0

            @pl.when(first)
            def _():
                for r, v in zip(a_refs, res[n_o:]):
                    r[...] = v.astype(F32)

            @pl.when(jnp.logical_not(first))
            def _():
                for r, v in zip(a_refs, res[n_o:]):
                    r[...] += v.astype(F32)

    in_specs = []
    for arr, off, w in rows:
        assert off % w == 0, (name, off, w)
        in_specs.append(pl.BlockSpec((tr, w), functools.partial(lambda i, c: (i, c), c=off // w)))
    for c in consts:
        in_specs.append(pl.BlockSpec(c.shape, lambda i: (0, 0)))
    out_shape = [jax.ShapeDtypeStruct((L, w), dt) for w, dt in outs] + [jax.ShapeDtypeStruct(s, F32) for s in accs]
    out_specs = [pl.BlockSpec((tr, w), lambda i: (i, 0)) for w, _ in outs] + [pl.BlockSpec(s, lambda i: (0, 0)) for s in accs]
    res = pl.pallas_call(
        body, name=name,
        out_shape=out_shape,
        grid=(L // tr,),
        in_specs=in_specs,
        out_specs=out_specs,
        compiler_params=_params(("arbitrary",) if n_a else ("parallel",)),
    )(*[r[0] for r in rows], *consts)
    return res


def _sig(x):
    return jax.nn.sigmoid(x)


def _silu_and_grad(x):
    s = _sig(x)
    return x * s, s * (1.0 + x * (1.0 - s))


_GELU_C = math.sqrt(2.0 / math.pi)


def _gelu(x):
    return 0.5 * x * (1.0 + jnp.tanh(_GELU_C * (x + 0.044715 * x * x * x)))


def _gelu_grad(x):
    t = jnp.tanh(_GELU_C * (x + 0.044715 * x * x * x))
    return 0.5 * (1.0 + t) + 0.5 * x * (1.0 - t * t) * _GELU_C * (1.0 + 3.0 * 0.044715 * x * x)


def _rms(x):
    return lax.rsqrt(jnp.mean(x * x, axis=-1, keepdims=True) + EPS)


def _rms_bwd(x, g, dy):
    r = _rms(x)
    n = x * r
    dn = dy * g
    return r * (dn - n * jnp.mean(dn * n, axis=-1, keepdims=True)), dy * n


def _colsum(v):
    return jnp.sum(v, axis=0, keepdims=True)


def _rope_partner(x):
    lane = lax.broadcasted_iota(jnp.int32, x.shape, x.ndim - 1)
    return jnp.where(lane % 64 < 32, pltpu.roll(x, 96, x.ndim - 1), pltpu.roll(x, 32, x.ndim - 1))


def _rope_tables(L):
    rows_n = L // GRID_W
    rows = jnp.repeat(jnp.arange(rows_n), GRID_W).astype(F32)
    cols = jnp.tile(jnp.arange(GRID_W), rows_n).astype(F32)
    n_freq = HEAD_DIM // 4
    inv_freq = ROPE_THETA ** (-jnp.arange(n_freq, dtype=F32) / n_freq)
    ar, ac = rows[:, None] * inv_freq[None, :], cols[:, None] * inv_freq[None, :]
    cos = jnp.concatenate([jnp.cos(ar), jnp.cos(ar), jnp.cos(ac), jnp.cos(ac)], axis=-1)
    sin = jnp.concatenate([-jnp.sin(ar), jnp.sin(ar), -jnp.sin(ac), jnp.sin(ac)], axis=-1)
    return cos, sin


def _heads(v):
    return [v[:, h * HEAD_DIM:(h + 1) * HEAD_DIM] for h in range(v.shape[1] // HEAD_DIM)]


def _attn_prep(z, q_norm, k_norm, cos, sin):
    def fn(q, k, v, cos, sin, gq, gk):
        def one(xh, g):
            xn = xh * _rms(xh) * g
            return xn * cos + _rope_partner(xn) * sin
        qr = jnp.concatenate([one(h, gq) for h in _heads(q)], axis=1)
        kr = jnp.concatenate([one(h, gk) for h in _heads(k)], axis=1)
        return qr, kr, v
    return _rowwise(fn, [(z, Z_Q, D_ATTN), (z, Z_K, D_KV), (z, Z_V, D_KV), (cos, 0, HEAD_DIM), (sin, 0, HEAD_DIM)],
                    [q_norm, k_norm], [(D_ATTN, BF16), (D_KV, BF16), (D_KV, BF16)], name="attn_prep")


def _attn_prep_bwd(dqr, dkr, z, q_norm, k_norm, cos, sin):
    def fn(dqr, dkr, q, k, cos, sin, gq, gk):
        def one(dyh, xh, g):
            dn = dyh * cos + _rope_partner(dyh * sin)
            return _rms_bwd(xh, g, dn)
        rq = [one(a, b, gq) for a, b in zip(_heads(dqr), _heads(q))]
        rk = [one(a, b, gk) for a, b in zip(_heads(dkr), _heads(k))]
        dq = jnp.concatenate([r[0] for r in rq], axis=1)
        dk = jnp.concatenate([r[0] for r in rk], axis=1)
        return dq, dk, _colsum(sum(r[1] for r in rq)), _colsum(sum(r[1] for r in rk))
    return _rowwise(fn, [(dqr, 0, D_ATTN), (dkr, 0, D_KV), (z, Z_Q, D_ATTN), (z, Z_K, D_KV), (cos, 0, HEAD_DIM), (sin, 0, HEAD_DIM)],
                    [q_norm, k_norm], [(D_ATTN, BF16), (D_KV, BF16)], [(1, HEAD_DIM), (1, HEAD_DIM)], name="attn_prep_bwd")


_QK_T = (((1,), (1,)), ((), ()))
_TA = (((0,), (0,)), ((), ()))
_REP = N_HEADS // N_KV


_EXP2_SCALE = HEAD_DIM ** -0.5 * math.log2(math.e)
ATTN_FWD_KEY_CHUNKS = 4
ATTN_BWD_KEY_CHUNKS = 8


def _attn_fwd(qr, kr, vb, g_ts, g_shards, *, tq=1024):
    L = qr.shape[0]
    tq = min(tq, L)
    kc = L // ATTN_FWD_KEY_CHUNKS
    n_g = len(g_ts)
    grid = (N_HEADS, L // tq)
    steps = grid[0] * grid[1]

    def body(q_ref, k_ref, v_ref, *rest):
        o_ref, lse_ref = rest[n_g:n_g + 2]
        g = _Gather(g_ts, rest[:n_g], rest[n_g + 2:2 * n_g + 2], rest[2 * n_g + 2:3 * n_g + 2], rest[3 * n_g + 2:])
        step = pl.program_id(0) * grid[1] + pl.program_id(1)
        pl.when(step == 0)(g.start)
        pl.when(step == (3 * steps) // 4)(g.forward)
        q = q_ref[...]
        m = jnp.full((tq, 1), -jnp.inf, F32)
        l = jnp.zeros((tq, 1), F32)
        o = jnp.zeros((tq, HEAD_DIM), F32)
        for c in range(ATTN_FWD_KEY_CHUNKS):
            ks = slice(c * kc, (c + 1) * kc)
            s = lax.dot_general(q, k_ref[ks, :], _QK_T, preferred_element_type=F32)
            m_new = jnp.maximum(m, jnp.max(s, axis=1, keepdims=True))
            a = jnp.exp2((m - m_new) * _EXP2_SCALE)
            p = jnp.exp2((s - m_new) * _EXP2_SCALE)
            l = a * l + jnp.sum(p, axis=1, keepdims=True)
            o = a * o + jnp.dot(p.astype(BF16), v_ref[ks, :], preferred_element_type=F32)
            m = m_new
        o_ref[...] = o * (1.0 / l)
        lse_ref[...] = m * _EXP2_SCALE + jnp.log2(l)
        pl.when(step == steps - 1)(g.finish)

    kv = pl.BlockSpec((L, HEAD_DIM), lambda h, i: (0, h // _REP))
    return pl.pallas_call(
        body, name="attn_fwd",
        out_shape=[jax.ShapeDtypeStruct((L, D_ATTN), F32), jax.ShapeDtypeStruct((N_HEADS, L, 1), F32)] +
                  [jax.ShapeDtypeStruct(_BIG[t][1], BF16) for t in g_ts],
        grid=grid,
        in_specs=[pl.BlockSpec((tq, HEAD_DIM), lambda h, i: (i, h)), kv, kv] + [_ANY] * n_g,
        out_specs=[pl.BlockSpec((tq, HEAD_DIM), lambda h, i: (i, h)),
                   pl.BlockSpec((None, tq, 1), lambda h, i: (h, i, 0))] + [_ANY] * n_g,
        scratch_shapes=_Gather.scratch(g_shards),
        compiler_params=_params(("arbitrary", "arbitrary")),
    )(qr, kr, vb, *g_shards)


def _attn_bwd(qr, kr, k_t, vb, do, lse, delta, ex_ts, ex_sums, *, tq=1024):
    L = qr.shape[0]
    tq = min(tq, L)
    scale = HEAD_DIM ** -0.5
    kc = L // ATTN_BWD_KEY_CHUNKS
    n_ex = len(ex_ts)
    grid = (N_KV, _REP, L // tq)

    def body(q_ref, k_ref, kt_ref, v_ref, do_ref, lse_ref, delta_ref, *rest):
        dq_ref, dk_ref, dv_ref = rest[n_ex:n_ex + 3]
        ex = _ChipExchange(ex_ts, rest[:n_ex], rest[n_ex + 3:2 * n_ex + 3], rest[2 * n_ex + 3:])
        step = (pl.program_id(0) * grid[1] + pl.program_id(1)) * grid[2] + pl.program_id(2)
        pl.when(step == 0)(ex.start)

        @pl.when((pl.program_id(1) == 0) & (pl.program_id(2) == 0))
        def _():
            dk_ref[...] = jnp.zeros_like(dk_ref)
            dv_ref[...] = jnp.zeros_like(dv_ref)

        q, do, lse, delta = q_ref[...], do_ref[...], lse_ref[...], delta_ref[...]
        dq_t = 0.0
        for c in range(ATTN_BWD_KEY_CHUNKS):
            ks = slice(c * kc, (c + 1) * kc)
            st = lax.dot_general(k_ref[ks, :], q, _QK_T, preferred_element_type=F32)
            p = jnp.exp2(st * _EXP2_SCALE - lse)
            dv_ref[ks, :] += jnp.dot(p.astype(BF16), do, preferred_element_type=F32)
            dp = lax.dot_general(v_ref[ks, :], do, _QK_T, preferred_element_type=F32)
            ds = (p * (dp - delta) * scale).astype(BF16)
            dk_ref[ks, :] += jnp.dot(ds, q, preferred_element_type=F32)
            dq_t = dq_t + jnp.dot(kt_ref[:, ks], ds, preferred_element_type=F32)
        dq_ref[...] = dq_t.T
        pl.when(step == grid[0] * grid[1] * grid[2] - 1)(ex.finish)

    head = lambda g, r, i: (i, g * _REP + r)
    kv = pl.BlockSpec((L, HEAD_DIM), lambda g, r, i: (0, g))
    per_query = pl.BlockSpec((None, 1, tq), lambda g, r, i: (g * _REP + r, 0, i))
    return pl.pallas_call(
        body, name="attn_bwd",
        out_shape=[jax.ShapeDtypeStruct((L, D_ATTN), F32), jax.ShapeDtypeStruct((L, D_KV), F32), jax.ShapeDtypeStruct((L, D_KV), F32)] +
                  _ChipExchange.out_shape(ex_ts),
        grid=grid,
        in_specs=[pl.BlockSpec((tq, HEAD_DIM), head), kv,
                  pl.BlockSpec((HEAD_DIM, L), lambda g, r, i: (g, 0)), kv,
                  pl.BlockSpec((tq, HEAD_DIM), head), per_query, per_query] + [_ANY] * n_ex,
        out_specs=[pl.BlockSpec((tq, HEAD_DIM), head), kv, kv] + [_ANY] * n_ex,
        scratch_shapes=_ChipExchange.scratch(ex_ts),
        compiler_params=_params(("arbitrary", "arbitrary", "arbitrary")),
    )(qr, kr, k_t, vb, do, lse, delta, *ex_sums)


SSM_BLK = 8
SSM_NB = SSM_G // SSM_BLK
SSM_SEG = 8
SSM_UNROLL = 4


def _unrolled_loop(n, step, carry):
    u = SSM_UNROLL

    def trip(i, c):
        for j in range(u):
            c = step(i * u + j, c)
        return c
    carry = lax.fori_loop(0, n // u, trip, carry)
    for t in range(n - n % u, n):
        carry = step(jnp.int32(t), carry)
    return carry


def _cplx_pow2(a, b, n):
    for _ in range(int(math.log2(n))):
        a, b = a * a - b * b, 2.0 * a * b
    return a, b


def _seg_scan(ref, a, b, T, reverse, entry=None, tap=None):
    npair = len(a)
    zero = jnp.zeros((SSM_SEG, 128), F32)

    def make_step(store, tap_fn=None):
        def step(t, carry):
            lt = (T - 1 - t) if reverse else t
            row = pl.multiple_of(lt * SSM_SEG, SSM_SEG)
            blk = ref[pl.ds(row, SSM_SEG), :]
            new = []
            for q in range(npair):
                re, im = carry[2 * q], carry[2 * q + 1]
                nre = a[q] * re - b[q] * im + blk[:, q * 256:q * 256 + 128]
                nim = a[q] * im + b[q] * re + blk[:, q * 256 + 128:q * 256 + 256]
                new += [nre, nim]
            if store:
                ref[pl.ds(row, SSM_SEG), :] = jnp.concatenate(new, axis=1)
            extra = carry[2 * npair:]
            return tuple(new) + (tuple(tap_fn(lt, new, extra)) if tap_fn else tuple(extra))
        return step

    def second_pass(init):
        if tap is None:
            _unrolled_loop(T, make_step(True), tuple(init))
            return init
        carry = _unrolled_loop(T - 1, make_step(True, tap[0]), tuple(init) + tuple(tap[2]))
        carry = make_step(True, tap[1])(jnp.int32(T - 1), carry)
        return init, carry[2 * npair:]

    if entry is not None:
        return second_pass(entry)
    ends = _unrolled_loop(T, make_step(False), (zero,) * (2 * npair))
    sub = lax.broadcasted_iota(jnp.int32, (SSM_SEG, 128), 0)
    keep = (sub != SSM_SEG - 1) if reverse else (sub != 0)
    shift = (SSM_SEG - 1) if reverse else 1
    init = []
    for q in range(npair):
        pa, pb = _cplx_pow2(a[q], b[q], T)
        xr, xi = zero, zero
        for _ in range(SSM_SEG - 1):
            fr = ends[2 * q] + pa * xr - pb * xi
            fi = ends[2 * q + 1] + pa * xi + pb * xr
            xr = jnp.where(keep, pltpu.roll(fr, shift, 0), 0.0)
            xi = jnp.where(keep, pltpu.roll(fi, shift, 0), 0.0)
        init += [xr, xi]
    return second_pass(init)


SSM_BW = SSM_BLK * SSM_H
SSM_SW = SSM_BLK * 2 * SSM_P
SSM_NPAIR = SSM_BLK // 2


def _seg_perm(a):
    L, C = a.shape
    return a.reshape(SSM_SEG, L // SSM_SEG, C).transpose(1, 0, 2).reshape(L, C)


def _seg_unperm(a):
    L, C = a.shape
    return a.reshape(L // SSM_SEG, SSM_SEG, C).transpose(1, 0, 2).reshape(L, C)


def _lam_rows(are_ref, aim_ref, d):
    a = [jnp.broadcast_to(are_ref[d, j:j + 1, :], (SSM_SEG, 128)) for j in range(SSM_NPAIR)]
    b = [jnp.broadcast_to(aim_ref[d, j:j + 1, :], (SSM_SEG, 128)) for j in range(SSM_NPAIR)]
    return a, b


_PAIR_SPEC = pl.BlockSpec((None, 2, SSM_NPAIR, 2, 2 * SSM_H, 128), lambda g: (g, 0, 0, 0, 0, 0))


def _pair_window(j, r):
    return slice(j * 2 * SSM_H, (j + 1) * 2 * SSM_H), slice(j * 256 + r * 128, j * 256 + (r + 1) * 128)


def _expand_pairs(c_ref, dense_ref):
    dense_ref[...] = jnp.zeros_like(dense_ref)
    for d in range(2):
        for j in range(SSM_NPAIR):
            for r in range(2):
                rows, cols = _pair_window(j, r)
                dense_ref[d, rows, cols] = c_ref[d, j, r].astype(dense_ref.dtype)


def _ssm_fwd(u_p, wb, wc, are, aim, dvec):
    L = u_p.shape[0]
    T = L // SSM_SEG
    RC = min(512, L)

    def body(u_ref, wb_ref, wc_ref, are_ref, aim_ref, d_ref, y_ref, ge_ref, entry_ref, x_scr, wb_s, wc_s):
        _expand_pairs(wb_ref, wb_s)
        _expand_pairs(wc_ref, wc_s)
        y_ref[...] = u_ref[...] * d_ref[...]
        for d in range(2):
            def bu_chunk(c, _):
                rows = pl.ds(pl.multiple_of(c * RC, RC), RC)
                x_scr[rows, :] = jnp.dot(u_ref[rows, :].astype(BF16), wb_s[d], preferred_element_type=F32)
                return 0
            lax.fori_loop(0, L // RC, bu_chunk, 0)
            a, b = _lam_rows(are_ref, aim_ref, d)
            entry_ref[d] = jnp.concatenate(_seg_scan(x_scr, a, b, T, reverse=(d == 1)), axis=1)

            def y_chunk(c, _):
                rows = pl.ds(pl.multiple_of(c * RC, RC), RC)
                y_ref[rows, :] += lax.dot_general(x_scr[rows, :].astype(BF16), wc_s[d], _QK_T, preferred_element_type=F32)
                return 0
            lax.fori_loop(0, L // RC, y_chunk, 0)
        ge_ref[...] = _gelu(y_ref[...]).astype(ge_ref.dtype)

    blk4 = lambda g: (g, 0, 0, 0)
    chan = pl.BlockSpec((L, SSM_BW), lambda g: (0, g))
    return pl.pallas_call(
        body, name="ssm_fwd",
        out_shape=[jax.ShapeDtypeStruct((L, D_SSM), F32), jax.ShapeDtypeStruct((L, D_SSM), BF16),
                   jax.ShapeDtypeStruct((SSM_NB, 2, SSM_SEG, SSM_SW), F32)],
        grid=(SSM_NB,),
        in_specs=[chan, _PAIR_SPEC, _PAIR_SPEC,
                  pl.BlockSpec((None, 2, SSM_NPAIR, 128), blk4),
                  pl.BlockSpec((None, 2, SSM_NPAIR, 128), blk4),
                  pl.BlockSpec((1, SSM_BW), lambda g: (0, g))],
        out_specs=[chan, chan, pl.BlockSpec((None, 2, SSM_SEG, SSM_SW), blk4)],
        scratch_shapes=[pltpu.VMEM((L, SSM_SW), F32), pltpu.VMEM((2, SSM_BW, SSM_SW), BF16), pltpu.VMEM((2, SSM_BW, SSM_SW), BF16)],
        compiler_params=_params(("parallel",)),
    )(u_p, wb, wc, are, aim, dvec)


def _ssm_bwd(u_p, dy_p, entry, wb, wc, are, aim, dvec, sib_ts, sib_grads):
    L = u_p.shape[0]
    T = L // SSM_SEG
    RC = min(512, L)
    n_sib = len(sib_ts)

    def lam_acc(acc, s, x):
        new = []
        for q in range(SSM_NPAIR):
            sr, si, xr, xi = s[2 * q], s[2 * q + 1], x[2 * q], x[2 * q + 1]
            new += [acc[2 * q] + sr * xr + si * xi, acc[2 * q + 1] + si * xr - sr * xi]
        return tuple(new)

    def body(u_ref, dy_ref, entry_ref, wb_ref, wc_ref, are_ref, aim_ref, d_ref, *rest):
        du_ref, dwb_ref, dwc_ref, dare_ref, daim_ref, dd_ref = rest[n_sib:n_sib + 6]
        x_scr, s_scr, wb_s, wc_s, dwb_s, dwc_s = rest[2 * n_sib + 6:2 * n_sib + 12]
        sib = _SiblingExchange(sib_ts, rest[:n_sib], rest[n_sib + 6:2 * n_sib + 6], rest[2 * n_sib + 12:])
        pl.when(pl.program_id(0) == 0)(sib.start)
        _expand_pairs(wb_ref, wb_s)
        _expand_pairs(wc_ref, wc_s)
        du_ref[...] = dy_ref[...] * d_ref[...]
        dd_ref[...] = _colsum(dy_ref[...] * u_ref[...])
        dwb_s[...] = jnp.zeros_like(dwb_s)
        dwc_s[...] = jnp.zeros_like(dwc_s)
        for d in range(2):
            rev = d == 1

            def in_chunk(c, _):
                rows = pl.ds(pl.multiple_of(c * RC, RC), RC)
                x_scr[rows, :] = jnp.dot(u_ref[rows, :].astype(BF16), wb_s[d], preferred_element_type=F32)
                s_scr[rows, :] = jnp.dot(dy_ref[rows, :].astype(BF16), wc_s[d], preferred_element_type=F32)
                return 0
            lax.fori_loop(0, L // RC, in_chunk, 0)
            a, b = _lam_rows(are_ref, aim_ref, d)
            x_in = _seg_scan(x_scr, a, b, T, reverse=rev, entry=[entry_ref[d, :, q * 128:(q + 1) * 128] for q in range(2 * SSM_NPAIR)])

            def pair_with_row(lt, s_new, acc):
                xrow = pl.multiple_of((lt + 1 if rev else lt - 1) * SSM_SEG, SSM_SEG)
                xb = x_scr[pl.ds(xrow, SSM_SEG), :]
                return lam_acc(acc, s_new, [xb[:, i * 128:(i + 1) * 128] for i in range(2 * SSM_NPAIR)])

            zeros = (jnp.zeros((SSM_SEG, 128), F32),) * (2 * SSM_NPAIR)
            _, acc = _seg_scan(s_scr, a, [-v for v in b], T, reverse=not rev,
                               tap=(pair_with_row, lambda lt, s_new, acc: lam_acc(acc, s_new, x_in), zeros))
            for q in range(SSM_NPAIR):
                dare_ref[d, q:q + 1, :] = _colsum(acc[2 * q])
                daim_ref[d, q:q + 1, :] = _colsum(acc[2 * q + 1])

            def out_chunk(c, _):
                rows = pl.ds(pl.multiple_of(c * RC, RC), RC)
                xs, ss = x_scr[rows, :].astype(BF16), s_scr[rows, :].astype(BF16)
                uu, dd = u_ref[rows, :].astype(BF16), dy_ref[rows, :].astype(BF16)
                dwc_s[d] += lax.dot_general(dd, xs, _TA, preferred_element_type=F32)
                dwb_s[d] += lax.dot_general(uu, ss, _TA, preferred_element_type=F32)
                du_ref[rows, :] += lax.dot_general(ss, wb_s[d], _QK_T, preferred_element_type=F32)
                return 0
            lax.fori_loop(0, L // RC, out_chunk, 0)
            for j in range(SSM_NPAIR):
                for r in range(2):
                    rows, cols = _pair_window(j, r)
                    dwb_ref[d, j, r] = dwb_s[d, rows, cols]
                    dwc_ref[d, j, r] = dwc_s[d, rows, cols]
        pl.when(pl.program_id(0) == SSM_NB - 1)(sib.finish)

    blk4 = lambda g: (g, 0, 0, 0)
    chan = pl.BlockSpec((L, SSM_BW), lambda g: (0, g))
    par_specs = [_PAIR_SPEC, _PAIR_SPEC,
                 pl.BlockSpec((None, 2, SSM_NPAIR, 128), blk4),
                 pl.BlockSpec((None, 2, SSM_NPAIR, 128), blk4),
                 pl.BlockSpec((1, SSM_BW), lambda g: (0, g))]
    dense = lambda dt: pltpu.VMEM((2, SSM_BW, SSM_SW), dt)
    return pl.pallas_call(
        body, name="ssm_bwd",
        out_shape=[jax.ShapeDtypeStruct((L, D_SSM), F32),
                   jax.ShapeDtypeStruct(wb.shape, F32),
                   jax.ShapeDtypeStruct(wc.shape, F32),
                   jax.ShapeDtypeStruct((SSM_NB, 2, SSM_NPAIR, 128), F32),
                   jax.ShapeDtypeStruct((SSM_NB, 2, SSM_NPAIR, 128), F32),
                   jax.ShapeDtypeStruct((1, D_SSM), F32)] + _SiblingExchange.out_shape(sib_ts),
        grid=(SSM_NB,),
        in_specs=[chan, chan, pl.BlockSpec((None, 2, SSM_SEG, SSM_SW), blk4)] + par_specs + [_ANY] * n_sib,
        out_specs=[chan] + par_specs + [_ANY] * n_sib,
        scratch_shapes=[pltpu.VMEM((L, SSM_SW), F32), pltpu.VMEM((L, SSM_SW), F32), dense(BF16), dense(BF16), dense(F32), dense(F32)] +
                       _SiblingExchange.scratch(sib_ts),
        compiler_params=_params(("arbitrary",)),
    )(u_p, dy_p, entry, wb, wc, are, aim, dvec, *sib_grads)


def _ssm_disc(a_re, a_im, log_dt, b_re, b_im):
    lam = lax.complex(jnp.minimum(a_re, -1e-4), a_im)
    dt = jnp.exp(log_dt)[..., None]
    lam_bar = jnp.exp(lam * dt)
    b_bar = ((lam_bar - 1.0) / lam)[..., None] * lax.complex(b_re, b_im)
    return jnp.real(lam_bar), jnp.imag(lam_bar), jnp.real(b_bar), jnp.imag(b_bar)


_EYE2 = np.eye(2, dtype=np.float32)[:, None, :, None]


def _to_pairs(t):
    t = t.reshape(2, SSM_NB, SSM_NPAIR, 2, 2, SSM_H, SSM_P).transpose(1, 0, 2, 4, 3, 5, 6)
    return (t[..., None, :] * _EYE2).reshape(SSM_NB, 2, SSM_NPAIR, 2, 2 * SSM_H, 2 * SSM_P)


def _from_pairs(c):
    t = jnp.sum(c.reshape(SSM_NB, 2, SSM_NPAIR, 2, 2, SSM_H, 2, SSM_P) * _EYE2, axis=6)
    return t.transpose(1, 0, 2, 4, 3, 5, 6).reshape(2, SSM_G, 2, SSM_H, SSM_P)


def _to_lam(v):
    return v.reshape(2, SSM_NB, SSM_NPAIR, 128).transpose(1, 0, 2, 3)


def _from_lam(v):
    return v.transpose(1, 0, 2, 3).reshape(2, SSM_G, SSM_P)


_MESH = pl.DeviceIdType.MESH
_ANY = pl.BlockSpec(memory_space=pl.ANY)
_BIG = (("w_in", (D_MODEL, D_IN), 1, D_IN // N_CHIPS),
        ("w_glu", (D_SSM, 2 * D_SSM), 1, 2 * D_SSM // N_CHIPS),
        ("w_out", (D_ATTN + D_SSM, D_MODEL), 0, (D_ATTN + D_SSM) // N_CHIPS),
        ("w_ple_gate", (D_MODEL, D_MODEL), 0, D_MODEL // N_CHIPS),
        ("w_ple_proj", (PLE_DIM, D_MODEL), 1, D_MODEL // N_CHIPS))


def _place():
    x, y, c = lax.axis_index("x"), lax.axis_index("y"), lax.axis_index("c")
    return x, y, c, [(1 - x, y), (x, 1 - y), (1 - x, 1 - y)]


class _Gather:
    def __init__(self, ts, srcs, dsts, stage, sems):
        self.ts, self.srcs, self.dsts, self.stage = ts, srcs, dsts, stage
        self.send_sems, self.recv_sems, self.fwd_send_sems, self.fwd_recv_sems, self.loc_sems = sems
        self.x, self.y, self.c, self.chips = _place()
        self.n = len(ts)

    @staticmethod
    def scratch(shards):
        sems = pltpu.SemaphoreType.DMA((3, len(shards)))
        return [pltpu.VMEM(s.shape, BF16) for s in shards] + [sems, sems, sems, sems, pltpu.SemaphoreType.DMA((len(shards),))]

    def _shard_of(self, i, kk):
        _, _, axis, sz = _BIG[self.ts[i]]
        sl = pl.ds(pl.multiple_of(kk * sz, sz), sz)
        return self.dsts[i].at[:, sl] if axis == 1 else self.dsts[i].at[sl, :]

    @staticmethod
    def _half_of(ref, cc):
        n = ref.shape[0] // 2
        return ref.at[pl.ds(pl.multiple_of(cc * n, n), n), :]

    def _ici(self, j, i, kk):
        px, py = self.chips[j]
        return pltpu.make_async_remote_copy(
            src_ref=self._half_of(self.srcs[i], self.c), dst_ref=self._half_of(self._shard_of(i, kk), self.c),
            send_sem=self.send_sems.at[j, i], recv_sem=self.recv_sems.at[j, i],
            device_id=(px, py, self.c), device_id_type=_MESH)

    def _forward(self, j, i, kk, cc):
        part = self._half_of(self._shard_of(i, kk), cc)
        return pltpu.make_async_remote_copy(
            src_ref=part, dst_ref=part, send_sem=self.fwd_send_sems.at[j, i], recv_sem=self.fwd_recv_sems.at[j, i],
            device_id=(self.x, self.y, 1 - self.c), device_id_type=_MESH)

    def _load(self, i):
        return pltpu.make_async_copy(self.srcs[i], self.stage[i], self.loc_sems.at[i])

    def _place_own(self, i):
        return pltpu.make_async_copy(self.stage[i], self._shard_of(i, 2 * self.x + self.y), self.loc_sems.at[i])

    def _peers(self):
        return [(i, j, 2 * px + py) for i in range(self.n) for j, (px, py) in enumerate(self.chips)]

    def start(self):
        for i in range(self.n):
            self._load(i).start()
        for i, j, _ in self._peers():
            self._ici(j, i, 2 * self.x + self.y).start()

    def forward(self):
        for i in range(self.n):
            self._load(i).wait()
            self._place_own(i).start()
        for i, j, kk in self._peers():
            self._ici(j, i, kk).wait_recv()
            self._forward(j, i, kk, self.c).start()

    def finish(self):
        for i, j, kk in self._peers():
            self._forward(j, i, kk, 1 - self.c).wait_recv()
        for i, j, kk in self._peers():
            self._ici(j, i, kk).wait_send()
            self._forward(j, i, kk, self.c).wait_send()
        for i in range(self.n):
            self._place_own(i).wait()


def _matmul_in_gather(a, shard, *, tm=1024):
    t = 0
    (_, (K, N), _, sz) = _BIG[t]
    M = a.shape[0]
    tm = min(tm, M)
    gm = M // tm
    x, y = lax.axis_index("x"), lax.axis_index("y")
    order = jnp.stack([2 * x + y, 2 * (1 - x) + y, 2 * x + 1 - y, 2 * (1 - x) + 1 - y]).astype(jnp.int32)

    def body(order_ref, a_ref, shard_ref, z_ref, full_ref, b_vm, *sems):
        g = _Gather([t], [shard_ref], [full_ref], [b_vm], sems[:5])
        load_sem = sems[5]
        s, i = pl.program_id(0), pl.program_id(1)

        @pl.when((s == 0) & (i == 0))
        def _():
            g.start()
            g._load(0).wait()
            g._place_own(0).start()

        for j, (px, py) in enumerate(g.chips):
            @pl.when((s == j + 1) & (i == 0))
            def _(j=j, kk=2 * px + py):
                if j == 0:
                    g._place_own(0).wait()
                g._ici(j, 0, kk).wait_recv()
                g._forward(j, 0, kk, g.c).start()
                g._forward(j, 0, kk, 1 - g.c).wait_recv()
                cp = pltpu.make_async_copy(g._shard_of(0, kk), b_vm, load_sem.at[0])
                cp.start()
                cp.wait()

        z_ref[...] = jnp.dot(a_ref[...], b_vm[...], preferred_element_type=F32)

        @pl.when((s == N_CHIPS - 1) & (i == gm - 1))
        def _():
            for j, (px, py) in enumerate(g.chips):
                g._ici(j, 0, 2 * px + py).wait_send()
                g._forward(j, 0, 2 * px + py, g.c).wait_send()

    return pl.pallas_call(
        body, name="mm_in",
        out_shape=[jax.ShapeDtypeStruct((M, N), F32), jax.ShapeDtypeStruct((K, N), BF16)],
        grid_spec=pltpu.PrefetchScalarGridSpec(
            num_scalar_prefetch=1, grid=(N_CHIPS, gm),
            in_specs=[pl.BlockSpec((tm, K), lambda s, i, o: (i, 0)), _ANY],
            out_specs=[pl.BlockSpec((tm, sz), lambda s, i, o: (i, o[s])), _ANY],
            scratch_shapes=_Gather.scratch([shard]) + [pltpu.SemaphoreType.DMA((1,))]),
        compiler_params=_params(("arbitrary", "arbitrary")),
    )(order, a, shard)


SMALL_W = 1024
SMALL_ROWS = 72
N_SMALL = 8 * SMALL_ROWS * SMALL_W
_RED = tuple((shape, ax, (shape[0] // 2, sz) if ax == 1 else (sz // 2, shape[1]), BF16) for _, shape, ax, sz in _BIG) + \
    (((8 * SMALL_ROWS, SMALL_W), 0, (SMALL_ROWS, SMALL_W), F32),)
_RED_TR = 128


def _piece(ref, t, kk, cc):
    _, ax, (pr, pc), _ = _RED[t]
    if ax == 1:
        return ref.at[pl.ds(pl.multiple_of(cc * pr, pr), pr), pl.ds(pl.multiple_of(kk * pc, pc), pc)]
    return ref.at[pl.ds(pl.multiple_of((2 * kk + cc) * pr, pr), pr), :]


def _half_shape(t):
    shape, ax, (pr, pc), _ = _RED[t]
    return (pr, shape[1]) if ax == 1 else (N_CHIPS * pr, pc)


def _piece_in_half(ref, t, kk):
    _, ax, (pr, pc), _ = _RED[t]
    return ref.at[:, pl.ds(pl.multiple_of(kk * pc, pc), pc)] if ax == 1 else ref.at[pl.ds(pl.multiple_of(kk * pr, pr), pr), :]


class _SiblingExchange:
    def __init__(self, ts, srcs, dsts, sems):
        send_sems, recv_sems = sems
        x, y, c, _ = _place()

        def copies():
            pairs = []
            for i, t in enumerate(ts):
                _, ax, (pr, _), _ = _RED[t]
                if ax == 1:
                    pairs.append((srcs[i].at[pl.ds(pl.multiple_of((1 - c) * pr, pr), pr), :], dsts[i]))
                else:
                    pairs += [(_piece(srcs[i], t, kk, 1 - c), _piece_in_half(dsts[i], t, kk)) for kk in range(N_CHIPS)]
            return [pltpu.make_async_remote_copy(src_ref=s, dst_ref=d, send_sem=send_sems.at[i], recv_sem=recv_sems.at[i],
                                                 device_id=(x, y, 1 - c), device_id_type=_MESH) for i, (s, d) in enumerate(pairs)]
        self.copies = copies

    @staticmethod
    def scratch(ts):
        n_dma = sum(1 if _RED[t][1] == 1 else N_CHIPS for t in ts)
        return [pltpu.SemaphoreType.DMA((n_dma,)), pltpu.SemaphoreType.DMA((n_dma,))]

    @staticmethod
    def out_shape(ts):
        return [jax.ShapeDtypeStruct(_half_shape(t), F32) for t in ts]

    def start(self):
        for cp in self.copies():
            cp.start()

    def finish(self):
        for cp in self.copies():
            cp.wait()


def _grad_sibling_exchange(ts, grads, name):
    n = len(ts)

    def body(*refs):
        ex = _SiblingExchange(ts, refs[:n], refs[n:2 * n], refs[2 * n:])
        ex.start()
        ex.finish()

    return pl.pallas_call(
        body, name=name,
        out_shape=_SiblingExchange.out_shape(ts),
        in_specs=[_ANY] * n, out_specs=[_ANY] * n,
        scratch_shapes=_SiblingExchange.scratch(ts),
    )(*grads)


def _chip_sum(t, g, rs, place):
    shape, ax, (pr, pc), dt = _RED[t]
    W = shape[1]
    tr = min(pr, _RED_TR)
    nb = pr // tr

    def body(place_ref, g_ref, rs_ref, o_ref):
        o_ref[...] = (g_ref[...] + rs_ref[...]).astype(o_ref.dtype)

    return pl.pallas_call(
        body, name="grad_chip_sum_%d" % t,
        out_shape=jax.ShapeDtypeStruct(rs.shape, dt),
        grid_spec=pltpu.PrefetchScalarGridSpec(
            num_scalar_prefetch=1, grid=(1 if ax == 1 else N_CHIPS, nb),
            in_specs=[pl.BlockSpec((tr, W), lambda kk, i, pr_: ((2 * kk + pr_[0]) * nb + i, 0)),
                      pl.BlockSpec((tr, W), lambda kk, i, pr_: (kk * nb + i, 0))],
            out_specs=pl.BlockSpec((tr, W), lambda kk, i, pr_: (kk * nb + i, 0))),
        compiler_params=_params(("parallel", "parallel")),
    )(place, g, rs)


class _ChipExchange:
    def __init__(self, ts, srcs, dsts, sems):
        self.send_sems, self.recv_sems = sems
        x, y, c, chips = _place()
        self.copies = lambda: [
            pltpu.make_async_remote_copy(src_ref=_piece_in_half(srcs[i], t, 2 * px + py), dst_ref=dsts[i].at[j],
                                         send_sem=self.send_sems.at[j, i], recv_sem=self.recv_sems.at[j, i],
                                         device_id=(px, py, c), device_id_type=_MESH)
            for i, t in enumerate(ts) for j, (px, py) in enumerate(chips)]

    @staticmethod
    def scratch(ts):
        return [pltpu.SemaphoreType.DMA((3, len(ts))), pltpu.SemaphoreType.DMA((3, len(ts)))]

    @staticmethod
    def out_shape(ts):
        return [jax.ShapeDtypeStruct((3,) + _RED[t][2], _RED[t][3]) for t in ts]

    def start(self):
        for cp in self.copies():
            cp.start()

    def finish(self):
        for cp in self.copies():
            cp.wait()


def _grad_chip_exchange(ts, sums):
    n = len(ts)

    def body(*refs):
        ex = _ChipExchange(ts, refs[:n], refs[n:2 * n], refs[2 * n:])
        ex.start()
        ex.finish()

    return pl.pallas_call(
        body, name="grad_chip_exchange",
        out_shape=_ChipExchange.out_shape(ts),
        in_specs=[_ANY] * n, out_specs=[_ANY] * n,
        scratch_shapes=_ChipExchange.scratch(ts),
    )(*sums)


def _total_sum(t, g, rs, rc, place):
    shape, ax, (pr, pc), _ = _RED[t]
    tr = min(pr, _RED_TR)
    nb = pr // tr
    small = t == len(_RED) - 1

    def body(place_ref, g_ref, rs_ref, rc_ref, o_ref):
        o_ref[...] = (g_ref[...] + rs_ref[...]) + rc_ref[0].astype(F32) + rc_ref[1].astype(F32) + rc_ref[2].astype(F32)

    if ax == 1:
        g_map = lambda i, pr_: (pr_[0] * nb + i, pr_[1])
        rs_map = lambda i, pr_: (i, pr_[1])
    else:
        g_map = lambda i, pr_: ((2 * pr_[1] + pr_[0]) * nb + i, 0)
        rs_map = lambda i, pr_: (pr_[1] * nb + i, 0)
    o_map = (lambda i, pr_: ((2 * pr_[1] + pr_[0]) * nb + i, 0)) if small else (lambda i, pr_: (pr_[0] * nb + i, 0))
    return pl.pallas_call(
        body, name="grad_total_sum_%d" % t,
        out_shape=jax.ShapeDtypeStruct(((8 if small else 2) * pr, pc), F32),
        grid_spec=pltpu.PrefetchScalarGridSpec(
            num_scalar_prefetch=1, grid=(nb,),
            in_specs=[pl.BlockSpec((tr, pc), g_map), pl.BlockSpec((tr, pc), rs_map),
                      pl.BlockSpec((3, tr, pc), lambda i, pr_: (0, i, 0))],
            out_specs=pl.BlockSpec((tr, pc), o_map)),
        compiler_params=_params(("parallel",)),
    )(place, g, rs, rc)


def _grad_final_exchange(totals):
    n = len(_RED)
    nb = n - 1

    def body(*refs):
        srcs, dsts, (send_sems, recv_sems) = refs[:n], refs[n:2 * n], refs[2 * n:]
        x, y, c, chips = _place()
        me = 4 * x + 2 * y + c
        others = [(x, y, 1 - c)] + [(px, py, cc) for (px, py) in chips for cc in (c, 1 - c)]

        def half(ref, t, cc):
            pr = _RED[t][2][0]
            return ref.at[pl.ds(pl.multiple_of(cc * pr, pr), pr), :]

        def eighth(ref, dev):
            return ref.at[pl.ds(pl.multiple_of(dev * SMALL_ROWS, SMALL_ROWS), SMALL_ROWS), :]

        def big_copy(t, cc):
            return pltpu.make_async_remote_copy(src_ref=half(srcs[t], t, cc), dst_ref=half(dsts[t], t, cc), send_sem=send_sems.at[t],
                                                recv_sem=recv_sems.at[t], device_id=others[0], device_id_type=_MESH)

        def small_copy(i, dev):
            return pltpu.make_async_remote_copy(src_ref=eighth(srcs[nb], dev), dst_ref=eighth(dsts[nb], dev),
                                                send_sem=send_sems.at[nb + i], recv_sem=recv_sems.at[nb + i],
                                                device_id=others[i], device_id_type=_MESH)

        sends = [big_copy(t, c) for t in range(nb)] + [small_copy(i, me) for i in range(7)]
        for cp in sends:
            cp.start()
        for t in range(nb):
            big_copy(t, 1 - c).wait_recv()
        for i, (px, py, pc) in enumerate(others):
            small_copy(i, 4 * px + 2 * py + pc).wait_recv()
        for cp in sends:
            cp.wait_send()

    return pl.pallas_call(
        body, name="grad_final_exchange",
        out_shape=[jax.ShapeDtypeStruct(a.shape, F32) for a in totals],
        in_specs=[_ANY] * n, out_specs=[_ANY] * n,
        input_output_aliases={t: t for t in range(n)},
        scratch_shapes=[pltpu.SemaphoreType.DMA((nb + 7,)), pltpu.SemaphoreType.DMA((nb + 7,))],
    )(*totals)


def _grad_place():
    return jnp.stack([lax.axis_index("c"), 2 * lax.axis_index("x") + lax.axis_index("y")]).astype(jnp.int32)


def _reduce_begin(ts, grads, place, tag, from_sibling=None):
    if from_sibling is None:
        from_sibling = _grad_sibling_exchange(ts, grads, "grad_sibling_exchange_" + tag)
    return from_sibling, [_chip_sum(t, g, r, place) for t, g, r in zip(ts, grads, from_sibling)]


def _reduce_end(ts, grads, from_sibling, from_chips, place):
    return [_total_sum(t, g, r, q, place) for t, g, r, q in zip(ts, grads, from_sibling, from_chips)]


_EARLY = (1, 2, 3, 4)
_W_IN = (0,)
_SMALL_RED = (5,)


def _adamw_math(w, g, m, v):
    m = ADAM_B1 * m + (1.0 - ADAM_B1) * g
    v = ADAM_B2 * v + (1.0 - ADAM_B2) * (g * g)
    m_hat = m / (1.0 - ADAM_B1 ** ADAM_STEP)
    v_hat = v / (1.0 - ADAM_B2 ** ADAM_STEP)
    return -ADAM_LR * (m_hat / (jnp.sqrt(v_hat) + ADAM_EPS) + ADAM_WD * w), m, v


def _adamw(w, g, m, v, name):
    W = w.shape[1]
    return _rowwise(_adamw_math, [(a, 0, W) for a in (w, g, m, v)], [], [(W, F32)] * 3, tr=128, name=name)


def _adamw_whole(ws, gs, ms, vs, name):
    n = len(ws)

    def body(*refs):
        ins, outs = refs[:4 * n], refs[4 * n:]
        for i in range(n):
            res = _adamw_math(*[ins[j * n + i][...] for j in range(4)])
            for j in range(3):
                outs[j * n + i][...] = res[j]

    res = pl.pallas_call(
        body, name=name,
        out_shape=[jax.ShapeDtypeStruct(a.shape, F32) for a in ws] * 3,
        compiler_params=pltpu.CompilerParams(vmem_limit_bytes=VMEM_LIMIT_V7X),
    )(*ws, *gs, *ms, *vs)
    return res[:n], res[n:2 * n], res[2 * n:]


def _chunks(arr, off, width, w=512):
    return [(arr, off + i * w, w) for i in range(width // w)]


def _cat(vs):
    return jnp.concatenate(vs, axis=1)


def _forward_backward(x, p_b, tgt, shards, small):
    L = x.shape[0]
    row = lambda v: v.reshape(1, -1)
    g_mix, g_ple, g_fin = row(small["norm_mix"]), row(small["norm_ple"]), row(small["norm_final"])
    gq, gk, b_glu = row(small["q_norm"]), row(small["k_norm"]), row(small["b_glu"])
    cos, sin = _rope_tables(L)

    hn_b, = _rowwise(lambda x, g: x * _rms(x) * g, [(x, 0, D_MODEL)], [g_mix], [(D_MODEL, BF16)], name="norm_mix")
    z, w_in = _matmul_in_gather(hn_b, shards[0])
    qr, kr, vb = _attn_prep(z, gq, gk, cos, sin)
    o, lse, w_glu, w_out, w_pg, w_pp = _attn_fwd(qr, kr, vb, [1, 2, 3, 4], shards[1:])

    ssm_names = ("ssm_a_re", "ssm_a_im", "ssm_log_dt", "ssm_b_re", "ssm_b_im")
    (lre, lim, bre, bim), disc_vjp = jax.vjp(_ssm_disc, *[small[n][0] for n in ssm_names])
    ssm = (_to_pairs(jnp.swapaxes(jnp.stack([bre, bim], axis=2), -1, -2)),
           _to_pairs(jnp.stack([small["ssm_c_re"][0], -small["ssm_c_im"][0]], axis=2)),
           _to_lam(lre), _to_lam(lim), row(small["ssm_d"]))
    u_p = _seg_perm(z[:, Z_U:Z_U + D_SSM])
    y_p, ge_p, ssm_entry = _ssm_fwd(u_p, *ssm)
    ge_b = _seg_unperm(ge_p)
    glu = _matmul(ge_b, w_glu, name="mm_glu")

    def merge(ga0, ga1, a, b, gs0, gs1, o, bias):
        sa, _ = _silu_and_grad(_cat([ga0, ga1]))
        ss, _ = _silu_and_grad(_cat([gs0, gs1]))
        y2 = (a + bias[:, :D_SSM]) * _sig(b + bias[:, D_SSM:])
        return _cat([o * sa, y2 * ss])
    merge_rows = _chunks(z, Z_GA, D_ATTN) + [(glu, 0, D_SSM), (glu, D_SSM, D_SSM)] + _chunks(z, Z_GS, D_SSM) + [(o, 0, D_ATTN)]
    cat_b, = _rowwise(merge, merge_rows, [b_glu], [(D_MODEL, BF16)], name="merge")
    t_out = _matmul(cat_b, w_out, name="mm_out")

    def resid(x, t, g):
        h1 = x + t
        return h1, h1 * _rms(h1) * g
    h1, hp_b = _rowwise(resid, [(x, 0, D_MODEL), (t_out, 0, D_MODEL)], [g_ple], [(D_MODEL, F32), (D_MODEL, BF16)], name="resid_norm")
    gl = _matmul(hp_b, w_pg, name="mm_ple_gate")
    pp = _matmul(p_b, w_pp, name="mm_ple_proj")

    def head(h1, gl, pp, tgt, g):
        gate = _sig(gl)
        h2 = h1 + gate * pp
        r = _rms(h2)
        n = h2 * r
        err = n * g - tgt
        dy = err * (1.0 / D_MODEL)
        dn = dy * g
        dh2 = r * (dn - n * jnp.mean(dn * n, axis=-1, keepdims=True))
        dgate = dh2 * pp
        return dh2, dh2 * gate, dgate * gate * (1.0 - gate), _colsum(dy * n), _colsum(0.5 * err * err * (1.0 / D_MODEL))
    dh2, dpp_b, dgl_b, dg_fin, loss_cols = _rowwise(
        head, [(a, 0, D_MODEL) for a in (h1, gl, pp, tgt)], [g_fin],
        [(D_MODEL, F32), (D_MODEL, BF16), (D_MODEL, BF16)], [(1, D_MODEL), (1, D_MODEL)], name="loss_head")

    dw_pp = _matmul(p_b, dpp_b, ta=True, name="mm_d_w_ple_proj")
    dw_pg = _matmul(hp_b, dgl_b, ta=True, name="mm_d_w_ple_gate")
    dhp = _matmul(dgl_b, w_pg, tb=True, name="mm_d_hp")

    def resid_bwd(dhp, h1, dh2, g):
        dx, dg = _rms_bwd(h1, g, dhp)
        dh1 = dh2 + dx
        return dh1, dh1, _colsum(dg)
    dh1, dh1_b, dg_ple = _rowwise(resid_bwd, [(a, 0, D_MODEL) for a in (dhp, h1, dh2)], [g_ple],
                                  [(D_MODEL, F32), (D_MODEL, BF16)], [(1, D_MODEL)], name="resid_norm_bwd")
    dw_out = _matmul(cat_b, dh1_b, ta=True, name="mm_d_w_out")
    dcat = _matmul(dh1_b, w_out, tb=True, name="mm_d_cat")

    def merge_bwd(dya, dys, ga0, ga1, a, b, gs0, gs1, o, bias):
        ga, gs = _cat([ga0, ga1]), _cat([gs0, gs1])
        sa, dsa = _silu_and_grad(ga)
        ss, dss = _silu_and_grad(gs)
        a, sb = a + bias[:, :D_SSM], _sig(b + bias[:, D_SSM:])
        dy2 = dys * ss
        dglu = _cat([dy2 * sb, dy2 * a * sb * (1.0 - sb)])
        do = dya * sa
        lane = lax.broadcasted_iota(jnp.int32, (do.shape[0], HEAD_DIM), 1)
        delta = sum(jnp.where(lane == i, jnp.sum(h, axis=1, keepdims=True), 0.0)
                    for i, h in enumerate(_heads(do * o)))
        return do, dya * o * dsa, dys * (a * sb) * dss, dglu, delta, _colsum(dglu)
    do_b, dga_b, dgs_b, dglu_b, delta, db_glu = _rowwise(
        merge_bwd, [(dcat, 0, D_ATTN), (dcat, D_ATTN, D_SSM)] + merge_rows, [b_glu],
        [(D_ATTN, BF16), (D_ATTN, BF16), (D_SSM, BF16), (2 * D_SSM, BF16), (HEAD_DIM, F32)], [(1, 2 * D_SSM)], name="merge_bwd")
    dw_glu = _matmul(ge_b, dglu_b, ta=True, name="mm_d_w_glu")
    dge = _matmul(dglu_b, w_glu, tb=True, name="mm_d_ge")
    place = _grad_place()
    early_grads = [dw_glu, dw_out, dw_pg, dw_pp]
    dy_p, = _rowwise(lambda dge, y: dge * _gelu_grad(y), [(_seg_perm(dge), 0, D_SSM), (y_p, 0, D_SSM)], [], [(D_SSM, F32)],
                     name="gelu_bwd")
    du_p, dwb, dwc, dare, daim, d_ssm_d, *early_sib = _ssm_bwd(u_p, dy_p, ssm_entry, *ssm, _EARLY, early_grads)
    dcc, dbb = _from_pairs(dwc), jnp.swapaxes(_from_pairs(dwb), -1, -2)
    dc_re, dc_im = dcc[:, :, 0], -dcc[:, :, 1]
    da_re, da_im, dlog_dt, db_re, db_im = disc_vjp((_from_lam(dare), _from_lam(daim), dbb[:, :, 0], dbb[:, :, 1]))

    early_sib, early_sums = _reduce_begin(_EARLY, early_grads, place, "early", from_sibling=early_sib)
    dqr, dkr, dv, *early_chips = _attn_bwd(qr, kr, kr.T, vb, do_b, lse.reshape(N_HEADS, 1, L), delta[:, :N_HEADS].T.reshape(N_HEADS, 1, L),
                                           _EARLY, early_sums)
    early_totals = _reduce_end(_EARLY, early_grads, early_sib, early_chips, place)
    dq_b, dk_b, dgq, dgk = _attn_prep_bwd(dqr, dkr, z, gq, gk, cos, sin)
    dz_b = _cat([dq_b, dk_b, dv.astype(BF16), dga_b, _seg_unperm(du_p).astype(BF16), dgs_b])
    dw_in = _matmul(hn_b, dz_b, ta=True, name="mm_d_w_in")
    w_in_sib, w_in_sums = _reduce_begin(_W_IN, [dw_in], place, "w_in")
    dhn, *w_in_chips = _matmul(dz_b, w_in, tb=True, name="mm_d_hn", exchange=(_W_IN, w_in_sums))
    w_in_total, = _reduce_end(_W_IN, [dw_in], w_in_sib, w_in_chips, place)

    def norm_bwd(dhn, x, dh1, g):
        dx, dg = _rms_bwd(x, g, dhn)
        return dh1 + dx, _colsum(dg)
    grad_x, dg_mix = _rowwise(norm_bwd, [(a, 0, D_MODEL) for a in (dhn, x, dh1)], [g_mix], [(D_MODEL, F32)], [(1, D_MODEL)],
                              name="norm_mix_bwd")

    small_grads = {"norm_mix": dg_mix, "q_norm": dgq, "k_norm": dgk, "ssm_a_re": da_re, "ssm_a_im": da_im, "ssm_log_dt": dlog_dt,
                   "ssm_b_re": db_re, "ssm_b_im": db_im, "ssm_c_re": dc_re, "ssm_c_im": dc_im, "ssm_d": d_ssm_d,
                   "b_glu": db_glu, "norm_ple": dg_ple, "norm_final": dg_fin}
    return jnp.sum(loss_cols), grad_x, [w_in_total] + early_totals, small_grads, place


_SMALL = ("norm_mix", "q_norm", "k_norm", "ssm_a_re", "ssm_a_im", "ssm_log_dt", "ssm_b_re", "ssm_b_im", "ssm_c_re", "ssm_c_im",
          "ssm_d", "b_glu", "norm_ple", "norm_final")
_WEIGHTS = ("norm_mix", "w_in", "q_norm", "k_norm", "ssm_a_re", "ssm_a_im", "ssm_log_dt", "ssm_b_re", "ssm_b_im", "ssm_c_re",
            "ssm_c_im", "ssm_d", "w_glu", "b_glu", "w_out", "norm_ple", "w_ple_gate", "w_ple_proj", "norm_final")


_SMALL_ADAMW_GROUPS = (("ssm_b_re",), ("ssm_b_im",), ("ssm_c_re", "ssm_c_im"),
                       ("norm_mix", "q_norm", "k_norm", "ssm_a_re", "ssm_a_im", "ssm_log_dt", "ssm_d", "b_glu", "norm_ple",
                        "norm_final"))


def _flat_small(d):
    flat = jnp.concatenate([d[n].reshape(-1).astype(F32) for n in _SMALL])
    return jnp.pad(flat, (0, N_SMALL - flat.shape[0]))


def _split_small(flat, like):
    out, off = {}, 0
    for n in _SMALL:
        sz = math.prod(like[n].shape)
        out[n] = flat[off:off + sz].reshape(like[n].shape)
        off += sz
    return out


def kernel(x, p, norm_mix, w_in, q_norm, k_norm, ssm_a_re, ssm_a_im, ssm_log_dt, ssm_b_re, ssm_b_im, ssm_c_re, ssm_c_im, ssm_d, w_glu, b_glu, w_out, norm_ple, w_ple_gate, w_ple_proj, norm_final, loss_target, m_norm_mix, m_w_in, m_q_norm, m_k_norm, m_ssm_a_re, m_ssm_a_im, m_ssm_log_dt, m_ssm_b_re, m_ssm_b_im, m_ssm_c_re, m_ssm_c_im, m_ssm_d, m_w_glu, m_b_glu, m_w_out, m_norm_ple, m_w_ple_gate, m_w_ple_proj, m_norm_final, v_norm_mix, v_w_in, v_q_norm, v_k_norm, v_ssm_a_re, v_ssm_a_im, v_ssm_log_dt, v_ssm_b_re, v_ssm_b_im, v_ssm_c_re, v_ssm_c_im, v_ssm_d, v_w_glu, v_b_glu, v_w_out, v_norm_ple, v_w_ple_gate, v_w_ple_proj, v_norm_final):
    w = dict(norm_mix=norm_mix, w_in=w_in, q_norm=q_norm, k_norm=k_norm, ssm_a_re=ssm_a_re, ssm_a_im=ssm_a_im,
             ssm_log_dt=ssm_log_dt, ssm_b_re=ssm_b_re, ssm_b_im=ssm_b_im, ssm_c_re=ssm_c_re, ssm_c_im=ssm_c_im, ssm_d=ssm_d,
             w_glu=w_glu, b_glu=b_glu, w_out=w_out, norm_ple=norm_ple, w_ple_gate=w_ple_gate, w_ple_proj=w_ple_proj,
             norm_final=norm_final)
    m = dict(norm_mix=m_norm_mix, w_in=m_w_in, q_norm=m_q_norm, k_norm=m_k_norm, ssm_a_re=m_ssm_a_re, ssm_a_im=m_ssm_a_im,
             ssm_log_dt=m_ssm_log_dt, ssm_b_re=m_ssm_b_re, ssm_b_im=m_ssm_b_im, ssm_c_re=m_ssm_c_re, ssm_c_im=m_ssm_c_im,
             ssm_d=m_ssm_d, w_glu=m_w_glu, b_glu=m_b_glu, w_out=m_w_out, norm_ple=m_norm_ple, w_ple_gate=m_w_ple_gate,
             w_ple_proj=m_w_ple_proj, norm_final=m_norm_final)
    v = dict(norm_mix=v_norm_mix, w_in=v_w_in, q_norm=v_q_norm, k_norm=v_k_norm, ssm_a_re=v_ssm_a_re, ssm_a_im=v_ssm_a_im,
             ssm_log_dt=v_ssm_log_dt, ssm_b_re=v_ssm_b_re, ssm_b_im=v_ssm_b_im, ssm_c_re=v_ssm_c_re, ssm_c_im=v_ssm_c_im,
             ssm_d=v_ssm_d, w_glu=v_w_glu, b_glu=v_b_glu, w_out=v_w_out, norm_ple=v_norm_ple, w_ple_gate=v_w_ple_gate,
             w_ple_proj=v_w_ple_proj, norm_final=v_norm_final)
    big_names = [n for n, _, _, _ in _BIG]

    small = {n: w[n] for n in _SMALL}
    loss_part, grad_x, big_totals, small_grads, place = _forward_backward(
        x[0], p[0, 0].astype(BF16), loss_target[0], [w[n][0].astype(BF16) for n in big_names], small)
    loss = lax.psum(loss_part, ("x", "y", "c"))

    small_flat = [_flat_small(small_grads).reshape(8 * SMALL_ROWS, SMALL_W)]
    small_sib, small_sums = _reduce_begin(_SMALL_RED, small_flat, place, "small")
    small_total = _reduce_end(_SMALL_RED, small_flat, small_sib, _grad_chip_exchange(_SMALL_RED, small_sums), place)
    *big_red, small_red = _grad_final_exchange(big_totals + small_total)
    grads = _split_small(small_red.reshape(-1), w)
    delta, new_m, new_v = {}, {}, {}
    for n, g in zip(big_names, big_red):
        grads[n] = g[None]
        d_, m_, v_ = _adamw(w[n][0], g, m[n][0], v[n][0], "adamw_" + n)
        delta[n], new_m[n], new_v[n] = d_[None], m_[None], v_[None]
    at_least_2d = lambda a: a.reshape(1, -1) if a.ndim == 1 else a
    for i, names in enumerate(_SMALL_ADAMW_GROUPS):
        d_, m_, v_ = _adamw_whole(*[[at_least_2d(src[n]) for n in names] for src in (w, grads, m, v)], "adamw_small_%d" % i)
        for j, n in enumerate(names):
            delta[n], new_m[n], new_v[n] = (a[j].reshape(w[n].shape) for a in (d_, m_, v_))
    return (loss, grad_x[None], *[grads[n] for n in _WEIGHTS], *[delta[n] for n in _WEIGHTS],
            *[new_m[n] for n in _WEIGHTS], *[new_v[n] for n in _WEIGHTS])
```

```python
import functools
import math

import jax
import jax.numpy as jnp
import numpy as np
from jax import lax
from jax.experimental import pallas as pl
from jax.experimental.pallas import tpu as pltpu

D_MODEL = 2048
GRID_W = 64
PLE_DIM = 256
D_ATTN = 1024
N_HEADS = 8
N_KV = 2
HEAD_DIM = 128
ROPE_THETA = 10000.0
D_SSM = 1024
SSM_H = 16
SSM_G = 64
SSM_P = 64
D_KV = N_KV * HEAD_DIM
D_IN = 2 * D_ATTN + 2 * D_KV + 2 * D_SSM
EPS = 1e-6
Z_Q, Z_K, Z_V, Z_GA, Z_U, Z_GS = 0, 1024, 1280, 1536, 2560, 3584

ADAM_LR, ADAM_B1, ADAM_B2, ADAM_EPS, ADAM_WD, ADAM_STEP = 0.001, 0.9, 0.999, 1e-08, 0.01, 10

N_CHIPS = 4
VMEM_LIMIT_V7X = 56 * 1024 * 1024
F32 = jnp.float32
BF16 = jnp.bfloat16


def _params(sem, vmem=VMEM_LIMIT_V7X):
    return pltpu.CompilerParams(dimension_semantics=sem, vmem_limit_bytes=vmem)


def _matmul(a, b, *, ta=False, tb=False, out_dtype=F32, tm=1024, tn=None, name, exchange=None):
    M, K = (a.shape[1], a.shape[0]) if ta else a.shape
    N = b.shape[0] if tb else b.shape[1]
    if tn is None:
        tn = 1024 if (N % 1024 == 0 and K <= 4096) else 512
    tm, tn = min(tm, M), min(tn, N)
    assert M % tm == 0 and N % tn == 0, (name, M, N, K)
    dims = (((0 if ta else 1,), (1 if tb else 0,)), ((), ()))
    ex_ts, ex_sums = exchange if exchange else ((), ())
    n_ex = len(ex_ts)
    gm, gn = M // tm, N // tn

    def body(a_ref, b_ref, *rest):
        o_ref = rest[n_ex]
        if n_ex:
            ex = _ChipExchange(ex_ts, rest[:n_ex], rest[n_ex + 1:2 * n_ex + 1], rest[2 * n_ex + 1:])
            step = pl.program_id(0) * gn + pl.program_id(1)
            pl.when(step == 0)(ex.start)
        o_ref[...] = lax.dot_general(a_ref[...], b_ref[...], dims, preferred_element_type=F32).astype(o_ref.dtype)
        if n_ex:
            pl.when(step == gm * gn - 1)(ex.finish)

    a_spec = pl.BlockSpec((K, tm), lambda i, j: (0, i)) if ta else pl.BlockSpec((tm, K), lambda i, j: (i, 0))
    b_spec = pl.BlockSpec((tn, K), lambda i, j: (j, 0)) if tb else pl.BlockSpec((K, tn), lambda i, j: (0, j))
    res = pl.pallas_call(
        body, name=name,
        out_shape=[jax.ShapeDtypeStruct((M, N), out_dtype)] + (_ChipExchange.out_shape(ex_ts) if n_ex else []),
        grid=(gm, gn),
        in_specs=[a_spec, b_spec] + [_ANY] * n_ex,
        out_specs=[pl.BlockSpec((tm, tn), lambda i, j: (i, j))] + [_ANY] * n_ex,
        scratch_shapes=_ChipExchange.scratch(ex_ts) if n_ex else [],
        compiler_params=_params(("arbitrary", "arbitrary") if n_ex else ("parallel", "parallel")),
    )(a, b, *ex_sums)
    return res if n_ex else res[0]


def _rowwise(fn, rows, consts, outs, accs=(), *, tr=256, name):
    L = rows[0][0].shape[0]
    tr = math.gcd(tr, L)
    assert tr % 8 == 0 or tr == L, (name, L, tr)
    n_in, n_c, n_o, n_a = len(rows), len(consts), len(outs), len(accs)

    def body(*refs):
        ins = [r[...] for r in refs[:n_in + n_c]]
        res = fn(*ins)
        if not isinstance(res, (tuple, list)):
            res = (res,)
        o_refs = refs[n_in + n_c:n_in + n_c + n_o]
        a_refs = refs[n_in + n_c + n_o:]
        for r, v in zip(o_refs, res[:n_o]):
            r[...] = v.astype(r.dtype)
        if n_a:
            first = pl.program_id(0) == 0

            @pl.when(first)
            def _():
                for r, v in zip(a_refs, res[n_o:]):
                    r[...] = v.astype(F32)

            @pl.when(jnp.logical_not(first))
            def _():
                for r, v in zip(a_refs, res[n_o:]):
                    r[...] += v.astype(F32)

    in_specs = []
    for arr, off, w in rows:
        assert off % w == 0, (name, off, w)
        in_specs.append(pl.BlockSpec((tr, w), functools.partial(lambda i, c: (i, c), c=off // w)))
    for c in consts:
        in_specs.append(pl.BlockSpec(c.shape, lambda i: (0, 0)))
    out_shape = [jax.ShapeDtypeStruct((L, w), dt) for w, dt in outs] + [jax.ShapeDtypeStruct(s, F32) for s in accs]
    out_specs = [pl.BlockSpec((tr, w), lambda i: (i, 0)) for w, _ in outs] + [pl.BlockSpec(s, lambda i: (0, 0)) for s in accs]
    res = pl.pallas_call(
        body, name=name,
        out_shape=out_shape,
        grid=(L // tr,),
        in_specs=in_specs,
        out_specs=out_specs,
        compiler_params=_params(("arbitrary",) if n_a else ("parallel",)),
    )(*[r[0] for r in rows], *consts)
    return res


def _sig(x):
    return jax.nn.sigmoid(x)


def _silu_and_grad(x):
    s = _sig(x)
    return x * s, s * (1.0 + x * (1.0 - s))


_GELU_C = math.sqrt(2.0 / math.pi)


def _gelu(x):
    return 0.5 * x * (1.0 + jnp.tanh(_GELU_C * (x + 0.044715 * x * x * x)))


def _gelu_grad(x):
    t = jnp.tanh(_GELU_C * (x + 0.044715 * x * x * x))
    return 0.5 * (1.0 + t) + 0.5 * x * (1.0 - t * t) * _GELU_C * (1.0 + 3.0 * 0.044715 * x * x)


def _rms(x):
    return lax.rsqrt(jnp.mean(x * x, axis=-1, keepdims=True) + EPS)


def _rms_bwd(x, g, dy):
    r = _rms(x)
    n = x * r
    dn = dy * g
    return r * (dn - n * jnp.mean(dn * n, axis=-1, keepdims=True)), dy * n


def _colsum(v):
    return jnp.sum(v, axis=0, keepdims=True)


def _rope_partner(x):
    lane = lax.broadcasted_iota(jnp.int32, x.shape, x.ndim - 1)
    return jnp.where(lane % 64 < 32, pltpu.roll(x, 96, x.ndim - 1), pltpu.roll(x, 32, x.ndim - 1))


def _rope_tables(L):
    rows_n = L // GRID_W
    rows = np.repeat(np.arange(rows_n), GRID_W).astype(np.float32)
    cols = np.tile(np.arange(GRID_W), rows_n).astype(np.float32)
    n_freq = HEAD_DIM // 4
    inv_freq = np.float32(ROPE_THETA) ** (-np.arange(n_freq, dtype=np.float32) / np.float32(n_freq))
    ar, ac = rows[:, None] * inv_freq[None, :], cols[:, None] * inv_freq[None, :]
    cos = np.concatenate([np.cos(ar), np.cos(ar), np.cos(ac), np.cos(ac)], axis=-1).astype(np.float32)
    sin = np.concatenate([-np.sin(ar), np.sin(ar), -np.sin(ac), np.sin(ac)], axis=-1).astype(np.float32)
    return jnp.asarray(cos), jnp.asarray(sin)


def _heads(v):
    return [v[:, h * HEAD_DIM:(h + 1) * HEAD_DIM] for h in range(v.shape[1] // HEAD_DIM)]


def _attn_prep(z, q_norm, k_norm, cos, sin):
    def fn(q, k, v, cos, sin, gq, gk):
        def one(xh, g):
            xn = xh * _rms(xh) * g
            return xn * cos + _rope_partner(xn) * sin
        qr = jnp.concatenate([one(h, gq) for h in _heads(q)], axis=1)
        kr = jnp.concatenate([one(h, gk) for h in _heads(k)], axis=1)
        return qr, kr, v
    return _rowwise(fn, [(z, Z_Q, D_ATTN), (z, Z_K, D_KV), (z, Z_V, D_KV), (cos, 0, HEAD_DIM), (sin, 0, HEAD_DIM)],
                    [q_norm, k_norm], [(D_ATTN, BF16), (D_KV, BF16), (D_KV, BF16)], name="attn_prep")


def _attn_prep_bwd(dqr, dkr, z, q_norm, k_norm, cos, sin):
    def fn(dqr, dkr, q, k, cos, sin, gq, gk):
        def one(dyh, xh, g):
            dn = dyh * cos + _rope_partner(dyh * sin)
            return _rms_bwd(xh, g, dn)
        rq = [one(a, b, gq) for a, b in zip(_heads(dqr), _heads(q))]
        rk = [one(a, b, gk) for a, b in zip(_heads(dkr), _heads(k))]
        dq = jnp.concatenate([r[0] for r in rq], axis=1)
        dk = jnp.concatenate([r[0] for r in rk], axis=1)
        return dq, dk, _colsum(sum(r[1] for r in rq)), _colsum(sum(r[1] for r in rk))
    return _rowwise(fn, [(dqr, 0, D_ATTN), (dkr, 0, D_KV), (z, Z_Q, D_ATTN), (z, Z_K, D_KV), (cos, 0, HEAD_DIM), (sin, 0, HEAD_DIM)],
                    [q_norm, k_norm], [(D_ATTN, BF16), (D_KV, BF16)], [(1, HEAD_DIM), (1, HEAD_DIM)], name="attn_prep_bwd")


_QK_T = (((1,), (1,)), ((), ()))
_TA = (((0,), (0,)), ((), ()))
_REP = N_HEADS // N_KV


_EXP2_SCALE = HEAD_DIM ** -0.5 * math.log2(math.e)
ATTN_FWD_KEY_CHUNKS = 4
ATTN_BWD_KEY_CHUNKS = 8


def _attn_fwd(qr, kr, vb, g_ts, g_shards, *, tq=1024):
    L = qr.shape[0]
    tq = min(tq, L)
    kc = L // ATTN_FWD_KEY_CHUNKS
    n_g = len(g_ts)
    grid = (N_HEADS, L // tq)
    steps = grid[0] * grid[1]

    def body(q_ref, k_ref, v_ref, *rest):
        o_ref, lse_ref = rest[n_g:n_g + 2]
        g = _Gather(g_ts, rest[:n_g], rest[n_g + 2:2 * n_g + 2], rest[2 * n_g + 2:3 * n_g + 2], rest[3 * n_g + 2:])
        step = pl.program_id(0) * grid[1] + pl.program_id(1)
        pl.when(step == 0)(g.start)
        pl.when(step == (3 * steps) // 4)(g.forward)
        q = q_ref[...]
        m = jnp.full((tq, 1), -jnp.inf, F32)
        l = jnp.zeros((tq, 1), F32)
        o = jnp.zeros((tq, HEAD_DIM), F32)
        for c in range(ATTN_FWD_KEY_CHUNKS):
            ks = slice(c * kc, (c + 1) * kc)
            s = lax.dot_general(q, k_ref[ks, :], _QK_T, preferred_element_type=F32)
            m_new = jnp.maximum(m, jnp.max(s, axis=1, keepdims=True))
            a = jnp.exp2((m - m_new) * _EXP2_SCALE)
            p = jnp.exp2((s - m_new) * _EXP2_SCALE)
            l = a * l + jnp.sum(p, axis=1, keepdims=True)
            o = a * o + jnp.dot(p.astype(BF16), v_ref[ks, :], preferred_element_type=F32)
            m = m_new
        o_ref[...] = o * (1.0 / l)
        lse_ref[...] = m * _EXP2_SCALE + jnp.log2(l)
        pl.when(step == steps - 1)(g.finish)

    kv = pl.BlockSpec((L, HEAD_DIM), lambda h, i: (0, h // _REP))
    return pl.pallas_call(
        body, name="attn_fwd",
        out_shape=[jax.ShapeDtypeStruct((L, D_ATTN), F32), jax.ShapeDtypeStruct((N_HEADS, L, 1), F32)] +
                  [jax.ShapeDtypeStruct(_BIG[t][1], BF16) for t in g_ts],
        grid=grid,
        in_specs=[pl.BlockSpec((tq, HEAD_DIM), lambda h, i: (i, h)), kv, kv] + [_ANY] * n_g,
        out_specs=[pl.BlockSpec((tq, HEAD_DIM), lambda h, i: (i, h)),
                   pl.BlockSpec((None, tq, 1), lambda h, i: (h, i, 0))] + [_ANY] * n_g,
        scratch_shapes=_Gather.scratch(g_shards),
        compiler_params=_params(("arbitrary", "arbitrary")),
    )(qr, kr, vb, *g_shards)


def _attn_bwd(qr, kr, k_t, vb, do, lse, delta, ex_ts, ex_sums, *, tq=1024):
    L = qr.shape[0]
    tq = min(tq, L)
    scale = HEAD_DIM ** -0.5
    kc = L // ATTN_BWD_KEY_CHUNKS
    n_ex = len(ex_ts)
    grid = (N_KV, _REP, L // tq)

    def body(q_ref, k_ref, kt_ref, v_ref, do_ref, lse_ref, delta_ref, *rest):
        dq_ref, dk_ref, dv_ref = rest[n_ex:n_ex + 3]
        ex = _ChipExchange(ex_ts, rest[:n_ex], rest[n_ex + 3:2 * n_ex + 3], rest[2 * n_ex + 3:])
        step = (pl.program_id(0) * grid[1] + pl.program_id(1)) * grid[2] + pl.program_id(2)
        pl.when(step == 0)(ex.start)

        @pl.when((pl.program_id(1) == 0) & (pl.program_id(2) == 0))
        def _():
            dk_ref[...] = jnp.zeros_like(dk_ref)
            dv_ref[...] = jnp.zeros_like(dv_ref)

        q, do, lse, delta = q_ref[...], do_ref[...], lse_ref[...], delta_ref[...]
        dq_t = 0.0
        for c in range(ATTN_BWD_KEY_CHUNKS):
            ks = slice(c * kc, (c + 1) * kc)
            st = lax.dot_general(k_ref[ks, :], q, _QK_T, preferred_element_type=F32)
            p = jnp.exp2(st * _EXP2_SCALE - lse)
            dv_ref[ks, :] += jnp.dot(p.astype(BF16), do, preferred_element_type=F32)
            dp = lax.dot_general(v_ref[ks, :], do, _QK_T, preferred_element_type=F32)
            ds = (p * (dp - delta) * scale).astype(BF16)
            dk_ref[ks, :] += jnp.dot(ds, q, preferred_element_type=F32)
            dq_t = dq_t + jnp.dot(kt_ref[:, ks], ds, preferred_element_type=F32)
        dq_ref[...] = dq_t.T
        pl.when(step == grid[0] * grid[1] * grid[2] - 1)(ex.finish)

    head = lambda g, r, i: (i, g * _REP + r)
    kv = pl.BlockSpec((L, HEAD_DIM), lambda g, r, i: (0, g))
    per_query = pl.BlockSpec((None, 1, tq), lambda g, r, i: (g * _REP + r, 0, i))
    return pl.pallas_call(
        body, name="attn_bwd",
        out_shape=[jax.ShapeDtypeStruct((L, D_ATTN), F32), jax.ShapeDtypeStruct((L, D_KV), F32), jax.ShapeDtypeStruct((L, D_KV), F32)] +
                  _ChipExchange.out_shape(ex_ts),
        grid=grid,
        in_specs=[pl.BlockSpec((tq, HEAD_DIM), head), kv,
                  pl.BlockSpec((HEAD_DIM, L), lambda g, r, i: (g, 0)), kv,
                  pl.BlockSpec((tq, HEAD_DIM), head), per_query, per_query] + [_ANY] * n_ex,
        out_specs=[pl.BlockSpec((tq, HEAD_DIM), head), kv, kv] + [_ANY] * n_ex,
        scratch_shapes=_ChipExchange.scratch(ex_ts),
        compiler_params=_params(("arbitrary", "arbitrary", "arbitrary")),
    )(qr, kr, k_t, vb, do, lse, delta, *ex_sums)


SSM_BLK = 8
SSM_NB = SSM_G // SSM_BLK
SSM_SEG = 8
SSM_UNROLL = 4


def _unrolled_loop(n, step, carry):
    u = SSM_UNROLL

    def trip(i, c):
        for j in range(u):
            c = step(i * u + j, c)
        return c
    carry = lax.fori_loop(0, n // u, trip, carry)
    for t in range(n - n % u, n):
        carry = step(jnp.int32(t), carry)
    return carry


def _cplx_pow2(a, b, n):
    for _ in range(int(math.log2(n))):
        a, b = a * a - b * b, 2.0 * a * b
    return a, b


def _seg_scan(ref, a, b, T, reverse, entry=None, tap=None):
    npair = len(a)
    zero = jnp.zeros((SSM_SEG, 128), F32)

    def make_step(store, tap_fn=None):
        def step(t, carry):
            lt = (T - 1 - t) if reverse else t
            row = pl.multiple_of(lt * SSM_SEG, SSM_SEG)
            blk = ref[pl.ds(row, SSM_SEG), :]
            new = []
            for q in range(npair):
                re, im = carry[2 * q], carry[2 * q + 1]
                nre = a[q] * re - b[q] * im + blk[:, q * 256:q * 256 + 128]
                nim = a[q] * im + b[q] * re + blk[:, q * 256 + 128:q * 256 + 256]
                new += [nre, nim]
            if store:
                ref[pl.ds(row, SSM_SEG), :] = jnp.concatenate(new, axis=1)
            extra = carry[2 * npair:]
            return tuple(new) + (tuple(tap_fn(lt, new, extra)) if tap_fn else tuple(extra))
        return step

    def second_pass(init):
        if tap is None:
            _unrolled_loop(T, make_step(True), tuple(init))
            return init
        carry = _unrolled_loop(T - 1, make_step(True, tap[0]), tuple(init) + tuple(tap[2]))
        carry = make_step(True, tap[1])(jnp.int32(T - 1), carry)
        return init, carry[2 * npair:]

    if entry is not None:
        return second_pass(entry)
    ends = _unrolled_loop(T, make_step(False), (zero,) * (2 * npair))
    sub = lax.broadcasted_iota(jnp.int32, (SSM_SEG, 128), 0)
    keep = (sub != SSM_SEG - 1) if reverse else (sub != 0)
    shift = (SSM_SEG - 1) if reverse else 1
    init = []
    for q in range(npair):
        pa, pb = _cplx_pow2(a[q], b[q], T)
        xr, xi = zero, zero
        for _ in range(SSM_SEG - 1):
            fr = ends[2 * q] + pa * xr - pb * xi
            fi = ends[2 * q + 1] + pa * xi + pb * xr
            xr = jnp.where(keep, pltpu.roll(fr, shift, 0), 0.0)
            xi = jnp.where(keep, pltpu.roll(fi, shift, 0), 0.0)
        init += [xr, xi]
    return second_pass(init)


SSM_BW = SSM_BLK * SSM_H
SSM_SW = SSM_BLK * 2 * SSM_P
SSM_NPAIR = SSM_BLK // 2


def _seg_perm(a):
    L, C = a.shape
    return a.reshape(SSM_SEG, L // SSM_SEG, C).transpose(1, 0, 2).reshape(L, C)


def _seg_unperm(a):
    L, C = a.shape
    return a.reshape(L // SSM_SEG, SSM_SEG, C).transpose(1, 0, 2).reshape(L, C)


def _lam_rows(are_ref, aim_ref, d):
    a = [jnp.broadcast_to(are_ref[d, j:j + 1, :], (SSM_SEG, 128)) for j in range(SSM_NPAIR)]
    b = [jnp.broadcast_to(aim_ref[d, j:j + 1, :], (SSM_SEG, 128)) for j in range(SSM_NPAIR)]
    return a, b


_PAIR_SPEC = pl.BlockSpec((None, 2, SSM_NPAIR, 2, 2 * SSM_H, 128), lambda g: (g, 0, 0, 0, 0, 0))


def _pair_window(j, r):
    return slice(j * 2 * SSM_H, (j + 1) * 2 * SSM_H), slice(j * 256 + r * 128, j * 256 + (r + 1) * 128)


def _expand_pairs(c_ref, dense_ref):
    dense_ref[...] = jnp.zeros_like(dense_ref)
    for d in range(2):
        for j in range(SSM_NPAIR):
            for r in range(2):
                rows, cols = _pair_window(j, r)
                dense_ref[d, rows, cols] = c_ref[d, j, r].astype(dense_ref.dtype)


def _ssm_fwd(u_p, wb, wc, are, aim, dvec):
    L = u_p.shape[0]
    T = L // SSM_SEG
    RC = min(512, L)

    def body(u_ref, wb_ref, wc_ref, are_ref, aim_ref, d_ref, y_ref, ge_ref, entry_ref, x_scr, wb_s, wc_s):
        _expand_pairs(wb_ref, wb_s)
        _expand_pairs(wc_ref, wc_s)
        y_ref[...] = u_ref[...] * d_ref[...]
        for d in range(2):
            def bu_chunk(c, _):
                rows = pl.ds(pl.multiple_of(c * RC, RC), RC)
                x_scr[rows, :] = jnp.dot(u_ref[rows, :].astype(BF16), wb_s[d], preferred_element_type=F32)
                return 0
            lax.fori_loop(0, L // RC, bu_chunk, 0)
            a, b = _lam_rows(are_ref, aim_ref, d)
            entry_ref[d] = jnp.concatenate(_seg_scan(x_scr, a, b, T, reverse=(d == 1)), axis=1)

            def y_chunk(c, _):
                rows = pl.ds(pl.multiple_of(c * RC, RC), RC)
                y_ref[rows, :] += lax.dot_general(x_scr[rows, :].astype(BF16), wc_s[d], _QK_T, preferred_element_type=F32)
                return 0
            lax.fori_loop(0, L // RC, y_chunk, 0)
        ge_ref[...] = _gelu(y_ref[...]).astype(ge_ref.dtype)

    blk4 = lambda g: (g, 0, 0, 0)
    chan = pl.BlockSpec((L, SSM_BW), lambda g: (0, g))
    return pl.pallas_call(
        body, name="ssm_fwd",
        out_shape=[jax.ShapeDtypeStruct((L, D_SSM), F32), jax.ShapeDtypeStruct((L, D_SSM), BF16),
                   jax.ShapeDtypeStruct((SSM_NB, 2, SSM_SEG, SSM_SW), F32)],
        grid=(SSM_NB,),
        in_specs=[chan, _PAIR_SPEC, _PAIR_SPEC,
                  pl.BlockSpec((None, 2, SSM_NPAIR, 128), blk4),
                  pl.BlockSpec((None, 2, SSM_NPAIR, 128), blk4),
                  pl.BlockSpec((1, SSM_BW), lambda g: (0, g))],
        out_specs=[chan, chan, pl.BlockSpec((None, 2, SSM_SEG, SSM_SW), blk4)],
        scratch_shapes=[pltpu.VMEM((L, SSM_SW), F32), pltpu.VMEM((2, SSM_BW, SSM_SW), BF16), pltpu.VMEM((2, SSM_BW, SSM_SW), BF16)],
        compiler_params=_params(("parallel",)),
    )(u_p, wb, wc, are, aim, dvec)


def _ssm_bwd(u_p, dy_p, entry, wb, wc, are, aim, dvec, sib_ts, sib_grads):
    L = u_p.shape[0]
    T = L // SSM_SEG
    RC = min(512, L)
    n_sib = len(sib_ts)

    def lam_acc(acc, s, x):
        new = []
        for q in range(SSM_NPAIR):
            sr, si, xr, xi = s[2 * q], s[2 * q + 1], x[2 * q], x[2 * q + 1]
            new += [acc[2 * q] + sr * xr + si * xi, acc[2 * q + 1] + si * xr - sr * xi]
        return tuple(new)

    def body(u_ref, dy_ref, entry_ref, wb_ref, wc_ref, are_ref, aim_ref, d_ref, *rest):
        du_ref, dwb_ref, dwc_ref, dare_ref, daim_ref, dd_ref = rest[n_sib:n_sib + 6]
        x_scr, s_scr, wb_s, wc_s, dwb_s, dwc_s = rest[2 * n_sib + 6:2 * n_sib + 12]
        sib = _SiblingExchange(sib_ts, rest[:n_sib], rest[n_sib + 6:2 * n_sib + 6], rest[2 * n_sib + 12:])
        pl.when(pl.program_id(0) == 0)(sib.start)
        _expand_pairs(wb_ref, wb_s)
        _expand_pairs(wc_ref, wc_s)
        du_ref[...] = dy_ref[...] * d_ref[...]
        dd_ref[...] = _colsum(dy_ref[...] * u_ref[...])
        dwb_s[...] = jnp.zeros_like(dwb_s)
        dwc_s[...] = jnp.zeros_like(dwc_s)
        for d in range(2):
            rev = d == 1

            def in_chunk(c, _):
                rows = pl.ds(pl.multiple_of(c * RC, RC), RC)
                x_scr[rows, :] = jnp.dot(u_ref[rows, :].astype(BF16), wb_s[d], preferred_element_type=F32)
                s_scr[rows, :] = jnp.dot(dy_ref[rows, :].astype(BF16), wc_s[d], preferred_element_type=F32)
                return 0
            lax.fori_loop(0, L // RC, in_chunk, 0)
            a, b = _lam_rows(are_ref, aim_ref, d)
            x_in = _seg_scan(x_scr, a, b, T, reverse=rev, entry=[entry_ref[d, :, q * 128:(q + 1) * 128] for q in range(2 * SSM_NPAIR)])

            def pair_with_row(lt, s_new, acc):
                xrow = pl.multiple_of((lt + 1 if rev else lt - 1) * SSM_SEG, SSM_SEG)
                xb = x_scr[pl.ds(xrow, SSM_SEG), :]
                return lam_acc(acc, s_new, [xb[:, i * 128:(i + 1) * 128] for i in range(2 * SSM_NPAIR)])

            zeros = (jnp.zeros((SSM_SEG, 128), F32),) * (2 * SSM_NPAIR)
            _, acc = _seg_scan(s_scr, a, [-v for v in b], T, reverse=not rev,
                               tap=(pair_with_row, lambda lt, s_new, acc: lam_acc(acc, s_new, x_in), zeros))
            for q in range(SSM_NPAIR):
                dare_ref[d, q:q + 1, :] = _colsum(acc[2 * q])
                daim_ref[d, q:q + 1, :] = _colsum(acc[2 * q + 1])

            def out_chunk(c, _):
                rows = pl.ds(pl.multiple_of(c * RC, RC), RC)
                xs, ss = x_scr[rows, :].astype(BF16), s_scr[rows, :].astype(BF16)
                uu, dd = u_ref[rows, :].astype(BF16), dy_ref[rows, :].astype(BF16)
                dwc_s[d] += lax.dot_general(dd, xs, _TA, preferred_element_type=F32)
                dwb_s[d] += lax.dot_general(uu, ss, _TA, preferred_element_type=F32)
                du_ref[rows, :] += lax.dot_general(ss, wb_s[d], _QK_T, preferred_element_type=F32)
                return 0
            lax.fori_loop(0, L // RC, out_chunk, 0)
            for j in range(SSM_NPAIR):
                for r in range(2):
                    rows, cols = _pair_window(j, r)
                    dwb_ref[d, j, r] = dwb_s[d, rows, cols]
                    dwc_ref[d, j, r] = dwc_s[d, rows, cols]
        pl.when(pl.program_id(0) == SSM_NB - 1)(sib.finish)

    blk4 = lambda g: (g, 0, 0, 0)
    chan = pl.BlockSpec((L, SSM_BW), lambda g: (0, g))
    par_specs = [_PAIR_SPEC, _PAIR_SPEC,
                 pl.BlockSpec((None, 2, SSM_NPAIR, 128), blk4),
                 pl.BlockSpec((None, 2, SSM_NPAIR, 128), blk4),
                 pl.BlockSpec((1, SSM_BW), lambda g: (0, g))]
    dense = lambda dt: pltpu.VMEM((2, SSM_BW, SSM_SW), dt)
    return pl.pallas_call(
        body, name="ssm_bwd",
        out_shape=[jax.ShapeDtypeStruct((L, D_SSM), F32),
                   jax.ShapeDtypeStruct(wb.shape, F32),
                   jax.ShapeDtypeStruct(wc.shape, F32),
                   jax.ShapeDtypeStruct((SSM_NB, 2, SSM_NPAIR, 128), F32),
                   jax.ShapeDtypeStruct((SSM_NB, 2, SSM_NPAIR, 128), F32),
                   jax.ShapeDtypeStruct((1, D_SSM), F32)] + _SiblingExchange.out_shape(sib_ts),
        grid=(SSM_NB,),
        in_specs=[chan, chan, pl.BlockSpec((None, 2, SSM_SEG, SSM_SW), blk4)] + par_specs + [_ANY] * n_sib,
        out_specs=[chan] + par_specs + [_ANY] * n_sib,
        scratch_shapes=[pltpu.VMEM((L, SSM_SW), F32), pltpu.VMEM((L, SSM_SW), F32), dense(BF16), dense(BF16), dense(F32), dense(F32)] +
                       _SiblingExchange.scratch(sib_ts),
        compiler_params=_params(("arbitrary",)),
    )(u_p, dy_p, entry, wb, wc, are, aim, dvec, *sib_grads)


def _ssm_disc(a_re, a_im, log_dt, b_re, b_im):
    lam = lax.complex(jnp.minimum(a_re, -1e-4), a_im)
    dt = jnp.exp(log_dt)[..., None]
    lam_bar = jnp.exp(lam * dt)
    b_bar = ((lam_bar - 1.0) / lam)[..., None] * lax.complex(b_re, b_im)
    return jnp.real(lam_bar), jnp.imag(lam_bar), jnp.real(b_bar), jnp.imag(b_bar)


_EYE2 = np.eye(2, dtype=np.float32)[:, None, :, None]


def _to_pairs(t):
    t = t.reshape(2, SSM_NB, SSM_NPAIR, 2, 2, SSM_H, SSM_P).transpose(1, 0, 2, 4, 3, 5, 6)
    return (t[..., None, :] * _EYE2).reshape(SSM_NB, 2, SSM_NPAIR, 2, 2 * SSM_H, 2 * SSM_P)


def _from_pairs(c):
    t = c.reshape(SSM_NB, 2, SSM_NPAIR, 2, 2, SSM_H, 2, SSM_P)
    t = jnp.stack([t[:, :, :, :, 0, :, 0, :], t[:, :, :, :, 1, :, 1, :]], axis=4)
    return t.transpose(1, 0, 2, 4, 3, 5, 6).reshape(2, SSM_G, 2, SSM_H, SSM_P)


def _to_lam(v):
    return v.reshape(2, SSM_NB, SSM_NPAIR, 128).transpose(1, 0, 2, 3)


def _from_lam(v):
    return v.transpose(1, 0, 2, 3).reshape(2, SSM_G, SSM_P)


_MESH = pl.DeviceIdType.MESH
_ANY = pl.BlockSpec(memory_space=pl.ANY)
_BIG = (("w_in", (D_MODEL, D_IN), 1, D_IN // N_CHIPS),
        ("w_glu", (D_SSM, 2 * D_SSM), 1, 2 * D_SSM // N_CHIPS),
        ("w_out", (D_ATTN + D_SSM, D_MODEL), 0, (D_ATTN + D_SSM) // N_CHIPS),
        ("w_ple_gate", (D_MODEL, D_MODEL), 0, D_MODEL // N_CHIPS),
        ("w_ple_proj", (PLE_DIM, D_MODEL), 1, D_MODEL // N_CHIPS))


def _place():
    x, y, c = lax.axis_index("x"), lax.axis_index("y"), lax.axis_index("c")
    return x, y, c, [(1 - x, y), (x, 1 - y), (1 - x, 1 - y)]


class _Gather:
    def __init__(self, ts, srcs, dsts, stage, sems):
        self.ts, self.srcs, self.dsts, self.stage = ts, srcs, dsts, stage
        self.send_sems, self.recv_sems, self.fwd_send_sems, self.fwd_recv_sems, self.loc_sems = sems
        self.x, self.y, self.c, self.chips = _place()
        self.n = len(ts)

    @staticmethod
    def scratch(shards):
        sems = pltpu.SemaphoreType.DMA((3, len(shards)))
        return [pltpu.VMEM(s.shape, BF16) for s in shards] + [sems, sems, sems, sems, pltpu.SemaphoreType.DMA((len(shards),))]

    def _shard_of(self, i, kk):
        _, _, axis, sz = _BIG[self.ts[i]]
        sl = pl.ds(pl.multiple_of(kk * sz, sz), sz)
        return self.dsts[i].at[:, sl] if axis == 1 else self.dsts[i].at[sl, :]

    @staticmethod
    def _half_of(ref, cc):
        n = ref.shape[0] // 2
        return ref.at[pl.ds(pl.multiple_of(cc * n, n), n), :]

    def _ici(self, j, i, kk):
        px, py = self.chips[j]
        return pltpu.make_async_remote_copy(
            src_ref=self._half_of(self.srcs[i], self.c), dst_ref=self._half_of(self._shard_of(i, kk), self.c),
            send_sem=self.send_sems.at[j, i], recv_sem=self.recv_sems.at[j, i],
            device_id=(px, py, self.c), device_id_type=_MESH)

    def _forward(self, j, i, kk, cc):
        part = self._half_of(self._shard_of(i, kk), cc)
        return pltpu.make_async_remote_copy(
            src_ref=part, dst_ref=part, send_sem=self.fwd_send_sems.at[j, i], recv_sem=self.fwd_recv_sems.at[j, i],
            device_id=(self.x, self.y, 1 - self.c), device_id_type=_MESH)

    def _load(self, i):
        return pltpu.make_async_copy(self.srcs[i], self.stage[i], self.loc_sems.at[i])

    def _place_own(self, i):
        return pltpu.make_async_copy(self.stage[i], self._shard_of(i, 2 * self.x + self.y), self.loc_sems.at[i])

    def _peers(self):
        return [(i, j, 2 * px + py) for i in range(self.n) for j, (px, py) in enumerate(self.chips)]

    def start(self):
        for i in range(self.n):
            self._load(i).start()
        for i, j, _ in self._peers():
            self._ici(j, i, 2 * self.x + self.y).start()

    def forward(self):
        for i in range(self.n):
            self._load(i).wait()
            self._place_own(i).start()
        for i, j, kk in self._peers():
            self._ici(j, i, kk).wait_recv()
            self._forward(j, i, kk, self.c).start()

    def finish(self):
        for i, j, kk in self._peers():
            self._forward(j, i, kk, 1 - self.c).wait_recv()
        for i, j, kk in self._peers():
            self._ici(j, i, kk).wait_send()
            self._forward(j, i, kk, self.c).wait_send()
        for i in range(self.n):
            self._place_own(i).wait()


def _matmul_in_gather(a, shard, *, tm=1024):
    t = 0
    (_, (K, N), _, sz) = _BIG[t]
    M = a.shape[0]
    tm = min(tm, M)
    gm = M // tm
    x, y = lax.axis_index("x"), lax.axis_index("y")
    order = jnp.stack([2 * x + y, 2 * (1 - x) + y, 2 * x + 1 - y, 2 * (1 - x) + 1 - y]).astype(jnp.int32)

    def body(order_ref, a_ref, shard_ref, z_ref, full_ref, b_vm, *sems):
        g = _Gather([t], [shard_ref], [full_ref], [b_vm], sems[:5])
        load_sem = sems[5]
        s, i = pl.program_id(0), pl.program_id(1)

        @pl.when((s == 0) & (i == 0))
        def _():
            g.start()
            g._load(0).wait()
            g._place_own(0).start()

        for j, (px, py) in enumerate(g.chips):
            @pl.when((s == j + 1) & (i == 0))
            def _(j=j, kk=2 * px + py):
                if j == 0:
                    g._place_own(0).wait()
                g._ici(j, 0, kk).wait_recv()
                g._forward(j, 0, kk, g.c).start()
                g._forward(j, 0, kk, 1 - g.c).wait_recv()
                cp = pltpu.make_async_copy(g._shard_of(0, kk), b_vm, load_sem.at[0])
                cp.start()
                cp.wait()

        z_ref[...] = jnp.dot(a_ref[...], b_vm[...], preferred_element_type=F32)

        @pl.when((s == N_CHIPS - 1) & (i == gm - 1))
        def _():
            for j, (px, py) in enumerate(g.chips):
                g._ici(j, 0, 2 * px + py).wait_send()
                g._forward(j, 0, 2 * px + py, g.c).wait_send()

    return pl.pallas_call(
        body, name="mm_in",
        out_shape=[jax.ShapeDtypeStruct((M, N), F32), jax.ShapeDtypeStruct((K, N), BF16)],
        grid_spec=pltpu.PrefetchScalarGridSpec(
            num_scalar_prefetch=1, grid=(N_CHIPS, gm),
            in_specs=[pl.BlockSpec((tm, K), lambda s, i, o: (i, 0)), _ANY],
            out_specs=[pl.BlockSpec((tm, sz), lambda s, i, o: (i, o[s])), _ANY],
            scratch_shapes=_Gather.scratch([shard]) + [pltpu.SemaphoreType.DMA((1,))]),
        compiler_params=_params(("arbitrary", "arbitrary")),
    )(order, a, shard)


SMALL_W = 1024
SMALL_ROWS = 72
N_SMALL = 8 * SMALL_ROWS * SMALL_W
_RED = tuple((shape, ax, (shape[0] // 2, sz) if ax == 1 else (sz // 2, shape[1]), BF16) for _, shape, ax, sz in _BIG) + \
    (((8 * SMALL_ROWS, SMALL_W), 0, (SMALL_ROWS, SMALL_W), F32),)
_RED_TR = 128


def _piece(ref, t, kk, cc):
    _, ax, (pr, pc), _ = _RED[t]
    if ax == 1:
        return ref.at[pl.ds(pl.multiple_of(cc * pr, pr), pr), pl.ds(pl.multiple_of(kk * pc, pc), pc)]
    return ref.at[pl.ds(pl.multiple_of((2 * kk + cc) * pr, pr), pr), :]


def _half_shape(t):
    shape, ax, (pr, pc), _ = _RED[t]
    return (pr, shape[1]) if ax == 1 else (N_CHIPS * pr, pc)


def _piece_in_half(ref, t, kk):
    _, ax, (pr, pc), _ = _RED[t]
    return ref.at[:, pl.ds(pl.multiple_of(kk * pc, pc), pc)] if ax == 1 else ref.at[pl.ds(pl.multiple_of(kk * pr, pr), pr), :]


class _SiblingExchange:
    def __init__(self, ts, srcs, dsts, sems):
        send_sems, recv_sems = sems
        x, y, c, _ = _place()

        def copies():
            pairs = []
            for i, t in enumerate(ts):
                _, ax, (pr, _), _ = _RED[t]
                if ax == 1:
                    pairs.append((srcs[i].at[pl.ds(pl.multiple_of((1 - c) * pr, pr), pr), :], dsts[i]))
                else:
                    pairs += [(_piece(srcs[i], t, kk, 1 - c), _piece_in_half(dsts[i], t, kk)) for kk in range(N_CHIPS)]
            return [pltpu.make_async_remote_copy(src_ref=s, dst_ref=d, send_sem=send_sems.at[i], recv_sem=recv_sems.at[i],
                                                 device_id=(x, y, 1 - c), device_id_type=_MESH) for i, (s, d) in enumerate(pairs)]
        self.copies = copies

    @staticmethod
    def scratch(ts):
        n_dma = sum(1 if _RED[t][1] == 1 else N_CHIPS for t in ts)
        return [pltpu.SemaphoreType.DMA((n_dma,)), pltpu.SemaphoreType.DMA((n_dma,))]

    @staticmethod
    def out_shape(ts):
        return [jax.ShapeDtypeStruct(_half_shape(t), F32) for t in ts]

    def start(self):
        for cp in self.copies():
            cp.start()

    def finish(self):
        for cp in self.copies():
            cp.wait()


def _grad_sibling_exchange(ts, grads, name):
    n = len(ts)

    def body(*refs):
        ex = _SiblingExchange(ts, refs[:n], refs[n:2 * n], refs[2 * n:])
        ex.start()
        ex.finish()

    return pl.pallas_call(
        body, name=name,
        out_shape=_SiblingExchange.out_shape(ts),
        in_specs=[_ANY] * n, out_specs=[_ANY] * n,
        scratch_shapes=_SiblingExchange.scratch(ts),
    )(*grads)


def _chip_sum(t, g, rs, place):
    shape, ax, (pr, pc), dt = _RED[t]
    W = shape[1]
    tr = min(pr, _RED_TR)
    nb = pr // tr

    def body(place_ref, g_ref, rs_ref, o_ref):
        o_ref[...] = (g_ref[...] + rs_ref[...]).astype(o_ref.dtype)

    return pl.pallas_call(
        body, name="grad_chip_sum_%d" % t,
        out_shape=jax.ShapeDtypeStruct(rs.shape, dt),
        grid_spec=pltpu.PrefetchScalarGridSpec(
            num_scalar_prefetch=1, grid=(1 if ax == 1 else N_CHIPS, nb),
            in_specs=[pl.BlockSpec((tr, W), lambda kk, i, pr_: ((2 * kk + pr_[0]) * nb + i, 0)),
                      pl.BlockSpec((tr, W), lambda kk, i, pr_: (kk * nb + i, 0))],
            out_specs=pl.BlockSpec((tr, W), lambda kk, i, pr_: (kk * nb + i, 0))),
        compiler_params=_params(("parallel", "parallel")),
    )(place, g, rs)


class _ChipExchange:
    def __init__(self, ts, srcs, dsts, sems):
        self.send_sems, self.recv_sems = sems
        x, y, c, chips = _place()
        self.copies = lambda: [
            pltpu.make_async_remote_copy(src_ref=_piece_in_half(srcs[i], t, 2 * px + py), dst_ref=dsts[i].at[j],
                                         send_sem=self.send_sems.at[j, i], recv_sem=self.recv_sems.at[j, i],
                                         device_id=(px, py, c), device_id_type=_MESH)
            for i, t in enumerate(ts) for j, (px, py) in enumerate(chips)]

    @staticmethod
    def scratch(ts):
        return [pltpu.SemaphoreType.DMA((3, len(ts))), pltpu.SemaphoreType.DMA((3, len(ts)))]

    @staticmethod
    def out_shape(ts):
        return [jax.ShapeDtypeStruct((3,) + _RED[t][2], _RED[t][3]) for t in ts]

    def start(self):
        for cp in self.copies():
            cp.start()

    def finish(self):
        for cp in self.copies():
            cp.wait()


def _grad_chip_exchange(ts, sums):
    n = len(ts)

    def body(*refs):
        ex = _ChipExchange(ts, refs[:n], refs[n:2 * n], refs[2 * n:])
        ex.start()
        ex.finish()

    return pl.pallas_call(
        body, name="grad_chip_exchange",
        out_shape=_ChipExchange.out_shape(ts),
        in_specs=[_ANY] * n, out_specs=[_ANY] * n,
        scratch_shapes=_ChipExchange.scratch(ts),
    )(*sums)


def _total_sum(t, g, rs, rc, place):
    shape, ax, (pr, pc), _ = _RED[t]
    tr = min(pr, _RED_TR)
    nb = pr // tr
    small = t == len(_RED) - 1

    def body(place_ref, g_ref, rs_ref, rc_ref, o_ref):
        o_ref[...] = (g_ref[...] + rs_ref[...]) + rc_ref[0].astype(F32) + rc_ref[1].astype(F32) + rc_ref[2].astype(F32)

    if ax == 1:
        g_map = lambda i, pr_: (pr_[0] * nb + i, pr_[1])
        rs_map = lambda i, pr_: (i, pr_[1])
    else:
        g_map = lambda i, pr_: ((2 * pr_[1] + pr_[0]) * nb + i, 0)
        rs_map = lambda i, pr_: (pr_[1] * nb + i, 0)
    o_map = (lambda i, pr_: ((2 * pr_[1] + pr_[0]) * nb + i, 0)) if small else (lambda i, pr_: (pr_[0] * nb + i, 0))
    return pl.pallas_call(
        body, name="grad_total_sum_%d" % t,
        out_shape=jax.ShapeDtypeStruct(((8 if small else 2) * pr, pc), F32),
        grid_spec=pltpu.PrefetchScalarGridSpec(
            num_scalar_prefetch=1, grid=(nb,),
            in_specs=[pl.BlockSpec((tr, pc), g_map), pl.BlockSpec((tr, pc), rs_map),
                      pl.BlockSpec((3, tr, pc), lambda i, pr_: (0, i, 0))],
            out_specs=pl.BlockSpec((tr, pc), o_map)),
        compiler_params=_params(("parallel",)),
    )(place, g, rs, rc)


def _grad_final_exchange(totals):
    n = len(_RED)
    nb = n - 1

    def body(*refs):
        srcs, dsts, (send_sems, recv_sems) = refs[:n], refs[n:2 * n], refs[2 * n:]
        x, y, c, chips = _place()
        me = 4 * x + 2 * y + c
        others = [(x, y, 1 - c)] + [(px, py, cc) for (px, py) in chips for cc in (c, 1 - c)]

        def half(ref, t, cc):
            pr = _RED[t][2][0]
            return ref.at[pl.ds(pl.multiple_of(cc * pr, pr), pr), :]

        def eighth(ref, dev):
            return ref.at[pl.ds(pl.multiple_of(dev * SMALL_ROWS, SMALL_ROWS), SMALL_ROWS), :]

        def big_copy(t, cc):
            return pltpu.make_async_remote_copy(src_ref=half(srcs[t], t, cc), dst_ref=half(dsts[t], t, cc), send_sem=send_sems.at[t],
                                                recv_sem=recv_sems.at[t], device_id=others[0], device_id_type=_MESH)

        def small_copy(i, dev):
            return pltpu.make_async_remote_copy(src_ref=eighth(srcs[nb], dev), dst_ref=eighth(dsts[nb], dev),
                                                send_sem=send_sems.at[nb + i], recv_sem=recv_sems.at[nb + i],
                                                device_id=others[i], device_id_type=_MESH)

        sends = [big_copy(t, c) for t in range(nb)] + [small_copy(i, me) for i in range(7)]
        for cp in sends:
            cp.start()
        for t in range(nb):
            big_copy(t, 1 - c).wait_recv()
        for i, (px, py, pc) in enumerate(others):
            small_copy(i, 4 * px + 2 * py + pc).wait_recv()
        for cp in sends:
            cp.wait_send()

    return pl.pallas_call(
        body, name="grad_final_exchange",
        out_shape=[jax.ShapeDtypeStruct(a.shape, F32) for a in totals],
        in_specs=[_ANY] * n, out_specs=[_ANY] * n,
        input_output_aliases={t: t for t in range(n)},
        scratch_shapes=[pltpu.SemaphoreType.DMA((nb + 7,)), pltpu.SemaphoreType.DMA((nb + 7,))],
    )(*totals)


def _grad_place():
    return jnp.stack([lax.axis_index("c"), 2 * lax.axis_index("x") + lax.axis_index("y")]).astype(jnp.int32)


def _reduce_begin(ts, grads, place, tag, from_sibling=None):
    if from_sibling is None:
        from_sibling = _grad_sibling_exchange(ts, grads, "grad_sibling_exchange_" + tag)
    return from_sibling, [_chip_sum(t, g, r, place) for t, g, r in zip(ts, grads, from_sibling)]


def _reduce_end(ts, grads, from_sibling, from_chips, place):
    return [_total_sum(t, g, r, q, place) for t, g, r, q in zip(ts, grads, from_sibling, from_chips)]


_EARLY = (1, 2, 3, 4)
_W_IN = (0,)
_SMALL_RED = (5,)


def _adamw_math(w, g, m, v):
    m = ADAM_B1 * m + (1.0 - ADAM_B1) * g
    v = ADAM_B2 * v + (1.0 - ADAM_B2) * (g * g)
    m_hat = m / (1.0 - ADAM_B1 ** ADAM_STEP)
    v_hat = v / (1.0 - ADAM_B2 ** ADAM_STEP)
    return -ADAM_LR * (m_hat / (jnp.sqrt(v_hat) + ADAM_EPS) + ADAM_WD * w), m, v


def _adamw(w, g, m, v, name):
    W = w.shape[1]
    return _rowwise(_adamw_math, [(a, 0, W) for a in (w, g, m, v)], [], [(W, F32)] * 3, tr=128, name=name)


def _adamw_whole(ws, gs, ms, vs, name):
    n = len(ws)

    def body(*refs):
        ins, outs = refs[:4 * n], refs[4 * n:]
        for i in range(n):
            res = _adamw_math(*[ins[j * n + i][...] for j in range(4)])
            for j in range(3):
                outs[j * n + i][...] = res[j]

    res = pl.pallas_call(
        body, name=name,
        out_shape=[jax.ShapeDtypeStruct(a.shape, F32) for a in ws] * 3,
        compiler_params=pltpu.CompilerParams(vmem_limit_bytes=VMEM_LIMIT_V7X),
    )(*ws, *gs, *ms, *vs)
    return res[:n], res[n:2 * n], res[2 * n:]


def _chunks(arr, off, width, w=512):
    return [(arr, off + i * w, w) for i in range(width // w)]


def _cat(vs):
    return jnp.concatenate(vs, axis=1)


def _forward_backward(x, p_b, tgt, shards, small):
    L = x.shape[0]
    row = lambda v: v.reshape(1, -1)
    g_mix, g_ple, g_fin = row(small["norm_mix"]), row(small["norm_ple"]), row(small["norm_final"])
    gq, gk, b_glu = row(small["q_norm"]), row(small["k_norm"]), row(small["b_glu"])
    cos, sin = _rope_tables(L)

    hn_b, = _rowwise(lambda x, g: x * _rms(x) * g, [(x, 0, D_MODEL)], [g_mix], [(D_MODEL, BF16)], name="norm_mix")
    z, w_in = _matmul_in_gather(hn_b, shards[0])
    qr, kr, vb = _attn_prep(z, gq, gk, cos, sin)
    o, lse, w_glu, w_out, w_pg, w_pp = _attn_fwd(qr, kr, vb, [1, 2, 3, 4], shards[1:])

    ssm_names = ("ssm_a_re", "ssm_a_im", "ssm_log_dt", "ssm_b_re", "ssm_b_im")
    (lre, lim, bre, bim), disc_vjp = jax.vjp(_ssm_disc, *[small[n][0] for n in ssm_names])
    ssm = (_to_pairs(jnp.swapaxes(jnp.stack([bre, bim], axis=2), -1, -2)),
           _to_pairs(jnp.stack([small["ssm_c_re"][0], -small["ssm_c_im"][0]], axis=2)),
           _to_lam(lre), _to_lam(lim), row(small["ssm_d"]))
    u_p = _seg_perm(z[:, Z_U:Z_U + D_SSM])
    y_p, ge_p, ssm_entry = _ssm_fwd(u_p, *ssm)
    ge_b = _seg_unperm(ge_p)
    glu = _matmul(ge_b, w_glu, name="mm_glu")

    def merge(ga0, ga1, a, b, gs0, gs1, o, bias):
        sa, _ = _silu_and_grad(_cat([ga0, ga1]))
        ss, _ = _silu_and_grad(_cat([gs0, gs1]))
        y2 = (a + bias[:, :D_SSM]) * _sig(b + bias[:, D_SSM:])
        return _cat([o * sa, y2 * ss])
    merge_rows = _chunks(z, Z_GA, D_ATTN) + [(glu, 0, D_SSM), (glu, D_SSM, D_SSM)] + _chunks(z, Z_GS, D_SSM) + [(o, 0, D_ATTN)]
    cat_b, = _rowwise(merge, merge_rows, [b_glu], [(D_MODEL, BF16)], name="merge")
    t_out = _matmul(cat_b, w_out, name="mm_out")

    def resid(x, t, g):
        h1 = x + t
        return h1, h1 * _rms(h1) * g
    h1, hp_b = _rowwise(resid, [(x, 0, D_MODEL), (t_out, 0, D_MODEL)], [g_ple], [(D_MODEL, F32), (D_MODEL, BF16)], name="resid_norm")
    gl = _matmul(hp_b, w_pg, name="mm_ple_gate")
    pp = _matmul(p_b, w_pp, name="mm_ple_proj")

    def head(h1, gl, pp, tgt, g):
        gate = _sig(gl)
        h2 = h1 + gate * pp
        r = _rms(h2)
        n = h2 * r
        err = n * g - tgt
        dy = err * (1.0 / D_MODEL)
        dn = dy * g
        dh2 = r * (dn - n * jnp.mean(dn * n, axis=-1, keepdims=True))
        dgate = dh2 * pp
        return dh2, dh2 * gate, dgate * gate * (1.0 - gate), _colsum(dy * n), _colsum(0.5 * err * err * (1.0 / D_MODEL))
    dh2, dpp_b, dgl_b, dg_fin, loss_cols = _rowwise(
        head, [(a, 0, D_MODEL) for a in (h1, gl, pp, tgt)], [g_fin],
        [(D_MODEL, F32), (D_MODEL, BF16), (D_MODEL, BF16)], [(1, D_MODEL), (1, D_MODEL)], name="loss_head")

    dw_pp = _matmul(p_b, dpp_b, ta=True, name="mm_d_w_ple_proj")
    dw_pg = _matmul(hp_b, dgl_b, ta=True, name="mm_d_w_ple_gate")
    dhp = _matmul(dgl_b, w_pg, tb=True, name="mm_d_hp")

    def resid_bwd(dhp, h1, dh2, g):
        dx, dg = _rms_bwd(h1, g, dhp)
        dh1 = dh2 + dx
        return dh1, dh1, _colsum(dg)
    dh1, dh1_b, dg_ple = _rowwise(resid_bwd, [(a, 0, D_MODEL) for a in (dhp, h1, dh2)], [g_ple],
                                  [(D_MODEL, F32), (D_MODEL, BF16)], [(1, D_MODEL)], name="resid_norm_bwd")
    dw_out = _matmul(cat_b, dh1_b, ta=True, name="mm_d_w_out")
    dcat = _matmul(dh1_b, w_out, tb=True, name="mm_d_cat")

    def merge_bwd(dya, dys, ga0, ga1, a, b, gs0, gs1, o, bias):
        ga, gs = _cat([ga0, ga1]), _cat([gs0, gs1])
        sa, dsa = _silu_and_grad(ga)
        ss, dss = _silu_and_grad(gs)
        a, sb = a + bias[:, :D_SSM], _sig(b + bias[:, D_SSM:])
        dy2 = dys * ss
        dglu = _cat([dy2 * sb, dy2 * a * sb * (1.0 - sb)])
        do = dya * sa
        lane = lax.broadcasted_iota(jnp.int32, (do.shape[0], HEAD_DIM), 1)
        delta = sum(jnp.where(lane == i, jnp.sum(h, axis=1, keepdims=True), 0.0)
                    for i, h in enumerate(_heads(do * o)))
        return do, dya * o * dsa, dys * (a * sb) * dss, dglu, delta, _colsum(dglu)
    do_b, dga_b, dgs_b, dglu_b, delta, db_glu = _rowwise(
        merge_bwd, [(dcat, 0, D_ATTN), (dcat, D_ATTN, D_SSM)] + merge_rows, [b_glu],
        [(D_ATTN, BF16), (D_ATTN, BF16), (D_SSM, BF16), (2 * D_SSM, BF16), (HEAD_DIM, F32)], [(1, 2 * D_SSM)], name="merge_bwd")
    dw_glu = _matmul(ge_b, dglu_b, ta=True, name="mm_d_w_glu")
    dge = _matmul(dglu_b, w_glu, tb=True, name="mm_d_ge")
    place = _grad_place()
    early_grads = [dw_glu, dw_out, dw_pg, dw_pp]
    dy_p, = _rowwise(lambda dge, y: dge * _gelu_grad(y), [(_seg_perm(dge), 0, D_SSM), (y_p, 0, D_SSM)], [], [(D_SSM, F32)],
                     name="gelu_bwd")
    du_p, dwb, dwc, dare, daim, d_ssm_d, *early_sib = _ssm_bwd(u_p, dy_p, ssm_entry, *ssm, _EARLY, early_grads)
    dcc, dbb = _from_pairs(dwc), jnp.swapaxes(_from_pairs(dwb), -1, -2)
    dc_re, dc_im = dcc[:, :, 0], -dcc[:, :, 1]
    da_re, da_im, dlog_dt, db_re, db_im = disc_vjp((_from_lam(dare), _from_lam(daim), dbb[:, :, 0], dbb[:, :, 1]))

    early_sib, early_sums = _reduce_begin(_EARLY, early_grads, place, "early", from_sibling=early_sib)
    dqr, dkr, dv, *early_chips = _attn_bwd(qr, kr, kr.T, vb, do_b, lse.reshape(N_HEADS, 1, L), delta[:, :N_HEADS].T.reshape(N_HEADS, 1, L),
                                           _EARLY, early_sums)
    early_totals = _reduce_end(_EARLY, early_grads, early_sib, early_chips, place)
    dq_b, dk_b, dgq, dgk = _attn_prep_bwd(dqr, dkr, z, gq, gk, cos, sin)
    dz_b = _cat([dq_b, dk_b, dv.astype(BF16), dga_b, _seg_unperm(du_p).astype(BF16), dgs_b])
    dw_in = _matmul(hn_b, dz_b, ta=True, name="mm_d_w_in")
    w_in_sib, w_in_sums = _reduce_begin(_W_IN, [dw_in], place, "w_in")
    dhn, *w_in_chips = _matmul(dz_b, w_in, tb=True, name="mm_d_hn", exchange=(_W_IN, w_in_sums))
    w_in_total, = _reduce_end(_W_IN, [dw_in], w_in_sib, w_in_chips, place)

    def norm_bwd(dhn, x, dh1, g):
        dx, dg = _rms_bwd(x, g, dhn)
        return dh1 + dx, _colsum(dg)
    grad_x, dg_mix = _rowwise(norm_bwd, [(a, 0, D_MODEL) for a in (dhn, x, dh1)], [g_mix], [(D_MODEL, F32)], [(1, D_MODEL)],
                              name="norm_mix_bwd")

    small_grads = {"norm_mix": dg_mix, "q_norm": dgq, "k_norm": dgk, "ssm_a_re": da_re, "ssm_a_im": da_im, "ssm_log_dt": dlog_dt,
                   "ssm_b_re": db_re, "ssm_b_im": db_im, "ssm_c_re": dc_re, "ssm_c_im": dc_im, "ssm_d": d_ssm_d,
                   "b_glu": db_glu, "norm_ple": dg_ple, "norm_final": dg_fin}
    return jnp.sum(loss_cols), grad_x, [w_in_total] + early_totals, small_grads, place


_SMALL = ("norm_mix", "q_norm", "k_norm", "ssm_a_re", "ssm_a_im", "ssm_log_dt", "ssm_b_re", "ssm_b_im", "ssm_c_re", "ssm_c_im",
          "ssm_d", "b_glu", "norm_ple", "norm_final")
_WEIGHTS = ("norm_mix", "w_in", "q_norm", "k_norm", "ssm_a_re", "ssm_a_im", "ssm_log_dt", "ssm_b_re", "ssm_b_im", "ssm_c_re",
            "ssm_c_im", "ssm_d", "w_glu", "b_glu", "w_out", "norm_ple", "w_ple_gate", "w_ple_proj", "norm_final")


_SMALL_ADAMW_GROUPS = (("ssm_b_re",), ("ssm_b_im",), ("ssm_c_re", "ssm_c_im"),
                       ("norm_mix", "q_norm", "k_norm", "ssm_a_re", "ssm_a_im", "ssm_log_dt", "ssm_d", "b_glu", "norm_ple",
                        "norm_final"))


def _flat_small(d):
    flat = jnp.concatenate([d[n].reshape(-1).astype(F32) for n in _SMALL])
    return jnp.pad(flat, (0, N_SMALL - flat.shape[0]))


def _split_small(flat, like):
    out, off = {}, 0
    for n in _SMALL:
        sz = math.prod(like[n].shape)
        out[n] = flat[off:off + sz].reshape(like[n].shape)
        off += sz
    return out


def kernel(x, p, norm_mix, w_in, q_norm, k_norm, ssm_a_re, ssm_a_im, ssm_log_dt, ssm_b_re, ssm_b_im, ssm_c_re, ssm_c_im, ssm_d, w_glu, b_glu, w_out, norm_ple, w_ple_gate, w_ple_proj, norm_final, loss_target, m_norm_mix, m_w_in, m_q_norm, m_k_norm, m_ssm_a_re, m_ssm_a_im, m_ssm_log_dt, m_ssm_b_re, m_ssm_b_im, m_ssm_c_re, m_ssm_c_im, m_ssm_d, m_w_glu, m_b_glu, m_w_out, m_norm_ple, m_w_ple_gate, m_w_ple_proj, m_norm_final, v_norm_mix, v_w_in, v_q_norm, v_k_norm, v_ssm_a_re, v_ssm_a_im, v_ssm_log_dt, v_ssm_b_re, v_ssm_b_im, v_ssm_c_re, v_ssm_c_im, v_ssm_d, v_w_glu, v_b_glu, v_w_out, v_norm_ple, v_w_ple_gate, v_w_ple_proj, v_norm_final):
    w = dict(norm_mix=norm_mix, w_in=w_in, q_norm=q_norm, k_norm=k_norm, ssm_a_re=ssm_a_re, ssm_a_im=ssm_a_im,
             ssm_log_dt=ssm_log_dt, ssm_b_re=ssm_b_re, ssm_b_im=ssm_b_im, ssm_c_re=ssm_c_re, ssm_c_im=ssm_c_im, ssm_d=ssm_d,
             w_glu=w_glu, b_glu=b_glu, w_out=w_out, norm_ple=norm_ple, w_ple_gate=w_ple_gate, w_ple_proj=w_ple_proj,
             norm_final=norm_final)
    m = dict(norm_mix=m_norm_mix, w_in=m_w_in, q_norm=m_q_norm, k_norm=m_k_norm, ssm_a_re=m_ssm_a_re, ssm_a_im=m_ssm_a_im,
             ssm_log_dt=m_ssm_log_dt, ssm_b_re=m_ssm_b_re, ssm_b_im=m_ssm_b_im, ssm_c_re=m_ssm_c_re, ssm_c_im=m_ssm_c_im,
             ssm_d=m_ssm_d, w_glu=m_w_glu, b_glu=m_b_glu, w_out=m_w_out, norm_ple=m_norm_ple, w_ple_gate=m_w_ple_gate,
             w_ple_proj=m_w_ple_proj, norm_final=m_norm_final)
    v = dict(norm_mix=v_norm_mix, w_in=v_w_in, q_norm=v_q_norm, k_norm=v_k_norm, ssm_a_re=v_ssm_a_re, ssm_a_im=v_ssm_a_im,
             ssm_log_dt=v_ssm_log_dt, ssm_b_re=v_ssm_b_re, ssm_b_im=v_ssm_b_im, ssm_c_re=v_ssm_c_re, ssm_c_im=v_ssm_c_im,
             ssm_d=v_ssm_d, w_glu=v_w_glu, b_glu=v_b_glu, w_out=v_w_out, norm_ple=v_norm_ple, w_ple_gate=v_w_ple_gate,
             w_ple_proj=v_w_ple_proj, norm_final=v_norm_final)
    big_names = [n for n, _, _, _ in _BIG]

    small = {n: w[n] for n in _SMALL}
    loss_part, grad_x, big_totals, small_grads, place = _forward_backward(
        x[0], p[0, 0].astype(BF16), loss_target[0], [w[n][0].astype(BF16) for n in big_names], small)
    loss = lax.psum(loss_part, ("x", "y", "c"))

    small_flat = [_flat_small(small_grads).reshape(8 * SMALL_ROWS, SMALL_W)]
    small_sib, small_sums = _reduce_begin(_SMALL_RED, small_flat, place, "small")
    small_total = _reduce_end(_SMALL_RED, small_flat, small_sib, _grad_chip_exchange(_SMALL_RED, small_sums), place)
    *big_red, small_red = _grad_final_exchange(big_totals + small_total)
    grads = _split_small(small_red.reshape(-1), w)
    delta, new_m, new_v = {}, {}, {}
    for n, g in zip(big_names, big_red):
        grads[n] = g[None]
        d_, m_, v_ = _adamw(w[n][0], g, m[n][0], v[n][0], "adamw_" + n)
        delta[n], new_m[n], new_v[n] = d_[None], m_[None], v_[None]
    at_least_2d = lambda a: a.reshape(1, -1) if a.ndim == 1 else a
    for i, names in enumerate(_SMALL_ADAMW_GROUPS):
        d_, m_, v_ = _adamw_whole(*[[at_least_2d(src[n]) for n in names] for src in (w, grads, m, v)], "adamw_small_%d" % i)
        for j, n in enumerate(names):
            delta[n], new_m[n], new_v[n] = (a[j].reshape(w[n].shape) for a in (d_, m_, v_))
    return (loss, grad_x[None], *[grads[n] for n in _WEIGHTS], *[delta[n] for n in _WEIGHTS],
            *[new_m[n] for n in _WEIGHTS], *[new_v[n] for n in _WEIGHTS])
```

```python
import functools
import math

import jax
import jax.numpy as jnp
import numpy as np
from jax import lax
from jax.experimental import pallas as pl
from jax.experimental.pallas import tpu as pltpu

D_MODEL = 2048
GRID_W = 64
PLE_DIM = 256
D_ATTN = 1024
N_HEADS = 8
N_KV = 2
HEAD_DIM = 128
ROPE_THETA = 10000.0
D_SSM = 1024
SSM_H = 16
SSM_G = 64
SSM_P = 64
D_KV = N_KV * HEAD_DIM
D_IN = 2 * D_ATTN + 2 * D_KV + 2 * D_SSM
EPS = 1e-6
Z_Q, Z_K, Z_V, Z_GA, Z_U, Z_GS = 0, 1024, 1280, 1536, 2560, 3584

ADAM_LR, ADAM_B1, ADAM_B2, ADAM_EPS, ADAM_WD, ADAM_STEP = 0.001, 0.9, 0.999, 1e-08, 0.01, 10

N_CHIPS = 4
VMEM_LIMIT_V7X = 56 * 1024 * 1024
F32 = jnp.float32
BF16 = jnp.bfloat16


def _params(sem, vmem=VMEM_LIMIT_V7X):
    return pltpu.CompilerParams(dimension_semantics=sem, vmem_limit_bytes=vmem)


def _matmul(a, b, *, ta=False, tb=False, out_dtype=F32, tm=1024, tn=None, name, exchange=None):
    M, K = (a.shape[1], a.shape[0]) if ta else a.shape
    N = b.shape[0] if tb else b.shape[1]
    if tn is None:
        tn = 1024 if (N % 1024 == 0 and K <= 4096) else 512
    tm, tn = min(tm, M), min(tn, N)
    assert M % tm == 0 and N % tn == 0, (name, M, N, K)
    dims = (((0 if ta else 1,), (1 if tb else 0,)), ((), ()))
    ex_ts, ex_sums = exchange if exchange else ((), ())
    n_ex = len(ex_ts)
    gm, gn = M // tm, N // tn

    def body(a_ref, b_ref, *rest):
        o_ref = rest[n_ex]
        if n_ex:
            ex = _ChipExchange(ex_ts, rest[:n_ex], rest[n_ex + 1:2 * n_ex + 1], rest[2 * n_ex + 1:])
            step = pl.program_id(0) * gn + pl.program_id(1)
            pl.when(step == 0)(ex.start)
        o_ref[...] = lax.dot_general(a_ref[...], b_ref[...], dims, preferred_element_type=F32).astype(o_ref.dtype)
        if n_ex:
            pl.when(step == gm * gn - 1)(ex.finish)

    a_spec = pl.BlockSpec((K, tm), lambda i, j: (0, i)) if ta else pl.BlockSpec((tm, K), lambda i, j: (i, 0))
    b_spec = pl.BlockSpec((tn, K), lambda i, j: (j, 0)) if tb else pl.BlockSpec((K, tn), lambda i, j: (0, j))
    res = pl.pallas_call(
        body, name=name,
        out_shape=[jax.ShapeDtypeStruct((M, N), out_dtype)] + (_ChipExchange.out_shape(ex_ts) if n_ex else []),
        grid=(gm, gn),
        in_specs=[a_spec, b_spec] + [_ANY] * n_ex,
        out_specs=[pl.BlockSpec((tm, tn), lambda i, j: (i, j))] + [_ANY] * n_ex,
        scratch_shapes=_ChipExchange.scratch(ex_ts) if n_ex else [],
        compiler_params=_params(("arbitrary", "arbitrary") if n_ex else ("parallel", "parallel")),
    )(a, b, *ex_sums)
    return res if n_ex else res[0]


def _rowwise(fn, rows, consts, outs, accs=(), *, tr=256, name):
    L = rows[0][0].shape[0]
    tr = math.gcd(tr, L)
    assert tr % 8 == 0 or tr == L, (name, L, tr)
    n_in, n_c, n_o, n_a = len(rows), len(consts), len(outs), len(accs)

    def body(*refs):
        ins = [r[...] for r in refs[:n_in + n_c]]
        res = fn(*ins)
        if not isinstance(res, (tuple, list)):
            res = (res,)
        o_refs = refs[n_in + n_c:n_in + n_c + n_o]
        a_refs = refs[n_in + n_c + n_o:]
        for r, v in zip(o_refs, res[:n_o]):
            r[...] = v.astype(r.dtype)
        if n_a:
            first = pl.program_id(0) == 0

            @pl.when(first)
            def _():
                for r, v in zip(a_refs, res[n_o:]):
                    r[...] = v.astype(F32)

            @pl.when(jnp.logical_not(first))
            def _():
                for r, v in zip(a_refs, res[n_o:]):
                    r[...] += v.astype(F32)

    in_specs = []
    for arr, off, w in rows:
        assert off % w == 0, (name, off, w)
        in_specs.append(pl.BlockSpec((tr, w), functools.partial(lambda i, c: (i, c), c=off // w)))
    for c in consts:
        in_specs.append(pl.BlockSpec(c.shape, lambda i: (0, 0)))
    out_shape = [jax.ShapeDtypeStruct((L, w), dt) for w, dt in outs] + [jax.ShapeDtypeStruct(s, F32) for s in accs]
    out_specs = [pl.BlockSpec((tr, w), lambda i: (i, 0)) for w, _ in outs] + [pl.BlockSpec(s, lambda i: (0, 0)) for s in accs]
    res = pl.pallas_call(
        body, name=name,
        out_shape=out_shape,
        grid=(L // tr,),
        in_specs=in_specs,
        out_specs=out_specs,
        compiler_params=_params(("arbitrary",) if n_a else ("parallel",)),
    )(*[r[0] for r in rows], *consts)
    return res


def _sig(x):
    return jax.nn.sigmoid(x)


def _silu_and_grad(x):
    s = _sig(x)
    return x * s, s * (1.0 + x * (1.0 - s))


_GELU_C = math.sqrt(2.0 / math.pi)


def _gelu(x):
    return 0.5 * x * (1.0 + jnp.tanh(_GELU_C * (x + 0.044715 * x * x * x)))


def _gelu_grad(x):
    t = jnp.tanh(_GELU_C * (x + 0.044715 * x * x * x))
    return 0.5 * (1.0 + t) + 0.5 * x * (1.0 - t * t) * _GELU_C * (1.0 + 3.0 * 0.044715 * x * x)


def _rms(x):
    return lax.rsqrt(jnp.mean(x * x, axis=-1, keepdims=True) + EPS)


def _rms_bwd(x, g, dy):
    r = _rms(x)
    n = x * r
    dn = dy * g
    return r * (dn - n * jnp.mean(dn * n, axis=-1, keepdims=True)), dy * n


def _colsum(v):
    return jnp.sum(v, axis=0, keepdims=True)


def _rope_partner(x):
    lane = lax.broadcasted_iota(jnp.int32, x.shape, x.ndim - 1)
    return jnp.where(lane % 64 < 32, pltpu.roll(x, 96, x.ndim - 1), pltpu.roll(x, 32, x.ndim - 1))


def _rope_tables(L):
    rows_n = L // GRID_W
    rows = np.repeat(np.arange(rows_n), GRID_W).astype(np.float32)
    cols = np.tile(np.arange(GRID_W), rows_n).astype(np.float32)
    n_freq = HEAD_DIM // 4
    inv_freq = np.float32(ROPE_THETA) ** (-np.arange(n_freq, dtype=np.float32) / np.float32(n_freq))
    ar, ac = rows[:, None] * inv_freq[None, :], cols[:, None] * inv_freq[None, :]
    cos = np.concatenate([np.cos(ar), np.cos(ar), np.cos(ac), np.cos(ac)], axis=-1).astype(np.float32)
    sin = np.concatenate([-np.sin(ar), np.sin(ar), -np.sin(ac), np.sin(ac)], axis=-1).astype(np.float32)
    return jnp.asarray(cos), jnp.asarray(sin)


def _heads(v):
    return [v[:, h * HEAD_DIM:(h + 1) * HEAD_DIM] for h in range(v.shape[1] // HEAD_DIM)]


def _attn_prep(z, q_norm, k_norm, cos, sin):
    def fn(q, k, v, cos, sin, gq, gk):
        def one(xh, g):
            xn = xh * _rms(xh) * g
            return xn * cos + _rope_partner(xn) * sin
        qr = jnp.concatenate([one(h, gq) for h in _heads(q)], axis=1)
        kr = jnp.concatenate([one(h, gk) for h in _heads(k)], axis=1)
        return qr, kr, v
    return _rowwise(fn, [(z, Z_Q, D_ATTN), (z, Z_K, D_KV), (z, Z_V, D_KV), (cos, 0, HEAD_DIM), (sin, 0, HEAD_DIM)],
                    [q_norm, k_norm], [(D_ATTN, BF16), (D_KV, BF16), (D_KV, BF16)], name="attn_prep")


def _attn_prep_bwd(dqr, dkr, z, q_norm, k_norm, cos, sin):
    def fn(dqr, dkr, q, k, cos, sin, gq, gk):
        def one(dyh, xh, g):
            dn = dyh * cos + _rope_partner(dyh * sin)
            return _rms_bwd(xh, g, dn)
        rq = [one(a, b, gq) for a, b in zip(_heads(dqr), _heads(q))]
        rk = [one(a, b, gk) for a, b in zip(_heads(dkr), _heads(k))]
        dq = jnp.concatenate([r[0] for r in rq], axis=1)
        dk = jnp.concatenate([r[0] for r in rk], axis=1)
        return dq, dk, _colsum(sum(r[1] for r in rq)), _colsum(sum(r[1] for r in rk))
    return _rowwise(fn, [(dqr, 0, D_ATTN), (dkr, 0, D_KV), (z, Z_Q, D_ATTN), (z, Z_K, D_KV), (cos, 0, HEAD_DIM), (sin, 0, HEAD_DIM)],
                    [q_norm, k_norm], [(D_ATTN, BF16), (D_KV, BF16)], [(1, HEAD_DIM), (1, HEAD_DIM)], name="attn_prep_bwd")


_QK_T = (((1,), (1,)), ((), ()))
_TA = (((0,), (0,)), ((), ()))
_REP = N_HEADS // N_KV


_EXP2_SCALE = HEAD_DIM ** -0.5 * math.log2(math.e)
ATTN_FWD_KEY_CHUNKS = 4
ATTN_BWD_KEY_CHUNKS = 8


def _attn_fwd(qr, kr, vb, g_ts, g_shards, *, tq=1024):
    L = qr.shape[0]
    tq = min(tq, L)
    kc = L // ATTN_FWD_KEY_CHUNKS
    n_g = len(g_ts)
    grid = (N_HEADS, L // tq)
    steps = grid[0] * grid[1]

    def body(q_ref, k_ref, v_ref, *rest):
        o_ref, lse_ref = rest[n_g:n_g + 2]
        g = _Gather(g_ts, rest[:n_g], rest[n_g + 2:2 * n_g + 2], rest[2 * n_g + 2:3 * n_g + 2], rest[3 * n_g + 2:])
        step = pl.program_id(0) * grid[1] + pl.program_id(1)
        pl.when(step == 0)(g.start)
        pl.when(step == (3 * steps) // 4)(g.forward)
        q = q_ref[...]
        m = jnp.full((tq, 1), -jnp.inf, F32)
        l = jnp.zeros((tq, 1), F32)
        o = jnp.zeros((tq, HEAD_DIM), F32)
        for c in range(ATTN_FWD_KEY_CHUNKS):
            ks = slice(c * kc, (c + 1) * kc)
            s = lax.dot_general(q, k_ref[ks, :], _QK_T, preferred_element_type=F32)
            m_new = jnp.maximum(m, jnp.max(s, axis=1, keepdims=True))
            a = jnp.exp2((m - m_new) * _EXP2_SCALE)
            p = jnp.exp2((s - m_new) * _EXP2_SCALE)
            l = a * l + jnp.sum(p, axis=1, keepdims=True)
            o = a * o + jnp.dot(p.astype(BF16), v_ref[ks, :], preferred_element_type=F32)
            m = m_new
        o_ref[...] = o * (1.0 / l)
        lse_ref[...] = m * _EXP2_SCALE + jnp.log2(l)
        pl.when(step == steps - 1)(g.finish)

    kv = pl.BlockSpec((L, HEAD_DIM), lambda h, i: (0, h // _REP))
    return pl.pallas_call(
        body, name="attn_fwd",
        out_shape=[jax.ShapeDtypeStruct((L, D_ATTN), F32), jax.ShapeDtypeStruct((N_HEADS, L, 1), F32)] +
                  [jax.ShapeDtypeStruct(_BIG[t][1], BF16) for t in g_ts],
        grid=grid,
        in_specs=[pl.BlockSpec((tq, HEAD_DIM), lambda h, i: (i, h)), kv, kv] + [_ANY] * n_g,
        out_specs=[pl.BlockSpec((tq, HEAD_DIM), lambda h, i: (i, h)),
                   pl.BlockSpec((None, tq, 1), lambda h, i: (h, i, 0))] + [_ANY] * n_g,
        scratch_shapes=_Gather.scratch(g_shards),
        compiler_params=_params(("arbitrary", "arbitrary")),
    )(qr, kr, vb, *g_shards)


def _attn_bwd(qr, kr, k_t, vb, do, lse, delta, ex_ts, ex_sums, *, tq=1024):
    L = qr.shape[0]
    tq = min(tq, L)
    scale = HEAD_DIM ** -0.5
    kc = L // ATTN_BWD_KEY_CHUNKS
    n_ex = len(ex_ts)
    grid = (N_KV, _REP, L // tq)

    def body(q_ref, k_ref, kt_ref, v_ref, do_ref, lse_ref, delta_ref, *rest):
        dq_ref, dk_ref, dv_ref = rest[n_ex:n_ex + 3]
        ex = _ChipExchange(ex_ts, rest[:n_ex], rest[n_ex + 3:2 * n_ex + 3], rest[2 * n_ex + 3:])
        step = (pl.program_id(0) * grid[1] + pl.program_id(1)) * grid[2] + pl.program_id(2)
        pl.when(step == 0)(ex.start)

        @pl.when((pl.program_id(1) == 0) & (pl.program_id(2) == 0))
        def _():
            dk_ref[...] = jnp.zeros_like(dk_ref)
            dv_ref[...] = jnp.zeros_like(dv_ref)

        q, do, lse, delta = q_ref[...], do_ref[...], lse_ref[...], delta_ref[...]
        dq_t = 0.0
        for c in range(ATTN_BWD_KEY_CHUNKS):
            ks = slice(c * kc, (c + 1) * kc)
            st = lax.dot_general(k_ref[ks, :], q, _QK_T, preferred_element_type=F32)
            p = jnp.exp2(st * _EXP2_SCALE - lse)
            dv_ref[ks, :] += jnp.dot(p.astype(BF16), do, preferred_element_type=F32)
            dp = lax.dot_general(v_ref[ks, :], do, _QK_T, preferred_element_type=F32)
            ds = (p * (dp - delta) * scale).astype(BF16)
            dk_ref[ks, :] += jnp.dot(ds, q, preferred_element_type=F32)
            dq_t = dq_t + jnp.dot(kt_ref[:, ks], ds, preferred_element_type=F32)
        dq_ref[...] = dq_t.T
        pl.when(step == grid[0] * grid[1] * grid[2] - 1)(ex.finish)

    head = lambda g, r, i: (i, g * _REP + r)
    kv = pl.BlockSpec((L, HEAD_DIM), lambda g, r, i: (0, g))
    per_query = pl.BlockSpec((None, 1, tq), lambda g, r, i: (g * _REP + r, 0, i))
    return pl.pallas_call(
        body, name="attn_bwd",
        out_shape=[jax.ShapeDtypeStruct((L, D_ATTN), F32), jax.ShapeDtypeStruct((L, D_KV), F32), jax.ShapeDtypeStruct((L, D_KV), F32)] +
                  _ChipExchange.out_shape(ex_ts),
        grid=grid,
        in_specs=[pl.BlockSpec((tq, HEAD_DIM), head), kv,
                  pl.BlockSpec((HEAD_DIM, L), lambda g, r, i: (g, 0)), kv,
                  pl.BlockSpec((tq, HEAD_DIM), head), per_query, per_query] + [_ANY] * n_ex,
        out_specs=[pl.BlockSpec((tq, HEAD_DIM), head), kv, kv] + [_ANY] * n_ex,
        scratch_shapes=_ChipExchange.scratch(ex_ts),
        compiler_params=_params(("arbitrary", "arbitrary", "arbitrary")),
    )(qr, kr, k_t, vb, do, lse, delta, *ex_sums)


SSM_BLK = 8
SSM_NB = SSM_G // SSM_BLK
SSM_SEG = 8
SSM_UNROLL = 4


def _unrolled_loop(n, step, carry):
    u = SSM_UNROLL

    def trip(i, c):
        for j in range(u):
            c = step(i * u + j, c)
        return c
    carry = lax.fori_loop(0, n // u, trip, carry)
    for t in range(n - n % u, n):
        carry = step(jnp.int32(t), carry)
    return carry


def _cplx_pow2(a, b, n):
    for _ in range(int(math.log2(n))):
        a, b = a * a - b * b, 2.0 * a * b
    return a, b


def _seg_scan(ref, a, b, T, reverse, entry=None, tap=None):
    npair = len(a)
    zero = jnp.zeros((SSM_SEG, 128), F32)

    def make_step(store, tap_fn=None):
        def step(t, carry):
            lt = (T - 1 - t) if reverse else t
            row = pl.multiple_of(lt * SSM_SEG, SSM_SEG)
            blk = ref[pl.ds(row, SSM_SEG), :]
            new = []
            for q in range(npair):
                re, im = carry[2 * q], carry[2 * q + 1]
                nre = a[q] * re - b[q] * im + blk[:, q * 256:q * 256 + 128]
                nim = a[q] * im + b[q] * re + blk[:, q * 256 + 128:q * 256 + 256]
                new += [nre, nim]
            if store:
                ref[pl.ds(row, SSM_SEG), :] = jnp.concatenate(new, axis=1)
            extra = carry[2 * npair:]
            return tuple(new) + (tuple(tap_fn(lt, new, extra)) if tap_fn else tuple(extra))
        return step

    def second_pass(init):
        if tap is None:
            _unrolled_loop(T, make_step(True), tuple(init))
            return init
        carry = _unrolled_loop(T - 1, make_step(True, tap[0]), tuple(init) + tuple(tap[2]))
        carry = make_step(True, tap[1])(jnp.int32(T - 1), carry)
        return init, carry[2 * npair:]

    if entry is not None:
        return second_pass(entry)
    ends = _unrolled_loop(T, make_step(False), (zero,) * (2 * npair))
    sub = lax.broadcasted_iota(jnp.int32, (SSM_SEG, 128), 0)
    keep = (sub != SSM_SEG - 1) if reverse else (sub != 0)
    shift = (SSM_SEG - 1) if reverse else 1
    init = []
    for q in range(npair):
        pa, pb = _cplx_pow2(a[q], b[q], T)
        xr, xi = zero, zero
        for _ in range(SSM_SEG - 1):
            fr = ends[2 * q] + pa * xr - pb * xi
            fi = ends[2 * q + 1] + pa * xi + pb * xr
            xr = jnp.where(keep, pltpu.roll(fr, shift, 0), 0.0)
            xi = jnp.where(keep, pltpu.roll(fi, shift, 0), 0.0)
        init += [xr, xi]
    return second_pass(init)


SSM_BW = SSM_BLK * SSM_H
SSM_SW = SSM_BLK * 2 * SSM_P
SSM_NPAIR = SSM_BLK // 2


def _seg_perm(a):
    L, C = a.shape
    return a.reshape(SSM_SEG, L // SSM_SEG, C).transpose(1, 0, 2).reshape(L, C)


def _seg_unperm(a):
    L, C = a.shape
    return a.reshape(L // SSM_SEG, SSM_SEG, C).transpose(1, 0, 2).reshape(L, C)


def _lam_rows(are_ref, aim_ref, d):
    a = [jnp.broadcast_to(are_ref[d, j:j + 1, :], (SSM_SEG, 128)) for j in range(SSM_NPAIR)]
    b = [jnp.broadcast_to(aim_ref[d, j:j + 1, :], (SSM_SEG, 128)) for j in range(SSM_NPAIR)]
    return a, b


_PAIR_SPEC = pl.BlockSpec((None, 2, SSM_NPAIR, 2, 2 * SSM_H, 128), lambda g: (g, 0, 0, 0, 0, 0))


def _pair_window(j, r):
    return slice(j * 2 * SSM_H, (j + 1) * 2 * SSM_H), slice(j * 256 + r * 128, j * 256 + (r + 1) * 128)


def _expand_pairs(c_ref, dense_ref):
    dense_ref[...] = jnp.zeros_like(dense_ref)
    for d in range(2):
        for j in range(SSM_NPAIR):
            for r in range(2):
                rows, cols = _pair_window(j, r)
                dense_ref[d, rows, cols] = c_ref[d, j, r].astype(dense_ref.dtype)


def _ssm_fwd(u_p, wb, wc, are, aim, dvec):
    L = u_p.shape[0]
    T = L // SSM_SEG
    RC = min(512, L)

    def body(u_ref, wb_ref, wc_ref, are_ref, aim_ref, d_ref, y_ref, ge_ref, entry_ref, x_scr, wb_s, wc_s):
        _expand_pairs(wb_ref, wb_s)
        _expand_pairs(wc_ref, wc_s)
        y_ref[...] = u_ref[...] * d_ref[...]
        for d in range(2):
            def bu_chunk(c, _):
                rows = pl.ds(pl.multiple_of(c * RC, RC), RC)
                x_scr[rows, :] = jnp.dot(u_ref[rows, :].astype(BF16), wb_s[d], preferred_element_type=F32)
                return 0
            lax.fori_loop(0, L // RC, bu_chunk, 0)
            a, b = _lam_rows(are_ref, aim_ref, d)
            entry_ref[d] = jnp.concatenate(_seg_scan(x_scr, a, b, T, reverse=(d == 1)), axis=1)

            def y_chunk(c, _):
                rows = pl.ds(pl.multiple_of(c * RC, RC), RC)
                y_ref[rows, :] += lax.dot_general(x_scr[rows, :].astype(BF16), wc_s[d], _QK_T, preferred_element_type=F32)
                return 0
            lax.fori_loop(0, L // RC, y_chunk, 0)
        ge_ref[...] = _gelu(y_ref[...]).astype(ge_ref.dtype)

    blk4 = lambda g: (g, 0, 0, 0)
    chan = pl.BlockSpec((L, SSM_BW), lambda g: (0, g))
    return pl.pallas_call(
        body, name="ssm_fwd",
        out_shape=[jax.ShapeDtypeStruct((L, D_SSM), F32), jax.ShapeDtypeStruct((L, D_SSM), BF16),
                   jax.ShapeDtypeStruct((SSM_NB, 2, SSM_SEG, SSM_SW), F32)],
        grid=(SSM_NB,),
        in_specs=[chan, _PAIR_SPEC, _PAIR_SPEC,
                  pl.BlockSpec((None, 2, SSM_NPAIR, 128), blk4),
                  pl.BlockSpec((None, 2, SSM_NPAIR, 128), blk4),
                  pl.BlockSpec((1, SSM_BW), lambda g: (0, g))],
        out_specs=[chan, chan, pl.BlockSpec((None, 2, SSM_SEG, SSM_SW), blk4)],
        scratch_shapes=[pltpu.VMEM((L, SSM_SW), F32), pltpu.VMEM((2, SSM_BW, SSM_SW), BF16), pltpu.VMEM((2, SSM_BW, SSM_SW), BF16)],
        compiler_params=_params(("parallel",)),
    )(u_p, wb, wc, are, aim, dvec)


def _ssm_bwd(u_p, dy_p, entry, wb, wc, are, aim, dvec, sib_ts, sib_grads):
    L = u_p.shape[0]
    T = L // SSM_SEG
    RC = min(512, L)
    n_sib = len(sib_ts)

    def lam_acc(acc, s, x):
        new = []
        for q in range(SSM_NPAIR):
            sr, si, xr, xi = s[2 * q], s[2 * q + 1], x[2 * q], x[2 * q + 1]
            new += [acc[2 * q] + sr * xr + si * xi, acc[2 * q + 1] + si * xr - sr * xi]
        return tuple(new)

    def body(u_ref, dy_ref, entry_ref, wb_ref, wc_ref, are_ref, aim_ref, d_ref, *rest):
        du_ref, dwb_ref, dwc_ref, dare_ref, daim_ref, dd_ref = rest[n_sib:n_sib + 6]
        x_scr, s_scr, wb_s, wc_s, dwb_s, dwc_s = rest[2 * n_sib + 6:2 * n_sib + 12]
        sib = _SiblingExchange(sib_ts, rest[:n_sib], rest[n_sib + 6:2 * n_sib + 6], rest[2 * n_sib + 12:])
        pl.when(pl.program_id(0) == 0)(sib.start)
        _expand_pairs(wb_ref, wb_s)
        _expand_pairs(wc_ref, wc_s)
        du_ref[...] = dy_ref[...] * d_ref[...]
        dd_ref[...] = _colsum(dy_ref[...] * u_ref[...])
        dwb_s[...] = jnp.zeros_like(dwb_s)
        dwc_s[...] = jnp.zeros_like(dwc_s)
        for d in range(2):
            rev = d == 1

            def in_chunk(c, _):
                rows = pl.ds(pl.multiple_of(c * RC, RC), RC)
                x_scr[rows, :] = jnp.dot(u_ref[rows, :].astype(BF16), wb_s[d], preferred_element_type=F32)
                s_scr[rows, :] = jnp.dot(dy_ref[rows, :].astype(BF16), wc_s[d], preferred_element_type=F32)
                return 0
            lax.fori_loop(0, L // RC, in_chunk, 0)
            a, b = _lam_rows(are_ref, aim_ref, d)
            x_in = _seg_scan(x_scr, a, b, T, reverse=rev, entry=[entry_ref[d, :, q * 128:(q + 1) * 128] for q in range(2 * SSM_NPAIR)])

            def pair_with_row(lt, s_new, acc):
                xrow = pl.multiple_of((lt + 1 if rev else lt - 1) * SSM_SEG, SSM_SEG)
                xb = x_scr[pl.ds(xrow, SSM_SEG), :]
                return lam_acc(acc, s_new, [xb[:, i * 128:(i + 1) * 128] for i in range(2 * SSM_NPAIR)])

            zeros = (jnp.zeros((SSM_SEG, 128), F32),) * (2 * SSM_NPAIR)
            _, acc = _seg_scan(s_scr, a, [-v for v in b], T, reverse=not rev,
                               tap=(pair_with_row, lambda lt, s_new, acc: lam_acc(acc, s_new, x_in), zeros))
            for q in range(SSM_NPAIR):
                dare_ref[d, q:q + 1, :] = _colsum(acc[2 * q])
                daim_ref[d, q:q + 1, :] = _colsum(acc[2 * q + 1])

            def out_chunk(c, _):
                rows = pl.ds(pl.multiple_of(c * RC, RC), RC)
                xs, ss = x_scr[rows, :].astype(BF16), s_scr[rows, :].astype(BF16)
                uu, dd = u_ref[rows, :].astype(BF16), dy_ref[rows, :].astype(BF16)
                dwc_s[d] += lax.dot_general(dd, xs, _TA, preferred_element_type=F32)
                dwb_s[d] += lax.dot_general(uu, ss, _TA, preferred_element_type=F32)
                du_ref[rows, :] += lax.dot_general(ss, wb_s[d], _QK_T, preferred_element_type=F32)
                return 0
            lax.fori_loop(0, L // RC, out_chunk, 0)
            for j in range(SSM_NPAIR):
                for r in range(2):
                    rows, cols = _pair_window(j, r)
                    dwb_ref[d, j, r] = dwb_s[d, rows, cols]
                    dwc_ref[d, j, r] = dwc_s[d, rows, cols]
        pl.when(pl.program_id(0) == SSM_NB - 1)(sib.finish)

    blk4 = lambda g: (g, 0, 0, 0)
    chan = pl.BlockSpec((L, SSM_BW), lambda g: (0, g))
    par_specs = [_PAIR_SPEC, _PAIR_SPEC,
                 pl.BlockSpec((None, 2, SSM_NPAIR, 128), blk4),
                 pl.BlockSpec((None, 2, SSM_NPAIR, 128), blk4),
                 pl.BlockSpec((1, SSM_BW), lambda g: (0, g))]
    dense = lambda dt: pltpu.VMEM((2, SSM_BW, SSM_SW), dt)
    return pl.pallas_call(
        body, name="ssm_bwd",
        out_shape=[jax.ShapeDtypeStruct((L, D_SSM), F32),
                   jax.ShapeDtypeStruct(wb.shape, F32),
                   jax.ShapeDtypeStruct(wc.shape, F32),
                   jax.ShapeDtypeStruct((SSM_NB, 2, SSM_NPAIR, 128), F32),
                   jax.ShapeDtypeStruct((SSM_NB, 2, SSM_NPAIR, 128), F32),
                   jax.ShapeDtypeStruct((1, D_SSM), F32)] + _SiblingExchange.out_shape(sib_ts),
        grid=(SSM_NB,),
        in_specs=[chan, chan, pl.BlockSpec((None, 2, SSM_SEG, SSM_SW), blk4)] + par_specs + [_ANY] * n_sib,
        out_specs=[chan] + par_specs + [_ANY] * n_sib,
        scratch_shapes=[pltpu.VMEM((L, SSM_SW), F32), pltpu.VMEM((L, SSM_SW), F32), dense(BF16), dense(BF16), dense(F32), dense(F32)] +
                       _SiblingExchange.scratch(sib_ts),
        compiler_params=_params(("arbitrary",)),
    )(u_p, dy_p, entry, wb, wc, are, aim, dvec, *sib_grads)


def _ssm_disc(a_re, a_im, log_dt, b_re, b_im):
    lam = lax.complex(jnp.minimum(a_re, -1e-4), a_im)
    dt = jnp.exp(log_dt)[..., None]
    lam_bar = jnp.exp(lam * dt)
    b_bar = ((lam_bar - 1.0) / lam)[..., None] * lax.complex(b_re, b_im)
    return jnp.real(lam_bar), jnp.imag(lam_bar), jnp.real(b_bar), jnp.imag(b_bar)


_EYE2 = np.eye(2, dtype=np.float32)[:, None, :, None]


def _to_pairs(t):
    t = t.reshape(2, SSM_NB, SSM_NPAIR, 2, 2, SSM_H, SSM_P).transpose(1, 0, 2, 4, 3, 5, 6)
    return (t[..., None, :] * _EYE2).reshape(SSM_NB, 2, SSM_NPAIR, 2, 2 * SSM_H, 2 * SSM_P)


def _from_pairs(c):
    t = c.reshape(SSM_NB, 2, SSM_NPAIR, 2, 2, SSM_H, 2, SSM_P)
    t = jnp.stack([t[:, :, :, :, 0, :, 0, :], t[:, :, :, :, 1, :, 1, :]], axis=4)
    return t.transpose(1, 0, 2, 4, 3, 5, 6).reshape(2, SSM_G, 2, SSM_H, SSM_P)


def _to_lam(v):
    return v.reshape(2, SSM_NB, SSM_NPAIR, 128).transpose(1, 0, 2, 3)


def _from_lam(v):
    return v.transpose(1, 0, 2, 3).reshape(2, SSM_G, SSM_P)


_MESH = pl.DeviceIdType.MESH
_ANY = pl.BlockSpec(memory_space=pl.ANY)
_BIG = (("w_in", (D_MODEL, D_IN), 1, D_IN // N_CHIPS),
        ("w_glu", (D_SSM, 2 * D_SSM), 1, 2 * D_SSM // N_CHIPS),
        ("w_out", (D_ATTN + D_SSM, D_MODEL), 0, (D_ATTN + D_SSM) // N_CHIPS),
        ("w_ple_gate", (D_MODEL, D_MODEL), 0, D_MODEL // N_CHIPS),
        ("w_ple_proj", (PLE_DIM, D_MODEL), 1, D_MODEL // N_CHIPS))


def _place():
    x, y, c = lax.axis_index("x"), lax.axis_index("y"), lax.axis_index("c")
    return x, y, c, [(1 - x, y), (x, 1 - y), (1 - x, 1 - y)]


class _Gather:
    def __init__(self, ts, srcs, dsts, stage, sems):
        self.ts, self.srcs, self.dsts, self.stage = ts, srcs, dsts, stage
        self.send_sems, self.recv_sems, self.fwd_send_sems, self.fwd_recv_sems, self.loc_sems = sems
        self.x, self.y, self.c, self.chips = _place()
        self.n = len(ts)

    @staticmethod
    def scratch(shards):
        sems = pltpu.SemaphoreType.DMA((3, len(shards)))
        return [pltpu.VMEM(s.shape, BF16) for s in shards] + [sems, sems, sems, sems, pltpu.SemaphoreType.DMA((len(shards),))]

    def _shard_of(self, i, kk):
        _, _, axis, sz = _BIG[self.ts[i]]
        sl = pl.ds(pl.multiple_of(kk * sz, sz), sz)
        return self.dsts[i].at[:, sl] if axis == 1 else self.dsts[i].at[sl, :]

    @staticmethod
    def _half_of(ref, cc):
        n = ref.shape[0] // 2
        return ref.at[pl.ds(pl.multiple_of(cc * n, n), n), :]

    def _ici(self, j, i, kk):
        px, py = self.chips[j]
        return pltpu.make_async_remote_copy(
            src_ref=self._half_of(self.srcs[i], self.c), dst_ref=self._half_of(self._shard_of(i, kk), self.c),
            send_sem=self.send_sems.at[j, i], recv_sem=self.recv_sems.at[j, i],
            device_id=(px, py, self.c), device_id_type=_MESH)

    def _forward(self, j, i, kk, cc):
        part = self._half_of(self._shard_of(i, kk), cc)
        return pltpu.make_async_remote_copy(
            src_ref=part, dst_ref=part, send_sem=self.fwd_send_sems.at[j, i], recv_sem=self.fwd_recv_sems.at[j, i],
            device_id=(self.x, self.y, 1 - self.c), device_id_type=_MESH)

    def _load(self, i):
        return pltpu.make_async_copy(self.srcs[i], self.stage[i], self.loc_sems.at[i])

    def _place_own(self, i):
        return pltpu.make_async_copy(self.stage[i], self._shard_of(i, 2 * self.x + self.y), self.loc_sems.at[i])

    def _peers(self):
        return [(i, j, 2 * px + py) for i in range(self.n) for j, (px, py) in enumerate(self.chips)]

    def start(self, relations=(0, 1, 2)):
        for i in range(self.n):
            self._load(i).start()
        self.send(relations)

    def send(self, relations):
        for i, j, _ in self._peers():
            if j in relations:
                self._ici(j, i, 2 * self.x + self.y).start()

    def forward(self):
        for i in range(self.n):
            self._load(i).wait()
            self._place_own(i).start()
        for i, j, kk in self._peers():
            self._ici(j, i, kk).wait_recv()
            self._forward(j, i, kk, self.c).start()

    def finish(self):
        for i, j, kk in self._peers():
            self._forward(j, i, kk, 1 - self.c).wait_recv()
        for i, j, kk in self._peers():
            self._ici(j, i, kk).wait_send()
            self._forward(j, i, kk, self.c).wait_send()
        for i in range(self.n):
            self._place_own(i).wait()


def _matmul_in_gather(a, shard, *, tm=1024):
    t = 0
    (_, (K, N), _, sz) = _BIG[t]
    M = a.shape[0]
    tm = min(tm, M)
    gm = M // tm
    x, y = lax.axis_index("x"), lax.axis_index("y")
    order = jnp.stack([2 * x + y, 2 * (1 - x) + y, 2 * x + 1 - y, 2 * (1 - x) + 1 - y]).astype(jnp.int32)

    def body(order_ref, a_ref, shard_ref, z_ref, full_ref, b_vm, *sems):
        g = _Gather([t], [shard_ref], [full_ref], [b_vm], sems[:5])
        load_sem = sems[5]
        s, i = pl.program_id(0), pl.program_id(1)

        @pl.when((s == 0) & (i == 0))
        def _():
            g.start(relations=(0, 1))
            g._load(0).wait()
            g._place_own(0).start()

        for j, (px, py) in enumerate(g.chips):
            @pl.when((s == j + 1) & (i == 0))
            def _(j=j, kk=2 * px + py):
                if j == 0:
                    g._place_own(0).wait()
                g._ici(j, 0, kk).wait_recv()
                if j == 0:
                    g.send((2,))
                g._forward(j, 0, kk, g.c).start()
                g._forward(j, 0, kk, 1 - g.c).wait_recv()
                cp = pltpu.make_async_copy(g._shard_of(0, kk), b_vm, load_sem.at[0])
                cp.start()
                cp.wait()

        z_ref[...] = jnp.dot(a_ref[...], b_vm[...], preferred_element_type=F32)

        @pl.when((s == N_CHIPS - 1) & (i == gm - 1))
        def _():
            for j, (px, py) in enumerate(g.chips):
                g._ici(j, 0, 2 * px + py).wait_send()
                g._forward(j, 0, 2 * px + py, g.c).wait_send()

    return pl.pallas_call(
        body, name="mm_in",
        out_shape=[jax.ShapeDtypeStruct((M, N), F32), jax.ShapeDtypeStruct((K, N), BF16)],
        grid_spec=pltpu.PrefetchScalarGridSpec(
            num_scalar_prefetch=1, grid=(N_CHIPS, gm),
            in_specs=[pl.BlockSpec((tm, K), lambda s, i, o: (i, 0)), _ANY],
            out_specs=[pl.BlockSpec((tm, sz), lambda s, i, o: (i, o[s])), _ANY],
            scratch_shapes=_Gather.scratch([shard]) + [pltpu.SemaphoreType.DMA((1,))]),
        compiler_params=_params(("arbitrary", "arbitrary")),
    )(order, a, shard)


SMALL_W = 1024
SMALL_ROWS = 72
N_SMALL = 8 * SMALL_ROWS * SMALL_W
_RED = tuple((shape, ax, (shape[0] // 2, sz) if ax == 1 else (sz // 2, shape[1]), BF16) for _, shape, ax, sz in _BIG) + \
    (((8 * SMALL_ROWS, SMALL_W), 0, (SMALL_ROWS, SMALL_W), F32),)
_RED_TR = 128


def _piece(ref, t, kk, cc):
    _, ax, (pr, pc), _ = _RED[t]
    if ax == 1:
        return ref.at[pl.ds(pl.multiple_of(cc * pr, pr), pr), pl.ds(pl.multiple_of(kk * pc, pc), pc)]
    return ref.at[pl.ds(pl.multiple_of((2 * kk + cc) * pr, pr), pr), :]


def _half_shape(t):
    shape, ax, (pr, pc), _ = _RED[t]
    return (pr, shape[1]) if ax == 1 else (N_CHIPS * pr, pc)


def _piece_in_half(ref, t, kk):
    _, ax, (pr, pc), _ = _RED[t]
    return ref.at[:, pl.ds(pl.multiple_of(kk * pc, pc), pc)] if ax == 1 else ref.at[pl.ds(pl.multiple_of(kk * pr, pr), pr), :]


class _SiblingExchange:
    def __init__(self, ts, srcs, dsts, sems):
        send_sems, recv_sems = sems
        x, y, c, _ = _place()

        def copies():
            pairs = []
            for i, t in enumerate(ts):
                _, ax, (pr, _), _ = _RED[t]
                if ax == 1:
                    pairs.append((srcs[i].at[pl.ds(pl.multiple_of((1 - c) * pr, pr), pr), :], dsts[i]))
                else:
                    pairs += [(_piece(srcs[i], t, kk, 1 - c), _piece_in_half(dsts[i], t, kk)) for kk in range(N_CHIPS)]
            return [pltpu.make_async_remote_copy(src_ref=s, dst_ref=d, send_sem=send_sems.at[i], recv_sem=recv_sems.at[i],
                                                 device_id=(x, y, 1 - c), device_id_type=_MESH) for i, (s, d) in enumerate(pairs)]
        self.copies = copies

    @staticmethod
    def scratch(ts):
        n_dma = sum(1 if _RED[t][1] == 1 else N_CHIPS for t in ts)
        return [pltpu.SemaphoreType.DMA((n_dma,)), pltpu.SemaphoreType.DMA((n_dma,))]

    @staticmethod
    def out_shape(ts):
        return [jax.ShapeDtypeStruct(_half_shape(t), F32) for t in ts]

    def start(self):
        for cp in self.copies():
            cp.start()

    def finish(self):
        for cp in self.copies():
            cp.wait()


def _grad_sibling_exchange(ts, grads, name):
    n = len(ts)

    def body(*refs):
        ex = _SiblingExchange(ts, refs[:n], refs[n:2 * n], refs[2 * n:])
        ex.start()
        ex.finish()

    return pl.pallas_call(
        body, name=name,
        out_shape=_SiblingExchange.out_shape(ts),
        in_specs=[_ANY] * n, out_specs=[_ANY] * n,
        scratch_shapes=_SiblingExchange.scratch(ts),
    )(*grads)


def _chip_sum(t, g, rs, place):
    shape, ax, (pr, pc), dt = _RED[t]
    W = shape[1]
    tr = min(pr, _RED_TR)
    nb = pr // tr

    def body(place_ref, g_ref, rs_ref, o_ref):
        o_ref[...] = (g_ref[...] + rs_ref[...]).astype(o_ref.dtype)

    return pl.pallas_call(
        body, name="grad_chip_sum_%d" % t,
        out_shape=jax.ShapeDtypeStruct(rs.shape, dt),
        grid_spec=pltpu.PrefetchScalarGridSpec(
            num_scalar_prefetch=1, grid=(1 if ax == 1 else N_CHIPS, nb),
            in_specs=[pl.BlockSpec((tr, W), lambda kk, i, pr_: ((2 * kk + pr_[0]) * nb + i, 0)),
                      pl.BlockSpec((tr, W), lambda kk, i, pr_: (kk * nb + i, 0))],
            out_specs=pl.BlockSpec((tr, W), lambda kk, i, pr_: (kk * nb + i, 0))),
        compiler_params=_params(("parallel", "parallel")),
    )(place, g, rs)


class _ChipExchange:
    def __init__(self, ts, srcs, dsts, sems):
        self.send_sems, self.recv_sems = sems
        x, y, c, chips = _place()
        self.copies = lambda: [
            pltpu.make_async_remote_copy(src_ref=_piece_in_half(srcs[i], t, 2 * px + py), dst_ref=dsts[i].at[j],
                                         send_sem=self.send_sems.at[j, i], recv_sem=self.recv_sems.at[j, i],
                                         device_id=(px, py, c), device_id_type=_MESH)
            for i, t in enumerate(ts) for j, (px, py) in enumerate(chips)]

    @staticmethod
    def scratch(ts):
        return [pltpu.SemaphoreType.DMA((3, len(ts))), pltpu.SemaphoreType.DMA((3, len(ts)))]

    @staticmethod
    def out_shape(ts):
        return [jax.ShapeDtypeStruct((3,) + _RED[t][2], _RED[t][3]) for t in ts]

    def start(self):
        for cp in self.copies():
            cp.start()

    def finish(self):
        for cp in self.copies():
            cp.wait()


def _grad_chip_exchange(ts, sums):
    n = len(ts)

    def body(*refs):
        ex = _ChipExchange(ts, refs[:n], refs[n:2 * n], refs[2 * n:])
        ex.start()
        ex.finish()

    return pl.pallas_call(
        body, name="grad_chip_exchange",
        out_shape=_ChipExchange.out_shape(ts),
        in_specs=[_ANY] * n, out_specs=[_ANY] * n,
        scratch_shapes=_ChipExchange.scratch(ts),
    )(*sums)


def _total_sum(t, g, rs, rc, place):
    shape, ax, (pr, pc), _ = _RED[t]
    tr = min(pr, _RED_TR)
    nb = pr // tr
    small = t == len(_RED) - 1

    def body(place_ref, g_ref, rs_ref, rc_ref, o_ref):
        o_ref[...] = (g_ref[...] + rs_ref[...]) + rc_ref[0].astype(F32) + rc_ref[1].astype(F32) + rc_ref[2].astype(F32)

    if ax == 1:
        g_map = lambda i, pr_: (pr_[0] * nb + i, pr_[1])
        rs_map = lambda i, pr_: (i, pr_[1])
    else:
        g_map = lambda i, pr_: ((2 * pr_[1] + pr_[0]) * nb + i, 0)
        rs_map = lambda i, pr_: (pr_[1] * nb + i, 0)
    o_map = (lambda i, pr_: ((2 * pr_[1] + pr_[0]) * nb + i, 0)) if small else (lambda i, pr_: (pr_[0] * nb + i, 0))
    return pl.pallas_call(
        body, name="grad_total_sum_%d" % t,
        out_shape=jax.ShapeDtypeStruct(((8 if small else 2) * pr, pc), F32),
        grid_spec=pltpu.PrefetchScalarGridSpec(
            num_scalar_prefetch=1, grid=(nb,),
            in_specs=[pl.BlockSpec((tr, pc), g_map), pl.BlockSpec((tr, pc), rs_map),
                      pl.BlockSpec((3, tr, pc), lambda i, pr_: (0, i, 0))],
            out_specs=pl.BlockSpec((tr, pc), o_map)),
        compiler_params=_params(("parallel",)),
    )(place, g, rs, rc)


def _grad_final_exchange(totals):
    n = len(_RED)
    nb = n - 1

    def body(*refs):
        srcs, dsts, (send_sems, recv_sems) = refs[:n], refs[n:2 * n], refs[2 * n:]
        x, y, c, chips = _place()
        me = 4 * x + 2 * y + c
        others = [(x, y, 1 - c)] + [(px, py, cc) for (px, py) in chips for cc in (c, 1 - c)]

        def half(ref, t, cc):
            pr = _RED[t][2][0]
            return ref.at[pl.ds(pl.multiple_of(cc * pr, pr), pr), :]

        def eighth(ref, dev):
            return ref.at[pl.ds(pl.multiple_of(dev * SMALL_ROWS, SMALL_ROWS), SMALL_ROWS), :]

        def big_copy(t, cc):
            return pltpu.make_async_remote_copy(src_ref=half(srcs[t], t, cc), dst_ref=half(dsts[t], t, cc), send_sem=send_sems.at[t],
                                                recv_sem=recv_sems.at[t], device_id=others[0], device_id_type=_MESH)

        def small_copy(i, dev):
            return pltpu.make_async_remote_copy(src_ref=eighth(srcs[nb], dev), dst_ref=eighth(dsts[nb], dev),
                                                send_sem=send_sems.at[nb + i], recv_sem=recv_sems.at[nb + i],
                                                device_id=others[i], device_id_type=_MESH)

        sends = [big_copy(t, c) for t in range(nb)] + [small_copy(i, me) for i in range(7)]
        for cp in sends:
            cp.start()
        for t in range(nb):
            big_copy(t, 1 - c).wait_recv()
        for i, (px, py, pc) in enumerate(others):
            small_copy(i, 4 * px + 2 * py + pc).wait_recv()
        for cp in sends:
            cp.wait_send()

    return pl.pallas_call(
        body, name="grad_final_exchange",
        out_shape=[jax.ShapeDtypeStruct(a.shape, F32) for a in totals],
        in_specs=[_ANY] * n, out_specs=[_ANY] * n,
        input_output_aliases={t: t for t in range(n)},
        scratch_shapes=[pltpu.SemaphoreType.DMA((nb + 7,)), pltpu.SemaphoreType.DMA((nb + 7,))],
    )(*totals)


def _grad_place():
    return jnp.stack([lax.axis_index("c"), 2 * lax.axis_index("x") + lax.axis_index("y")]).astype(jnp.int32)


def _reduce_begin(ts, grads, place, tag, from_sibling=None):
    if from_sibling is None:
        from_sibling = _grad_sibling_exchange(ts, grads, "grad_sibling_exchange_" + tag)
    return from_sibling, [_chip_sum(t, g, r, place) for t, g, r in zip(ts, grads, from_sibling)]


def _reduce_end(ts, grads, from_sibling, from_chips, place):
    return [_total_sum(t, g, r, q, place) for t, g, r, q in zip(ts, grads, from_sibling, from_chips)]


_EARLY = (1, 2, 3, 4)
_W_IN = (0,)
_SMALL_RED = (5,)


def _adamw_math(w, g, m, v):
    m = ADAM_B1 * m + (1.0 - ADAM_B1) * g
    v = ADAM_B2 * v + (1.0 - ADAM_B2) * (g * g)
    m_hat = m / (1.0 - ADAM_B1 ** ADAM_STEP)
    v_hat = v / (1.0 - ADAM_B2 ** ADAM_STEP)
    return -ADAM_LR * (m_hat / (jnp.sqrt(v_hat) + ADAM_EPS) + ADAM_WD * w), m, v


def _adamw(w, g, m, v, name):
    W = w.shape[1]
    return _rowwise(_adamw_math, [(a, 0, W) for a in (w, g, m, v)], [], [(W, F32)] * 3, tr=128, name=name)


def _adamw_whole(ws, gs, ms, vs, name):
    n = len(ws)

    def body(*refs):
        ins, outs = refs[:4 * n], refs[4 * n:]
        for i in range(n):
            res = _adamw_math(*[ins[j * n + i][...] for j in range(4)])
            for j in range(3):
                outs[j * n + i][...] = res[j]

    res = pl.pallas_call(
        body, name=name,
        out_shape=[jax.ShapeDtypeStruct(a.shape, F32) for a in ws] * 3,
        compiler_params=pltpu.CompilerParams(vmem_limit_bytes=VMEM_LIMIT_V7X),
    )(*ws, *gs, *ms, *vs)
    return res[:n], res[n:2 * n], res[2 * n:]


def _chunks(arr, off, width, w=512):
    return [(arr, off + i * w, w) for i in range(width // w)]


def _cat(vs):
    return jnp.concatenate(vs, axis=1)


def _forward_backward(x, p_b, tgt, shards, small):
    L = x.shape[0]
    row = lambda v: v.reshape(1, -1)
    g_mix, g_ple, g_fin = row(small["norm_mix"]), row(small["norm_ple"]), row(small["norm_final"])
    gq, gk, b_glu = row(small["q_norm"]), row(small["k_norm"]), row(small["b_glu"])
    cos, sin = _rope_tables(L)

    hn_b, = _rowwise(lambda x, g: x * _rms(x) * g, [(x, 0, D_MODEL)], [g_mix], [(D_MODEL, BF16)], name="norm_mix")
    z, w_in = _matmul_in_gather(hn_b, shards[0])
    qr, kr, vb = _attn_prep(z, gq, gk, cos, sin)
    o, lse, w_glu, w_out, w_pg, w_pp = _attn_fwd(qr, kr, vb, [1, 2, 3, 4], shards[1:])

    ssm_names = ("ssm_a_re", "ssm_a_im", "ssm_log_dt", "ssm_b_re", "ssm_b_im")
    (lre, lim, bre, bim), disc_vjp = jax.vjp(_ssm_disc, *[small[n][0] for n in ssm_names])
    ssm = (_to_pairs(jnp.swapaxes(jnp.stack([bre, bim], axis=2), -1, -2)),
           _to_pairs(jnp.stack([small["ssm_c_re"][0], -small["ssm_c_im"][0]], axis=2)),
           _to_lam(lre), _to_lam(lim), row(small["ssm_d"]))
    u_p = _seg_perm(z[:, Z_U:Z_U + D_SSM])
    y_p, ge_p, ssm_entry = _ssm_fwd(u_p, *ssm)
    ge_b = _seg_unperm(ge_p)
    glu = _matmul(ge_b, w_glu, name="mm_glu")

    def merge(ga0, ga1, a, b, gs0, gs1, o, bias):
        sa, _ = _silu_and_grad(_cat([ga0, ga1]))
        ss, _ = _silu_and_grad(_cat([gs0, gs1]))
        y2 = (a + bias[:, :D_SSM]) * _sig(b + bias[:, D_SSM:])
        return _cat([o * sa, y2 * ss])
    merge_rows = _chunks(z, Z_GA, D_ATTN) + [(glu, 0, D_SSM), (glu, D_SSM, D_SSM)] + _chunks(z, Z_GS, D_SSM) + [(o, 0, D_ATTN)]
    cat_b, = _rowwise(merge, merge_rows, [b_glu], [(D_MODEL, BF16)], name="merge")
    t_out = _matmul(cat_b, w_out, name="mm_out")

    def resid(x, t, g):
        h1 = x + t
        return h1, h1 * _rms(h1) * g
    h1, hp_b = _rowwise(resid, [(x, 0, D_MODEL), (t_out, 0, D_MODEL)], [g_ple], [(D_MODEL, F32), (D_MODEL, BF16)], name="resid_norm")
    gl = _matmul(hp_b, w_pg, name="mm_ple_gate")
    pp = _matmul(p_b, w_pp, name="mm_ple_proj")

    def head(h1, gl, pp, tgt, g):
        gate = _sig(gl)
        h2 = h1 + gate * pp
        r = _rms(h2)
        n = h2 * r
        err = n * g - tgt
        dy = err * (1.0 / D_MODEL)
        dn = dy * g
        dh2 = r * (dn - n * jnp.mean(dn * n, axis=-1, keepdims=True))
        dgate = dh2 * pp
        return dh2, dh2 * gate, dgate * gate * (1.0 - gate), _colsum(dy * n), _colsum(0.5 * err * err * (1.0 / D_MODEL))
    dh2, dpp_b, dgl_b, dg_fin, loss_cols = _rowwise(
        head, [(a, 0, D_MODEL) for a in (h1, gl, pp, tgt)], [g_fin],
        [(D_MODEL, F32), (D_MODEL, BF16), (D_MODEL, BF16)], [(1, D_MODEL), (1, D_MODEL)], name="loss_head")

    dw_pp = _matmul(p_b, dpp_b, ta=True, name="mm_d_w_ple_proj")
    dw_pg = _matmul(hp_b, dgl_b, ta=True, name="mm_d_w_ple_gate")
    dhp = _matmul(dgl_b, w_pg, tb=True, name="mm_d_hp")

    def resid_bwd(dhp, h1, dh2, g):
        dx, dg = _rms_bwd(h1, g, dhp)
        dh1 = dh2 + dx
        return dh1, dh1, _colsum(dg)
    dh1, dh1_b, dg_ple = _rowwise(resid_bwd, [(a, 0, D_MODEL) for a in (dhp, h1, dh2)], [g_ple],
                                  [(D_MODEL, F32), (D_MODEL, BF16)], [(1, D_MODEL)], name="resid_norm_bwd")
    dw_out = _matmul(cat_b, dh1_b, ta=True, name="mm_d_w_out")
    dcat = _matmul(dh1_b, w_out, tb=True, name="mm_d_cat")

    def merge_bwd(dya, dys, ga0, ga1, a, b, gs0, gs1, o, bias):
        ga, gs = _cat([ga0, ga1]), _cat([gs0, gs1])
        sa, dsa = _silu_and_grad(ga)
        ss, dss = _silu_and_grad(gs)
        a, sb = a + bias[:, :D_SSM], _sig(b + bias[:, D_SSM:])
        dy2 = dys * ss
        dglu = _cat([dy2 * sb, dy2 * a * sb * (1.0 - sb)])
        do = dya * sa
        lane = lax.broadcasted_iota(jnp.int32, (do.shape[0], HEAD_DIM), 1)
        delta = sum(jnp.where(lane == i, jnp.sum(h, axis=1, keepdims=True), 0.0)
                    for i, h in enumerate(_heads(do * o)))
        return do, dya * o * dsa, dys * (a * sb) * dss, dglu, delta, _colsum(dglu)
    do_b, dga_b, dgs_b, dglu_b, delta, db_glu = _rowwise(
        merge_bwd, [(dcat, 0, D_ATTN), (dcat, D_ATTN, D_SSM)] + merge_rows, [b_glu],
        [(D_ATTN, BF16), (D_ATTN, BF16), (D_SSM, BF16), (2 * D_SSM, BF16), (HEAD_DIM, F32)], [(1, 2 * D_SSM)], name="merge_bwd")
    dw_glu = _matmul(ge_b, dglu_b, ta=True, name="mm_d_w_glu")
    dge = _matmul(dglu_b, w_glu, tb=True, name="mm_d_ge")
    place = _grad_place()
    early_grads = [dw_glu, dw_out, dw_pg, dw_pp]
    dy_p, = _rowwise(lambda dge, y: dge * _gelu_grad(y), [(_seg_perm(dge), 0, D_SSM), (y_p, 0, D_SSM)], [], [(D_SSM, F32)],
                     name="gelu_bwd")
    du_p, dwb, dwc, dare, daim, d_ssm_d, *early_sib = _ssm_bwd(u_p, dy_p, ssm_entry, *ssm, _EARLY, early_grads)
    dcc, dbb = _from_pairs(dwc), jnp.swapaxes(_from_pairs(dwb), -1, -2)
    dc_re, dc_im = dcc[:, :, 0], -dcc[:, :, 1]
    da_re, da_im, dlog_dt, db_re, db_im = disc_vjp((_from_lam(dare), _from_lam(daim), dbb[:, :, 0], dbb[:, :, 1]))

    early_sib, early_sums = _reduce_begin(_EARLY, early_grads, place, "early", from_sibling=early_sib)
    dqr, dkr, dv, *early_chips = _attn_bwd(qr, kr, kr.T, vb, do_b, lse.reshape(N_HEADS, 1, L), delta[:, :N_HEADS].T.reshape(N_HEADS, 1, L),
                                           _EARLY, early_sums)
    early_totals = _reduce_end(_EARLY, early_grads, early_sib, early_chips, place)
    dq_b, dk_b, dgq, dgk = _attn_prep_bwd(dqr, dkr, z, gq, gk, cos, sin)
    dz_b = _cat([dq_b, dk_b, dv.astype(BF16), dga_b, _seg_unperm(du_p).astype(BF16), dgs_b])
    dw_in = _matmul(hn_b, dz_b, ta=True, name="mm_d_w_in")
    w_in_sib, w_in_sums = _reduce_begin(_W_IN, [dw_in], place, "w_in")
    dhn, *w_in_chips = _matmul(dz_b, w_in, tb=True, name="mm_d_hn", exchange=(_W_IN, w_in_sums))
    w_in_total, = _reduce_end(_W_IN, [dw_in], w_in_sib, w_in_chips, place)

    def norm_bwd(dhn, x, dh1, g):
        dx, dg = _rms_bwd(x, g, dhn)
        return dh1 + dx, _colsum(dg)
    grad_x, dg_mix = _rowwise(norm_bwd, [(a, 0, D_MODEL) for a in (dhn, x, dh1)], [g_mix], [(D_MODEL, F32)], [(1, D_MODEL)],
                              name="norm_mix_bwd")

    small_grads = {"norm_mix": dg_mix, "q_norm": dgq, "k_norm": dgk, "ssm_a_re": da_re, "ssm_a_im": da_im, "ssm_log_dt": dlog_dt,
                   "ssm_b_re": db_re, "ssm_b_im": db_im, "ssm_c_re": dc_re, "ssm_c_im": dc_im, "ssm_d": d_ssm_d,
                   "b_glu": db_glu, "norm_ple": dg_ple, "norm_final": dg_fin}
    return jnp.sum(loss_cols), grad_x, [w_in_total] + early_totals, small_grads, place


_SMALL = ("norm_mix", "q_norm", "k_norm", "ssm_a_re", "ssm_a_im", "ssm_log_dt", "ssm_b_re", "ssm_b_im", "ssm_c_re", "ssm_c_im",
          "ssm_d", "b_glu", "norm_ple", "norm_final")
_WEIGHTS = ("norm_mix", "w_in", "q_norm", "k_norm", "ssm_a_re", "ssm_a_im", "ssm_log_dt", "ssm_b_re", "ssm_b_im", "ssm_c_re",
            "ssm_c_im", "ssm_d", "w_glu", "b_glu", "w_out", "norm_ple", "w_ple_gate", "w_ple_proj", "norm_final")


_SMALL_ADAMW_GROUPS = (("ssm_b_re",), ("ssm_b_im",), ("ssm_c_re", "ssm_c_im"),
                       ("norm_mix", "q_norm", "k_norm", "ssm_a_re", "ssm_a_im", "ssm_log_dt", "ssm_d", "b_glu", "norm_ple",
                        "norm_final"))


def _flat_small(d):
    flat = jnp.concatenate([d[n].reshape(-1).astype(F32) for n in _SMALL])
    return jnp.pad(flat, (0, N_SMALL - flat.shape[0]))


def _split_small(flat, like):
    out, off = {}, 0
    for n in _SMALL:
        sz = math.prod(like[n].shape)
        out[n] = flat[off:off + sz].reshape(like[n].shape)
        off += sz
    return out


def kernel(x, p, norm_mix, w_in, q_norm, k_norm, ssm_a_re, ssm_a_im, ssm_log_dt, ssm_b_re, ssm_b_im, ssm_c_re, ssm_c_im, ssm_d, w_glu, b_glu, w_out, norm_ple, w_ple_gate, w_ple_proj, norm_final, loss_target, m_norm_mix, m_w_in, m_q_norm, m_k_norm, m_ssm_a_re, m_ssm_a_im, m_ssm_log_dt, m_ssm_b_re, m_ssm_b_im, m_ssm_c_re, m_ssm_c_im, m_ssm_d, m_w_glu, m_b_glu, m_w_out, m_norm_ple, m_w_ple_gate, m_w_ple_proj, m_norm_final, v_norm_mix, v_w_in, v_q_norm, v_k_norm, v_ssm_a_re, v_ssm_a_im, v_ssm_log_dt, v_ssm_b_re, v_ssm_b_im, v_ssm_c_re, v_ssm_c_im, v_ssm_d, v_w_glu, v_b_glu, v_w_out, v_norm_ple, v_w_ple_gate, v_w_ple_proj, v_norm_final):
    w = dict(norm_mix=norm_mix, w_in=w_in, q_norm=q_norm, k_norm=k_norm, ssm_a_re=ssm_a_re, ssm_a_im=ssm_a_im,
             ssm_log_dt=ssm_log_dt, ssm_b_re=ssm_b_re, ssm_b_im=ssm_b_im, ssm_c_re=ssm_c_re, ssm_c_im=ssm_c_im, ssm_d=ssm_d,
             w_glu=w_glu, b_glu=b_glu, w_out=w_out, norm_ple=norm_ple, w_ple_gate=w_ple_gate, w_ple_proj=w_ple_proj,
             norm_final=norm_final)
    m = dict(norm_mix=m_norm_mix, w_in=m_w_in, q_norm=m_q_norm, k_norm=m_k_norm, ssm_a_re=m_ssm_a_re, ssm_a_im=m_ssm_a_im,
             ssm_log_dt=m_ssm_log_dt, ssm_b_re=m_ssm_b_re, ssm_b_im=m_ssm_b_im, ssm_c_re=m_ssm_c_re, ssm_c_im=m_ssm_c_im,
             ssm_d=m_ssm_d, w_glu=m_w_glu, b_glu=m_b_glu, w_out=m_w_out, norm_ple=m_norm_ple, w_ple_gate=m_w_ple_gate,
             w_ple_proj=m_w_ple_proj, norm_final=m_norm_final)
    v = dict(norm_mix=v_norm_mix, w_in=v_w_in, q_norm=v_q_norm, k_norm=v_k_norm, ssm_a_re=v_ssm_a_re, ssm_a_im=v_ssm_a_im,
             ssm_log_dt=v_ssm_log_dt, ssm_b_re=v_ssm_b_re, ssm_b_im=v_ssm_b_im, ssm_c_re=v_ssm_c_re, ssm_c_im=v_ssm_c_im,
             ssm_d=v_ssm_d, w_glu=v_w_glu, b_glu=v_b_glu, w_out=v_w_out, norm_ple=v_norm_ple, w_ple_gate=v_w_ple_gate,
             w_ple_proj=v_w_ple_proj, norm_final=v_norm_final)
    big_names = [n for n, _, _, _ in _BIG]

    small = {n: w[n] for n in _SMALL}
    loss_part, grad_x, big_totals, small_grads, place = _forward_backward(
        x[0], p[0, 0].astype(BF16), loss_target[0], [w[n][0].astype(BF16) for n in big_names], small)
    loss = lax.psum(loss_part, ("x", "y", "c"))

    small_flat = [_flat_small(small_grads).reshape(8 * SMALL_ROWS, SMALL_W)]
    small_sib, small_sums = _reduce_begin(_SMALL_RED, small_flat, place, "small")
    small_total = _reduce_end(_SMALL_RED, small_flat, small_sib, _grad_chip_exchange(_SMALL_RED, small_sums), place)
    *big_red, small_red = _grad_final_exchange(big_totals + small_total)
    grads = _split_small(small_red.reshape(-1), w)
    delta, new_m, new_v = {}, {}, {}
    for n, g in zip(big_names, big_red):
        grads[n] = g[None]
        d_, m_, v_ = _adamw(w[n][0], g, m[n][0], v[n][0], "adamw_" + n)
        delta[n], new_m[n], new_v[n] = d_[None], m_[None], v_[None]
    at_least_2d = lambda a: a.reshape(1, -1) if a.ndim == 1 else a
    for i, names in enumerate(_SMALL_ADAMW_GROUPS):
        d_, m_, v_ = _adamw_whole(*[[at_least_2d(src[n]) for n in names] for src in (w, grads, m, v)], "adamw_small_%d" % i)
        for j, n in enumerate(names):
            delta[n], new_m[n], new_v[n] = (a[j].reshape(w[n].shape) for a in (d_, m_, v_))
    return (loss, grad_x[None], *[grads[n] for n in _WEIGHTS], *[delta[n] for n in _WEIGHTS],
            *[new_m[n] for n in _WEIGHTS], *[new_v[n] for n in _WEIGHTS])
```

```python
import functools
import math

import jax
import jax.numpy as jnp
import numpy as np
from jax import lax
from jax.experimental import pallas as pl
from jax.experimental.pallas import tpu as pltpu

D_MODEL = 2048
GRID_W = 64
PLE_DIM = 256
D_ATTN = 1024
N_HEADS = 8
N_KV = 2
HEAD_DIM = 128
ROPE_THETA = 10000.0
D_SSM = 1024
SSM_H = 16
SSM_G = 64
SSM_P = 64
D_KV = N_KV * HEAD_DIM
D_IN = 2 * D_ATTN + 2 * D_KV + 2 * D_SSM
EPS = 1e-6
Z_Q, Z_K, Z_V, Z_GA, Z_U, Z_GS = 0, 1024, 1280, 1536, 2560, 3584

ADAM_LR, ADAM_B1, ADAM_B2, ADAM_EPS, ADAM_WD, ADAM_STEP = 0.001, 0.9, 0.999, 1e-08, 0.01, 10

N_CHIPS = 4
VMEM_LIMIT_V7X = 56 * 1024 * 1024
F32 = jnp.float32
BF16 = jnp.bfloat16


def _params(sem, vmem=VMEM_LIMIT_V7X):
    return pltpu.CompilerParams(dimension_semantics=sem, vmem_limit_bytes=vmem)


def _matmul(a, b, *, ta=False, tb=False, out_dtype=F32, tm=1024, tn=None, name, exchange=None):
    M, K = (a.shape[1], a.shape[0]) if ta else a.shape
    N = b.shape[0] if tb else b.shape[1]
    if tn is None:
        tn = 1024 if (N % 1024 == 0 and K <= 4096) else 512
    tm, tn = min(tm, M), min(tn, N)
    assert M % tm == 0 and N % tn == 0, (name, M, N, K)
    dims = (((0 if ta else 1,), (1 if tb else 0,)), ((), ()))
    ex_ts, ex_sums = exchange if exchange else ((), ())
    n_ex = len(ex_ts)
    gm, gn = M // tm, N // tn

    def body(a_ref, b_ref, *rest):
        o_ref = rest[n_ex]
        if n_ex:
            ex = _ChipExchange(ex_ts, rest[:n_ex], rest[n_ex + 1:2 * n_ex + 1], rest[2 * n_ex + 1:])
            step = pl.program_id(0) * gn + pl.program_id(1)
            pl.when(step == 0)(ex.start)
        o_ref[...] = lax.dot_general(a_ref[...], b_ref[...], dims, preferred_element_type=F32).astype(o_ref.dtype)
        if n_ex:
            pl.when(step == gm * gn - 1)(ex.finish)

    a_spec = pl.BlockSpec((K, tm), lambda i, j: (0, i)) if ta else pl.BlockSpec((tm, K), lambda i, j: (i, 0))
    b_spec = pl.BlockSpec((tn, K), lambda i, j: (j, 0)) if tb else pl.BlockSpec((K, tn), lambda i, j: (0, j))
    res = pl.pallas_call(
        body, name=name,
        out_shape=[jax.ShapeDtypeStruct((M, N), out_dtype)] + (_ChipExchange.out_shape(ex_ts) if n_ex else []),
        grid=(gm, gn),
        in_specs=[a_spec, b_spec] + [_ANY] * n_ex,
        out_specs=[pl.BlockSpec((tm, tn), lambda i, j: (i, j))] + [_ANY] * n_ex,
        scratch_shapes=_ChipExchange.scratch(ex_ts) if n_ex else [],
        compiler_params=_params(("arbitrary", "arbitrary") if n_ex else ("parallel", "parallel")),
    )(a, b, *ex_sums)
    return res if n_ex else res[0]


def _rowwise(fn, rows, consts, outs, accs=(), *, tr=256, name):
    L = rows[0][0].shape[0]
    tr = math.gcd(tr, L)
    assert tr % 8 == 0 or tr == L, (name, L, tr)
    n_in, n_c, n_o, n_a = len(rows), len(consts), len(outs), len(accs)

    def body(*refs):
        ins = [r[...] for r in refs[:n_in + n_c]]
        res = fn(*ins)
        if not isinstance(res, (tuple, list)):
            res = (res,)
        o_refs = refs[n_in + n_c:n_in + n_c + n_o]
        a_refs = refs[n_in + n_c + n_o:]
        for r, v in zip(o_refs, res[:n_o]):
            r[...] = v.astype(r.dtype)
        if n_a:
            first = pl.program_id(0) == 0

            @pl.when(first)
            def _():
                for r, v in zip(a_refs, res[n_o:]):
                    r[...] = v.astype(F32)

            @pl.when(jnp.logical_not(first))
            def _():
                for r, v in zip(a_refs, res[n_o:]):
                    r[...] += v.astype(F32)

    in_specs = []
    for arr, off, w in rows:
        assert off % w == 0, (name, off, w)
        in_specs.append(pl.BlockSpec((tr, w), functools.partial(lambda i, c: (i, c), c=off // w)))
    for c in consts:
        in_specs.append(pl.BlockSpec(c.shape, lambda i: (0, 0)))
    out_shape = [jax.ShapeDtypeStruct((L, w), dt) for w, dt in outs] + [jax.ShapeDtypeStruct(s, F32) for s in accs]
    out_specs = [pl.BlockSpec((tr, w), lambda i: (i, 0)) for w, _ in outs] + [pl.BlockSpec(s, lambda i: (0, 0)) for s in accs]
    res = pl.pallas_call(
        body, name=name,
        out_shape=out_shape,
        grid=(L // tr,),
        in_specs=in_specs,
        out_specs=out_specs,
        compiler_params=_params(("arbitrary",) if n_a else ("parallel",)),
    )(*[r[0] for r in rows], *consts)
    return res


def _sig(x):
    return jax.nn.sigmoid(x)


def _silu_and_grad(x):
    s = _sig(x)
    return x * s, s * (1.0 + x * (1.0 - s))


_GELU_C = math.sqrt(2.0 / math.pi)


def _gelu(x):
    return 0.5 * x * (1.0 + jnp.tanh(_GELU_C * (x + 0.044715 * x * x * x)))


def _gelu_grad(x):
    t = jnp.tanh(_GELU_C * (x + 0.044715 * x * x * x))
    return 0.5 * (1.0 + t) + 0.5 * x * (1.0 - t * t) * _GELU_C * (1.0 + 3.0 * 0.044715 * x * x)


def _rms(x):
    return lax.rsqrt(jnp.mean(x * x, axis=-1, keepdims=True) + EPS)


def _rms_bwd(x, g, dy):
    r = _rms(x)
    n = x * r
    dn = dy * g
    return r * (dn - n * jnp.mean(dn * n, axis=-1, keepdims=True)), dy * n


def _colsum(v):
    return jnp.sum(v, axis=0, keepdims=True)


def _rope_partner(x):
    lane = lax.broadcasted_iota(jnp.int32, x.shape, x.ndim - 1)
    return jnp.where(lane % 64 < 32, pltpu.roll(x, 96, x.ndim - 1), pltpu.roll(x, 32, x.ndim - 1))


def _rope_tables(L):
    rows_n = L // GRID_W
    rows = np.repeat(np.arange(rows_n), GRID_W).astype(np.float32)
    cols = np.tile(np.arange(GRID_W), rows_n).astype(np.float32)
    n_freq = HEAD_DIM // 4
    inv_freq = np.float32(ROPE_THETA) ** (-np.arange(n_freq, dtype=np.float32) / np.float32(n_freq))
    ar, ac = rows[:, None] * inv_freq[None, :], cols[:, None] * inv_freq[None, :]
    cos = np.concatenate([np.cos(ar), np.cos(ar), np.cos(ac), np.cos(ac)], axis=-1).astype(np.float32)
    sin = np.concatenate([-np.sin(ar), np.sin(ar), -np.sin(ac), np.sin(ac)], axis=-1).astype(np.float32)
    return jnp.asarray(cos), jnp.asarray(sin)


def _heads(v):
    return [v[:, h * HEAD_DIM:(h + 1) * HEAD_DIM] for h in range(v.shape[1] // HEAD_DIM)]


def _attn_prep(z, q_norm, k_norm, cos, sin):
    def fn(q, k, v, cos, sin, gq, gk):
        def one(xh, g):
            xn = xh * _rms(xh) * g
            return xn * cos + _rope_partner(xn) * sin
        qr = jnp.concatenate([one(h, gq) for h in _heads(q)], axis=1)
        kr = jnp.concatenate([one(h, gk) for h in _heads(k)], axis=1)
        return qr, kr, v
    return _rowwise(fn, [(z, Z_Q, D_ATTN), (z, Z_K, D_KV), (z, Z_V, D_KV), (cos, 0, HEAD_DIM), (sin, 0, HEAD_DIM)],
                    [q_norm, k_norm], [(D_ATTN, BF16), (D_KV, BF16), (D_KV, BF16)], name="attn_prep")


def _attn_prep_bwd(dqr, dkr, z, q_norm, k_norm, cos, sin):
    def fn(dqr, dkr, q, k, cos, sin, gq, gk):
        def one(dyh, xh, g):
            dn = dyh * cos + _rope_partner(dyh * sin)
            return _rms_bwd(xh, g, dn)
        rq = [one(a, b, gq) for a, b in zip(_heads(dqr), _heads(q))]
        rk = [one(a, b, gk) for a, b in zip(_heads(dkr), _heads(k))]
        dq = jnp.concatenate([r[0] for r in rq], axis=1)
        dk = jnp.concatenate([r[0] for r in rk], axis=1)
        return dq, dk, _colsum(sum(r[1] for r in rq)), _colsum(sum(r[1] for r in rk))
    return _rowwise(fn, [(dqr, 0, D_ATTN), (dkr, 0, D_KV), (z, Z_Q, D_ATTN), (z, Z_K, D_KV), (cos, 0, HEAD_DIM), (sin, 0, HEAD_DIM)],
                    [q_norm, k_norm], [(D_ATTN, BF16), (D_KV, BF16)], [(1, HEAD_DIM), (1, HEAD_DIM)], name="attn_prep_bwd")


_QK_T = (((1,), (1,)), ((), ()))
_TA = (((0,), (0,)), ((), ()))
_REP = N_HEADS // N_KV


_EXP2_SCALE = HEAD_DIM ** -0.5 * math.log2(math.e)
ATTN_FWD_KEY_CHUNKS = 4
ATTN_BWD_KEY_CHUNKS = 8


def _attn_fwd(qr, kr, vb, g_ts, g_shards, *, tq=1024):
    L = qr.shape[0]
    tq = min(tq, L)
    kc = L // ATTN_FWD_KEY_CHUNKS
    n_g = len(g_ts)
    grid = (N_HEADS, L // tq)
    steps = grid[0] * grid[1]

    def body(q_ref, k_ref, v_ref, *rest):
        o_ref, lse_ref = rest[n_g:n_g + 2]
        g = _Gather(g_ts, rest[:n_g], rest[n_g + 2:2 * n_g + 2], rest[2 * n_g + 2:3 * n_g + 2], rest[3 * n_g + 2:])
        step = pl.program_id(0) * grid[1] + pl.program_id(1)
        pl.when(step == 0)(g.start)
        pl.when(step == (3 * steps) // 4)(g.forward)
        q = q_ref[...]
        m = jnp.full((tq, 1), -jnp.inf, F32)
        l = jnp.zeros((tq, 1), F32)
        o = jnp.zeros((tq, HEAD_DIM), F32)
        for c in range(ATTN_FWD_KEY_CHUNKS):
            ks = slice(c * kc, (c + 1) * kc)
            s = lax.dot_general(q, k_ref[ks, :], _QK_T, preferred_element_type=F32)
            m_new = jnp.maximum(m, jnp.max(s, axis=1, keepdims=True))
            a = jnp.exp2((m - m_new) * _EXP2_SCALE)
            p = jnp.exp2((s - m_new) * _EXP2_SCALE)
            l = a * l + jnp.sum(p, axis=1, keepdims=True)
            o = a * o + jnp.dot(p.astype(BF16), v_ref[ks, :], preferred_element_type=F32)
            m = m_new
        o_ref[...] = o * (1.0 / l)
        lse_ref[...] = m * _EXP2_SCALE + jnp.log2(l)
        pl.when(step == steps - 1)(g.finish)

    kv = pl.BlockSpec((L, HEAD_DIM), lambda h, i: (0, h // _REP))
    return pl.pallas_call(
        body, name="attn_fwd",
        out_shape=[jax.ShapeDtypeStruct((L, D_ATTN), F32), jax.ShapeDtypeStruct((N_HEADS, L, 1), F32)] +
                  [jax.ShapeDtypeStruct(_BIG[t][1], BF16) for t in g_ts],
        grid=grid,
        in_specs=[pl.BlockSpec((tq, HEAD_DIM), lambda h, i: (i, h)), kv, kv] + [_ANY] * n_g,
        out_specs=[pl.BlockSpec((tq, HEAD_DIM), lambda h, i: (i, h)),
                   pl.BlockSpec((None, tq, 1), lambda h, i: (h, i, 0))] + [_ANY] * n_g,
        scratch_shapes=_Gather.scratch(g_shards),
        compiler_params=_params(("arbitrary", "arbitrary")),
    )(qr, kr, vb, *g_shards)


def _attn_bwd(qr, kr, k_t, vb, do, lse, delta, ex_ts, ex_sums, *, tq=1024):
    L = qr.shape[0]
    tq = min(tq, L)
    scale = HEAD_DIM ** -0.5
    kc = L // ATTN_BWD_KEY_CHUNKS
    n_ex = len(ex_ts)
    grid = (N_KV, _REP, L // tq)

    def body(q_ref, k_ref, kt_ref, v_ref, do_ref, lse_ref, delta_ref, *rest):
        dq_ref, dk_ref, dv_ref = rest[n_ex:n_ex + 3]
        ex = _ChipExchange(ex_ts, rest[:n_ex], rest[n_ex + 3:2 * n_ex + 3], rest[2 * n_ex + 3:])
        step = (pl.program_id(0) * grid[1] + pl.program_id(1)) * grid[2] + pl.program_id(2)
        pl.when(step == 0)(ex.start)

        @pl.when((pl.program_id(1) == 0) & (pl.program_id(2) == 0))
        def _():
            dk_ref[...] = jnp.zeros_like(dk_ref)
            dv_ref[...] = jnp.zeros_like(dv_ref)

        q, do, lse, delta = q_ref[...], do_ref[...], lse_ref[...], delta_ref[...]
        dq_t = 0.0
        for c in range(ATTN_BWD_KEY_CHUNKS):
            ks = slice(c * kc, (c + 1) * kc)
            st = lax.dot_general(k_ref[ks, :], q, _QK_T, preferred_element_type=F32)
            p = jnp.exp2(st * _EXP2_SCALE - lse)
            dv_ref[ks, :] += jnp.dot(p.astype(BF16), do, preferred_element_type=F32)
            dp = lax.dot_general(v_ref[ks, :], do, _QK_T, preferred_element_type=F32)
            ds = (p * (dp - delta) * scale).astype(BF16)
            dk_ref[ks, :] += jnp.dot(ds, q, preferred_element_type=F32)
            dq_t = dq_t + jnp.dot(kt_ref[:, ks], ds, preferred_element_type=F32)
        dq_ref[...] = dq_t.T
        pl.when(step == grid[0] * grid[1] * grid[2] - 1)(ex.finish)

    head = lambda g, r, i: (i, g * _REP + r)
    kv = pl.BlockSpec((L, HEAD_DIM), lambda g, r, i: (0, g))
    per_query = pl.BlockSpec((None, 1, tq), lambda g, r, i: (g * _REP + r, 0, i))
    return pl.pallas_call(
        body, name="attn_bwd",
        out_shape=[jax.ShapeDtypeStruct((L, D_ATTN), F32), jax.ShapeDtypeStruct((L, D_KV), F32), jax.ShapeDtypeStruct((L, D_KV), F32)] +
                  _ChipExchange.out_shape(ex_ts),
        grid=grid,
        in_specs=[pl.BlockSpec((tq, HEAD_DIM), head), kv,
                  pl.BlockSpec((HEAD_DIM, L), lambda g, r, i: (g, 0)), kv,
                  pl.BlockSpec((tq, HEAD_DIM), head), per_query, per_query] + [_ANY] * n_ex,
        out_specs=[pl.BlockSpec((tq, HEAD_DIM), head), kv, kv] + [_ANY] * n_ex,
        scratch_shapes=_ChipExchange.scratch(ex_ts),
        compiler_params=_params(("arbitrary", "arbitrary", "arbitrary")),
    )(qr, kr, k_t, vb, do, lse, delta, *ex_sums)


SSM_BLK = 8
SSM_NB = SSM_G // SSM_BLK
SSM_SEG = 8
SSM_UNROLL = 4


def _unrolled_loop(n, step, carry):
    u = SSM_UNROLL

    def trip(i, c):
        for j in range(u):
            c = step(i * u + j, c)
        return c
    carry = lax.fori_loop(0, n // u, trip, carry)
    for t in range(n - n % u, n):
        carry = step(jnp.int32(t), carry)
    return carry


def _cplx_pow2(a, b, n):
    for _ in range(int(math.log2(n))):
        a, b = a * a - b * b, 2.0 * a * b
    return a, b


def _seg_scan(ref, a, b, T, reverse, entry=None, tap=None):
    npair = len(a)
    zero = jnp.zeros((SSM_SEG, 128), F32)

    def make_step(store, tap_fn=None):
        def step(t, carry):
            lt = (T - 1 - t) if reverse else t
            row = pl.multiple_of(lt * SSM_SEG, SSM_SEG)
            blk = ref[pl.ds(row, SSM_SEG), :]
            new = []
            for q in range(npair):
                re, im = carry[2 * q], carry[2 * q + 1]
                nre = a[q] * re - b[q] * im + blk[:, q * 256:q * 256 + 128]
                nim = a[q] * im + b[q] * re + blk[:, q * 256 + 128:q * 256 + 256]
                new += [nre, nim]
            if store:
                ref[pl.ds(row, SSM_SEG), :] = jnp.concatenate(new, axis=1)
            extra = carry[2 * npair:]
            return tuple(new) + (tuple(tap_fn(lt, new, extra)) if tap_fn else tuple(extra))
        return step

    def second_pass(init):
        if tap is None:
            _unrolled_loop(T, make_step(True), tuple(init))
            return init
        carry = _unrolled_loop(T - 1, make_step(True, tap[0]), tuple(init) + tuple(tap[2]))
        carry = make_step(True, tap[1])(jnp.int32(T - 1), carry)
        return init, carry[2 * npair:]

    if entry is not None:
        return second_pass(entry)
    ends = _unrolled_loop(T, make_step(False), (zero,) * (2 * npair))
    sub = lax.broadcasted_iota(jnp.int32, (SSM_SEG, 128), 0)
    keep = (sub != SSM_SEG - 1) if reverse else (sub != 0)
    shift = (SSM_SEG - 1) if reverse else 1
    init = []
    for q in range(npair):
        pa, pb = _cplx_pow2(a[q], b[q], T)
        xr, xi = zero, zero
        for _ in range(SSM_SEG - 1):
            fr = ends[2 * q] + pa * xr - pb * xi
            fi = ends[2 * q + 1] + pa * xi + pb * xr
            xr = jnp.where(keep, pltpu.roll(fr, shift, 0), 0.0)
            xi = jnp.where(keep, pltpu.roll(fi, shift, 0), 0.0)
        init += [xr, xi]
    return second_pass(init)


SSM_BW = SSM_BLK * SSM_H
SSM_SW = SSM_BLK * 2 * SSM_P
SSM_NPAIR = SSM_BLK // 2
SSM_FWD_MATMUL_ROWS = 2048
SSM_BWD_MATMUL_ROWS = 1024


def _seg_perm(a):
    L, C = a.shape
    return a.reshape(SSM_SEG, L // SSM_SEG, C).transpose(1, 0, 2).reshape(L, C)


def _seg_unperm(a):
    L, C = a.shape
    return a.reshape(L // SSM_SEG, SSM_SEG, C).transpose(1, 0, 2).reshape(L, C)


def _lam_rows(are_ref, aim_ref, d):
    a = [jnp.broadcast_to(are_ref[d, j:j + 1, :], (SSM_SEG, 128)) for j in range(SSM_NPAIR)]
    b = [jnp.broadcast_to(aim_ref[d, j:j + 1, :], (SSM_SEG, 128)) for j in range(SSM_NPAIR)]
    return a, b


_PAIR_SPEC = pl.BlockSpec((None, 2, SSM_NPAIR, 2, 2 * SSM_H, 128), lambda g: (g, 0, 0, 0, 0, 0))


def _pair_window(j, r):
    return slice(j * 2 * SSM_H, (j + 1) * 2 * SSM_H), slice(j * 256 + r * 128, j * 256 + (r + 1) * 128)


def _expand_pairs(c_ref, dense_ref):
    dense_ref[...] = jnp.zeros_like(dense_ref)
    for d in range(2):
        for j in range(SSM_NPAIR):
            for r in range(2):
                rows, cols = _pair_window(j, r)
                dense_ref[d, rows, cols] = c_ref[d, j, r].astype(dense_ref.dtype)


def _ssm_fwd(u_p, wb, wc, are, aim, dvec):
    L = u_p.shape[0]
    T = L // SSM_SEG
    RC = min(SSM_FWD_MATMUL_ROWS, L)

    def body(u_ref, wb_ref, wc_ref, are_ref, aim_ref, d_ref, y_ref, ge_ref, entry_ref, x_scr, wb_s, wc_s):
        _expand_pairs(wb_ref, wb_s)
        _expand_pairs(wc_ref, wc_s)
        y_ref[...] = u_ref[...] * d_ref[...]
        for d in range(2):
            def bu_chunk(c, _):
                rows = pl.ds(pl.multiple_of(c * RC, RC), RC)
                x_scr[rows, :] = jnp.dot(u_ref[rows, :].astype(BF16), wb_s[d], preferred_element_type=F32)
                return 0
            lax.fori_loop(0, L // RC, bu_chunk, 0)
            a, b = _lam_rows(are_ref, aim_ref, d)
            entry_ref[d] = jnp.concatenate(_seg_scan(x_scr, a, b, T, reverse=(d == 1)), axis=1)

            def y_chunk(c, _):
                rows = pl.ds(pl.multiple_of(c * RC, RC), RC)
                y_ref[rows, :] += lax.dot_general(x_scr[rows, :].astype(BF16), wc_s[d], _QK_T, preferred_element_type=F32)
                return 0
            lax.fori_loop(0, L // RC, y_chunk, 0)
        ge_ref[...] = _gelu(y_ref[...]).astype(ge_ref.dtype)

    blk4 = lambda g: (g, 0, 0, 0)
    chan = pl.BlockSpec((L, SSM_BW), lambda g: (0, g))
    return pl.pallas_call(
        body, name="ssm_fwd",
        out_shape=[jax.ShapeDtypeStruct((L, D_SSM), F32), jax.ShapeDtypeStruct((L, D_SSM), BF16),
                   jax.ShapeDtypeStruct((SSM_NB, 2, SSM_SEG, SSM_SW), F32)],
        grid=(SSM_NB,),
        in_specs=[chan, _PAIR_SPEC, _PAIR_SPEC,
                  pl.BlockSpec((None, 2, SSM_NPAIR, 128), blk4),
                  pl.BlockSpec((None, 2, SSM_NPAIR, 128), blk4),
                  pl.BlockSpec((1, SSM_BW), lambda g: (0, g))],
        out_specs=[chan, chan, pl.BlockSpec((None, 2, SSM_SEG, SSM_SW), blk4)],
        scratch_shapes=[pltpu.VMEM((L, SSM_SW), F32), pltpu.VMEM((2, SSM_BW, SSM_SW), BF16), pltpu.VMEM((2, SSM_BW, SSM_SW), BF16)],
        compiler_params=_params(("parallel",)),
    )(u_p, wb, wc, are, aim, dvec)


def _ssm_bwd(u_p, dy_p, entry, wb, wc, are, aim, dvec, sib_ts, sib_grads):
    L = u_p.shape[0]
    T = L // SSM_SEG
    RC = min(SSM_BWD_MATMUL_ROWS, L)
    n_sib = len(sib_ts)

    def lam_acc(acc, s, x):
        new = []
        for q in range(SSM_NPAIR):
            sr, si, xr, xi = s[2 * q], s[2 * q + 1], x[2 * q], x[2 * q + 1]
            new += [acc[2 * q] + sr * xr + si * xi, acc[2 * q + 1] + si * xr - sr * xi]
        return tuple(new)

    def body(u_ref, dy_ref, entry_ref, wb_ref, wc_ref, are_ref, aim_ref, d_ref, *rest):
        du_ref, dwb_ref, dwc_ref, dare_ref, daim_ref, dd_ref = rest[n_sib:n_sib + 6]
        x_scr, s_scr, wb_s, wc_s, dwb_s, dwc_s = rest[2 * n_sib + 6:2 * n_sib + 12]
        sib = _SiblingExchange(sib_ts, rest[:n_sib], rest[n_sib + 6:2 * n_sib + 6], rest[2 * n_sib + 12:])
        pl.when(pl.program_id(0) == 0)(sib.start)
        _expand_pairs(wb_ref, wb_s)
        _expand_pairs(wc_ref, wc_s)
        du_ref[...] = dy_ref[...] * d_ref[...]
        dd_ref[...] = _colsum(dy_ref[...] * u_ref[...])
        dwb_s[...] = jnp.zeros_like(dwb_s)
        dwc_s[...] = jnp.zeros_like(dwc_s)
        for d in range(2):
            rev = d == 1

            def in_chunk(c, _):
                rows = pl.ds(pl.multiple_of(c * RC, RC), RC)
                x_scr[rows, :] = jnp.dot(u_ref[rows, :].astype(BF16), wb_s[d], preferred_element_type=F32)
                s_scr[rows, :] = jnp.dot(dy_ref[rows, :].astype(BF16), wc_s[d], preferred_element_type=F32)
                return 0
            lax.fori_loop(0, L // RC, in_chunk, 0)
            a, b = _lam_rows(are_ref, aim_ref, d)
            x_in = _seg_scan(x_scr, a, b, T, reverse=rev, entry=[entry_ref[d, :, q * 128:(q + 1) * 128] for q in range(2 * SSM_NPAIR)])

            def pair_with_row(lt, s_new, acc):
                xrow = pl.multiple_of((lt + 1 if rev else lt - 1) * SSM_SEG, SSM_SEG)
                xb = x_scr[pl.ds(xrow, SSM_SEG), :]
                return lam_acc(acc, s_new, [xb[:, i * 128:(i + 1) * 128] for i in range(2 * SSM_NPAIR)])

            zeros = (jnp.zeros((SSM_SEG, 128), F32),) * (2 * SSM_NPAIR)
            _, acc = _seg_scan(s_scr, a, [-v for v in b], T, reverse=not rev,
                               tap=(pair_with_row, lambda lt, s_new, acc: lam_acc(acc, s_new, x_in), zeros))
            for q in range(SSM_NPAIR):
                dare_ref[d, q:q + 1, :] = _colsum(acc[2 * q])
                daim_ref[d, q:q + 1, :] = _colsum(acc[2 * q + 1])

            def out_chunk(c, _):
                rows = pl.ds(pl.multiple_of(c * RC, RC), RC)
                xs, ss = x_scr[rows, :].astype(BF16), s_scr[rows, :].astype(BF16)
                uu, dd = u_ref[rows, :].astype(BF16), dy_ref[rows, :].astype(BF16)
                dwc_s[d] += lax.dot_general(dd, xs, _TA, preferred_element_type=F32)
                dwb_s[d] += lax.dot_general(uu, ss, _TA, preferred_element_type=F32)
                du_ref[rows, :] += lax.dot_general(ss, wb_s[d], _QK_T, preferred_element_type=F32)
                return 0
            lax.fori_loop(0, L // RC, out_chunk, 0)
            for j in range(SSM_NPAIR):
                for r in range(2):
                    rows, cols = _pair_window(j, r)
                    dwb_ref[d, j, r] = dwb_s[d, rows, cols]
                    dwc_ref[d, j, r] = dwc_s[d, rows, cols]
        pl.when(pl.program_id(0) == SSM_NB - 1)(sib.finish)

    blk4 = lambda g: (g, 0, 0, 0)
    chan = pl.BlockSpec((L, SSM_BW), lambda g: (0, g))
    par_specs = [_PAIR_SPEC, _PAIR_SPEC,
                 pl.BlockSpec((None, 2, SSM_NPAIR, 128), blk4),
                 pl.BlockSpec((None, 2, SSM_NPAIR, 128), blk4),
                 pl.BlockSpec((1, SSM_BW), lambda g: (0, g))]
    dense = lambda dt: pltpu.VMEM((2, SSM_BW, SSM_SW), dt)
    return pl.pallas_call(
        body, name="ssm_bwd",
        out_shape=[jax.ShapeDtypeStruct((L, D_SSM), F32),
                   jax.ShapeDtypeStruct(wb.shape, F32),
                   jax.ShapeDtypeStruct(wc.shape, F32),
                   jax.ShapeDtypeStruct((SSM_NB, 2, SSM_NPAIR, 128), F32),
                   jax.ShapeDtypeStruct((SSM_NB, 2, SSM_NPAIR, 128), F32),
                   jax.ShapeDtypeStruct((1, D_SSM), F32)] + _SiblingExchange.out_shape(sib_ts),
        grid=(SSM_NB,),
        in_specs=[chan, chan, pl.BlockSpec((None, 2, SSM_SEG, SSM_SW), blk4)] + par_specs + [_ANY] * n_sib,
        out_specs=[chan] + par_specs + [_ANY] * n_sib,
        scratch_shapes=[pltpu.VMEM((L, SSM_SW), F32), pltpu.VMEM((L, SSM_SW), F32), dense(BF16), dense(BF16), dense(F32), dense(F32)] +
                       _SiblingExchange.scratch(sib_ts),
        compiler_params=_params(("arbitrary",)),
    )(u_p, dy_p, entry, wb, wc, are, aim, dvec, *sib_grads)


def _ssm_disc(a_re, a_im, log_dt, b_re, b_im):
    lam = lax.complex(jnp.minimum(a_re, -1e-4), a_im)
    dt = jnp.exp(log_dt)[..., None]
    lam_bar = jnp.exp(lam * dt)
    b_bar = ((lam_bar - 1.0) / lam)[..., None] * lax.complex(b_re, b_im)
    return jnp.real(lam_bar), jnp.imag(lam_bar), jnp.real(b_bar), jnp.imag(b_bar)


_EYE2 = np.eye(2, dtype=np.float32)[:, None, :, None]


def _to_pairs(t):
    t = t.reshape(2, SSM_NB, SSM_NPAIR, 2, 2, SSM_H, SSM_P).transpose(1, 0, 2, 4, 3, 5, 6)
    return (t[..., None, :] * _EYE2).reshape(SSM_NB, 2, SSM_NPAIR, 2, 2 * SSM_H, 2 * SSM_P)


def _from_pairs(c):
    t = c.reshape(SSM_NB, 2, SSM_NPAIR, 2, 2, SSM_H, 2, SSM_P)
    t = jnp.stack([t[:, :, :, :, 0, :, 0, :], t[:, :, :, :, 1, :, 1, :]], axis=4)
    return t.transpose(1, 0, 2, 4, 3, 5, 6).reshape(2, SSM_G, 2, SSM_H, SSM_P)


def _to_lam(v):
    return v.reshape(2, SSM_NB, SSM_NPAIR, 128).transpose(1, 0, 2, 3)


def _from_lam(v):
    return v.transpose(1, 0, 2, 3).reshape(2, SSM_G, SSM_P)


_MESH = pl.DeviceIdType.MESH
_ANY = pl.BlockSpec(memory_space=pl.ANY)
_BIG = (("w_in", (D_MODEL, D_IN), 1, D_IN // N_CHIPS),
        ("w_glu", (D_SSM, 2 * D_SSM), 1, 2 * D_SSM // N_CHIPS),
        ("w_out", (D_ATTN + D_SSM, D_MODEL), 0, (D_ATTN + D_SSM) // N_CHIPS),
        ("w_ple_gate", (D_MODEL, D_MODEL), 0, D_MODEL // N_CHIPS),
        ("w_ple_proj", (PLE_DIM, D_MODEL), 1, D_MODEL // N_CHIPS))


def _place():
    x, y, c = lax.axis_index("x"), lax.axis_index("y"), lax.axis_index("c")
    return x, y, c, [(1 - x, y), (x, 1 - y), (1 - x, 1 - y)]


class _Gather:
    def __init__(self, ts, srcs, dsts, stage, sems):
        self.ts, self.srcs, self.dsts, self.stage = ts, srcs, dsts, stage
        self.send_sems, self.recv_sems, self.fwd_send_sems, self.fwd_recv_sems, self.loc_sems = sems
        self.x, self.y, self.c, self.chips = _place()
        self.n = len(ts)

    @staticmethod
    def scratch(shards):
        sems = pltpu.SemaphoreType.DMA((3, len(shards)))
        return [pltpu.VMEM(s.shape, BF16) for s in shards] + [sems, sems, sems, sems, pltpu.SemaphoreType.DMA((len(shards),))]

    def _shard_of(self, i, kk):
        _, _, axis, sz = _BIG[self.ts[i]]
        sl = pl.ds(pl.multiple_of(kk * sz, sz), sz)
        return self.dsts[i].at[:, sl] if axis == 1 else self.dsts[i].at[sl, :]

    @staticmethod
    def _half_of(ref, cc):
        n = ref.shape[0] // 2
        return ref.at[pl.ds(pl.multiple_of(cc * n, n), n), :]

    def _ici(self, j, i, kk):
        px, py = self.chips[j]
        return pltpu.make_async_remote_copy(
            src_ref=self._half_of(self.srcs[i], self.c), dst_ref=self._half_of(self._shard_of(i, kk), self.c),
            send_sem=self.send_sems.at[j, i], recv_sem=self.recv_sems.at[j, i],
            device_id=(px, py, self.c), device_id_type=_MESH)

    def _forward(self, j, i, kk, cc):
        part = self._half_of(self._shard_of(i, kk), cc)
        return pltpu.make_async_remote_copy(
            src_ref=part, dst_ref=part, send_sem=self.fwd_send_sems.at[j, i], recv_sem=self.fwd_recv_sems.at[j, i],
            device_id=(self.x, self.y, 1 - self.c), device_id_type=_MESH)

    def _load(self, i):
        return pltpu.make_async_copy(self.srcs[i], self.stage[i], self.loc_sems.at[i])

    def _place_own(self, i):
        return pltpu.make_async_copy(self.stage[i], self._shard_of(i, 2 * self.x + self.y), self.loc_sems.at[i])

    def _peers(self):
        return [(i, j, 2 * px + py) for i in range(self.n) for j, (px, py) in enumerate(self.chips)]

    def start(self, relations=(0, 1, 2)):
        for i in range(self.n):
            self._load(i).start()
        self.send(relations)

    def send(self, relations):
        for i, j, _ in self._peers():
            if j in relations:
                self._ici(j, i, 2 * self.x + self.y).start()

    def forward(self):
        for i in range(self.n):
            self._load(i).wait()
            self._place_own(i).start()
        for i, j, kk in self._peers():
            self._ici(j, i, kk).wait_recv()
            self._forward(j, i, kk, self.c).start()

    def finish(self):
        for i, j, kk in self._peers():
            self._forward(j, i, kk, 1 - self.c).wait_recv()
        for i, j, kk in self._peers():
            self._ici(j, i, kk).wait_send()
            self._forward(j, i, kk, self.c).wait_send()
        for i in range(self.n):
            self._place_own(i).wait()


def _matmul_in_gather(a, shard, *, tm=1024):
    t = 0
    (_, (K, N), _, sz) = _BIG[t]
    M = a.shape[0]
    tm = min(tm, M)
    gm = M // tm
    x, y = lax.axis_index("x"), lax.axis_index("y")
    order = jnp.stack([2 * x + y, 2 * (1 - x) + y, 2 * x + 1 - y, 2 * (1 - x) + 1 - y]).astype(jnp.int32)

    def body(order_ref, a_ref, shard_ref, z_ref, full_ref, b_vm, *sems):
        g = _Gather([t], [shard_ref], [full_ref], [b_vm], sems[:5])
        load_sem = sems[5]
        s, i = pl.program_id(0), pl.program_id(1)

        @pl.when((s == 0) & (i == 0))
        def _():
            g.start(relations=(0, 1))
            g._load(0).wait()
            g._place_own(0).start()

        for j, (px, py) in enumerate(g.chips):
            @pl.when((s == j + 1) & (i == 0))
            def _(j=j, kk=2 * px + py):
                if j == 0:
                    g._place_own(0).wait()
                g._ici(j, 0, kk).wait_recv()
                if j == 0:
                    g.send((2,))
                g._forward(j, 0, kk, g.c).start()
                g._forward(j, 0, kk, 1 - g.c).wait_recv()
                cp = pltpu.make_async_copy(g._shard_of(0, kk), b_vm, load_sem.at[0])
                cp.start()
                cp.wait()

        z_ref[...] = jnp.dot(a_ref[...], b_vm[...], preferred_element_type=F32)

        @pl.when((s == N_CHIPS - 1) & (i == gm - 1))
        def _():
            for j, (px, py) in enumerate(g.chips):
                g._ici(j, 0, 2 * px + py).wait_send()
                g._forward(j, 0, 2 * px + py, g.c).wait_send()

    return pl.pallas_call(
        body, name="mm_in",
        out_shape=[jax.ShapeDtypeStruct((M, N), F32), jax.ShapeDtypeStruct((K, N), BF16)],
        grid_spec=pltpu.PrefetchScalarGridSpec(
            num_scalar_prefetch=1, grid=(N_CHIPS, gm),
            in_specs=[pl.BlockSpec((tm, K), lambda s, i, o: (i, 0)), _ANY],
            out_specs=[pl.BlockSpec((tm, sz), lambda s, i, o: (i, o[s])), _ANY],
            scratch_shapes=_Gather.scratch([shard]) + [pltpu.SemaphoreType.DMA((1,))]),
        compiler_params=_params(("arbitrary", "arbitrary")),
    )(order, a, shard)


SMALL_W = 1024
SMALL_ROWS = 72
N_SMALL = 8 * SMALL_ROWS * SMALL_W
_RED = tuple((shape, ax, (shape[0] // 2, sz) if ax == 1 else (sz // 2, shape[1]), BF16) for _, shape, ax, sz in _BIG) + \
    (((8 * SMALL_ROWS, SMALL_W), 0, (SMALL_ROWS, SMALL_W), F32),)
_RED_TR = 128


def _piece(ref, t, kk, cc):
    _, ax, (pr, pc), _ = _RED[t]
    if ax == 1:
        return ref.at[pl.ds(pl.multiple_of(cc * pr, pr), pr), pl.ds(pl.multiple_of(kk * pc, pc), pc)]
    return ref.at[pl.ds(pl.multiple_of((2 * kk + cc) * pr, pr), pr), :]


def _half_shape(t):
    shape, ax, (pr, pc), _ = _RED[t]
    return (pr, shape[1]) if ax == 1 else (N_CHIPS * pr, pc)


def _piece_in_half(ref, t, kk):
    _, ax, (pr, pc), _ = _RED[t]
    return ref.at[:, pl.ds(pl.multiple_of(kk * pc, pc), pc)] if ax == 1 else ref.at[pl.ds(pl.multiple_of(kk * pr, pr), pr), :]


class _SiblingExchange:
    def __init__(self, ts, srcs, dsts, sems):
        send_sems, recv_sems = sems
        x, y, c, _ = _place()

        def copies():
            pairs = []
            for i, t in enumerate(ts):
                _, ax, (pr, _), _ = _RED[t]
                if ax == 1:
                    pairs.append((srcs[i].at[pl.ds(pl.multiple_of((1 - c) * pr, pr), pr), :], dsts[i]))
                else:
                    pairs += [(_piece(srcs[i], t, kk, 1 - c), _piece_in_half(dsts[i], t, kk)) for kk in range(N_CHIPS)]
            return [pltpu.make_async_remote_copy(src_ref=s, dst_ref=d, send_sem=send_sems.at[i], recv_sem=recv_sems.at[i],
                                                 device_id=(x, y, 1 - c), device_id_type=_MESH) for i, (s, d) in enumerate(pairs)]
        self.copies = copies

    @staticmethod
    def scratch(ts):
        n_dma = sum(1 if _RED[t][1] == 1 else N_CHIPS for t in ts)
        return [pltpu.SemaphoreType.DMA((n_dma,)), pltpu.SemaphoreType.DMA((n_dma,))]

    @staticmethod
    def out_shape(ts):
        return [jax.ShapeDtypeStruct(_half_shape(t), F32) for t in ts]

    def start(self):
        for cp in self.copies():
            cp.start()

    def finish(self):
        for cp in self.copies():
            cp.wait()


def _grad_sibling_exchange(ts, grads, name):
    n = len(ts)

    def body(*refs):
        ex = _SiblingExchange(ts, refs[:n], refs[n:2 * n], refs[2 * n:])
        ex.start()
        ex.finish()

    return pl.pallas_call(
        body, name=name,
        out_shape=_SiblingExchange.out_shape(ts),
        in_specs=[_ANY] * n, out_specs=[_ANY] * n,
        scratch_shapes=_SiblingExchange.scratch(ts),
    )(*grads)


def _chip_sum(t, g, rs, place):
    shape, ax, (pr, pc), dt = _RED[t]
    W = shape[1]
    tr = min(pr, _RED_TR)
    nb = pr // tr

    def body(place_ref, g_ref, rs_ref, o_ref):
        o_ref[...] = (g_ref[...] + rs_ref[...]).astype(o_ref.dtype)

    return pl.pallas_call(
        body, name="grad_chip_sum_%d" % t,
        out_shape=jax.ShapeDtypeStruct(rs.shape, dt),
        grid_spec=pltpu.PrefetchScalarGridSpec(
            num_scalar_prefetch=1, grid=(1 if ax == 1 else N_CHIPS, nb),
            in_specs=[pl.BlockSpec((tr, W), lambda kk, i, pr_: ((2 * kk + pr_[0]) * nb + i, 0)),
                      pl.BlockSpec((tr, W), lambda kk, i, pr_: (kk * nb + i, 0))],
            out_specs=pl.BlockSpec((tr, W), lambda kk, i, pr_: (kk * nb + i, 0))),
        compiler_params=_params(("parallel", "parallel")),
    )(place, g, rs)


class _ChipExchange:
    def __init__(self, ts, srcs, dsts, sems):
        self.send_sems, self.recv_sems = sems
        x, y, c, chips = _place()
        self.copies = lambda: [
            pltpu.make_async_remote_copy(src_ref=_piece_in_half(srcs[i], t, 2 * px + py), dst_ref=dsts[i].at[j],
                                         send_sem=self.send_sems.at[j, i], recv_sem=self.recv_sems.at[j, i],
                                         device_id=(px, py, c), device_id_type=_MESH)
            for i, t in enumerate(ts) for j, (px, py) in enumerate(chips)]

    @staticmethod
    def scratch(ts):
        return [pltpu.SemaphoreType.DMA((3, len(ts))), pltpu.SemaphoreType.DMA((3, len(ts)))]

    @staticmethod
    def out_shape(ts):
        return [jax.ShapeDtypeStruct((3,) + _RED[t][2], _RED[t][3]) for t in ts]

    def start(self):
        for cp in self.copies():
            cp.start()

    def finish(self):
        for cp in self.copies():
            cp.wait()


def _grad_chip_exchange(ts, sums):
    n = len(ts)

    def body(*refs):
        ex = _ChipExchange(ts, refs[:n], refs[n:2 * n], refs[2 * n:])
        ex.start()
        ex.finish()

    return pl.pallas_call(
        body, name="grad_chip_exchange",
        out_shape=_ChipExchange.out_shape(ts),
        in_specs=[_ANY] * n, out_specs=[_ANY] * n,
        scratch_shapes=_ChipExchange.scratch(ts),
    )(*sums)


def _total_sum(t, g, rs, rc, place):
    shape, ax, (pr, pc), _ = _RED[t]
    tr = min(pr, _RED_TR)
    nb = pr // tr
    small = t == len(_RED) - 1

    def body(place_ref, g_ref, rs_ref, rc_ref, o_ref):
        o_ref[...] = (g_ref[...] + rs_ref[...]) + rc_ref[0].astype(F32) + rc_ref[1].astype(F32) + rc_ref[2].astype(F32)

    if ax == 1:
        g_map = lambda i, pr_: (pr_[0] * nb + i, pr_[1])
        rs_map = lambda i, pr_: (i, pr_[1])
    else:
        g_map = lambda i, pr_: ((2 * pr_[1] + pr_[0]) * nb + i, 0)
        rs_map = lambda i, pr_: (pr_[1] * nb + i, 0)
    o_map = (lambda i, pr_: ((2 * pr_[1] + pr_[0]) * nb + i, 0)) if small else (lambda i, pr_: (pr_[0] * nb + i, 0))
    return pl.pallas_call(
        body, name="grad_total_sum_%d" % t,
        out_shape=jax.ShapeDtypeStruct(((8 if small else 2) * pr, pc), F32),
        grid_spec=pltpu.PrefetchScalarGridSpec(
            num_scalar_prefetch=1, grid=(nb,),
            in_specs=[pl.BlockSpec((tr, pc), g_map), pl.BlockSpec((tr, pc), rs_map),
                      pl.BlockSpec((3, tr, pc), lambda i, pr_: (0, i, 0))],
            out_specs=pl.BlockSpec((tr, pc), o_map)),
        compiler_params=_params(("parallel",)),
    )(place, g, rs, rc)


def _grad_final_exchange(totals):
    n = len(_RED)
    nb = n - 1

    def body(*refs):
        srcs, dsts, (send_sems, recv_sems) = refs[:n], refs[n:2 * n], refs[2 * n:]
        x, y, c, chips = _place()
        me = 4 * x + 2 * y + c
        others = [(x, y, 1 - c)] + [(px, py, cc) for (px, py) in chips for cc in (c, 1 - c)]

        def half(ref, t, cc):
            pr = _RED[t][2][0]
            return ref.at[pl.ds(pl.multiple_of(cc * pr, pr), pr), :]

        def eighth(ref, dev):
            return ref.at[pl.ds(pl.multiple_of(dev * SMALL_ROWS, SMALL_ROWS), SMALL_ROWS), :]

        def big_copy(t, cc):
            return pltpu.make_async_remote_copy(src_ref=half(srcs[t], t, cc), dst_ref=half(dsts[t], t, cc), send_sem=send_sems.at[t],
                                                recv_sem=recv_sems.at[t], device_id=others[0], device_id_type=_MESH)

        def small_copy(i, dev):
            return pltpu.make_async_remote_copy(src_ref=eighth(srcs[nb], dev), dst_ref=eighth(dsts[nb], dev),
                                                send_sem=send_sems.at[nb + i], recv_sem=recv_sems.at[nb + i],
                                                device_id=others[i], device_id_type=_MESH)

        sends = [big_copy(t, c) for t in range(nb)] + [small_copy(i, me) for i in range(7)]
        for cp in sends:
            cp.start()
        for t in range(nb):
            big_copy(t, 1 - c).wait_recv()
        for i, (px, py, pc) in enumerate(others):
            small_copy(i, 4 * px + 2 * py + pc).wait_recv()
        for cp in sends:
            cp.wait_send()

    return pl.pallas_call(
        body, name="grad_final_exchange",
        out_shape=[jax.ShapeDtypeStruct(a.shape, F32) for a in totals],
        in_specs=[_ANY] * n, out_specs=[_ANY] * n,
        input_output_aliases={t: t for t in range(n)},
        scratch_shapes=[pltpu.SemaphoreType.DMA((nb + 7,)), pltpu.SemaphoreType.DMA((nb + 7,))],
    )(*totals)


def _grad_place():
    return jnp.stack([lax.axis_index("c"), 2 * lax.axis_index("x") + lax.axis_index("y")]).astype(jnp.int32)


def _reduce_begin(ts, grads, place, tag, from_sibling=None):
    if from_sibling is None:
        from_sibling = _grad_sibling_exchange(ts, grads, "grad_sibling_exchange_" + tag)
    return from_sibling, [_chip_sum(t, g, r, place) for t, g, r in zip(ts, grads, from_sibling)]


def _reduce_end(ts, grads, from_sibling, from_chips, place):
    return [_total_sum(t, g, r, q, place) for t, g, r, q in zip(ts, grads, from_sibling, from_chips)]


_EARLY = (1, 2, 3, 4)
_W_IN = (0,)
_SMALL_RED = (5,)


def _adamw_math(w, g, m, v):
    m = ADAM_B1 * m + (1.0 - ADAM_B1) * g
    v = ADAM_B2 * v + (1.0 - ADAM_B2) * (g * g)
    m_hat = m / (1.0 - ADAM_B1 ** ADAM_STEP)
    v_hat = v / (1.0 - ADAM_B2 ** ADAM_STEP)
    return -ADAM_LR * (m_hat / (jnp.sqrt(v_hat) + ADAM_EPS) + ADAM_WD * w), m, v


def _adamw(w, g, m, v, name):
    W = w.shape[1]
    return _rowwise(_adamw_math, [(a, 0, W) for a in (w, g, m, v)], [], [(W, F32)] * 3, tr=128, name=name)


def _adamw_whole(ws, gs, ms, vs, name):
    n = len(ws)

    def body(*refs):
        ins, outs = refs[:4 * n], refs[4 * n:]
        for i in range(n):
            res = _adamw_math(*[ins[j * n + i][...] for j in range(4)])
            for j in range(3):
                outs[j * n + i][...] = res[j]

    res = pl.pallas_call(
        body, name=name,
        out_shape=[jax.ShapeDtypeStruct(a.shape, F32) for a in ws] * 3,
        compiler_params=pltpu.CompilerParams(vmem_limit_bytes=VMEM_LIMIT_V7X),
    )(*ws, *gs, *ms, *vs)
    return res[:n], res[n:2 * n], res[2 * n:]


def _chunks(arr, off, width, w=512):
    return [(arr, off + i * w, w) for i in range(width // w)]


def _cat(vs):
    return jnp.concatenate(vs, axis=1)


def _forward_backward(x, p_b, tgt, shards, small):
    L = x.shape[0]
    row = lambda v: v.reshape(1, -1)
    g_mix, g_ple, g_fin = row(small["norm_mix"]), row(small["norm_ple"]), row(small["norm_final"])
    gq, gk, b_glu = row(small["q_norm"]), row(small["k_norm"]), row(small["b_glu"])
    cos, sin = _rope_tables(L)

    hn_b, = _rowwise(lambda x, g: x * _rms(x) * g, [(x, 0, D_MODEL)], [g_mix], [(D_MODEL, BF16)], name="norm_mix")
    z, w_in = _matmul_in_gather(hn_b, shards[0])
    qr, kr, vb = _attn_prep(z, gq, gk, cos, sin)
    o, lse, w_glu, w_out, w_pg, w_pp = _attn_fwd(qr, kr, vb, [1, 2, 3, 4], shards[1:])

    ssm_names = ("ssm_a_re", "ssm_a_im", "ssm_log_dt", "ssm_b_re", "ssm_b_im")
    (lre, lim, bre, bim), disc_vjp = jax.vjp(_ssm_disc, *[small[n][0] for n in ssm_names])
    ssm = (_to_pairs(jnp.swapaxes(jnp.stack([bre, bim], axis=2), -1, -2)),
           _to_pairs(jnp.stack([small["ssm_c_re"][0], -small["ssm_c_im"][0]], axis=2)),
           _to_lam(lre), _to_lam(lim), row(small["ssm_d"]))
    u_p = _seg_perm(z[:, Z_U:Z_U + D_SSM])
    y_p, ge_p, ssm_entry = _ssm_fwd(u_p, *ssm)
    ge_b = _seg_unperm(ge_p)
    glu = _matmul(ge_b, w_glu, name="mm_glu")

    def merge(ga0, ga1, a, b, gs0, gs1, o, bias):
        sa, _ = _silu_and_grad(_cat([ga0, ga1]))
        ss, _ = _silu_and_grad(_cat([gs0, gs1]))
        y2 = (a + bias[:, :D_SSM]) * _sig(b + bias[:, D_SSM:])
        return _cat([o * sa, y2 * ss])
    merge_rows = _chunks(z, Z_GA, D_ATTN) + [(glu, 0, D_SSM), (glu, D_SSM, D_SSM)] + _chunks(z, Z_GS, D_SSM) + [(o, 0, D_ATTN)]
    cat_b, = _rowwise(merge, merge_rows, [b_glu], [(D_MODEL, BF16)], name="merge")
    t_out = _matmul(cat_b, w_out, name="mm_out")

    def resid(x, t, g):
        h1 = x + t
        return h1, h1 * _rms(h1) * g
    h1, hp_b = _rowwise(resid, [(x, 0, D_MODEL), (t_out, 0, D_MODEL)], [g_ple], [(D_MODEL, F32), (D_MODEL, BF16)], name="resid_norm")
    gl = _matmul(hp_b, w_pg, name="mm_ple_gate")
    pp = _matmul(p_b, w_pp, name="mm_ple_proj")

    def head(h1, gl, pp, tgt, g):
        gate = _sig(gl)
        h2 = h1 + gate * pp
        r = _rms(h2)
        n = h2 * r
        err = n * g - tgt
        dy = err * (1.0 / D_MODEL)
        dn = dy * g
        dh2 = r * (dn - n * jnp.mean(dn * n, axis=-1, keepdims=True))
        dgate = dh2 * pp
        return dh2, dh2 * gate, dgate * gate * (1.0 - gate), _colsum(dy * n), _colsum(0.5 * err * err * (1.0 / D_MODEL))
    dh2, dpp_b, dgl_b, dg_fin, loss_cols = _rowwise(
        head, [(a, 0, D_MODEL) for a in (h1, gl, pp, tgt)], [g_fin],
        [(D_MODEL, F32), (D_MODEL, BF16), (D_MODEL, BF16)], [(1, D_MODEL), (1, D_MODEL)], name="loss_head")

    dw_pp = _matmul(p_b, dpp_b, ta=True, name="mm_d_w_ple_proj")
    dw_pg = _matmul(hp_b, dgl_b, ta=True, name="mm_d_w_ple_gate")
    dhp = _matmul(dgl_b, w_pg, tb=True, name="mm_d_hp")

    def resid_bwd(dhp, h1, dh2, g):
        dx, dg = _rms_bwd(h1, g, dhp)
        dh1 = dh2 + dx
        return dh1, dh1, _colsum(dg)
    dh1, dh1_b, dg_ple = _rowwise(resid_bwd, [(a, 0, D_MODEL) for a in (dhp, h1, dh2)], [g_ple],
                                  [(D_MODEL, F32), (D_MODEL, BF16)], [(1, D_MODEL)], name="resid_norm_bwd")
    dw_out = _matmul(cat_b, dh1_b, ta=True, name="mm_d_w_out")
    dcat = _matmul(dh1_b, w_out, tb=True, name="mm_d_cat")

    def merge_bwd(dya, dys, ga0, ga1, a, b, gs0, gs1, o, bias):
        ga, gs = _cat([ga0, ga1]), _cat([gs0, gs1])
        sa, dsa = _silu_and_grad(ga)
        ss, dss = _silu_and_grad(gs)
        a, sb = a + bias[:, :D_SSM], _sig(b + bias[:, D_SSM:])
        dy2 = dys * ss
        dglu = _cat([dy2 * sb, dy2 * a * sb * (1.0 - sb)])
        do = dya * sa
        lane = lax.broadcasted_iota(jnp.int32, (do.shape[0], HEAD_DIM), 1)
        delta = sum(jnp.where(lane == i, jnp.sum(h, axis=1, keepdims=True), 0.0)
                    for i, h in enumerate(_heads(do * o)))
        return do, dya * o * dsa, dys * (a * sb) * dss, dglu, delta, _colsum(dglu)
    do_b, dga_b, dgs_b, dglu_b, delta, db_glu = _rowwise(
        merge_bwd, [(dcat, 0, D_ATTN), (dcat, D_ATTN, D_SSM)] + merge_rows, [b_glu],
        [(D_ATTN, BF16), (D_ATTN, BF16), (D_SSM, BF16), (2 * D_SSM, BF16), (HEAD_DIM, F32)], [(1, 2 * D_SSM)], name="merge_bwd")
    dw_glu = _matmul(ge_b, dglu_b, ta=True, name="mm_d_w_glu")
    dge = _matmul(dglu_b, w_glu, tb=True, name="mm_d_ge")
    place = _grad_place()
    early_grads = [dw_glu, dw_out, dw_pg, dw_pp]
    dy_p, = _rowwise(lambda dge, y: dge * _gelu_grad(y), [(_seg_perm(dge), 0, D_SSM), (y_p, 0, D_SSM)], [], [(D_SSM, F32)],
                     name="gelu_bwd")
    du_p, dwb, dwc, dare, daim, d_ssm_d, *early_sib = _ssm_bwd(u_p, dy_p, ssm_entry, *ssm, _EARLY, early_grads)
    dcc, dbb = _from_pairs(dwc), jnp.swapaxes(_from_pairs(dwb), -1, -2)
    dc_re, dc_im = dcc[:, :, 0], -dcc[:, :, 1]
    da_re, da_im, dlog_dt, db_re, db_im = disc_vjp((_from_lam(dare), _from_lam(daim), dbb[:, :, 0], dbb[:, :, 1]))

    early_sib, early_sums = _reduce_begin(_EARLY, early_grads, place, "early", from_sibling=early_sib)
    dqr, dkr, dv, *early_chips = _attn_bwd(qr, kr, kr.T, vb, do_b, lse.reshape(N_HEADS, 1, L), delta[:, :N_HEADS].T.reshape(N_HEADS, 1, L),
                                           _EARLY, early_sums)
    early_totals = _reduce_end(_EARLY, early_grads, early_sib, early_chips, place)
    dq_b, dk_b, dgq, dgk = _attn_prep_bwd(dqr, dkr, z, gq, gk, cos, sin)
    dz_b = _cat([dq_b, dk_b, dv.astype(BF16), dga_b, _seg_unperm(du_p).astype(BF16), dgs_b])
    dw_in = _matmul(hn_b, dz_b, ta=True, name="mm_d_w_in")
    w_in_sib, w_in_sums = _reduce_begin(_W_IN, [dw_in], place, "w_in")
    dhn, *w_in_chips = _matmul(dz_b, w_in, tb=True, name="mm_d_hn", exchange=(_W_IN, w_in_sums))
    w_in_total, = _reduce_end(_W_IN, [dw_in], w_in_sib, w_in_chips, place)

    def norm_bwd(dhn, x, dh1, g):
        dx, dg = _rms_bwd(x, g, dhn)
        return dh1 + dx, _colsum(dg)
    grad_x, dg_mix = _rowwise(norm_bwd, [(a, 0, D_MODEL) for a in (dhn, x, dh1)], [g_mix], [(D_MODEL, F32)], [(1, D_MODEL)],
                              name="norm_mix_bwd")

    small_grads = {"norm_mix": dg_mix, "q_norm": dgq, "k_norm": dgk, "ssm_a_re": da_re, "ssm_a_im": da_im, "ssm_log_dt": dlog_dt,
                   "ssm_b_re": db_re, "ssm_b_im": db_im, "ssm_c_re": dc_re, "ssm_c_im": dc_im, "ssm_d": d_ssm_d,
                   "b_glu": db_glu, "norm_ple": dg_ple, "norm_final": dg_fin}
    return jnp.sum(loss_cols), grad_x, [w_in_total] + early_totals, small_grads, place


_SMALL = ("norm_mix", "q_norm", "k_norm", "ssm_a_re", "ssm_a_im", "ssm_log_dt", "ssm_b_re", "ssm_b_im", "ssm_c_re", "ssm_c_im",
          "ssm_d", "b_glu", "norm_ple", "norm_final")
_WEIGHTS = ("norm_mix", "w_in", "q_norm", "k_norm", "ssm_a_re", "ssm_a_im", "ssm_log_dt", "ssm_b_re", "ssm_b_im", "ssm_c_re",
            "ssm_c_im", "ssm_d", "w_glu", "b_glu", "w_out", "norm_ple", "w_ple_gate", "w_ple_proj", "norm_final")


_SMALL_ADAMW_GROUPS = (("ssm_b_re",), ("ssm_b_im",), ("ssm_c_re", "ssm_c_im"),
                       ("norm_mix", "q_norm", "k_norm", "ssm_a_re", "ssm_a_im", "ssm_log_dt", "ssm_d", "b_glu", "norm_ple",
                        "norm_final"))


def _flat_small(d):
    flat = jnp.concatenate([d[n].reshape(-1).astype(F32) for n in _SMALL])
    return jnp.pad(flat, (0, N_SMALL - flat.shape[0]))


def _split_small(flat, like):
    out, off = {}, 0
    for n in _SMALL:
        sz = math.prod(like[n].shape)
        out[n] = flat[off:off + sz].reshape(like[n].shape)
        off += sz
    return out


def kernel(x, p, norm_mix, w_in, q_norm, k_norm, ssm_a_re, ssm_a_im, ssm_log_dt, ssm_b_re, ssm_b_im, ssm_c_re, ssm_c_im, ssm_d, w_glu, b_glu, w_out, norm_ple, w_ple_gate, w_ple_proj, norm_final, loss_target, m_norm_mix, m_w_in, m_q_norm, m_k_norm, m_ssm_a_re, m_ssm_a_im, m_ssm_log_dt, m_ssm_b_re, m_ssm_b_im, m_ssm_c_re, m_ssm_c_im, m_ssm_d, m_w_glu, m_b_glu, m_w_out, m_norm_ple, m_w_ple_gate, m_w_ple_proj, m_norm_final, v_norm_mix, v_w_in, v_q_norm, v_k_norm, v_ssm_a_re, v_ssm_a_im, v_ssm_log_dt, v_ssm_b_re, v_ssm_b_im, v_ssm_c_re, v_ssm_c_im, v_ssm_d, v_w_glu, v_b_glu, v_w_out, v_norm_ple, v_w_ple_gate, v_w_ple_proj, v_norm_final):
    w = dict(norm_mix=norm_mix, w_in=w_in, q_norm=q_norm, k_norm=k_norm, ssm_a_re=ssm_a_re, ssm_a_im=ssm_a_im,
             ssm_log_dt=ssm_log_dt, ssm_b_re=ssm_b_re, ssm_b_im=ssm_b_im, ssm_c_re=ssm_c_re, ssm_c_im=ssm_c_im, ssm_d=ssm_d,
             w_glu=w_glu, b_glu=b_glu, w_out=w_out, norm_ple=norm_ple, w_ple_gate=w_ple_gate, w_ple_proj=w_ple_proj,
             norm_final=norm_final)
    m = dict(norm_mix=m_norm_mix, w_in=m_w_in, q_norm=m_q_norm, k_norm=m_k_norm, ssm_a_re=m_ssm_a_re, ssm_a_im=m_ssm_a_im,
             ssm_log_dt=m_ssm_log_dt, ssm_b_re=m_ssm_b_re, ssm_b_im=m_ssm_b_im, ssm_c_re=m_ssm_c_re, ssm_c_im=m_ssm_c_im,
             ssm_d=m_ssm_d, w_glu=m_w_glu, b_glu=m_b_glu, w_out=m_w_out, norm_ple=m_norm_ple, w_ple_gate=m_w_ple_gate,
             w_ple_proj=m_w_ple_proj, norm_final=m_norm_final)
    v = dict(norm_mix=v_norm_mix, w_in=v_w_in, q_norm=v_q_norm, k_norm=v_k_norm, ssm_a_re=v_ssm_a_re, ssm_a_im=v_ssm_a_im,
             ssm_log_dt=v_ssm_log_dt, ssm_b_re=v_ssm_b_re, ssm_b_im=v_ssm_b_im, ssm_c_re=v_ssm_c_re, ssm_c_im=v_ssm_c_im,
             ssm_d=v_ssm_d, w_glu=v_w_glu, b_glu=v_b_glu, w_out=v_w_out, norm_ple=v_norm_ple, w_ple_gate=v_w_ple_gate,
             w_ple_proj=v_w_ple_proj, norm_final=v_norm_final)
    big_names = [n for n, _, _, _ in _BIG]

    small = {n: w[n] for n in _SMALL}
    loss_part, grad_x, big_totals, small_grads, place = _forward_backward(
        x[0], p[0, 0].astype(BF16), loss_target[0], [w[n][0].astype(BF16) for n in big_names], small)
    loss = lax.psum(loss_part, ("x", "y", "c"))

    small_flat = [_flat_small(small_grads).reshape(8 * SMALL_ROWS, SMALL_W)]
    small_sib, small_sums = _reduce_begin(_SMALL_RED, small_flat, place, "small")
    small_total = _reduce_end(_SMALL_RED, small_flat, small_sib, _grad_chip_exchange(_SMALL_RED, small_sums), place)
    *big_red, small_red = _grad_final_exchange(big_totals + small_total)
    grads = _split_small(small_red.reshape(-1), w)
    delta, new_m, new_v = {}, {}, {}
    for n, g in zip(big_names, big_red):
        grads[n] = g[None]
        d_, m_, v_ = _adamw(w[n][0], g, m[n][0], v[n][0], "adamw_" + n)
        delta[n], new_m[n], new_v[n] = d_[None], m_[None], v_[None]
    at_least_2d = lambda a: a.reshape(1, -1) if a.ndim == 1 else a
    for i, names in enumerate(_SMALL_ADAMW_GROUPS):
        d_, m_, v_ = _adamw_whole(*[[at_least_2d(src[n]) for n in names] for src in (w, grads, m, v)], "adamw_small_%d" % i)
        for j, n in enumerate(names):
            delta[n], new_m[n], new_v[n] = (a[j].reshape(w[n].shape) for a in (d_, m_, v_))
    return (loss, grad_x[None], *[grads[n] for n in _WEIGHTS], *[delta[n] for n in _WEIGHTS],
            *[new_m[n] for n in _WEIGHTS], *[new_v[n] for n in _WEIGHTS])
```

```python
import functools
import math

import jax
import jax.numpy as jnp
import numpy as np
from jax import lax
from jax.experimental import pallas as pl
from jax.experimental.pallas import tpu as pltpu

D_MODEL = 2048
GRID_W = 64
PLE_DIM = 256
D_ATTN = 1024
N_HEADS = 8
N_KV = 2
HEAD_DIM = 128
ROPE_THETA = 10000.0
D_SSM = 1024
SSM_H = 16
SSM_G = 64
SSM_P = 64
D_KV = N_KV * HEAD_DIM
D_IN = 2 * D_ATTN + 2 * D_KV + 2 * D_SSM
EPS = 1e-6
Z_Q, Z_K, Z_V, Z_GA, Z_U, Z_GS = 0, 1024, 1280, 1536, 2560, 3584

ADAM_LR, ADAM_B1, ADAM_B2, ADAM_EPS, ADAM_WD, ADAM_STEP = 0.001, 0.9, 0.999, 1e-08, 0.01, 10

N_CHIPS = 4
VMEM_LIMIT_V7X = 56 * 1024 * 1024
F32 = jnp.float32
BF16 = jnp.bfloat16


def _params(sem, vmem=VMEM_LIMIT_V7X):
    return pltpu.CompilerParams(dimension_semantics=sem, vmem_limit_bytes=vmem)


def _matmul(a, b, *, ta=False, tb=False, out_dtype=F32, tm=1024, tn=None, name, exchange=None):
    M, K = (a.shape[1], a.shape[0]) if ta else a.shape
    N = b.shape[0] if tb else b.shape[1]
    if tn is None:
        tn = 1024 if (N % 1024 == 0 and K <= 4096) else 512
    tm, tn = min(tm, M), min(tn, N)
    assert M % tm == 0 and N % tn == 0, (name, M, N, K)
    dims = (((0 if ta else 1,), (1 if tb else 0,)), ((), ()))
    ex_ts, ex_sums = exchange if exchange else ((), ())
    n_ex = len(ex_ts)
    gm, gn = M // tm, N // tn

    def body(a_ref, b_ref, *rest):
        o_ref = rest[n_ex]
        if n_ex:
            ex = _ChipExchange(ex_ts, rest[:n_ex], rest[n_ex + 1:2 * n_ex + 1], rest[2 * n_ex + 1:])
            step = pl.program_id(0) * gn + pl.program_id(1)
            pl.when(step == 0)(ex.start)
        o_ref[...] = lax.dot_general(a_ref[...], b_ref[...], dims, preferred_element_type=F32).astype(o_ref.dtype)
        if n_ex:
            pl.when(step == gm * gn - 1)(ex.finish)

    a_spec = pl.BlockSpec((K, tm), lambda i, j: (0, i)) if ta else pl.BlockSpec((tm, K), lambda i, j: (i, 0))
    b_spec = pl.BlockSpec((tn, K), lambda i, j: (j, 0)) if tb else pl.BlockSpec((K, tn), lambda i, j: (0, j))
    res = pl.pallas_call(
        body, name=name,
        out_shape=[jax.ShapeDtypeStruct((M, N), out_dtype)] + (_ChipExchange.out_shape(ex_ts) if n_ex else []),
        grid=(gm, gn),
        in_specs=[a_spec, b_spec] + [_ANY] * n_ex,
        out_specs=[pl.BlockSpec((tm, tn), lambda i, j: (i, j))] + [_ANY] * n_ex,
        scratch_shapes=_ChipExchange.scratch(ex_ts) if n_ex else [],
        compiler_params=_params(("arbitrary", "arbitrary") if n_ex else ("parallel", "parallel")),
    )(a, b, *ex_sums)
    return res if n_ex else res[0]


def _rowwise(fn, rows, consts, outs, accs=(), *, tr=256, name):
    L = rows[0][0].shape[0]
    tr = math.gcd(tr, L)
    assert tr % 8 == 0 or tr == L, (name, L, tr)
    n_in, n_c, n_o, n_a = len(rows), len(consts), len(outs), len(accs)

    def body(*refs):
        ins = [r[...] for r in refs[:n_in + n_c]]
        res = fn(*ins)
        if not isinstance(res, (tuple, list)):
            res = (res,)
        o_refs = refs[n_in + n_c:n_in + n_c + n_o]
        a_refs = refs[n_in + n_c + n_o:]
        for r, v in zip(o_refs, res[:n_o]):
            r[...] = v.astype(r.dtype)
        if n_a:
            first = pl.program_id(0) == 0

            @pl.when(first)
            def _():
                for r, v in zip(a_refs, res[n_o:]):
                    r[...] = v.astype(F32)

            @pl.when(jnp.logical_not(first))
            def _():
                for r, v in zip(a_refs, res[n_o:]):
                    r[...] += v.astype(F32)

    in_specs = []
    for arr, off, w in rows:
        assert off % w == 0, (name, off, w)
        in_specs.append(pl.BlockSpec((tr, w), functools.partial(lambda i, c: (i, c), c=off // w)))
    for c in consts:
        in_specs.append(pl.BlockSpec(c.shape, lambda i: (0, 0)))
    out_shape = [jax.ShapeDtypeStruct((L, w), dt) for w, dt in outs] + [jax.ShapeDtypeStruct(s, F32) for s in accs]
    out_specs = [pl.BlockSpec((tr, w), lambda i: (i, 0)) for w, _ in outs] + [pl.BlockSpec(s, lambda i: (0, 0)) for s in accs]
    res = pl.pallas_call(
        body, name=name,
        out_shape=out_shape,
        grid=(L // tr,),
        in_specs=in_specs,
        out_specs=out_specs,
        compiler_params=_params(("arbitrary",) if n_a else ("parallel",)),
    )(*[r[0] for r in rows], *consts)
    return res


def _sig(x):
    return jax.nn.sigmoid(x)


def _silu_and_grad(x):
    s = _sig(x)
    return x * s, s * (1.0 + x * (1.0 - s))


_GELU_C = math.sqrt(2.0 / math.pi)


def _gelu(x):
    return 0.5 * x * (1.0 + jnp.tanh(_GELU_C * (x + 0.044715 * x * x * x)))


def _gelu_grad(x):
    t = jnp.tanh(_GELU_C * (x + 0.044715 * x * x * x))
    return 0.5 * (1.0 + t) + 0.5 * x * (1.0 - t * t) * _GELU_C * (1.0 + 3.0 * 0.044715 * x * x)


def _rms(x):
    return lax.rsqrt(jnp.mean(x * x, axis=-1, keepdims=True) + EPS)


def _rms_bwd(x, g, dy):
    r = _rms(x)
    n = x * r
    dn = dy * g
    return r * (dn - n * jnp.mean(dn * n, axis=-1, keepdims=True)), dy * n


def _colsum(v):
    return jnp.sum(v, axis=0, keepdims=True)


def _rope_partner(x):
    lane = lax.broadcasted_iota(jnp.int32, x.shape, x.ndim - 1)
    return jnp.where(lane % 64 < 32, pltpu.roll(x, 96, x.ndim - 1), pltpu.roll(x, 32, x.ndim - 1))


def _rope_tables(L):
    rows_n = L // GRID_W
    rows = np.repeat(np.arange(rows_n), GRID_W).astype(np.float32)
    cols = np.tile(np.arange(GRID_W), rows_n).astype(np.float32)
    n_freq = HEAD_DIM // 4
    inv_freq = np.float32(ROPE_THETA) ** (-np.arange(n_freq, dtype=np.float32) / np.float32(n_freq))
    ar, ac = rows[:, None] * inv_freq[None, :], cols[:, None] * inv_freq[None, :]
    cos = np.concatenate([np.cos(ar), np.cos(ar), np.cos(ac), np.cos(ac)], axis=-1).astype(np.float32)
    sin = np.concatenate([-np.sin(ar), np.sin(ar), -np.sin(ac), np.sin(ac)], axis=-1).astype(np.float32)
    return jnp.asarray(cos), jnp.asarray(sin)


def _heads(v):
    return [v[:, h * HEAD_DIM:(h + 1) * HEAD_DIM] for h in range(v.shape[1] // HEAD_DIM)]


def _attn_prep(z, q_norm, k_norm, cos, sin):
    def fn(q, k, v, cos, sin, gq, gk):
        def one(xh, g):
            xn = xh * _rms(xh) * g
            return xn * cos + _rope_partner(xn) * sin
        qr = jnp.concatenate([one(h, gq) for h in _heads(q)], axis=1)
        kr = jnp.concatenate([one(h, gk) for h in _heads(k)], axis=1)
        return qr, kr, v
    return _rowwise(fn, [(z, Z_Q, D_ATTN), (z, Z_K, D_KV), (z, Z_V, D_KV), (cos, 0, HEAD_DIM), (sin, 0, HEAD_DIM)],
                    [q_norm, k_norm], [(D_ATTN, BF16), (D_KV, BF16), (D_KV, BF16)], name="attn_prep")


def _attn_prep_bwd(dqr, dkr, z, q_norm, k_norm, cos, sin):
    def fn(dqr, dkr, q, k, cos, sin, gq, gk):
        def one(dyh, xh, g):
            dn = dyh * cos + _rope_partner(dyh * sin)
            return _rms_bwd(xh, g, dn)
        rq = [one(a, b, gq) for a, b in zip(_heads(dqr), _heads(q))]
        rk = [one(a, b, gk) for a, b in zip(_heads(dkr), _heads(k))]
        dq = jnp.concatenate([r[0] for r in rq], axis=1)
        dk = jnp.concatenate([r[0] for r in rk], axis=1)
        return dq, dk, _colsum(sum(r[1] for r in rq)), _colsum(sum(r[1] for r in rk))
    return _rowwise(fn, [(dqr, 0, D_ATTN), (dkr, 0, D_KV), (z, Z_Q, D_ATTN), (z, Z_K, D_KV), (cos, 0, HEAD_DIM), (sin, 0, HEAD_DIM)],
                    [q_norm, k_norm], [(D_ATTN, BF16), (D_KV, BF16)], [(1, HEAD_DIM), (1, HEAD_DIM)], name="attn_prep_bwd")


_QK_T = (((1,), (1,)), ((), ()))
_TA = (((0,), (0,)), ((), ()))
_REP = N_HEADS // N_KV


_EXP2_SCALE = HEAD_DIM ** -0.5 * math.log2(math.e)
ATTN_FWD_KEY_CHUNKS = 4
ATTN_BWD_KEY_CHUNKS = 8


def _attn_fwd(qr, kr, vb, g_ts, g_shards, *, tq=1024):
    L = qr.shape[0]
    tq = min(tq, L)
    kc = L // ATTN_FWD_KEY_CHUNKS
    n_g = len(g_ts)
    grid = (N_HEADS, L // tq)
    steps = grid[0] * grid[1]

    def body(q_ref, k_ref, v_ref, *rest):
        o_ref, lse_ref = rest[n_g:n_g + 2]
        g = _Gather(g_ts, rest[:n_g], rest[n_g + 2:2 * n_g + 2], rest[2 * n_g + 2:3 * n_g + 2], rest[3 * n_g + 2:])
        step = pl.program_id(0) * grid[1] + pl.program_id(1)
        pl.when(step == 0)(g.start)
        pl.when(step == (3 * steps) // 4)(g.forward)
        q = q_ref[...]
        m = jnp.full((tq, 1), -jnp.inf, F32)
        l = jnp.zeros((tq, 1), F32)
        o = jnp.zeros((tq, HEAD_DIM), F32)
        for c in range(ATTN_FWD_KEY_CHUNKS):
            ks = slice(c * kc, (c + 1) * kc)
            s = lax.dot_general(q, k_ref[ks, :], _QK_T, preferred_element_type=F32)
            m_new = jnp.maximum(m, jnp.max(s, axis=1, keepdims=True))
            a = jnp.exp2((m - m_new) * _EXP2_SCALE)
            p = jnp.exp2((s - m_new) * _EXP2_SCALE)
            l = a * l + jnp.sum(p, axis=1, keepdims=True)
            o = a * o + jnp.dot(p.astype(BF16), v_ref[ks, :], preferred_element_type=F32)
            m = m_new
        o_ref[...] = o * (1.0 / l)
        lse_ref[...] = m * _EXP2_SCALE + jnp.log2(l)
        pl.when(step == steps - 1)(g.finish)

    kv = pl.BlockSpec((L, HEAD_DIM), lambda h, i: (0, h // _REP))
    return pl.pallas_call(
        body, name="attn_fwd",
        out_shape=[jax.ShapeDtypeStruct((L, D_ATTN), F32), jax.ShapeDtypeStruct((N_HEADS, L, 1), F32)] +
                  [jax.ShapeDtypeStruct(_BIG[t][1], BF16) for t in g_ts],
        grid=grid,
        in_specs=[pl.BlockSpec((tq, HEAD_DIM), lambda h, i: (i, h)), kv, kv] + [_ANY] * n_g,
        out_specs=[pl.BlockSpec((tq, HEAD_DIM), lambda h, i: (i, h)),
                   pl.BlockSpec((None, tq, 1), lambda h, i: (h, i, 0))] + [_ANY] * n_g,
        scratch_shapes=_Gather.scratch(g_shards),
        compiler_params=_params(("arbitrary", "arbitrary")),
    )(qr, kr, vb, *g_shards)


def _attn_bwd(qr, kr, k_t, vb, do, lse, delta, ex_ts, ex_sums, *, tq=1024):
    L = qr.shape[0]
    tq = min(tq, L)
    scale = HEAD_DIM ** -0.5
    kc = L // ATTN_BWD_KEY_CHUNKS
    n_ex = len(ex_ts)
    grid = (N_KV, _REP, L // tq)

    def body(q_ref, k_ref, kt_ref, v_ref, do_ref, lse_ref, delta_ref, *rest):
        dq_ref, dk_ref, dv_ref = rest[n_ex:n_ex + 3]
        ex = _ChipExchange(ex_ts, rest[:n_ex], rest[n_ex + 3:2 * n_ex + 3], rest[2 * n_ex + 3:])
        step = (pl.program_id(0) * grid[1] + pl.program_id(1)) * grid[2] + pl.program_id(2)
        pl.when(step == 0)(ex.start)

        @pl.when((pl.program_id(1) == 0) & (pl.program_id(2) == 0))
        def _():
            dk_ref[...] = jnp.zeros_like(dk_ref)
            dv_ref[...] = jnp.zeros_like(dv_ref)

        q, do, lse, delta = q_ref[...], do_ref[...], lse_ref[...], delta_ref[...]
        dq_t = 0.0
        for c in range(ATTN_BWD_KEY_CHUNKS):
            ks = slice(c * kc, (c + 1) * kc)
            st = lax.dot_general(k_ref[ks, :], q, _QK_T, preferred_element_type=F32)
            p = jnp.exp2(st * _EXP2_SCALE - lse)
            dv_ref[ks, :] += jnp.dot(p.astype(BF16), do, preferred_element_type=F32)
            dp = lax.dot_general(v_ref[ks, :], do, _QK_T, preferred_element_type=F32)
            ds = (p * (dp - delta) * scale).astype(BF16)
            dk_ref[ks, :] += jnp.dot(ds, q, preferred_element_type=F32)
            dq_t = dq_t + jnp.dot(kt_ref[:, ks], ds, preferred_element_type=F32)
        dq_ref[...] = dq_t.T
        pl.when(step == grid[0] * grid[1] * grid[2] - 1)(ex.finish)

    head = lambda g, r, i: (i, g * _REP + r)
    kv = pl.BlockSpec((L, HEAD_DIM), lambda g, r, i: (0, g))
    per_query = pl.BlockSpec((None, 1, tq), lambda g, r, i: (g * _REP + r, 0, i))
    return pl.pallas_call(
        body, name="attn_bwd",
        out_shape=[jax.ShapeDtypeStruct((L, D_ATTN), F32), jax.ShapeDtypeStruct((L, D_KV), F32), jax.ShapeDtypeStruct((L, D_KV), F32)] +
                  _ChipExchange.out_shape(ex_ts),
        grid=grid,
        in_specs=[pl.BlockSpec((tq, HEAD_DIM), head), kv,
                  pl.BlockSpec((HEAD_DIM, L), lambda g, r, i: (g, 0)), kv,
                  pl.BlockSpec((tq, HEAD_DIM), head), per_query, per_query] + [_ANY] * n_ex,
        out_specs=[pl.BlockSpec((tq, HEAD_DIM), head), kv, kv] + [_ANY] * n_ex,
        scratch_shapes=_ChipExchange.scratch(ex_ts),
        compiler_params=_params(("arbitrary", "arbitrary", "arbitrary")),
    )(qr, kr, k_t, vb, do, lse, delta, *ex_sums)


SSM_BLK = 8
SSM_NB = SSM_G // SSM_BLK
SSM_SEG = 8
SSM_UNROLL = 4


def _unrolled_loop(n, step, carry):
    u = SSM_UNROLL

    def trip(i, c):
        for j in range(u):
            c = step(i * u + j, c)
        return c
    carry = lax.fori_loop(0, n // u, trip, carry)
    for t in range(n - n % u, n):
        carry = step(jnp.int32(t), carry)
    return carry


def _cplx_pow2(a, b, n):
    for _ in range(int(math.log2(n))):
        a, b = a * a - b * b, 2.0 * a * b
    return a, b


def _seg_scan(ref, a, b, T, reverse, entry=None, tap=None):
    npair = len(a)
    zero = jnp.zeros((SSM_SEG, 128), F32)

    def make_step(store, tap_fn=None):
        def step(t, carry):
            lt = (T - 1 - t) if reverse else t
            row = pl.multiple_of(lt * SSM_SEG, SSM_SEG)
            blk = ref[pl.ds(row, SSM_SEG), :]
            new = []
            for q in range(npair):
                re, im = carry[2 * q], carry[2 * q + 1]
                nre = a[q] * re - b[q] * im + blk[:, q * 256:q * 256 + 128]
                nim = a[q] * im + b[q] * re + blk[:, q * 256 + 128:q * 256 + 256]
                new += [nre, nim]
            if store:
                ref[pl.ds(row, SSM_SEG), :] = jnp.concatenate(new, axis=1)
            extra = carry[2 * npair:]
            return tuple(new) + (tuple(tap_fn(lt, new, extra)) if tap_fn else tuple(extra))
        return step

    def second_pass(init):
        if tap is None:
            _unrolled_loop(T, make_step(True), tuple(init))
            return init
        carry = _unrolled_loop(T - 1, make_step(True, tap[0]), tuple(init) + tuple(tap[2]))
        carry = make_step(True, tap[1])(jnp.int32(T - 1), carry)
        return init, carry[2 * npair:]

    if entry is not None:
        return second_pass(entry)
    ends = _unrolled_loop(T, make_step(False), (zero,) * (2 * npair))
    sub = lax.broadcasted_iota(jnp.int32, (SSM_SEG, 128), 0)
    keep = (sub != SSM_SEG - 1) if reverse else (sub != 0)
    shift = (SSM_SEG - 1) if reverse else 1
    init = []
    for q in range(npair):
        pa, pb = _cplx_pow2(a[q], b[q], T)
        xr, xi = zero, zero
        for _ in range(SSM_SEG - 1):
            fr = ends[2 * q] + pa * xr - pb * xi
            fi = ends[2 * q + 1] + pa * xi + pb * xr
            xr = jnp.where(keep, pltpu.roll(fr, shift, 0), 0.0)
            xi = jnp.where(keep, pltpu.roll(fi, shift, 0), 0.0)
        init += [xr, xi]
    return second_pass(init)


SSM_BW = SSM_BLK * SSM_H
SSM_SW = SSM_BLK * 2 * SSM_P
SSM_NPAIR = SSM_BLK // 2
SSM_FWD_MATMUL_ROWS = 2048
SSM_BWD_MATMUL_ROWS = 2048


def _seg_perm(a):
    L, C = a.shape
    return a.reshape(SSM_SEG, L // SSM_SEG, C).transpose(1, 0, 2).reshape(L, C)


def _seg_unperm(a):
    L, C = a.shape
    return a.reshape(L // SSM_SEG, SSM_SEG, C).transpose(1, 0, 2).reshape(L, C)


def _lam_rows(are_ref, aim_ref, d):
    a = [jnp.broadcast_to(are_ref[d, j:j + 1, :], (SSM_SEG, 128)) for j in range(SSM_NPAIR)]
    b = [jnp.broadcast_to(aim_ref[d, j:j + 1, :], (SSM_SEG, 128)) for j in range(SSM_NPAIR)]
    return a, b


_PAIR_SPEC = pl.BlockSpec((None, 2, SSM_NPAIR, 2, 2 * SSM_H, 128), lambda g: (g, 0, 0, 0, 0, 0))


def _pair_window(j, r):
    return slice(j * 2 * SSM_H, (j + 1) * 2 * SSM_H), slice(j * 256 + r * 128, j * 256 + (r + 1) * 128)


def _expand_pairs(c_ref, dense_ref):
    dense_ref[...] = jnp.zeros_like(dense_ref)
    for d in range(2):
        for j in range(SSM_NPAIR):
            for r in range(2):
                rows, cols = _pair_window(j, r)
                dense_ref[d, rows, cols] = c_ref[d, j, r].astype(dense_ref.dtype)


def _ssm_fwd(u_p, wb, wc, are, aim, dvec):
    L = u_p.shape[0]
    T = L // SSM_SEG
    RC = min(SSM_FWD_MATMUL_ROWS, L)

    def body(u_ref, wb_ref, wc_ref, are_ref, aim_ref, d_ref, y_ref, ge_ref, entry_ref, x_scr, wb_s, wc_s):
        _expand_pairs(wb_ref, wb_s)
        _expand_pairs(wc_ref, wc_s)
        y_ref[...] = u_ref[...] * d_ref[...]
        for d in range(2):
            def bu_chunk(c, _):
                rows = pl.ds(pl.multiple_of(c * RC, RC), RC)
                x_scr[rows, :] = jnp.dot(u_ref[rows, :].astype(BF16), wb_s[d], preferred_element_type=F32)
                return 0
            lax.fori_loop(0, L // RC, bu_chunk, 0)
            a, b = _lam_rows(are_ref, aim_ref, d)
            entry_ref[d] = jnp.concatenate(_seg_scan(x_scr, a, b, T, reverse=(d == 1)), axis=1)

            def y_chunk(c, _):
                rows = pl.ds(pl.multiple_of(c * RC, RC), RC)
                y_ref[rows, :] += lax.dot_general(x_scr[rows, :].astype(BF16), wc_s[d], _QK_T, preferred_element_type=F32)
                return 0
            lax.fori_loop(0, L // RC, y_chunk, 0)
        ge_ref[...] = _gelu(y_ref[...]).astype(ge_ref.dtype)

    blk4 = lambda g: (g, 0, 0, 0)
    chan = pl.BlockSpec((L, SSM_BW), lambda g: (0, g))
    return pl.pallas_call(
        body, name="ssm_fwd",
        out_shape=[jax.ShapeDtypeStruct((L, D_SSM), F32), jax.ShapeDtypeStruct((L, D_SSM), BF16),
                   jax.ShapeDtypeStruct((SSM_NB, 2, SSM_SEG, SSM_SW), F32)],
        grid=(SSM_NB,),
        in_specs=[chan, _PAIR_SPEC, _PAIR_SPEC,
                  pl.BlockSpec((None, 2, SSM_NPAIR, 128), blk4),
                  pl.BlockSpec((None, 2, SSM_NPAIR, 128), blk4),
                  pl.BlockSpec((1, SSM_BW), lambda g: (0, g))],
        out_specs=[chan, chan, pl.BlockSpec((None, 2, SSM_SEG, SSM_SW), blk4)],
        scratch_shapes=[pltpu.VMEM((L, SSM_SW), F32), pltpu.VMEM((2, SSM_BW, SSM_SW), BF16), pltpu.VMEM((2, SSM_BW, SSM_SW), BF16)],
        compiler_params=_params(("parallel",)),
    )(u_p, wb, wc, are, aim, dvec)


def _ssm_bwd(u_p, dy_p, entry, wb, wc, are, aim, dvec, sib_ts, sib_grads):
    L = u_p.shape[0]
    T = L // SSM_SEG
    RC = min(SSM_BWD_MATMUL_ROWS, L)
    n_sib = len(sib_ts)

    def lam_acc(acc, s, x):
        new = []
        for q in range(SSM_NPAIR):
            sr, si, xr, xi = s[2 * q], s[2 * q + 1], x[2 * q], x[2 * q + 1]
            new += [acc[2 * q] + sr * xr + si * xi, acc[2 * q + 1] + si * xr - sr * xi]
        return tuple(new)

    def body(u_ref, dy_ref, entry_ref, wb_ref, wc_ref, are_ref, aim_ref, d_ref, *rest):
        du_ref, dwb_ref, dwc_ref, dare_ref, daim_ref, dd_ref = rest[n_sib:n_sib + 6]
        x_scr, s_scr, wb_s, wc_s, dwb_s, dwc_s = rest[2 * n_sib + 6:2 * n_sib + 12]
        sib = _SiblingExchange(sib_ts, rest[:n_sib], rest[n_sib + 6:2 * n_sib + 6], rest[2 * n_sib + 12:])
        pl.when(pl.program_id(0) == 0)(sib.start)
        _expand_pairs(wb_ref, wb_s)
        _expand_pairs(wc_ref, wc_s)
        du_ref[...] = dy_ref[...] * d_ref[...]
        dd_ref[...] = _colsum(dy_ref[...] * u_ref[...])
        dwb_s[...] = jnp.zeros_like(dwb_s)
        dwc_s[...] = jnp.zeros_like(dwc_s)
        for d in range(2):
            rev = d == 1

            def in_chunk(c, _):
                rows = pl.ds(pl.multiple_of(c * RC, RC), RC)
                x_scr[rows, :] = jnp.dot(u_ref[rows, :].astype(BF16), wb_s[d], preferred_element_type=F32)
                s_scr[rows, :] = jnp.dot(dy_ref[rows, :].astype(BF16), wc_s[d], preferred_element_type=F32)
                return 0
            lax.fori_loop(0, L // RC, in_chunk, 0)
            a, b = _lam_rows(are_ref, aim_ref, d)
            x_in = _seg_scan(x_scr, a, b, T, reverse=rev, entry=[entry_ref[d, :, q * 128:(q + 1) * 128] for q in range(2 * SSM_NPAIR)])

            def pair_with_row(lt, s_new, acc):
                xrow = pl.multiple_of((lt + 1 if rev else lt - 1) * SSM_SEG, SSM_SEG)
                xb = x_scr[pl.ds(xrow, SSM_SEG), :]
                return lam_acc(acc, s_new, [xb[:, i * 128:(i + 1) * 128] for i in range(2 * SSM_NPAIR)])

            zeros = (jnp.zeros((SSM_SEG, 128), F32),) * (2 * SSM_NPAIR)
            _, acc = _seg_scan(s_scr, a, [-v for v in b], T, reverse=not rev,
                               tap=(pair_with_row, lambda lt, s_new, acc: lam_acc(acc, s_new, x_in), zeros))
            for q in range(SSM_NPAIR):
                dare_ref[d, q:q + 1, :] = _colsum(acc[2 * q])
                daim_ref[d, q:q + 1, :] = _colsum(acc[2 * q + 1])

            def out_chunk(c, _):
                rows = pl.ds(pl.multiple_of(c * RC, RC), RC)
                xs, ss = x_scr[rows, :].astype(BF16), s_scr[rows, :].astype(BF16)
                uu, dd = u_ref[rows, :].astype(BF16), dy_ref[rows, :].astype(BF16)
                dwc_s[d] += lax.dot_general(dd, xs, _TA, preferred_element_type=F32)
                dwb_s[d] += lax.dot_general(uu, ss, _TA, preferred_element_type=F32)
                du_ref[rows, :] += lax.dot_general(ss, wb_s[d], _QK_T, preferred_element_type=F32)
                return 0
            lax.fori_loop(0, L // RC, out_chunk, 0)
            for j in range(SSM_NPAIR):
                for r in range(2):
                    rows, cols = _pair_window(j, r)
                    dwb_ref[d, j, r] = dwb_s[d, rows, cols]
                    dwc_ref[d, j, r] = dwc_s[d, rows, cols]
        pl.when(pl.program_id(0) == SSM_NB - 1)(sib.finish)

    blk4 = lambda g: (g, 0, 0, 0)
    chan = pl.BlockSpec((L, SSM_BW), lambda g: (0, g))
    par_specs = [_PAIR_SPEC, _PAIR_SPEC,
                 pl.BlockSpec((None, 2, SSM_NPAIR, 128), blk4),
                 pl.BlockSpec((None, 2, SSM_NPAIR, 128), blk4),
                 pl.BlockSpec((1, SSM_BW), lambda g: (0, g))]
    dense = lambda dt: pltpu.VMEM((2, SSM_BW, SSM_SW), dt)
    return pl.pallas_call(
        body, name="ssm_bwd",
        out_shape=[jax.ShapeDtypeStruct((L, D_SSM), F32),
                   jax.ShapeDtypeStruct(wb.shape, F32),
                   jax.ShapeDtypeStruct(wc.shape, F32),
                   jax.ShapeDtypeStruct((SSM_NB, 2, SSM_NPAIR, 128), F32),
                   jax.ShapeDtypeStruct((SSM_NB, 2, SSM_NPAIR, 128), F32),
                   jax.ShapeDtypeStruct((1, D_SSM), F32)] + _SiblingExchange.out_shape(sib_ts),
        grid=(SSM_NB,),
        in_specs=[chan, chan, pl.BlockSpec((None, 2, SSM_SEG, SSM_SW), blk4)] + par_specs + [_ANY] * n_sib,
        out_specs=[chan] + par_specs + [_ANY] * n_sib,
        scratch_shapes=[pltpu.VMEM((L, SSM_SW), F32), pltpu.VMEM((L, SSM_SW), F32), dense(BF16), dense(BF16), dense(F32), dense(F32)] +
                       _SiblingExchange.scratch(sib_ts),
        compiler_params=_params(("arbitrary",)),
    )(u_p, dy_p, entry, wb, wc, are, aim, dvec, *sib_grads)


def _ssm_disc(a_re, a_im, log_dt, b_re, b_im):
    lam = lax.complex(jnp.minimum(a_re, -1e-4), a_im)
    dt = jnp.exp(log_dt)[..., None]
    lam_bar = jnp.exp(lam * dt)
    b_bar = ((lam_bar - 1.0) / lam)[..., None] * lax.complex(b_re, b_im)
    return jnp.real(lam_bar), jnp.imag(lam_bar), jnp.real(b_bar), jnp.imag(b_bar)


_EYE2 = np.eye(2, dtype=np.float32)[:, None, :, None]


def _to_pairs(t):
    t = t.reshape(2, SSM_NB, SSM_NPAIR, 2, 2, SSM_H, SSM_P).transpose(1, 0, 2, 4, 3, 5, 6)
    return (t[..., None, :] * _EYE2).reshape(SSM_NB, 2, SSM_NPAIR, 2, 2 * SSM_H, 2 * SSM_P)


def _from_pairs(c):
    t = c.reshape(SSM_NB, 2, SSM_NPAIR, 2, 2, SSM_H, 2, SSM_P)
    t = jnp.stack([t[:, :, :, :, 0, :, 0, :], t[:, :, :, :, 1, :, 1, :]], axis=4)
    return t.transpose(1, 0, 2, 4, 3, 5, 6).reshape(2, SSM_G, 2, SSM_H, SSM_P)


def _to_lam(v):
    return v.reshape(2, SSM_NB, SSM_NPAIR, 128).transpose(1, 0, 2, 3)


def _from_lam(v):
    return v.transpose(1, 0, 2, 3).reshape(2, SSM_G, SSM_P)


_MESH = pl.DeviceIdType.MESH
_ANY = pl.BlockSpec(memory_space=pl.ANY)
_BIG = (("w_in", (D_MODEL, D_IN), 1, D_IN // N_CHIPS),
        ("w_glu", (D_SSM, 2 * D_SSM), 1, 2 * D_SSM // N_CHIPS),
        ("w_out", (D_ATTN + D_SSM, D_MODEL), 0, (D_ATTN + D_SSM) // N_CHIPS),
        ("w_ple_gate", (D_MODEL, D_MODEL), 0, D_MODEL // N_CHIPS),
        ("w_ple_proj", (PLE_DIM, D_MODEL), 1, D_MODEL // N_CHIPS))


def _place():
    x, y, c = lax.axis_index("x"), lax.axis_index("y"), lax.axis_index("c")
    return x, y, c, [(1 - x, y), (x, 1 - y), (1 - x, 1 - y)]


class _Gather:
    def __init__(self, ts, srcs, dsts, stage, sems):
        self.ts, self.srcs, self.dsts, self.stage = ts, srcs, dsts, stage
        self.send_sems, self.recv_sems, self.fwd_send_sems, self.fwd_recv_sems, self.loc_sems = sems
        self.x, self.y, self.c, self.chips = _place()
        self.n = len(ts)

    @staticmethod
    def scratch(shards):
        sems = pltpu.SemaphoreType.DMA((3, len(shards)))
        return [pltpu.VMEM(s.shape, BF16) for s in shards] + [sems, sems, sems, sems, pltpu.SemaphoreType.DMA((len(shards),))]

    def _shard_of(self, i, kk):
        _, _, axis, sz = _BIG[self.ts[i]]
        sl = pl.ds(pl.multiple_of(kk * sz, sz), sz)
        return self.dsts[i].at[:, sl] if axis == 1 else self.dsts[i].at[sl, :]

    @staticmethod
    def _half_of(ref, cc):
        n = ref.shape[0] // 2
        return ref.at[pl.ds(pl.multiple_of(cc * n, n), n), :]

    def _ici(self, j, i, kk):
        px, py = self.chips[j]
        return pltpu.make_async_remote_copy(
            src_ref=self._half_of(self.srcs[i], self.c), dst_ref=self._half_of(self._shard_of(i, kk), self.c),
            send_sem=self.send_sems.at[j, i], recv_sem=self.recv_sems.at[j, i],
            device_id=(px, py, self.c), device_id_type=_MESH)

    def _forward(self, j, i, kk, cc):
        part = self._half_of(self._shard_of(i, kk), cc)
        return pltpu.make_async_remote_copy(
            src_ref=part, dst_ref=part, send_sem=self.fwd_send_sems.at[j, i], recv_sem=self.fwd_recv_sems.at[j, i],
            device_id=(self.x, self.y, 1 - self.c), device_id_type=_MESH)

    def _load(self, i):
        return pltpu.make_async_copy(self.srcs[i], self.stage[i], self.loc_sems.at[i])

    def _place_own(self, i):
        return pltpu.make_async_copy(self.stage[i], self._shard_of(i, 2 * self.x + self.y), self.loc_sems.at[i])

    def _peers(self):
        return [(i, j, 2 * px + py) for i in range(self.n) for j, (px, py) in enumerate(self.chips)]

    def start(self, relations=(0, 1, 2)):
        for i in range(self.n):
            self._load(i).start()
        self.send(relations)

    def send(self, relations):
        for i, j, _ in self._peers():
            if j in relations:
                self._ici(j, i, 2 * self.x + self.y).start()

    def forward(self):
        for i in range(self.n):
            self._load(i).wait()
            self._place_own(i).start()
        for i, j, kk in self._peers():
            self._ici(j, i, kk).wait_recv()
            self._forward(j, i, kk, self.c).start()

    def finish(self):
        for i, j, kk in self._peers():
            self._forward(j, i, kk, 1 - self.c).wait_recv()
        for i, j, kk in self._peers():
            self._ici(j, i, kk).wait_send()
            self._forward(j, i, kk, self.c).wait_send()
        for i in range(self.n):
            self._place_own(i).wait()


def _matmul_in_gather(a, shard, *, tm=1024):
    t = 0
    (_, (K, N), _, sz) = _BIG[t]
    M = a.shape[0]
    tm = min(tm, M)
    gm = M // tm
    x, y = lax.axis_index("x"), lax.axis_index("y")
    order = jnp.stack([2 * x + y, 2 * (1 - x) + y, 2 * x + 1 - y, 2 * (1 - x) + 1 - y]).astype(jnp.int32)

    def body(order_ref, a_ref, shard_ref, z_ref, full_ref, b_vm, *sems):
        g = _Gather([t], [shard_ref], [full_ref], [b_vm], sems[:5])
        load_sem = sems[5]
        s, i = pl.program_id(0), pl.program_id(1)

        @pl.when((s == 0) & (i == 0))
        def _():
            g.start(relations=(0, 1))
            g._load(0).wait()
            g._place_own(0).start()

        for j, (px, py) in enumerate(g.chips):
            @pl.when((s == j + 1) & (i == 0))
            def _(j=j, kk=2 * px + py):
                if j == 0:
                    g._place_own(0).wait()
                g._ici(j, 0, kk).wait_recv()
                if j == 0:
                    g.send((2,))
                g._forward(j, 0, kk, g.c).start()
                g._forward(j, 0, kk, 1 - g.c).wait_recv()
                cp = pltpu.make_async_copy(g._shard_of(0, kk), b_vm, load_sem.at[0])
                cp.start()
                cp.wait()

        z_ref[...] = jnp.dot(a_ref[...], b_vm[...], preferred_element_type=F32)

        @pl.when((s == N_CHIPS - 1) & (i == gm - 1))
        def _():
            for j, (px, py) in enumerate(g.chips):
                g._ici(j, 0, 2 * px + py).wait_send()
                g._forward(j, 0, 2 * px + py, g.c).wait_send()

    return pl.pallas_call(
        body, name="mm_in",
        out_shape=[jax.ShapeDtypeStruct((M, N), F32), jax.ShapeDtypeStruct((K, N), BF16)],
        grid_spec=pltpu.PrefetchScalarGridSpec(
            num_scalar_prefetch=1, grid=(N_CHIPS, gm),
            in_specs=[pl.BlockSpec((tm, K), lambda s, i, o: (i, 0)), _ANY],
            out_specs=[pl.BlockSpec((tm, sz), lambda s, i, o: (i, o[s])), _ANY],
            scratch_shapes=_Gather.scratch([shard]) + [pltpu.SemaphoreType.DMA((1,))]),
        compiler_params=_params(("arbitrary", "arbitrary")),
    )(order, a, shard)


SMALL_W = 1024
SMALL_ROWS = 72
N_SMALL = 8 * SMALL_ROWS * SMALL_W
_RED = tuple((shape, ax, (shape[0] // 2, sz) if ax == 1 else (sz // 2, shape[1]), BF16) for _, shape, ax, sz in _BIG) + \
    (((8 * SMALL_ROWS, SMALL_W), 0, (SMALL_ROWS, SMALL_W), F32),)
_RED_TR = 128


def _piece(ref, t, kk, cc):
    _, ax, (pr, pc), _ = _RED[t]
    if ax == 1:
        return ref.at[pl.ds(pl.multiple_of(cc * pr, pr), pr), pl.ds(pl.multiple_of(kk * pc, pc), pc)]
    return ref.at[pl.ds(pl.multiple_of((2 * kk + cc) * pr, pr), pr), :]


def _half_shape(t):
    shape, ax, (pr, pc), _ = _RED[t]
    return (pr, shape[1]) if ax == 1 else (N_CHIPS * pr, pc)


def _piece_in_half(ref, t, kk):
    _, ax, (pr, pc), _ = _RED[t]
    return ref.at[:, pl.ds(pl.multiple_of(kk * pc, pc), pc)] if ax == 1 else ref.at[pl.ds(pl.multiple_of(kk * pr, pr), pr), :]


class _SiblingExchange:
    def __init__(self, ts, srcs, dsts, sems):
        send_sems, recv_sems = sems
        x, y, c, _ = _place()

        def copies():
            pairs = []
            for i, t in enumerate(ts):
                _, ax, (pr, _), _ = _RED[t]
                if ax == 1:
                    pairs.append((srcs[i].at[pl.ds(pl.multiple_of((1 - c) * pr, pr), pr), :], dsts[i]))
                else:
                    pairs += [(_piece(srcs[i], t, kk, 1 - c), _piece_in_half(dsts[i], t, kk)) for kk in range(N_CHIPS)]
            return [pltpu.make_async_remote_copy(src_ref=s, dst_ref=d, send_sem=send_sems.at[i], recv_sem=recv_sems.at[i],
                                                 device_id=(x, y, 1 - c), device_id_type=_MESH) for i, (s, d) in enumerate(pairs)]
        self.copies = copies

    @staticmethod
    def scratch(ts):
        n_dma = sum(1 if _RED[t][1] == 1 else N_CHIPS for t in ts)
        return [pltpu.SemaphoreType.DMA((n_dma,)), pltpu.SemaphoreType.DMA((n_dma,))]

    @staticmethod
    def out_shape(ts):
        return [jax.ShapeDtypeStruct(_half_shape(t), F32) for t in ts]

    def start(self):
        for cp in self.copies():
            cp.start()

    def finish(self):
        for cp in self.copies():
            cp.wait()


def _grad_sibling_exchange(ts, grads, name):
    n = len(ts)

    def body(*refs):
        ex = _SiblingExchange(ts, refs[:n], refs[n:2 * n], refs[2 * n:])
        ex.start()
        ex.finish()

    return pl.pallas_call(
        body, name=name,
        out_shape=_SiblingExchange.out_shape(ts),
        in_specs=[_ANY] * n, out_specs=[_ANY] * n,
        scratch_shapes=_SiblingExchange.scratch(ts),
    )(*grads)


def _chip_sum(t, g, rs, place):
    shape, ax, (pr, pc), dt = _RED[t]
    W = shape[1]
    tr = min(pr, _RED_TR)
    nb = pr // tr

    def body(place_ref, g_ref, rs_ref, o_ref):
        o_ref[...] = (g_ref[...] + rs_ref[...]).astype(o_ref.dtype)

    return pl.pallas_call(
        body, name="grad_chip_sum_%d" % t,
        out_shape=jax.ShapeDtypeStruct(rs.shape, dt),
        grid_spec=pltpu.PrefetchScalarGridSpec(
            num_scalar_prefetch=1, grid=(1 if ax == 1 else N_CHIPS, nb),
            in_specs=[pl.BlockSpec((tr, W), lambda kk, i, pr_: ((2 * kk + pr_[0]) * nb + i, 0)),
                      pl.BlockSpec((tr, W), lambda kk, i, pr_: (kk * nb + i, 0))],
            out_specs=pl.BlockSpec((tr, W), lambda kk, i, pr_: (kk * nb + i, 0))),
        compiler_params=_params(("parallel", "parallel")),
    )(place, g, rs)


class _ChipExchange:
    def __init__(self, ts, srcs, dsts, sems):
        self.send_sems, self.recv_sems = sems
        x, y, c, chips = _place()
        self.copies = lambda: [
            pltpu.make_async_remote_copy(src_ref=_piece_in_half(srcs[i], t, 2 * px + py), dst_ref=dsts[i].at[j],
                                         send_sem=self.send_sems.at[j, i], recv_sem=self.recv_sems.at[j, i],
                                         device_id=(px, py, c), device_id_type=_MESH)
            for i, t in enumerate(ts) for j, (px, py) in enumerate(chips)]

    @staticmethod
    def scratch(ts):
        return [pltpu.SemaphoreType.DMA((3, len(ts))), pltpu.SemaphoreType.DMA((3, len(ts)))]

    @staticmethod
    def out_shape(ts):
        return [jax.ShapeDtypeStruct((3,) + _RED[t][2], _RED[t][3]) for t in ts]

    def start(self):
        for cp in self.copies():
            cp.start()

    def finish(self):
        for cp in self.copies():
            cp.wait()


def _total_sum(t, g, rs, rc, place):
    shape, ax, (pr, pc), _ = _RED[t]
    tr = min(pr, _RED_TR)
    nb = pr // tr
    small = t == len(_RED) - 1

    def body(place_ref, g_ref, rs_ref, rc_ref, o_ref):
        o_ref[...] = (g_ref[...] + rs_ref[...]) + rc_ref[0].astype(F32) + rc_ref[1].astype(F32) + rc_ref[2].astype(F32)

    if ax == 1:
        g_map = lambda i, pr_: (pr_[0] * nb + i, pr_[1])
        rs_map = lambda i, pr_: (i, pr_[1])
    else:
        g_map = lambda i, pr_: ((2 * pr_[1] + pr_[0]) * nb + i, 0)
        rs_map = lambda i, pr_: (pr_[1] * nb + i, 0)
    o_map = (lambda i, pr_: ((2 * pr_[1] + pr_[0]) * nb + i, 0)) if small else (lambda i, pr_: (pr_[0] * nb + i, 0))
    return pl.pallas_call(
        body, name="grad_total_sum_%d" % t,
        out_shape=jax.ShapeDtypeStruct(((8 if small else 2) * pr, pc), F32),
        grid_spec=pltpu.PrefetchScalarGridSpec(
            num_scalar_prefetch=1, grid=(nb,),
            in_specs=[pl.BlockSpec((tr, pc), g_map), pl.BlockSpec((tr, pc), rs_map),
                      pl.BlockSpec((3, tr, pc), lambda i, pr_: (0, i, 0))],
            out_specs=pl.BlockSpec((tr, pc), o_map)),
        compiler_params=_params(("parallel",)),
    )(place, g, rs, rc)


def _small_direct_reduce(flat, place):
    def exchange(src_ref, dst_ref, send_sems, recv_sems):
        x, y, c, chips = _place()
        others = [(x, y, 1 - c)] + [(px, py, cc) for (px, py) in chips for cc in (c, 1 - c)]
        cps = [pltpu.make_async_remote_copy(
            src_ref=src_ref.at[pl.ds(pl.multiple_of((4 * px + 2 * py + pc) * SMALL_ROWS, SMALL_ROWS), SMALL_ROWS), :],
            dst_ref=dst_ref.at[i], send_sem=send_sems.at[i], recv_sem=recv_sems.at[i],
            device_id=(px, py, pc), device_id_type=_MESH) for i, (px, py, pc) in enumerate(others)]
        for cp in cps:
            cp.start()
        for cp in cps:
            cp.wait()

    received = pl.pallas_call(
        exchange, name="grad_small_exchange",
        out_shape=jax.ShapeDtypeStruct((7, SMALL_ROWS, SMALL_W), F32),
        in_specs=[_ANY], out_specs=_ANY,
        scratch_shapes=[pltpu.SemaphoreType.DMA((7,)), pltpu.SemaphoreType.DMA((7,))],
    )(flat)

    def add(place_ref, g_ref, r_ref, o_ref):
        t = g_ref[...]
        for i in range(7):
            t = t + r_ref[i]
        o_ref[...] = t

    mine = lambda i, pr_: (2 * pr_[1] + pr_[0], 0)
    return pl.pallas_call(
        add, name="grad_small_sum",
        out_shape=jax.ShapeDtypeStruct(flat.shape, F32),
        grid_spec=pltpu.PrefetchScalarGridSpec(
            num_scalar_prefetch=1, grid=(1,),
            in_specs=[pl.BlockSpec((SMALL_ROWS, SMALL_W), mine), pl.BlockSpec((7, SMALL_ROWS, SMALL_W), lambda i, pr_: (0, 0, 0))],
            out_specs=pl.BlockSpec((SMALL_ROWS, SMALL_W), mine)),
        compiler_params=_params(("arbitrary",)),
    )(place, flat, received)


def _grad_final_exchange(totals):
    n = len(_RED)
    nb = n - 1

    def body(*refs):
        srcs, dsts, (send_sems, recv_sems) = refs[:n], refs[n:2 * n], refs[2 * n:]
        x, y, c, chips = _place()
        me = 4 * x + 2 * y + c
        others = [(x, y, 1 - c)] + [(px, py, cc) for (px, py) in chips for cc in (c, 1 - c)]

        def half(ref, t, cc):
            pr = _RED[t][2][0]
            return ref.at[pl.ds(pl.multiple_of(cc * pr, pr), pr), :]

        def eighth(ref, dev):
            return ref.at[pl.ds(pl.multiple_of(dev * SMALL_ROWS, SMALL_ROWS), SMALL_ROWS), :]

        def big_copy(t, cc):
            return pltpu.make_async_remote_copy(src_ref=half(srcs[t], t, cc), dst_ref=half(dsts[t], t, cc), send_sem=send_sems.at[t],
                                                recv_sem=recv_sems.at[t], device_id=others[0], device_id_type=_MESH)

        def small_copy(i, dev):
            return pltpu.make_async_remote_copy(src_ref=eighth(srcs[nb], dev), dst_ref=eighth(dsts[nb], dev),
                                                send_sem=send_sems.at[nb + i], recv_sem=recv_sems.at[nb + i],
                                                device_id=others[i], device_id_type=_MESH)

        sends = [big_copy(t, c) for t in range(nb)] + [small_copy(i, me) for i in range(7)]
        for cp in sends:
            cp.start()
        for t in range(nb):
            big_copy(t, 1 - c).wait_recv()
        for i, (px, py, pc) in enumerate(others):
            small_copy(i, 4 * px + 2 * py + pc).wait_recv()
        for cp in sends:
            cp.wait_send()

    return pl.pallas_call(
        body, name="grad_final_exchange",
        out_shape=[jax.ShapeDtypeStruct(a.shape, F32) for a in totals],
        in_specs=[_ANY] * n, out_specs=[_ANY] * n,
        input_output_aliases={t: t for t in range(n)},
        scratch_shapes=[pltpu.SemaphoreType.DMA((nb + 7,)), pltpu.SemaphoreType.DMA((nb + 7,))],
    )(*totals)


def _grad_place():
    return jnp.stack([lax.axis_index("c"), 2 * lax.axis_index("x") + lax.axis_index("y")]).astype(jnp.int32)


def _reduce_begin(ts, grads, place, tag, from_sibling=None):
    if from_sibling is None:
        from_sibling = _grad_sibling_exchange(ts, grads, "grad_sibling_exchange_" + tag)
    return from_sibling, [_chip_sum(t, g, r, place) for t, g, r in zip(ts, grads, from_sibling)]


def _reduce_end(ts, grads, from_sibling, from_chips, place):
    return [_total_sum(t, g, r, q, place) for t, g, r, q in zip(ts, grads, from_sibling, from_chips)]


_EARLY = (1, 2, 3, 4)
_W_IN = (0,)


def _adamw_math(w, g, m, v):
    m = ADAM_B1 * m + (1.0 - ADAM_B1) * g
    v = ADAM_B2 * v + (1.0 - ADAM_B2) * (g * g)
    m_hat = m / (1.0 - ADAM_B1 ** ADAM_STEP)
    v_hat = v / (1.0 - ADAM_B2 ** ADAM_STEP)
    return -ADAM_LR * (m_hat / (jnp.sqrt(v_hat) + ADAM_EPS) + ADAM_WD * w), m, v


def _adamw(w, g, m, v, name):
    W = w.shape[1]
    return _rowwise(_adamw_math, [(a, 0, W) for a in (w, g, m, v)], [], [(W, F32)] * 3, tr=128, name=name)


def _adamw_whole(ws, gs, ms, vs, name):
    n = len(ws)

    def body(*refs):
        ins, outs = refs[:4 * n], refs[4 * n:]
        for i in range(n):
            res = _adamw_math(*[ins[j * n + i][...] for j in range(4)])
            for j in range(3):
                outs[j * n + i][...] = res[j]

    res = pl.pallas_call(
        body, name=name,
        out_shape=[jax.ShapeDtypeStruct(a.shape, F32) for a in ws] * 3,
        compiler_params=pltpu.CompilerParams(vmem_limit_bytes=VMEM_LIMIT_V7X),
    )(*ws, *gs, *ms, *vs)
    return res[:n], res[n:2 * n], res[2 * n:]


def _chunks(arr, off, width, w=512):
    return [(arr, off + i * w, w) for i in range(width // w)]


def _cat(vs):
    return jnp.concatenate(vs, axis=1)


def _forward_backward(x, p_b, tgt, shards, small):
    L = x.shape[0]
    row = lambda v: v.reshape(1, -1)
    g_mix, g_ple, g_fin = row(small["norm_mix"]), row(small["norm_ple"]), row(small["norm_final"])
    gq, gk, b_glu = row(small["q_norm"]), row(small["k_norm"]), row(small["b_glu"])
    cos, sin = _rope_tables(L)

    hn_b, = _rowwise(lambda x, g: x * _rms(x) * g, [(x, 0, D_MODEL)], [g_mix], [(D_MODEL, BF16)], name="norm_mix")
    z, w_in = _matmul_in_gather(hn_b, shards[0])
    qr, kr, vb = _attn_prep(z, gq, gk, cos, sin)
    o, lse, w_glu, w_out, w_pg, w_pp = _attn_fwd(qr, kr, vb, [1, 2, 3, 4], shards[1:])

    ssm_names = ("ssm_a_re", "ssm_a_im", "ssm_log_dt", "ssm_b_re", "ssm_b_im")
    (lre, lim, bre, bim), disc_vjp = jax.vjp(_ssm_disc, *[small[n][0] for n in ssm_names])
    ssm = (_to_pairs(jnp.swapaxes(jnp.stack([bre, bim], axis=2), -1, -2)),
           _to_pairs(jnp.stack([small["ssm_c_re"][0], -small["ssm_c_im"][0]], axis=2)),
           _to_lam(lre), _to_lam(lim), row(small["ssm_d"]))
    u_p = _seg_perm(z[:, Z_U:Z_U + D_SSM])
    y_p, ge_p, ssm_entry = _ssm_fwd(u_p, *ssm)
    ge_b = _seg_unperm(ge_p)
    glu = _matmul(ge_b, w_glu, name="mm_glu")

    def merge(ga0, ga1, a, b, gs0, gs1, o, bias):
        sa, _ = _silu_and_grad(_cat([ga0, ga1]))
        ss, _ = _silu_and_grad(_cat([gs0, gs1]))
        y2 = (a + bias[:, :D_SSM]) * _sig(b + bias[:, D_SSM:])
        return _cat([o * sa, y2 * ss])
    merge_rows = _chunks(z, Z_GA, D_ATTN) + [(glu, 0, D_SSM), (glu, D_SSM, D_SSM)] + _chunks(z, Z_GS, D_SSM) + [(o, 0, D_ATTN)]
    cat_b, = _rowwise(merge, merge_rows, [b_glu], [(D_MODEL, BF16)], name="merge")
    t_out = _matmul(cat_b, w_out, name="mm_out")

    def resid(x, t, g):
        h1 = x + t
        return h1, h1 * _rms(h1) * g
    h1, hp_b = _rowwise(resid, [(x, 0, D_MODEL), (t_out, 0, D_MODEL)], [g_ple], [(D_MODEL, F32), (D_MODEL, BF16)], name="resid_norm")
    gl = _matmul(hp_b, w_pg, name="mm_ple_gate")
    pp = _matmul(p_b, w_pp, name="mm_ple_proj")

    def head(h1, gl, pp, tgt, g):
        gate = _sig(gl)
        h2 = h1 + gate * pp
        r = _rms(h2)
        n = h2 * r
        err = n * g - tgt
        dy = err * (1.0 / D_MODEL)
        dn = dy * g
        dh2 = r * (dn - n * jnp.mean(dn * n, axis=-1, keepdims=True))
        dgate = dh2 * pp
        return dh2, dh2 * gate, dgate * gate * (1.0 - gate), _colsum(dy * n), _colsum(0.5 * err * err * (1.0 / D_MODEL))
    dh2, dpp_b, dgl_b, dg_fin, loss_cols = _rowwise(
        head, [(a, 0, D_MODEL) for a in (h1, gl, pp, tgt)], [g_fin],
        [(D_MODEL, F32), (D_MODEL, BF16), (D_MODEL, BF16)], [(1, D_MODEL), (1, D_MODEL)], name="loss_head")

    dw_pp = _matmul(p_b, dpp_b, ta=True, name="mm_d_w_ple_proj")
    dw_pg = _matmul(hp_b, dgl_b, ta=True, name="mm_d_w_ple_gate")
    dhp = _matmul(dgl_b, w_pg, tb=True, name="mm_d_hp")

    def resid_bwd(dhp, h1, dh2, g):
        dx, dg = _rms_bwd(h1, g, dhp)
        dh1 = dh2 + dx
        return dh1, dh1, _colsum(dg)
    dh1, dh1_b, dg_ple = _rowwise(resid_bwd, [(a, 0, D_MODEL) for a in (dhp, h1, dh2)], [g_ple],
                                  [(D_MODEL, F32), (D_MODEL, BF16)], [(1, D_MODEL)], name="resid_norm_bwd")
    dw_out = _matmul(cat_b, dh1_b, ta=True, name="mm_d_w_out")
    dcat = _matmul(dh1_b, w_out, tb=True, name="mm_d_cat")

    def merge_bwd(dya, dys, ga0, ga1, a, b, gs0, gs1, o, bias):
        ga, gs = _cat([ga0, ga1]), _cat([gs0, gs1])
        sa, dsa = _silu_and_grad(ga)
        ss, dss = _silu_and_grad(gs)
        a, sb = a + bias[:, :D_SSM], _sig(b + bias[:, D_SSM:])
        dy2 = dys * ss
        dglu = _cat([dy2 * sb, dy2 * a * sb * (1.0 - sb)])
        do = dya * sa
        lane = lax.broadcasted_iota(jnp.int32, (do.shape[0], HEAD_DIM), 1)
        delta = sum(jnp.where(lane == i, jnp.sum(h, axis=1, keepdims=True), 0.0)
                    for i, h in enumerate(_heads(do * o)))
        return do, dya * o * dsa, dys * (a * sb) * dss, dglu, delta, _colsum(dglu)
    do_b, dga_b, dgs_b, dglu_b, delta, db_glu = _rowwise(
        merge_bwd, [(dcat, 0, D_ATTN), (dcat, D_ATTN, D_SSM)] + merge_rows, [b_glu],
        [(D_ATTN, BF16), (D_ATTN, BF16), (D_SSM, BF16), (2 * D_SSM, BF16), (HEAD_DIM, F32)], [(1, 2 * D_SSM)], name="merge_bwd")
    dw_glu = _matmul(ge_b, dglu_b, ta=True, name="mm_d_w_glu")
    dge = _matmul(dglu_b, w_glu, tb=True, name="mm_d_ge")
    place = _grad_place()
    early_grads = [dw_glu, dw_out, dw_pg, dw_pp]
    dy_p, = _rowwise(lambda dge, y: dge * _gelu_grad(y), [(_seg_perm(dge), 0, D_SSM), (y_p, 0, D_SSM)], [], [(D_SSM, F32)],
                     name="gelu_bwd")
    du_p, dwb, dwc, dare, daim, d_ssm_d, *early_sib = _ssm_bwd(u_p, dy_p, ssm_entry, *ssm, _EARLY, early_grads)
    dcc, dbb = _from_pairs(dwc), jnp.swapaxes(_from_pairs(dwb), -1, -2)
    dc_re, dc_im = dcc[:, :, 0], -dcc[:, :, 1]
    da_re, da_im, dlog_dt, db_re, db_im = disc_vjp((_from_lam(dare), _from_lam(daim), dbb[:, :, 0], dbb[:, :, 1]))

    early_sib, early_sums = _reduce_begin(_EARLY, early_grads, place, "early", from_sibling=early_sib)
    dqr, dkr, dv, *early_chips = _attn_bwd(qr, kr, kr.T, vb, do_b, lse.reshape(N_HEADS, 1, L), delta[:, :N_HEADS].T.reshape(N_HEADS, 1, L),
                                           _EARLY, early_sums)
    early_totals = _reduce_end(_EARLY, early_grads, early_sib, early_chips, place)
    dq_b, dk_b, dgq, dgk = _attn_prep_bwd(dqr, dkr, z, gq, gk, cos, sin)
    dz_b = _cat([dq_b, dk_b, dv.astype(BF16), dga_b, _seg_unperm(du_p).astype(BF16), dgs_b])
    dw_in = _matmul(hn_b, dz_b, ta=True, name="mm_d_w_in")
    w_in_sib, w_in_sums = _reduce_begin(_W_IN, [dw_in], place, "w_in")
    dhn, *w_in_chips = _matmul(dz_b, w_in, tb=True, name="mm_d_hn", exchange=(_W_IN, w_in_sums))
    w_in_total, = _reduce_end(_W_IN, [dw_in], w_in_sib, w_in_chips, place)

    def norm_bwd(dhn, x, dh1, g):
        dx, dg = _rms_bwd(x, g, dhn)
        return dh1 + dx, _colsum(dg)
    grad_x, dg_mix = _rowwise(norm_bwd, [(a, 0, D_MODEL) for a in (dhn, x, dh1)], [g_mix], [(D_MODEL, F32)], [(1, D_MODEL)],
                              name="norm_mix_bwd")

    small_grads = {"norm_mix": dg_mix, "q_norm": dgq, "k_norm": dgk, "ssm_a_re": da_re, "ssm_a_im": da_im, "ssm_log_dt": dlog_dt,
                   "ssm_b_re": db_re, "ssm_b_im": db_im, "ssm_c_re": dc_re, "ssm_c_im": dc_im, "ssm_d": d_ssm_d,
                   "b_glu": db_glu, "norm_ple": dg_ple, "norm_final": dg_fin}
    return jnp.sum(loss_cols), grad_x, [w_in_total] + early_totals, small_grads, place


_SMALL = ("norm_mix", "q_norm", "k_norm", "ssm_a_re", "ssm_a_im", "ssm_log_dt", "ssm_b_re", "ssm_b_im", "ssm_c_re", "ssm_c_im",
          "ssm_d", "b_glu", "norm_ple", "norm_final")
_WEIGHTS = ("norm_mix", "w_in", "q_norm", "k_norm", "ssm_a_re", "ssm_a_im", "ssm_log_dt", "ssm_b_re", "ssm_b_im", "ssm_c_re",
            "ssm_c_im", "ssm_d", "w_glu", "b_glu", "w_out", "norm_ple", "w_ple_gate", "w_ple_proj", "norm_final")


_SMALL_ADAMW_GROUPS = (("ssm_b_re",), ("ssm_b_im",), ("ssm_c_re", "ssm_c_im"),
                       ("norm_mix", "q_norm", "k_norm", "ssm_a_re", "ssm_a_im", "ssm_log_dt", "ssm_d", "b_glu", "norm_ple",
                        "norm_final"))


def _flat_small(d):
    flat = jnp.concatenate([d[n].reshape(-1).astype(F32) for n in _SMALL])
    return jnp.pad(flat, (0, N_SMALL - flat.shape[0]))


def _split_small(flat, like):
    out, off = {}, 0
    for n in _SMALL:
        sz = math.prod(like[n].shape)
        out[n] = flat[off:off + sz].reshape(like[n].shape)
        off += sz
    return out


def kernel(x, p, norm_mix, w_in, q_norm, k_norm, ssm_a_re, ssm_a_im, ssm_log_dt, ssm_b_re, ssm_b_im, ssm_c_re, ssm_c_im, ssm_d, w_glu, b_glu, w_out, norm_ple, w_ple_gate, w_ple_proj, norm_final, loss_target, m_norm_mix, m_w_in, m_q_norm, m_k_norm, m_ssm_a_re, m_ssm_a_im, m_ssm_log_dt, m_ssm_b_re, m_ssm_b_im, m_ssm_c_re, m_ssm_c_im, m_ssm_d, m_w_glu, m_b_glu, m_w_out, m_norm_ple, m_w_ple_gate, m_w_ple_proj, m_norm_final, v_norm_mix, v_w_in, v_q_norm, v_k_norm, v_ssm_a_re, v_ssm_a_im, v_ssm_log_dt, v_ssm_b_re, v_ssm_b_im, v_ssm_c_re, v_ssm_c_im, v_ssm_d, v_w_glu, v_b_glu, v_w_out, v_norm_ple, v_w_ple_gate, v_w_ple_proj, v_norm_final):
    w = dict(norm_mix=norm_mix, w_in=w_in, q_norm=q_norm, k_norm=k_norm, ssm_a_re=ssm_a_re, ssm_a_im=ssm_a_im,
             ssm_log_dt=ssm_log_dt, ssm_b_re=ssm_b_re, ssm_b_im=ssm_b_im, ssm_c_re=ssm_c_re, ssm_c_im=ssm_c_im, ssm_d=ssm_d,
             w_glu=w_glu, b_glu=b_glu, w_out=w_out, norm_ple=norm_ple, w_ple_gate=w_ple_gate, w_ple_proj=w_ple_proj,
             norm_final=norm_final)
    m = dict(norm_mix=m_norm_mix, w_in=m_w_in, q_norm=m_q_norm, k_norm=m_k_norm, ssm_a_re=m_ssm_a_re, ssm_a_im=m_ssm_a_im,
             ssm_log_dt=m_ssm_log_dt, ssm_b_re=m_ssm_b_re, ssm_b_im=m_ssm_b_im, ssm_c_re=m_ssm_c_re, ssm_c_im=m_ssm_c_im,
             ssm_d=m_ssm_d, w_glu=m_w_glu, b_glu=m_b_glu, w_out=m_w_out, norm_ple=m_norm_ple, w_ple_gate=m_w_ple_gate,
             w_ple_proj=m_w_ple_proj, norm_final=m_norm_final)
    v = dict(norm_mix=v_norm_mix, w_in=v_w_in, q_norm=v_q_norm, k_norm=v_k_norm, ssm_a_re=v_ssm_a_re, ssm_a_im=v_ssm_a_im,
             ssm_log_dt=v_ssm_log_dt, ssm_b_re=v_ssm_b_re, ssm_b_im=v_ssm_b_im, ssm_c_re=v_ssm_c_re, ssm_c_im=v_ssm_c_im,
             ssm_d=v_ssm_d, w_glu=v_w_glu, b_glu=v_b_glu, w_out=v_w_out, norm_ple=v_norm_ple, w_ple_gate=v_w_ple_gate,
             w_ple_proj=v_w_ple_proj, norm_final=v_norm_final)
    big_names = [n for n, _, _, _ in _BIG]

    small = {n: w[n] for n in _SMALL}
    loss_part, grad_x, big_totals, small_grads, place = _forward_backward(
        x[0], p[0, 0].astype(BF16), loss_target[0], [w[n][0].astype(BF16) for n in big_names], small)
    loss = lax.psum(loss_part, ("x", "y", "c"))

    small_total = _small_direct_reduce(_flat_small(small_grads).reshape(8 * SMALL_ROWS, SMALL_W), place)
    *big_red, small_red = _grad_final_exchange(big_totals + [small_total])
    grads = _split_small(small_red.reshape(-1), w)
    delta, new_m, new_v = {}, {}, {}
    for n, g in zip(big_names, big_red):
        grads[n] = g[None]
        d_, m_, v_ = _adamw(w[n][0], g, m[n][0], v[n][0], "adamw_" + n)
        delta[n], new_m[n], new_v[n] = d_[None], m_[None], v_[None]
    at_least_2d = lambda a: a.reshape(1, -1) if a.ndim == 1 else a
    for i, names in enumerate(_SMALL_ADAMW_GROUPS):
        d_, m_, v_ = _adamw_whole(*[[at_least_2d(src[n]) for n in names] for src in (w, grads, m, v)], "adamw_small_%d" % i)
        for j, n in enumerate(names):
            delta[n], new_m[n], new_v[n] = (a[j].reshape(w[n].shape) for a in (d_, m_, v_))
    return (loss, grad_x[None], *[grads[n] for n in _WEIGHTS], *[delta[n] for n in _WEIGHTS],
            *[new_m[n] for n in _WEIGHTS], *[new_v[n] for n in _WEIGHTS])
```

```python
import functools
import math

import jax
import jax.numpy as jnp
import numpy as np
from jax import lax
from jax.experimental import pallas as pl
from jax.experimental.pallas import tpu as pltpu

D_MODEL = 2048
GRID_W = 64
PLE_DIM = 256
D_ATTN = 1024
N_HEADS = 8
N_KV = 2
HEAD_DIM = 128
ROPE_THETA = 10000.0
D_SSM = 1024
SSM_H = 16
SSM_G = 64
SSM_P = 64
D_KV = N_KV * HEAD_DIM
D_IN = 2 * D_ATTN + 2 * D_KV + 2 * D_SSM
EPS = 1e-6
Z_Q, Z_K, Z_V, Z_GA, Z_U, Z_GS = 0, 1024, 1280, 1536, 2560, 3584

ADAM_LR, ADAM_B1, ADAM_B2, ADAM_EPS, ADAM_WD, ADAM_STEP = 0.001, 0.9, 0.999, 1e-08, 0.01, 10

N_CHIPS = 4
VMEM_LIMIT_V7X = 56 * 1024 * 1024
F32 = jnp.float32
BF16 = jnp.bfloat16


def _params(sem, vmem=VMEM_LIMIT_V7X):
    return pltpu.CompilerParams(dimension_semantics=sem, vmem_limit_bytes=vmem)


def _matmul(a, b, *, ta=False, tb=False, out_dtype=F32, tm=1024, tn=None, name, exchange=None, add=None):
    M, K = (a.shape[1], a.shape[0]) if ta else a.shape
    N = b.shape[0] if tb else b.shape[1]
    if tn is None:
        tn = 1024 if (N % 1024 == 0 and K <= 4096) else 512
    tm, tn = min(tm, M), min(tn, N)
    assert M % tm == 0 and N % tn == 0, (name, M, N, K)
    dims = (((0 if ta else 1,), (1 if tb else 0,)), ((), ()))
    ex_ts, ex_sums = exchange if exchange else ((), ())
    n_ex = len(ex_ts)
    gm, gn = M // tm, N // tn

    n_add = 0 if add is None else 1
    assert not (n_add and n_ex)

    def body(a_ref, b_ref, *rest):
        o_ref = rest[n_ex + n_add]
        if n_ex:
            ex = _ChipExchange(ex_ts, rest[:n_ex], rest[n_ex + 1:2 * n_ex + 1], rest[2 * n_ex + 1:])
            step = pl.program_id(0) * gn + pl.program_id(1)
            pl.when(step == 0)(ex.start)
        acc = lax.dot_general(a_ref[...], b_ref[...], dims, preferred_element_type=F32)
        if n_add:
            acc = rest[0][...] + acc
        o_ref[...] = acc.astype(o_ref.dtype)
        if n_ex:
            pl.when(step == gm * gn - 1)(ex.finish)

    a_spec = pl.BlockSpec((K, tm), lambda i, j: (0, i)) if ta else pl.BlockSpec((tm, K), lambda i, j: (i, 0))
    b_spec = pl.BlockSpec((tn, K), lambda i, j: (j, 0)) if tb else pl.BlockSpec((K, tn), lambda i, j: (0, j))
    tile = pl.BlockSpec((tm, tn), lambda i, j: (i, j))
    res = pl.pallas_call(
        body, name=name,
        out_shape=[jax.ShapeDtypeStruct((M, N), out_dtype)] + (_ChipExchange.out_shape(ex_ts) if n_ex else []),
        grid=(gm, gn),
        in_specs=[a_spec, b_spec] + [tile] * n_add + [_ANY] * n_ex,
        out_specs=[tile] + [_ANY] * n_ex,
        scratch_shapes=_ChipExchange.scratch(ex_ts) if n_ex else [],
        compiler_params=_params(("arbitrary", "arbitrary") if n_ex else ("parallel", "parallel")),
    )(a, b, *([add] if n_add else []), *ex_sums)
    return res if n_ex else res[0]


def _rowwise(fn, rows, consts, outs, accs=(), *, tr=256, name):
    L = rows[0][0].shape[0]
    tr = math.gcd(tr, L)
    assert tr % 8 == 0 or tr == L, (name, L, tr)
    n_in, n_c, n_o, n_a = len(rows), len(consts), len(outs), len(accs)

    def body(*refs):
        ins = [r[...] for r in refs[:n_in + n_c]]
        res = fn(*ins)
        if not isinstance(res, (tuple, list)):
            res = (res,)
        o_refs = refs[n_in + n_c:n_in + n_c + n_o]
        a_refs = refs[n_in + n_c + n_o:]
        for r, v in zip(o_refs, res[:n_o]):
            r[...] = v.astype(r.dtype)
        if n_a:
            first = pl.program_id(0) == 0

            @pl.when(first)
            def _():
                for r, v in zip(a_refs, res[n_o:]):
                    r[...] = v.astype(F32)

            @pl.when(jnp.logical_not(first))
            def _():
                for r, v in zip(a_refs, res[n_o:]):
                    r[...] += v.astype(F32)

    in_specs = []
    for arr, off, w in rows:
        assert off % w == 0, (name, off, w)
        in_specs.append(pl.BlockSpec((tr, w), functools.partial(lambda i, c: (i, c), c=off // w)))
    for c in consts:
        in_specs.append(pl.BlockSpec(c.shape, lambda i: (0, 0)))
    out_shape = [jax.ShapeDtypeStruct((L, w), dt) for w, dt in outs] + [jax.ShapeDtypeStruct(s, F32) for s in accs]
    out_specs = [pl.BlockSpec((tr, w), lambda i: (i, 0)) for w, _ in outs] + [pl.BlockSpec(s, lambda i: (0, 0)) for s in accs]
    res = pl.pallas_call(
        body, name=name,
        out_shape=out_shape,
        grid=(L // tr,),
        in_specs=in_specs,
        out_specs=out_specs,
        compiler_params=_params(("arbitrary",) if n_a else ("parallel",)),
    )(*[r[0] for r in rows], *consts)
    return res


def _sig(x):
    return jax.nn.sigmoid(x)


def _silu_and_grad(x):
    s = _sig(x)
    return x * s, s * (1.0 + x * (1.0 - s))


_GELU_C = math.sqrt(2.0 / math.pi)


def _gelu(x):
    return 0.5 * x * (1.0 + jnp.tanh(_GELU_C * (x + 0.044715 * x * x * x)))


def _gelu_grad(x):
    t = jnp.tanh(_GELU_C * (x + 0.044715 * x * x * x))
    return 0.5 * (1.0 + t) + 0.5 * x * (1.0 - t * t) * _GELU_C * (1.0 + 3.0 * 0.044715 * x * x)


def _rms(x):
    return lax.rsqrt(jnp.mean(x * x, axis=-1, keepdims=True) + EPS)


def _rms_bwd(x, g, dy):
    r = _rms(x)
    n = x * r
    dn = dy * g
    return r * (dn - n * jnp.mean(dn * n, axis=-1, keepdims=True)), dy * n


def _colsum(v):
    return jnp.sum(v, axis=0, keepdims=True)


def _rope_partner(x):
    lane = lax.broadcasted_iota(jnp.int32, x.shape, x.ndim - 1)
    return jnp.where(lane % 64 < 32, pltpu.roll(x, 96, x.ndim - 1), pltpu.roll(x, 32, x.ndim - 1))


def _rope_tables(L):
    rows_n = L // GRID_W
    rows = np.repeat(np.arange(rows_n), GRID_W).astype(np.float32)
    cols = np.tile(np.arange(GRID_W), rows_n).astype(np.float32)
    n_freq = HEAD_DIM // 4
    inv_freq = np.float32(ROPE_THETA) ** (-np.arange(n_freq, dtype=np.float32) / np.float32(n_freq))
    ar, ac = rows[:, None] * inv_freq[None, :], cols[:, None] * inv_freq[None, :]
    cos = np.concatenate([np.cos(ar), np.cos(ar), np.cos(ac), np.cos(ac)], axis=-1).astype(np.float32)
    sin = np.concatenate([-np.sin(ar), np.sin(ar), -np.sin(ac), np.sin(ac)], axis=-1).astype(np.float32)
    return jnp.asarray(cos), jnp.asarray(sin)


def _heads(v):
    return [v[:, h * HEAD_DIM:(h + 1) * HEAD_DIM] for h in range(v.shape[1] // HEAD_DIM)]


def _attn_prep(z, q_norm, k_norm, cos, sin):
    def fn(q, k, v, cos, sin, gq, gk):
        def one(xh, g):
            xn = xh * _rms(xh) * g
            return xn * cos + _rope_partner(xn) * sin
        qr = jnp.concatenate([one(h, gq) for h in _heads(q)], axis=1)
        kr = jnp.concatenate([one(h, gk) for h in _heads(k)], axis=1)
        return qr, kr, v
    return _rowwise(fn, [(z, Z_Q, D_ATTN), (z, Z_K, D_KV), (z, Z_V, D_KV), (cos, 0, HEAD_DIM), (sin, 0, HEAD_DIM)],
                    [q_norm, k_norm], [(D_ATTN, BF16), (D_KV, BF16), (D_KV, BF16)], name="attn_prep")


def _attn_prep_bwd(dqr, dkr, z, q_norm, k_norm, cos, sin):
    def fn(dqr, dkr, q, k, cos, sin, gq, gk):
        def one(dyh, xh, g):
            dn = dyh * cos + _rope_partner(dyh * sin)
            return _rms_bwd(xh, g, dn)
        rq = [one(a, b, gq) for a, b in zip(_heads(dqr), _heads(q))]
        rk = [one(a, b, gk) for a, b in zip(_heads(dkr), _heads(k))]
        dq = jnp.concatenate([r[0] for r in rq], axis=1)
        dk = jnp.concatenate([r[0] for r in rk], axis=1)
        return dq, dk, _colsum(sum(r[1] for r in rq)), _colsum(sum(r[1] for r in rk))
    return _rowwise(fn, [(dqr, 0, D_ATTN), (dkr, 0, D_KV), (z, Z_Q, D_ATTN), (z, Z_K, D_KV), (cos, 0, HEAD_DIM), (sin, 0, HEAD_DIM)],
                    [q_norm, k_norm], [(D_ATTN, BF16), (D_KV, BF16)], [(1, HEAD_DIM), (1, HEAD_DIM)], name="attn_prep_bwd")


_QK_T = (((1,), (1,)), ((), ()))
_TA = (((0,), (0,)), ((), ()))
_REP = N_HEADS // N_KV


_EXP2_SCALE = HEAD_DIM ** -0.5 * math.log2(math.e)
ATTN_FWD_KEY_CHUNKS = 4
ATTN_BWD_KEY_CHUNKS = 8


def _attn_fwd(qr, kr, vb, g_ts, g_shards, *, tq=1024):
    L = qr.shape[0]
    tq = min(tq, L)
    kc = L // ATTN_FWD_KEY_CHUNKS
    n_g = len(g_ts)
    grid = (N_HEADS, L // tq)
    steps = grid[0] * grid[1]

    def body(q_ref, k_ref, v_ref, *rest):
        o_ref, lse_ref = rest[n_g:n_g + 2]
        g = _Gather(g_ts, rest[:n_g], rest[n_g + 2:2 * n_g + 2], rest[2 * n_g + 2:3 * n_g + 2], rest[3 * n_g + 2:])
        step = pl.program_id(0) * grid[1] + pl.program_id(1)
        pl.when(step == 0)(g.start)
        pl.when(step == (3 * steps) // 4)(g.forward)
        q = q_ref[...]
        m = jnp.full((tq, 1), -jnp.inf, F32)
        l = jnp.zeros((tq, 1), F32)
        o = jnp.zeros((tq, HEAD_DIM), F32)
        for c in range(ATTN_FWD_KEY_CHUNKS):
            ks = slice(c * kc, (c + 1) * kc)
            s = lax.dot_general(q, k_ref[ks, :], _QK_T, preferred_element_type=F32)
            m_new = jnp.maximum(m, jnp.max(s, axis=1, keepdims=True))
            a = jnp.exp2((m - m_new) * _EXP2_SCALE)
            p = jnp.exp2((s - m_new) * _EXP2_SCALE)
            l = a * l + jnp.sum(p, axis=1, keepdims=True)
            o = a * o + jnp.dot(p.astype(BF16), v_ref[ks, :], preferred_element_type=F32)
            m = m_new
        o_ref[...] = o * (1.0 / l)
        lse_ref[...] = m * _EXP2_SCALE + jnp.log2(l)
        pl.when(step == steps - 1)(g.finish)

    kv = pl.BlockSpec((L, HEAD_DIM), lambda h, i: (0, h // _REP))
    return pl.pallas_call(
        body, name="attn_fwd",
        out_shape=[jax.ShapeDtypeStruct((L, D_ATTN), F32), jax.ShapeDtypeStruct((N_HEADS, L, 1), F32)] +
                  [jax.ShapeDtypeStruct(_BIG[t][1], BF16) for t in g_ts],
        grid=grid,
        in_specs=[pl.BlockSpec((tq, HEAD_DIM), lambda h, i: (i, h)), kv, kv] + [_ANY] * n_g,
        out_specs=[pl.BlockSpec((tq, HEAD_DIM), lambda h, i: (i, h)),
                   pl.BlockSpec((None, tq, 1), lambda h, i: (h, i, 0))] + [_ANY] * n_g,
        scratch_shapes=_Gather.scratch(g_shards),
        compiler_params=_params(("arbitrary", "arbitrary")),
    )(qr, kr, vb, *g_shards)


def _attn_bwd(qr, kr, k_t, vb, do, lse, delta, ex_ts, ex_sums, *, tq=1024):
    L = qr.shape[0]
    tq = min(tq, L)
    scale = HEAD_DIM ** -0.5
    kc = L // ATTN_BWD_KEY_CHUNKS
    n_ex = len(ex_ts)
    grid = (N_KV, _REP, L // tq)

    def body(q_ref, k_ref, kt_ref, v_ref, do_ref, lse_ref, delta_ref, *rest):
        dq_ref, dk_ref, dv_ref = rest[n_ex:n_ex + 3]
        ex = _ChipExchange(ex_ts, rest[:n_ex], rest[n_ex + 3:2 * n_ex + 3], rest[2 * n_ex + 3:])
        step = (pl.program_id(0) * grid[1] + pl.program_id(1)) * grid[2] + pl.program_id(2)
        pl.when(step == 0)(ex.start)

        @pl.when((pl.program_id(1) == 0) & (pl.program_id(2) == 0))
        def _():
            dk_ref[...] = jnp.zeros_like(dk_ref)
            dv_ref[...] = jnp.zeros_like(dv_ref)

        q, do, lse, delta = q_ref[...], do_ref[...], lse_ref[...], delta_ref[...]
        dq_t = 0.0
        for c in range(ATTN_BWD_KEY_CHUNKS):
            ks = slice(c * kc, (c + 1) * kc)
            st = lax.dot_general(k_ref[ks, :], q, _QK_T, preferred_element_type=F32)
            p = jnp.exp2(st * _EXP2_SCALE - lse)
            dv_ref[ks, :] += jnp.dot(p.astype(BF16), do, preferred_element_type=F32)
            dp = lax.dot_general(v_ref[ks, :], do, _QK_T, preferred_element_type=F32)
            ds = (p * (dp - delta) * scale).astype(BF16)
            dk_ref[ks, :] += jnp.dot(ds, q, preferred_element_type=F32)
            dq_t = dq_t + jnp.dot(kt_ref[:, ks], ds, preferred_element_type=F32)
        dq_ref[...] = dq_t.T
        pl.when(step == grid[0] * grid[1] * grid[2] - 1)(ex.finish)

    head = lambda g, r, i: (i, g * _REP + r)
    kv = pl.BlockSpec((L, HEAD_DIM), lambda g, r, i: (0, g))
    per_query = pl.BlockSpec((None, 1, tq), lambda g, r, i: (g * _REP + r, 0, i))
    return pl.pallas_call(
        body, name="attn_bwd",
        out_shape=[jax.ShapeDtypeStruct((L, D_ATTN), F32), jax.ShapeDtypeStruct((L, D_KV), F32), jax.ShapeDtypeStruct((L, D_KV), F32)] +
                  _ChipExchange.out_shape(ex_ts),
        grid=grid,
        in_specs=[pl.BlockSpec((tq, HEAD_DIM), head), kv,
                  pl.BlockSpec((HEAD_DIM, L), lambda g, r, i: (g, 0)), kv,
                  pl.BlockSpec((tq, HEAD_DIM), head), per_query, per_query] + [_ANY] * n_ex,
        out_specs=[pl.BlockSpec((tq, HEAD_DIM), head), kv, kv] + [_ANY] * n_ex,
        scratch_shapes=_ChipExchange.scratch(ex_ts),
        compiler_params=_params(("arbitrary", "arbitrary", "arbitrary")),
    )(qr, kr, k_t, vb, do, lse, delta, *ex_sums)


SSM_BLK = 8
SSM_NB = SSM_G // SSM_BLK
SSM_SEG = 8
SSM_UNROLL = 4


def _unrolled_loop(n, step, carry):
    u = SSM_UNROLL

    def trip(i, c):
        for j in range(u):
            c = step(i * u + j, c)
        return c
    carry = lax.fori_loop(0, n // u, trip, carry)
    for t in range(n - n % u, n):
        carry = step(jnp.int32(t), carry)
    return carry


def _cplx_pow2(a, b, n):
    for _ in range(int(math.log2(n))):
        a, b = a * a - b * b, 2.0 * a * b
    return a, b


def _seg_scan(ref, a, b, T, reverse, entry=None, tap=None):
    npair = len(a)
    zero = jnp.zeros((SSM_SEG, 128), F32)

    def make_step(store, tap_fn=None):
        def step(t, carry):
            lt = (T - 1 - t) if reverse else t
            row = pl.multiple_of(lt * SSM_SEG, SSM_SEG)
            blk = ref[pl.ds(row, SSM_SEG), :]
            new = []
            for q in range(npair):
                re, im = carry[2 * q], carry[2 * q + 1]
                nre = a[q] * re - b[q] * im + blk[:, q * 256:q * 256 + 128]
                nim = a[q] * im + b[q] * re + blk[:, q * 256 + 128:q * 256 + 256]
                new += [nre, nim]
            if store:
                ref[pl.ds(row, SSM_SEG), :] = jnp.concatenate(new, axis=1)
            extra = carry[2 * npair:]
            return tuple(new) + (tuple(tap_fn(lt, new, extra)) if tap_fn else tuple(extra))
        return step

    def second_pass(init):
        if tap is None:
            _unrolled_loop(T, make_step(True), tuple(init))
            return init
        carry = _unrolled_loop(T - 1, make_step(True, tap[0]), tuple(init) + tuple(tap[2]))
        carry = make_step(True, tap[1])(jnp.int32(T - 1), carry)
        return init, carry[2 * npair:]

    if entry is not None:
        return second_pass(entry)
    ends = _unrolled_loop(T, make_step(False), (zero,) * (2 * npair))
    sub = lax.broadcasted_iota(jnp.int32, (SSM_SEG, 128), 0)
    keep = (sub != SSM_SEG - 1) if reverse else (sub != 0)
    shift = (SSM_SEG - 1) if reverse else 1
    init = []
    for q in range(npair):
        pa, pb = _cplx_pow2(a[q], b[q], T)
        xr, xi = zero, zero
        for _ in range(SSM_SEG - 1):
            fr = ends[2 * q] + pa * xr - pb * xi
            fi = ends[2 * q + 1] + pa * xi + pb * xr
            xr = jnp.where(keep, pltpu.roll(fr, shift, 0), 0.0)
            xi = jnp.where(keep, pltpu.roll(fi, shift, 0), 0.0)
        init += [xr, xi]
    return second_pass(init)


SSM_BW = SSM_BLK * SSM_H
SSM_SW = SSM_BLK * 2 * SSM_P
SSM_NPAIR = SSM_BLK // 2
SSM_FWD_MATMUL_ROWS = 2048
SSM_BWD_MATMUL_ROWS = 2048


def _seg_perm(a):
    L, C = a.shape
    return a.reshape(SSM_SEG, L // SSM_SEG, C).transpose(1, 0, 2).reshape(L, C)


def _seg_unperm(a):
    L, C = a.shape
    return a.reshape(L // SSM_SEG, SSM_SEG, C).transpose(1, 0, 2).reshape(L, C)


def _lam_rows(are_ref, aim_ref, d):
    a = [jnp.broadcast_to(are_ref[d, j:j + 1, :], (SSM_SEG, 128)) for j in range(SSM_NPAIR)]
    b = [jnp.broadcast_to(aim_ref[d, j:j + 1, :], (SSM_SEG, 128)) for j in range(SSM_NPAIR)]
    return a, b


_PAIR_SPEC = pl.BlockSpec((None, 2, SSM_NPAIR, 2, 2 * SSM_H, 128), lambda g: (g, 0, 0, 0, 0, 0))


def _pair_window(j, r):
    return slice(j * 2 * SSM_H, (j + 1) * 2 * SSM_H), slice(j * 256 + r * 128, j * 256 + (r + 1) * 128)


def _expand_pairs(c_ref, dense_ref):
    dense_ref[...] = jnp.zeros_like(dense_ref)
    for d in range(2):
        for j in range(SSM_NPAIR):
            for r in range(2):
                rows, cols = _pair_window(j, r)
                dense_ref[d, rows, cols] = c_ref[d, j, r].astype(dense_ref.dtype)


def _ssm_fwd(u_p, wb, wc, are, aim, dvec):
    L = u_p.shape[0]
    T = L // SSM_SEG
    RC = min(SSM_FWD_MATMUL_ROWS, L)

    def body(u_ref, wb_ref, wc_ref, are_ref, aim_ref, d_ref, y_ref, ge_ref, entry_ref, x_scr, wb_s, wc_s):
        _expand_pairs(wb_ref, wb_s)
        _expand_pairs(wc_ref, wc_s)
        y_ref[...] = u_ref[...] * d_ref[...]
        for d in range(2):
            def bu_chunk(c, _):
                rows = pl.ds(pl.multiple_of(c * RC, RC), RC)
                x_scr[rows, :] = jnp.dot(u_ref[rows, :].astype(BF16), wb_s[d], preferred_element_type=F32)
                return 0
            lax.fori_loop(0, L // RC, bu_chunk, 0)
            a, b = _lam_rows(are_ref, aim_ref, d)
            entry_ref[d] = jnp.concatenate(_seg_scan(x_scr, a, b, T, reverse=(d == 1)), axis=1)

            def y_chunk(c, _):
                rows = pl.ds(pl.multiple_of(c * RC, RC), RC)
                y_ref[rows, :] += lax.dot_general(x_scr[rows, :].astype(BF16), wc_s[d], _QK_T, preferred_element_type=F32)
                return 0
            lax.fori_loop(0, L // RC, y_chunk, 0)
        ge_ref[...] = _gelu(y_ref[...]).astype(ge_ref.dtype)

    blk4 = lambda g: (g, 0, 0, 0)
    chan = pl.BlockSpec((L, SSM_BW), lambda g: (0, g))
    return pl.pallas_call(
        body, name="ssm_fwd",
        out_shape=[jax.ShapeDtypeStruct((L, D_SSM), F32), jax.ShapeDtypeStruct((L, D_SSM), BF16),
                   jax.ShapeDtypeStruct((SSM_NB, 2, SSM_SEG, SSM_SW), F32)],
        grid=(SSM_NB,),
        in_specs=[chan, _PAIR_SPEC, _PAIR_SPEC,
                  pl.BlockSpec((None, 2, SSM_NPAIR, 128), blk4),
                  pl.BlockSpec((None, 2, SSM_NPAIR, 128), blk4),
                  pl.BlockSpec((1, SSM_BW), lambda g: (0, g))],
        out_specs=[chan, chan, pl.BlockSpec((None, 2, SSM_SEG, SSM_SW), blk4)],
        scratch_shapes=[pltpu.VMEM((L, SSM_SW), F32), pltpu.VMEM((2, SSM_BW, SSM_SW), BF16), pltpu.VMEM((2, SSM_BW, SSM_SW), BF16)],
        compiler_params=_params(("parallel",)),
    )(u_p, wb, wc, are, aim, dvec)


def _ssm_bwd(u_p, dy_p, entry, wb, wc, are, aim, dvec, sib_ts, sib_grads):
    L = u_p.shape[0]
    T = L // SSM_SEG
    RC = min(SSM_BWD_MATMUL_ROWS, L)
    n_sib = len(sib_ts)

    def lam_acc(acc, s, x):
        new = []
        for q in range(SSM_NPAIR):
            sr, si, xr, xi = s[2 * q], s[2 * q + 1], x[2 * q], x[2 * q + 1]
            new += [acc[2 * q] + sr * xr + si * xi, acc[2 * q + 1] + si * xr - sr * xi]
        return tuple(new)

    def body(u_ref, dy_ref, entry_ref, wb_ref, wc_ref, are_ref, aim_ref, d_ref, *rest):
        du_ref, dwb_ref, dwc_ref, dare_ref, daim_ref, dd_ref = rest[n_sib:n_sib + 6]
        x_scr, s_scr, wb_s, wc_s, dwb_s, dwc_s = rest[2 * n_sib + 6:2 * n_sib + 12]
        sib = _SiblingExchange(sib_ts, rest[:n_sib], rest[n_sib + 6:2 * n_sib + 6], rest[2 * n_sib + 12:])
        pl.when(pl.program_id(0) == 0)(sib.start)
        _expand_pairs(wb_ref, wb_s)
        _expand_pairs(wc_ref, wc_s)
        du_ref[...] = dy_ref[...] * d_ref[...]
        dd_ref[...] = _colsum(dy_ref[...] * u_ref[...])
        dwb_s[...] = jnp.zeros_like(dwb_s)
        dwc_s[...] = jnp.zeros_like(dwc_s)
        for d in range(2):
            rev = d == 1

            def in_chunk(c, _):
                rows = pl.ds(pl.multiple_of(c * RC, RC), RC)
                x_scr[rows, :] = jnp.dot(u_ref[rows, :].astype(BF16), wb_s[d], preferred_element_type=F32)
                s_scr[rows, :] = jnp.dot(dy_ref[rows, :].astype(BF16), wc_s[d], preferred_element_type=F32)
                return 0
            lax.fori_loop(0, L // RC, in_chunk, 0)
            a, b = _lam_rows(are_ref, aim_ref, d)
            x_in = _seg_scan(x_scr, a, b, T, reverse=rev, entry=[entry_ref[d, :, q * 128:(q + 1) * 128] for q in range(2 * SSM_NPAIR)])

            def pair_with_row(lt, s_new, acc):
                xrow = pl.multiple_of((lt + 1 if rev else lt - 1) * SSM_SEG, SSM_SEG)
                xb = x_scr[pl.ds(xrow, SSM_SEG), :]
                return lam_acc(acc, s_new, [xb[:, i * 128:(i + 1) * 128] for i in range(2 * SSM_NPAIR)])

            zeros = (jnp.zeros((SSM_SEG, 128), F32),) * (2 * SSM_NPAIR)
            _, acc = _seg_scan(s_scr, a, [-v for v in b], T, reverse=not rev,
                               tap=(pair_with_row, lambda lt, s_new, acc: lam_acc(acc, s_new, x_in), zeros))
            for q in range(SSM_NPAIR):
                dare_ref[d, q:q + 1, :] = _colsum(acc[2 * q])
                daim_ref[d, q:q + 1, :] = _colsum(acc[2 * q + 1])

            def out_chunk(c, _):
                rows = pl.ds(pl.multiple_of(c * RC, RC), RC)
                xs, ss = x_scr[rows, :].astype(BF16), s_scr[rows, :].astype(BF16)
                uu, dd = u_ref[rows, :].astype(BF16), dy_ref[rows, :].astype(BF16)
                dwc_s[d] += lax.dot_general(dd, xs, _TA, preferred_element_type=F32)
                dwb_s[d] += lax.dot_general(uu, ss, _TA, preferred_element_type=F32)
                du_ref[rows, :] += lax.dot_general(ss, wb_s[d], _QK_T, preferred_element_type=F32)
                return 0
            lax.fori_loop(0, L // RC, out_chunk, 0)
            for j in range(SSM_NPAIR):
                for r in range(2):
                    rows, cols = _pair_window(j, r)
                    dwb_ref[d, j, r] = dwb_s[d, rows, cols]
                    dwc_ref[d, j, r] = dwc_s[d, rows, cols]
        pl.when(pl.program_id(0) == SSM_NB - 1)(sib.finish)

    blk4 = lambda g: (g, 0, 0, 0)
    chan = pl.BlockSpec((L, SSM_BW), lambda g: (0, g))
    par_specs = [_PAIR_SPEC, _PAIR_SPEC,
                 pl.BlockSpec((None, 2, SSM_NPAIR, 128), blk4),
                 pl.BlockSpec((None, 2, SSM_NPAIR, 128), blk4),
                 pl.BlockSpec((1, SSM_BW), lambda g: (0, g))]
    dense = lambda dt: pltpu.VMEM((2, SSM_BW, SSM_SW), dt)
    return pl.pallas_call(
        body, name="ssm_bwd",
        out_shape=[jax.ShapeDtypeStruct((L, D_SSM), F32),
                   jax.ShapeDtypeStruct(wb.shape, F32),
                   jax.ShapeDtypeStruct(wc.shape, F32),
                   jax.ShapeDtypeStruct((SSM_NB, 2, SSM_NPAIR, 128), F32),
                   jax.ShapeDtypeStruct((SSM_NB, 2, SSM_NPAIR, 128), F32),
                   jax.ShapeDtypeStruct((1, D_SSM), F32)] + _SiblingExchange.out_shape(sib_ts),
        grid=(SSM_NB,),
        in_specs=[chan, chan, pl.BlockSpec((None, 2, SSM_SEG, SSM_SW), blk4)] + par_specs + [_ANY] * n_sib,
        out_specs=[chan] + par_specs + [_ANY] * n_sib,
        scratch_shapes=[pltpu.VMEM((L, SSM_SW), F32), pltpu.VMEM((L, SSM_SW), F32), dense(BF16), dense(BF16), dense(F32), dense(F32)] +
                       _SiblingExchange.scratch(sib_ts),
        compiler_params=_params(("arbitrary",)),
    )(u_p, dy_p, entry, wb, wc, are, aim, dvec, *sib_grads)


def _ssm_disc(a_re, a_im, log_dt, b_re, b_im):
    lam = lax.complex(jnp.minimum(a_re, -1e-4), a_im)
    dt = jnp.exp(log_dt)[..., None]
    lam_bar = jnp.exp(lam * dt)
    b_bar = ((lam_bar - 1.0) / lam)[..., None] * lax.complex(b_re, b_im)
    return jnp.real(lam_bar), jnp.imag(lam_bar), jnp.real(b_bar), jnp.imag(b_bar)


_EYE2 = np.eye(2, dtype=np.float32)[:, None, :, None]


def _to_pairs(t):
    t = t.reshape(2, SSM_NB, SSM_NPAIR, 2, 2, SSM_H, SSM_P).transpose(1, 0, 2, 4, 3, 5, 6)
    return (t[..., None, :] * _EYE2).reshape(SSM_NB, 2, SSM_NPAIR, 2, 2 * SSM_H, 2 * SSM_P)


def _from_pairs(c):
    t = c.reshape(SSM_NB, 2, SSM_NPAIR, 2, 2, SSM_H, 2, SSM_P)
    t = jnp.stack([t[:, :, :, :, 0, :, 0, :], t[:, :, :, :, 1, :, 1, :]], axis=4)
    return t.transpose(1, 0, 2, 4, 3, 5, 6).reshape(2, SSM_G, 2, SSM_H, SSM_P)


def _to_lam(v):
    return v.reshape(2, SSM_NB, SSM_NPAIR, 128).transpose(1, 0, 2, 3)


def _from_lam(v):
    return v.transpose(1, 0, 2, 3).reshape(2, SSM_G, SSM_P)


_MESH = pl.DeviceIdType.MESH
_ANY = pl.BlockSpec(memory_space=pl.ANY)
_BIG = (("w_in", (D_MODEL, D_IN), 1, D_IN // N_CHIPS),
        ("w_glu", (D_SSM, 2 * D_SSM), 1, 2 * D_SSM // N_CHIPS),
        ("w_out", (D_ATTN + D_SSM, D_MODEL), 0, (D_ATTN + D_SSM) // N_CHIPS),
        ("w_ple_gate", (D_MODEL, D_MODEL), 0, D_MODEL // N_CHIPS),
        ("w_ple_proj", (PLE_DIM, D_MODEL), 1, D_MODEL // N_CHIPS))


def _place():
    x, y, c = lax.axis_index("x"), lax.axis_index("y"), lax.axis_index("c")
    return x, y, c, [(1 - x, y), (x, 1 - y), (1 - x, 1 - y)]


class _Gather:
    def __init__(self, ts, srcs, dsts, stage, sems):
        self.ts, self.srcs, self.dsts, self.stage = ts, srcs, dsts, stage
        self.send_sems, self.recv_sems, self.fwd_send_sems, self.fwd_recv_sems, self.loc_sems = sems
        self.x, self.y, self.c, self.chips = _place()
        self.n = len(ts)

    @staticmethod
    def scratch(shards):
        sems = pltpu.SemaphoreType.DMA((3, len(shards)))
        return [pltpu.VMEM(s.shape, BF16) for s in shards] + [sems, sems, sems, sems, pltpu.SemaphoreType.DMA((len(shards),))]

    def _shard_of(self, i, kk):
        _, _, axis, sz = _BIG[self.ts[i]]
        sl = pl.ds(pl.multiple_of(kk * sz, sz), sz)
        return self.dsts[i].at[:, sl] if axis == 1 else self.dsts[i].at[sl, :]

    @staticmethod
    def _half_of(ref, cc):
        n = ref.shape[0] // 2
        return ref.at[pl.ds(pl.multiple_of(cc * n, n), n), :]

    def _ici(self, j, i, kk):
        px, py = self.chips[j]
        return pltpu.make_async_remote_copy(
            src_ref=self._half_of(self.srcs[i], self.c), dst_ref=self._half_of(self._shard_of(i, kk), self.c),
            send_sem=self.send_sems.at[j, i], recv_sem=self.recv_sems.at[j, i],
            device_id=(px, py, self.c), device_id_type=_MESH)

    def _forward(self, j, i, kk, cc):
        part = self._half_of(self._shard_of(i, kk), cc)
        return pltpu.make_async_remote_copy(
            src_ref=part, dst_ref=part, send_sem=self.fwd_send_sems.at[j, i], recv_sem=self.fwd_recv_sems.at[j, i],
            device_id=(self.x, self.y, 1 - self.c), device_id_type=_MESH)

    def _load(self, i):
        return pltpu.make_async_copy(self.srcs[i], self.stage[i], self.loc_sems.at[i])

    def _place_own(self, i):
        return pltpu.make_async_copy(self.stage[i], self._shard_of(i, 2 * self.x + self.y), self.loc_sems.at[i])

    def _peers(self):
        return [(i, j, 2 * px + py) for i in range(self.n) for j, (px, py) in enumerate(self.chips)]

    def start(self, relations=(0, 1, 2)):
        for i in range(self.n):
            self._load(i).start()
        self.send(relations)

    def send(self, relations):
        for i, j, _ in self._peers():
            if j in relations:
                self._ici(j, i, 2 * self.x + self.y).start()

    def forward(self):
        for i in range(self.n):
            self._load(i).wait()
            self._place_own(i).start()
        for i, j, kk in self._peers():
            self._ici(j, i, kk).wait_recv()
            self._forward(j, i, kk, self.c).start()

    def finish(self):
        for i, j, kk in self._peers():
            self._forward(j, i, kk, 1 - self.c).wait_recv()
        for i, j, kk in self._peers():
            self._ici(j, i, kk).wait_send()
            self._forward(j, i, kk, self.c).wait_send()
        for i in range(self.n):
            self._place_own(i).wait()


def _matmul_in_gather(a, shard, *, tm=1024):
    t = 0
    (_, (K, N), _, sz) = _BIG[t]
    M = a.shape[0]
    tm = min(tm, M)
    gm = M // tm
    x, y = lax.axis_index("x"), lax.axis_index("y")
    order = jnp.stack([2 * x + y, 2 * (1 - x) + y, 2 * x + 1 - y, 2 * (1 - x) + 1 - y]).astype(jnp.int32)

    def body(order_ref, a_ref, shard_ref, z_ref, full_ref, b_vm, *sems):
        g = _Gather([t], [shard_ref], [full_ref], [b_vm], sems[:5])
        load_sem = sems[5]
        s, i = pl.program_id(0), pl.program_id(1)

        @pl.when((s == 0) & (i == 0))
        def _():
            g.start(relations=(0, 1))
            g._load(0).wait()
            g._place_own(0).start()

        for j, (px, py) in enumerate(g.chips):
            @pl.when((s == j + 1) & (i == 0))
            def _(j=j, kk=2 * px + py):
                if j == 0:
                    g._place_own(0).wait()
                g._ici(j, 0, kk).wait_recv()
                if j == 0:
                    g.send((2,))
                g._forward(j, 0, kk, g.c).start()
                g._forward(j, 0, kk, 1 - g.c).wait_recv()
                cp = pltpu.make_async_copy(g._shard_of(0, kk), b_vm, load_sem.at[0])
                cp.start()
                cp.wait()

        z_ref[...] = jnp.dot(a_ref[...], b_vm[...], preferred_element_type=F32)

        @pl.when((s == N_CHIPS - 1) & (i == gm - 1))
        def _():
            for j, (px, py) in enumerate(g.chips):
                g._ici(j, 0, 2 * px + py).wait_send()
                g._forward(j, 0, 2 * px + py, g.c).wait_send()

    return pl.pallas_call(
        body, name="mm_in",
        out_shape=[jax.ShapeDtypeStruct((M, N), F32), jax.ShapeDtypeStruct((K, N), BF16)],
        grid_spec=pltpu.PrefetchScalarGridSpec(
            num_scalar_prefetch=1, grid=(N_CHIPS, gm),
            in_specs=[pl.BlockSpec((tm, K), lambda s, i, o: (i, 0)), _ANY],
            out_specs=[pl.BlockSpec((tm, sz), lambda s, i, o: (i, o[s])), _ANY],
            scratch_shapes=_Gather.scratch([shard]) + [pltpu.SemaphoreType.DMA((1,))]),
        compiler_params=_params(("arbitrary", "arbitrary")),
    )(order, a, shard)


SMALL_W = 1024
SMALL_ROWS = 72
N_SMALL = 8 * SMALL_ROWS * SMALL_W
_RED = tuple((shape, ax, (shape[0] // 2, sz) if ax == 1 else (sz // 2, shape[1]), BF16) for _, shape, ax, sz in _BIG) + \
    (((8 * SMALL_ROWS, SMALL_W), 0, (SMALL_ROWS, SMALL_W), F32),)
_RED_TR = 128


def _piece(ref, t, kk, cc):
    _, ax, (pr, pc), _ = _RED[t]
    if ax == 1:
        return ref.at[pl.ds(pl.multiple_of(cc * pr, pr), pr), pl.ds(pl.multiple_of(kk * pc, pc), pc)]
    return ref.at[pl.ds(pl.multiple_of((2 * kk + cc) * pr, pr), pr), :]


def _half_shape(t):
    shape, ax, (pr, pc), _ = _RED[t]
    return (pr, shape[1]) if ax == 1 else (N_CHIPS * pr, pc)


def _piece_in_half(ref, t, kk):
    _, ax, (pr, pc), _ = _RED[t]
    return ref.at[:, pl.ds(pl.multiple_of(kk * pc, pc), pc)] if ax == 1 else ref.at[pl.ds(pl.multiple_of(kk * pr, pr), pr), :]


class _SiblingExchange:
    def __init__(self, ts, srcs, dsts, sems):
        send_sems, recv_sems = sems
        x, y, c, _ = _place()

        def copies():
            pairs = []
            for i, t in enumerate(ts):
                _, ax, (pr, _), _ = _RED[t]
                if ax == 1:
                    pairs.append((srcs[i].at[pl.ds(pl.multiple_of((1 - c) * pr, pr), pr), :], dsts[i]))
                else:
                    pairs += [(_piece(srcs[i], t, kk, 1 - c), _piece_in_half(dsts[i], t, kk)) for kk in range(N_CHIPS)]
            return [pltpu.make_async_remote_copy(src_ref=s, dst_ref=d, send_sem=send_sems.at[i], recv_sem=recv_sems.at[i],
                                                 device_id=(x, y, 1 - c), device_id_type=_MESH) for i, (s, d) in enumerate(pairs)]
        self.copies = copies

    @staticmethod
    def scratch(ts):
        n_dma = sum(1 if _RED[t][1] == 1 else N_CHIPS for t in ts)
        return [pltpu.SemaphoreType.DMA((n_dma,)), pltpu.SemaphoreType.DMA((n_dma,))]

    @staticmethod
    def out_shape(ts):
        return [jax.ShapeDtypeStruct(_half_shape(t), F32) for t in ts]

    def start(self):
        for cp in self.copies():
            cp.start()

    def finish(self):
        for cp in self.copies():
            cp.wait()


def _grad_sibling_exchange(ts, grads, name):
    n = len(ts)

    def body(*refs):
        ex = _SiblingExchange(ts, refs[:n], refs[n:2 * n], refs[2 * n:])
        ex.start()
        ex.finish()

    return pl.pallas_call(
        body, name=name,
        out_shape=_SiblingExchange.out_shape(ts),
        in_specs=[_ANY] * n, out_specs=[_ANY] * n,
        scratch_shapes=_SiblingExchange.scratch(ts),
    )(*grads)


def _chip_sum(t, g, rs, place):
    shape, ax, (pr, pc), dt = _RED[t]
    W = shape[1]
    tr = min(pr, _RED_TR)
    nb = pr // tr

    def body(place_ref, g_ref, rs_ref, o_ref):
        o_ref[...] = (g_ref[...] + rs_ref[...]).astype(o_ref.dtype)

    return pl.pallas_call(
        body, name="grad_chip_sum_%d" % t,
        out_shape=jax.ShapeDtypeStruct(rs.shape, dt),
        grid_spec=pltpu.PrefetchScalarGridSpec(
            num_scalar_prefetch=1, grid=(1 if ax == 1 else N_CHIPS, nb),
            in_specs=[pl.BlockSpec((tr, W), lambda kk, i, pr_: ((2 * kk + pr_[0]) * nb + i, 0)),
                      pl.BlockSpec((tr, W), lambda kk, i, pr_: (kk * nb + i, 0))],
            out_specs=pl.BlockSpec((tr, W), lambda kk, i, pr_: (kk * nb + i, 0))),
        compiler_params=_params(("parallel", "parallel")),
    )(place, g, rs)


class _ChipExchange:
    def __init__(self, ts, srcs, dsts, sems):
        self.send_sems, self.recv_sems = sems
        x, y, c, chips = _place()
        self.copies = lambda: [
            pltpu.make_async_remote_copy(src_ref=_piece_in_half(srcs[i], t, 2 * px + py), dst_ref=dsts[i].at[j],
                                         send_sem=self.send_sems.at[j, i], recv_sem=self.recv_sems.at[j, i],
                                         device_id=(px, py, c), device_id_type=_MESH)
            for i, t in enumerate(ts) for j, (px, py) in enumerate(chips)]

    @staticmethod
    def scratch(ts):
        return [pltpu.SemaphoreType.DMA((3, len(ts))), pltpu.SemaphoreType.DMA((3, len(ts)))]

    @staticmethod
    def out_shape(ts):
        return [jax.ShapeDtypeStruct((3,) + _RED[t][2], _RED[t][3]) for t in ts]

    def start(self):
        for cp in self.copies():
            cp.start()

    def finish(self):
        for cp in self.copies():
            cp.wait()


def _total_sum(t, g, rs, rc, place):
    shape, ax, (pr, pc), _ = _RED[t]
    tr = min(pr, _RED_TR)
    nb = pr // tr
    small = t == len(_RED) - 1

    def body(place_ref, g_ref, rs_ref, rc_ref, o_ref):
        o_ref[...] = (g_ref[...] + rs_ref[...]) + rc_ref[0].astype(F32) + rc_ref[1].astype(F32) + rc_ref[2].astype(F32)

    if ax == 1:
        g_map = lambda i, pr_: (pr_[0] * nb + i, pr_[1])
        rs_map = lambda i, pr_: (i, pr_[1])
    else:
        g_map = lambda i, pr_: ((2 * pr_[1] + pr_[0]) * nb + i, 0)
        rs_map = lambda i, pr_: (pr_[1] * nb + i, 0)
    o_map = (lambda i, pr_: ((2 * pr_[1] + pr_[0]) * nb + i, 0)) if small else (lambda i, pr_: (pr_[0] * nb + i, 0))
    return pl.pallas_call(
        body, name="grad_total_sum_%d" % t,
        out_shape=jax.ShapeDtypeStruct(((8 if small else 2) * pr, pc), F32),
        grid_spec=pltpu.PrefetchScalarGridSpec(
            num_scalar_prefetch=1, grid=(nb,),
            in_specs=[pl.BlockSpec((tr, pc), g_map), pl.BlockSpec((tr, pc), rs_map),
                      pl.BlockSpec((3, tr, pc), lambda i, pr_: (0, i, 0))],
            out_specs=pl.BlockSpec((tr, pc), o_map)),
        compiler_params=_params(("parallel",)),
    )(place, g, rs, rc)


def _small_direct_reduce(flat, place):
    def exchange(src_ref, dst_ref, send_sems, recv_sems):
        x, y, c, chips = _place()
        others = [(x, y, 1 - c)] + [(px, py, cc) for (px, py) in chips for cc in (c, 1 - c)]
        cps = [pltpu.make_async_remote_copy(
            src_ref=src_ref.at[pl.ds(pl.multiple_of((4 * px + 2 * py + pc) * SMALL_ROWS, SMALL_ROWS), SMALL_ROWS), :],
            dst_ref=dst_ref.at[i], send_sem=send_sems.at[i], recv_sem=recv_sems.at[i],
            device_id=(px, py, pc), device_id_type=_MESH) for i, (px, py, pc) in enumerate(others)]
        for cp in cps:
            cp.start()
        for cp in cps:
            cp.wait()

    received = pl.pallas_call(
        exchange, name="grad_small_exchange",
        out_shape=jax.ShapeDtypeStruct((7, SMALL_ROWS, SMALL_W), F32),
        in_specs=[_ANY], out_specs=_ANY,
        scratch_shapes=[pltpu.SemaphoreType.DMA((7,)), pltpu.SemaphoreType.DMA((7,))],
    )(flat)

    def add(place_ref, g_ref, r_ref, o_ref):
        t = g_ref[...]
        for i in range(7):
            t = t + r_ref[i]
        o_ref[...] = t

    mine = lambda i, pr_: (2 * pr_[1] + pr_[0], 0)
    return pl.pallas_call(
        add, name="grad_small_sum",
        out_shape=jax.ShapeDtypeStruct(flat.shape, F32),
        grid_spec=pltpu.PrefetchScalarGridSpec(
            num_scalar_prefetch=1, grid=(1,),
            in_specs=[pl.BlockSpec((SMALL_ROWS, SMALL_W), mine), pl.BlockSpec((7, SMALL_ROWS, SMALL_W), lambda i, pr_: (0, 0, 0))],
            out_specs=pl.BlockSpec((SMALL_ROWS, SMALL_W), mine)),
        compiler_params=_params(("arbitrary",)),
    )(place, flat, received)


def _grad_final_exchange(totals):
    n = len(_RED)
    nb = n - 1

    def body(*refs):
        srcs, dsts, (send_sems, recv_sems) = refs[:n], refs[n:2 * n], refs[2 * n:]
        x, y, c, chips = _place()
        me = 4 * x + 2 * y + c
        others = [(x, y, 1 - c)] + [(px, py, cc) for (px, py) in chips for cc in (c, 1 - c)]

        def half(ref, t, cc):
            pr = _RED[t][2][0]
            return ref.at[pl.ds(pl.multiple_of(cc * pr, pr), pr), :]

        def eighth(ref, dev):
            return ref.at[pl.ds(pl.multiple_of(dev * SMALL_ROWS, SMALL_ROWS), SMALL_ROWS), :]

        def big_copy(t, cc):
            return pltpu.make_async_remote_copy(src_ref=half(srcs[t], t, cc), dst_ref=half(dsts[t], t, cc), send_sem=send_sems.at[t],
                                                recv_sem=recv_sems.at[t], device_id=others[0], device_id_type=_MESH)

        def small_copy(i, dev):
            return pltpu.make_async_remote_copy(src_ref=eighth(srcs[nb], dev), dst_ref=eighth(dsts[nb], dev),
                                                send_sem=send_sems.at[nb + i], recv_sem=recv_sems.at[nb + i],
                                                device_id=others[i], device_id_type=_MESH)

        sends = [big_copy(t, c) for t in range(nb)] + [small_copy(i, me) for i in range(7)]
        for cp in sends:
            cp.start()
        for t in range(nb):
            big_copy(t, 1 - c).wait_recv()
        for i, (px, py, pc) in enumerate(others):
            small_copy(i, 4 * px + 2 * py + pc).wait_recv()
        for cp in sends:
            cp.wait_send()

    return pl.pallas_call(
        body, name="grad_final_exchange",
        out_shape=[jax.ShapeDtypeStruct(a.shape, F32) for a in totals],
        in_specs=[_ANY] * n, out_specs=[_ANY] * n,
        input_output_aliases={t: t for t in range(n)},
        scratch_shapes=[pltpu.SemaphoreType.DMA((nb + 7,)), pltpu.SemaphoreType.DMA((nb + 7,))],
    )(*totals)


def _grad_place():
    return jnp.stack([lax.axis_index("c"), 2 * lax.axis_index("x") + lax.axis_index("y")]).astype(jnp.int32)


def _reduce_begin(ts, grads, place, tag, from_sibling=None):
    if from_sibling is None:
        from_sibling = _grad_sibling_exchange(ts, grads, "grad_sibling_exchange_" + tag)
    return from_sibling, [_chip_sum(t, g, r, place) for t, g, r in zip(ts, grads, from_sibling)]


def _reduce_end(ts, grads, from_sibling, from_chips, place):
    return [_total_sum(t, g, r, q, place) for t, g, r, q in zip(ts, grads, from_sibling, from_chips)]


_EARLY = (1, 2, 3, 4)
_W_IN = (0,)


def _adamw_math(w, g, m, v):
    m = ADAM_B1 * m + (1.0 - ADAM_B1) * g
    v = ADAM_B2 * v + (1.0 - ADAM_B2) * (g * g)
    m_hat = m / (1.0 - ADAM_B1 ** ADAM_STEP)
    v_hat = v / (1.0 - ADAM_B2 ** ADAM_STEP)
    return -ADAM_LR * (m_hat / (jnp.sqrt(v_hat) + ADAM_EPS) + ADAM_WD * w), m, v


def _adamw(w, g, m, v, name):
    W = w.shape[1]
    return _rowwise(_adamw_math, [(a, 0, W) for a in (w, g, m, v)], [], [(W, F32)] * 3, tr=128, name=name)


def _adamw_whole(ws, gs, ms, vs, name):
    n = len(ws)

    def body(*refs):
        ins, outs = refs[:4 * n], refs[4 * n:]
        for i in range(n):
            res = _adamw_math(*[ins[j * n + i][...] for j in range(4)])
            for j in range(3):
                outs[j * n + i][...] = res[j]

    res = pl.pallas_call(
        body, name=name,
        out_shape=[jax.ShapeDtypeStruct(a.shape, F32) for a in ws] * 3,
        compiler_params=pltpu.CompilerParams(vmem_limit_bytes=VMEM_LIMIT_V7X),
    )(*ws, *gs, *ms, *vs)
    return res[:n], res[n:2 * n], res[2 * n:]


def _chunks(arr, off, width, w=512):
    return [(arr, off + i * w, w) for i in range(width // w)]


def _cat(vs):
    return jnp.concatenate(vs, axis=1)


def _forward_backward(x, p_b, tgt, shards, small):
    L = x.shape[0]
    row = lambda v: v.reshape(1, -1)
    g_mix, g_ple, g_fin = row(small["norm_mix"]), row(small["norm_ple"]), row(small["norm_final"])
    gq, gk, b_glu = row(small["q_norm"]), row(small["k_norm"]), row(small["b_glu"])
    cos, sin = _rope_tables(L)

    hn_b, = _rowwise(lambda x, g: x * _rms(x) * g, [(x, 0, D_MODEL)], [g_mix], [(D_MODEL, BF16)], name="norm_mix")
    z, w_in = _matmul_in_gather(hn_b, shards[0])
    qr, kr, vb = _attn_prep(z, gq, gk, cos, sin)
    o, lse, w_glu, w_out, w_pg, w_pp = _attn_fwd(qr, kr, vb, [1, 2, 3, 4], shards[1:])

    ssm_names = ("ssm_a_re", "ssm_a_im", "ssm_log_dt", "ssm_b_re", "ssm_b_im")
    (lre, lim, bre, bim), disc_vjp = jax.vjp(_ssm_disc, *[small[n][0] for n in ssm_names])
    ssm = (_to_pairs(jnp.swapaxes(jnp.stack([bre, bim], axis=2), -1, -2)),
           _to_pairs(jnp.stack([small["ssm_c_re"][0], -small["ssm_c_im"][0]], axis=2)),
           _to_lam(lre), _to_lam(lim), row(small["ssm_d"]))
    u_p = _seg_perm(z[:, Z_U:Z_U + D_SSM])
    y_p, ge_p, ssm_entry = _ssm_fwd(u_p, *ssm)
    ge_b = _seg_unperm(ge_p)
    glu = _matmul(ge_b, w_glu, name="mm_glu")

    def merge(ga0, ga1, a, b, gs0, gs1, o, bias):
        sa, _ = _silu_and_grad(_cat([ga0, ga1]))
        ss, _ = _silu_and_grad(_cat([gs0, gs1]))
        y2 = (a + bias[:, :D_SSM]) * _sig(b + bias[:, D_SSM:])
        return _cat([o * sa, y2 * ss])
    merge_rows = _chunks(z, Z_GA, D_ATTN) + [(glu, 0, D_SSM), (glu, D_SSM, D_SSM)] + _chunks(z, Z_GS, D_SSM) + [(o, 0, D_ATTN)]
    cat_b, = _rowwise(merge, merge_rows, [b_glu], [(D_MODEL, BF16)], name="merge")
    h1 = _matmul(cat_b, w_out, name="mm_out", add=x)
    hp_b, = _rowwise(lambda h, g: h * _rms(h) * g, [(h1, 0, D_MODEL)], [g_ple], [(D_MODEL, BF16)], name="resid_norm")
    gl = _matmul(hp_b, w_pg, name="mm_ple_gate")
    pp = _matmul(p_b, w_pp, name="mm_ple_proj")

    def head(h1, gl, pp, tgt, g):
        gate = _sig(gl)
        h2 = h1 + gate * pp
        r = _rms(h2)
        n = h2 * r
        err = n * g - tgt
        dy = err * (1.0 / D_MODEL)
        dn = dy * g
        dh2 = r * (dn - n * jnp.mean(dn * n, axis=-1, keepdims=True))
        dgate = dh2 * pp
        return dh2, dh2 * gate, dgate * gate * (1.0 - gate), _colsum(dy * n), _colsum(0.5 * err * err * (1.0 / D_MODEL))
    dh2, dpp_b, dgl_b, dg_fin, loss_cols = _rowwise(
        head, [(a, 0, D_MODEL) for a in (h1, gl, pp, tgt)], [g_fin],
        [(D_MODEL, F32), (D_MODEL, BF16), (D_MODEL, BF16)], [(1, D_MODEL), (1, D_MODEL)], name="loss_head")

    dw_pp = _matmul(p_b, dpp_b, ta=True, name="mm_d_w_ple_proj")
    dw_pg = _matmul(hp_b, dgl_b, ta=True, name="mm_d_w_ple_gate")
    dhp = _matmul(dgl_b, w_pg, tb=True, name="mm_d_hp")

    def resid_bwd(dhp, h1, dh2, g):
        dx, dg = _rms_bwd(h1, g, dhp)
        dh1 = dh2 + dx
        return dh1, dh1, _colsum(dg)
    dh1, dh1_b, dg_ple = _rowwise(resid_bwd, [(a, 0, D_MODEL) for a in (dhp, h1, dh2)], [g_ple],
                                  [(D_MODEL, F32), (D_MODEL, BF16)], [(1, D_MODEL)], name="resid_norm_bwd")
    dw_out = _matmul(cat_b, dh1_b, ta=True, name="mm_d_w_out")
    dcat = _matmul(dh1_b, w_out, tb=True, out_dtype=BF16, name="mm_d_cat")

    def merge_bwd(dya, dys, ga0, ga1, a, b, gs0, gs1, o, bias):
        ga, gs = _cat([ga0, ga1]), _cat([gs0, gs1])
        sa, dsa = _silu_and_grad(ga)
        ss, dss = _silu_and_grad(gs)
        a, sb = a + bias[:, :D_SSM], _sig(b + bias[:, D_SSM:])
        dy2 = dys * ss
        dglu = _cat([dy2 * sb, dy2 * a * sb * (1.0 - sb)])
        do = dya * sa
        lane = lax.broadcasted_iota(jnp.int32, (do.shape[0], HEAD_DIM), 1)
        delta = sum(jnp.where(lane == i, jnp.sum(h, axis=1, keepdims=True), 0.0)
                    for i, h in enumerate(_heads(do * o)))
        return do, dya * o * dsa, dys * (a * sb) * dss, dglu, delta, _colsum(dglu)
    do_b, dga_b, dgs_b, dglu_b, delta, db_glu = _rowwise(
        merge_bwd, [(dcat, 0, D_ATTN), (dcat, D_ATTN, D_SSM)] + merge_rows, [b_glu],
        [(D_ATTN, BF16), (D_ATTN, BF16), (D_SSM, BF16), (2 * D_SSM, BF16), (HEAD_DIM, F32)], [(1, 2 * D_SSM)], name="merge_bwd")
    dw_glu = _matmul(ge_b, dglu_b, ta=True, name="mm_d_w_glu")
    dge = _matmul(dglu_b, w_glu, tb=True, name="mm_d_ge")
    place = _grad_place()
    early_grads = [dw_glu, dw_out, dw_pg, dw_pp]
    dy_p, = _rowwise(lambda dge, y: dge * _gelu_grad(y), [(_seg_perm(dge), 0, D_SSM), (y_p, 0, D_SSM)], [], [(D_SSM, F32)],
                     name="gelu_bwd")
    du_p, dwb, dwc, dare, daim, d_ssm_d, *early_sib = _ssm_bwd(u_p, dy_p, ssm_entry, *ssm, _EARLY, early_grads)
    dcc, dbb = _from_pairs(dwc), jnp.swapaxes(_from_pairs(dwb), -1, -2)
    dc_re, dc_im = dcc[:, :, 0], -dcc[:, :, 1]
    da_re, da_im, dlog_dt, db_re, db_im = disc_vjp((_from_lam(dare), _from_lam(daim), dbb[:, :, 0], dbb[:, :, 1]))

    early_sib, early_sums = _reduce_begin(_EARLY, early_grads, place, "early", from_sibling=early_sib)
    dqr, dkr, dv, *early_chips = _attn_bwd(qr, kr, kr.T, vb, do_b, lse.reshape(N_HEADS, 1, L), delta[:, :N_HEADS].T.reshape(N_HEADS, 1, L),
                                           _EARLY, early_sums)
    early_totals = _reduce_end(_EARLY, early_grads, early_sib, early_chips, place)
    dq_b, dk_b, dgq, dgk = _attn_prep_bwd(dqr, dkr, z, gq, gk, cos, sin)
    dz_b = _cat([dq_b, dk_b, dv.astype(BF16), dga_b, _seg_unperm(du_p).astype(BF16), dgs_b])
    dw_in = _matmul(hn_b, dz_b, ta=True, name="mm_d_w_in")
    w_in_sib, w_in_sums = _reduce_begin(_W_IN, [dw_in], place, "w_in")
    dhn, *w_in_chips = _matmul(dz_b, w_in, tb=True, name="mm_d_hn", exchange=(_W_IN, w_in_sums))
    w_in_total, = _reduce_end(_W_IN, [dw_in], w_in_sib, w_in_chips, place)

    def norm_bwd(dhn, x, dh1, g):
        dx, dg = _rms_bwd(x, g, dhn)
        return dh1 + dx, _colsum(dg)
    grad_x, dg_mix = _rowwise(norm_bwd, [(a, 0, D_MODEL) for a in (dhn, x, dh1)], [g_mix], [(D_MODEL, F32)], [(1, D_MODEL)],
                              name="norm_mix_bwd")

    small_grads = {"norm_mix": dg_mix, "q_norm": dgq, "k_norm": dgk, "ssm_a_re": da_re, "ssm_a_im": da_im, "ssm_log_dt": dlog_dt,
                   "ssm_b_re": db_re, "ssm_b_im": db_im, "ssm_c_re": dc_re, "ssm_c_im": dc_im, "ssm_d": d_ssm_d,
                   "b_glu": db_glu, "norm_ple": dg_ple, "norm_final": dg_fin}
    return jnp.sum(loss_cols), grad_x, [w_in_total] + early_totals, small_grads, place


_SMALL = ("norm_mix", "q_norm", "k_norm", "ssm_a_re", "ssm_a_im", "ssm_log_dt", "ssm_b_re", "ssm_b_im", "ssm_c_re", "ssm_c_im",
          "ssm_d", "b_glu", "norm_ple", "norm_final")
_WEIGHTS = ("norm_mix", "w_in", "q_norm", "k_norm", "ssm_a_re", "ssm_a_im", "ssm_log_dt", "ssm_b_re", "ssm_b_im", "ssm_c_re",
            "ssm_c_im", "ssm_d", "w_glu", "b_glu", "w_out", "norm_ple", "w_ple_gate", "w_ple_proj", "norm_final")


_SMALL_ADAMW_GROUPS = (("ssm_b_re",), ("ssm_b_im",), ("ssm_c_re", "ssm_c_im"),
                       ("norm_mix", "q_norm", "k_norm", "ssm_a_re", "ssm_a_im", "ssm_log_dt", "ssm_d", "b_glu", "norm_ple",
                        "norm_final"))


def _flat_small(d):
    flat = jnp.concatenate([d[n].reshape(-1).astype(F32) for n in _SMALL])
    return jnp.pad(flat, (0, N_SMALL - flat.shape[0]))


def _split_small(flat, like):
    out, off = {}, 0
    for n in _SMALL:
        sz = math.prod(like[n].shape)
        out[n] = flat[off:off + sz].reshape(like[n].shape)
        off += sz
    return out


def kernel(x, p, norm_mix, w_in, q_norm, k_norm, ssm_a_re, ssm_a_im, ssm_log_dt, ssm_b_re, ssm_b_im, ssm_c_re, ssm_c_im, ssm_d, w_glu, b_glu, w_out, norm_ple, w_ple_gate, w_ple_proj, norm_final, loss_target, m_norm_mix, m_w_in, m_q_norm, m_k_norm, m_ssm_a_re, m_ssm_a_im, m_ssm_log_dt, m_ssm_b_re, m_ssm_b_im, m_ssm_c_re, m_ssm_c_im, m_ssm_d, m_w_glu, m_b_glu, m_w_out, m_norm_ple, m_w_ple_gate, m_w_ple_proj, m_norm_final, v_norm_mix, v_w_in, v_q_norm, v_k_norm, v_ssm_a_re, v_ssm_a_im, v_ssm_log_dt, v_ssm_b_re, v_ssm_b_im, v_ssm_c_re, v_ssm_c_im, v_ssm_d, v_w_glu, v_b_glu, v_w_out, v_norm_ple, v_w_ple_gate, v_w_ple_proj, v_norm_final):
    w = dict(norm_mix=norm_mix, w_in=w_in, q_norm=q_norm, k_norm=k_norm, ssm_a_re=ssm_a_re, ssm_a_im=ssm_a_im,
             ssm_log_dt=ssm_log_dt, ssm_b_re=ssm_b_re, ssm_b_im=ssm_b_im, ssm_c_re=ssm_c_re, ssm_c_im=ssm_c_im, ssm_d=ssm_d,
             w_glu=w_glu, b_glu=b_glu, w_out=w_out, norm_ple=norm_ple, w_ple_gate=w_ple_gate, w_ple_proj=w_ple_proj,
             norm_final=norm_final)
    m = dict(norm_mix=m_norm_mix, w_in=m_w_in, q_norm=m_q_norm, k_norm=m_k_norm, ssm_a_re=m_ssm_a_re, ssm_a_im=m_ssm_a_im,
             ssm_log_dt=m_ssm_log_dt, ssm_b_re=m_ssm_b_re, ssm_b_im=m_ssm_b_im, ssm_c_re=m_ssm_c_re, ssm_c_im=m_ssm_c_im,
             ssm_d=m_ssm_d, w_glu=m_w_glu, b_glu=m_b_glu, w_out=m_w_out, norm_ple=m_norm_ple, w_ple_gate=m_w_ple_gate,
             w_ple_proj=m_w_ple_proj, norm_final=m_norm_final)
    v = dict(norm_mix=v_norm_mix, w_in=v_w_in, q_norm=v_q_norm, k_norm=v_k_norm, ssm_a_re=v_ssm_a_re, ssm_a_im=v_ssm_a_im,
             ssm_log_dt=v_ssm_log_dt, ssm_b_re=v_ssm_b_re, ssm_b_im=v_ssm_b_im, ssm_c_re=v_ssm_c_re, ssm_c_im=v_ssm_c_im,
             ssm_d=v_ssm_d, w_glu=v_w_glu, b_glu=v_b_glu, w_out=v_w_out, norm_ple=v_norm_ple, w_ple_gate=v_w_ple_gate,
             w_ple_proj=v_w_ple_proj, norm_final=v_norm_final)
    big_names = [n for n, _, _, _ in _BIG]

    small = {n: w[n] for n in _SMALL}
    loss_part, grad_x, big_totals, small_grads, place = _forward_backward(
        x[0], p[0, 0].astype(BF16), loss_target[0], [w[n][0].astype(BF16) for n in big_names], small)
    loss = lax.psum(loss_part, ("x", "y", "c"))

    small_total = _small_direct_reduce(_flat_small(small_grads).reshape(8 * SMALL_ROWS, SMALL_W), place)
    *big_red, small_red = _grad_final_exchange(big_totals + [small_total])
    grads = _split_small(small_red.reshape(-1), w)
    delta, new_m, new_v = {}, {}, {}
    for n, g in zip(big_names, big_red):
        grads[n] = g[None]
        d_, m_, v_ = _adamw(w[n][0], g, m[n][0], v[n][0], "adamw_" + n)
        delta[n], new_m[n], new_v[n] = d_[None], m_[None], v_[None]
    at_least_2d = lambda a: a.reshape(1, -1) if a.ndim == 1 else a
    for i, names in enumerate(_SMALL_ADAMW_GROUPS):
        d_, m_, v_ = _adamw_whole(*[[at_least_2d(src[n]) for n in names] for src in (w, grads, m, v)], "adamw_small_%d" % i)
        for j, n in enumerate(names):
            delta[n], new_m[n], new_v[n] = (a[j].reshape(w[n].shape) for a in (d_, m_, v_))
    return (loss, grad_x[None], *[grads[n] for n in _WEIGHTS], *[delta[n] for n in _WEIGHTS],
            *[new_m[n] for n in _WEIGHTS], *[new_v[n] for n in _WEIGHTS])
```
